```python
import jax, jax.numpy as jnp
from jax import lax
import numpy as np

D_MODEL = 1024
BATCH = 8
SEQ = 8192
DEPTH = 1

CHUNK = 64
EPS = 1e-5
N_BRANCH = 2
GMLP_BLOCK = 128
GMLP_WIDTH = 1024
GMLP_GROUPS = 8
GMLP_GDIM = GMLP_WIDTH // GMLP_GROUPS
SSM_INNER = 2 * D_MODEL
SSM_HEAD_DIM = 64
SSM_HEADS = SSM_INNER // SSM_HEAD_DIM
SSM_GROUPS = 4
SSM_HPG = SSM_HEADS // SSM_GROUPS
SSM_STATE = 128
SSM_CONV = 4
SSM_CHUNK = CHUNK
SSM_XBC = SSM_INNER + 2 * SSM_GROUPS * SSM_STATE
D_FF = 2816
FFN_CONV = 3
IN_COLS = N_BRANCH * D_MODEL + 2 * GMLP_WIDTH + SSM_INNER + SSM_XBC + SSM_HEADS

kernel_name = "hybrid_gmlp_ssd_gated_merge_block"


def rmsnorm(x, w):
    xf = x.astype(jnp.float32)
    y = xf * lax.rsqrt(jnp.mean(xf * xf, axis=-1, keepdims=True) + EPS)
    return (y * w.astype(jnp.float32)).astype(x.dtype)


def causal_dwconv(x, w, b):
    K, C = w.shape
    y = lax.conv_general_dilated(
        x, w[:, None, :].astype(x.dtype), window_strides=(1,), padding=[(K - 1, 0)],
        dimension_numbers=('NWC', 'WIO', 'NWC'), feature_group_count=C)
    return y + b.astype(x.dtype)


def gmlp_mixer(za, ln_w, ln_b, w_s, b_s):
    Bsz, S, _ = za.shape
    z = jax.nn.gelu(za)
    u, v = jnp.split(z, 2, axis=-1)
    nb = S // GMLP_BLOCK
    v = v.reshape(Bsz, nb, GMLP_BLOCK, GMLP_GROUPS, GMLP_GDIM)
    vf = v.astype(jnp.float32)
    mu = jnp.mean(vf, axis=-1, keepdims=True)
    var = jnp.mean(jnp.square(vf - mu), axis=-1, keepdims=True)
    v = ((vf - mu) * lax.rsqrt(var + EPS) * ln_w.astype(jnp.float32)
         + ln_b.astype(jnp.float32)).astype(z.dtype)
    chunk_id = jnp.arange(GMLP_BLOCK) // CHUNK
    mask = chunk_id[None, :] <= chunk_id[:, None]
    ws = jnp.where(mask[None], w_s, 0).astype(v.dtype)
    sv = jnp.einsum('gts,bnsgd->bntgd', ws, v) + b_s.T.astype(v.dtype)[None, None, :, :, None]
    return u * sv.reshape(Bsz, S, GMLP_WIDTH)


def ssd_scan(x, dt, A, Bm, Cm):
    Bsz, S = x.shape[:2]
    L = SSM_CHUNK
    nc = S // L
    f32 = jnp.float32
    x = x.astype(f32).reshape(Bsz, nc, L, SSM_GROUPS, SSM_HPG, SSM_HEAD_DIM)
    dt = dt.reshape(Bsz, nc, L, SSM_GROUPS, SSM_HPG)
    Bm = Bm.astype(f32).reshape(Bsz, nc, L, SSM_GROUPS, SSM_STATE)
    Cm = Cm.astype(f32).reshape(Bsz, nc, L, SSM_GROUPS, SSM_STATE)
    a_cum = jnp.cumsum(dt * A.reshape(SSM_GROUPS, SSM_HPG), axis=2)
    xdt = x * dt[..., None]
    seg = a_cum[:, :, :, None] - a_cum[:, :, None, :]
    causal = jnp.tril(jnp.ones((L, L), dtype=bool))
    decay = jnp.exp(jnp.where(causal[:, :, None, None], seg, -jnp.inf))
    cb = jnp.einsum('bclgn,bcsgn->bclsg', Cm, Bm)
    y_diag = jnp.einsum('bclsg,bclsgh,bcsghp->bclghp', cb, decay, xdt)
    decay_end = jnp.exp(a_cum[:, :, -1:] - a_cum)
    states = jnp.einsum('bcsgn,bcsgh,bcsghp->bcghpn', Bm, decay_end, xdt)
    chunk_decay = jnp.exp(a_cum[:, :, -1])

    def step(h, inp):
        st, dec = inp
        return h * dec[..., None, None] + st, h

    h0 = jnp.zeros((Bsz, SSM_GROUPS, SSM_HPG, SSM_HEAD_DIM, SSM_STATE), f32)
    _, prev = lax.scan(step, h0, (jnp.moveaxis(states, 1, 0), jnp.moveaxis(chunk_decay, 1, 0)))
    prev = jnp.moveaxis(prev, 0, 1)
    y_off = jnp.einsum('bclgn,bcghpn,bclgh->bclghp', Cm, prev, jnp.exp(a_cum))
    return (y_diag + y_off).reshape(Bsz, S, SSM_HEADS, SSM_HEAD_DIM)


def mamba2_mixer(z, xbc, dt_raw, conv_w, conv_b, dt_bias, a_log, d_skip, norm_w):
    Bsz, S, _ = z.shape
    xbc = jax.nn.silu(causal_dwconv(xbc, conv_w, conv_b))
    xs, Bm, Cm = jnp.split(xbc, [SSM_INNER, SSM_INNER + SSM_GROUPS * SSM_STATE], axis=-1)
    xs = xs.reshape(Bsz, S, SSM_HEADS, SSM_HEAD_DIM)
    Bm = Bm.reshape(Bsz, S, SSM_GROUPS, SSM_STATE)
    Cm = Cm.reshape(Bsz, S, SSM_GROUPS, SSM_STATE)
    dt = jax.nn.softplus(dt_raw.astype(jnp.float32) + dt_bias.astype(jnp.float32))
    A = -jnp.exp(a_log.astype(jnp.float32))
    y = ssd_scan(xs, dt, A, Bm, Cm) + d_skip.astype(jnp.float32)[:, None] * xs.astype(jnp.float32)
    y = y.reshape(Bsz, S, SSM_INNER) * jax.nn.silu(z.astype(jnp.float32))
    yg = y.reshape(Bsz, S, SSM_GROUPS, SSM_INNER // SSM_GROUPS)
    yg = yg * lax.rsqrt(jnp.mean(yg * yg, axis=-1, keepdims=True) + EPS)
    y = yg.reshape(Bsz, S, SSM_INNER) * norm_w.astype(jnp.float32)
    return y.astype(z.dtype)


def _fwd_setup_inputs(seed: int = 0) -> dict:
    key = jax.random.key(seed)
    ks = jax.random.split(key, 24)

    def nrm(k, shape, scale):
        return jax.random.normal(k, shape, jnp.float32) * scale

    def gain(k, shape):
        return 1.0 + 0.02 * jax.random.normal(k, shape, jnp.float32)

    dt0 = jnp.exp(jax.random.uniform(ks[12], (DEPTH, SSM_HEADS), jnp.float32,
                                     np.log(1e-3), np.log(1e-1)))
    dt_bias = dt0 + jnp.log(-jnp.expm1(-dt0))
    a_log = jnp.log(jax.random.uniform(ks[13], (DEPTH, SSM_HEADS), jnp.float32, 1.0, 16.0))
    return {
        "x": nrm(ks[0], (BATCH, SEQ, D_MODEL), 1.0),
        "mix_norm_w": gain(ks[1], (DEPTH, D_MODEL)),
        "w_in": nrm(ks[2], (DEPTH, D_MODEL, IN_COLS), D_MODEL ** -0.5),
        "gate_bias": nrm(ks[3], (DEPTH, N_BRANCH, D_MODEL), 0.01),
        "gmlp_ln_w": gain(ks[4], (DEPTH, GMLP_GROUPS, GMLP_GDIM)),
        "gmlp_ln_b": nrm(ks[5], (DEPTH, GMLP_GROUPS, GMLP_GDIM), 0.01),
        "gmlp_ws": nrm(ks[6], (DEPTH, GMLP_GROUPS, GMLP_BLOCK, GMLP_BLOCK), 0.5 * GMLP_BLOCK ** -0.5),
        "gmlp_bs": gain(ks[7], (DEPTH, GMLP_GROUPS, GMLP_BLOCK)),
        "ssm_conv_w": nrm(ks[8], (DEPTH, SSM_CONV, SSM_XBC), SSM_CONV ** -0.5),
        "ssm_conv_b": nrm(ks[9], (DEPTH, SSM_XBC), 0.01),
        "ssm_dt_bias": dt_bias,
        "ssm_a_log": a_log,
        "ssm_d": gain(ks[10], (DEPTH, SSM_HEADS)),
        "ssm_norm_w": gain(ks[11], (DEPTH, SSM_INNER)),
        "w_proj_a": nrm(ks[14], (DEPTH, GMLP_WIDTH, D_MODEL), GMLP_WIDTH ** -0.5),
        "w_proj_b": nrm(ks[15], (DEPTH, SSM_INNER, D_MODEL), SSM_INNER ** -0.5),
        "w_out": nrm(ks[16], (DEPTH, D_MODEL, D_MODEL), D_MODEL ** -0.5),
        "ffn_norm_w": gain(ks[17], (DEPTH, D_MODEL)),
        "ffn_w_up": nrm(ks[18], (DEPTH, D_MODEL, 2 * D_FF), D_MODEL ** -0.5),
        "ffn_conv_w": nrm(ks[19], (DEPTH, FFN_CONV, 2 * D_FF), FFN_CONV ** -0.5),
        "ffn_conv_b": nrm(ks[20], (DEPTH, 2 * D_FF), 0.01),
        "ffn_w_down": nrm(ks[21], (DEPTH, D_FF, D_MODEL), D_FF ** -0.5),
        "final_norm_w": gain(ks[22], (D_MODEL,)),
    }


def _fwd_reference(x, mix_norm_w, w_in, gate_bias, gmlp_ln_w, gmlp_ln_b, gmlp_ws, gmlp_bs,
              ssm_conv_w, ssm_conv_b, ssm_dt_bias, ssm_a_log, ssm_d, ssm_norm_w,
              w_proj_a, w_proj_b, w_out, ffn_norm_w, ffn_w_up, ffn_conv_w, ffn_conv_b,
              ffn_w_down, final_norm_w):
    splits = [D_MODEL, 2 * D_MODEL, 2 * D_MODEL + 2 * GMLP_WIDTH,
              2 * D_MODEL + 2 * GMLP_WIDTH + SSM_INNER,
              2 * D_MODEL + 2 * GMLP_WIDTH + SSM_INNER + SSM_XBC]
    h = x
    for l in range(DEPTH):
        xn = rmsnorm(h, mix_norm_w[l])
        proj = xn @ w_in[l]
        g_a, g_b, za, z, xbc, dt_raw = jnp.split(proj, splits, axis=-1)
        y_a = gmlp_mixer(za, gmlp_ln_w[l], gmlp_ln_b[l], gmlp_ws[l], gmlp_bs[l]) @ w_proj_a[l]
        y_b = mamba2_mixer(z, xbc, dt_raw, ssm_conv_w[l], ssm_conv_b[l], ssm_dt_bias[l],
                           ssm_a_log[l], ssm_d[l], ssm_norm_w[l]) @ w_proj_b[l]
        merged = (jax.nn.sigmoid(g_a + gate_bias[l, 0]) * y_a
                  + jax.nn.sigmoid(g_b + gate_bias[l, 1]) * y_b)
        h = h + merged @ w_out[l]
        hn = rmsnorm(h, ffn_norm_w[l])
        up = causal_dwconv(hn @ ffn_w_up[l], ffn_conv_w[l], ffn_conv_b[l])
        gate, val = jnp.split(up, 2, axis=-1)
        h = h + (jax.nn.silu(gate) * val) @ ffn_w_down[l]
    return rmsnorm(h, final_norm_w)


import jax as _jax
import jax.numpy as _jnp

TWIN_FORMAT = 'train_step'
FWD_PARAMS = ['x', 'mix_norm_w', 'w_in', 'gate_bias', 'gmlp_ln_w', 'gmlp_ln_b', 'gmlp_ws', 'gmlp_bs', 'ssm_conv_w', 'ssm_conv_b', 'ssm_dt_bias', 'ssm_a_log', 'ssm_d', 'ssm_norm_w', 'w_proj_a', 'w_proj_b', 'w_out', 'ffn_norm_w', 'ffn_w_up', 'ffn_conv_w', 'ffn_conv_b', 'ffn_w_down', 'final_norm_w']
TWIN_WEIGHTS = ['mix_norm_w', 'w_in', 'gate_bias', 'gmlp_ln_w', 'gmlp_ln_b', 'gmlp_ws', 'gmlp_bs', 'ssm_conv_w', 'ssm_conv_b', 'ssm_dt_bias', 'ssm_a_log', 'ssm_d', 'ssm_norm_w', 'w_proj_a', 'w_proj_b', 'w_out', 'ffn_norm_w', 'ffn_w_up', 'ffn_conv_w', 'ffn_conv_b', 'ffn_w_down', 'final_norm_w']
TWIN_DIFF_INPUT = 'x'
TWIN_INPUTS = ['x', 'mix_norm_w', 'w_in', 'gate_bias', 'gmlp_ln_w', 'gmlp_ln_b', 'gmlp_ws', 'gmlp_bs', 'ssm_conv_w', 'ssm_conv_b', 'ssm_dt_bias', 'ssm_a_log', 'ssm_d', 'ssm_norm_w', 'w_proj_a', 'w_proj_b', 'w_out', 'ffn_norm_w', 'ffn_w_up', 'ffn_conv_w', 'ffn_conv_b', 'ffn_w_down', 'final_norm_w', 'loss_target', 'm_mix_norm_w', 'm_w_in', 'm_gate_bias', 'm_gmlp_ln_w', 'm_gmlp_ln_b', 'm_gmlp_ws', 'm_gmlp_bs', 'm_ssm_conv_w', 'm_ssm_conv_b', 'm_ssm_dt_bias', 'm_ssm_a_log', 'm_ssm_d', 'm_ssm_norm_w', 'm_w_proj_a', 'm_w_proj_b', 'm_w_out', 'm_ffn_norm_w', 'm_ffn_w_up', 'm_ffn_conv_w', 'm_ffn_conv_b', 'm_ffn_w_down', 'm_final_norm_w', 'v_mix_norm_w', 'v_w_in', 'v_gate_bias', 'v_gmlp_ln_w', 'v_gmlp_ln_b', 'v_gmlp_ws', 'v_gmlp_bs', 'v_ssm_conv_w', 'v_ssm_conv_b', 'v_ssm_dt_bias', 'v_ssm_a_log', 'v_ssm_d', 'v_ssm_norm_w', 'v_w_proj_a', 'v_w_proj_b', 'v_w_out', 'v_ffn_norm_w', 'v_ffn_w_up', 'v_ffn_conv_w', 'v_ffn_conv_b', 'v_ffn_w_down', 'v_final_norm_w']
TWIN_OUTPUTS = ['loss', 'grad_x', 'grad_mix_norm_w', 'grad_w_in', 'grad_gate_bias', 'grad_gmlp_ln_w', 'grad_gmlp_ln_b', 'grad_gmlp_ws', 'grad_gmlp_bs', 'grad_ssm_conv_w', 'grad_ssm_conv_b', 'grad_ssm_dt_bias', 'grad_ssm_a_log', 'grad_ssm_d', 'grad_ssm_norm_w', 'grad_w_proj_a', 'grad_w_proj_b', 'grad_w_out', 'grad_ffn_norm_w', 'grad_ffn_w_up', 'grad_ffn_conv_w', 'grad_ffn_conv_b', 'grad_ffn_w_down', 'grad_final_norm_w', 'delta_mix_norm_w', 'delta_w_in', 'delta_gate_bias', 'delta_gmlp_ln_w', 'delta_gmlp_ln_b', 'delta_gmlp_ws', 'delta_gmlp_bs', 'delta_ssm_conv_w', 'delta_ssm_conv_b', 'delta_ssm_dt_bias', 'delta_ssm_a_log', 'delta_ssm_d', 'delta_ssm_norm_w', 'delta_w_proj_a', 'delta_w_proj_b', 'delta_w_out', 'delta_ffn_norm_w', 'delta_ffn_w_up', 'delta_ffn_conv_w', 'delta_ffn_conv_b', 'delta_ffn_w_down', 'delta_final_norm_w', 'new_m_mix_norm_w', 'new_m_w_in', 'new_m_gate_bias', 'new_m_gmlp_ln_w', 'new_m_gmlp_ln_b', 'new_m_gmlp_ws', 'new_m_gmlp_bs', 'new_m_ssm_conv_w', 'new_m_ssm_conv_b', 'new_m_ssm_dt_bias', 'new_m_ssm_a_log', 'new_m_ssm_d', 'new_m_ssm_norm_w', 'new_m_w_proj_a', 'new_m_w_proj_b', 'new_m_w_out', 'new_m_ffn_norm_w', 'new_m_ffn_w_up', 'new_m_ffn_conv_w', 'new_m_ffn_conv_b', 'new_m_ffn_w_down', 'new_m_final_norm_w', 'new_v_mix_norm_w', 'new_v_w_in', 'new_v_gate_bias', 'new_v_gmlp_ln_w', 'new_v_gmlp_ln_b', 'new_v_gmlp_ws', 'new_v_gmlp_bs', 'new_v_ssm_conv_w', 'new_v_ssm_conv_b', 'new_v_ssm_dt_bias', 'new_v_ssm_a_log', 'new_v_ssm_d', 'new_v_ssm_norm_w', 'new_v_w_proj_a', 'new_v_w_proj_b', 'new_v_w_out', 'new_v_ffn_norm_w', 'new_v_ffn_w_up', 'new_v_ffn_conv_w', 'new_v_ffn_conv_b', 'new_v_ffn_w_down', 'new_v_final_norm_w']
TWIN_LEAF_KINDS = {'loss': 'loss', 'grad_x': 'grad_x', 'grad_mix_norm_w': 'grad_w', 'grad_w_in': 'grad_w', 'grad_gate_bias': 'grad_w', 'grad_gmlp_ln_w': 'grad_w', 'grad_gmlp_ln_b': 'grad_w', 'grad_gmlp_ws': 'grad_w', 'grad_gmlp_bs': 'grad_w', 'grad_ssm_conv_w': 'grad_w', 'grad_ssm_conv_b': 'grad_w', 'grad_ssm_dt_bias': 'grad_w', 'grad_ssm_a_log': 'grad_w', 'grad_ssm_d': 'grad_w', 'grad_ssm_norm_w': 'grad_w', 'grad_w_proj_a': 'grad_w', 'grad_w_proj_b': 'grad_w', 'grad_w_out': 'grad_w', 'grad_ffn_norm_w': 'grad_w', 'grad_ffn_w_up': 'grad_w', 'grad_ffn_conv_w': 'grad_w', 'grad_ffn_conv_b': 'grad_w', 'grad_ffn_w_down': 'grad_w', 'grad_final_norm_w': 'grad_w', 'delta_mix_norm_w': 'delta_w', 'delta_w_in': 'delta_w', 'delta_gate_bias': 'delta_w', 'delta_gmlp_ln_w': 'delta_w', 'delta_gmlp_ln_b': 'delta_w', 'delta_gmlp_ws': 'delta_w', 'delta_gmlp_bs': 'delta_w', 'delta_ssm_conv_w': 'delta_w', 'delta_ssm_conv_b': 'delta_w', 'delta_ssm_dt_bias': 'delta_w', 'delta_ssm_a_log': 'delta_w', 'delta_ssm_d': 'delta_w', 'delta_ssm_norm_w': 'delta_w', 'delta_w_proj_a': 'delta_w', 'delta_w_proj_b': 'delta_w', 'delta_w_out': 'delta_w', 'delta_ffn_norm_w': 'delta_w', 'delta_ffn_w_up': 'delta_w', 'delta_ffn_conv_w': 'delta_w', 'delta_ffn_conv_b': 'delta_w', 'delta_ffn_w_down': 'delta_w', 'delta_final_norm_w': 'delta_w', 'new_m_mix_norm_w': 'new_m', 'new_m_w_in': 'new_m', 'new_m_gate_bias': 'new_m', 'new_m_gmlp_ln_w': 'new_m', 'new_m_gmlp_ln_b': 'new_m', 'new_m_gmlp_ws': 'new_m', 'new_m_gmlp_bs': 'new_m', 'new_m_ssm_conv_w': 'new_m', 'new_m_ssm_conv_b': 'new_m', 'new_m_ssm_dt_bias': 'new_m', 'new_m_ssm_a_log': 'new_m', 'new_m_ssm_d': 'new_m', 'new_m_ssm_norm_w': 'new_m', 'new_m_w_proj_a': 'new_m', 'new_m_w_proj_b': 'new_m', 'new_m_w_out': 'new_m', 'new_m_ffn_norm_w': 'new_m', 'new_m_ffn_w_up': 'new_m', 'new_m_ffn_conv_w': 'new_m', 'new_m_ffn_conv_b': 'new_m', 'new_m_ffn_w_down': 'new_m', 'new_m_final_norm_w': 'new_m', 'new_v_mix_norm_w': 'new_v', 'new_v_w_in': 'new_v', 'new_v_gate_bias': 'new_v', 'new_v_gmlp_ln_w': 'new_v', 'new_v_gmlp_ln_b': 'new_v', 'new_v_gmlp_ws': 'new_v', 'new_v_gmlp_bs': 'new_v', 'new_v_ssm_conv_w': 'new_v', 'new_v_ssm_conv_b': 'new_v', 'new_v_ssm_dt_bias': 'new_v', 'new_v_ssm_a_log': 'new_v', 'new_v_ssm_d': 'new_v', 'new_v_ssm_norm_w': 'new_v', 'new_v_w_proj_a': 'new_v', 'new_v_w_proj_b': 'new_v', 'new_v_w_out': 'new_v', 'new_v_ffn_norm_w': 'new_v', 'new_v_ffn_w_up': 'new_v', 'new_v_ffn_conv_w': 'new_v', 'new_v_ffn_conv_b': 'new_v', 'new_v_ffn_w_down': 'new_v', 'new_v_final_norm_w': 'new_v'}


def _forward(args):
    return _fwd_reference(*[args[k] for k in FWD_PARAMS])


def _output_shape():
    out = _jax.eval_shape(lambda: _forward(_fwd_setup_inputs(0)))
    return out.shape, out.dtype

N_MICROBATCH = 1
ADAM_LR = 0.001
ADAM_B1 = 0.9
ADAM_B2 = 0.999
ADAM_EPS = 1e-08
ADAM_WD = 0.01
ADAM_STEP = 10
PER_EXAMPLE_BATCH_AXIS = {'x': 0, 'loss_target': 0}
SHARED_INPUTS = []
_WEIGHT_DTYPES = {'mix_norm_w': _jnp.float32, 'w_in': _jnp.float32, 'gate_bias': _jnp.float32, 'gmlp_ln_w': _jnp.float32, 'gmlp_ln_b': _jnp.float32, 'gmlp_ws': _jnp.float32, 'gmlp_bs': _jnp.float32, 'ssm_conv_w': _jnp.float32, 'ssm_conv_b': _jnp.float32, 'ssm_dt_bias': _jnp.float32, 'ssm_a_log': _jnp.float32, 'ssm_d': _jnp.float32, 'ssm_norm_w': _jnp.float32, 'w_proj_a': _jnp.float32, 'w_proj_b': _jnp.float32, 'w_out': _jnp.float32, 'ffn_norm_w': _jnp.float32, 'ffn_w_up': _jnp.float32, 'ffn_conv_w': _jnp.float32, 'ffn_conv_b': _jnp.float32, 'ffn_w_down': _jnp.float32, 'final_norm_w': _jnp.float32}
MOMENT_SCALE = {'mix_norm_w': 2.274626e-01, 'w_in': 7.342061e-02, 'gate_bias': 4.235342e-02, 'gmlp_ln_w': 3.581734e-02, 'gmlp_ln_b': 3.538968e-02, 'gmlp_ws': 7.051924e-02, 'gmlp_bs': 8.013338e-02, 'ssm_conv_w': 7.745309e-02, 'ssm_conv_b': 1.044095e-01, 'ssm_dt_bias': 1.538820e-01, 'ssm_a_log': 2.940643e-01, 'ssm_d': 5.671416e-01, 'ssm_norm_w': 8.775555e-02, 'w_proj_a': 8.907392e-02, 'w_proj_b': 1.249835e-01, 'w_out': 1.534548e-01, 'ffn_norm_w': 1.656179e-01, 'ffn_w_up': 6.987629e-02, 'ffn_conv_w': 6.913958e-02, 'ffn_conv_b': 6.853275e-02, 'ffn_w_down': 1.148004e-01, 'final_norm_w': 6.395382e+01}


def _to_microbatches(a, axis):
    t = _jnp.moveaxis(a, axis, 0)
    t = t.reshape((N_MICROBATCH, t.shape[0] // N_MICROBATCH) + t.shape[1:])
    return _jnp.moveaxis(t, 1, axis + 1)


def setup_inputs(seed: int = 0) -> dict:
    inp = _fwd_setup_inputs(seed)
    key = _jax.random.fold_in(_jax.random.key(seed), 7919)
    shape, _ = _output_shape()
    out = dict(inp)
    out["loss_target"] = _jax.random.normal(_jax.random.fold_in(key, 0), shape, _jnp.float32)
    for i, name in enumerate(TWIN_WEIGHTS):
        w = inp[name].astype(_jnp.float32)
        if MOMENT_SCALE is None:
            s = _jnp.sqrt(_jnp.mean(_jnp.square(w)) + 1e-30)
        else:
            s = MOMENT_SCALE[name]
        km, kv = _jax.random.split(_jax.random.fold_in(key, i + 1))
        out[name] = w
        out["m_" + name] = s * _jax.random.normal(km, w.shape, _jnp.float32)
        out["v_" + name] = (s * s) * _jax.random.uniform(kv, w.shape, _jnp.float32, 0.5, 1.5)
    if N_MICROBATCH > 1:
        for name, axis in PER_EXAMPLE_BATCH_AXIS.items():
            out[name] = _to_microbatches(out[name], axis)
    return {'x': out['x'], 'mix_norm_w': out['mix_norm_w'], 'w_in': out['w_in'], 'gate_bias': out['gate_bias'], 'gmlp_ln_w': out['gmlp_ln_w'], 'gmlp_ln_b': out['gmlp_ln_b'], 'gmlp_ws': out['gmlp_ws'], 'gmlp_bs': out['gmlp_bs'], 'ssm_conv_w': out['ssm_conv_w'], 'ssm_conv_b': out['ssm_conv_b'], 'ssm_dt_bias': out['ssm_dt_bias'], 'ssm_a_log': out['ssm_a_log'], 'ssm_d': out['ssm_d'], 'ssm_norm_w': out['ssm_norm_w'], 'w_proj_a': out['w_proj_a'], 'w_proj_b': out['w_proj_b'], 'w_out': out['w_out'], 'ffn_norm_w': out['ffn_norm_w'], 'ffn_w_up': out['ffn_w_up'], 'ffn_conv_w': out['ffn_conv_w'], 'ffn_conv_b': out['ffn_conv_b'], 'ffn_w_down': out['ffn_w_down'], 'final_norm_w': out['final_norm_w'], 'loss_target': out['loss_target'], 'm_mix_norm_w': out['m_mix_norm_w'], 'm_w_in': out['m_w_in'], 'm_gate_bias': out['m_gate_bias'], 'm_gmlp_ln_w': out['m_gmlp_ln_w'], 'm_gmlp_ln_b': out['m_gmlp_ln_b'], 'm_gmlp_ws': out['m_gmlp_ws'], 'm_gmlp_bs': out['m_gmlp_bs'], 'm_ssm_conv_w': out['m_ssm_conv_w'], 'm_ssm_conv_b': out['m_ssm_conv_b'], 'm_ssm_dt_bias': out['m_ssm_dt_bias'], 'm_ssm_a_log': out['m_ssm_a_log'], 'm_ssm_d': out['m_ssm_d'], 'm_ssm_norm_w': out['m_ssm_norm_w'], 'm_w_proj_a': out['m_w_proj_a'], 'm_w_proj_b': out['m_w_proj_b'], 'm_w_out': out['m_w_out'], 'm_ffn_norm_w': out['m_ffn_norm_w'], 'm_ffn_w_up': out['m_ffn_w_up'], 'm_ffn_conv_w': out['m_ffn_conv_w'], 'm_ffn_conv_b': out['m_ffn_conv_b'], 'm_ffn_w_down': out['m_ffn_w_down'], 'm_final_norm_w': out['m_final_norm_w'], 'v_mix_norm_w': out['v_mix_norm_w'], 'v_w_in': out['v_w_in'], 'v_gate_bias': out['v_gate_bias'], 'v_gmlp_ln_w': out['v_gmlp_ln_w'], 'v_gmlp_ln_b': out['v_gmlp_ln_b'], 'v_gmlp_ws': out['v_gmlp_ws'], 'v_gmlp_bs': out['v_gmlp_bs'], 'v_ssm_conv_w': out['v_ssm_conv_w'], 'v_ssm_conv_b': out['v_ssm_conv_b'], 'v_ssm_dt_bias': out['v_ssm_dt_bias'], 'v_ssm_a_log': out['v_ssm_a_log'], 'v_ssm_d': out['v_ssm_d'], 'v_ssm_norm_w': out['v_ssm_norm_w'], 'v_w_proj_a': out['v_w_proj_a'], 'v_w_proj_b': out['v_w_proj_b'], 'v_w_out': out['v_w_out'], 'v_ffn_norm_w': out['v_ffn_norm_w'], 'v_ffn_w_up': out['v_ffn_w_up'], 'v_ffn_conv_w': out['v_ffn_conv_w'], 'v_ffn_conv_b': out['v_ffn_conv_b'], 'v_ffn_w_down': out['v_ffn_w_down'], 'v_final_norm_w': out['v_final_norm_w']}


def _loss(weights, diff, rest, loss_target):
    with _jax.named_scope("forward"):
        args = {**rest, TWIN_DIFF_INPUT: diff, **{k: w.astype(_WEIGHT_DTYPES[k]) for k, w in weights.items()}}
        y = _forward(args)
    with _jax.named_scope("loss_head"):
        err = _jnp.square(y.astype(_jnp.float32) - loss_target)
        return 0.5 * _jnp.sum(_jnp.mean(err, axis=-1)) if err.ndim else 0.5 * err


def _adamw(w, g, m, v):
    m = ADAM_B1 * m + (1.0 - ADAM_B1) * g
    v = ADAM_B2 * v + (1.0 - ADAM_B2) * _jnp.square(g)
    m_hat = m / (1.0 - ADAM_B1 ** ADAM_STEP)
    v_hat = v / (1.0 - ADAM_B2 ** ADAM_STEP)
    delta = -ADAM_LR * (m_hat / (_jnp.sqrt(v_hat) + ADAM_EPS) + ADAM_WD * w)
    return delta, m, v


def reference(x, mix_norm_w, w_in, gate_bias, gmlp_ln_w, gmlp_ln_b, gmlp_ws, gmlp_bs, ssm_conv_w, ssm_conv_b, ssm_dt_bias, ssm_a_log, ssm_d, ssm_norm_w, w_proj_a, w_proj_b, w_out, ffn_norm_w, ffn_w_up, ffn_conv_w, ffn_conv_b, ffn_w_down, final_norm_w, loss_target, m_mix_norm_w, m_w_in, m_gate_bias, m_gmlp_ln_w, m_gmlp_ln_b, m_gmlp_ws, m_gmlp_bs, m_ssm_conv_w, m_ssm_conv_b, m_ssm_dt_bias, m_ssm_a_log, m_ssm_d, m_ssm_norm_w, m_w_proj_a, m_w_proj_b, m_w_out, m_ffn_norm_w, m_ffn_w_up, m_ffn_conv_w, m_ffn_conv_b, m_ffn_w_down, m_final_norm_w, v_mix_norm_w, v_w_in, v_gate_bias, v_gmlp_ln_w, v_gmlp_ln_b, v_gmlp_ws, v_gmlp_bs, v_ssm_conv_w, v_ssm_conv_b, v_ssm_dt_bias, v_ssm_a_log, v_ssm_d, v_ssm_norm_w, v_w_proj_a, v_w_proj_b, v_w_out, v_ffn_norm_w, v_ffn_w_up, v_ffn_conv_w, v_ffn_conv_b, v_ffn_w_down, v_final_norm_w):
    given = dict(x=x, mix_norm_w=mix_norm_w, w_in=w_in, gate_bias=gate_bias, gmlp_ln_w=gmlp_ln_w, gmlp_ln_b=gmlp_ln_b, gmlp_ws=gmlp_ws, gmlp_bs=gmlp_bs, ssm_conv_w=ssm_conv_w, ssm_conv_b=ssm_conv_b, ssm_dt_bias=ssm_dt_bias, ssm_a_log=ssm_a_log, ssm_d=ssm_d, ssm_norm_w=ssm_norm_w, w_proj_a=w_proj_a, w_proj_b=w_proj_b, w_out=w_out, ffn_norm_w=ffn_norm_w, ffn_w_up=ffn_w_up, ffn_conv_w=ffn_conv_w, ffn_conv_b=ffn_conv_b, ffn_w_down=ffn_w_down, final_norm_w=final_norm_w, loss_target=loss_target, m_mix_norm_w=m_mix_norm_w, m_w_in=m_w_in, m_gate_bias=m_gate_bias, m_gmlp_ln_w=m_gmlp_ln_w, m_gmlp_ln_b=m_gmlp_ln_b, m_gmlp_ws=m_gmlp_ws, m_gmlp_bs=m_gmlp_bs, m_ssm_conv_w=m_ssm_conv_w, m_ssm_conv_b=m_ssm_conv_b, m_ssm_dt_bias=m_ssm_dt_bias, m_ssm_a_log=m_ssm_a_log, m_ssm_d=m_ssm_d, m_ssm_norm_w=m_ssm_norm_w, m_w_proj_a=m_w_proj_a, m_w_proj_b=m_w_proj_b, m_w_out=m_w_out, m_ffn_norm_w=m_ffn_norm_w, m_ffn_w_up=m_ffn_w_up, m_ffn_conv_w=m_ffn_conv_w, m_ffn_conv_b=m_ffn_conv_b, m_ffn_w_down=m_ffn_w_down, m_final_norm_w=m_final_norm_w, v_mix_norm_w=v_mix_norm_w, v_w_in=v_w_in, v_gate_bias=v_gate_bias, v_gmlp_ln_w=v_gmlp_ln_w, v_gmlp_ln_b=v_gmlp_ln_b, v_gmlp_ws=v_gmlp_ws, v_gmlp_bs=v_gmlp_bs, v_ssm_conv_w=v_ssm_conv_w, v_ssm_conv_b=v_ssm_conv_b, v_ssm_dt_bias=v_ssm_dt_bias, v_ssm_a_log=v_ssm_a_log, v_ssm_d=v_ssm_d, v_ssm_norm_w=v_ssm_norm_w, v_w_proj_a=v_w_proj_a, v_w_proj_b=v_w_proj_b, v_w_out=v_w_out, v_ffn_norm_w=v_ffn_norm_w, v_ffn_w_up=v_ffn_w_up, v_ffn_conv_w=v_ffn_conv_w, v_ffn_conv_b=v_ffn_conv_b, v_ffn_w_down=v_ffn_w_down, v_final_norm_w=v_final_norm_w)
    weights = {n: given[n] for n in TWIN_WEIGHTS}
    shared = {n: given[n] for n in SHARED_INPUTS}
    per_example = {n: given[n] for n in ['x']}
    grad_fn = _jax.value_and_grad(_loss, argnums=(0, 1))

    def one_microbatch(ex, loss_target):
        ex = dict(ex)
        diff = ex.pop(TWIN_DIFF_INPUT)
        return grad_fn(weights, diff, {**shared, **ex}, loss_target)

    if N_MICROBATCH == 1:
        loss, (grad_w, grad_x) = one_microbatch(per_example, given["loss_target"])
    else:
        def body(carry, xs):
            loss_sum, grad_sum = carry
            l_k, (gw_k, gx_k) = one_microbatch(xs[0], xs[1])
            with _jax.named_scope("update"):
                return (loss_sum + l_k, _jax.tree.map(_jnp.add, grad_sum, gw_k)), gx_k

        init = (_jnp.zeros((), _jnp.float32), _jax.tree.map(_jnp.zeros_like, weights))
        (loss, grad_w), grad_x = _jax.lax.scan(body, init, (per_example, given["loss_target"]))
    with _jax.named_scope("update"):
        delta_w, new_m, new_v = {}, {}, {}
        for n in TWIN_WEIGHTS:
            delta_w[n], new_m[n], new_v[n] = _adamw(weights[n], grad_w[n], given["m_" + n], given["v_" + n])
    return (loss, grad_x, *[grad_w[n] for n in TWIN_WEIGHTS], *[delta_w[n] for n in TWIN_WEIGHTS],
            *[new_m[n] for n in TWIN_WEIGHTS], *[new_v[n] for n in TWIN_WEIGHTS])
```

```python
import jax
import jax.numpy as jnp
from jax import lax
from jax.experimental import pallas as pl
from jax.experimental.pallas import tpu as pltpu

F32 = jnp.float32
BF16 = jnp.bfloat16
MESH = pl.DeviceIdType.MESH

EPS = 1e-5
D_MODEL = 1024
GMLP_BLOCK = 128
GMLP_GROUPS = 8
CHUNK = 64
SSM_INNER = 2048
SSM_HEADS = 32
SSM_HEAD_DIM = 64
SSM_GROUPS = 4
SSM_HPG = 8
SSM_STATE = 128
SSM_CONV = 4
SSM_XBC = 3072
D_FF = 2816
FFN_CONV = 3
N_CHIPS = 4
N_DEV = 8

ADAM_LR = 0.001
ADAM_B1 = 0.9
ADAM_B2 = 0.999
ADAM_EPS = 1e-08
ADAM_WD = 0.01
ADAM_STEP = 10

VMEM_LIMIT_V7X = 56 * 1024 * 1024
SUBLANES = 8
LANES = 128


def _params(sem=None):
    return pltpu.CompilerParams(dimension_semantics=sem, vmem_limit_bytes=VMEM_LIMIT_V7X)


def _dot(a, b, ca=1, cb=0):
    return lax.dot_general(a.astype(BF16), b.astype(BF16), (((ca,), (cb,)), ((), ())),
                           preferred_element_type=F32)


def _mm(name, a, b, *, ta=False, tb=False, out_dtype=F32, bm, bn, bk, res=None):
    m, k = (a.shape[1], a.shape[0]) if ta else a.shape
    k2, n = (b.shape[1], b.shape[0]) if tb else b.shape
    assert k == k2 and m % bm == 0 and n % bn == 0 and k % bk == 0, (name, a.shape, b.shape)
    nk = k // bk
    a_spec = (pl.BlockSpec((bk, bm), lambda i, j, kk: (kk, i)) if ta
              else pl.BlockSpec((bm, bk), lambda i, j, kk: (i, kk)))
    b_spec = (pl.BlockSpec((bn, bk), lambda i, j, kk: (j, kk)) if tb
              else pl.BlockSpec((bk, bn), lambda i, j, kk: (kk, j)))
    o_spec = pl.BlockSpec((bm, bn), lambda i, j, kk: (i, j))
    has_res = res is not None

    def body(*refs):
        a_ref, b_ref = refs[0], refs[1]
        r_ref = refs[2] if has_res else None
        o_ref = refs[3] if has_res else refs[2]
        p = _dot(a_ref[...], b_ref[...], 0 if ta else 1, 1 if tb else 0)

        def finish(total):
            if has_res:
                total = total + r_ref[...]
            o_ref[...] = total.astype(out_dtype)

        if nk == 1:
            finish(p)
        else:
            acc_ref = refs[-1]
            kk = pl.program_id(2)

            @pl.when(kk == 0)
            def _():
                acc_ref[...] = p

            @pl.when(kk > 0)
            def _():
                acc_ref[...] += p

            @pl.when(kk == nk - 1)
            def _():
                finish(acc_ref[...])

    return pl.pallas_call(
        body, name=name,
        grid=(m // bm, n // bn, nk),
        in_specs=[a_spec, b_spec] + ([o_spec] if has_res else []),
        out_specs=o_spec,
        out_shape=jax.ShapeDtypeStruct((m, n), out_dtype),
        scratch_shapes=[pltpu.VMEM((bm, bn), F32)] if nk > 1 else [],
        compiler_params=_params(("parallel", "parallel", "arbitrary")),
    )(*([a, b] + ([res] if has_res else [])))


def _rows(name, fn, ins, params, outs, accs, *, tm, rs):
    nrow = ins[0][0].shape[-2]
    assert nrow % tm == 0 and tm % rs == 0, (name, nrow, tm, rs)
    n_in, n_p, n_out, n_acc = len(ins), len(params), len(outs), len(accs)
    in_specs = []
    for spec in ins:
        arr, width, cb = spec[:3]
        if len(spec) == 4:
            in_specs.append(pl.BlockSpec((None, tm, width), lambda i, cb=cb, lead=spec[3]: (lead, i, cb)))
        else:
            in_specs.append(pl.BlockSpec((tm, width), lambda i, cb=cb: (i, cb)))
    for p in params:
        in_specs.append(pl.BlockSpec(p.shape, lambda i, nd=p.ndim: (0,) * nd))
    out_specs = [pl.BlockSpec((tm, w), lambda i: (i, 0)) for w, _ in outs]
    out_specs += [pl.BlockSpec(s, lambda i: (0, 0)) for s in accs]
    out_shape = [jax.ShapeDtypeStruct((nrow, w), dt) for w, dt in outs]
    out_shape += [jax.ShapeDtypeStruct(s, F32) for s in accs]

    def body(*refs):
        in_refs = refs[:n_in]
        p_refs = refs[n_in:n_in + n_p]
        o_refs = refs[n_in + n_p:n_in + n_p + n_out]
        a_refs = refs[n_in + n_p + n_out:]
        pv = [p[...] for p in p_refs]

        if n_acc:
            @pl.when(pl.program_id(0) == 0)
            def _():
                for a_ref in a_refs:
                    a_ref[...] = jnp.zeros_like(a_ref)

        def step(r, carry):
            sl = pl.ds(pl.multiple_of(r * rs, rs), rs)
            vals = [ref[sl, :] for ref in in_refs]
            row_out, sums = fn(*vals, *pv)
            for o_ref, v in zip(o_refs, row_out):
                o_ref[sl, :] = v.astype(o_ref.dtype)
            return tuple(c + s for c, s in zip(carry, sums))

        init = tuple(jnp.zeros(s, F32) for s in accs)
        total = lax.fori_loop(0, tm // rs, step, init)
        for a_ref, t in zip(a_refs, total):
            a_ref[...] += t

    res = pl.pallas_call(
        body, name=name, grid=(nrow // tm,),
        in_specs=in_specs, out_specs=out_specs, out_shape=out_shape,
        compiler_params=_params(("arbitrary",)),
    )(*([s[0] for s in ins] + list(params)))
    return res


def _rms(x, w):
    return x * lax.rsqrt(jnp.mean(x * x, axis=-1, keepdims=True) + EPS) * w


def _colsum(v):
    return jnp.sum(v, axis=0, keepdims=True)


def _rms_fwd(name, x, w):
    def fn(xv, wv):
        return (_rms(xv, wv),), ()
    return _rows(name, fn, [(x, D_MODEL, 0)], [w], [(D_MODEL, BF16)], [], tm=512, rs=16)[0]


def _rms_bwd(name, x, w, dy, dres):
    def fn(xv, dyv, drv, wv):
        _, vjp = jax.vjp(_rms, xv, wv)
        dx, dw = vjp(dyv)
        return (drv + dx,), (dw,)
    return _rows(name, fn, [(x, D_MODEL, 0), (dy, D_MODEL, 0), (dres, D_MODEL, 0)], [w],
                 [(D_MODEL, F32)], [(1, D_MODEL)], tm=512, rs=16)


def _final_loss(name, h, target, w):
    def fn(hv, tv, wv):
        y, vjp = jax.vjp(_rms, hv, wv)
        err = y - tv
        part = 0.5 * jnp.sum(jnp.mean(err * err, axis=-1, keepdims=True), axis=0, keepdims=True)
        dh, dw = vjp(err / D_MODEL)
        return (dh,), (jnp.broadcast_to(part, (1, LANES)), dw)
    return _rows(name, fn, [(h, D_MODEL, 0), (target, D_MODEL, 0)], [w],
                 [(D_MODEL, F32)], [(1, LANES), (1, D_MODEL)], tm=512, rs=16)


def _merge(ga, gb, ya, yb, b0, b1):
    return jax.nn.sigmoid(ga + b0) * ya + jax.nn.sigmoid(gb + b1) * yb


def _merge_fwd(name, g, ya, yb, b0, b1):
    def fn(ga, gb, yav, ybv, b0v, b1v):
        return (_merge(ga, gb, yav, ybv, b0v, b1v),), ()
    return _rows(name, fn, [(g, D_MODEL, 0), (g, D_MODEL, 1), (ya, D_MODEL, 0), (yb, D_MODEL, 0)],
                 [b0, b1], [(D_MODEL, BF16)], [], tm=512, rs=16)[0]


def _merge_bwd(name, g, ya, yb, dm, b0, b1):
    def fn(ga, gb, yav, ybv, dmv, b0v, b1v):
        _, vjp = jax.vjp(_merge, ga, gb, yav, ybv, b0v, b1v)
        dga, dgb, dya, dyb, db0, db1 = vjp(dmv)
        return (jnp.concatenate([dga, dgb], axis=1), dya, dyb), (db0, db1)
    return _rows(name, fn,
                 [(g, D_MODEL, 0), (g, D_MODEL, 1), (ya, D_MODEL, 0), (yb, D_MODEL, 0), (dm, D_MODEL, 0)],
                 [b0, b1], [(2 * D_MODEL, BF16), (D_MODEL, BF16), (D_MODEL, BF16)],
                 [(1, D_MODEL), (1, D_MODEL)], tm=512, rs=16)


GROUP_W = SSM_INNER // SSM_GROUPS


def _gate_norm_group(y, z, nw):
    v = y * jax.nn.silu(z)
    return v * lax.rsqrt(jnp.mean(v * v, axis=-1, keepdims=True) + EPS) * nw


def _gate_norm_fwd(name, y, z, nw):
    def fn(yv, zv, nwv):
        parts = [_gate_norm_group(yv[:, k * GROUP_W:(k + 1) * GROUP_W], zv[:, k * GROUP_W:(k + 1) * GROUP_W],
                                  nwv[:, k * GROUP_W:(k + 1) * GROUP_W]) for k in range(SSM_GROUPS)]
        return (jnp.concatenate(parts, axis=1),), ()
    return _rows(name, fn, [(y, SSM_INNER, 0), (z, SSM_INNER, 0)], [nw], [(SSM_INNER, BF16)], [],
                 tm=512, rs=16)[0]


def _gate_norm_bwd(name, y, z, dout, nw):
    def fn(yv, zv, dv, nwv):
        dys, dzs, dns = [], [], []
        for k in range(SSM_GROUPS):
            sl = slice(k * GROUP_W, (k + 1) * GROUP_W)
            _, vjp = jax.vjp(_gate_norm_group, yv[:, sl], zv[:, sl], nwv[:, sl])
            dy, dz, dn = vjp(dv[:, sl])
            dys.append(dy), dzs.append(dz), dns.append(dn)
        return (jnp.concatenate(dys, axis=1), jnp.concatenate(dzs, axis=1)), (jnp.concatenate(dns, axis=1),)
    return _rows(name, fn, [(y, SSM_INNER, 0), (z, SSM_INNER, 0), (dout, SSM_INNER, 0)], [nw],
                 [(SSM_INNER, F32), (SSM_INNER, BF16)], [(1, SSM_INNER)], tm=512, rs=16)


def _softplus(v):
    return jnp.maximum(v, 0.0) + jnp.log1p(jnp.exp(-jnp.abs(v)))


def _chunk_cumsum(v, reverse=False):
    row = lax.broadcasted_iota(jnp.int32, v.shape, 0)
    step = 1
    while step < CHUNK:
        if reverse:
            shifted = pltpu.roll(v, CHUNK - step, axis=0)
            v = v + jnp.where(row < CHUNK - step, shifted, 0.0)
        else:
            shifted = pltpu.roll(v, step, axis=0)
            v = v + jnp.where(row >= step, shifted, 0.0)
        step *= 2
    return v


def _dt_prep(name, dt_raw, dt_bias, a_log):
    def fn(rv, bv, alv):
        dt = _softplus(rv + bv)
        return (dt, _chunk_cumsum(dt * (-jnp.exp(alv)))), ()
    return _rows(name, fn, [(dt_raw, SSM_HEADS, 0)], [dt_bias, a_log],
                 [(SSM_HEADS, F32), (SSM_HEADS, F32)], [], tm=512, rs=CHUNK)


def _dt_bwd(name, dt_raw, ddt, da1, da2, dt_bias, a_log):
    def fn(rv, ddv, d1, d2, bv, alv):
        pre = rv + bv
        dt = _softplus(pre)
        a_neg = -jnp.exp(alv)
        back = _chunk_cumsum(d1 + d2, reverse=True)
        d_dt = ddv + back * a_neg
        d_raw = d_dt * jax.nn.sigmoid(pre)
        return (d_raw,), (_colsum(d_raw), _colsum(back * dt) * a_neg)
    return _rows(name, fn, [(dt_raw, SSM_HEADS, 0), (ddt, SSM_HEADS, 0), (da1, SSM_HEADS, 0), (da2, SSM_HEADS, 0)],
                 [dt_bias, a_log], [(SSM_HEADS, BF16)], [(1, SSM_HEADS), (1, SSM_HEADS)], tm=512, rs=CHUNK)


def _adamw(name, w, g_parts, m, v, *, tm, rs):
    width = w.shape[1]
    n_g = len(g_parts)

    def fn(wv, mv, vv, *gs):
        g = gs[0]
        for extra in gs[1:]:
            g = g + extra
        m_new = ADAM_B1 * mv + (1.0 - ADAM_B1) * g
        v_new = ADAM_B2 * vv + (1.0 - ADAM_B2) * jnp.square(g)
        m_hat = m_new / (1.0 - ADAM_B1 ** ADAM_STEP)
        v_hat = v_new / (1.0 - ADAM_B2 ** ADAM_STEP)
        delta = -ADAM_LR * (m_hat / (jnp.sqrt(v_hat) + ADAM_EPS) + ADAM_WD * wv)
        return (g, delta, m_new, v_new), ()
    assert n_g >= 1
    return _rows(name, fn, [(w, width, 0), (m, width, 0), (v, width, 0)] + [(g, width, 0) for g in g_parts],
                 [], [(width, F32)] * 4, [], tm=tm, rs=rs)


def _sum_slots(name, stack, *, tm, rs):
    width = stack.shape[2]

    def fn(s0, s1, s2, s3):
        return (((s0 + s1) + s2) + s3,), ()
    return _rows(name, fn, [(stack, width, 0, k) for k in range(N_CHIPS)], [], [(width, F32)], [],
                 tm=tm, rs=rs)[0]


def _layernorm(v, w, b):
    mu = jnp.mean(v, axis=-1, keepdims=True)
    var = jnp.mean(jnp.square(v - mu), axis=-1, keepdims=True)
    return (v - mu) * lax.rsqrt(var + EPS) * w + b


def _gmlp_mask():
    t = lax.broadcasted_iota(jnp.int32, (GMLP_BLOCK, GMLP_BLOCK), 0) // CHUNK
    s = lax.broadcasted_iota(jnp.int32, (GMLP_BLOCK, GMLP_BLOCK), 1) // CHUNK
    return s <= t


GMLP_TM = 512


def _gmlp_fwd(name, za, ln_w, ln_b, ws, bs_col):
    nrow = za.shape[0]
    tm = GMLP_TM
    width = GMLP_GROUPS * GMLP_BLOCK

    def body(za_ref, lnw_ref, lnb_ref, ws_ref, bs_ref, o_ref, wm_ref):
        mask = _gmlp_mask()
        for g in range(GMLP_GROUPS):
            wm_ref[g] = jnp.where(mask, ws_ref[g], 0.0).astype(BF16)

        def block(n, carry):
            rows = pl.ds(pl.multiple_of(n * GMLP_BLOCK, GMLP_BLOCK), GMLP_BLOCK)
            for g in range(GMLP_GROUPS):
                cols = slice(g * GMLP_BLOCK, (g + 1) * GMLP_BLOCK)
                vcols = slice(width + g * GMLP_BLOCK, width + (g + 1) * GMLP_BLOCK)
                u = jax.nn.gelu(za_ref[rows, cols])
                v = jax.nn.gelu(za_ref[rows, vcols])
                vn = _layernorm(v, lnw_ref[g:g + 1, :], lnb_ref[g:g + 1, :])
                sv = _dot(wm_ref[g], vn) + bs_ref[g]
                o_ref[rows, cols] = (u * sv).astype(o_ref.dtype)
            return carry

        lax.fori_loop(0, tm // GMLP_BLOCK, block, 0)

    small = lambda a: pl.BlockSpec(a.shape, lambda i, nd=a.ndim: (0,) * nd)
    return pl.pallas_call(
        body, name=name, grid=(nrow // tm,),
        in_specs=[pl.BlockSpec((tm, 2 * width), lambda i: (i, 0)), small(ln_w), small(ln_b), small(ws), small(bs_col)],
        out_specs=pl.BlockSpec((tm, width), lambda i: (i, 0)),
        out_shape=jax.ShapeDtypeStruct((nrow, width), BF16),
        scratch_shapes=[pltpu.VMEM((GMLP_GROUPS, GMLP_BLOCK, GMLP_BLOCK), BF16)],
        compiler_params=_params(("arbitrary",)),
    )(za, ln_w, ln_b, ws, bs_col)


def _gmlp_bwd(name, za, dout, ln_w, ln_b, ws, bs_col):
    nrow = za.shape[0]
    tm = GMLP_TM
    width = GMLP_GROUPS * GMLP_BLOCK

    def body(za_ref, do_ref, lnw_ref, lnb_ref, ws_ref, bs_ref, dza_ref, dlnw_ref, dlnb_ref, dws_ref, dbs_ref, wm_ref):
        mask = _gmlp_mask()
        for g in range(GMLP_GROUPS):
            wm_ref[g] = jnp.where(mask, ws_ref[g], 0.0).astype(BF16)

        @pl.when(pl.program_id(0) == 0)
        def _():
            dlnw_ref[...] = jnp.zeros_like(dlnw_ref)
            dlnb_ref[...] = jnp.zeros_like(dlnb_ref)
            dws_ref[...] = jnp.zeros_like(dws_ref)
            dbs_ref[...] = jnp.zeros_like(dbs_ref)

        def block(n, carry):
            rows = pl.ds(pl.multiple_of(n * GMLP_BLOCK, GMLP_BLOCK), GMLP_BLOCK)
            for g in range(GMLP_GROUPS):
                cols = slice(g * GMLP_BLOCK, (g + 1) * GMLP_BLOCK)
                vcols = slice(width + g * GMLP_BLOCK, width + (g + 1) * GMLP_BLOCK)
                u, gelu_u_vjp = jax.vjp(jax.nn.gelu, za_ref[rows, cols])
                v, gelu_v_vjp = jax.vjp(jax.nn.gelu, za_ref[rows, vcols])
                vn, ln_vjp = jax.vjp(_layernorm, v, lnw_ref[g:g + 1, :], lnb_ref[g:g + 1, :])
                sv = _dot(wm_ref[g], vn) + bs_ref[g]
                d_o = do_ref[rows, cols]
                dsv = d_o * u
                d_wm = _dot(dsv, vn, 1, 1)
                dvn = _dot(wm_ref[g], dsv, 0, 0)
                dv, dlnw, dlnb = ln_vjp(dvn)
                dza_ref[rows, cols] = gelu_u_vjp(d_o * sv)[0].astype(dza_ref.dtype)
                dza_ref[rows, vcols] = gelu_v_vjp(dv)[0].astype(dza_ref.dtype)
                dlnw_ref[g:g + 1, :] += dlnw
                dlnb_ref[g:g + 1, :] += dlnb
                dws_ref[g] += jnp.where(mask, d_wm, 0.0)
                dbs_ref[g] += jnp.sum(dsv, axis=1, keepdims=True)
            return carry

        lax.fori_loop(0, tm // GMLP_BLOCK, block, 0)

    small = lambda a: pl.BlockSpec(a.shape, lambda i, nd=a.ndim: (0,) * nd)
    return pl.pallas_call(
        body, name=name, grid=(nrow // tm,),
        in_specs=[pl.BlockSpec((tm, 2 * width), lambda i: (i, 0)), pl.BlockSpec((tm, width), lambda i: (i, 0)),
                  small(ln_w), small(ln_b), small(ws), small(bs_col)],
        out_specs=[pl.BlockSpec((tm, 2 * width), lambda i: (i, 0)), small(ln_w), small(ln_b), small(ws), small(bs_col)],
        out_shape=[jax.ShapeDtypeStruct((nrow, 2 * width), BF16), jax.ShapeDtypeStruct(ln_w.shape, F32),
                   jax.ShapeDtypeStruct(ln_b.shape, F32), jax.ShapeDtypeStruct(ws.shape, F32),
                   jax.ShapeDtypeStruct(bs_col.shape, F32)],
        scratch_shapes=[pltpu.VMEM((GMLP_GROUPS, GMLP_BLOCK, GMLP_BLOCK), BF16)],
        compiler_params=_params(("arbitrary",)),
    )(za, dout, ln_w, ln_b, ws, bs_col)


CONV_TM = 256
CONV_RS = 32


def _tap_rows(w_ref):
    return [w_ref[k:k + 1, :] for k in range(w_ref.shape[0])]


def _conv_rows(win, w, rs):
    taps = len(w)
    out = w[taps - 1] * win[SUBLANES:, :]
    for k in range(taps - 1):
        back = taps - 1 - k
        out = out + w[k] * pltpu.roll(win, back, axis=0)[SUBLANES:, :]
    return out


def _conv_t_rows(win, w, rs):
    taps = len(w)
    out = w[taps - 1] * win[:rs, :]
    for k in range(taps - 1):
        ahead = taps - 1 - k
        out = out + w[k] * pltpu.roll(win, rs + SUBLANES - ahead, axis=0)[:rs, :]
    return out


def _conv_dw_rows(d, xwin, taps):
    rows = []
    for k in range(taps):
        back = taps - 1 - k
        xs = xwin[SUBLANES:, :] if back == 0 else pltpu.roll(xwin, back, axis=0)[SUBLANES:, :]
        rows.append(jnp.sum(d * xs, axis=0, keepdims=True))
    return rows


def _halo_specs(nrow, tm, tc):
    per = tm // SUBLANES
    last = nrow // SUBLANES - 1
    main = pl.BlockSpec((tm, tc), lambda j, i: (i, j))
    before = pl.BlockSpec((SUBLANES, tc), lambda j, i: (jnp.maximum(i * per - 1, 0), j))
    after = pl.BlockSpec((SUBLANES, tc), lambda j, i: (jnp.minimum((i + 1) * per, last), j))
    return main, before, after


def _col_spec(rows, tc):
    return pl.BlockSpec((rows, tc), lambda j, i: (0, j))


def _conv_fwd(name, x, w, b, *, tc):
    nrow, ncol = x.shape
    taps = w.shape[0]
    tm, rs = CONV_TM, CONV_RS
    main, before, _ = _halo_specs(nrow, tm, tc)

    def body(x_ref, xb_ref, w_ref, b_ref, o_ref):
        first = pl.program_id(1) == 0
        wv, bv = _tap_rows(w_ref), b_ref[...]

        def step(r, prev):
            sl = pl.ds(pl.multiple_of(r * rs, rs), rs)
            cur = x_ref[sl, :]
            o_ref[sl, :] = _conv_rows(jnp.concatenate([prev, cur], axis=0), wv, rs) + bv
            return cur[rs - SUBLANES:, :]

        lax.fori_loop(0, tm // rs, step, jnp.where(first, 0.0, xb_ref[...]))

    return pl.pallas_call(
        body, name=name, grid=(ncol // tc, nrow // tm),
        in_specs=[main, before, _col_spec(taps, tc), _col_spec(1, tc)],
        out_specs=main, out_shape=jax.ShapeDtypeStruct((nrow, ncol), F32),
        compiler_params=_params(("parallel", "arbitrary")),
    )(x, x, w, b)


def _conv_bwd(name, dpre, x, w, *, tc):
    nrow, ncol = x.shape
    taps = w.shape[0]
    tm, rs = CONV_TM, CONV_RS
    nsub = tm // rs
    main, before, after = _halo_specs(nrow, tm, tc)

    def body(d_ref, da_ref, x_ref, xb_ref, w_ref, dx_ref, dw_ref, db_ref):
        i = pl.program_id(1)
        first, last = i == 0, i == pl.num_programs(1) - 1
        wv = _tap_rows(w_ref)
        x_before = jnp.where(first, 0.0, xb_ref[...])

        @pl.when(first)
        def _():
            dw_ref[...] = jnp.zeros_like(dw_ref)
            db_ref[...] = jnp.zeros_like(db_ref)

        def step(q, carry):
            nxt, dw, db = carry
            r = nsub - 1 - q
            sl = pl.ds(pl.multiple_of(r * rs, rs), rs)
            cur = d_ref[sl, :]
            dx_ref[sl, :] = _conv_t_rows(jnp.concatenate([cur, nxt], axis=0), wv, rs).astype(dx_ref.dtype)
            inner = x_ref[pl.ds(pl.multiple_of(jnp.maximum(r * rs - SUBLANES, 0), SUBLANES), SUBLANES), :]
            xwin = jnp.concatenate([jnp.where(r == 0, x_before, inner), x_ref[sl, :]], axis=0)
            dw = tuple(a + s for a, s in zip(dw, _conv_dw_rows(cur, xwin, taps)))
            return cur[:SUBLANES, :], dw, db + _colsum(cur)

        zero_row = jnp.zeros((1, tc), F32)
        init = (jnp.where(last, 0.0, da_ref[...]), (zero_row,) * taps, zero_row)
        _, dw, db = lax.fori_loop(0, nsub, step, init)
        for k in range(taps):
            dw_ref[k:k + 1, :] += dw[k]
        db_ref[...] += db

    return pl.pallas_call(
        body, name=name, grid=(ncol // tc, nrow // tm),
        in_specs=[main, after, main, before, _col_spec(taps, tc)],
        out_specs=[main, _col_spec(taps, tc), _col_spec(1, tc)],
        out_shape=[jax.ShapeDtypeStruct((nrow, ncol), BF16), jax.ShapeDtypeStruct((taps, ncol), F32),
                   jax.ShapeDtypeStruct((1, ncol), F32)],
        compiler_params=_params(("parallel", "arbitrary")),
    )(dpre, dpre, x, x, w)


def _glu(gate, val):
    return jax.nn.silu(gate) * val


def _ffn_act_fwd(name, pg, pv, wg, wv, bg, bv, *, tc):
    nrow, ncol = pg.shape
    taps = wg.shape[0]
    tm, rs = CONV_TM, CONV_RS
    main, before, _ = _halo_specs(nrow, tm, tc)

    def body(pg_ref, pgb_ref, pv_ref, pvb_ref, wg_ref, wv_ref, bg_ref, bv_ref, g_ref, v_ref, a_ref):
        first = pl.program_id(1) == 0
        wgv, wvv, bgv, bvv = _tap_rows(wg_ref), _tap_rows(wv_ref), bg_ref[...], bv_ref[...]

        def step(r, carry):
            prev_g, prev_v = carry
            sl = pl.ds(pl.multiple_of(r * rs, rs), rs)
            cur_g, cur_v = pg_ref[sl, :], pv_ref[sl, :]
            gate = _conv_rows(jnp.concatenate([prev_g, cur_g], axis=0), wgv, rs) + bgv
            val = _conv_rows(jnp.concatenate([prev_v, cur_v], axis=0), wvv, rs) + bvv
            g_ref[sl, :] = gate
            v_ref[sl, :] = val
            a_ref[sl, :] = _glu(gate, val).astype(a_ref.dtype)
            return cur_g[rs - SUBLANES:, :], cur_v[rs - SUBLANES:, :]

        lax.fori_loop(0, tm // rs, step, (jnp.where(first, 0.0, pgb_ref[...]), jnp.where(first, 0.0, pvb_ref[...])))

    return pl.pallas_call(
        body, name=name, grid=(ncol // tc, nrow // tm),
        in_specs=[main, before, main, before, _col_spec(taps, tc), _col_spec(taps, tc), _col_spec(1, tc), _col_spec(1, tc)],
        out_specs=[main, main, main],
        out_shape=[jax.ShapeDtypeStruct((nrow, ncol), F32), jax.ShapeDtypeStruct((nrow, ncol), F32),
                   jax.ShapeDtypeStruct((nrow, ncol), BF16)],
        compiler_params=_params(("parallel", "arbitrary")),
    )(pg, pg, pv, pv, wg, wv, bg, bv)


def _ffn_act_bwd(name, dact, gate, val):
    def fn(dv, gv, vv):
        _, vjp = jax.vjp(_glu, gv, vv)
        dg, dval = vjp(dv)
        return (dg, dval), ()
    width = dact.shape[1]
    return _rows(name, fn, [(dact, width, 0), (gate, width, 0), (val, width, 0)], [],
                 [(width, F32), (width, F32)], [], tm=256, rs=8)


SSD_TM = 256
SSD_CHUNKS = SSD_TM // CHUNK
X_OFF, B_OFF, C_OFF = 0, SSM_INNER, SSM_INNER + SSM_GROUPS * SSM_STATE
HP = SSM_HPG * SSM_HEAD_DIM


def _causal():
    row = lax.broadcasted_iota(jnp.int32, (CHUNK, CHUNK), 0)
    col = lax.broadcasted_iota(jnp.int32, (CHUNK, CHUNK), 1)
    return col <= row


def _expand_heads(dst_ref, src_ref, rows):
    for h in range(SSM_HEADS):
        dst_ref[:, h * SSM_HEAD_DIM:(h + 1) * SSM_HEAD_DIM] = jnp.broadcast_to(
            src_ref[rows, h:h + 1], (CHUNK, SSM_HEAD_DIM))


def _chunk_decays(ac_ref, rows, last, dt_ref, ea_ref, ee_ref, dtx_ref, eax_ref, eex_ref):
    a_all = ac_ref[rows, :]
    ea_ref[...] = jnp.exp(a_all)
    ee_ref[...] = jnp.exp(ac_ref[last:last + 1, :] - a_all)
    whole = slice(0, CHUNK)
    _expand_heads(dtx_ref, dt_ref, rows)
    _expand_heads(eax_ref, ea_ref, whole)
    _expand_heads(eex_ref, ee_ref, whole)


def _ssd_fwd(name, pre, dt, a_cum, a_cum_t, d_skip):
    nrow = pre.shape[0]
    tm = SSD_TM

    def body(pre_ref, dt_ref, ac_ref, act_ref, d_ref, y_ref, st_ref,
             h_ref, xbc_ref, dtx_ref, eax_ref, eex_ref, yoff_ref, ea_ref, ee_ref):
        @pl.when(pl.program_id(0) == 0)
        def _():
            h_ref[...] = jnp.zeros_like(h_ref)

        causal = _causal()
        for ci in range(SSD_CHUNKS):
            rows = slice(ci * CHUNK, (ci + 1) * CHUNK)
            last = ci * CHUNK + CHUNK - 1
            xbc_ref[...] = jax.nn.silu(pre_ref[rows, :])
            _chunk_decays(ac_ref, rows, last, dt_ref, ea_ref, ee_ref, dtx_ref, eax_ref, eex_ref)
            st_ref[ci] = h_ref[...]
            for g in range(SSM_GROUPS):
                gcols = slice(g * HP, (g + 1) * HP)
                bm = xbc_ref[:, B_OFF + g * SSM_STATE:B_OFF + (g + 1) * SSM_STATE]
                cm = xbc_ref[:, C_OFF + g * SSM_STATE:C_OFF + (g + 1) * SSM_STATE]
                cb = _dot(cm, bm, 1, 1)
                h_g = h_ref[gcols, :]
                yoff_ref[...] = _dot(cm, h_g, 1, 1) * eax_ref[:, gcols]
                xdt_g = xbc_ref[:, gcols] * dtx_ref[:, gcols]
                for hh in range(SSM_HPG):
                    h = g * SSM_HPG + hh
                    hcols = slice(h * SSM_HEAD_DIM, (h + 1) * SSM_HEAD_DIM)
                    lcols = slice(hh * SSM_HEAD_DIM, (hh + 1) * SSM_HEAD_DIM)
                    seg = ac_ref[rows, h:h + 1] - act_ref[h:h + 1, rows]
                    decay = jnp.where(causal, jnp.exp(jnp.where(causal, seg, 0.0)), 0.0)
                    xdt = xbc_ref[:, hcols] * dtx_ref[:, hcols]
                    y = _dot(cb * decay, xdt) + yoff_ref[:, lcols] + d_ref[:, h:h + 1] * xbc_ref[:, hcols]
                    y_ref[rows, hcols] = y
                new = _dot(xdt_g * eex_ref[:, gcols], bm, 0, 0)
                for hh in range(SSM_HPG):
                    h = g * SSM_HPG + hh
                    hrows = slice(h * SSM_HEAD_DIM, (h + 1) * SSM_HEAD_DIM)
                    lrows = slice(hh * SSM_HEAD_DIM, (hh + 1) * SSM_HEAD_DIM)
                    h_ref[hrows, :] = jnp.exp(ac_ref[last:last + 1, h:h + 1]) * h_ref[hrows, :] + new[lrows, :]

    nchunk = nrow // CHUNK
    return pl.pallas_call(
        body, name=name, grid=(nrow // tm,),
        in_specs=[pl.BlockSpec((tm, SSM_XBC), lambda i: (i, 0)), pl.BlockSpec((tm, SSM_HEADS), lambda i: (i, 0)),
                  pl.BlockSpec((tm, SSM_HEADS), lambda i: (i, 0)), pl.BlockSpec((SSM_HEADS, tm), lambda i: (0, i)),
                  pl.BlockSpec((1, SSM_HEADS), lambda i: (0, 0))],
        out_specs=[pl.BlockSpec((tm, SSM_INNER), lambda i: (i, 0)),
                   pl.BlockSpec((SSD_CHUNKS, SSM_INNER, SSM_STATE), lambda i: (i, 0, 0))],
        out_shape=[jax.ShapeDtypeStruct((nrow, SSM_INNER), F32),
                   jax.ShapeDtypeStruct((nchunk, SSM_INNER, SSM_STATE), F32)],
        scratch_shapes=[pltpu.VMEM((SSM_INNER, SSM_STATE), F32), pltpu.VMEM((CHUNK, SSM_XBC), F32),
                        pltpu.VMEM((CHUNK, SSM_INNER), F32), pltpu.VMEM((CHUNK, SSM_INNER), F32),
                        pltpu.VMEM((CHUNK, SSM_INNER), F32), pltpu.VMEM((CHUNK, HP), F32),
                        pltpu.VMEM((CHUNK, SSM_HEADS), F32), pltpu.VMEM((CHUNK, SSM_HEADS), F32)],
        compiler_params=_params(("arbitrary",)),
    )(pre, dt, a_cum, a_cum_t, d_skip)


def _ssd_bwd(name, pre, dt, a_cum, a_cum_t, d_skip, states, dy):
    nrow = pre.shape[0]
    tm = SSD_TM
    ntile = nrow // tm

    def body(pre_ref, dt_ref, ac_ref, act_ref, d_ref, st_ref, dy_ref,
             dpre_ref, ddt_ref, da_ref, dat_ref, dd_ref,
             dh_ref, xbc_ref, dxbc_ref, dtx_ref, eax_ref, eex_ref, yoff_ref, bdh_ref, ea_ref, ee_ref):
        @pl.when(pl.program_id(0) == 0)
        def _():
            dh_ref[...] = jnp.zeros_like(dh_ref)
            dd_ref[...] = jnp.zeros_like(dd_ref)

        causal = _causal()
        is_last_row = lax.broadcasted_iota(jnp.int32, (CHUNK, 1), 0) == CHUNK - 1
        head_lane = lax.broadcasted_iota(jnp.int32, (CHUNK, SSM_HEADS), 1)
        head_lane_row = lax.broadcasted_iota(jnp.int32, (1, SSM_HEADS), 1)
        for ci in reversed(range(SSD_CHUNKS)):
            rows = slice(ci * CHUNK, (ci + 1) * CHUNK)
            last = ci * CHUNK + CHUNK - 1
            pre_v = pre_ref[rows, :]
            xbc_ref[...] = jax.nn.silu(pre_v)
            _chunk_decays(ac_ref, rows, last, dt_ref, ea_ref, ee_ref, dtx_ref, eax_ref, eex_ref)
            ddt = jnp.zeros((CHUNK, SSM_HEADS), F32)
            da = jnp.zeros((CHUNK, SSM_HEADS), F32)
            dd = jnp.zeros((1, SSM_HEADS), F32)
            for g in range(SSM_GROUPS):
                gcols = slice(g * HP, (g + 1) * HP)
                bcols = slice(B_OFF + g * SSM_STATE, B_OFF + (g + 1) * SSM_STATE)
                ccols = slice(C_OFF + g * SSM_STATE, C_OFF + (g + 1) * SSM_STATE)
                bm, cm = xbc_ref[:, bcols], xbc_ref[:, ccols]
                cb = _dot(cm, bm, 1, 1)
                h_g = st_ref[ci, gcols, :]
                dh_g = dh_ref[gcols, :]
                dyea_g = dy_ref[rows, gcols] * eax_ref[:, gcols]
                xde_g = xbc_ref[:, gcols] * dtx_ref[:, gcols] * eex_ref[:, gcols]
                yoff_ref[...] = _dot(cm, h_g, 1, 1)
                bdh_ref[...] = _dot(bm, dh_g, 1, 1)
                dw_sum = jnp.zeros((CHUNK, CHUNK), F32)
                for hh in range(SSM_HPG):
                    h = g * SSM_HPG + hh
                    hcols = slice(h * SSM_HEAD_DIM, (h + 1) * SSM_HEAD_DIM)
                    lcols = slice(hh * SSM_HEAD_DIM, (hh + 1) * SSM_HEAD_DIM)
                    seg = ac_ref[rows, h:h + 1] - act_ref[h:h + 1, rows]
                    decay = jnp.where(causal, jnp.exp(jnp.where(causal, seg, 0.0)), 0.0)
                    mm = cb * decay
                    x_h = xbc_ref[:, hcols]
                    xdt = x_h * dtx_ref[:, hcols]
                    dy_h = dy_ref[rows, hcols]
                    dm = _dot(dy_h, xdt, 1, 1)
                    ebdh = eex_ref[:, hcols] * bdh_ref[:, lcols]
                    dxd = _dot(mm, dy_h, 0, 0) + ebdh
                    dw_sum = dw_sum + dm * decay
                    q = dm * mm
                    t_state = xdt * ebdh
                    r1 = jnp.sum(q, axis=1, keepdims=True) + jnp.sum(
                        eax_ref[:, hcols] * dy_h * yoff_ref[:, lcols] - t_state, axis=1, keepdims=True)
                    r2 = jnp.sum(dxd * x_h, axis=1, keepdims=True)
                    hdh = jnp.sum(st_ref[ci, hcols, :] * dh_ref[hcols, :], axis=1, keepdims=True)
                    e_last = jnp.exp(ac_ref[last:last + 1, h:h + 1])
                    da_last = jnp.sum(t_state, keepdims=True) + e_last * jnp.sum(hdh, keepdims=True)
                    col = r1 + jnp.where(is_last_row, da_last, 0.0)
                    da = jnp.where(head_lane == h, col, da)
                    ddt = jnp.where(head_lane == h, r2, ddt)
                    dat_ref[h:h + 1, rows] = -jnp.sum(q, axis=0, keepdims=True)
                    dd = jnp.where(head_lane_row == h, jnp.sum(dy_h * x_h, keepdims=True), dd)
                    dxbc_ref[:, hcols] = dxd * dtx_ref[:, hcols] + d_ref[:, h:h + 1] * dy_h
                dxbc_ref[:, ccols] = _dot(dw_sum, bm) + _dot(dyea_g, h_g)
                dxbc_ref[:, bcols] = _dot(dw_sum, cm, 0, 0) + _dot(xde_g, dh_g)
                dh_new = _dot(dyea_g, cm, 0, 0)
                for hh in range(SSM_HPG):
                    h = g * SSM_HPG + hh
                    hrows = slice(h * SSM_HEAD_DIM, (h + 1) * SSM_HEAD_DIM)
                    lrows = slice(hh * SSM_HEAD_DIM, (hh + 1) * SSM_HEAD_DIM)
                    dh_ref[hrows, :] = jnp.exp(ac_ref[last:last + 1, h:h + 1]) * dh_ref[hrows, :] + dh_new[lrows, :]
            sig = jax.nn.sigmoid(pre_v)
            dpre_ref[rows, :] = dxbc_ref[...] * (sig * (1.0 + pre_v * (1.0 - sig)))
            ddt_ref[rows, :] = ddt
            da_ref[rows, :] = da
            dd_ref[...] += dd

    rev = lambda i: ntile - 1 - i
    return pl.pallas_call(
        body, name=name, grid=(ntile,),
        in_specs=[pl.BlockSpec((tm, SSM_XBC), lambda i: (rev(i), 0)), pl.BlockSpec((tm, SSM_HEADS), lambda i: (rev(i), 0)),
                  pl.BlockSpec((tm, SSM_HEADS), lambda i: (rev(i), 0)), pl.BlockSpec((SSM_HEADS, tm), lambda i: (0, rev(i))),
                  pl.BlockSpec((1, SSM_HEADS), lambda i: (0, 0)),
                  pl.BlockSpec((SSD_CHUNKS, SSM_INNER, SSM_STATE), lambda i: (rev(i), 0, 0)),
                  pl.BlockSpec((tm, SSM_INNER), lambda i: (rev(i), 0))],
        out_specs=[pl.BlockSpec((tm, SSM_XBC), lambda i: (rev(i), 0)), pl.BlockSpec((tm, SSM_HEADS), lambda i: (rev(i), 0)),
                   pl.BlockSpec((tm, SSM_HEADS), lambda i: (rev(i), 0)), pl.BlockSpec((SSM_HEADS, tm), lambda i: (0, rev(i))),
                   pl.BlockSpec((1, SSM_HEADS), lambda i: (0, 0))],
        out_shape=[jax.ShapeDtypeStruct((nrow, SSM_XBC), F32), jax.ShapeDtypeStruct((nrow, SSM_HEADS), F32),
                   jax.ShapeDtypeStruct((nrow, SSM_HEADS), F32), jax.ShapeDtypeStruct((SSM_HEADS, nrow), F32),
                   jax.ShapeDtypeStruct((1, SSM_HEADS), F32)],
        scratch_shapes=[pltpu.VMEM((SSM_INNER, SSM_STATE), F32), pltpu.VMEM((CHUNK, SSM_XBC), F32),
                        pltpu.VMEM((CHUNK, SSM_XBC), F32), pltpu.VMEM((CHUNK, SSM_INNER), F32),
                        pltpu.VMEM((CHUNK, SSM_INNER), F32), pltpu.VMEM((CHUNK, SSM_INNER), F32),
                        pltpu.VMEM((CHUNK, HP), F32), pltpu.VMEM((CHUNK, HP), F32),
                        pltpu.VMEM((CHUNK, SSM_HEADS), F32), pltpu.VMEM((CHUNK, SSM_HEADS), F32)],
        compiler_params=_params(("arbitrary",)),
    )(pre, dt, a_cum, a_cum_t, d_skip, states, dy)


def _local_step(x, target, w):
    g = {}
    bs_col = w["gmlp_bs"].reshape(GMLP_GROUPS, GMLP_BLOCK, 1)
    b0, b1 = w["gate_bias"][0:1], w["gate_bias"][1:2]

    xn = _rms_fwd("mix_norm", x, w["mix_norm_w"])
    big = dict(bm=1024, bn=1024, bk=1024)
    gates = _mm("in_gates", xn, w["w_g"], **big)
    za = _mm("in_gmlp", xn, w["w_za"], **big)
    z = _mm("in_z", xn, w["w_z"], **big)
    xbc = _mm("in_xbc", xn, w["w_xbc"], **big)
    dt_raw = _mm("in_dt", xn, w["w_dt"], bm=1024, bn=SSM_HEADS, bk=1024)

    ya_pre = _gmlp_fwd("gmlp_fwd", za, w["gmlp_ln_w"], w["gmlp_ln_b"], w["gmlp_ws"], bs_col)
    y_a = _mm("proj_a", ya_pre, w["w_proj_a"], **big)

    pre = _conv_fwd("ssm_conv_fwd", xbc, w["ssm_conv_w"], w["ssm_conv_b"], tc=1024)
    dt, a_cum = _dt_prep("dt_prep", dt_raw, w["ssm_dt_bias"], w["ssm_a_log"])
    a_cum_t = a_cum.T
    y_ssd, states = _ssd_fwd("ssd_fwd", pre, dt, a_cum, a_cum_t, w["ssm_d"])
    yb_pre = _gate_norm_fwd("gate_norm_fwd", y_ssd, z, w["ssm_norm_w"])
    y_b = _mm("proj_b", yb_pre, w["w_proj_b"], **big)

    merged = _merge_fwd("merge_fwd", gates, y_a, y_b, b0, b1)
    h1 = _mm("out_proj", merged, w["w_out"], res=x, **big)

    hn = _rms_fwd("ffn_norm", h1, w["ffn_norm_w"])
    half = dict(bm=1024, bn=D_FF // 2, bk=1024)
    pg = _mm("ffn_up_gate", hn, w["w_up_g"], **half)
    pv = _mm("ffn_up_val", hn, w["w_up_v"], **half)
    cw, cb = w["ffn_conv_w"], w["ffn_conv_b"]
    gate, val, act = _ffn_act_fwd("ffn_act_fwd", pg, pv, cw[:, :D_FF], cw[:, D_FF:], cb[:, :D_FF], cb[:, D_FF:],
                                  tc=D_FF // 2)
    h2 = _mm("ffn_down", act, w["w_down"], res=h1, bm=512, bn=1024, bk=D_FF // 2)

    dh2, loss_part, g["final_norm_w"] = _final_loss("final_loss", h2, target, w["final_norm_w"].reshape(1, D_MODEL))

    dact = _mm("d_act", dh2, w["w_down"], tb=True, bm=1024, bn=D_FF // 2, bk=1024)
    g["w_down"] = _mm("dw_down", act, dh2, ta=True, bm=D_FF // 2, bn=1024, bk=512)
    dgate, dval = _ffn_act_bwd("ffn_act_bwd", dact, gate, val)
    dpg, dcwg, dcbg = _conv_bwd("ffn_conv_bwd_gate", dgate, pg, cw[:, :D_FF], tc=D_FF // 2)
    dpv, dcwv, dcbv = _conv_bwd("ffn_conv_bwd_val", dval, pv, cw[:, D_FF:], tc=D_FF // 2)
    g["ffn_conv_w"] = jnp.concatenate([dcwg, dcwv], axis=1)
    g["ffn_conv_b"] = jnp.concatenate([dcbg, dcbv], axis=1)
    back = dict(bm=1024, bn=1024, bk=D_FF // 2)
    dhn = _mm("d_hn_gate", dpg, w["w_up_g"], tb=True, **back)
    dhn = _mm("d_hn_val", dpv, w["w_up_v"], tb=True, res=dhn, **back)
    wgrad = dict(ta=True, bk=512)
    g["w_up_g"] = _mm("dw_up_gate", hn, dpg, bm=1024, bn=D_FF // 2, **wgrad)
    g["w_up_v"] = _mm("dw_up_val", hn, dpv, bm=1024, bn=D_FF // 2, **wgrad)
    dh1, g["ffn_norm_w"] = _rms_bwd("ffn_norm_bwd", h1, w["ffn_norm_w"], dhn, dh2)

    dmerged = _mm("d_merged", dh1, w["w_out"], tb=True, **big)
    g["w_out"] = _mm("dw_out", merged, dh1, bm=1024, bn=1024, **wgrad)
    dgates, dya, dyb, db0, db1 = _merge_bwd("merge_bwd", gates, y_a, y_b, dmerged, b0, b1)
    g["gate_bias"] = jnp.concatenate([db0, db1], axis=0)

    dya_pre = _mm("d_ya_pre", dya, w["w_proj_a"], tb=True, **big)
    g["w_proj_a"] = _mm("dw_proj_a", ya_pre, dya, bm=1024, bn=1024, **wgrad)
    dza, g["gmlp_ln_w"], g["gmlp_ln_b"], g["gmlp_ws"], dbs = _gmlp_bwd(
        "gmlp_bwd", za, dya_pre, w["gmlp_ln_w"], w["gmlp_ln_b"], w["gmlp_ws"], bs_col)
    g["gmlp_bs"] = dbs.reshape(GMLP_GROUPS, GMLP_BLOCK)

    dyb_pre = _mm("d_yb_pre", dyb, w["w_proj_b"], tb=True, **big)
    g["w_proj_b"] = _mm("dw_proj_b", yb_pre, dyb, bm=1024, bn=1024, **wgrad)
    dy_ssd, dz, g["ssm_norm_w"] = _gate_norm_bwd("gate_norm_bwd", y_ssd, z, dyb_pre, w["ssm_norm_w"])
    dpre, ddt, da_tok, da_head_t, g["ssm_d"] = _ssd_bwd("ssd_bwd", pre, dt, a_cum, a_cum_t, w["ssm_d"], states, dy_ssd)
    ddt_raw, g["ssm_dt_bias"], g["ssm_a_log"] = _dt_bwd("dt_bwd", dt_raw, ddt, da_tok, da_head_t.T,
                                                         w["ssm_dt_bias"], w["ssm_a_log"])
    dxbc, g["ssm_conv_w"], g["ssm_conv_b"] = _conv_bwd("ssm_conv_bwd", dpre, xbc, w["ssm_conv_w"], tc=1024)

    dxn = _mm("d_xn_gates", dgates, w["w_g"], tb=True, **big)
    dxn = _mm("d_xn_gmlp", dza, w["w_za"], tb=True, res=dxn, **big)
    dxn = _mm("d_xn_z", dz, w["w_z"], tb=True, res=dxn, **big)
    dxn = _mm("d_xn_xbc", dxbc, w["w_xbc"], tb=True, res=dxn, **big)
    dxn = _mm("d_xn_dt", ddt_raw, w["w_dt"], tb=True, res=dxn, bm=1024, bn=1024, bk=SSM_HEADS)
    g["w_g"] = _mm("dw_gates", xn, dgates, bm=1024, bn=1024, **wgrad)
    g["w_za"] = _mm("dw_gmlp", xn, dza, bm=1024, bn=1024, **wgrad)
    g["w_z"] = _mm("dw_z", xn, dz, bm=1024, bn=1024, **wgrad)
    g["w_xbc"] = _mm("dw_xbc", xn, dxbc, bm=1024, bn=1024, **wgrad)
    g["w_dt"] = _mm("dw_dt", xn, ddt_raw, bm=1024, bn=SSM_HEADS, **wgrad)
    grad_x, g["mix_norm_w"] = _rms_bwd("mix_norm_bwd", x, w["mix_norm_w"], dxn, dh1)
    return loss_part, grad_x, g


def _position():
    return lax.axis_index("x"), lax.axis_index("y"), lax.axis_index("c")


def _exchange_chips(name, arrs, scatter):
    n = len(arrs)
    shapes = [a.shape if scatter else (N_CHIPS,) + a.shape for a in arrs]

    def body(*refs):
        ins, outs = refs[:n], refs[n:2 * n]
        send_sems, recv_sems, local_sems = refs[2 * n:]
        x, y, c = _position()
        me = 2 * x + y
        peers = [(1 - x, y), (x, 1 - y), (1 - x, 1 - y)]
        local, sends = [], []
        for i in range(n):
            cp = pltpu.make_async_copy(ins[i].at[me] if scatter else ins[i], outs[i].at[me], local_sems.at[i])
            cp.start()
            local.append(cp)
        for i in range(n):
            for k, (px, py) in enumerate(peers):
                cp = pltpu.make_async_remote_copy(
                    src_ref=ins[i].at[2 * px + py] if scatter else ins[i], dst_ref=outs[i].at[me],
                    send_sem=send_sems.at[i, k], recv_sem=recv_sems.at[i, k],
                    device_id=(px, py, c), device_id_type=MESH)
                cp.start()
                sends.append(cp)
        for i in range(n):
            for k, (px, py) in enumerate(peers):
                pltpu.make_async_remote_copy(
                    src_ref=ins[i].at[me] if scatter else ins[i], dst_ref=outs[i].at[2 * px + py],
                    send_sem=send_sems.at[i, k], recv_sem=recv_sems.at[i, k],
                    device_id=(px, py, c), device_id_type=MESH).wait_recv()
        for cp in sends:
            cp.wait_send()
        for cp in local:
            cp.wait()

    hbm = pl.BlockSpec(memory_space=pl.ANY)
    return pl.pallas_call(
        body, name=name,
        in_specs=[hbm] * n, out_specs=[hbm] * n,
        out_shape=[jax.ShapeDtypeStruct(s, a.dtype) for s, a in zip(shapes, arrs)],
        scratch_shapes=[pltpu.SemaphoreType.DMA((n, N_CHIPS - 1)), pltpu.SemaphoreType.DMA((n, N_CHIPS - 1)),
                        pltpu.SemaphoreType.DMA((n,))],
        compiler_params=pltpu.CompilerParams(has_side_effects=True),
    )(*arrs)


def _swap_cores(name, arrs):
    n = len(arrs)

    def body(*refs):
        ins, outs = refs[:n], refs[n:2 * n]
        send_sems, recv_sems = refs[2 * n:]
        x, y, c = _position()
        copies = [pltpu.make_async_remote_copy(src_ref=ins[i], dst_ref=outs[i], send_sem=send_sems.at[i],
                                               recv_sem=recv_sems.at[i], device_id=(x, y, 1 - c), device_id_type=MESH)
                  for i in range(n)]
        for cp in copies:
            cp.start()
        for cp in copies:
            cp.wait_recv()
        for cp in copies:
            cp.wait_send()

    hbm = pl.BlockSpec(memory_space=pl.ANY)
    return pl.pallas_call(
        body, name=name, in_specs=[hbm] * n, out_specs=[hbm] * n,
        out_shape=[jax.ShapeDtypeStruct(a.shape, a.dtype) for a in arrs],
        scratch_shapes=[pltpu.SemaphoreType.DMA((n,)), pltpu.SemaphoreType.DMA((n,))],
        compiler_params=pltpu.CompilerParams(has_side_effects=True),
    )(*arrs)


def _all_reduce(name, pack):
    def body(in_ref, out_ref, buf, send_sems, recv_sems):
        x, y, c = _position()
        me = 4 * x + 2 * y + c
        flips = [(dx, dy, dc) for dx in (0, 1) for dy in (0, 1) for dc in (0, 1) if (dx, dy, dc) != (0, 0, 0)]
        peers = [((1 - x) if dx else x, (1 - y) if dy else y, (1 - c) if dc else c) for dx, dy, dc in flips]
        buf[me] = in_ref[...]
        sends = []
        for k, peer in enumerate(peers):
            cp = pltpu.make_async_remote_copy(src_ref=in_ref, dst_ref=buf.at[me], send_sem=send_sems.at[k],
                                              recv_sem=recv_sems.at[k], device_id=peer, device_id_type=MESH)
            cp.start()
            sends.append(cp)
        for k, (px, py, pc) in enumerate(peers):
            pltpu.make_async_remote_copy(src_ref=in_ref, dst_ref=buf.at[4 * px + 2 * py + pc], send_sem=send_sems.at[k],
                                         recv_sem=recv_sems.at[k], device_id=(px, py, pc), device_id_type=MESH).wait_recv()
        total = buf[0]
        for j in range(1, N_DEV):
            total = total + buf[j]
        out_ref[...] = total
        for cp in sends:
            cp.wait_send()

    vmem = pl.BlockSpec(memory_space=pltpu.VMEM)
    return pl.pallas_call(
        body, name=name, in_specs=[vmem], out_specs=vmem,
        out_shape=jax.ShapeDtypeStruct(pack.shape, F32),
        scratch_shapes=[pltpu.VMEM((N_DEV,) + pack.shape, F32), pltpu.SemaphoreType.DMA((N_DEV - 1,)),
                        pltpu.SemaphoreType.DMA((N_DEV - 1,))],
        compiler_params=pltpu.CompilerParams(has_side_effects=True, vmem_limit_bytes=VMEM_LIMIT_V7X),
    )(pack)


PACK_UNIT = SUBLANES * LANES


def _pack(arrs):
    flat = []
    for a in arrs:
        v = a.reshape(-1).astype(F32)
        flat.append(jnp.pad(v, (0, -v.size % PACK_UNIT)))
    return jnp.concatenate(flat).reshape(-1, LANES)


def _unpack(pack, shapes):
    flat = pack.reshape(-1)
    out, off = [], 0
    for s in shapes:
        size = 1
        for d in s:
            size *= d
        out.append(flat[off:off + size].reshape(s))
        off += size + (-size % PACK_UNIT)
    return out


SMALL = ["mix_norm_w", "gate_bias", "gmlp_ln_w", "gmlp_ln_b", "gmlp_ws", "gmlp_bs", "ssm_conv_w", "ssm_conv_b",
         "ssm_dt_bias", "ssm_a_log", "ssm_d", "ssm_norm_w", "ffn_norm_w", "ffn_conv_w", "ffn_conv_b", "final_norm_w"]
SMALL_SHARDED = ("gate_bias", "ssm_conv_w", "ffn_conv_w")
BIG = ["w_in", "w_proj_a", "w_proj_b", "w_out", "ffn_w_up", "ffn_w_down"]
WEIGHTS = ["mix_norm_w", "w_in", "gate_bias", "gmlp_ln_w", "gmlp_ln_b", "gmlp_ws", "gmlp_bs", "ssm_conv_w",
           "ssm_conv_b", "ssm_dt_bias", "ssm_a_log", "ssm_d", "ssm_norm_w", "w_proj_a", "w_proj_b", "w_out",
           "ffn_norm_w", "ffn_w_up", "ffn_conv_w", "ffn_conv_b", "ffn_w_down", "final_norm_w"]
IN_SPLITS = [0, 2048, 4096, 6144, 9216, 9248]


def _columns_from_chips(stack):
    return jnp.transpose(stack, (1, 0, 2)).reshape(stack.shape[1], -1)


def _columns_to_chips(full, parts=N_CHIPS):
    rows, cols = full.shape
    return jnp.transpose(full.reshape(rows, parts, cols // parts), (1, 0, 2))


def kernel(x, mix_norm_w, w_in, gate_bias, gmlp_ln_w, gmlp_ln_b, gmlp_ws, gmlp_bs, ssm_conv_w, ssm_conv_b, ssm_dt_bias, ssm_a_log, ssm_d, ssm_norm_w, w_proj_a, w_proj_b, w_out, ffn_norm_w, ffn_w_up, ffn_conv_w, ffn_conv_b, ffn_w_down, final_norm_w, loss_target, m_mix_norm_w, m_w_in, m_gate_bias, m_gmlp_ln_w, m_gmlp_ln_b, m_gmlp_ws, m_gmlp_bs, m_ssm_conv_w, m_ssm_conv_b, m_ssm_dt_bias, m_ssm_a_log, m_ssm_d, m_ssm_norm_w, m_w_proj_a, m_w_proj_b, m_w_out, m_ffn_norm_w, m_ffn_w_up, m_ffn_conv_w, m_ffn_conv_b, m_ffn_w_down, m_final_norm_w, v_mix_norm_w, v_w_in, v_gate_bias, v_gmlp_ln_w, v_gmlp_ln_b, v_gmlp_ws, v_gmlp_bs, v_ssm_conv_w, v_ssm_conv_b, v_ssm_dt_bias, v_ssm_a_log, v_ssm_d, v_ssm_norm_w, v_w_proj_a, v_w_proj_b, v_w_out, v_ffn_norm_w, v_ffn_w_up, v_ffn_conv_w, v_ffn_conv_b, v_ffn_w_down, v_final_norm_w):
    args = dict(locals())
    weights = {n: args[n] for n in WEIGHTS}
    moments_m = {n: args["m_" + n] for n in WEIGHTS}
    moments_v = {n: args["v_" + n] for n in WEIGHTS}
    chip = 2 * lax.axis_index("x") + lax.axis_index("y")

    shards = [weights[n][0].astype(BF16) for n in BIG] + [weights[n][0] for n in SMALL_SHARDED]
    gathered = _exchange_chips("gather_weights", shards, scatter=False)
    w_in_s, w_pa_s, w_pb_s, w_out_s, w_up_s, w_down_s, gb_s, scw_s, fcw_s = gathered
    w_in_full = _columns_from_chips(w_in_s)
    full = {"w_" + nm: w_in_full[:, IN_SPLITS[k]:IN_SPLITS[k + 1]] for k, nm in enumerate(["g", "za", "z", "xbc", "dt"])}
    full["w_proj_a"] = w_pa_s.reshape(-1, D_MODEL)
    full["w_proj_b"] = w_pb_s.reshape(-1, D_MODEL)
    full["w_out"] = w_out_s.reshape(-1, D_MODEL)
    full["w_down"] = w_down_s.reshape(-1, D_MODEL)
    full["w_up_g"] = _columns_from_chips(w_up_s[:2])
    full["w_up_v"] = _columns_from_chips(w_up_s[2:])
    full["gate_bias"] = _columns_from_chips(gb_s)
    full["ssm_conv_w"] = _columns_from_chips(scw_s)
    full["ffn_conv_w"] = _columns_from_chips(fcw_s)
    for n in SMALL:
        if n not in SMALL_SHARDED:
            full[n] = weights[n] if n == "final_norm_w" else weights[n][0]
    for n in ("mix_norm_w", "ffn_norm_w", "ssm_conv_b", "ssm_dt_bias", "ssm_a_log", "ssm_d", "ssm_norm_w", "ffn_conv_b"):
        full[n] = full[n].reshape(1, -1)

    loss_part, grad_x, g = _local_step(x[0], loss_target[0], full)

    small_shapes = [(1, LANES)] + [g[n].shape for n in SMALL]
    reduced = _unpack(_all_reduce("reduce_small", _pack([loss_part] + [g[n] for n in SMALL])), small_shapes)
    loss = reduced[0][0, 0]
    small_grads = {}
    for n, r in zip(SMALL, reduced[1:]):
        if n in SMALL_SHARDED:
            width = weights[n].shape[2]
            r = lax.dynamic_slice_in_dim(r, chip * width, width, axis=1)
        small_grads[n] = r.reshape(weights[n].shape)
    packs = [_pack([d[n] for n in SMALL]) for d in (weights, moments_m, moments_v, small_grads)]
    upd = _adamw("adamw_small", packs[0], [packs[3]], packs[1], packs[2], tm=packs[0].shape[0], rs=SUBLANES)
    small_out = [_unpack(u, [weights[n].shape for n in SMALL]) for u in upd]

    dw_in = jnp.concatenate([g["w_g"], g["w_za"], g["w_z"], g["w_xbc"], g["w_dt"]], axis=1)
    stacks = [
        _columns_to_chips(dw_in),
        g["w_proj_a"].reshape(N_CHIPS, -1, D_MODEL),
        g["w_proj_b"].reshape(N_CHIPS, -1, D_MODEL),
        g["w_out"].reshape(N_CHIPS, -1, D_MODEL),
        jnp.concatenate([_columns_to_chips(g["w_up_g"], 2), _columns_to_chips(g["w_up_v"], 2)], axis=0),
        g["w_down"].reshape(N_CHIPS, -1, D_MODEL),
    ]
    received = _exchange_chips("scatter_grads", stacks, scatter=True)
    tiles = {"w_in": 128, "w_proj_a": 256, "w_proj_b": 256, "w_out": 256, "ffn_w_up": 128, "ffn_w_down": 176}
    mine = [_sum_slots("sum_" + n, r, tm=tiles[n], rs=SUBLANES) for n, r in zip(BIG, received)]
    theirs = _swap_cores("swap_grads", mine)
    big_out = {}
    for n, a, b in zip(BIG, mine, theirs):
        big_out[n] = _adamw("adamw_" + n, weights[n][0], [a, b], moments_m[n][0], moments_v[n][0],
                            tm=tiles[n], rs=SUBLANES)

    per_kind = [[], [], [], []]
    for n in WEIGHTS:
        for kind in range(4):
            if n in big_out:
                per_kind[kind].append(big_out[n][kind].reshape(weights[n].shape))
            else:
                per_kind[kind].append(small_out[kind][SMALL.index(n)])
    return (loss, grad_x[None], *per_kind[0], *per_kind[1], *per_kind[2], *per_kind[3])
```

```python
import jax
import jax.numpy as jnp
from jax import lax
from jax.experimental import pallas as pl
from jax.experimental.pallas import tpu as pltpu

F32 = jnp.float32
BF16 = jnp.bfloat16
MESH = pl.DeviceIdType.MESH

EPS = 1e-5
D_MODEL = 1024
GMLP_BLOCK = 128
GMLP_GROUPS = 8
CHUNK = 64
SSM_INNER = 2048
SSM_HEADS = 32
SSM_HEAD_DIM = 64
SSM_GROUPS = 4
SSM_HPG = 8
SSM_STATE = 128
SSM_CONV = 4
SSM_XBC = 3072
D_FF = 2816
FFN_CONV = 3
N_CHIPS = 4
N_DEV = 8

ADAM_LR = 0.001
ADAM_B1 = 0.9
ADAM_B2 = 0.999
ADAM_EPS = 1e-08
ADAM_WD = 0.01
ADAM_STEP = 10

VMEM_LIMIT_V7X = 56 * 1024 * 1024
SUBLANES = 8
LANES = 128


def _params(sem=None):
    return pltpu.CompilerParams(dimension_semantics=sem, vmem_limit_bytes=VMEM_LIMIT_V7X)


def _dot(a, b, ca=1, cb=0):
    return lax.dot_general(a.astype(BF16), b.astype(BF16), (((ca,), (cb,)), ((), ())),
                           preferred_element_type=F32)


def _mm(name, a, b, *, ta=False, tb=False, out_dtype=F32, bm, bn, bk, res=None):
    m, k = (a.shape[1], a.shape[0]) if ta else a.shape
    k2, n = (b.shape[1], b.shape[0]) if tb else b.shape
    assert k == k2 and m % bm == 0 and n % bn == 0 and k % bk == 0, (name, a.shape, b.shape)
    nk = k // bk
    a_spec = (pl.BlockSpec((bk, bm), lambda i, j, kk: (kk, i)) if ta
              else pl.BlockSpec((bm, bk), lambda i, j, kk: (i, kk)))
    b_spec = (pl.BlockSpec((bn, bk), lambda i, j, kk: (j, kk)) if tb
              else pl.BlockSpec((bk, bn), lambda i, j, kk: (kk, j)))
    o_spec = pl.BlockSpec((bm, bn), lambda i, j, kk: (i, j))
    has_res = res is not None

    def body(*refs):
        a_ref, b_ref = refs[0], refs[1]
        r_ref = refs[2] if has_res else None
        o_ref = refs[3] if has_res else refs[2]
        p = _dot(a_ref[...], b_ref[...], 0 if ta else 1, 1 if tb else 0)

        def finish(total):
            if has_res:
                total = total + r_ref[...]
            o_ref[...] = total.astype(out_dtype)

        if nk == 1:
            finish(p)
        else:
            acc_ref = refs[-1]
            kk = pl.program_id(2)

            @pl.when(kk == 0)
            def _():
                acc_ref[...] = p

            @pl.when(kk > 0)
            def _():
                acc_ref[...] += p

            @pl.when(kk == nk - 1)
            def _():
                finish(acc_ref[...])

    return pl.pallas_call(
        body, name=name,
        grid=(m // bm, n // bn, nk),
        in_specs=[a_spec, b_spec] + ([o_spec] if has_res else []),
        out_specs=o_spec,
        out_shape=jax.ShapeDtypeStruct((m, n), out_dtype),
        scratch_shapes=[pltpu.VMEM((bm, bn), F32)] if nk > 1 else [],
        compiler_params=_params(("parallel", "parallel", "arbitrary")),
    )(*([a, b] + ([res] if has_res else [])))


def _rows(name, fn, ins, params, outs, accs, *, tm, rs):
    nrow = ins[0][0].shape[-2]
    assert nrow % tm == 0 and tm % rs == 0, (name, nrow, tm, rs)
    n_in, n_p, n_out, n_acc = len(ins), len(params), len(outs), len(accs)
    in_specs = []
    for spec in ins:
        arr, width, cb = spec[:3]
        if len(spec) == 4:
            in_specs.append(pl.BlockSpec((None, tm, width), lambda i, cb=cb, lead=spec[3]: (lead, i, cb)))
        else:
            in_specs.append(pl.BlockSpec((tm, width), lambda i, cb=cb: (i, cb)))
    for p in params:
        in_specs.append(pl.BlockSpec(p.shape, lambda i, nd=p.ndim: (0,) * nd))
    out_specs = [pl.BlockSpec((tm, w), lambda i: (i, 0)) for w, _ in outs]
    out_specs += [pl.BlockSpec(s, lambda i: (0, 0)) for s in accs]
    out_shape = [jax.ShapeDtypeStruct((nrow, w), dt) for w, dt in outs]
    out_shape += [jax.ShapeDtypeStruct(s, F32) for s in accs]

    def body(*refs):
        in_refs = refs[:n_in]
        p_refs = refs[n_in:n_in + n_p]
        o_refs = refs[n_in + n_p:n_in + n_p + n_out]
        a_refs = refs[n_in + n_p + n_out:]
        pv = [p[...] for p in p_refs]

        if n_acc:
            @pl.when(pl.program_id(0) == 0)
            def _():
                for a_ref in a_refs:
                    a_ref[...] = jnp.zeros_like(a_ref)

        def step(r, carry):
            sl = pl.ds(pl.multiple_of(r * rs, rs), rs)
            vals = [ref[sl, :] for ref in in_refs]
            row_out, sums = fn(*vals, *pv)
            for o_ref, v in zip(o_refs, row_out):
                o_ref[sl, :] = v.astype(o_ref.dtype)
            return tuple(c + s for c, s in zip(carry, sums))

        init = tuple(jnp.zeros(s, F32) for s in accs)
        total = lax.fori_loop(0, tm // rs, step, init)
        for a_ref, t in zip(a_refs, total):
            a_ref[...] += t

    res = pl.pallas_call(
        body, name=name, grid=(nrow // tm,),
        in_specs=in_specs, out_specs=out_specs, out_shape=out_shape,
        compiler_params=_params(("arbitrary",)),
    )(*([s[0] for s in ins] + list(params)))
    return res


def _rms(x, w):
    return x * lax.rsqrt(jnp.mean(x * x, axis=-1, keepdims=True) + EPS) * w


def _colsum(v):
    return jnp.sum(v, axis=0, keepdims=True)


def _rms_fwd(name, x, w):
    def fn(xv, wv):
        return (_rms(xv, wv),), ()
    return _rows(name, fn, [(x, D_MODEL, 0)], [w], [(D_MODEL, BF16)], [], tm=512, rs=16)[0]


def _rms_bwd(name, x, w, dy, dres):
    def fn(xv, dyv, drv, wv):
        _, vjp = jax.vjp(_rms, xv, wv)
        dx, dw = vjp(dyv)
        return (drv + dx,), (dw,)
    return _rows(name, fn, [(x, D_MODEL, 0), (dy, D_MODEL, 0), (dres, D_MODEL, 0)], [w],
                 [(D_MODEL, F32)], [(1, D_MODEL)], tm=512, rs=16)


def _final_loss(name, h, target, w):
    def fn(hv, tv, wv):
        y, vjp = jax.vjp(_rms, hv, wv)
        err = y - tv
        part = 0.5 * jnp.sum(jnp.mean(err * err, axis=-1, keepdims=True), axis=0, keepdims=True)
        dh, dw = vjp(err / D_MODEL)
        return (dh,), (jnp.broadcast_to(part, (1, LANES)), dw)
    return _rows(name, fn, [(h, D_MODEL, 0), (target, D_MODEL, 0)], [w],
                 [(D_MODEL, F32)], [(1, LANES), (1, D_MODEL)], tm=512, rs=16)


def _merge(ga, gb, ya, yb, b0, b1):
    return jax.nn.sigmoid(ga + b0) * ya + jax.nn.sigmoid(gb + b1) * yb


def _merge_fwd(name, g, ya, yb, b0, b1):
    def fn(ga, gb, yav, ybv, b0v, b1v):
        return (_merge(ga, gb, yav, ybv, b0v, b1v),), ()
    return _rows(name, fn, [(g, D_MODEL, 0), (g, D_MODEL, 1), (ya, D_MODEL, 0), (yb, D_MODEL, 0)],
                 [b0, b1], [(D_MODEL, BF16)], [], tm=512, rs=16)[0]


def _merge_bwd(name, g, ya, yb, dm, b0, b1):
    def fn(ga, gb, yav, ybv, dmv, b0v, b1v):
        _, vjp = jax.vjp(_merge, ga, gb, yav, ybv, b0v, b1v)
        dga, dgb, dya, dyb, db0, db1 = vjp(dmv)
        return (jnp.concatenate([dga, dgb], axis=1), dya, dyb), (db0, db1)
    return _rows(name, fn,
                 [(g, D_MODEL, 0), (g, D_MODEL, 1), (ya, D_MODEL, 0), (yb, D_MODEL, 0), (dm, D_MODEL, 0)],
                 [b0, b1], [(2 * D_MODEL, BF16), (D_MODEL, BF16), (D_MODEL, BF16)],
                 [(1, D_MODEL), (1, D_MODEL)], tm=512, rs=16)


GROUP_W = SSM_INNER // SSM_GROUPS


def _gate_norm_group(y, z, nw):
    v = y * jax.nn.silu(z)
    return v * lax.rsqrt(jnp.mean(v * v, axis=-1, keepdims=True) + EPS) * nw


def _gate_norm_fwd(name, y, z, nw):
    def fn(yv, zv, nwv):
        parts = [_gate_norm_group(yv[:, k * GROUP_W:(k + 1) * GROUP_W], zv[:, k * GROUP_W:(k + 1) * GROUP_W],
                                  nwv[:, k * GROUP_W:(k + 1) * GROUP_W]) for k in range(SSM_GROUPS)]
        return (jnp.concatenate(parts, axis=1),), ()
    return _rows(name, fn, [(y, SSM_INNER, 0), (z, SSM_INNER, 0)], [nw], [(SSM_INNER, BF16)], [],
                 tm=512, rs=16)[0]


def _gate_norm_bwd(name, y, z, dout, nw):
    def fn(yv, zv, dv, nwv):
        dys, dzs, dns = [], [], []
        for k in range(SSM_GROUPS):
            sl = slice(k * GROUP_W, (k + 1) * GROUP_W)
            _, vjp = jax.vjp(_gate_norm_group, yv[:, sl], zv[:, sl], nwv[:, sl])
            dy, dz, dn = vjp(dv[:, sl])
            dys.append(dy), dzs.append(dz), dns.append(dn)
        return (jnp.concatenate(dys, axis=1), jnp.concatenate(dzs, axis=1)), (jnp.concatenate(dns, axis=1),)
    return _rows(name, fn, [(y, SSM_INNER, 0), (z, SSM_INNER, 0), (dout, SSM_INNER, 0)], [nw],
                 [(SSM_INNER, F32), (SSM_INNER, BF16)], [(1, SSM_INNER)], tm=512, rs=16)


def _softplus(v):
    return jnp.maximum(v, 0.0) + jnp.log1p(jnp.exp(-jnp.abs(v)))


def _chunk_cumsum(v, reverse=False):
    row = lax.broadcasted_iota(jnp.int32, v.shape, 0)
    step = 1
    while step < CHUNK:
        if reverse:
            shifted = pltpu.roll(v, CHUNK - step, axis=0)
            v = v + jnp.where(row < CHUNK - step, shifted, 0.0)
        else:
            shifted = pltpu.roll(v, step, axis=0)
            v = v + jnp.where(row >= step, shifted, 0.0)
        step *= 2
    return v


def _dt_prep(name, dt_raw, dt_bias, a_log):
    def fn(rv, bv, alv):
        dt = _softplus(rv + bv)
        return (dt, _chunk_cumsum(dt * (-jnp.exp(alv)))), ()
    return _rows(name, fn, [(dt_raw, SSM_HEADS, 0)], [dt_bias, a_log],
                 [(SSM_HEADS, F32), (SSM_HEADS, F32)], [], tm=512, rs=CHUNK)


def _dt_bwd(name, dt_raw, ddt, da1, da2, dt_bias, a_log):
    def fn(rv, ddv, d1, d2, bv, alv):
        pre = rv + bv
        dt = _softplus(pre)
        a_neg = -jnp.exp(alv)
        back = _chunk_cumsum(d1 + d2, reverse=True)
        d_dt = ddv + back * a_neg
        d_raw = d_dt * jax.nn.sigmoid(pre)
        return (d_raw,), (_colsum(d_raw), _colsum(back * dt) * a_neg)
    return _rows(name, fn, [(dt_raw, SSM_HEADS, 0), (ddt, SSM_HEADS, 0), (da1, SSM_HEADS, 0), (da2, SSM_HEADS, 0)],
                 [dt_bias, a_log], [(SSM_HEADS, BF16)], [(1, SSM_HEADS), (1, SSM_HEADS)], tm=512, rs=CHUNK)


def _adamw(name, w, g_parts, m, v, *, tm, rs):
    width = w.shape[1]
    n_g = len(g_parts)

    def fn(wv, mv, vv, *gs):
        g = gs[0]
        for extra in gs[1:]:
            g = g + extra
        m_new = ADAM_B1 * mv + (1.0 - ADAM_B1) * g
        v_new = ADAM_B2 * vv + (1.0 - ADAM_B2) * jnp.square(g)
        m_hat = m_new / (1.0 - ADAM_B1 ** ADAM_STEP)
        v_hat = v_new / (1.0 - ADAM_B2 ** ADAM_STEP)
        delta = -ADAM_LR * (m_hat / (jnp.sqrt(v_hat) + ADAM_EPS) + ADAM_WD * wv)
        return (g, delta, m_new, v_new), ()
    assert n_g >= 1
    return _rows(name, fn, [(w, width, 0), (m, width, 0), (v, width, 0)] + [(g, width, 0) for g in g_parts],
                 [], [(width, F32)] * 4, [], tm=tm, rs=rs)


def _sum_slots(name, stack, *, tm, rs):
    width = stack.shape[2]

    def fn(*slots):
        s0, s1, s2, s3 = (s.astype(F32) for s in slots)
        return (((s0 + s1) + s2) + s3,), ()
    return _rows(name, fn, [(stack, width, 0, k) for k in range(N_CHIPS)], [], [(width, F32)], [],
                 tm=tm, rs=rs)[0]


def _layernorm(v, w, b):
    mu = jnp.mean(v, axis=-1, keepdims=True)
    var = jnp.mean(jnp.square(v - mu), axis=-1, keepdims=True)
    return (v - mu) * lax.rsqrt(var + EPS) * w + b


def _gmlp_mask():
    t = lax.broadcasted_iota(jnp.int32, (GMLP_BLOCK, GMLP_BLOCK), 0) // CHUNK
    s = lax.broadcasted_iota(jnp.int32, (GMLP_BLOCK, GMLP_BLOCK), 1) // CHUNK
    return s <= t


GMLP_TM = 512


def _gmlp_fwd(name, za, ln_w, ln_b, ws, bs_col):
    nrow = za.shape[0]
    tm = GMLP_TM
    width = GMLP_GROUPS * GMLP_BLOCK

    def body(za_ref, lnw_ref, lnb_ref, ws_ref, bs_ref, o_ref, wm_ref):
        mask = _gmlp_mask()
        for g in range(GMLP_GROUPS):
            wm_ref[g] = jnp.where(mask, ws_ref[g], 0.0).astype(BF16)

        def block(n, carry):
            rows = pl.ds(pl.multiple_of(n * GMLP_BLOCK, GMLP_BLOCK), GMLP_BLOCK)
            for g in range(GMLP_GROUPS):
                cols = slice(g * GMLP_BLOCK, (g + 1) * GMLP_BLOCK)
                vcols = slice(width + g * GMLP_BLOCK, width + (g + 1) * GMLP_BLOCK)
                u = jax.nn.gelu(za_ref[rows, cols])
                v = jax.nn.gelu(za_ref[rows, vcols])
                vn = _layernorm(v, lnw_ref[g:g + 1, :], lnb_ref[g:g + 1, :])
                sv = _dot(wm_ref[g], vn) + bs_ref[g]
                o_ref[rows, cols] = (u * sv).astype(o_ref.dtype)
            return carry

        lax.fori_loop(0, tm // GMLP_BLOCK, block, 0)

    small = lambda a: pl.BlockSpec(a.shape, lambda i, nd=a.ndim: (0,) * nd)
    return pl.pallas_call(
        body, name=name, grid=(nrow // tm,),
        in_specs=[pl.BlockSpec((tm, 2 * width), lambda i: (i, 0)), small(ln_w), small(ln_b), small(ws), small(bs_col)],
        out_specs=pl.BlockSpec((tm, width), lambda i: (i, 0)),
        out_shape=jax.ShapeDtypeStruct((nrow, width), BF16),
        scratch_shapes=[pltpu.VMEM((GMLP_GROUPS, GMLP_BLOCK, GMLP_BLOCK), BF16)],
        compiler_params=_params(("arbitrary",)),
    )(za, ln_w, ln_b, ws, bs_col)


def _gmlp_bwd(name, za, dout, ln_w, ln_b, ws, bs_col):
    nrow = za.shape[0]
    tm = GMLP_TM
    width = GMLP_GROUPS * GMLP_BLOCK

    def body(za_ref, do_ref, lnw_ref, lnb_ref, ws_ref, bs_ref, dza_ref, dlnw_ref, dlnb_ref, dws_ref, dbs_ref, wm_ref):
        mask = _gmlp_mask()
        for g in range(GMLP_GROUPS):
            wm_ref[g] = jnp.where(mask, ws_ref[g], 0.0).astype(BF16)

        @pl.when(pl.program_id(0) == 0)
        def _():
            dlnw_ref[...] = jnp.zeros_like(dlnw_ref)
            dlnb_ref[...] = jnp.zeros_like(dlnb_ref)
            dws_ref[...] = jnp.zeros_like(dws_ref)
            dbs_ref[...] = jnp.zeros_like(dbs_ref)

        def block(n, carry):
            rows = pl.ds(pl.multiple_of(n * GMLP_BLOCK, GMLP_BLOCK), GMLP_BLOCK)
            for g in range(GMLP_GROUPS):
                cols = slice(g * GMLP_BLOCK, (g + 1) * GMLP_BLOCK)
                vcols = slice(width + g * GMLP_BLOCK, width + (g + 1) * GMLP_BLOCK)
                u, gelu_u_vjp = jax.vjp(jax.nn.gelu, za_ref[rows, cols])
                v, gelu_v_vjp = jax.vjp(jax.nn.gelu, za_ref[rows, vcols])
                vn, ln_vjp = jax.vjp(_layernorm, v, lnw_ref[g:g + 1, :], lnb_ref[g:g + 1, :])
                sv = _dot(wm_ref[g], vn) + bs_ref[g]
                d_o = do_ref[rows, cols]
                dsv = d_o * u
                d_wm = _dot(dsv, vn, 1, 1)
                dvn = _dot(wm_ref[g], dsv, 0, 0)
                dv, dlnw, dlnb = ln_vjp(dvn)
                dza_ref[rows, cols] = gelu_u_vjp(d_o * sv)[0].astype(dza_ref.dtype)
                dza_ref[rows, vcols] = gelu_v_vjp(dv)[0].astype(dza_ref.dtype)
                dlnw_ref[g:g + 1, :] += dlnw
                dlnb_ref[g:g + 1, :] += dlnb
                dws_ref[g] += jnp.where(mask, d_wm, 0.0)
                dbs_ref[g] += jnp.sum(dsv, axis=1, keepdims=True)
            return carry

        lax.fori_loop(0, tm // GMLP_BLOCK, block, 0)

    small = lambda a: pl.BlockSpec(a.shape, lambda i, nd=a.ndim: (0,) * nd)
    return pl.pallas_call(
        body, name=name, grid=(nrow // tm,),
        in_specs=[pl.BlockSpec((tm, 2 * width), lambda i: (i, 0)), pl.BlockSpec((tm, width), lambda i: (i, 0)),
                  small(ln_w), small(ln_b), small(ws), small(bs_col)],
        out_specs=[pl.BlockSpec((tm, 2 * width), lambda i: (i, 0)), small(ln_w), small(ln_b), small(ws), small(bs_col)],
        out_shape=[jax.ShapeDtypeStruct((nrow, 2 * width), BF16), jax.ShapeDtypeStruct(ln_w.shape, F32),
                   jax.ShapeDtypeStruct(ln_b.shape, F32), jax.ShapeDtypeStruct(ws.shape, F32),
                   jax.ShapeDtypeStruct(bs_col.shape, F32)],
        scratch_shapes=[pltpu.VMEM((GMLP_GROUPS, GMLP_BLOCK, GMLP_BLOCK), BF16)],
        compiler_params=_params(("arbitrary",)),
    )(za, dout, ln_w, ln_b, ws, bs_col)


CONV_TM = 256
CONV_RS = 32


def _tap_rows(w_ref):
    return [w_ref[k:k + 1, :] for k in range(w_ref.shape[0])]


def _conv_rows(win, w, rs):
    taps = len(w)
    out = w[taps - 1] * win[SUBLANES:, :]
    for k in range(taps - 1):
        back = taps - 1 - k
        out = out + w[k] * pltpu.roll(win, back, axis=0)[SUBLANES:, :]
    return out


def _conv_t_rows(win, w, rs):
    taps = len(w)
    out = w[taps - 1] * win[:rs, :]
    for k in range(taps - 1):
        ahead = taps - 1 - k
        out = out + w[k] * pltpu.roll(win, rs + SUBLANES - ahead, axis=0)[:rs, :]
    return out


def _conv_dw_rows(d, xwin, taps):
    rows = []
    for k in range(taps):
        back = taps - 1 - k
        xs = xwin[SUBLANES:, :] if back == 0 else pltpu.roll(xwin, back, axis=0)[SUBLANES:, :]
        rows.append(jnp.sum(d * xs, axis=0, keepdims=True))
    return rows


def _halo_specs(nrow, tm, tc):
    per = tm // SUBLANES
    last = nrow // SUBLANES - 1
    main = pl.BlockSpec((tm, tc), lambda j, i: (i, j))
    before = pl.BlockSpec((SUBLANES, tc), lambda j, i: (jnp.maximum(i * per - 1, 0), j))
    after = pl.BlockSpec((SUBLANES, tc), lambda j, i: (jnp.minimum((i + 1) * per, last), j))
    return main, before, after


def _col_spec(rows, tc):
    return pl.BlockSpec((rows, tc), lambda j, i: (0, j))


def _conv_fwd(name, x, w, b, *, tc):
    nrow, ncol = x.shape
    taps = w.shape[0]
    tm, rs = CONV_TM, CONV_RS
    main, before, _ = _halo_specs(nrow, tm, tc)

    def body(x_ref, xb_ref, w_ref, b_ref, o_ref):
        first = pl.program_id(1) == 0
        wv, bv = _tap_rows(w_ref), b_ref[...]

        def step(r, prev):
            sl = pl.ds(pl.multiple_of(r * rs, rs), rs)
            cur = x_ref[sl, :]
            o_ref[sl, :] = _conv_rows(jnp.concatenate([prev, cur], axis=0), wv, rs) + bv
            return cur[rs - SUBLANES:, :]

        lax.fori_loop(0, tm // rs, step, jnp.where(first, 0.0, xb_ref[...]))

    return pl.pallas_call(
        body, name=name, grid=(ncol // tc, nrow // tm),
        in_specs=[main, before, _col_spec(taps, tc), _col_spec(1, tc)],
        out_specs=main, out_shape=jax.ShapeDtypeStruct((nrow, ncol), F32),
        compiler_params=_params(("parallel", "arbitrary")),
    )(x, x, w, b)


def _conv_bwd(name, dpre, x, w, *, tc):
    nrow, ncol = x.shape
    taps = w.shape[0]
    tm, rs = CONV_TM, CONV_RS
    nsub = tm // rs
    main, before, after = _halo_specs(nrow, tm, tc)

    def body(d_ref, da_ref, x_ref, xb_ref, w_ref, dx_ref, dw_ref, db_ref):
        i = pl.program_id(1)
        first, last = i == 0, i == pl.num_programs(1) - 1
        wv = _tap_rows(w_ref)
        x_before = jnp.where(first, 0.0, xb_ref[...])

        @pl.when(first)
        def _():
            dw_ref[...] = jnp.zeros_like(dw_ref)
            db_ref[...] = jnp.zeros_like(db_ref)

        def step(q, carry):
            nxt, dw, db = carry
            r = nsub - 1 - q
            sl = pl.ds(pl.multiple_of(r * rs, rs), rs)
            cur = d_ref[sl, :]
            dx_ref[sl, :] = _conv_t_rows(jnp.concatenate([cur, nxt], axis=0), wv, rs).astype(dx_ref.dtype)
            inner = x_ref[pl.ds(pl.multiple_of(jnp.maximum(r * rs - SUBLANES, 0), SUBLANES), SUBLANES), :]
            xwin = jnp.concatenate([jnp.where(r == 0, x_before, inner), x_ref[sl, :]], axis=0)
            dw = tuple(a + s for a, s in zip(dw, _conv_dw_rows(cur, xwin, taps)))
            return cur[:SUBLANES, :], dw, db + _colsum(cur)

        zero_row = jnp.zeros((1, tc), F32)
        init = (jnp.where(last, 0.0, da_ref[...]), (zero_row,) * taps, zero_row)
        _, dw, db = lax.fori_loop(0, nsub, step, init)
        for k in range(taps):
            dw_ref[k:k + 1, :] += dw[k]
        db_ref[...] += db

    return pl.pallas_call(
        body, name=name, grid=(ncol // tc, nrow // tm),
        in_specs=[main, after, main, before, _col_spec(taps, tc)],
        out_specs=[main, _col_spec(taps, tc), _col_spec(1, tc)],
        out_shape=[jax.ShapeDtypeStruct((nrow, ncol), BF16), jax.ShapeDtypeStruct((taps, ncol), F32),
                   jax.ShapeDtypeStruct((1, ncol), F32)],
        compiler_params=_params(("parallel", "arbitrary")),
    )(dpre, dpre, x, x, w)


def _glu(gate, val):
    return jax.nn.silu(gate) * val


def _ffn_act_fwd(name, pg, pv, wg, wv, bg, bv, *, tc):
    nrow, ncol = pg.shape
    taps = wg.shape[0]
    tm, rs = CONV_TM, CONV_RS
    main, before, _ = _halo_specs(nrow, tm, tc)

    def body(pg_ref, pgb_ref, pv_ref, pvb_ref, wg_ref, wv_ref, bg_ref, bv_ref, g_ref, v_ref, a_ref):
        first = pl.program_id(1) == 0
        wgv, wvv, bgv, bvv = _tap_rows(wg_ref), _tap_rows(wv_ref), bg_ref[...], bv_ref[...]

        def step(r, carry):
            prev_g, prev_v = carry
            sl = pl.ds(pl.multiple_of(r * rs, rs), rs)
            cur_g, cur_v = pg_ref[sl, :], pv_ref[sl, :]
            gate = _conv_rows(jnp.concatenate([prev_g, cur_g], axis=0), wgv, rs) + bgv
            val = _conv_rows(jnp.concatenate([prev_v, cur_v], axis=0), wvv, rs) + bvv
            g_ref[sl, :] = gate
            v_ref[sl, :] = val
            a_ref[sl, :] = _glu(gate, val).astype(a_ref.dtype)
            return cur_g[rs - SUBLANES:, :], cur_v[rs - SUBLANES:, :]

        lax.fori_loop(0, tm // rs, step, (jnp.where(first, 0.0, pgb_ref[...]), jnp.where(first, 0.0, pvb_ref[...])))

    return pl.pallas_call(
        body, name=name, grid=(ncol // tc, nrow // tm),
        in_specs=[main, before, main, before, _col_spec(taps, tc), _col_spec(taps, tc), _col_spec(1, tc), _col_spec(1, tc)],
        out_specs=[main, main, main],
        out_shape=[jax.ShapeDtypeStruct((nrow, ncol), F32), jax.ShapeDtypeStruct((nrow, ncol), F32),
                   jax.ShapeDtypeStruct((nrow, ncol), BF16)],
        compiler_params=_params(("parallel", "arbitrary")),
    )(pg, pg, pv, pv, wg, wv, bg, bv)


def _ffn_act_bwd(name, dact, gate, val):
    def fn(dv, gv, vv):
        _, vjp = jax.vjp(_glu, gv, vv)
        dg, dval = vjp(dv)
        return (dg, dval), ()
    width = dact.shape[1]
    return _rows(name, fn, [(dact, width, 0), (gate, width, 0), (val, width, 0)], [],
                 [(width, F32), (width, F32)], [], tm=256, rs=8)


SSD_TM = 256
SSD_CHUNKS = SSD_TM // CHUNK
X_OFF, B_OFF, C_OFF = 0, SSM_INNER, SSM_INNER + SSM_GROUPS * SSM_STATE
HP = SSM_HPG * SSM_HEAD_DIM


def _causal_tiled():
    row = lax.broadcasted_iota(jnp.int32, (CHUNK, HP), 0)
    src = lax.broadcasted_iota(jnp.int32, (CHUNK, HP), 1) & (CHUNK - 1)
    return src <= row


def _split3(v):
    hi = v.astype(BF16)
    rest = v - hi.astype(F32)
    mid = rest.astype(BF16)
    lo = (rest - mid.astype(F32)).astype(BF16)
    return hi, mid, lo


def _dot_exact(a, ind):
    parts = [lax.dot_general(p, ind, (((1,), (0,)), ((), ())), preferred_element_type=F32) for p in _split3(a)]
    return (parts[0] + parts[1]) + parts[2]


def _head_indicator():
    head = lax.broadcasted_iota(jnp.int32, (SSM_HEADS, SSM_INNER), 0)
    chan = lax.broadcasted_iota(jnp.int32, (SSM_HEADS, SSM_INNER), 1)
    return (chan // SSM_HEAD_DIM == head).astype(BF16)


def _chunk_decays(ci, dt_ref, ac_ref, ind, ax_ref, dtx_ref, eax_ref, eex_ref, tail_ref):
    rows = pl.ds(pl.multiple_of(ci * CHUNK, CHUNK), CHUNK)
    ax_ref[...] = _dot_exact(ac_ref[rows, :], ind)
    dtx_ref[...] = _dot_exact(dt_ref[rows, :], ind)
    eax_ref[...] = jnp.exp(ax_ref[...])
    eex_ref[...] = jnp.exp(ax_ref[CHUNK - 1:CHUNK, :] - ax_ref[...])
    tail = pl.ds(pl.multiple_of(ci * CHUNK + CHUNK - SUBLANES, SUBLANES), SUBLANES)
    tail_ref[...] = jnp.exp(ac_ref[tail, :])


def _group_decay(ci, g, ax_ref, af_ref, xbc_ref, causal):
    gcols = slice(g * HP, (g + 1) * HP)
    bm = xbc_ref[:, B_OFF + g * SSM_STATE:B_OFF + (g + 1) * SSM_STATE]
    cm = xbc_ref[:, C_OFF + g * SSM_STATE:C_OFF + (g + 1) * SSM_STATE]
    cb_tiled = _dot(cm, jnp.concatenate([bm] * SSM_HPG, axis=0), 1, 1)
    seg = ax_ref[:, gcols] - af_ref[ci, :, gcols]
    decay = jnp.where(causal, jnp.exp(jnp.where(causal, seg, 0.0)), 0.0)
    return bm, cm, cb_tiled * decay, decay


def _ssd_fwd(name, pre, dt, a_cum, a_flat, d_x, ind):
    nrow = pre.shape[0]
    tm = SSD_TM

    def body(pre_ref, dt_ref, ac_ref, af_ref, dx_ref, ind_ref, y_ref, st_ref,
             h_ref, xbc_ref, ax_ref, dtx_ref, eax_ref, eex_ref, m_ref, xd_ref, yd_ref, tail_ref):
        @pl.when(pl.program_id(0) == 0)
        def _():
            h_ref[...] = jnp.zeros_like(h_ref)

        causal = _causal_tiled()
        ind = ind_ref[...]

        def chunk(ci, carry):
            rows = pl.ds(pl.multiple_of(ci * CHUNK, CHUNK), CHUNK)
            xbc_ref[...] = jax.nn.silu(pre_ref[rows, :])
            _chunk_decays(ci, dt_ref, ac_ref, ind, ax_ref, dtx_ref, eax_ref, eex_ref, tail_ref)
            st_ref[ci] = h_ref[...]
            for g in range(SSM_GROUPS):
                gcols = slice(g * HP, (g + 1) * HP)
                bm, cm, m_all, _ = _group_decay(ci, g, ax_ref, af_ref, xbc_ref, causal)
                m_ref[...] = m_all
                x_g = xbc_ref[:, gcols]
                xd = x_g * dtx_ref[:, gcols]
                xd_ref[...] = xd
                h_g = h_ref[gcols, :]
                for hh in range(SSM_HPG):
                    lc = slice(hh * SSM_HEAD_DIM, (hh + 1) * SSM_HEAD_DIM)
                    yd_ref[:, lc] = _dot(m_ref[:, lc], xd_ref[:, lc])
                y_ref[rows, gcols] = (yd_ref[...] + _dot(cm, h_g, 1, 1) * eax_ref[:, gcols]
                                      + dx_ref[:, gcols] * x_g)
                new = _dot(xd * eex_ref[:, gcols], bm, 0, 0)
                for hh in range(SSM_HPG):
                    h = g * SSM_HPG + hh
                    hrows = slice(h * SSM_HEAD_DIM, (h + 1) * SSM_HEAD_DIM)
                    lrows = slice(hh * SSM_HEAD_DIM, (hh + 1) * SSM_HEAD_DIM)
                    h_ref[hrows, :] = tail_ref[SUBLANES - 1:SUBLANES, h:h + 1] * h_ref[hrows, :] + new[lrows, :]
            return carry

        lax.fori_loop(0, SSD_CHUNKS, chunk, 0)

    nchunk = nrow // CHUNK
    whole = lambda a: pl.BlockSpec(a.shape, lambda i, nd=a.ndim: (0,) * nd)
    wide = lambda: pltpu.VMEM((CHUNK, SSM_INNER), F32)
    group = lambda: pltpu.VMEM((CHUNK, HP), F32)
    return pl.pallas_call(
        body, name=name, grid=(nrow // tm,),
        in_specs=[pl.BlockSpec((tm, SSM_XBC), lambda i: (i, 0)), pl.BlockSpec((tm, SSM_HEADS), lambda i: (i, 0)),
                  pl.BlockSpec((tm, SSM_HEADS), lambda i: (i, 0)),
                  pl.BlockSpec((SSD_CHUNKS, 1, SSM_INNER), lambda i: (i, 0, 0)), whole(d_x), whole(ind)],
        out_specs=[pl.BlockSpec((tm, SSM_INNER), lambda i: (i, 0)),
                   pl.BlockSpec((SSD_CHUNKS, SSM_INNER, SSM_STATE), lambda i: (i, 0, 0))],
        out_shape=[jax.ShapeDtypeStruct((nrow, SSM_INNER), F32),
                   jax.ShapeDtypeStruct((nchunk, SSM_INNER, SSM_STATE), F32)],
        scratch_shapes=[pltpu.VMEM((SSM_INNER, SSM_STATE), F32), pltpu.VMEM((CHUNK, SSM_XBC), F32),
                        wide(), wide(), wide(), wide(), group(), group(), group(),
                        pltpu.VMEM((SUBLANES, SSM_HEADS), F32)],
        compiler_params=_params(("arbitrary",)),
    )(pre, dt, a_cum, a_flat, d_x, ind)


def _ssd_bwd(name, pre, dt, a_cum, a_flat, d_x, ind, ind_t, states, dy):
    nrow = pre.shape[0]
    tm = SSD_TM
    ntile = nrow // tm

    def body(pre_ref, dt_ref, ac_ref, af_ref, dx_ref, ind_ref, indt_ref, st_ref, dy_ref,
             dpre_ref, ddt_ref, da_ref, daf_ref, dd_ref,
             dh_ref, xbc_ref, dxbc_ref, ax_ref, dtx_ref, eax_ref, eex_ref, ra_ref, ts_ref, r2_ref,
             m_ref, l_ref, xd_ref, dm_ref, dxd_ref, fold_ref, hd_ref, tail_ref):
        @pl.when(pl.program_id(0) == 0)
        def _():
            dh_ref[...] = jnp.zeros_like(dh_ref)
            dd_ref[...] = jnp.zeros_like(dd_ref)

        causal = _causal_tiled()
        ind, ind_t = ind_ref[...], indt_ref[...]
        is_last_row = lax.broadcasted_iota(jnp.int32, (CHUNK, 1), 0) == CHUNK - 1
        ones = jnp.ones((CHUNK, SSM_STATE), BF16)

        def chunk(k, ddx):
            ci = SSD_CHUNKS - 1 - k
            rows = pl.ds(pl.multiple_of(ci * CHUNK, CHUNK), CHUNK)
            pre_v = pre_ref[rows, :]
            xbc_ref[...] = jax.nn.silu(pre_v)
            _chunk_decays(ci, dt_ref, ac_ref, ind, ax_ref, dtx_ref, eax_ref, eex_ref, tail_ref)
            ddx_parts = []
            for g in range(SSM_GROUPS):
                gcols = slice(g * HP, (g + 1) * HP)
                bcols = slice(B_OFF + g * SSM_STATE, B_OFF + (g + 1) * SSM_STATE)
                ccols = slice(C_OFF + g * SSM_STATE, C_OFF + (g + 1) * SSM_STATE)
                bm, cm, m_all, decay = _group_decay(ci, g, ax_ref, af_ref, xbc_ref, causal)
                m_ref[...] = m_all
                l_ref[...] = decay
                x_g = xbc_ref[:, gcols]
                xd = x_g * dtx_ref[:, gcols]
                xd_ref[...] = xd
                h_g = st_ref[ci, gcols, :]
                dh_g = dh_ref[gcols, :]
                dy_g = dy_ref[rows, gcols]
                for hh in range(SSM_HPG):
                    h = g * SSM_HPG + hh
                    hcols = slice(h * SSM_HEAD_DIM, (h + 1) * SSM_HEAD_DIM)
                    lc = slice(hh * SSM_HEAD_DIM, (hh + 1) * SSM_HEAD_DIM)
                    dy_h = dy_ref[rows, hcols]
                    dm_ref[:, lc] = _dot(dy_h, xd_ref[:, lc], 1, 1)
                    dxd_ref[:, lc] = _dot(m_ref[:, lc], dy_h, 0, 0)
                ebdh = eex_ref[:, gcols] * _dot(bm, dh_g, 1, 1)
                dxd = dxd_ref[...] + ebdh
                dm = dm_ref[...]
                t = dm * l_ref[...]
                t128 = (t[:, 0:LANES] + t[:, LANES:2 * LANES]) + (t[:, 2 * LANES:3 * LANES] + t[:, 3 * LANES:])
                fold_ref[...] = t128 + pltpu.roll(t128, CHUNK, axis=1)
                dw_sum = fold_ref[:, 0:CHUNK]
                q = dm * m_ref[...]
                dyea = dy_g * eax_ref[:, gcols]
                ra_ref[:, gcols] = q + dyea * _dot(cm, h_g, 1, 1)
                ts_ref[:, gcols] = xd * ebdh
                r2_ref[:, gcols] = dxd * x_g
                daf_ref[ci, :, gcols] = -jnp.sum(q, axis=0, keepdims=True)
                ddx_parts.append(jnp.sum(dy_g * x_g, axis=0, keepdims=True))
                dxbc_ref[:, gcols] = dxd * dtx_ref[:, gcols] + dx_ref[:, gcols] * dy_g
                dxbc_ref[:, ccols] = _dot(dw_sum, bm) + _dot(dyea, h_g)
                dxbc_ref[:, bcols] = _dot(dw_sum, cm, 0, 0) + _dot(xd * eex_ref[:, gcols], dh_g)
                dh_new = _dot(dyea, cm, 0, 0)
                for hh in range(SSM_HPG):
                    h = g * SSM_HPG + hh
                    hrows = slice(h * SSM_HEAD_DIM, (h + 1) * SSM_HEAD_DIM)
                    lrows = slice(hh * SSM_HEAD_DIM, (hh + 1) * SSM_HEAD_DIM)
                    hd_ref[h:h + 1, :] = jnp.sum(st_ref[ci, hrows, :] * dh_ref[hrows, :], axis=0, keepdims=True)
                    dh_ref[hrows, :] = tail_ref[SUBLANES - 1:SUBLANES, h:h + 1] * dh_ref[hrows, :] + dh_new[lrows, :]
            ra = _dot_exact(ra_ref[...], ind_t)
            ts = _dot_exact(ts_ref[...], ind_t)
            hdh = sum(lax.dot_general(ones, p, (((1,), (1,)), ((), ())), preferred_element_type=F32)
                      for p in _split3(hd_ref[...]))
            da_last = jnp.sum(ts, axis=0, keepdims=True) + tail_ref[SUBLANES - 1:SUBLANES, :] * hdh
            da_ref[rows, :] = ra - ts + jnp.where(is_last_row, da_last, 0.0)
            ddt_ref[rows, :] = _dot_exact(r2_ref[...], ind_t)
            sig = jax.nn.sigmoid(pre_v)
            dpre_ref[rows, :] = dxbc_ref[...] * (sig * (1.0 + pre_v * (1.0 - sig)))
            return ddx + jnp.concatenate(ddx_parts, axis=1)

        ddx = lax.fori_loop(0, SSD_CHUNKS, chunk, jnp.zeros((1, SSM_INNER), F32))
        dd_ref[...] += _dot_exact(jnp.broadcast_to(ddx, (SUBLANES, SSM_INNER)), ind_t)

    rev = lambda i: ntile - 1 - i
    whole = lambda a: pl.BlockSpec(a.shape, lambda i, nd=a.ndim: (0,) * nd)
    wide = lambda: pltpu.VMEM((CHUNK, SSM_INNER), F32)
    group = lambda: pltpu.VMEM((CHUNK, HP), F32)
    return pl.pallas_call(
        body, name=name, grid=(ntile,),
        in_specs=[pl.BlockSpec((tm, SSM_XBC), lambda i: (rev(i), 0)), pl.BlockSpec((tm, SSM_HEADS), lambda i: (rev(i), 0)),
                  pl.BlockSpec((tm, SSM_HEADS), lambda i: (rev(i), 0)),
                  pl.BlockSpec((SSD_CHUNKS, 1, SSM_INNER), lambda i: (rev(i), 0, 0)),
                  whole(d_x), whole(ind), whole(ind_t),
                  pl.BlockSpec((SSD_CHUNKS, SSM_INNER, SSM_STATE), lambda i: (rev(i), 0, 0)),
                  pl.BlockSpec((tm, SSM_INNER), lambda i: (rev(i), 0))],
        out_specs=[pl.BlockSpec((tm, SSM_XBC), lambda i: (rev(i), 0)), pl.BlockSpec((tm, SSM_HEADS), lambda i: (rev(i), 0)),
                   pl.BlockSpec((tm, SSM_HEADS), lambda i: (rev(i), 0)),
                   pl.BlockSpec((SSD_CHUNKS, 1, SSM_INNER), lambda i: (rev(i), 0, 0)),
                   pl.BlockSpec((SUBLANES, SSM_HEADS), lambda i: (0, 0))],
        out_shape=[jax.ShapeDtypeStruct((nrow, SSM_XBC), F32), jax.ShapeDtypeStruct((nrow, SSM_HEADS), F32),
                   jax.ShapeDtypeStruct((nrow, SSM_HEADS), F32), jax.ShapeDtypeStruct((nrow // CHUNK, 1, SSM_INNER), F32),
                   jax.ShapeDtypeStruct((SUBLANES, SSM_HEADS), F32)],
        scratch_shapes=[pltpu.VMEM((SSM_INNER, SSM_STATE), F32), pltpu.VMEM((CHUNK, SSM_XBC), F32),
                        pltpu.VMEM((CHUNK, SSM_XBC), F32), wide(), wide(), wide(), wide(), wide(), wide(), wide(),
                        group(), group(), group(), group(), group(), pltpu.VMEM((CHUNK, LANES), F32),
                        pltpu.VMEM((SSM_HEADS, SSM_STATE), F32), pltpu.VMEM((SUBLANES, SSM_HEADS), F32)],
        compiler_params=_params(("arbitrary",)),
    )(pre, dt, a_cum, a_flat, d_x, ind, ind_t, states, dy)


def _local_step(x, target, w):
    g = {}
    bs_col = w["gmlp_bs"].reshape(GMLP_GROUPS, GMLP_BLOCK, 1)
    b0, b1 = w["gate_bias"][0:1], w["gate_bias"][1:2]

    xn = _rms_fwd("mix_norm", x, w["mix_norm_w"])
    big = dict(bm=1024, bn=1024, bk=1024)
    gates = _mm("in_gates", xn, w["w_g"], **big)
    za = _mm("in_gmlp", xn, w["w_za"], **big)
    z = _mm("in_z", xn, w["w_z"], **big)
    xbc = _mm("in_xbc", xn, w["w_xbc"], **big)
    dt_raw = _mm("in_dt", xn, w["w_dt"], bm=1024, bn=SSM_HEADS, bk=1024)

    ya_pre = _gmlp_fwd("gmlp_fwd", za, w["gmlp_ln_w"], w["gmlp_ln_b"], w["gmlp_ws"], bs_col)
    y_a = _mm("proj_a", ya_pre, w["w_proj_a"], **big)

    pre = _conv_fwd("ssm_conv_fwd", xbc, w["ssm_conv_w"], w["ssm_conv_b"], tc=1024)
    dt, a_cum = _dt_prep("dt_prep", dt_raw, w["ssm_dt_bias"], w["ssm_a_log"])
    a_flat = jnp.transpose(a_cum.reshape(-1, CHUNK, SSM_HEADS), (0, 2, 1)).reshape(-1, 1, SSM_INNER)
    d_x = jnp.repeat(w["ssm_d"], SSM_HEAD_DIM, axis=1)
    ind = _head_indicator()
    y_ssd, states = _ssd_fwd("ssd_fwd", pre, dt, a_cum, a_flat, d_x, ind)
    yb_pre = _gate_norm_fwd("gate_norm_fwd", y_ssd, z, w["ssm_norm_w"])
    y_b = _mm("proj_b", yb_pre, w["w_proj_b"], **big)

    merged = _merge_fwd("merge_fwd", gates, y_a, y_b, b0, b1)
    h1 = _mm("out_proj", merged, w["w_out"], res=x, **big)

    hn = _rms_fwd("ffn_norm", h1, w["ffn_norm_w"])
    half = dict(bm=1024, bn=D_FF // 2, bk=1024)
    pg = _mm("ffn_up_gate", hn, w["w_up_g"], **half)
    pv = _mm("ffn_up_val", hn, w["w_up_v"], **half)
    cw, cb = w["ffn_conv_w"], w["ffn_conv_b"]
    gate, val, act = _ffn_act_fwd("ffn_act_fwd", pg, pv, cw[:, :D_FF], cw[:, D_FF:], cb[:, :D_FF], cb[:, D_FF:],
                                  tc=D_FF // 2)
    h2 = _mm("ffn_down", act, w["w_down"], res=h1, bm=512, bn=1024, bk=D_FF // 2)

    dh2, loss_part, g["final_norm_w"] = _final_loss("final_loss", h2, target, w["final_norm_w"].reshape(1, D_MODEL))

    dact = _mm("d_act", dh2, w["w_down"], tb=True, bm=1024, bn=D_FF // 2, bk=1024)
    wgrad = dict(ta=True, bk=512, out_dtype=BF16)
    g["w_down"] = _mm("dw_down", act, dh2, bm=D_FF // 2, bn=1024, **wgrad)
    dgate, dval = _ffn_act_bwd("ffn_act_bwd", dact, gate, val)
    dpg, dcwg, dcbg = _conv_bwd("ffn_conv_bwd_gate", dgate, pg, cw[:, :D_FF], tc=D_FF // 2)
    dpv, dcwv, dcbv = _conv_bwd("ffn_conv_bwd_val", dval, pv, cw[:, D_FF:], tc=D_FF // 2)
    g["ffn_conv_w"] = jnp.concatenate([dcwg, dcwv], axis=1)
    g["ffn_conv_b"] = jnp.concatenate([dcbg, dcbv], axis=1)
    back = dict(bm=1024, bn=1024, bk=D_FF // 2)
    dhn = _mm("d_hn_gate", dpg, w["w_up_g"], tb=True, **back)
    dhn = _mm("d_hn_val", dpv, w["w_up_v"], tb=True, res=dhn, **back)
    g["w_up_g"] = _mm("dw_up_gate", hn, dpg, bm=1024, bn=D_FF // 2, **wgrad)
    g["w_up_v"] = _mm("dw_up_val", hn, dpv, bm=1024, bn=D_FF // 2, **wgrad)
    dh1, g["ffn_norm_w"] = _rms_bwd("ffn_norm_bwd", h1, w["ffn_norm_w"], dhn, dh2)

    dmerged = _mm("d_merged", dh1, w["w_out"], tb=True, **big)
    g["w_out"] = _mm("dw_out", merged, dh1, bm=1024, bn=1024, **wgrad)
    dgates, dya, dyb, db0, db1 = _merge_bwd("merge_bwd", gates, y_a, y_b, dmerged, b0, b1)
    g["gate_bias"] = jnp.concatenate([db0, db1], axis=0)

    dya_pre = _mm("d_ya_pre", dya, w["w_proj_a"], tb=True, **big)
    g["w_proj_a"] = _mm("dw_proj_a", ya_pre, dya, bm=1024, bn=1024, **wgrad)
    dza, g["gmlp_ln_w"], g["gmlp_ln_b"], g["gmlp_ws"], dbs = _gmlp_bwd(
        "gmlp_bwd", za, dya_pre, w["gmlp_ln_w"], w["gmlp_ln_b"], w["gmlp_ws"], bs_col)
    g["gmlp_bs"] = dbs.reshape(GMLP_GROUPS, GMLP_BLOCK)

    dyb_pre = _mm("d_yb_pre", dyb, w["w_proj_b"], tb=True, **big)
    g["w_proj_b"] = _mm("dw_proj_b", yb_pre, dyb, bm=1024, bn=1024, **wgrad)
    dy_ssd, dz, g["ssm_norm_w"] = _gate_norm_bwd("gate_norm_bwd", y_ssd, z, dyb_pre, w["ssm_norm_w"])
    dpre, ddt, da_tok, da_flat, dd = _ssd_bwd("ssd_bwd", pre, dt, a_cum, a_flat, d_x, ind, ind.T, states, dy_ssd)
    g["ssm_d"] = dd[0:1]
    da_src = jnp.transpose(da_flat.reshape(-1, SSM_HEADS, CHUNK), (0, 2, 1)).reshape(-1, SSM_HEADS)
    ddt_raw, g["ssm_dt_bias"], g["ssm_a_log"] = _dt_bwd("dt_bwd", dt_raw, ddt, da_tok, da_src,
                                                         w["ssm_dt_bias"], w["ssm_a_log"])
    dxbc, g["ssm_conv_w"], g["ssm_conv_b"] = _conv_bwd("ssm_conv_bwd", dpre, xbc, w["ssm_conv_w"], tc=1024)

    dxn = _mm("d_xn_gates", dgates, w["w_g"], tb=True, **big)
    dxn = _mm("d_xn_gmlp", dza, w["w_za"], tb=True, res=dxn, **big)
    dxn = _mm("d_xn_z", dz, w["w_z"], tb=True, res=dxn, **big)
    dxn = _mm("d_xn_xbc", dxbc, w["w_xbc"], tb=True, res=dxn, **big)
    dxn = _mm("d_xn_dt", ddt_raw, w["w_dt"], tb=True, res=dxn, bm=1024, bn=1024, bk=SSM_HEADS)
    g["w_g"] = _mm("dw_gates", xn, dgates, bm=1024, bn=1024, **wgrad)
    g["w_za"] = _mm("dw_gmlp", xn, dza, bm=1024, bn=1024, **wgrad)
    g["w_z"] = _mm("dw_z", xn, dz, bm=1024, bn=1024, **wgrad)
    g["w_xbc"] = _mm("dw_xbc", xn, dxbc, bm=1024, bn=1024, **wgrad)
    g["w_dt"] = _mm("dw_dt", xn, ddt_raw, bm=1024, bn=SSM_HEADS, **wgrad)
    grad_x, g["mix_norm_w"] = _rms_bwd("mix_norm_bwd", x, w["mix_norm_w"], dxn, dh1)
    return loss_part, grad_x, g


def _position():
    return lax.axis_index("x"), lax.axis_index("y"), lax.axis_index("c")


def _exchange_chips(name, arrs, scatter):
    n = len(arrs)
    shapes = [a.shape if scatter else (N_CHIPS,) + a.shape for a in arrs]

    def body(*refs):
        ins, outs = refs[:n], refs[n:2 * n]
        send_sems, recv_sems, local_sems = refs[2 * n:]
        x, y, c = _position()
        me = 2 * x + y
        peers = [(1 - x, y), (x, 1 - y), (1 - x, 1 - y)]
        local, sends = [], []
        for i in range(n):
            cp = pltpu.make_async_copy(ins[i].at[me] if scatter else ins[i], outs[i].at[me], local_sems.at[i])
            cp.start()
            local.append(cp)
        for i in range(n):
            for k, (px, py) in enumerate(peers):
                cp = pltpu.make_async_remote_copy(
                    src_ref=ins[i].at[2 * px + py] if scatter else ins[i], dst_ref=outs[i].at[me],
                    send_sem=send_sems.at[i, k], recv_sem=recv_sems.at[i, k],
                    device_id=(px, py, c), device_id_type=MESH)
                cp.start()
                sends.append(cp)
        for i in range(n):
            for k, (px, py) in enumerate(peers):
                pltpu.make_async_remote_copy(
                    src_ref=ins[i].at[me] if scatter else ins[i], dst_ref=outs[i].at[2 * px + py],
                    send_sem=send_sems.at[i, k], recv_sem=recv_sems.at[i, k],
                    device_id=(px, py, c), device_id_type=MESH).wait_recv()
        for cp in sends:
            cp.wait_send()
        for cp in local:
            cp.wait()

    hbm = pl.BlockSpec(memory_space=pl.ANY)
    return pl.pallas_call(
        body, name=name,
        in_specs=[hbm] * n, out_specs=[hbm] * n,
        out_shape=[jax.ShapeDtypeStruct(s, a.dtype) for s, a in zip(shapes, arrs)],
        scratch_shapes=[pltpu.SemaphoreType.DMA((n, N_CHIPS - 1)), pltpu.SemaphoreType.DMA((n, N_CHIPS - 1)),
                        pltpu.SemaphoreType.DMA((n,))],
        compiler_params=pltpu.CompilerParams(has_side_effects=True),
    )(*arrs)


def _swap_cores(name, arrs):
    n = len(arrs)

    def body(*refs):
        ins, outs = refs[:n], refs[n:2 * n]
        send_sems, recv_sems = refs[2 * n:]
        x, y, c = _position()
        copies = [pltpu.make_async_remote_copy(src_ref=ins[i], dst_ref=outs[i], send_sem=send_sems.at[i],
                                               recv_sem=recv_sems.at[i], device_id=(x, y, 1 - c), device_id_type=MESH)
                  for i in range(n)]
        for cp in copies:
            cp.start()
        for cp in copies:
            cp.wait_recv()
        for cp in copies:
            cp.wait_send()

    hbm = pl.BlockSpec(memory_space=pl.ANY)
    return pl.pallas_call(
        body, name=name, in_specs=[hbm] * n, out_specs=[hbm] * n,
        out_shape=[jax.ShapeDtypeStruct(a.shape, a.dtype) for a in arrs],
        scratch_shapes=[pltpu.SemaphoreType.DMA((n,)), pltpu.SemaphoreType.DMA((n,))],
        compiler_params=pltpu.CompilerParams(has_side_effects=True),
    )(*arrs)


def _all_reduce(name, pack):
    def body(in_ref, out_ref, buf, send_sems, recv_sems):
        x, y, c = _position()
        me = 4 * x + 2 * y + c
        flips = [(dx, dy, dc) for dx in (0, 1) for dy in (0, 1) for dc in (0, 1) if (dx, dy, dc) != (0, 0, 0)]
        peers = [((1 - x) if dx else x, (1 - y) if dy else y, (1 - c) if dc else c) for dx, dy, dc in flips]
        buf[me] = in_ref[...]
        sends = []
        for k, peer in enumerate(peers):
            cp = pltpu.make_async_remote_copy(src_ref=in_ref, dst_ref=buf.at[me], send_sem=send_sems.at[k],
                                              recv_sem=recv_sems.at[k], device_id=peer, device_id_type=MESH)
            cp.start()
            sends.append(cp)
        for k, (px, py, pc) in enumerate(peers):
            pltpu.make_async_remote_copy(src_ref=in_ref, dst_ref=buf.at[4 * px + 2 * py + pc], send_sem=send_sems.at[k],
                                         recv_sem=recv_sems.at[k], device_id=(px, py, pc), device_id_type=MESH).wait_recv()
        total = buf[0]
        for j in range(1, N_DEV):
            total = total + buf[j]
        out_ref[...] = total
        for cp in sends:
            cp.wait_send()

    vmem = pl.BlockSpec(memory_space=pltpu.VMEM)
    return pl.pallas_call(
        body, name=name, in_specs=[vmem], out_specs=vmem,
        out_shape=jax.ShapeDtypeStruct(pack.shape, F32),
        scratch_shapes=[pltpu.VMEM((N_DEV,) + pack.shape, F32), pltpu.SemaphoreType.DMA((N_DEV - 1,)),
                        pltpu.SemaphoreType.DMA((N_DEV - 1,))],
        compiler_params=pltpu.CompilerParams(has_side_effects=True, vmem_limit_bytes=VMEM_LIMIT_V7X),
    )(pack)


PACK_UNIT = SUBLANES * LANES


def _pack(arrs):
    flat = []
    for a in arrs:
        v = a.reshape(-1).astype(F32)
        flat.append(jnp.pad(v, (0, -v.size % PACK_UNIT)))
    return jnp.concatenate(flat).reshape(-1, LANES)


def _unpack(pack, shapes):
    flat = pack.reshape(-1)
    out, off = [], 0
    for s in shapes:
        size = 1
        for d in s:
            size *= d
        out.append(flat[off:off + size].reshape(s))
        off += size + (-size % PACK_UNIT)
    return out


SMALL = ["mix_norm_w", "gate_bias", "gmlp_ln_w", "gmlp_ln_b", "gmlp_ws", "gmlp_bs", "ssm_conv_w", "ssm_conv_b",
         "ssm_dt_bias", "ssm_a_log", "ssm_d", "ssm_norm_w", "ffn_norm_w", "ffn_conv_w", "ffn_conv_b", "final_norm_w"]
SMALL_SHARDED = ("gate_bias", "ssm_conv_w", "ffn_conv_w")
BIG = ["w_in", "w_proj_a", "w_proj_b", "w_out", "ffn_w_up", "ffn_w_down"]
WEIGHTS = ["mix_norm_w", "w_in", "gate_bias", "gmlp_ln_w", "gmlp_ln_b", "gmlp_ws", "gmlp_bs", "ssm_conv_w",
           "ssm_conv_b", "ssm_dt_bias", "ssm_a_log", "ssm_d", "ssm_norm_w", "w_proj_a", "w_proj_b", "w_out",
           "ffn_norm_w", "ffn_w_up", "ffn_conv_w", "ffn_conv_b", "ffn_w_down", "final_norm_w"]
IN_SPLITS = [0, 2048, 4096, 6144, 9216, 9248]


def _columns_from_chips(stack):
    return jnp.transpose(stack, (1, 0, 2)).reshape(stack.shape[1], -1)


def _columns_to_chips(full, parts=N_CHIPS):
    rows, cols = full.shape
    return jnp.transpose(full.reshape(rows, parts, cols // parts), (1, 0, 2))


def kernel(x, mix_norm_w, w_in, gate_bias, gmlp_ln_w, gmlp_ln_b, gmlp_ws, gmlp_bs, ssm_conv_w, ssm_conv_b, ssm_dt_bias, ssm_a_log, ssm_d, ssm_norm_w, w_proj_a, w_proj_b, w_out, ffn_norm_w, ffn_w_up, ffn_conv_w, ffn_conv_b, ffn_w_down, final_norm_w, loss_target, m_mix_norm_w, m_w_in, m_gate_bias, m_gmlp_ln_w, m_gmlp_ln_b, m_gmlp_ws, m_gmlp_bs, m_ssm_conv_w, m_ssm_conv_b, m_ssm_dt_bias, m_ssm_a_log, m_ssm_d, m_ssm_norm_w, m_w_proj_a, m_w_proj_b, m_w_out, m_ffn_norm_w, m_ffn_w_up, m_ffn_conv_w, m_ffn_conv_b, m_ffn_w_down, m_final_norm_w, v_mix_norm_w, v_w_in, v_gate_bias, v_gmlp_ln_w, v_gmlp_ln_b, v_gmlp_ws, v_gmlp_bs, v_ssm_conv_w, v_ssm_conv_b, v_ssm_dt_bias, v_ssm_a_log, v_ssm_d, v_ssm_norm_w, v_w_proj_a, v_w_proj_b, v_w_out, v_ffn_norm_w, v_ffn_w_up, v_ffn_conv_w, v_ffn_conv_b, v_ffn_w_down, v_final_norm_w):
    args = dict(locals())
    weights = {n: args[n] for n in WEIGHTS}
    moments_m = {n: args["m_" + n] for n in WEIGHTS}
    moments_v = {n: args["v_" + n] for n in WEIGHTS}
    chip = 2 * lax.axis_index("x") + lax.axis_index("y")

    shards = [weights[n][0].astype(BF16) for n in BIG] + [weights[n][0] for n in SMALL_SHARDED]
    gathered = _exchange_chips("gather_weights", shards, scatter=False)
    w_in_s, w_pa_s, w_pb_s, w_out_s, w_up_s, w_down_s, gb_s, scw_s, fcw_s = gathered
    w_in_full = _columns_from_chips(w_in_s)
    full = {"w_" + nm: w_in_full[:, IN_SPLITS[k]:IN_SPLITS[k + 1]] for k, nm in enumerate(["g", "za", "z", "xbc", "dt"])}
    full["w_proj_a"] = w_pa_s.reshape(-1, D_MODEL)
    full["w_proj_b"] = w_pb_s.reshape(-1, D_MODEL)
    full["w_out"] = w_out_s.reshape(-1, D_MODEL)
    full["w_down"] = w_down_s.reshape(-1, D_MODEL)
    full["w_up_g"] = _columns_from_chips(w_up_s[:2])
    full["w_up_v"] = _columns_from_chips(w_up_s[2:])
    full["gate_bias"] = _columns_from_chips(gb_s)
    full["ssm_conv_w"] = _columns_from_chips(scw_s)
    full["ffn_conv_w"] = _columns_from_chips(fcw_s)
    for n in SMALL:
        if n not in SMALL_SHARDED:
            full[n] = weights[n] if n == "final_norm_w" else weights[n][0]
    for n in ("mix_norm_w", "ffn_norm_w", "ssm_conv_b", "ssm_dt_bias", "ssm_a_log", "ssm_d", "ssm_norm_w", "ffn_conv_b"):
        full[n] = full[n].reshape(1, -1)

    loss_part, grad_x, g = _local_step(x[0], loss_target[0], full)

    small_shapes = [(1, LANES)] + [g[n].shape for n in SMALL]
    reduced = _unpack(_all_reduce("reduce_small", _pack([loss_part] + [g[n] for n in SMALL])), small_shapes)
    loss = reduced[0][0, 0]
    small_grads = {}
    for n, r in zip(SMALL, reduced[1:]):
        if n in SMALL_SHARDED:
            width = weights[n].shape[2]
            r = lax.dynamic_slice_in_dim(r, chip * width, width, axis=1)
        small_grads[n] = r.reshape(weights[n].shape)
    packs = [_pack([d[n] for n in SMALL]) for d in (weights, moments_m, moments_v, small_grads)]
    upd = _adamw("adamw_small", packs[0], [packs[3]], packs[1], packs[2], tm=packs[0].shape[0], rs=SUBLANES)
    small_out = [_unpack(u, [weights[n].shape for n in SMALL]) for u in upd]

    dw_in = jnp.concatenate([g["w_g"], g["w_za"], g["w_z"], g["w_xbc"], g["w_dt"]], axis=1)
    stacks = [
        _columns_to_chips(dw_in),
        g["w_proj_a"].reshape(N_CHIPS, -1, D_MODEL),
        g["w_proj_b"].reshape(N_CHIPS, -1, D_MODEL),
        g["w_out"].reshape(N_CHIPS, -1, D_MODEL),
        jnp.concatenate([_columns_to_chips(g["w_up_g"], 2), _columns_to_chips(g["w_up_v"], 2)], axis=0),
        g["w_down"].reshape(N_CHIPS, -1, D_MODEL),
    ]
    received = _exchange_chips("scatter_grads", stacks, scatter=True)
    tiles = {"w_in": 128, "w_proj_a": 256, "w_proj_b": 256, "w_out": 256, "ffn_w_up": 128, "ffn_w_down": 176}
    mine = [_sum_slots("sum_" + n, r, tm=tiles[n], rs=2 * SUBLANES) for n, r in zip(BIG, received)]
    theirs = _swap_cores("swap_grads", mine)
    big_out = {}
    for n, a, b in zip(BIG, mine, theirs):
        big_out[n] = _adamw("adamw_" + n, weights[n][0], [a, b], moments_m[n][0], moments_v[n][0],
                            tm=tiles[n], rs=SUBLANES)

    per_kind = [[], [], [], []]
    for n in WEIGHTS:
        for kind in range(4):
            if n in big_out:
                per_kind[kind].append(big_out[n][kind].reshape(weights[n].shape))
            else:
                per_kind[kind].append(small_out[kind][SMALL.index(n)])
    return (loss, grad_x[None], *per_kind[0], *per_kind[1], *per_kind[2], *per_kind[3])
```

```python
import jax
import jax.numpy as jnp
from jax import lax
from jax.experimental import pallas as pl
from jax.experimental.pallas import tpu as pltpu

F32 = jnp.float32
BF16 = jnp.bfloat16
MESH = pl.DeviceIdType.MESH

EPS = 1e-5
D_MODEL = 1024
GMLP_BLOCK = 128
GMLP_GROUPS = 8
CHUNK = 64
SSM_INNER = 2048
SSM_HEADS = 32
SSM_HEAD_DIM = 64
SSM_GROUPS = 4
SSM_HPG = 8
SSM_STATE = 128
SSM_CONV = 4
SSM_XBC = 3072
D_FF = 2816
FFN_CONV = 3
N_CHIPS = 4
N_DEV = 8

ADAM_LR = 0.001
ADAM_B1 = 0.9
ADAM_B2 = 0.999
ADAM_EPS = 1e-08
ADAM_WD = 0.01
ADAM_STEP = 10

VMEM_LIMIT_V7X = 56 * 1024 * 1024
SUBLANES = 8
LANES = 128


def _params(sem=None):
    return pltpu.CompilerParams(dimension_semantics=sem, vmem_limit_bytes=VMEM_LIMIT_V7X)


def _dot(a, b, ca=1, cb=0):
    return lax.dot_general(a.astype(BF16), b.astype(BF16), (((ca,), (cb,)), ((), ())),
                           preferred_element_type=F32)


def _mm(name, a, b, *, ta=False, tb=False, out_dtype=F32, bm, bn, bk, res=None):
    m, k = (a.shape[1], a.shape[0]) if ta else a.shape
    k2, n = (b.shape[1], b.shape[0]) if tb else b.shape
    assert k == k2 and m % bm == 0 and n % bn == 0 and k % bk == 0, (name, a.shape, b.shape)
    nk = k // bk
    a_spec = (pl.BlockSpec((bk, bm), lambda i, j, kk: (kk, i)) if ta
              else pl.BlockSpec((bm, bk), lambda i, j, kk: (i, kk)))
    b_spec = (pl.BlockSpec((bn, bk), lambda i, j, kk: (j, kk)) if tb
              else pl.BlockSpec((bk, bn), lambda i, j, kk: (kk, j)))
    o_spec = pl.BlockSpec((bm, bn), lambda i, j, kk: (i, j))
    has_res = res is not None

    def body(*refs):
        a_ref, b_ref = refs[0], refs[1]
        r_ref = refs[2] if has_res else None
        o_ref = refs[3] if has_res else refs[2]
        p = _dot(a_ref[...], b_ref[...], 0 if ta else 1, 1 if tb else 0)

        def finish(total):
            if has_res:
                total = total + r_ref[...]
            o_ref[...] = total.astype(out_dtype)

        if nk == 1:
            finish(p)
        else:
            acc_ref = refs[-1]
            kk = pl.program_id(2)

            @pl.when(kk == 0)
            def _():
                acc_ref[...] = p

            @pl.when(kk > 0)
            def _():
                acc_ref[...] += p

            @pl.when(kk == nk - 1)
            def _():
                finish(acc_ref[...])

    return pl.pallas_call(
        body, name=name,
        grid=(m // bm, n // bn, nk),
        in_specs=[a_spec, b_spec] + ([o_spec] if has_res else []),
        out_specs=o_spec,
        out_shape=jax.ShapeDtypeStruct((m, n), out_dtype),
        scratch_shapes=[pltpu.VMEM((bm, bn), F32)] if nk > 1 else [],
        compiler_params=_params(("parallel", "parallel", "arbitrary")),
    )(*([a, b] + ([res] if has_res else [])))


def _rows(name, fn, ins, params, outs, accs, *, tm, rs):
    nrow = ins[0][0].shape[-2]
    assert nrow % tm == 0 and tm % rs == 0, (name, nrow, tm, rs)
    n_in, n_p, n_out, n_acc = len(ins), len(params), len(outs), len(accs)
    in_specs = []
    for spec in ins:
        arr, width, cb = spec[:3]
        if len(spec) == 4:
            in_specs.append(pl.BlockSpec((None, tm, width), lambda i, cb=cb, lead=spec[3]: (lead, i, cb)))
        else:
            in_specs.append(pl.BlockSpec((tm, width), lambda i, cb=cb: (i, cb)))
    for p in params:
        in_specs.append(pl.BlockSpec(p.shape, lambda i, nd=p.ndim: (0,) * nd))
    out_specs = [pl.BlockSpec((tm, w), lambda i: (i, 0)) for w, _ in outs]
    out_specs += [pl.BlockSpec(s, lambda i: (0, 0)) for s in accs]
    out_shape = [jax.ShapeDtypeStruct((nrow, w), dt) for w, dt in outs]
    out_shape += [jax.ShapeDtypeStruct(s, F32) for s in accs]

    def body(*refs):
        in_refs = refs[:n_in]
        p_refs = refs[n_in:n_in + n_p]
        o_refs = refs[n_in + n_p:n_in + n_p + n_out]
        a_refs = refs[n_in + n_p + n_out:]
        pv = [p[...] for p in p_refs]

        if n_acc:
            @pl.when(pl.program_id(0) == 0)
            def _():
                for a_ref in a_refs:
                    a_ref[...] = jnp.zeros_like(a_ref)

        def step(r, carry):
            sl = pl.ds(pl.multiple_of(r * rs, rs), rs)
            vals = [ref[sl, :] for ref in in_refs]
            row_out, sums = fn(*vals, *pv)
            for o_ref, v in zip(o_refs, row_out):
                o_ref[sl, :] = v.astype(o_ref.dtype)
            return tuple(c + s for c, s in zip(carry, sums))

        init = tuple(jnp.zeros(s, F32) for s in accs)
        total = lax.fori_loop(0, tm // rs, step, init)
        for a_ref, t in zip(a_refs, total):
            a_ref[...] += t

    res = pl.pallas_call(
        body, name=name, grid=(nrow // tm,),
        in_specs=in_specs, out_specs=out_specs, out_shape=out_shape,
        compiler_params=_params(("arbitrary",)),
    )(*([s[0] for s in ins] + list(params)))
    return res


def _rms(x, w):
    return x * lax.rsqrt(jnp.mean(x * x, axis=-1, keepdims=True) + EPS) * w


def _colsum(v):
    return jnp.sum(v, axis=0, keepdims=True)


def _rms_fwd(name, x, w):
    def fn(xv, wv):
        return (_rms(xv, wv),), ()
    return _rows(name, fn, [(x, D_MODEL, 0)], [w], [(D_MODEL, BF16)], [], tm=512, rs=16)[0]


def _rms_bwd(name, x, w, dy, dres):
    def fn(xv, dyv, drv, wv):
        _, vjp = jax.vjp(_rms, xv, wv)
        dx, dw = vjp(dyv)
        return (drv + dx,), (dw,)
    return _rows(name, fn, [(x, D_MODEL, 0), (dy, D_MODEL, 0), (dres, D_MODEL, 0)], [w],
                 [(D_MODEL, F32)], [(1, D_MODEL)], tm=512, rs=16)


def _final_loss(name, h, target, w):
    def fn(hv, tv, wv):
        y, vjp = jax.vjp(_rms, hv, wv)
        err = y - tv
        part = 0.5 * jnp.sum(jnp.mean(err * err, axis=-1, keepdims=True), axis=0, keepdims=True)
        dh, dw = vjp(err / D_MODEL)
        return (dh,), (jnp.broadcast_to(part, (1, LANES)), dw)
    return _rows(name, fn, [(h, D_MODEL, 0), (target, D_MODEL, 0)], [w],
                 [(D_MODEL, F32)], [(1, LANES), (1, D_MODEL)], tm=512, rs=16)


def _merge(ga, gb, ya, yb, b0, b1):
    return jax.nn.sigmoid(ga + b0) * ya + jax.nn.sigmoid(gb + b1) * yb


def _merge_fwd(name, g, ya, yb, b0, b1):
    def fn(ga, gb, yav, ybv, b0v, b1v):
        return (_merge(ga, gb, yav, ybv, b0v, b1v),), ()
    return _rows(name, fn, [(g, D_MODEL, 0), (g, D_MODEL, 1), (ya, D_MODEL, 0), (yb, D_MODEL, 0)],
                 [b0, b1], [(D_MODEL, BF16)], [], tm=512, rs=16)[0]


def _merge_bwd(name, g, ya, yb, dm, b0, b1):
    def fn(ga, gb, yav, ybv, dmv, b0v, b1v):
        _, vjp = jax.vjp(_merge, ga, gb, yav, ybv, b0v, b1v)
        dga, dgb, dya, dyb, db0, db1 = vjp(dmv)
        return (jnp.concatenate([dga, dgb], axis=1), dya, dyb), (db0, db1)
    return _rows(name, fn,
                 [(g, D_MODEL, 0), (g, D_MODEL, 1), (ya, D_MODEL, 0), (yb, D_MODEL, 0), (dm, D_MODEL, 0)],
                 [b0, b1], [(2 * D_MODEL, BF16), (D_MODEL, BF16), (D_MODEL, BF16)],
                 [(1, D_MODEL), (1, D_MODEL)], tm=512, rs=16)


GROUP_W = SSM_INNER // SSM_GROUPS


def _gate_norm_group(y, z, nw):
    v = y * jax.nn.silu(z)
    return v * lax.rsqrt(jnp.mean(v * v, axis=-1, keepdims=True) + EPS) * nw


def _gate_norm_fwd(name, y, z, nw):
    def fn(yv, zv, nwv):
        parts = [_gate_norm_group(yv[:, k * GROUP_W:(k + 1) * GROUP_W], zv[:, k * GROUP_W:(k + 1) * GROUP_W],
                                  nwv[:, k * GROUP_W:(k + 1) * GROUP_W]) for k in range(SSM_GROUPS)]
        return (jnp.concatenate(parts, axis=1),), ()
    return _rows(name, fn, [(y, SSM_INNER, 0), (z, SSM_INNER, 0)], [nw], [(SSM_INNER, BF16)], [],
                 tm=512, rs=16)[0]


def _gate_norm_bwd(name, y, z, dout, nw):
    def fn(yv, zv, dv, nwv):
        dys, dzs, dns = [], [], []
        for k in range(SSM_GROUPS):
            sl = slice(k * GROUP_W, (k + 1) * GROUP_W)
            _, vjp = jax.vjp(_gate_norm_group, yv[:, sl], zv[:, sl], nwv[:, sl])
            dy, dz, dn = vjp(dv[:, sl])
            dys.append(dy), dzs.append(dz), dns.append(dn)
        return (jnp.concatenate(dys, axis=1), jnp.concatenate(dzs, axis=1)), (jnp.concatenate(dns, axis=1),)
    return _rows(name, fn, [(y, SSM_INNER, 0), (z, SSM_INNER, 0), (dout, SSM_INNER, 0)], [nw],
                 [(SSM_INNER, F32), (SSM_INNER, BF16)], [(1, SSM_INNER)], tm=512, rs=16)


def _softplus(v):
    return jnp.maximum(v, 0.0) + jnp.log1p(jnp.exp(-jnp.abs(v)))


def _chunk_cumsum(v, reverse=False):
    row = lax.broadcasted_iota(jnp.int32, v.shape, 0)
    step = 1
    while step < CHUNK:
        if reverse:
            shifted = pltpu.roll(v, CHUNK - step, axis=0)
            v = v + jnp.where(row < CHUNK - step, shifted, 0.0)
        else:
            shifted = pltpu.roll(v, step, axis=0)
            v = v + jnp.where(row >= step, shifted, 0.0)
        step *= 2
    return v


def _dt_prep(name, dt_raw, dt_bias, a_log):
    def fn(rv, bv, alv):
        dt = _softplus(rv + bv)
        return (dt, _chunk_cumsum(dt * (-jnp.exp(alv)))), ()
    return _rows(name, fn, [(dt_raw, SSM_HEADS, 0)], [dt_bias, a_log],
                 [(SSM_HEADS, F32), (SSM_HEADS, F32)], [], tm=512, rs=CHUNK)


def _dt_bwd(name, dt_raw, ddt, da1, da2, dt_bias, a_log):
    def fn(rv, ddv, d1, d2, bv, alv):
        pre = rv + bv
        dt = _softplus(pre)
        a_neg = -jnp.exp(alv)
        back = _chunk_cumsum(d1 + d2, reverse=True)
        d_dt = ddv + back * a_neg
        d_raw = d_dt * jax.nn.sigmoid(pre)
        return (d_raw,), (_colsum(d_raw), _colsum(back * dt) * a_neg)
    return _rows(name, fn, [(dt_raw, SSM_HEADS, 0), (ddt, SSM_HEADS, 0), (da1, SSM_HEADS, 0), (da2, SSM_HEADS, 0)],
                 [dt_bias, a_log], [(SSM_HEADS, BF16)], [(1, SSM_HEADS), (1, SSM_HEADS)], tm=512, rs=CHUNK)


def _adamw(name, w, g_parts, m, v, *, tm, rs):
    width = w.shape[1]
    n_g = len(g_parts)

    def fn(wv, mv, vv, *gs):
        g = gs[0]
        for extra in gs[1:]:
            g = g + extra
        m_new = ADAM_B1 * mv + (1.0 - ADAM_B1) * g
        v_new = ADAM_B2 * vv + (1.0 - ADAM_B2) * jnp.square(g)
        m_hat = m_new / (1.0 - ADAM_B1 ** ADAM_STEP)
        v_hat = v_new / (1.0 - ADAM_B2 ** ADAM_STEP)
        delta = -ADAM_LR * (m_hat / (jnp.sqrt(v_hat) + ADAM_EPS) + ADAM_WD * wv)
        return (g, delta, m_new, v_new), ()
    assert n_g >= 1
    return _rows(name, fn, [(w, width, 0), (m, width, 0), (v, width, 0)] + [(g, width, 0) for g in g_parts],
                 [], [(width, F32)] * 4, [], tm=tm, rs=rs)


def _pair_sum(name, a, b, *, tm):
    shape = a.shape
    flat = (shape[0] * shape[1], shape[2])

    def fn(av, bv):
        return (av.astype(F32) + bv.astype(F32),), ()
    out = _rows(name, fn, [(a.reshape(flat), flat[1], 0), (b.reshape(flat), flat[1], 0)], [], [(flat[1], BF16)], [],
                tm=tm, rs=2 * SUBLANES)[0]
    return out.reshape(shape)


def _sum_slots(name, stack, *, tm, rs):
    width = stack.shape[2]

    def fn(*slots):
        s0, s1, s2, s3 = (s.astype(F32) for s in slots)
        return (((s0 + s1) + s2) + s3,), ()
    return _rows(name, fn, [(stack, width, 0, k) for k in range(N_CHIPS)], [], [(width, F32)], [],
                 tm=tm, rs=rs)[0]


def _layernorm(v, w, b):
    mu = jnp.mean(v, axis=-1, keepdims=True)
    var = jnp.mean(jnp.square(v - mu), axis=-1, keepdims=True)
    return (v - mu) * lax.rsqrt(var + EPS) * w + b


def _gmlp_mask():
    t = lax.broadcasted_iota(jnp.int32, (GMLP_BLOCK, GMLP_BLOCK), 0) // CHUNK
    s = lax.broadcasted_iota(jnp.int32, (GMLP_BLOCK, GMLP_BLOCK), 1) // CHUNK
    return s <= t


GMLP_TM = 512


def _gmlp_fwd(name, za, ln_w, ln_b, ws, bs_col):
    nrow = za.shape[0]
    tm = GMLP_TM
    width = GMLP_GROUPS * GMLP_BLOCK

    def body(za_ref, lnw_ref, lnb_ref, ws_ref, bs_ref, o_ref, wm_ref):
        mask = _gmlp_mask()
        for g in range(GMLP_GROUPS):
            wm_ref[g] = jnp.where(mask, ws_ref[g], 0.0).astype(BF16)

        def block(n, carry):
            rows = pl.ds(pl.multiple_of(n * GMLP_BLOCK, GMLP_BLOCK), GMLP_BLOCK)
            for g in range(GMLP_GROUPS):
                cols = slice(g * GMLP_BLOCK, (g + 1) * GMLP_BLOCK)
                vcols = slice(width + g * GMLP_BLOCK, width + (g + 1) * GMLP_BLOCK)
                u = jax.nn.gelu(za_ref[rows, cols])
                v = jax.nn.gelu(za_ref[rows, vcols])
                vn = _layernorm(v, lnw_ref[g:g + 1, :], lnb_ref[g:g + 1, :])
                sv = _dot(wm_ref[g], vn) + bs_ref[g]
                o_ref[rows, cols] = (u * sv).astype(o_ref.dtype)
            return carry

        lax.fori_loop(0, tm // GMLP_BLOCK, block, 0)

    small = lambda a: pl.BlockSpec(a.shape, lambda i, nd=a.ndim: (0,) * nd)
    return pl.pallas_call(
        body, name=name, grid=(nrow // tm,),
        in_specs=[pl.BlockSpec((tm, 2 * width), lambda i: (i, 0)), small(ln_w), small(ln_b), small(ws), small(bs_col)],
        out_specs=pl.BlockSpec((tm, width), lambda i: (i, 0)),
        out_shape=jax.ShapeDtypeStruct((nrow, width), BF16),
        scratch_shapes=[pltpu.VMEM((GMLP_GROUPS, GMLP_BLOCK, GMLP_BLOCK), BF16)],
        compiler_params=_params(("arbitrary",)),
    )(za, ln_w, ln_b, ws, bs_col)


def _gmlp_bwd(name, za, dout, ln_w, ln_b, ws, bs_col):
    nrow = za.shape[0]
    tm = GMLP_TM
    width = GMLP_GROUPS * GMLP_BLOCK

    def body(za_ref, do_ref, lnw_ref, lnb_ref, ws_ref, bs_ref, dza_ref, dlnw_ref, dlnb_ref, dws_ref, dbs_ref, wm_ref):
        mask = _gmlp_mask()
        for g in range(GMLP_GROUPS):
            wm_ref[g] = jnp.where(mask, ws_ref[g], 0.0).astype(BF16)

        @pl.when(pl.program_id(0) == 0)
        def _():
            dlnw_ref[...] = jnp.zeros_like(dlnw_ref)
            dlnb_ref[...] = jnp.zeros_like(dlnb_ref)
            dws_ref[...] = jnp.zeros_like(dws_ref)
            dbs_ref[...] = jnp.zeros_like(dbs_ref)

        def block(n, carry):
            rows = pl.ds(pl.multiple_of(n * GMLP_BLOCK, GMLP_BLOCK), GMLP_BLOCK)
            for g in range(GMLP_GROUPS):
                cols = slice(g * GMLP_BLOCK, (g + 1) * GMLP_BLOCK)
                vcols = slice(width + g * GMLP_BLOCK, width + (g + 1) * GMLP_BLOCK)
                u, gelu_u_vjp = jax.vjp(jax.nn.gelu, za_ref[rows, cols])
                v, gelu_v_vjp = jax.vjp(jax.nn.gelu, za_ref[rows, vcols])
                vn, ln_vjp = jax.vjp(_layernorm, v, lnw_ref[g:g + 1, :], lnb_ref[g:g + 1, :])
                sv = _dot(wm_ref[g], vn) + bs_ref[g]
                d_o = do_ref[rows, cols]
                dsv = d_o * u
                d_wm = _dot(dsv, vn, 1, 1)
                dvn = _dot(wm_ref[g], dsv, 0, 0)
                dv, dlnw, dlnb = ln_vjp(dvn)
                dza_ref[rows, cols] = gelu_u_vjp(d_o * sv)[0].astype(dza_ref.dtype)
                dza_ref[rows, vcols] = gelu_v_vjp(dv)[0].astype(dza_ref.dtype)
                dlnw_ref[g:g + 1, :] += dlnw
                dlnb_ref[g:g + 1, :] += dlnb
                dws_ref[g] += jnp.where(mask, d_wm, 0.0)
                dbs_ref[g] += jnp.sum(dsv, axis=1, keepdims=True)
            return carry

        lax.fori_loop(0, tm // GMLP_BLOCK, block, 0)

    small = lambda a: pl.BlockSpec(a.shape, lambda i, nd=a.ndim: (0,) * nd)
    return pl.pallas_call(
        body, name=name, grid=(nrow // tm,),
        in_specs=[pl.BlockSpec((tm, 2 * width), lambda i: (i, 0)), pl.BlockSpec((tm, width), lambda i: (i, 0)),
                  small(ln_w), small(ln_b), small(ws), small(bs_col)],
        out_specs=[pl.BlockSpec((tm, 2 * width), lambda i: (i, 0)), small(ln_w), small(ln_b), small(ws), small(bs_col)],
        out_shape=[jax.ShapeDtypeStruct((nrow, 2 * width), BF16), jax.ShapeDtypeStruct(ln_w.shape, F32),
                   jax.ShapeDtypeStruct(ln_b.shape, F32), jax.ShapeDtypeStruct(ws.shape, F32),
                   jax.ShapeDtypeStruct(bs_col.shape, F32)],
        scratch_shapes=[pltpu.VMEM((GMLP_GROUPS, GMLP_BLOCK, GMLP_BLOCK), BF16)],
        compiler_params=_params(("arbitrary",)),
    )(za, dout, ln_w, ln_b, ws, bs_col)


CONV_TM = 256
CONV_RS = 32


def _tap_rows(w_ref):
    return [w_ref[k:k + 1, :] for k in range(w_ref.shape[0])]


def _conv_rows(win, w, rs):
    taps = len(w)
    out = w[taps - 1] * win[SUBLANES:, :]
    for k in range(taps - 1):
        back = taps - 1 - k
        out = out + w[k] * pltpu.roll(win, back, axis=0)[SUBLANES:, :]
    return out


def _conv_t_rows(win, w, rs):
    taps = len(w)
    out = w[taps - 1] * win[:rs, :]
    for k in range(taps - 1):
        ahead = taps - 1 - k
        out = out + w[k] * pltpu.roll(win, rs + SUBLANES - ahead, axis=0)[:rs, :]
    return out


def _conv_dw_rows(d, xwin, taps):
    rows = []
    for k in range(taps):
        back = taps - 1 - k
        xs = xwin[SUBLANES:, :] if back == 0 else pltpu.roll(xwin, back, axis=0)[SUBLANES:, :]
        rows.append(jnp.sum(d * xs, axis=0, keepdims=True))
    return rows


def _halo_specs(nrow, tm, tc):
    per = tm // SUBLANES
    last = nrow // SUBLANES - 1
    main = pl.BlockSpec((tm, tc), lambda j, i: (i, j))
    before = pl.BlockSpec((SUBLANES, tc), lambda j, i: (jnp.maximum(i * per - 1, 0), j))
    after = pl.BlockSpec((SUBLANES, tc), lambda j, i: (jnp.minimum((i + 1) * per, last), j))
    return main, before, after


def _col_spec(rows, tc):
    return pl.BlockSpec((rows, tc), lambda j, i: (0, j))


def _conv_fwd(name, x, w, b, *, tc):
    nrow, ncol = x.shape
    taps = w.shape[0]
    tm, rs = CONV_TM, CONV_RS
    main, before, _ = _halo_specs(nrow, tm, tc)

    def body(x_ref, xb_ref, w_ref, b_ref, o_ref):
        first = pl.program_id(1) == 0
        wv, bv = _tap_rows(w_ref), b_ref[...]

        def step(r, prev):
            sl = pl.ds(pl.multiple_of(r * rs, rs), rs)
            cur = x_ref[sl, :]
            o_ref[sl, :] = _conv_rows(jnp.concatenate([prev, cur], axis=0), wv, rs) + bv
            return cur[rs - SUBLANES:, :]

        lax.fori_loop(0, tm // rs, step, jnp.where(first, 0.0, xb_ref[...]))

    return pl.pallas_call(
        body, name=name, grid=(ncol // tc, nrow // tm),
        in_specs=[main, before, _col_spec(taps, tc), _col_spec(1, tc)],
        out_specs=main, out_shape=jax.ShapeDtypeStruct((nrow, ncol), F32),
        compiler_params=_params(("parallel", "arbitrary")),
    )(x, x, w, b)


def _conv_bwd(name, dpre, x, w, *, tc):
    nrow, ncol = x.shape
    taps = w.shape[0]
    tm, rs = CONV_TM, CONV_RS
    nsub = tm // rs
    main, before, after = _halo_specs(nrow, tm, tc)

    def body(d_ref, da_ref, x_ref, xb_ref, w_ref, dx_ref, dw_ref, db_ref):
        i = pl.program_id(1)
        first, last = i == 0, i == pl.num_programs(1) - 1
        wv = _tap_rows(w_ref)
        x_before = jnp.where(first, 0.0, xb_ref[...])

        @pl.when(first)
        def _():
            dw_ref[...] = jnp.zeros_like(dw_ref)
            db_ref[...] = jnp.zeros_like(db_ref)

        def step(q, carry):
            nxt, dw, db = carry
            r = nsub - 1 - q
            sl = pl.ds(pl.multiple_of(r * rs, rs), rs)
            cur = d_ref[sl, :]
            dx_ref[sl, :] = _conv_t_rows(jnp.concatenate([cur, nxt], axis=0), wv, rs).astype(dx_ref.dtype)
            inner = x_ref[pl.ds(pl.multiple_of(jnp.maximum(r * rs - SUBLANES, 0), SUBLANES), SUBLANES), :]
            xwin = jnp.concatenate([jnp.where(r == 0, x_before, inner), x_ref[sl, :]], axis=0)
            dw = tuple(a + s for a, s in zip(dw, _conv_dw_rows(cur, xwin, taps)))
            return cur[:SUBLANES, :], dw, db + _colsum(cur)

        zero_row = jnp.zeros((1, tc), F32)
        init = (jnp.where(last, 0.0, da_ref[...]), (zero_row,) * taps, zero_row)
        _, dw, db = lax.fori_loop(0, nsub, step, init)
        for k in range(taps):
            dw_ref[k:k + 1, :] += dw[k]
        db_ref[...] += db

    return pl.pallas_call(
        body, name=name, grid=(ncol // tc, nrow // tm),
        in_specs=[main, after, main, before, _col_spec(taps, tc)],
        out_specs=[main, _col_spec(taps, tc), _col_spec(1, tc)],
        out_shape=[jax.ShapeDtypeStruct((nrow, ncol), BF16), jax.ShapeDtypeStruct((taps, ncol), F32),
                   jax.ShapeDtypeStruct((1, ncol), F32)],
        compiler_params=_params(("parallel", "arbitrary")),
    )(dpre, dpre, x, x, w)


def _glu(gate, val):
    return jax.nn.silu(gate) * val


def _ffn_act_fwd(name, pg, pv, wg, wv, bg, bv, *, tc):
    nrow, ncol = pg.shape
    taps = wg.shape[0]
    tm, rs = CONV_TM, CONV_RS
    main, before, _ = _halo_specs(nrow, tm, tc)

    def body(pg_ref, pgb_ref, pv_ref, pvb_ref, wg_ref, wv_ref, bg_ref, bv_ref, g_ref, v_ref, a_ref):
        first = pl.program_id(1) == 0
        wgv, wvv, bgv, bvv = _tap_rows(wg_ref), _tap_rows(wv_ref), bg_ref[...], bv_ref[...]

        def step(r, carry):
            prev_g, prev_v = carry
            sl = pl.ds(pl.multiple_of(r * rs, rs), rs)
            cur_g, cur_v = pg_ref[sl, :], pv_ref[sl, :]
            gate = _conv_rows(jnp.concatenate([prev_g, cur_g], axis=0), wgv, rs) + bgv
            val = _conv_rows(jnp.concatenate([prev_v, cur_v], axis=0), wvv, rs) + bvv
            g_ref[sl, :] = gate
            v_ref[sl, :] = val
            a_ref[sl, :] = _glu(gate, val).astype(a_ref.dtype)
            return cur_g[rs - SUBLANES:, :], cur_v[rs - SUBLANES:, :]

        lax.fori_loop(0, tm // rs, step, (jnp.where(first, 0.0, pgb_ref[...]), jnp.where(first, 0.0, pvb_ref[...])))

    return pl.pallas_call(
        body, name=name, grid=(ncol // tc, nrow // tm),
        in_specs=[main, before, main, before, _col_spec(taps, tc), _col_spec(taps, tc), _col_spec(1, tc), _col_spec(1, tc)],
        out_specs=[main, main, main],
        out_shape=[jax.ShapeDtypeStruct((nrow, ncol), F32), jax.ShapeDtypeStruct((nrow, ncol), F32),
                   jax.ShapeDtypeStruct((nrow, ncol), BF16)],
        compiler_params=_params(("parallel", "arbitrary")),
    )(pg, pg, pv, pv, wg, wv, bg, bv)


def _ffn_act_bwd(name, dact, gate, val):
    def fn(dv, gv, vv):
        _, vjp = jax.vjp(_glu, gv, vv)
        dg, dval = vjp(dv)
        return (dg, dval), ()
    width = dact.shape[1]
    return _rows(name, fn, [(dact, width, 0), (gate, width, 0), (val, width, 0)], [],
                 [(width, F32), (width, F32)], [], tm=256, rs=8)


SSD_TM = 256
SSD_CHUNKS = SSD_TM // CHUNK
X_OFF, B_OFF, C_OFF = 0, SSM_INNER, SSM_INNER + SSM_GROUPS * SSM_STATE
HP = SSM_HPG * SSM_HEAD_DIM


def _causal_tiled():
    row = lax.broadcasted_iota(jnp.int32, (CHUNK, HP), 0)
    src = lax.broadcasted_iota(jnp.int32, (CHUNK, HP), 1) & (CHUNK - 1)
    return src <= row


def _split3(v):
    hi = v.astype(BF16)
    rest = v - hi.astype(F32)
    mid = rest.astype(BF16)
    lo = (rest - mid.astype(F32)).astype(BF16)
    return hi, mid, lo


def _dot_exact(a, ind):
    parts = [lax.dot_general(p, ind, (((1,), (0,)), ((), ())), preferred_element_type=F32) for p in _split3(a)]
    return (parts[0] + parts[1]) + parts[2]


def _head_indicator():
    head = lax.broadcasted_iota(jnp.int32, (SSM_HEADS, SSM_INNER), 0)
    chan = lax.broadcasted_iota(jnp.int32, (SSM_HEADS, SSM_INNER), 1)
    return (chan // SSM_HEAD_DIM == head).astype(BF16)


def _chunk_decays(ci, dt_ref, ac_ref, ind, ax_ref, dtx_ref, eax_ref, eex_ref, tail_ref):
    rows = pl.ds(pl.multiple_of(ci * CHUNK, CHUNK), CHUNK)
    ax_ref[...] = _dot_exact(ac_ref[rows, :], ind)
    dtx_ref[...] = _dot_exact(dt_ref[rows, :], ind)
    eax_ref[...] = jnp.exp(ax_ref[...])
    eex_ref[...] = jnp.exp(ax_ref[CHUNK - 1:CHUNK, :] - ax_ref[...])
    tail = pl.ds(pl.multiple_of(ci * CHUNK + CHUNK - SUBLANES, SUBLANES), SUBLANES)
    tail_ref[...] = jnp.exp(ac_ref[tail, :])


def _group_decay(ci, g, ax_ref, af_ref, xbc_ref, causal):
    gcols = slice(g * HP, (g + 1) * HP)
    bm = xbc_ref[:, B_OFF + g * SSM_STATE:B_OFF + (g + 1) * SSM_STATE]
    cm = xbc_ref[:, C_OFF + g * SSM_STATE:C_OFF + (g + 1) * SSM_STATE]
    cb_tiled = _dot(cm, jnp.concatenate([bm] * SSM_HPG, axis=0), 1, 1)
    seg = ax_ref[:, gcols] - af_ref[ci, :, gcols]
    decay = jnp.where(causal, jnp.exp(jnp.where(causal, seg, 0.0)), 0.0)
    return bm, cm, cb_tiled * decay, decay


def _ssd_fwd(name, pre, dt, a_cum, a_flat, d_x, ind):
    nrow = pre.shape[0]
    tm = SSD_TM

    def body(pre_ref, dt_ref, ac_ref, af_ref, dx_ref, ind_ref, y_ref, st_ref,
             h_ref, xbc_ref, ax_ref, dtx_ref, eax_ref, eex_ref, m_ref, xd_ref, yd_ref, tail_ref):
        @pl.when(pl.program_id(0) == 0)
        def _():
            h_ref[...] = jnp.zeros_like(h_ref)

        causal = _causal_tiled()
        ind = ind_ref[...]

        def chunk(ci, carry):
            rows = pl.ds(pl.multiple_of(ci * CHUNK, CHUNK), CHUNK)
            xbc_ref[...] = jax.nn.silu(pre_ref[rows, :])
            _chunk_decays(ci, dt_ref, ac_ref, ind, ax_ref, dtx_ref, eax_ref, eex_ref, tail_ref)
            st_ref[ci] = h_ref[...]
            for g in range(SSM_GROUPS):
                gcols = slice(g * HP, (g + 1) * HP)
                bm, cm, m_all, _ = _group_decay(ci, g, ax_ref, af_ref, xbc_ref, causal)
                m_ref[...] = m_all
                x_g = xbc_ref[:, gcols]
                xd = x_g * dtx_ref[:, gcols]
                xd_ref[...] = xd
                h_g = h_ref[gcols, :]
                for hh in range(SSM_HPG):
                    lc = slice(hh * SSM_HEAD_DIM, (hh + 1) * SSM_HEAD_DIM)
                    yd_ref[:, lc] = _dot(m_ref[:, lc], xd_ref[:, lc])
                y_ref[rows, gcols] = (yd_ref[...] + _dot(cm, h_g, 1, 1) * eax_ref[:, gcols]
                                      + dx_ref[:, gcols] * x_g)
                new = _dot(xd * eex_ref[:, gcols], bm, 0, 0)
                for hh in range(SSM_HPG):
                    h = g * SSM_HPG + hh
                    hrows = slice(h * SSM_HEAD_DIM, (h + 1) * SSM_HEAD_DIM)
                    lrows = slice(hh * SSM_HEAD_DIM, (hh + 1) * SSM_HEAD_DIM)
                    h_ref[hrows, :] = tail_ref[SUBLANES - 1:SUBLANES, h:h + 1] * h_ref[hrows, :] + new[lrows, :]
            return carry

        lax.fori_loop(0, SSD_CHUNKS, chunk, 0)

    nchunk = nrow // CHUNK
    whole = lambda a: pl.BlockSpec(a.shape, lambda i, nd=a.ndim: (0,) * nd)
    wide = lambda: pltpu.VMEM((CHUNK, SSM_INNER), F32)
    group = lambda: pltpu.VMEM((CHUNK, HP), F32)
    return pl.pallas_call(
        body, name=name, grid=(nrow // tm,),
        in_specs=[pl.BlockSpec((tm, SSM_XBC), lambda i: (i, 0)), pl.BlockSpec((tm, SSM_HEADS), lambda i: (i, 0)),
                  pl.BlockSpec((tm, SSM_HEADS), lambda i: (i, 0)),
                  pl.BlockSpec((SSD_CHUNKS, 1, SSM_INNER), lambda i: (i, 0, 0)), whole(d_x), whole(ind)],
        out_specs=[pl.BlockSpec((tm, SSM_INNER), lambda i: (i, 0)),
                   pl.BlockSpec((SSD_CHUNKS, SSM_INNER, SSM_STATE), lambda i: (i, 0, 0))],
        out_shape=[jax.ShapeDtypeStruct((nrow, SSM_INNER), F32),
                   jax.ShapeDtypeStruct((nchunk, SSM_INNER, SSM_STATE), F32)],
        scratch_shapes=[pltpu.VMEM((SSM_INNER, SSM_STATE), F32), pltpu.VMEM((CHUNK, SSM_XBC), F32),
                        wide(), wide(), wide(), wide(), group(), group(), group(),
                        pltpu.VMEM((SUBLANES, SSM_HEADS), F32)],
        compiler_params=_params(("arbitrary",)),
    )(pre, dt, a_cum, a_flat, d_x, ind)


def _ssd_bwd(name, pre, dt, a_cum, a_flat, d_x, ind, ind_t, states, dy):
    nrow = pre.shape[0]
    tm = SSD_TM
    ntile = nrow // tm

    def body(pre_ref, dt_ref, ac_ref, af_ref, dx_ref, ind_ref, indt_ref, st_ref, dy_ref,
             dpre_ref, ddt_ref, da_ref, daf_ref, dd_ref,
             dh_ref, xbc_ref, dxbc_ref, ax_ref, dtx_ref, eax_ref, eex_ref, ra_ref, ts_ref, r2_ref,
             m_ref, l_ref, xd_ref, dm_ref, dxd_ref, fold_ref, hd_ref, tail_ref):
        @pl.when(pl.program_id(0) == 0)
        def _():
            dh_ref[...] = jnp.zeros_like(dh_ref)
            dd_ref[...] = jnp.zeros_like(dd_ref)

        causal = _causal_tiled()
        ind, ind_t = ind_ref[...], indt_ref[...]
        is_last_row = lax.broadcasted_iota(jnp.int32, (CHUNK, 1), 0) == CHUNK - 1
        ones = jnp.ones((CHUNK, SSM_STATE), BF16)

        def chunk(k, ddx):
            ci = SSD_CHUNKS - 1 - k
            rows = pl.ds(pl.multiple_of(ci * CHUNK, CHUNK), CHUNK)
            pre_v = pre_ref[rows, :]
            xbc_ref[...] = jax.nn.silu(pre_v)
            _chunk_decays(ci, dt_ref, ac_ref, ind, ax_ref, dtx_ref, eax_ref, eex_ref, tail_ref)
            ddx_parts = []
            for g in range(SSM_GROUPS):
                gcols = slice(g * HP, (g + 1) * HP)
                bcols = slice(B_OFF + g * SSM_STATE, B_OFF + (g + 1) * SSM_STATE)
                ccols = slice(C_OFF + g * SSM_STATE, C_OFF + (g + 1) * SSM_STATE)
                bm, cm, m_all, decay = _group_decay(ci, g, ax_ref, af_ref, xbc_ref, causal)
                m_ref[...] = m_all
                l_ref[...] = decay
                x_g = xbc_ref[:, gcols]
                xd = x_g * dtx_ref[:, gcols]
                xd_ref[...] = xd
                h_g = st_ref[ci, gcols, :]
                dh_g = dh_ref[gcols, :]
                dy_g = dy_ref[rows, gcols]
                for hh in range(SSM_HPG):
                    h = g * SSM_HPG + hh
                    hcols = slice(h * SSM_HEAD_DIM, (h + 1) * SSM_HEAD_DIM)
                    lc = slice(hh * SSM_HEAD_DIM, (hh + 1) * SSM_HEAD_DIM)
                    dy_h = dy_ref[rows, hcols]
                    dm_ref[:, lc] = _dot(dy_h, xd_ref[:, lc], 1, 1)
                    dxd_ref[:, lc] = _dot(m_ref[:, lc], dy_h, 0, 0)
                ebdh = eex_ref[:, gcols] * _dot(bm, dh_g, 1, 1)
                dxd = dxd_ref[...] + ebdh
                dm = dm_ref[...]
                t = dm * l_ref[...]
                t128 = (t[:, 0:LANES] + t[:, LANES:2 * LANES]) + (t[:, 2 * LANES:3 * LANES] + t[:, 3 * LANES:])
                fold_ref[...] = t128 + pltpu.roll(t128, CHUNK, axis=1)
                dw_sum = fold_ref[:, 0:CHUNK]
                q = dm * m_ref[...]
                dyea = dy_g * eax_ref[:, gcols]
                ra_ref[:, gcols] = q + dyea * _dot(cm, h_g, 1, 1)
                ts_ref[:, gcols] = xd * ebdh
                r2_ref[:, gcols] = dxd * x_g
                daf_ref[ci, :, gcols] = -jnp.sum(q, axis=0, keepdims=True)
                ddx_parts.append(jnp.sum(dy_g * x_g, axis=0, keepdims=True))
                dxbc_ref[:, gcols] = dxd * dtx_ref[:, gcols] + dx_ref[:, gcols] * dy_g
                dxbc_ref[:, ccols] = _dot(dw_sum, bm) + _dot(dyea, h_g)
                dxbc_ref[:, bcols] = _dot(dw_sum, cm, 0, 0) + _dot(xd * eex_ref[:, gcols], dh_g)
                dh_new = _dot(dyea, cm, 0, 0)
                for hh in range(SSM_HPG):
                    h = g * SSM_HPG + hh
                    hrows = slice(h * SSM_HEAD_DIM, (h + 1) * SSM_HEAD_DIM)
                    lrows = slice(hh * SSM_HEAD_DIM, (hh + 1) * SSM_HEAD_DIM)
                    hd_ref[h:h + 1, :] = jnp.sum(st_ref[ci, hrows, :] * dh_ref[hrows, :], axis=0, keepdims=True)
                    dh_ref[hrows, :] = tail_ref[SUBLANES - 1:SUBLANES, h:h + 1] * dh_ref[hrows, :] + dh_new[lrows, :]
            ra = _dot_exact(ra_ref[...], ind_t)
            ts = _dot_exact(ts_ref[...], ind_t)
            hdh = sum(lax.dot_general(ones, p, (((1,), (1,)), ((), ())), preferred_element_type=F32)
                      for p in _split3(hd_ref[...]))
            da_last = jnp.sum(ts, axis=0, keepdims=True) + tail_ref[SUBLANES - 1:SUBLANES, :] * hdh
            da_ref[rows, :] = ra - ts + jnp.where(is_last_row, da_last, 0.0)
            ddt_ref[rows, :] = _dot_exact(r2_ref[...], ind_t)
            sig = jax.nn.sigmoid(pre_v)
            dpre_ref[rows, :] = dxbc_ref[...] * (sig * (1.0 + pre_v * (1.0 - sig)))
            return ddx + jnp.concatenate(ddx_parts, axis=1)

        ddx = lax.fori_loop(0, SSD_CHUNKS, chunk, jnp.zeros((1, SSM_INNER), F32))
        dd_ref[...] += _dot_exact(jnp.broadcast_to(ddx, (SUBLANES, SSM_INNER)), ind_t)

    rev = lambda i: ntile - 1 - i
    whole = lambda a: pl.BlockSpec(a.shape, lambda i, nd=a.ndim: (0,) * nd)
    wide = lambda: pltpu.VMEM((CHUNK, SSM_INNER), F32)
    group = lambda: pltpu.VMEM((CHUNK, HP), F32)
    return pl.pallas_call(
        body, name=name, grid=(ntile,),
        in_specs=[pl.BlockSpec((tm, SSM_XBC), lambda i: (rev(i), 0)), pl.BlockSpec((tm, SSM_HEADS), lambda i: (rev(i), 0)),
                  pl.BlockSpec((tm, SSM_HEADS), lambda i: (rev(i), 0)),
                  pl.BlockSpec((SSD_CHUNKS, 1, SSM_INNER), lambda i: (rev(i), 0, 0)),
                  whole(d_x), whole(ind), whole(ind_t),
                  pl.BlockSpec((SSD_CHUNKS, SSM_INNER, SSM_STATE), lambda i: (rev(i), 0, 0)),
                  pl.BlockSpec((tm, SSM_INNER), lambda i: (rev(i), 0))],
        out_specs=[pl.BlockSpec((tm, SSM_XBC), lambda i: (rev(i), 0)), pl.BlockSpec((tm, SSM_HEADS), lambda i: (rev(i), 0)),
                   pl.BlockSpec((tm, SSM_HEADS), lambda i: (rev(i), 0)),
                   pl.BlockSpec((SSD_CHUNKS, 1, SSM_INNER), lambda i: (rev(i), 0, 0)),
                   pl.BlockSpec((SUBLANES, SSM_HEADS), lambda i: (0, 0))],
        out_shape=[jax.ShapeDtypeStruct((nrow, SSM_XBC), F32), jax.ShapeDtypeStruct((nrow, SSM_HEADS), F32),
                   jax.ShapeDtypeStruct((nrow, SSM_HEADS), F32), jax.ShapeDtypeStruct((nrow // CHUNK, 1, SSM_INNER), F32),
                   jax.ShapeDtypeStruct((SUBLANES, SSM_HEADS), F32)],
        scratch_shapes=[pltpu.VMEM((SSM_INNER, SSM_STATE), F32), pltpu.VMEM((CHUNK, SSM_XBC), F32),
                        pltpu.VMEM((CHUNK, SSM_XBC), F32), wide(), wide(), wide(), wide(), wide(), wide(), wide(),
                        group(), group(), group(), group(), group(), pltpu.VMEM((CHUNK, LANES), F32),
                        pltpu.VMEM((SSM_HEADS, SSM_STATE), F32), pltpu.VMEM((SUBLANES, SSM_HEADS), F32)],
        compiler_params=_params(("arbitrary",)),
    )(pre, dt, a_cum, a_flat, d_x, ind, ind_t, states, dy)


def _local_step(x, target, w):
    g = {}
    bs_col = w["gmlp_bs"].reshape(GMLP_GROUPS, GMLP_BLOCK, 1)
    b0, b1 = w["gate_bias"][0:1], w["gate_bias"][1:2]

    xn = _rms_fwd("mix_norm", x, w["mix_norm_w"])
    big = dict(bm=1024, bn=1024, bk=1024)
    gates = _mm("in_gates", xn, w["w_g"], **big)
    za = _mm("in_gmlp", xn, w["w_za"], **big)
    z = _mm("in_z", xn, w["w_z"], **big)
    xbc = _mm("in_xbc", xn, w["w_xbc"], **big)
    dt_raw = _mm("in_dt", xn, w["w_dt"], bm=1024, bn=SSM_HEADS, bk=1024)

    ya_pre = _gmlp_fwd("gmlp_fwd", za, w["gmlp_ln_w"], w["gmlp_ln_b"], w["gmlp_ws"], bs_col)
    y_a = _mm("proj_a", ya_pre, w["w_proj_a"], **big)

    pre = _conv_fwd("ssm_conv_fwd", xbc, w["ssm_conv_w"], w["ssm_conv_b"], tc=1024)
    dt, a_cum = _dt_prep("dt_prep", dt_raw, w["ssm_dt_bias"], w["ssm_a_log"])
    a_flat = jnp.transpose(a_cum.reshape(-1, CHUNK, SSM_HEADS), (0, 2, 1)).reshape(-1, 1, SSM_INNER)
    d_x = jnp.repeat(w["ssm_d"], SSM_HEAD_DIM, axis=1)
    ind = _head_indicator()
    y_ssd, states = _ssd_fwd("ssd_fwd", pre, dt, a_cum, a_flat, d_x, ind)
    yb_pre = _gate_norm_fwd("gate_norm_fwd", y_ssd, z, w["ssm_norm_w"])
    y_b = _mm("proj_b", yb_pre, w["w_proj_b"], **big)

    merged = _merge_fwd("merge_fwd", gates, y_a, y_b, b0, b1)
    h1 = _mm("out_proj", merged, w["w_out"], res=x, **big)

    hn = _rms_fwd("ffn_norm", h1, w["ffn_norm_w"])
    half = dict(bm=1024, bn=D_FF // 2, bk=1024)
    pg = _mm("ffn_up_gate", hn, w["w_up_g"], **half)
    pv = _mm("ffn_up_val", hn, w["w_up_v"], **half)
    cw, cb = w["ffn_conv_w"], w["ffn_conv_b"]
    gate, val, act = _ffn_act_fwd("ffn_act_fwd", pg, pv, cw[:, :D_FF], cw[:, D_FF:], cb[:, :D_FF], cb[:, D_FF:],
                                  tc=D_FF // 2)
    h2 = _mm("ffn_down", act, w["w_down"], res=h1, bm=512, bn=1024, bk=D_FF // 2)

    dh2, loss_part, g["final_norm_w"] = _final_loss("final_loss", h2, target, w["final_norm_w"].reshape(1, D_MODEL))

    dact = _mm("d_act", dh2, w["w_down"], tb=True, bm=1024, bn=D_FF // 2, bk=1024)
    wgrad = dict(ta=True, bk=512, out_dtype=BF16)
    g["w_down"] = _mm("dw_down", act, dh2, bm=D_FF // 2, bn=1024, **wgrad)
    dgate, dval = _ffn_act_bwd("ffn_act_bwd", dact, gate, val)
    dpg, dcwg, dcbg = _conv_bwd("ffn_conv_bwd_gate", dgate, pg, cw[:, :D_FF], tc=D_FF // 2)
    dpv, dcwv, dcbv = _conv_bwd("ffn_conv_bwd_val", dval, pv, cw[:, D_FF:], tc=D_FF // 2)
    g["ffn_conv_w"] = jnp.concatenate([dcwg, dcwv], axis=1)
    g["ffn_conv_b"] = jnp.concatenate([dcbg, dcbv], axis=1)
    back = dict(bm=1024, bn=1024, bk=D_FF // 2)
    dhn = _mm("d_hn_gate", dpg, w["w_up_g"], tb=True, **back)
    dhn = _mm("d_hn_val", dpv, w["w_up_v"], tb=True, res=dhn, **back)
    g["w_up_g"] = _mm("dw_up_gate", hn, dpg, bm=1024, bn=D_FF // 2, **wgrad)
    g["w_up_v"] = _mm("dw_up_val", hn, dpv, bm=1024, bn=D_FF // 2, **wgrad)
    dh1, g["ffn_norm_w"] = _rms_bwd("ffn_norm_bwd", h1, w["ffn_norm_w"], dhn, dh2)

    dmerged = _mm("d_merged", dh1, w["w_out"], tb=True, **big)
    g["w_out"] = _mm("dw_out", merged, dh1, bm=1024, bn=1024, **wgrad)
    dgates, dya, dyb, db0, db1 = _merge_bwd("merge_bwd", gates, y_a, y_b, dmerged, b0, b1)
    g["gate_bias"] = jnp.concatenate([db0, db1], axis=0)

    dya_pre = _mm("d_ya_pre", dya, w["w_proj_a"], tb=True, **big)
    g["w_proj_a"] = _mm("dw_proj_a", ya_pre, dya, bm=1024, bn=1024, **wgrad)
    dza, g["gmlp_ln_w"], g["gmlp_ln_b"], g["gmlp_ws"], dbs = _gmlp_bwd(
        "gmlp_bwd", za, dya_pre, w["gmlp_ln_w"], w["gmlp_ln_b"], w["gmlp_ws"], bs_col)
    g["gmlp_bs"] = dbs.reshape(GMLP_GROUPS, GMLP_BLOCK)

    dyb_pre = _mm("d_yb_pre", dyb, w["w_proj_b"], tb=True, **big)
    g["w_proj_b"] = _mm("dw_proj_b", yb_pre, dyb, bm=1024, bn=1024, **wgrad)
    dy_ssd, dz, g["ssm_norm_w"] = _gate_norm_bwd("gate_norm_bwd", y_ssd, z, dyb_pre, w["ssm_norm_w"])
    dpre, ddt, da_tok, da_flat, dd = _ssd_bwd("ssd_bwd", pre, dt, a_cum, a_flat, d_x, ind, ind.T, states, dy_ssd)
    g["ssm_d"] = dd[0:1]
    da_src = jnp.transpose(da_flat.reshape(-1, SSM_HEADS, CHUNK), (0, 2, 1)).reshape(-1, SSM_HEADS)
    ddt_raw, g["ssm_dt_bias"], g["ssm_a_log"] = _dt_bwd("dt_bwd", dt_raw, ddt, da_tok, da_src,
                                                         w["ssm_dt_bias"], w["ssm_a_log"])
    dxbc, g["ssm_conv_w"], g["ssm_conv_b"] = _conv_bwd("ssm_conv_bwd", dpre, xbc, w["ssm_conv_w"], tc=1024)

    dxn = _mm("d_xn_gates", dgates, w["w_g"], tb=True, **big)
    dxn = _mm("d_xn_gmlp", dza, w["w_za"], tb=True, res=dxn, **big)
    dxn = _mm("d_xn_z", dz, w["w_z"], tb=True, res=dxn, **big)
    dxn = _mm("d_xn_xbc", dxbc, w["w_xbc"], tb=True, res=dxn, **big)
    dxn = _mm("d_xn_dt", ddt_raw, w["w_dt"], tb=True, res=dxn, bm=1024, bn=1024, bk=SSM_HEADS)
    g["w_g"] = _mm("dw_gates", xn, dgates, bm=1024, bn=1024, **wgrad)
    g["w_za"] = _mm("dw_gmlp", xn, dza, bm=1024, bn=1024, **wgrad)
    g["w_z"] = _mm("dw_z", xn, dz, bm=1024, bn=1024, **wgrad)
    g["w_xbc"] = _mm("dw_xbc", xn, dxbc, bm=1024, bn=1024, **wgrad)
    g["w_dt"] = _mm("dw_dt", xn, ddt_raw, bm=1024, bn=SSM_HEADS, **wgrad)
    grad_x, g["mix_norm_w"] = _rms_bwd("mix_norm_bwd", x, w["mix_norm_w"], dxn, dh1)
    return loss_part, grad_x, g


def _position():
    return lax.axis_index("x"), lax.axis_index("y"), lax.axis_index("c")


def _exchange_chips(name, arrs, scatter):
    n = len(arrs)
    shapes = [a.shape if scatter else (N_CHIPS,) + a.shape for a in arrs]

    def body(*refs):
        ins, outs = refs[:n], refs[n:2 * n]
        send_sems, recv_sems, local_sems = refs[2 * n:]
        x, y, c = _position()
        me = 2 * x + y
        peers = [(1 - x, y), (x, 1 - y), (1 - x, 1 - y)]
        local, sends = [], []
        for i in range(n):
            cp = pltpu.make_async_copy(ins[i].at[me] if scatter else ins[i], outs[i].at[me], local_sems.at[i])
            cp.start()
            local.append(cp)
        for i in range(n):
            for k, (px, py) in enumerate(peers):
                cp = pltpu.make_async_remote_copy(
                    src_ref=ins[i].at[2 * px + py] if scatter else ins[i], dst_ref=outs[i].at[me],
                    send_sem=send_sems.at[i, k], recv_sem=recv_sems.at[i, k],
                    device_id=(px, py, c), device_id_type=MESH)
                cp.start()
                sends.append(cp)
        for i in range(n):
            for k, (px, py) in enumerate(peers):
                pltpu.make_async_remote_copy(
                    src_ref=ins[i].at[me] if scatter else ins[i], dst_ref=outs[i].at[2 * px + py],
                    send_sem=send_sems.at[i, k], recv_sem=recv_sems.at[i, k],
                    device_id=(px, py, c), device_id_type=MESH).wait_recv()
        for cp in sends:
            cp.wait_send()
        for cp in local:
            cp.wait()

    hbm = pl.BlockSpec(memory_space=pl.ANY)
    return pl.pallas_call(
        body, name=name,
        in_specs=[hbm] * n, out_specs=[hbm] * n,
        out_shape=[jax.ShapeDtypeStruct(s, a.dtype) for s, a in zip(shapes, arrs)],
        scratch_shapes=[pltpu.SemaphoreType.DMA((n, N_CHIPS - 1)), pltpu.SemaphoreType.DMA((n, N_CHIPS - 1)),
                        pltpu.SemaphoreType.DMA((n,))],
        compiler_params=pltpu.CompilerParams(has_side_effects=True),
    )(*arrs)


def _half_rows(ref_rows, which):
    half = ref_rows // 2
    return pl.ds(pl.multiple_of(which * half, 2 * SUBLANES), half)


def _gather_chips_split(name, split, whole):
    ns, nw = len(split), len(whole)
    arrs = list(split) + list(whole)
    n = ns + nw

    def body(*refs):
        ins, outs = refs[:n], refs[n:2 * n]
        send_sems, recv_sems, local_sems = refs[2 * n:]
        x, y, c = _position()
        me = 2 * x + y
        sibling = (x, y, 1 - c)
        chips = [(1 - x, y), (x, 1 - y), (1 - x, 1 - y)]

        def remote(i, k, src, dst, to):
            return pltpu.make_async_remote_copy(src_ref=src, dst_ref=dst, send_sem=send_sems.at[i, k],
                                                recv_sem=recv_sems.at[i, k], device_id=to, device_id_type=MESH)

        local, sends = [], []
        for i in range(n):
            cp = pltpu.make_async_copy(ins[i], outs[i].at[me], local_sems.at[i])
            cp.start()
            local.append(cp)
        for i in range(n):
            rows = _half_rows(arrs[i].shape[0], c) if i < ns else slice(None)
            for k, (px, py) in enumerate(chips):
                cp = remote(i, k, ins[i].at[rows], outs[i].at[me, rows], (px, py, c))
                cp.start()
                sends.append(cp)
        for i in range(ns):
            rows = _half_rows(arrs[i].shape[0], c)
            for k, (px, py) in enumerate(chips):
                landed = outs[i].at[2 * px + py, rows]
                remote(i, k, landed, landed, (px, py, c)).wait_recv()
                cp = remote(i, N_CHIPS - 1 + k, landed, landed, sibling)
                cp.start()
                sends.append(cp)
        for i in range(ns, n):
            for k, (px, py) in enumerate(chips):
                landed = outs[i].at[2 * px + py]
                remote(i, k, landed, landed, (px, py, c)).wait_recv()
        for i in range(ns):
            rows = _half_rows(arrs[i].shape[0], 1 - c)
            for k, (px, py) in enumerate(chips):
                landed = outs[i].at[2 * px + py, rows]
                remote(i, N_CHIPS - 1 + k, landed, landed, sibling).wait_recv()
        for cp in sends:
            cp.wait_send()
        for cp in local:
            cp.wait()

    hbm = pl.BlockSpec(memory_space=pl.ANY)
    nsem = 2 * (N_CHIPS - 1)
    return pl.pallas_call(
        body, name=name, in_specs=[hbm] * n, out_specs=[hbm] * n,
        out_shape=[jax.ShapeDtypeStruct((N_CHIPS,) + a.shape, a.dtype) for a in arrs],
        scratch_shapes=[pltpu.SemaphoreType.DMA((n, nsem)), pltpu.SemaphoreType.DMA((n, nsem)),
                        pltpu.SemaphoreType.DMA((n,))],
        compiler_params=pltpu.CompilerParams(has_side_effects=True),
    )(*arrs)


def _split_to_cores(name, stacks):
    n = len(stacks)

    def body(*refs):
        ins, mine, theirs = refs[:n], refs[n:2 * n], refs[2 * n:3 * n]
        send_sems, recv_sems, local_sems = refs[3 * n:]
        x, y, c = _position()
        local, sends = [], []
        for i in range(n):
            nrow = stacks[i].shape[1]
            cp = pltpu.make_async_copy(ins[i].at[:, _half_rows(nrow, c)], mine[i], local_sems.at[i])
            cp.start()
            local.append(cp)
            cp = pltpu.make_async_remote_copy(src_ref=ins[i].at[:, _half_rows(nrow, 1 - c)], dst_ref=theirs[i],
                                              send_sem=send_sems.at[i], recv_sem=recv_sems.at[i],
                                              device_id=(x, y, 1 - c), device_id_type=MESH)
            cp.start()
            sends.append(cp)
        for cp in sends:
            cp.wait_recv()
        for cp in sends:
            cp.wait_send()
        for cp in local:
            cp.wait()

    hbm = pl.BlockSpec(memory_space=pl.ANY)
    halves = [jax.ShapeDtypeStruct((a.shape[0], a.shape[1] // 2, a.shape[2]), a.dtype) for a in stacks]
    res = pl.pallas_call(
        body, name=name, in_specs=[hbm] * n, out_specs=[hbm] * (2 * n), out_shape=halves + halves,
        scratch_shapes=[pltpu.SemaphoreType.DMA((n,)), pltpu.SemaphoreType.DMA((n,)), pltpu.SemaphoreType.DMA((n,))],
        compiler_params=pltpu.CompilerParams(has_side_effects=True),
    )(*stacks)
    return res[:n], res[n:]


def _join_cores(name, halves):
    n = len(halves)

    def body(*refs):
        ins, outs = refs[:n], refs[n:2 * n]
        send_sems, recv_sems, local_sems = refs[2 * n:]
        x, y, c = _position()
        local, sends = [], []
        for i in range(n):
            nrow = 2 * halves[i].shape[0]
            cp = pltpu.make_async_copy(ins[i], outs[i].at[_half_rows(nrow, c)], local_sems.at[i])
            cp.start()
            local.append(cp)
            cp = pltpu.make_async_remote_copy(src_ref=ins[i], dst_ref=outs[i].at[_half_rows(nrow, c)],
                                              send_sem=send_sems.at[i], recv_sem=recv_sems.at[i],
                                              device_id=(x, y, 1 - c), device_id_type=MESH)
            cp.start()
            sends.append(cp)
        for i in range(n):
            nrow = 2 * halves[i].shape[0]
            pltpu.make_async_remote_copy(src_ref=ins[i], dst_ref=outs[i].at[_half_rows(nrow, 1 - c)],
                                         send_sem=send_sems.at[i], recv_sem=recv_sems.at[i],
                                         device_id=(x, y, 1 - c), device_id_type=MESH).wait_recv()
        for cp in sends:
            cp.wait_send()
        for cp in local:
            cp.wait()

    hbm = pl.BlockSpec(memory_space=pl.ANY)
    return pl.pallas_call(
        body, name=name, in_specs=[hbm] * n, out_specs=[hbm] * n,
        out_shape=[jax.ShapeDtypeStruct((2 * a.shape[0], a.shape[1]), a.dtype) for a in halves],
        scratch_shapes=[pltpu.SemaphoreType.DMA((n,)), pltpu.SemaphoreType.DMA((n,)), pltpu.SemaphoreType.DMA((n,))],
        compiler_params=pltpu.CompilerParams(has_side_effects=True),
    )(*halves)


def _all_reduce(name, pack):
    def body(in_ref, out_ref, buf, send_sems, recv_sems):
        x, y, c = _position()
        me = 4 * x + 2 * y + c
        flips = [(dx, dy, dc) for dx in (0, 1) for dy in (0, 1) for dc in (0, 1) if (dx, dy, dc) != (0, 0, 0)]
        peers = [((1 - x) if dx else x, (1 - y) if dy else y, (1 - c) if dc else c) for dx, dy, dc in flips]
        buf[me] = in_ref[...]
        sends = []
        for k, peer in enumerate(peers):
            cp = pltpu.make_async_remote_copy(src_ref=in_ref, dst_ref=buf.at[me], send_sem=send_sems.at[k],
                                              recv_sem=recv_sems.at[k], device_id=peer, device_id_type=MESH)
            cp.start()
            sends.append(cp)
        for k, (px, py, pc) in enumerate(peers):
            pltpu.make_async_remote_copy(src_ref=in_ref, dst_ref=buf.at[4 * px + 2 * py + pc], send_sem=send_sems.at[k],
                                         recv_sem=recv_sems.at[k], device_id=(px, py, pc), device_id_type=MESH).wait_recv()
        total = buf[0]
        for j in range(1, N_DEV):
            total = total + buf[j]
        out_ref[...] = total
        for cp in sends:
            cp.wait_send()

    vmem = pl.BlockSpec(memory_space=pltpu.VMEM)
    return pl.pallas_call(
        body, name=name, in_specs=[vmem], out_specs=vmem,
        out_shape=jax.ShapeDtypeStruct(pack.shape, F32),
        scratch_shapes=[pltpu.VMEM((N_DEV,) + pack.shape, F32), pltpu.SemaphoreType.DMA((N_DEV - 1,)),
                        pltpu.SemaphoreType.DMA((N_DEV - 1,))],
        compiler_params=pltpu.CompilerParams(has_side_effects=True, vmem_limit_bytes=VMEM_LIMIT_V7X),
    )(pack)


PACK_UNIT = SUBLANES * LANES


def _pack(arrs):
    flat = []
    for a in arrs:
        v = a.reshape(-1).astype(F32)
        flat.append(jnp.pad(v, (0, -v.size % PACK_UNIT)))
    return jnp.concatenate(flat).reshape(-1, LANES)


def _unpack(pack, shapes):
    flat = pack.reshape(-1)
    out, off = [], 0
    for s in shapes:
        size = 1
        for d in s:
            size *= d
        out.append(flat[off:off + size].reshape(s))
        off += size + (-size % PACK_UNIT)
    return out


SMALL = ["mix_norm_w", "gate_bias", "gmlp_ln_w", "gmlp_ln_b", "gmlp_ws", "gmlp_bs", "ssm_conv_w", "ssm_conv_b",
         "ssm_dt_bias", "ssm_a_log", "ssm_d", "ssm_norm_w", "ffn_norm_w", "ffn_conv_w", "ffn_conv_b", "final_norm_w"]
SMALL_SHARDED = ("gate_bias", "ssm_conv_w", "ffn_conv_w")
BIG = ["w_in", "w_proj_a", "w_proj_b", "w_out", "ffn_w_up", "ffn_w_down"]
WEIGHTS = ["mix_norm_w", "w_in", "gate_bias", "gmlp_ln_w", "gmlp_ln_b", "gmlp_ws", "gmlp_bs", "ssm_conv_w",
           "ssm_conv_b", "ssm_dt_bias", "ssm_a_log", "ssm_d", "ssm_norm_w", "w_proj_a", "w_proj_b", "w_out",
           "ffn_norm_w", "ffn_w_up", "ffn_conv_w", "ffn_conv_b", "ffn_w_down", "final_norm_w"]
IN_SPLITS = [0, 2048, 4096, 6144, 9216, 9248]


def _columns_from_chips(stack):
    return jnp.transpose(stack, (1, 0, 2)).reshape(stack.shape[1], -1)


def _columns_to_chips(full, parts=N_CHIPS):
    rows, cols = full.shape
    return jnp.transpose(full.reshape(rows, parts, cols // parts), (1, 0, 2))


def kernel(x, mix_norm_w, w_in, gate_bias, gmlp_ln_w, gmlp_ln_b, gmlp_ws, gmlp_bs, ssm_conv_w, ssm_conv_b, ssm_dt_bias, ssm_a_log, ssm_d, ssm_norm_w, w_proj_a, w_proj_b, w_out, ffn_norm_w, ffn_w_up, ffn_conv_w, ffn_conv_b, ffn_w_down, final_norm_w, loss_target, m_mix_norm_w, m_w_in, m_gate_bias, m_gmlp_ln_w, m_gmlp_ln_b, m_gmlp_ws, m_gmlp_bs, m_ssm_conv_w, m_ssm_conv_b, m_ssm_dt_bias, m_ssm_a_log, m_ssm_d, m_ssm_norm_w, m_w_proj_a, m_w_proj_b, m_w_out, m_ffn_norm_w, m_ffn_w_up, m_ffn_conv_w, m_ffn_conv_b, m_ffn_w_down, m_final_norm_w, v_mix_norm_w, v_w_in, v_gate_bias, v_gmlp_ln_w, v_gmlp_ln_b, v_gmlp_ws, v_gmlp_bs, v_ssm_conv_w, v_ssm_conv_b, v_ssm_dt_bias, v_ssm_a_log, v_ssm_d, v_ssm_norm_w, v_w_proj_a, v_w_proj_b, v_w_out, v_ffn_norm_w, v_ffn_w_up, v_ffn_conv_w, v_ffn_conv_b, v_ffn_w_down, v_final_norm_w):
    args = dict(locals())
    weights = {n: args[n] for n in WEIGHTS}
    moments_m = {n: args["m_" + n] for n in WEIGHTS}
    moments_v = {n: args["v_" + n] for n in WEIGHTS}
    chip = 2 * lax.axis_index("x") + lax.axis_index("y")

    gathered = _gather_chips_split("gather_weights", [weights[n][0].astype(BF16) for n in BIG],
                                   [weights[n][0] for n in SMALL_SHARDED])
    w_in_s, w_pa_s, w_pb_s, w_out_s, w_up_s, w_down_s, gb_s, scw_s, fcw_s = gathered
    w_in_full = _columns_from_chips(w_in_s)
    full = {"w_" + nm: w_in_full[:, IN_SPLITS[k]:IN_SPLITS[k + 1]] for k, nm in enumerate(["g", "za", "z", "xbc", "dt"])}
    full["w_proj_a"] = w_pa_s.reshape(-1, D_MODEL)
    full["w_proj_b"] = w_pb_s.reshape(-1, D_MODEL)
    full["w_out"] = w_out_s.reshape(-1, D_MODEL)
    full["w_down"] = w_down_s.reshape(-1, D_MODEL)
    full["w_up_g"] = _columns_from_chips(w_up_s[:2])
    full["w_up_v"] = _columns_from_chips(w_up_s[2:])
    full["gate_bias"] = _columns_from_chips(gb_s)
    full["ssm_conv_w"] = _columns_from_chips(scw_s)
    full["ffn_conv_w"] = _columns_from_chips(fcw_s)
    for n in SMALL:
        if n not in SMALL_SHARDED:
            full[n] = weights[n] if n == "final_norm_w" else weights[n][0]
    for n in ("mix_norm_w", "ffn_norm_w", "ssm_conv_b", "ssm_dt_bias", "ssm_a_log", "ssm_d", "ssm_norm_w", "ffn_conv_b"):
        full[n] = full[n].reshape(1, -1)

    loss_part, grad_x, g = _local_step(x[0], loss_target[0], full)

    small_shapes = [(1, LANES)] + [g[n].shape for n in SMALL]
    reduced = _unpack(_all_reduce("reduce_small", _pack([loss_part] + [g[n] for n in SMALL])), small_shapes)
    loss = reduced[0][0, 0]
    small_grads = {}
    for n, r in zip(SMALL, reduced[1:]):
        if n in SMALL_SHARDED:
            width = weights[n].shape[2]
            r = lax.dynamic_slice_in_dim(r, chip * width, width, axis=1)
        small_grads[n] = r.reshape(weights[n].shape)
    packs = [_pack([d[n] for n in SMALL]) for d in (weights, moments_m, moments_v, small_grads)]
    upd = _adamw("adamw_small", packs[0], [packs[3]], packs[1], packs[2], tm=packs[0].shape[0], rs=SUBLANES)
    small_out = [_unpack(u, [weights[n].shape for n in SMALL]) for u in upd]

    dw_in = jnp.concatenate([g["w_g"], g["w_za"], g["w_z"], g["w_xbc"], g["w_dt"]], axis=1)
    stacks = [
        _columns_to_chips(dw_in),
        g["w_proj_a"].reshape(N_CHIPS, -1, D_MODEL),
        g["w_proj_b"].reshape(N_CHIPS, -1, D_MODEL),
        g["w_out"].reshape(N_CHIPS, -1, D_MODEL),
        jnp.concatenate([_columns_to_chips(g["w_up_g"], 2), _columns_to_chips(g["w_up_v"], 2)], axis=0),
        g["w_down"].reshape(N_CHIPS, -1, D_MODEL),
    ]
    half_tiles = {"w_in": 128, "w_proj_a": 128, "w_proj_b": 256, "w_out": 128, "ffn_w_up": 128, "ffn_w_down": 176}
    tiles = {"w_in": 128, "w_proj_a": 256, "w_proj_b": 256, "w_out": 256, "ffn_w_up": 128, "ffn_w_down": 176}
    own_half, other_half = _split_to_cores("split_grads", stacks)
    pair = [_pair_sum("pair_" + n, a, b, tm=half_tiles[n]) for n, a, b in zip(BIG, own_half, other_half)]
    received = _exchange_chips("scatter_grads", pair, scatter=True)
    halves = [_sum_slots("sum_" + n, r, tm=half_tiles[n], rs=2 * SUBLANES) for n, r in zip(BIG, received)]
    reduced = _join_cores("join_grads", halves)
    big_out = {}
    for n, grad in zip(BIG, reduced):
        big_out[n] = _adamw("adamw_" + n, weights[n][0], [grad], moments_m[n][0], moments_v[n][0],
                            tm=tiles[n], rs=SUBLANES)

    per_kind = [[], [], [], []]
    for n in WEIGHTS:
        for kind in range(4):
            if n in big_out:
                per_kind[kind].append(big_out[n][kind].reshape(weights[n].shape))
            else:
                per_kind[kind].append(small_out[kind][SMALL.index(n)])
    return (loss, grad_x[None], *per_kind[0], *per_kind[1], *per_kind[2], *per_kind[3])
```

```python
import jax
import jax.numpy as jnp
from jax import lax
from jax.experimental import pallas as pl
from jax.experimental.pallas import tpu as pltpu

F32 = jnp.float32
BF16 = jnp.bfloat16
MESH = pl.DeviceIdType.MESH

EPS = 1e-5
D_MODEL = 1024
GMLP_BLOCK = 128
GMLP_GROUPS = 8
CHUNK = 64
SSM_INNER = 2048
SSM_HEADS = 32
SSM_HEAD_DIM = 64
SSM_GROUPS = 4
SSM_HPG = 8
SSM_STATE = 128
SSM_CONV = 4
SSM_XBC = 3072
D_FF = 2816
FFN_CONV = 3
N_CHIPS = 4
N_DEV = 8

ADAM_LR = 0.001
ADAM_B1 = 0.9
ADAM_B2 = 0.999
ADAM_EPS = 1e-08
ADAM_WD = 0.01
ADAM_STEP = 10

VMEM_LIMIT_V7X = 56 * 1024 * 1024
SUBLANES = 8
LANES = 128


def _params(sem=None):
    return pltpu.CompilerParams(dimension_semantics=sem, vmem_limit_bytes=VMEM_LIMIT_V7X)


def _dot(a, b, ca=1, cb=0):
    return lax.dot_general(a.astype(BF16), b.astype(BF16), (((ca,), (cb,)), ((), ())),
                           preferred_element_type=F32)


def _mm(name, a, b, *, ta=False, tb=False, out_dtype=F32, bm, bn, bk, res=None):
    m, k = (a.shape[1], a.shape[0]) if ta else a.shape
    k2, n = (b.shape[1], b.shape[0]) if tb else b.shape
    assert k == k2 and m % bm == 0 and n % bn == 0 and k % bk == 0, (name, a.shape, b.shape)
    nk = k // bk
    a_spec = (pl.BlockSpec((bk, bm), lambda i, j, kk: (kk, i)) if ta
              else pl.BlockSpec((bm, bk), lambda i, j, kk: (i, kk)))
    b_spec = (pl.BlockSpec((bn, bk), lambda i, j, kk: (j, kk)) if tb
              else pl.BlockSpec((bk, bn), lambda i, j, kk: (kk, j)))
    o_spec = pl.BlockSpec((bm, bn), lambda i, j, kk: (i, j))
    has_res = res is not None

    def body(*refs):
        a_ref, b_ref = refs[0], refs[1]
        r_ref = refs[2] if has_res else None
        o_ref = refs[3] if has_res else refs[2]
        p = _dot(a_ref[...], b_ref[...], 0 if ta else 1, 1 if tb else 0)

        def finish(total):
            if has_res:
                total = total + r_ref[...]
            o_ref[...] = total.astype(out_dtype)

        if nk == 1:
            finish(p)
        else:
            acc_ref = refs[-1]
            kk = pl.program_id(2)

            @pl.when(kk == 0)
            def _():
                acc_ref[...] = p

            @pl.when(kk > 0)
            def _():
                acc_ref[...] += p

            @pl.when(kk == nk - 1)
            def _():
                finish(acc_ref[...])

    return pl.pallas_call(
        body, name=name,
        grid=(m // bm, n // bn, nk),
        in_specs=[a_spec, b_spec] + ([o_spec] if has_res else []),
        out_specs=o_spec,
        out_shape=jax.ShapeDtypeStruct((m, n), out_dtype),
        scratch_shapes=[pltpu.VMEM((bm, bn), F32)] if nk > 1 else [],
        compiler_params=_params(("parallel", "parallel", "arbitrary")),
    )(*([a, b] + ([res] if has_res else [])))


def _rows(name, fn, ins, params, outs, accs, *, tm, rs):
    nrow = ins[0][0].shape[-2]
    assert nrow % tm == 0 and tm % rs == 0, (name, nrow, tm, rs)
    n_in, n_p, n_out, n_acc = len(ins), len(params), len(outs), len(accs)
    in_specs = []
    for spec in ins:
        arr, width, cb = spec[:3]
        if len(spec) == 4:
            in_specs.append(pl.BlockSpec((None, tm, width), lambda i, cb=cb, lead=spec[3]: (lead, i, cb)))
        else:
            in_specs.append(pl.BlockSpec((tm, width), lambda i, cb=cb: (i, cb)))
    for p in params:
        in_specs.append(pl.BlockSpec(p.shape, lambda i, nd=p.ndim: (0,) * nd))
    out_specs = [pl.BlockSpec((tm, w), lambda i: (i, 0)) for w, _ in outs]
    out_specs += [pl.BlockSpec(s, lambda i: (0, 0)) for s in accs]
    out_shape = [jax.ShapeDtypeStruct((nrow, w), dt) for w, dt in outs]
    out_shape += [jax.ShapeDtypeStruct(s, F32) for s in accs]

    def body(*refs):
        in_refs = refs[:n_in]
        p_refs = refs[n_in:n_in + n_p]
        o_refs = refs[n_in + n_p:n_in + n_p + n_out]
        a_refs = refs[n_in + n_p + n_out:]
        pv = [p[...] for p in p_refs]

        if n_acc:
            @pl.when(pl.program_id(0) == 0)
            def _():
                for a_ref in a_refs:
                    a_ref[...] = jnp.zeros_like(a_ref)

        def step(r, carry):
            sl = pl.ds(pl.multiple_of(r * rs, rs), rs)
            vals = [ref[sl, :] for ref in in_refs]
            row_out, sums = fn(*vals, *pv)
            for o_ref, v in zip(o_refs, row_out):
                o_ref[sl, :] = v.astype(o_ref.dtype)
            return tuple(c + s for c, s in zip(carry, sums))

        init = tuple(jnp.zeros(s, F32) for s in accs)
        total = lax.fori_loop(0, tm // rs, step, init)
        for a_ref, t in zip(a_refs, total):
            a_ref[...] += t

    res = pl.pallas_call(
        body, name=name, grid=(nrow // tm,),
        in_specs=in_specs, out_specs=out_specs, out_shape=out_shape,
        compiler_params=_params(("arbitrary",)),
    )(*([s[0] for s in ins] + list(params)))
    return res


def _rms(x, w):
    return x * lax.rsqrt(jnp.mean(x * x, axis=-1, keepdims=True) + EPS) * w


def _colsum(v):
    return jnp.sum(v, axis=0, keepdims=True)


def _rms_fwd(name, x, w):
    def fn(xv, wv):
        return (_rms(xv, wv),), ()
    return _rows(name, fn, [(x, D_MODEL, 0)], [w], [(D_MODEL, BF16)], [], tm=512, rs=16)[0]


def _rms_bwd(name, x, w, dy, dres):
    def fn(xv, dyv, drv, wv):
        _, vjp = jax.vjp(_rms, xv, wv)
        dx, dw = vjp(dyv)
        return (drv + dx,), (dw,)
    return _rows(name, fn, [(x, D_MODEL, 0), (dy, D_MODEL, 0), (dres, D_MODEL, 0)], [w],
                 [(D_MODEL, F32)], [(1, D_MODEL)], tm=512, rs=16)


def _final_loss(name, h, target, w):
    def fn(hv, tv, wv):
        y, vjp = jax.vjp(_rms, hv, wv)
        err = y - tv
        part = 0.5 * jnp.sum(jnp.mean(err * err, axis=-1, keepdims=True), axis=0, keepdims=True)
        dh, dw = vjp(err / D_MODEL)
        return (dh,), (jnp.broadcast_to(part, (1, LANES)), dw)
    return _rows(name, fn, [(h, D_MODEL, 0), (target, D_MODEL, 0)], [w],
                 [(D_MODEL, F32)], [(1, LANES), (1, D_MODEL)], tm=512, rs=16)


def _merge(ga, gb, ya, yb, b0, b1):
    return jax.nn.sigmoid(ga + b0) * ya + jax.nn.sigmoid(gb + b1) * yb


def _merge_fwd(name, g, ya, yb, b0, b1):
    def fn(ga, gb, yav, ybv, b0v, b1v):
        return (_merge(ga, gb, yav, ybv, b0v, b1v),), ()
    return _rows(name, fn, [(g, D_MODEL, 0), (g, D_MODEL, 1), (ya, D_MODEL, 0), (yb, D_MODEL, 0)],
                 [b0, b1], [(D_MODEL, BF16)], [], tm=512, rs=16)[0]


def _merge_bwd(name, g, ya, yb, dm, b0, b1):
    def fn(ga, gb, yav, ybv, dmv, b0v, b1v):
        _, vjp = jax.vjp(_merge, ga, gb, yav, ybv, b0v, b1v)
        dga, dgb, dya, dyb, db0, db1 = vjp(dmv)
        return (jnp.concatenate([dga, dgb], axis=1), dya, dyb), (db0, db1)
    return _rows(name, fn,
                 [(g, D_MODEL, 0), (g, D_MODEL, 1), (ya, D_MODEL, 0), (yb, D_MODEL, 0), (dm, D_MODEL, 0)],
                 [b0, b1], [(2 * D_MODEL, BF16), (D_MODEL, BF16), (D_MODEL, BF16)],
                 [(1, D_MODEL), (1, D_MODEL)], tm=512, rs=16)


GROUP_W = SSM_INNER // SSM_GROUPS


def _gate_norm_group(y, z, nw):
    v = y * jax.nn.silu(z)
    return v * lax.rsqrt(jnp.mean(v * v, axis=-1, keepdims=True) + EPS) * nw


def _gate_norm_fwd(name, y, z, nw):
    def fn(yv, zv, nwv):
        parts = [_gate_norm_group(yv[:, k * GROUP_W:(k + 1) * GROUP_W], zv[:, k * GROUP_W:(k + 1) * GROUP_W],
                                  nwv[:, k * GROUP_W:(k + 1) * GROUP_W]) for k in range(SSM_GROUPS)]
        return (jnp.concatenate(parts, axis=1),), ()
    return _rows(name, fn, [(y, SSM_INNER, 0), (z, SSM_INNER, 0)], [nw], [(SSM_INNER, BF16)], [],
                 tm=512, rs=16)[0]


def _gate_norm_bwd(name, y, z, dout, nw):
    def fn(yv, zv, dv, nwv):
        dys, dzs, dns = [], [], []
        for k in range(SSM_GROUPS):
            sl = slice(k * GROUP_W, (k + 1) * GROUP_W)
            _, vjp = jax.vjp(_gate_norm_group, yv[:, sl], zv[:, sl], nwv[:, sl])
            dy, dz, dn = vjp(dv[:, sl])
            dys.append(dy), dzs.append(dz), dns.append(dn)
        return (jnp.concatenate(dys, axis=1), jnp.concatenate(dzs, axis=1)), (jnp.concatenate(dns, axis=1),)
    return _rows(name, fn, [(y, SSM_INNER, 0), (z, SSM_INNER, 0), (dout, SSM_INNER, 0)], [nw],
                 [(SSM_INNER, F32), (SSM_INNER, BF16)], [(1, SSM_INNER)], tm=512, rs=16)


def _softplus(v):
    return jnp.maximum(v, 0.0) + jnp.log1p(jnp.exp(-jnp.abs(v)))


def _chunk_cumsum(v, reverse=False):
    row = lax.broadcasted_iota(jnp.int32, v.shape, 0)
    step = 1
    while step < CHUNK:
        if reverse:
            shifted = pltpu.roll(v, CHUNK - step, axis=0)
            v = v + jnp.where(row < CHUNK - step, shifted, 0.0)
        else:
            shifted = pltpu.roll(v, step, axis=0)
            v = v + jnp.where(row >= step, shifted, 0.0)
        step *= 2
    return v


def _dt_prep(name, dt_raw, dt_bias, a_log):
    def fn(rv, bv, alv):
        dt = _softplus(rv + bv)
        return (dt, _chunk_cumsum(dt * (-jnp.exp(alv)))), ()
    return _rows(name, fn, [(dt_raw, SSM_HEADS, 0)], [dt_bias, a_log],
                 [(SSM_HEADS, F32), (SSM_HEADS, F32)], [], tm=512, rs=CHUNK)


def _dt_bwd(name, dt_raw, ddt, da1, da2, dt_bias, a_log):
    def fn(rv, ddv, d1, d2, bv, alv):
        pre = rv + bv
        dt = _softplus(pre)
        a_neg = -jnp.exp(alv)
        back = _chunk_cumsum(d1 + d2, reverse=True)
        d_dt = ddv + back * a_neg
        d_raw = d_dt * jax.nn.sigmoid(pre)
        return (d_raw,), (_colsum(d_raw), _colsum(back * dt) * a_neg)
    return _rows(name, fn, [(dt_raw, SSM_HEADS, 0), (ddt, SSM_HEADS, 0), (da1, SSM_HEADS, 0), (da2, SSM_HEADS, 0)],
                 [dt_bias, a_log], [(SSM_HEADS, BF16)], [(1, SSM_HEADS), (1, SSM_HEADS)], tm=512, rs=CHUNK)


def _adamw(name, w, g_parts, m, v, *, tm, rs):
    width = w.shape[1]
    n_g = len(g_parts)

    def fn(wv, mv, vv, *gs):
        g = gs[0]
        for extra in gs[1:]:
            g = g + extra
        m_new = ADAM_B1 * mv + (1.0 - ADAM_B1) * g
        v_new = ADAM_B2 * vv + (1.0 - ADAM_B2) * jnp.square(g)
        m_hat = m_new / (1.0 - ADAM_B1 ** ADAM_STEP)
        v_hat = v_new / (1.0 - ADAM_B2 ** ADAM_STEP)
        delta = -ADAM_LR * (m_hat / (jnp.sqrt(v_hat) + ADAM_EPS) + ADAM_WD * wv)
        return (g, delta, m_new, v_new), ()
    assert n_g >= 1
    return _rows(name, fn, [(w, width, 0), (m, width, 0), (v, width, 0)] + [(g, width, 0) for g in g_parts],
                 [], [(width, F32)] * 4, [], tm=tm, rs=rs)


def _pair_sum(name, a, b, *, tm):
    shape = a.shape
    flat = (shape[0] * shape[1], shape[2])

    def fn(av, bv):
        return (av.astype(F32) + bv.astype(F32),), ()
    out = _rows(name, fn, [(a.reshape(flat), flat[1], 0), (b.reshape(flat), flat[1], 0)], [], [(flat[1], BF16)], [],
                tm=tm, rs=2 * SUBLANES)[0]
    return out.reshape(shape)


def _sum_slots(name, stack, *, tm, rs):
    width = stack.shape[2]

    def fn(*slots):
        s0, s1, s2, s3 = (s.astype(F32) for s in slots)
        return (((s0 + s1) + s2) + s3,), ()
    return _rows(name, fn, [(stack, width, 0, k) for k in range(N_CHIPS)], [], [(width, F32)], [],
                 tm=tm, rs=rs)[0]


def _layernorm(v, w, b):
    mu = jnp.mean(v, axis=-1, keepdims=True)
    var = jnp.mean(jnp.square(v - mu), axis=-1, keepdims=True)
    return (v - mu) * lax.rsqrt(var + EPS) * w + b


def _gmlp_mask():
    t = lax.broadcasted_iota(jnp.int32, (GMLP_BLOCK, GMLP_BLOCK), 0) // CHUNK
    s = lax.broadcasted_iota(jnp.int32, (GMLP_BLOCK, GMLP_BLOCK), 1) // CHUNK
    return s <= t


GMLP_TM = 512


def _gmlp_fwd(name, za, ln_w, ln_b, ws, bs_col):
    nrow = za.shape[0]
    tm = GMLP_TM
    width = GMLP_GROUPS * GMLP_BLOCK

    def body(za_ref, lnw_ref, lnb_ref, ws_ref, bs_ref, o_ref, wm_ref):
        mask = _gmlp_mask()
        for g in range(GMLP_GROUPS):
            wm_ref[g] = jnp.where(mask, ws_ref[g], 0.0).astype(BF16)

        def block(n, carry):
            rows = pl.ds(pl.multiple_of(n * GMLP_BLOCK, GMLP_BLOCK), GMLP_BLOCK)
            for g in range(GMLP_GROUPS):
                cols = slice(g * GMLP_BLOCK, (g + 1) * GMLP_BLOCK)
                vcols = slice(width + g * GMLP_BLOCK, width + (g + 1) * GMLP_BLOCK)
                u = jax.nn.gelu(za_ref[rows, cols])
                v = jax.nn.gelu(za_ref[rows, vcols])
                vn = _layernorm(v, lnw_ref[g:g + 1, :], lnb_ref[g:g + 1, :])
                sv = _dot(wm_ref[g], vn) + bs_ref[g]
                o_ref[rows, cols] = (u * sv).astype(o_ref.dtype)
            return carry

        lax.fori_loop(0, tm // GMLP_BLOCK, block, 0)

    small = lambda a: pl.BlockSpec(a.shape, lambda i, nd=a.ndim: (0,) * nd)
    return pl.pallas_call(
        body, name=name, grid=(nrow // tm,),
        in_specs=[pl.BlockSpec((tm, 2 * width), lambda i: (i, 0)), small(ln_w), small(ln_b), small(ws), small(bs_col)],
        out_specs=pl.BlockSpec((tm, width), lambda i: (i, 0)),
        out_shape=jax.ShapeDtypeStruct((nrow, width), BF16),
        scratch_shapes=[pltpu.VMEM((GMLP_GROUPS, GMLP_BLOCK, GMLP_BLOCK), BF16)],
        compiler_params=_params(("arbitrary",)),
    )(za, ln_w, ln_b, ws, bs_col)


def _gmlp_bwd(name, za, dout, ln_w, ln_b, ws, bs_col):
    nrow = za.shape[0]
    tm = GMLP_TM
    width = GMLP_GROUPS * GMLP_BLOCK

    def body(za_ref, do_ref, lnw_ref, lnb_ref, ws_ref, bs_ref, dza_ref, dlnw_ref, dlnb_ref, dws_ref, dbs_ref, wm_ref):
        mask = _gmlp_mask()
        for g in range(GMLP_GROUPS):
            wm_ref[g] = jnp.where(mask, ws_ref[g], 0.0).astype(BF16)

        @pl.when(pl.program_id(0) == 0)
        def _():
            dlnw_ref[...] = jnp.zeros_like(dlnw_ref)
            dlnb_ref[...] = jnp.zeros_like(dlnb_ref)
            dws_ref[...] = jnp.zeros_like(dws_ref)
            dbs_ref[...] = jnp.zeros_like(dbs_ref)

        def block(n, carry):
            rows = pl.ds(pl.multiple_of(n * GMLP_BLOCK, GMLP_BLOCK), GMLP_BLOCK)
            for g in range(GMLP_GROUPS):
                cols = slice(g * GMLP_BLOCK, (g + 1) * GMLP_BLOCK)
                vcols = slice(width + g * GMLP_BLOCK, width + (g + 1) * GMLP_BLOCK)
                u, gelu_u_vjp = jax.vjp(jax.nn.gelu, za_ref[rows, cols])
                v, gelu_v_vjp = jax.vjp(jax.nn.gelu, za_ref[rows, vcols])
                vn, ln_vjp = jax.vjp(_layernorm, v, lnw_ref[g:g + 1, :], lnb_ref[g:g + 1, :])
                sv = _dot(wm_ref[g], vn) + bs_ref[g]
                d_o = do_ref[rows, cols]
                dsv = d_o * u
                d_wm = _dot(dsv, vn, 1, 1)
                dvn = _dot(wm_ref[g], dsv, 0, 0)
                dv, dlnw, dlnb = ln_vjp(dvn)
                dza_ref[rows, cols] = gelu_u_vjp(d_o * sv)[0].astype(dza_ref.dtype)
                dza_ref[rows, vcols] = gelu_v_vjp(dv)[0].astype(dza_ref.dtype)
                dlnw_ref[g:g + 1, :] += dlnw
                dlnb_ref[g:g + 1, :] += dlnb
                dws_ref[g] += jnp.where(mask, d_wm, 0.0)
                dbs_ref[g] += jnp.sum(dsv, axis=1, keepdims=True)
            return carry

        lax.fori_loop(0, tm // GMLP_BLOCK, block, 0)

    small = lambda a: pl.BlockSpec(a.shape, lambda i, nd=a.ndim: (0,) * nd)
    return pl.pallas_call(
        body, name=name, grid=(nrow // tm,),
        in_specs=[pl.BlockSpec((tm, 2 * width), lambda i: (i, 0)), pl.BlockSpec((tm, width), lambda i: (i, 0)),
                  small(ln_w), small(ln_b), small(ws), small(bs_col)],
        out_specs=[pl.BlockSpec((tm, 2 * width), lambda i: (i, 0)), small(ln_w), small(ln_b), small(ws), small(bs_col)],
        out_shape=[jax.ShapeDtypeStruct((nrow, 2 * width), BF16), jax.ShapeDtypeStruct(ln_w.shape, F32),
                   jax.ShapeDtypeStruct(ln_b.shape, F32), jax.ShapeDtypeStruct(ws.shape, F32),
                   jax.ShapeDtypeStruct(bs_col.shape, F32)],
        scratch_shapes=[pltpu.VMEM((GMLP_GROUPS, GMLP_BLOCK, GMLP_BLOCK), BF16)],
        compiler_params=_params(("arbitrary",)),
    )(za, dout, ln_w, ln_b, ws, bs_col)


CONV_TM = 256
CONV_RS = 32


def _tap_rows(w_ref):
    return [w_ref[k:k + 1, :] for k in range(w_ref.shape[0])]


def _conv_rows(win, w, rs):
    taps = len(w)
    out = w[taps - 1] * win[SUBLANES:, :]
    for k in range(taps - 1):
        back = taps - 1 - k
        out = out + w[k] * pltpu.roll(win, back, axis=0)[SUBLANES:, :]
    return out


def _conv_t_rows(win, w, rs):
    taps = len(w)
    out = w[taps - 1] * win[:rs, :]
    for k in range(taps - 1):
        ahead = taps - 1 - k
        out = out + w[k] * pltpu.roll(win, rs + SUBLANES - ahead, axis=0)[:rs, :]
    return out


def _conv_dw_rows(d, xwin, taps):
    rows = []
    for k in range(taps):
        back = taps - 1 - k
        xs = xwin[SUBLANES:, :] if back == 0 else pltpu.roll(xwin, back, axis=0)[SUBLANES:, :]
        rows.append(jnp.sum(d * xs, axis=0, keepdims=True))
    return rows


def _halo_specs(nrow, tm, tc):
    per = tm // SUBLANES
    last = nrow // SUBLANES - 1
    main = pl.BlockSpec((tm, tc), lambda j, i: (i, j))
    before = pl.BlockSpec((SUBLANES, tc), lambda j, i: (jnp.maximum(i * per - 1, 0), j))
    after = pl.BlockSpec((SUBLANES, tc), lambda j, i: (jnp.minimum((i + 1) * per, last), j))
    return main, before, after


def _col_spec(rows, tc):
    return pl.BlockSpec((rows, tc), lambda j, i: (0, j))


def _conv_fwd(name, x, w, b, *, tc):
    nrow, ncol = x.shape
    taps = w.shape[0]
    tm, rs = CONV_TM, CONV_RS
    main, before, _ = _halo_specs(nrow, tm, tc)

    def body(x_ref, xb_ref, w_ref, b_ref, o_ref):
        first = pl.program_id(1) == 0
        wv, bv = _tap_rows(w_ref), b_ref[...]

        def step(r, prev):
            sl = pl.ds(pl.multiple_of(r * rs, rs), rs)
            cur = x_ref[sl, :]
            o_ref[sl, :] = _conv_rows(jnp.concatenate([prev, cur], axis=0), wv, rs) + bv
            return cur[rs - SUBLANES:, :]

        lax.fori_loop(0, tm // rs, step, jnp.where(first, 0.0, xb_ref[...]))

    return pl.pallas_call(
        body, name=name, grid=(ncol // tc, nrow // tm),
        in_specs=[main, before, _col_spec(taps, tc), _col_spec(1, tc)],
        out_specs=main, out_shape=jax.ShapeDtypeStruct((nrow, ncol), F32),
        compiler_params=_params(("parallel", "arbitrary")),
    )(x, x, w, b)


def _conv_bwd(name, dpre, x, w, *, tc):
    nrow, ncol = x.shape
    taps = w.shape[0]
    tm, rs = CONV_TM, CONV_RS
    nsub = tm // rs
    main, before, after = _halo_specs(nrow, tm, tc)

    def body(d_ref, da_ref, x_ref, xb_ref, w_ref, dx_ref, dw_ref, db_ref):
        i = pl.program_id(1)
        first, last = i == 0, i == pl.num_programs(1) - 1
        wv = _tap_rows(w_ref)
        x_before = jnp.where(first, 0.0, xb_ref[...])

        @pl.when(first)
        def _():
            dw_ref[...] = jnp.zeros_like(dw_ref)
            db_ref[...] = jnp.zeros_like(db_ref)

        def step(q, carry):
            nxt, dw, db = carry
            r = nsub - 1 - q
            sl = pl.ds(pl.multiple_of(r * rs, rs), rs)
            cur = d_ref[sl, :]
            dx_ref[sl, :] = _conv_t_rows(jnp.concatenate([cur, nxt], axis=0), wv, rs).astype(dx_ref.dtype)
            inner = x_ref[pl.ds(pl.multiple_of(jnp.maximum(r * rs - SUBLANES, 0), SUBLANES), SUBLANES), :]
            xwin = jnp.concatenate([jnp.where(r == 0, x_before, inner), x_ref[sl, :]], axis=0)
            dw = tuple(a + s for a, s in zip(dw, _conv_dw_rows(cur, xwin, taps)))
            return cur[:SUBLANES, :], dw, db + _colsum(cur)

        zero_row = jnp.zeros((1, tc), F32)
        init = (jnp.where(last, 0.0, da_ref[...]), (zero_row,) * taps, zero_row)
        _, dw, db = lax.fori_loop(0, nsub, step, init)
        for k in range(taps):
            dw_ref[k:k + 1, :] += dw[k]
        db_ref[...] += db

    return pl.pallas_call(
        body, name=name, grid=(ncol // tc, nrow // tm),
        in_specs=[main, after, main, before, _col_spec(taps, tc)],
        out_specs=[main, _col_spec(taps, tc), _col_spec(1, tc)],
        out_shape=[jax.ShapeDtypeStruct((nrow, ncol), BF16), jax.ShapeDtypeStruct((taps, ncol), F32),
                   jax.ShapeDtypeStruct((1, ncol), F32)],
        compiler_params=_params(("parallel", "arbitrary")),
    )(dpre, dpre, x, x, w)


def _glu(gate, val):
    return jax.nn.silu(gate) * val


def _ffn_act_fwd(name, pg, pv, wg, wv, bg, bv, *, tc):
    nrow, ncol = pg.shape
    taps = wg.shape[0]
    tm, rs = CONV_TM, CONV_RS
    main, before, _ = _halo_specs(nrow, tm, tc)

    def body(pg_ref, pgb_ref, pv_ref, pvb_ref, wg_ref, wv_ref, bg_ref, bv_ref, g_ref, v_ref, a_ref):
        first = pl.program_id(1) == 0
        wgv, wvv, bgv, bvv = _tap_rows(wg_ref), _tap_rows(wv_ref), bg_ref[...], bv_ref[...]

        def step(r, carry):
            prev_g, prev_v = carry
            sl = pl.ds(pl.multiple_of(r * rs, rs), rs)
            cur_g, cur_v = pg_ref[sl, :], pv_ref[sl, :]
            gate = _conv_rows(jnp.concatenate([prev_g, cur_g], axis=0), wgv, rs) + bgv
            val = _conv_rows(jnp.concatenate([prev_v, cur_v], axis=0), wvv, rs) + bvv
            g_ref[sl, :] = gate
            v_ref[sl, :] = val
            a_ref[sl, :] = _glu(gate, val).astype(a_ref.dtype)
            return cur_g[rs - SUBLANES:, :], cur_v[rs - SUBLANES:, :]

        lax.fori_loop(0, tm // rs, step, (jnp.where(first, 0.0, pgb_ref[...]), jnp.where(first, 0.0, pvb_ref[...])))

    return pl.pallas_call(
        body, name=name, grid=(ncol // tc, nrow // tm),
        in_specs=[main, before, main, before, _col_spec(taps, tc), _col_spec(taps, tc), _col_spec(1, tc), _col_spec(1, tc)],
        out_specs=[main, main, main],
        out_shape=[jax.ShapeDtypeStruct((nrow, ncol), F32), jax.ShapeDtypeStruct((nrow, ncol), F32),
                   jax.ShapeDtypeStruct((nrow, ncol), BF16)],
        compiler_params=_params(("parallel", "arbitrary")),
    )(pg, pg, pv, pv, wg, wv, bg, bv)


def _ffn_act_bwd(name, dact, gate, val):
    def fn(dv, gv, vv):
        _, vjp = jax.vjp(_glu, gv, vv)
        dg, dval = vjp(dv)
        return (dg, dval), ()
    width = dact.shape[1]
    return _rows(name, fn, [(dact, width, 0), (gate, width, 0), (val, width, 0)], [],
                 [(width, F32), (width, F32)], [], tm=256, rs=8)


SSD_TM = 256
SSD_CHUNKS = SSD_TM // CHUNK
X_OFF, B_OFF, C_OFF = 0, SSM_INNER, SSM_INNER + SSM_GROUPS * SSM_STATE
HP = SSM_HPG * SSM_HEAD_DIM


def _causal_tiled():
    row = lax.broadcasted_iota(jnp.int32, (CHUNK, HP), 0)
    src = lax.broadcasted_iota(jnp.int32, (CHUNK, HP), 1) & (CHUNK - 1)
    return src <= row


def _split3(v):
    hi = v.astype(BF16)
    rest = v - hi.astype(F32)
    mid = rest.astype(BF16)
    lo = (rest - mid.astype(F32)).astype(BF16)
    return hi, mid, lo


def _dot_exact(a, ind):
    parts = [lax.dot_general(p, ind, (((1,), (0,)), ((), ())), preferred_element_type=F32) for p in _split3(a)]
    return (parts[0] + parts[1]) + parts[2]


def _head_indicator():
    head = lax.broadcasted_iota(jnp.int32, (SSM_HEADS, SSM_INNER), 0)
    chan = lax.broadcasted_iota(jnp.int32, (SSM_HEADS, SSM_INNER), 1)
    return (chan // SSM_HEAD_DIM == head).astype(BF16)


def _chunk_decays(ci, dt_ref, ac_ref, ind, ax_ref, dtx_ref, eax_ref, eex_ref, tail_ref):
    rows = pl.ds(pl.multiple_of(ci * CHUNK, CHUNK), CHUNK)
    ax_ref[...] = _dot_exact(ac_ref[rows, :], ind)
    dtx_ref[...] = _dot_exact(dt_ref[rows, :], ind)
    eax_ref[...] = jnp.exp(ax_ref[...])
    eex_ref[...] = jnp.exp(ax_ref[CHUNK - 1:CHUNK, :] - ax_ref[...])
    tail = pl.ds(pl.multiple_of(ci * CHUNK + CHUNK - SUBLANES, SUBLANES), SUBLANES)
    tail_ref[...] = jnp.exp(ac_ref[tail, :])


def _group_decay(ci, g, ax_ref, af_ref, xbc_ref, causal):
    gcols = slice(g * HP, (g + 1) * HP)
    bm = xbc_ref[:, B_OFF + g * SSM_STATE:B_OFF + (g + 1) * SSM_STATE]
    cm = xbc_ref[:, C_OFF + g * SSM_STATE:C_OFF + (g + 1) * SSM_STATE]
    cb_tiled = _dot(cm, jnp.concatenate([bm] * SSM_HPG, axis=0), 1, 1)
    seg = ax_ref[:, gcols] - af_ref[ci, :, gcols]
    decay = jnp.where(causal, jnp.exp(jnp.where(causal, seg, 0.0)), 0.0)
    return bm, cm, cb_tiled * decay, decay


def _ssd_fwd(name, pre, dt, a_cum, a_flat, d_x, ind):
    nrow = pre.shape[0]
    tm = SSD_TM

    def body(pre_ref, dt_ref, ac_ref, af_ref, dx_ref, ind_ref, y_ref, st_ref,
             h_ref, xbc_ref, ax_ref, dtx_ref, eax_ref, eex_ref, m_ref, xd_ref, yd_ref, tail_ref):
        @pl.when(pl.program_id(0) == 0)
        def _():
            h_ref[...] = jnp.zeros_like(h_ref)

        causal = _causal_tiled()
        ind = ind_ref[...]

        def chunk(ci, carry):
            rows = pl.ds(pl.multiple_of(ci * CHUNK, CHUNK), CHUNK)
            xbc_ref[...] = jax.nn.silu(pre_ref[rows, :])
            _chunk_decays(ci, dt_ref, ac_ref, ind, ax_ref, dtx_ref, eax_ref, eex_ref, tail_ref)
            st_ref[ci] = h_ref[...]
            for g in range(SSM_GROUPS):
                gcols = slice(g * HP, (g + 1) * HP)
                bm, cm, m_all, _ = _group_decay(ci, g, ax_ref, af_ref, xbc_ref, causal)
                m_ref[...] = m_all
                x_g = xbc_ref[:, gcols]
                xd = x_g * dtx_ref[:, gcols]
                xd_ref[...] = xd
                h_g = h_ref[gcols, :]
                for hh in range(SSM_HPG):
                    lc = slice(hh * SSM_HEAD_DIM, (hh + 1) * SSM_HEAD_DIM)
                    yd_ref[:, lc] = _dot(m_ref[:, lc], xd_ref[:, lc])
                y_ref[rows, gcols] = (yd_ref[...] + _dot(cm, h_g, 1, 1) * eax_ref[:, gcols]
                                      + dx_ref[:, gcols] * x_g)
                new = _dot(xd * eex_ref[:, gcols], bm, 0, 0)
                for hh in range(SSM_HPG):
                    h = g * SSM_HPG + hh
                    hrows = slice(h * SSM_HEAD_DIM, (h + 1) * SSM_HEAD_DIM)
                    lrows = slice(hh * SSM_HEAD_DIM, (hh + 1) * SSM_HEAD_DIM)
                    h_ref[hrows, :] = tail_ref[SUBLANES - 1:SUBLANES, h:h + 1] * h_ref[hrows, :] + new[lrows, :]
            return carry

        lax.fori_loop(0, SSD_CHUNKS, chunk, 0)

    nchunk = nrow // CHUNK
    whole = lambda a: pl.BlockSpec(a.shape, lambda i, nd=a.ndim: (0,) * nd)
    wide = lambda: pltpu.VMEM((CHUNK, SSM_INNER), F32)
    group = lambda: pltpu.VMEM((CHUNK, HP), F32)
    return pl.pallas_call(
        body, name=name, grid=(nrow // tm,),
        in_specs=[pl.BlockSpec((tm, SSM_XBC), lambda i: (i, 0)), pl.BlockSpec((tm, SSM_HEADS), lambda i: (i, 0)),
                  pl.BlockSpec((tm, SSM_HEADS), lambda i: (i, 0)),
                  pl.BlockSpec((SSD_CHUNKS, 1, SSM_INNER), lambda i: (i, 0, 0)), whole(d_x), whole(ind)],
        out_specs=[pl.BlockSpec((tm, SSM_INNER), lambda i: (i, 0)),
                   pl.BlockSpec((SSD_CHUNKS, SSM_INNER, SSM_STATE), lambda i: (i, 0, 0))],
        out_shape=[jax.ShapeDtypeStruct((nrow, SSM_INNER), F32),
                   jax.ShapeDtypeStruct((nchunk, SSM_INNER, SSM_STATE), F32)],
        scratch_shapes=[pltpu.VMEM((SSM_INNER, SSM_STATE), F32), pltpu.VMEM((CHUNK, SSM_XBC), F32),
                        wide(), wide(), wide(), wide(), group(), group(), group(),
                        pltpu.VMEM((SUBLANES, SSM_HEADS), F32)],
        compiler_params=_params(("arbitrary",)),
    )(pre, dt, a_cum, a_flat, d_x, ind)


def _ssd_bwd(name, pre, dt, a_cum, a_flat, d_x, ind, ind_t, states, dy):
    nrow = pre.shape[0]
    tm = SSD_TM
    ntile = nrow // tm

    def body(pre_ref, dt_ref, ac_ref, af_ref, dx_ref, ind_ref, indt_ref, st_ref, dy_ref,
             dpre_ref, ddt_ref, da_ref, daf_ref, dd_ref,
             dh_ref, xbc_ref, dxbc_ref, ax_ref, dtx_ref, eax_ref, eex_ref, ra_ref, ts_ref, r2_ref,
             m_ref, l_ref, xd_ref, dm_ref, dxd_ref, fold_ref, hd_ref, tail_ref):
        @pl.when(pl.program_id(0) == 0)
        def _():
            dh_ref[...] = jnp.zeros_like(dh_ref)
            dd_ref[...] = jnp.zeros_like(dd_ref)

        causal = _causal_tiled()
        ind, ind_t = ind_ref[...], indt_ref[...]
        is_last_row = lax.broadcasted_iota(jnp.int32, (CHUNK, 1), 0) == CHUNK - 1
        ones = jnp.ones((CHUNK, SSM_STATE), BF16)

        def chunk(k, ddx):
            ci = SSD_CHUNKS - 1 - k
            rows = pl.ds(pl.multiple_of(ci * CHUNK, CHUNK), CHUNK)
            pre_v = pre_ref[rows, :]
            xbc_ref[...] = jax.nn.silu(pre_v)
            _chunk_decays(ci, dt_ref, ac_ref, ind, ax_ref, dtx_ref, eax_ref, eex_ref, tail_ref)
            ddx_parts = []
            for g in range(SSM_GROUPS):
                gcols = slice(g * HP, (g + 1) * HP)
                bcols = slice(B_OFF + g * SSM_STATE, B_OFF + (g + 1) * SSM_STATE)
                ccols = slice(C_OFF + g * SSM_STATE, C_OFF + (g + 1) * SSM_STATE)
                bm, cm, m_all, decay = _group_decay(ci, g, ax_ref, af_ref, xbc_ref, causal)
                m_ref[...] = m_all
                l_ref[...] = decay
                x_g = xbc_ref[:, gcols]
                xd = x_g * dtx_ref[:, gcols]
                xd_ref[...] = xd
                h_g = st_ref[ci, gcols, :]
                dh_g = dh_ref[gcols, :]
                dy_g = dy_ref[rows, gcols]
                for hh in range(SSM_HPG):
                    h = g * SSM_HPG + hh
                    hcols = slice(h * SSM_HEAD_DIM, (h + 1) * SSM_HEAD_DIM)
                    lc = slice(hh * SSM_HEAD_DIM, (hh + 1) * SSM_HEAD_DIM)
                    dy_h = dy_ref[rows, hcols]
                    dm_ref[:, lc] = _dot(dy_h, xd_ref[:, lc], 1, 1)
                    dxd_ref[:, lc] = _dot(m_ref[:, lc], dy_h, 0, 0)
                ebdh = eex_ref[:, gcols] * _dot(bm, dh_g, 1, 1)
                dxd = dxd_ref[...] + ebdh
                dm = dm_ref[...]
                t = dm * l_ref[...]
                t128 = (t[:, 0:LANES] + t[:, LANES:2 * LANES]) + (t[:, 2 * LANES:3 * LANES] + t[:, 3 * LANES:])
                fold_ref[...] = t128 + pltpu.roll(t128, CHUNK, axis=1)
                dw_sum = fold_ref[:, 0:CHUNK]
                q = dm * m_ref[...]
                dyea = dy_g * eax_ref[:, gcols]
                ra_ref[:, gcols] = q + dyea * _dot(cm, h_g, 1, 1)
                ts_ref[:, gcols] = xd * ebdh
                r2_ref[:, gcols] = dxd * x_g
                daf_ref[ci, :, gcols] = -jnp.sum(q, axis=0, keepdims=True)
                ddx_parts.append(jnp.sum(dy_g * x_g, axis=0, keepdims=True))
                dxbc_ref[:, gcols] = dxd * dtx_ref[:, gcols] + dx_ref[:, gcols] * dy_g
                dxbc_ref[:, ccols] = _dot(dw_sum, bm) + _dot(dyea, h_g)
                dxbc_ref[:, bcols] = _dot(dw_sum, cm, 0, 0) + _dot(xd * eex_ref[:, gcols], dh_g)
                dh_new = _dot(dyea, cm, 0, 0)
                for hh in range(SSM_HPG):
                    h = g * SSM_HPG + hh
                    hrows = slice(h * SSM_HEAD_DIM, (h + 1) * SSM_HEAD_DIM)
                    lrows = slice(hh * SSM_HEAD_DIM, (hh + 1) * SSM_HEAD_DIM)
                    hd_ref[h:h + 1, :] = jnp.sum(st_ref[ci, hrows, :] * dh_ref[hrows, :], axis=0, keepdims=True)
                    dh_ref[hrows, :] = tail_ref[SUBLANES - 1:SUBLANES, h:h + 1] * dh_ref[hrows, :] + dh_new[lrows, :]
            ra = _dot_exact(ra_ref[...], ind_t)
            ts = _dot_exact(ts_ref[...], ind_t)
            hdh = sum(lax.dot_general(ones, p, (((1,), (1,)), ((), ())), preferred_element_type=F32)
                      for p in _split3(hd_ref[...]))
            da_last = jnp.sum(ts, axis=0, keepdims=True) + tail_ref[SUBLANES - 1:SUBLANES, :] * hdh
            da_ref[rows, :] = ra - ts + jnp.where(is_last_row, da_last, 0.0)
            ddt_ref[rows, :] = _dot_exact(r2_ref[...], ind_t)
            sig = jax.nn.sigmoid(pre_v)
            dpre_ref[rows, :] = dxbc_ref[...] * (sig * (1.0 + pre_v * (1.0 - sig)))
            return ddx + jnp.concatenate(ddx_parts, axis=1)

        ddx = lax.fori_loop(0, SSD_CHUNKS, chunk, jnp.zeros((1, SSM_INNER), F32))
        dd_ref[...] += _dot_exact(jnp.broadcast_to(ddx, (SUBLANES, SSM_INNER)), ind_t)

    rev = lambda i: ntile - 1 - i
    whole = lambda a: pl.BlockSpec(a.shape, lambda i, nd=a.ndim: (0,) * nd)
    wide = lambda: pltpu.VMEM((CHUNK, SSM_INNER), F32)
    group = lambda: pltpu.VMEM((CHUNK, HP), F32)
    return pl.pallas_call(
        body, name=name, grid=(ntile,),
        in_specs=[pl.BlockSpec((tm, SSM_XBC), lambda i: (rev(i), 0)), pl.BlockSpec((tm, SSM_HEADS), lambda i: (rev(i), 0)),
                  pl.BlockSpec((tm, SSM_HEADS), lambda i: (rev(i), 0)),
                  pl.BlockSpec((SSD_CHUNKS, 1, SSM_INNER), lambda i: (rev(i), 0, 0)),
                  whole(d_x), whole(ind), whole(ind_t),
                  pl.BlockSpec((SSD_CHUNKS, SSM_INNER, SSM_STATE), lambda i: (rev(i), 0, 0)),
                  pl.BlockSpec((tm, SSM_INNER), lambda i: (rev(i), 0))],
        out_specs=[pl.BlockSpec((tm, SSM_XBC), lambda i: (rev(i), 0)), pl.BlockSpec((tm, SSM_HEADS), lambda i: (rev(i), 0)),
                   pl.BlockSpec((tm, SSM_HEADS), lambda i: (rev(i), 0)),
                   pl.BlockSpec((SSD_CHUNKS, 1, SSM_INNER), lambda i: (rev(i), 0, 0)),
                   pl.BlockSpec((SUBLANES, SSM_HEADS), lambda i: (0, 0))],
        out_shape=[jax.ShapeDtypeStruct((nrow, SSM_XBC), F32), jax.ShapeDtypeStruct((nrow, SSM_HEADS), F32),
                   jax.ShapeDtypeStruct((nrow, SSM_HEADS), F32), jax.ShapeDtypeStruct((nrow // CHUNK, 1, SSM_INNER), F32),
                   jax.ShapeDtypeStruct((SUBLANES, SSM_HEADS), F32)],
        scratch_shapes=[pltpu.VMEM((SSM_INNER, SSM_STATE), F32), pltpu.VMEM((CHUNK, SSM_XBC), F32),
                        pltpu.VMEM((CHUNK, SSM_XBC), F32), wide(), wide(), wide(), wide(), wide(), wide(), wide(),
                        group(), group(), group(), group(), group(), pltpu.VMEM((CHUNK, LANES), F32),
                        pltpu.VMEM((SSM_HEADS, SSM_STATE), F32), pltpu.VMEM((SUBLANES, SSM_HEADS), F32)],
        compiler_params=_params(("arbitrary",)),
    )(pre, dt, a_cum, a_flat, d_x, ind, ind_t, states, dy)


def _local_step(x, target, w):
    g = {}
    bs_col = w["gmlp_bs"].reshape(GMLP_GROUPS, GMLP_BLOCK, 1)
    b0, b1 = w["gate_bias"][0:1], w["gate_bias"][1:2]

    xn = _rms_fwd("mix_norm", x, w["mix_norm_w"])
    big = dict(bm=1024, bn=1024, bk=1024)
    gates = _mm("in_gates", xn, w["w_g"], **big)
    za = _mm("in_gmlp", xn, w["w_za"], **big)
    z = _mm("in_z", xn, w["w_z"], **big)
    xbc = _mm("in_xbc", xn, w["w_xbc"], **big)
    dt_raw = _mm("in_dt", xn, w["w_dt"], bm=1024, bn=SSM_HEADS, bk=1024)

    ya_pre = _gmlp_fwd("gmlp_fwd", za, w["gmlp_ln_w"], w["gmlp_ln_b"], w["gmlp_ws"], bs_col)
    y_a = _mm("proj_a", ya_pre, w["w_proj_a"], **big)

    pre = _conv_fwd("ssm_conv_fwd", xbc, w["ssm_conv_w"], w["ssm_conv_b"], tc=1024)
    dt, a_cum = _dt_prep("dt_prep", dt_raw, w["ssm_dt_bias"], w["ssm_a_log"])
    a_flat = jnp.transpose(a_cum.reshape(-1, CHUNK, SSM_HEADS), (0, 2, 1)).reshape(-1, 1, SSM_INNER)
    d_x = jnp.repeat(w["ssm_d"], SSM_HEAD_DIM, axis=1)
    ind = _head_indicator()
    y_ssd, states = _ssd_fwd("ssd_fwd", pre, dt, a_cum, a_flat, d_x, ind)
    yb_pre = _gate_norm_fwd("gate_norm_fwd", y_ssd, z, w["ssm_norm_w"])
    y_b = _mm("proj_b", yb_pre, w["w_proj_b"], **big)

    merged = _merge_fwd("merge_fwd", gates, y_a, y_b, b0, b1)
    h1 = _mm("out_proj", merged, w["w_out"], res=x, **big)

    hn = _rms_fwd("ffn_norm", h1, w["ffn_norm_w"])
    half = dict(bm=1024, bn=D_FF // 2, bk=1024)
    pg = _mm("ffn_up_gate", hn, w["w_up_g"], **half)
    pv = _mm("ffn_up_val", hn, w["w_up_v"], **half)
    cw, cb = w["ffn_conv_w"], w["ffn_conv_b"]
    gate, val, act = _ffn_act_fwd("ffn_act_fwd", pg, pv, cw[:, :D_FF], cw[:, D_FF:], cb[:, :D_FF], cb[:, D_FF:],
                                  tc=D_FF // 2)
    h2 = _mm("ffn_down", act, w["w_down"], res=h1, bm=512, bn=1024, bk=D_FF // 2)

    dh2, loss_part, g["final_norm_w"] = _final_loss("final_loss", h2, target, w["final_norm_w"].reshape(1, D_MODEL))

    dact = _mm("d_act", dh2, w["w_down"], tb=True, bm=1024, bn=D_FF // 2, bk=1024)
    wgrad = dict(ta=True, bk=512, out_dtype=BF16)
    g["w_down"] = _mm("dw_down", act, dh2, bm=D_FF // 2, bn=1024, **wgrad)
    dgate, dval = _ffn_act_bwd("ffn_act_bwd", dact, gate, val)
    dpg, dcwg, dcbg = _conv_bwd("ffn_conv_bwd_gate", dgate, pg, cw[:, :D_FF], tc=D_FF // 2)
    dpv, dcwv, dcbv = _conv_bwd("ffn_conv_bwd_val", dval, pv, cw[:, D_FF:], tc=D_FF // 2)
    g["ffn_conv_w"] = jnp.concatenate([dcwg, dcwv], axis=1)
    g["ffn_conv_b"] = jnp.concatenate([dcbg, dcbv], axis=1)
    back = dict(bm=1024, bn=1024, bk=D_FF // 2)
    dhn = _mm("d_hn_gate", dpg, w["w_up_g"], tb=True, **back)
    dhn = _mm("d_hn_val", dpv, w["w_up_v"], tb=True, res=dhn, **back)
    g["w_up_g"] = _mm("dw_up_gate", hn, dpg, bm=1024, bn=D_FF // 2, **wgrad)
    g["w_up_v"] = _mm("dw_up_val", hn, dpv, bm=1024, bn=D_FF // 2, **wgrad)
    dh1, g["ffn_norm_w"] = _rms_bwd("ffn_norm_bwd", h1, w["ffn_norm_w"], dhn, dh2)

    dmerged = _mm("d_merged", dh1, w["w_out"], tb=True, **big)
    g["w_out"] = _mm("dw_out", merged, dh1, bm=1024, bn=1024, **wgrad)
    dgates, dya, dyb, db0, db1 = _merge_bwd("merge_bwd", gates, y_a, y_b, dmerged, b0, b1)
    g["gate_bias"] = jnp.concatenate([db0, db1], axis=0)

    dya_pre = _mm("d_ya_pre", dya, w["w_proj_a"], tb=True, **big)
    g["w_proj_a"] = _mm("dw_proj_a", ya_pre, dya, bm=1024, bn=1024, **wgrad)
    dza, g["gmlp_ln_w"], g["gmlp_ln_b"], g["gmlp_ws"], dbs = _gmlp_bwd(
        "gmlp_bwd", za, dya_pre, w["gmlp_ln_w"], w["gmlp_ln_b"], w["gmlp_ws"], bs_col)
    g["gmlp_bs"] = dbs.reshape(GMLP_GROUPS, GMLP_BLOCK)

    dyb_pre = _mm("d_yb_pre", dyb, w["w_proj_b"], tb=True, **big)
    g["w_proj_b"] = _mm("dw_proj_b", yb_pre, dyb, bm=1024, bn=1024, **wgrad)
    dy_ssd, dz, g["ssm_norm_w"] = _gate_norm_bwd("gate_norm_bwd", y_ssd, z, dyb_pre, w["ssm_norm_w"])
    dpre, ddt, da_tok, da_flat, dd = _ssd_bwd("ssd_bwd", pre, dt, a_cum, a_flat, d_x, ind, ind.T, states, dy_ssd)
    g["ssm_d"] = dd[0:1]
    da_src = jnp.transpose(da_flat.reshape(-1, SSM_HEADS, CHUNK), (0, 2, 1)).reshape(-1, SSM_HEADS)
    ddt_raw, g["ssm_dt_bias"], g["ssm_a_log"] = _dt_bwd("dt_bwd", dt_raw, ddt, da_tok, da_src,
                                                         w["ssm_dt_bias"], w["ssm_a_log"])
    dxbc, g["ssm_conv_w"], g["ssm_conv_b"] = _conv_bwd("ssm_conv_bwd", dpre, xbc, w["ssm_conv_w"], tc=1024)

    dxn = _mm("d_xn_gates", dgates, w["w_g"], tb=True, **big)
    dxn = _mm("d_xn_gmlp", dza, w["w_za"], tb=True, res=dxn, **big)
    dxn = _mm("d_xn_z", dz, w["w_z"], tb=True, res=dxn, **big)
    dxn = _mm("d_xn_xbc", dxbc, w["w_xbc"], tb=True, res=dxn, **big)
    dxn = _mm("d_xn_dt", ddt_raw, w["w_dt"], tb=True, res=dxn, bm=1024, bn=1024, bk=SSM_HEADS)
    g["w_g"] = _mm("dw_gates", xn, dgates, bm=1024, bn=1024, **wgrad)
    g["w_za"] = _mm("dw_gmlp", xn, dza, bm=1024, bn=1024, **wgrad)
    g["w_z"] = _mm("dw_z", xn, dz, bm=1024, bn=1024, **wgrad)
    g["w_xbc"] = _mm("dw_xbc", xn, dxbc, bm=1024, bn=1024, **wgrad)
    g["w_dt"] = _mm("dw_dt", xn, ddt_raw, bm=1024, bn=SSM_HEADS, **wgrad)
    grad_x, g["mix_norm_w"] = _rms_bwd("mix_norm_bwd", x, w["mix_norm_w"], dxn, dh1)
    return loss_part, grad_x, g


def _position():
    return lax.axis_index("x"), lax.axis_index("y"), lax.axis_index("c")


def _own_slot(stack, own):
    chip = 2 * lax.axis_index("x") + lax.axis_index("y")
    return lax.dynamic_update_index_in_dim(stack, own, chip, axis=0)


def _scatter_chips(name, arrs):
    n = len(arrs)

    def body(*refs):
        ins, outs = refs[:n], refs[n:2 * n]
        send_sems, recv_sems = refs[2 * n:]
        x, y, c = _position()
        me = 2 * x + y
        peers = [(1 - x, y), (x, 1 - y), (1 - x, 1 - y)]
        sends = []
        for i in range(n):
            for k, (px, py) in enumerate(peers):
                cp = pltpu.make_async_remote_copy(
                    src_ref=ins[i].at[2 * px + py], dst_ref=outs[i].at[me],
                    send_sem=send_sems.at[i, k], recv_sem=recv_sems.at[i, k],
                    device_id=(px, py, c), device_id_type=MESH)
                cp.start()
                sends.append(cp)
        for i in range(n):
            for k, (px, py) in enumerate(peers):
                pltpu.make_async_remote_copy(
                    src_ref=ins[i].at[me], dst_ref=outs[i].at[2 * px + py],
                    send_sem=send_sems.at[i, k], recv_sem=recv_sems.at[i, k],
                    device_id=(px, py, c), device_id_type=MESH).wait_recv()
        for cp in sends:
            cp.wait_send()

    hbm = pl.BlockSpec(memory_space=pl.ANY)
    return pl.pallas_call(
        body, name=name,
        in_specs=[hbm] * n, out_specs=[hbm] * n,
        out_shape=[jax.ShapeDtypeStruct(a.shape, a.dtype) for a in arrs],
        scratch_shapes=[pltpu.SemaphoreType.DMA((n, N_CHIPS - 1)), pltpu.SemaphoreType.DMA((n, N_CHIPS - 1))],
        compiler_params=pltpu.CompilerParams(has_side_effects=True),
    )(*arrs)


def _half_rows(ref_rows, which):
    half = ref_rows // 2
    return pl.ds(pl.multiple_of(which * half, 2 * SUBLANES), half)


def _gather_chips_split(name, split, whole):
    ns, nw = len(split), len(whole)
    arrs = list(split) + list(whole)
    n = ns + nw

    def body(*refs):
        ins, outs = refs[:n], refs[n:2 * n]
        send_sems, recv_sems = refs[2 * n:]
        x, y, c = _position()
        me = 2 * x + y
        sibling = (x, y, 1 - c)
        chips = [(1 - x, y), (x, 1 - y), (1 - x, 1 - y)]

        def remote(i, k, src, dst, to):
            return pltpu.make_async_remote_copy(src_ref=src, dst_ref=dst, send_sem=send_sems.at[i, k],
                                                recv_sem=recv_sems.at[i, k], device_id=to, device_id_type=MESH)

        sends = []
        for i in range(n):
            rows = _half_rows(arrs[i].shape[0], c) if i < ns else slice(None)
            for k, (px, py) in enumerate(chips):
                cp = remote(i, k, ins[i].at[rows], outs[i].at[me, rows], (px, py, c))
                cp.start()
                sends.append(cp)
        for i in range(ns):
            rows = _half_rows(arrs[i].shape[0], c)
            for k, (px, py) in enumerate(chips):
                landed = outs[i].at[2 * px + py, rows]
                remote(i, k, landed, landed, (px, py, c)).wait_recv()
                cp = remote(i, N_CHIPS - 1 + k, landed, landed, sibling)
                cp.start()
                sends.append(cp)
        for i in range(ns, n):
            for k, (px, py) in enumerate(chips):
                landed = outs[i].at[2 * px + py]
                remote(i, k, landed, landed, (px, py, c)).wait_recv()
        for i in range(ns):
            rows = _half_rows(arrs[i].shape[0], 1 - c)
            for k, (px, py) in enumerate(chips):
                landed = outs[i].at[2 * px + py, rows]
                remote(i, N_CHIPS - 1 + k, landed, landed, sibling).wait_recv()
        for cp in sends:
            cp.wait_send()

    hbm = pl.BlockSpec(memory_space=pl.ANY)
    nsem = 2 * (N_CHIPS - 1)
    return pl.pallas_call(
        body, name=name, in_specs=[hbm] * n, out_specs=[hbm] * n,
        out_shape=[jax.ShapeDtypeStruct((N_CHIPS,) + a.shape, a.dtype) for a in arrs],
        scratch_shapes=[pltpu.SemaphoreType.DMA((n, nsem)), pltpu.SemaphoreType.DMA((n, nsem))],
        compiler_params=pltpu.CompilerParams(has_side_effects=True),
    )(*arrs)


def _swap_cores(name, arrs):
    n = len(arrs)

    def body(*refs):
        ins, outs = refs[:n], refs[n:2 * n]
        send_sems, recv_sems = refs[2 * n:]
        x, y, c = _position()
        copies = [pltpu.make_async_remote_copy(src_ref=ins[i], dst_ref=outs[i], send_sem=send_sems.at[i],
                                               recv_sem=recv_sems.at[i], device_id=(x, y, 1 - c), device_id_type=MESH)
                  for i in range(n)]
        for cp in copies:
            cp.start()
        for cp in copies:
            cp.wait_recv()
        for cp in copies:
            cp.wait_send()

    hbm = pl.BlockSpec(memory_space=pl.ANY)
    return pl.pallas_call(
        body, name=name, in_specs=[hbm] * n, out_specs=[hbm] * n,
        out_shape=[jax.ShapeDtypeStruct(a.shape, a.dtype) for a in arrs],
        scratch_shapes=[pltpu.SemaphoreType.DMA((n,)), pltpu.SemaphoreType.DMA((n,))],
        compiler_params=pltpu.CompilerParams(has_side_effects=True),
    )(*arrs)


def _row_half(a, which, axis):
    half = a.shape[axis] // 2
    return lax.dynamic_slice_in_dim(a, which * half, half, axis=axis)


def _all_reduce(name, pack):
    def body(in_ref, out_ref, buf, send_sems, recv_sems):
        x, y, c = _position()
        me = 4 * x + 2 * y + c
        flips = [(dx, dy, dc) for dx in (0, 1) for dy in (0, 1) for dc in (0, 1) if (dx, dy, dc) != (0, 0, 0)]
        peers = [((1 - x) if dx else x, (1 - y) if dy else y, (1 - c) if dc else c) for dx, dy, dc in flips]
        buf[me] = in_ref[...]
        sends = []
        for k, peer in enumerate(peers):
            cp = pltpu.make_async_remote_copy(src_ref=in_ref, dst_ref=buf.at[me], send_sem=send_sems.at[k],
                                              recv_sem=recv_sems.at[k], device_id=peer, device_id_type=MESH)
            cp.start()
            sends.append(cp)
        for k, (px, py, pc) in enumerate(peers):
            pltpu.make_async_remote_copy(src_ref=in_ref, dst_ref=buf.at[4 * px + 2 * py + pc], send_sem=send_sems.at[k],
                                         recv_sem=recv_sems.at[k], device_id=(px, py, pc), device_id_type=MESH).wait_recv()
        total = buf[0]
        for j in range(1, N_DEV):
            total = total + buf[j]
        out_ref[...] = total
        for cp in sends:
            cp.wait_send()

    vmem = pl.BlockSpec(memory_space=pltpu.VMEM)
    return pl.pallas_call(
        body, name=name, in_specs=[vmem], out_specs=vmem,
        out_shape=jax.ShapeDtypeStruct(pack.shape, F32),
        scratch_shapes=[pltpu.VMEM((N_DEV,) + pack.shape, F32), pltpu.SemaphoreType.DMA((N_DEV - 1,)),
                        pltpu.SemaphoreType.DMA((N_DEV - 1,))],
        compiler_params=pltpu.CompilerParams(has_side_effects=True, vmem_limit_bytes=VMEM_LIMIT_V7X),
    )(pack)


PACK_UNIT = SUBLANES * LANES


def _pack(arrs):
    flat = []
    for a in arrs:
        v = a.reshape(-1).astype(F32)
        flat.append(jnp.pad(v, (0, -v.size % PACK_UNIT)))
    return jnp.concatenate(flat).reshape(-1, LANES)


def _unpack(pack, shapes):
    flat = pack.reshape(-1)
    out, off = [], 0
    for s in shapes:
        size = 1
        for d in s:
            size *= d
        out.append(flat[off:off + size].reshape(s))
        off += size + (-size % PACK_UNIT)
    return out


SMALL = ["mix_norm_w", "gate_bias", "gmlp_ln_w", "gmlp_ln_b", "gmlp_ws", "gmlp_bs", "ssm_conv_w", "ssm_conv_b",
         "ssm_dt_bias", "ssm_a_log", "ssm_d", "ssm_norm_w", "ffn_norm_w", "ffn_conv_w", "ffn_conv_b", "final_norm_w"]
SMALL_SHARDED = ("gate_bias", "ssm_conv_w", "ffn_conv_w")
BIG = ["w_in", "w_proj_a", "w_proj_b", "w_out", "ffn_w_up", "ffn_w_down"]
WEIGHTS = ["mix_norm_w", "w_in", "gate_bias", "gmlp_ln_w", "gmlp_ln_b", "gmlp_ws", "gmlp_bs", "ssm_conv_w",
           "ssm_conv_b", "ssm_dt_bias", "ssm_a_log", "ssm_d", "ssm_norm_w", "w_proj_a", "w_proj_b", "w_out",
           "ffn_norm_w", "ffn_w_up", "ffn_conv_w", "ffn_conv_b", "ffn_w_down", "final_norm_w"]
IN_SPLITS = [0, 2048, 4096, 6144, 9216, 9248]


def _columns_from_chips(stack):
    return jnp.transpose(stack, (1, 0, 2)).reshape(stack.shape[1], -1)


def _columns_to_chips(full, parts=N_CHIPS):
    rows, cols = full.shape
    return jnp.transpose(full.reshape(rows, parts, cols // parts), (1, 0, 2))


def kernel(x, mix_norm_w, w_in, gate_bias, gmlp_ln_w, gmlp_ln_b, gmlp_ws, gmlp_bs, ssm_conv_w, ssm_conv_b, ssm_dt_bias, ssm_a_log, ssm_d, ssm_norm_w, w_proj_a, w_proj_b, w_out, ffn_norm_w, ffn_w_up, ffn_conv_w, ffn_conv_b, ffn_w_down, final_norm_w, loss_target, m_mix_norm_w, m_w_in, m_gate_bias, m_gmlp_ln_w, m_gmlp_ln_b, m_gmlp_ws, m_gmlp_bs, m_ssm_conv_w, m_ssm_conv_b, m_ssm_dt_bias, m_ssm_a_log, m_ssm_d, m_ssm_norm_w, m_w_proj_a, m_w_proj_b, m_w_out, m_ffn_norm_w, m_ffn_w_up, m_ffn_conv_w, m_ffn_conv_b, m_ffn_w_down, m_final_norm_w, v_mix_norm_w, v_w_in, v_gate_bias, v_gmlp_ln_w, v_gmlp_ln_b, v_gmlp_ws, v_gmlp_bs, v_ssm_conv_w, v_ssm_conv_b, v_ssm_dt_bias, v_ssm_a_log, v_ssm_d, v_ssm_norm_w, v_w_proj_a, v_w_proj_b, v_w_out, v_ffn_norm_w, v_ffn_w_up, v_ffn_conv_w, v_ffn_conv_b, v_ffn_w_down, v_final_norm_w):
    args = dict(locals())
    weights = {n: args[n] for n in WEIGHTS}
    moments_m = {n: args["m_" + n] for n in WEIGHTS}
    moments_v = {n: args["v_" + n] for n in WEIGHTS}
    chip = 2 * lax.axis_index("x") + lax.axis_index("y")

    shards = [weights[n][0].astype(BF16) for n in BIG] + [weights[n][0] for n in SMALL_SHARDED]
    gathered = _gather_chips_split("gather_weights", shards[:len(BIG)], shards[len(BIG):])
    gathered = [_own_slot(stack, own) for stack, own in zip(gathered, shards)]
    w_in_s, w_pa_s, w_pb_s, w_out_s, w_up_s, w_down_s, gb_s, scw_s, fcw_s = gathered
    w_in_full = _columns_from_chips(w_in_s)
    full = {"w_" + nm: w_in_full[:, IN_SPLITS[k]:IN_SPLITS[k + 1]] for k, nm in enumerate(["g", "za", "z", "xbc", "dt"])}
    full["w_proj_a"] = w_pa_s.reshape(-1, D_MODEL)
    full["w_proj_b"] = w_pb_s.reshape(-1, D_MODEL)
    full["w_out"] = w_out_s.reshape(-1, D_MODEL)
    full["w_down"] = w_down_s.reshape(-1, D_MODEL)
    full["w_up_g"] = _columns_from_chips(w_up_s[:2])
    full["w_up_v"] = _columns_from_chips(w_up_s[2:])
    full["gate_bias"] = _columns_from_chips(gb_s)
    full["ssm_conv_w"] = _columns_from_chips(scw_s)
    full["ffn_conv_w"] = _columns_from_chips(fcw_s)
    for n in SMALL:
        if n not in SMALL_SHARDED:
            full[n] = weights[n] if n == "final_norm_w" else weights[n][0]
    for n in ("mix_norm_w", "ffn_norm_w", "ssm_conv_b", "ssm_dt_bias", "ssm_a_log", "ssm_d", "ssm_norm_w", "ffn_conv_b"):
        full[n] = full[n].reshape(1, -1)

    loss_part, grad_x, g = _local_step(x[0], loss_target[0], full)

    small_shapes = [(1, LANES)] + [g[n].shape for n in SMALL]
    reduced = _unpack(_all_reduce("reduce_small", _pack([loss_part] + [g[n] for n in SMALL])), small_shapes)
    loss = reduced[0][0, 0]
    small_grads = {}
    for n, r in zip(SMALL, reduced[1:]):
        if n in SMALL_SHARDED:
            width = weights[n].shape[2]
            r = lax.dynamic_slice_in_dim(r, chip * width, width, axis=1)
        small_grads[n] = r.reshape(weights[n].shape)
    packs = [_pack([d[n] for n in SMALL]) for d in (weights, moments_m, moments_v, small_grads)]
    upd = _adamw("adamw_small", packs[0], [packs[3]], packs[1], packs[2], tm=packs[0].shape[0], rs=SUBLANES)
    small_out = [_unpack(u, [weights[n].shape for n in SMALL]) for u in upd]

    dw_in = jnp.concatenate([g["w_g"], g["w_za"], g["w_z"], g["w_xbc"], g["w_dt"]], axis=1)
    stacks = [
        _columns_to_chips(dw_in),
        g["w_proj_a"].reshape(N_CHIPS, -1, D_MODEL),
        g["w_proj_b"].reshape(N_CHIPS, -1, D_MODEL),
        g["w_out"].reshape(N_CHIPS, -1, D_MODEL),
        jnp.concatenate([_columns_to_chips(g["w_up_g"], 2), _columns_to_chips(g["w_up_v"], 2)], axis=0),
        g["w_down"].reshape(N_CHIPS, -1, D_MODEL),
    ]
    half_tiles = {"w_in": 128, "w_proj_a": 128, "w_proj_b": 256, "w_out": 128, "ffn_w_up": 128, "ffn_w_down": 176}
    tiles = {"w_in": 128, "w_proj_a": 256, "w_proj_b": 256, "w_out": 256, "ffn_w_up": 128, "ffn_w_down": 176}
    core = lax.axis_index("c")
    own_half = [_row_half(s, core, 1) for s in stacks]
    other_half = _swap_cores("pair_grads", [_row_half(s, 1 - core, 1) for s in stacks])
    pair = [_pair_sum("pair_" + n, a, b, tm=half_tiles[n]) for n, a, b in zip(BIG, own_half, other_half)]
    received = _scatter_chips("scatter_grads", pair)
    received = [_own_slot(r, lax.dynamic_index_in_dim(p, chip, 0, keepdims=False)) for r, p in zip(received, pair)]
    halves = [_sum_slots("sum_" + n, r, tm=half_tiles[n], rs=2 * SUBLANES) for n, r in zip(BIG, received)]
    other = _swap_cores("join_grads", halves)
    reduced = [jnp.where(core == 0, jnp.concatenate([a, b], axis=0), jnp.concatenate([b, a], axis=0))
               for a, b in zip(halves, other)]
    big_out = {}
    for n, grad in zip(BIG, reduced):
        big_out[n] = _adamw("adamw_" + n, weights[n][0], [grad], moments_m[n][0], moments_v[n][0],
                            tm=tiles[n], rs=SUBLANES)

    per_kind = [[], [], [], []]
    for n in WEIGHTS:
        for kind in range(4):
            if n in big_out:
                per_kind[kind].append(big_out[n][kind].reshape(weights[n].shape))
            else:
                per_kind[kind].append(small_out[kind][SMALL.index(n)])
    return (loss, grad_x[None], *per_kind[0], *per_kind[1], *per_kind[2], *per_kind[3])
```

```python
import jax
import jax.numpy as jnp
from jax import lax
from jax.experimental import pallas as pl
from jax.experimental.pallas import tpu as pltpu

F32 = jnp.float32
BF16 = jnp.bfloat16
MESH = pl.DeviceIdType.MESH

EPS = 1e-5
D_MODEL = 1024
GMLP_BLOCK = 128
GMLP_GROUPS = 8
CHUNK = 64
SSM_INNER = 2048
SSM_HEADS = 32
SSM_HEAD_DIM = 64
SSM_GROUPS = 4
SSM_HPG = 8
SSM_STATE = 128
SSM_CONV = 4
SSM_XBC = 3072
D_FF = 2816
FFN_CONV = 3
N_CHIPS = 4
N_DEV = 8

ADAM_LR = 0.001
ADAM_B1 = 0.9
ADAM_B2 = 0.999
ADAM_EPS = 1e-08
ADAM_WD = 0.01
ADAM_STEP = 10

VMEM_LIMIT_V7X = 56 * 1024 * 1024
SUBLANES = 8
LANES = 128


def _params(sem=None):
    return pltpu.CompilerParams(dimension_semantics=sem, vmem_limit_bytes=VMEM_LIMIT_V7X)


def _dot(a, b, ca=1, cb=0):
    return lax.dot_general(a.astype(BF16), b.astype(BF16), (((ca,), (cb,)), ((), ())),
                           preferred_element_type=F32)


def _mm(name, a, b, *, ta=False, tb=False, out_dtype=F32, bm, bn, bk, res=None):
    m, k = (a.shape[1], a.shape[0]) if ta else a.shape
    k2, n = (b.shape[1], b.shape[0]) if tb else b.shape
    assert k == k2 and m % bm == 0 and n % bn == 0 and k % bk == 0, (name, a.shape, b.shape)
    nk = k // bk
    a_spec = (pl.BlockSpec((bk, bm), lambda i, j, kk: (kk, i)) if ta
              else pl.BlockSpec((bm, bk), lambda i, j, kk: (i, kk)))
    b_spec = (pl.BlockSpec((bn, bk), lambda i, j, kk: (j, kk)) if tb
              else pl.BlockSpec((bk, bn), lambda i, j, kk: (kk, j)))
    o_spec = pl.BlockSpec((bm, bn), lambda i, j, kk: (i, j))
    has_res = res is not None

    def body(*refs):
        a_ref, b_ref = refs[0], refs[1]
        r_ref = refs[2] if has_res else None
        o_ref = refs[3] if has_res else refs[2]
        p = _dot(a_ref[...], b_ref[...], 0 if ta else 1, 1 if tb else 0)

        def finish(total):
            if has_res:
                total = total + r_ref[...]
            o_ref[...] = total.astype(out_dtype)

        if nk == 1:
            finish(p)
        else:
            acc_ref = refs[-1]
            kk = pl.program_id(2)

            @pl.when(kk == 0)
            def _():
                acc_ref[...] = p

            @pl.when(kk > 0)
            def _():
                acc_ref[...] += p

            @pl.when(kk == nk - 1)
            def _():
                finish(acc_ref[...])

    return pl.pallas_call(
        body, name=name,
        grid=(m // bm, n // bn, nk),
        in_specs=[a_spec, b_spec] + ([o_spec] if has_res else []),
        out_specs=o_spec,
        out_shape=jax.ShapeDtypeStruct((m, n), out_dtype),
        scratch_shapes=[pltpu.VMEM((bm, bn), F32)] if nk > 1 else [],
        compiler_params=_params(("parallel", "parallel", "arbitrary")),
    )(*([a, b] + ([res] if has_res else [])))


def _rows(name, fn, ins, params, outs, accs, *, tm, rs):
    nrow = ins[0][0].shape[-2]
    assert nrow % tm == 0 and tm % rs == 0, (name, nrow, tm, rs)
    n_in, n_p, n_out, n_acc = len(ins), len(params), len(outs), len(accs)
    in_specs = []
    for spec in ins:
        arr, width, cb = spec[:3]
        if len(spec) == 4:
            in_specs.append(pl.BlockSpec((None, tm, width), lambda i, cb=cb, lead=spec[3]: (lead, i, cb)))
        else:
            in_specs.append(pl.BlockSpec((tm, width), lambda i, cb=cb: (i, cb)))
    for p in params:
        in_specs.append(pl.BlockSpec(p.shape, lambda i, nd=p.ndim: (0,) * nd))
    out_specs = [pl.BlockSpec((tm, w), lambda i: (i, 0)) for w, _ in outs]
    out_specs += [pl.BlockSpec(s, lambda i: (0, 0)) for s in accs]
    out_shape = [jax.ShapeDtypeStruct((nrow, w), dt) for w, dt in outs]
    out_shape += [jax.ShapeDtypeStruct(s, F32) for s in accs]

    def body(*refs):
        in_refs = refs[:n_in]
        p_refs = refs[n_in:n_in + n_p]
        o_refs = refs[n_in + n_p:n_in + n_p + n_out]
        a_refs = refs[n_in + n_p + n_out:]
        pv = [p[...] for p in p_refs]

        if n_acc:
            @pl.when(pl.program_id(0) == 0)
            def _():
                for a_ref in a_refs:
                    a_ref[...] = jnp.zeros_like(a_ref)

        def step(r, carry):
            sl = pl.ds(pl.multiple_of(r * rs, rs), rs)
            vals = [ref[sl, :].astype(F32) for ref in in_refs]
            row_out, sums = fn(*vals, *pv)
            for o_ref, v in zip(o_refs, row_out):
                o_ref[sl, :] = v.astype(o_ref.dtype)
            return tuple(c + s for c, s in zip(carry, sums))

        init = tuple(jnp.zeros(s, F32) for s in accs)
        total = lax.fori_loop(0, tm // rs, step, init)
        for a_ref, t in zip(a_refs, total):
            a_ref[...] += t

    res = pl.pallas_call(
        body, name=name, grid=(nrow // tm,),
        in_specs=in_specs, out_specs=out_specs, out_shape=out_shape,
        compiler_params=_params(("arbitrary",)),
    )(*([s[0] for s in ins] + list(params)))
    return res


def _rms(x, w):
    return x * lax.rsqrt(jnp.mean(x * x, axis=-1, keepdims=True) + EPS) * w


def _colsum(v):
    return jnp.sum(v, axis=0, keepdims=True)


def _rms_fwd(name, x, w):
    def fn(xv, wv):
        return (_rms(xv, wv),), ()
    return _rows(name, fn, [(x, D_MODEL, 0)], [w], [(D_MODEL, BF16)], [], tm=512, rs=16)[0]


def _rms_bwd(name, x, w, dy, dres):
    def fn(xv, dyv, drv, wv):
        _, vjp = jax.vjp(_rms, xv, wv)
        dx, dw = vjp(dyv)
        return (drv + dx,), (dw,)
    return _rows(name, fn, [(x, D_MODEL, 0), (dy, D_MODEL, 0), (dres, D_MODEL, 0)], [w],
                 [(D_MODEL, F32)], [(1, D_MODEL)], tm=512, rs=16)


def _final_loss(name, h, target, w):
    def fn(hv, tv, wv):
        y, vjp = jax.vjp(_rms, hv, wv)
        err = y - tv
        part = 0.5 * jnp.sum(jnp.mean(err * err, axis=-1, keepdims=True), axis=0, keepdims=True)
        dh, dw = vjp(err / D_MODEL)
        return (dh,), (jnp.broadcast_to(part, (1, LANES)), dw)
    return _rows(name, fn, [(h, D_MODEL, 0), (target, D_MODEL, 0)], [w],
                 [(D_MODEL, F32)], [(1, LANES), (1, D_MODEL)], tm=512, rs=16)


def _merge(ga, gb, ya, yb, b0, b1):
    return jax.nn.sigmoid(ga + b0) * ya + jax.nn.sigmoid(gb + b1) * yb


def _merge_fwd(name, g, ya, yb, b0, b1):
    def fn(ga, gb, yav, ybv, b0v, b1v):
        return (_merge(ga, gb, yav, ybv, b0v, b1v),), ()
    return _rows(name, fn, [(g, D_MODEL, 0), (g, D_MODEL, 1), (ya, D_MODEL, 0), (yb, D_MODEL, 0)],
                 [b0, b1], [(D_MODEL, BF16)], [], tm=512, rs=16)[0]


def _merge_bwd(name, g, ya, yb, dm, b0, b1):
    def fn(ga, gb, yav, ybv, dmv, b0v, b1v):
        _, vjp = jax.vjp(_merge, ga, gb, yav, ybv, b0v, b1v)
        dga, dgb, dya, dyb, db0, db1 = vjp(dmv)
        return (jnp.concatenate([dga, dgb], axis=1), dya, dyb), (db0, db1)
    return _rows(name, fn,
                 [(g, D_MODEL, 0), (g, D_MODEL, 1), (ya, D_MODEL, 0), (yb, D_MODEL, 0), (dm, D_MODEL, 0)],
                 [b0, b1], [(2 * D_MODEL, BF16), (D_MODEL, BF16), (D_MODEL, BF16)],
                 [(1, D_MODEL), (1, D_MODEL)], tm=512, rs=16)


GROUP_W = SSM_INNER // SSM_GROUPS


def _gate_norm_group(y, z, nw):
    v = y * jax.nn.silu(z)
    return v * lax.rsqrt(jnp.mean(v * v, axis=-1, keepdims=True) + EPS) * nw


def _gate_norm_fwd(name, y, z, nw):
    def fn(yv, zv, nwv):
        parts = [_gate_norm_group(yv[:, k * GROUP_W:(k + 1) * GROUP_W], zv[:, k * GROUP_W:(k + 1) * GROUP_W],
                                  nwv[:, k * GROUP_W:(k + 1) * GROUP_W]) for k in range(SSM_GROUPS)]
        return (jnp.concatenate(parts, axis=1),), ()
    return _rows(name, fn, [(y, SSM_INNER, 0), (z, SSM_INNER, 0)], [nw], [(SSM_INNER, BF16)], [],
                 tm=512, rs=16)[0]


def _gate_norm_bwd(name, y, z, dout, nw):
    def fn(yv, zv, dv, nwv):
        dys, dzs, dns = [], [], []
        for k in range(SSM_GROUPS):
            sl = slice(k * GROUP_W, (k + 1) * GROUP_W)
            _, vjp = jax.vjp(_gate_norm_group, yv[:, sl], zv[:, sl], nwv[:, sl])
            dy, dz, dn = vjp(dv[:, sl])
            dys.append(dy), dzs.append(dz), dns.append(dn)
        return (jnp.concatenate(dys, axis=1), jnp.concatenate(dzs, axis=1)), (jnp.concatenate(dns, axis=1),)
    return _rows(name, fn, [(y, SSM_INNER, 0), (z, SSM_INNER, 0), (dout, SSM_INNER, 0)], [nw],
                 [(SSM_INNER, F32), (SSM_INNER, BF16)], [(1, SSM_INNER)], tm=512, rs=16)


def _softplus(v):
    return jnp.maximum(v, 0.0) + jnp.log1p(jnp.exp(-jnp.abs(v)))


def _chunk_cumsum(v, reverse=False):
    row = lax.broadcasted_iota(jnp.int32, v.shape, 0)
    step = 1
    while step < CHUNK:
        if reverse:
            shifted = pltpu.roll(v, CHUNK - step, axis=0)
            v = v + jnp.where(row < CHUNK - step, shifted, 0.0)
        else:
            shifted = pltpu.roll(v, step, axis=0)
            v = v + jnp.where(row >= step, shifted, 0.0)
        step *= 2
    return v


def _dt_prep(name, dt_raw, dt_bias, a_log):
    def fn(rv, bv, alv):
        dt = _softplus(rv + bv)
        return (dt, _chunk_cumsum(dt * (-jnp.exp(alv)))), ()
    return _rows(name, fn, [(dt_raw, SSM_HEADS, 0)], [dt_bias, a_log],
                 [(SSM_HEADS, F32), (SSM_HEADS, F32)], [], tm=512, rs=CHUNK)


def _dt_bwd(name, dt_raw, ddt, da1, da2, dt_bias, a_log):
    def fn(rv, ddv, d1, d2, bv, alv):
        pre = rv + bv
        dt = _softplus(pre)
        a_neg = -jnp.exp(alv)
        back = _chunk_cumsum(d1 + d2, reverse=True)
        d_dt = ddv + back * a_neg
        d_raw = d_dt * jax.nn.sigmoid(pre)
        return (d_raw,), (_colsum(d_raw), _colsum(back * dt) * a_neg)
    return _rows(name, fn, [(dt_raw, SSM_HEADS, 0), (ddt, SSM_HEADS, 0), (da1, SSM_HEADS, 0), (da2, SSM_HEADS, 0)],
                 [dt_bias, a_log], [(SSM_HEADS, BF16)], [(1, SSM_HEADS), (1, SSM_HEADS)], tm=512, rs=CHUNK)


def _adamw(name, w, g_parts, m, v, *, tm, rs):
    width = w.shape[1]
    n_g = len(g_parts)

    def fn(wv, mv, vv, *gs):
        g = gs[0]
        for extra in gs[1:]:
            g = g + extra
        m_new = ADAM_B1 * mv + (1.0 - ADAM_B1) * g
        v_new = ADAM_B2 * vv + (1.0 - ADAM_B2) * jnp.square(g)
        m_hat = m_new / (1.0 - ADAM_B1 ** ADAM_STEP)
        v_hat = v_new / (1.0 - ADAM_B2 ** ADAM_STEP)
        delta = -ADAM_LR * (m_hat / (jnp.sqrt(v_hat) + ADAM_EPS) + ADAM_WD * wv)
        return (g, delta, m_new, v_new), ()
    assert n_g >= 1
    return _rows(name, fn, [(w, width, 0), (m, width, 0), (v, width, 0)] + [(g, width, 0) for g in g_parts],
                 [], [(width, F32)] * 4, [], tm=tm, rs=rs)


def _pair_sum(name, a, b, *, tm):
    shape = a.shape
    flat = (shape[0] * shape[1], shape[2])

    def fn(av, bv):
        return (av.astype(F32) + bv.astype(F32),), ()
    out = _rows(name, fn, [(a.reshape(flat), flat[1], 0), (b.reshape(flat), flat[1], 0)], [], [(flat[1], BF16)], [],
                tm=tm, rs=2 * SUBLANES)[0]
    return out.reshape(shape)


def _sum_slots(name, stack, *, tm, rs):
    width = stack.shape[2]

    def fn(*slots):
        s0, s1, s2, s3 = (s.astype(F32) for s in slots)
        return (((s0 + s1) + s2) + s3,), ()
    return _rows(name, fn, [(stack, width, 0, k) for k in range(N_CHIPS)], [], [(width, F32)], [],
                 tm=tm, rs=rs)[0]


def _layernorm(v, w, b):
    mu = jnp.mean(v, axis=-1, keepdims=True)
    var = jnp.mean(jnp.square(v - mu), axis=-1, keepdims=True)
    return (v - mu) * lax.rsqrt(var + EPS) * w + b


def _gmlp_mask():
    t = lax.broadcasted_iota(jnp.int32, (GMLP_BLOCK, GMLP_BLOCK), 0) // CHUNK
    s = lax.broadcasted_iota(jnp.int32, (GMLP_BLOCK, GMLP_BLOCK), 1) // CHUNK
    return s <= t


GMLP_TM = 512


def _gmlp_fwd(name, za, ln_w, ln_b, ws, bs_col):
    nrow = za.shape[0]
    tm = GMLP_TM
    width = GMLP_GROUPS * GMLP_BLOCK

    def body(za_ref, lnw_ref, lnb_ref, ws_ref, bs_ref, o_ref, wm_ref):
        mask = _gmlp_mask()
        for g in range(GMLP_GROUPS):
            wm_ref[g] = jnp.where(mask, ws_ref[g], 0.0).astype(BF16)

        def block(n, carry):
            rows = pl.ds(pl.multiple_of(n * GMLP_BLOCK, GMLP_BLOCK), GMLP_BLOCK)
            for g in range(GMLP_GROUPS):
                cols = slice(g * GMLP_BLOCK, (g + 1) * GMLP_BLOCK)
                vcols = slice(width + g * GMLP_BLOCK, width + (g + 1) * GMLP_BLOCK)
                u = jax.nn.gelu(za_ref[rows, cols].astype(F32))
                v = jax.nn.gelu(za_ref[rows, vcols].astype(F32))
                vn = _layernorm(v, lnw_ref[g:g + 1, :], lnb_ref[g:g + 1, :])
                sv = _dot(wm_ref[g], vn) + bs_ref[g]
                o_ref[rows, cols] = (u * sv).astype(o_ref.dtype)
            return carry

        lax.fori_loop(0, tm // GMLP_BLOCK, block, 0)

    small = lambda a: pl.BlockSpec(a.shape, lambda i, nd=a.ndim: (0,) * nd)
    return pl.pallas_call(
        body, name=name, grid=(nrow // tm,),
        in_specs=[pl.BlockSpec((tm, 2 * width), lambda i: (i, 0)), small(ln_w), small(ln_b), small(ws), small(bs_col)],
        out_specs=pl.BlockSpec((tm, width), lambda i: (i, 0)),
        out_shape=jax.ShapeDtypeStruct((nrow, width), BF16),
        scratch_shapes=[pltpu.VMEM((GMLP_GROUPS, GMLP_BLOCK, GMLP_BLOCK), BF16)],
        compiler_params=_params(("arbitrary",)),
    )(za, ln_w, ln_b, ws, bs_col)


def _gmlp_bwd(name, za, dout, ln_w, ln_b, ws, bs_col):
    nrow = za.shape[0]
    tm = GMLP_TM
    width = GMLP_GROUPS * GMLP_BLOCK

    def body(za_ref, do_ref, lnw_ref, lnb_ref, ws_ref, bs_ref, dza_ref, dlnw_ref, dlnb_ref, dws_ref, dbs_ref, wm_ref):
        mask = _gmlp_mask()
        for g in range(GMLP_GROUPS):
            wm_ref[g] = jnp.where(mask, ws_ref[g], 0.0).astype(BF16)

        @pl.when(pl.program_id(0) == 0)
        def _():
            dlnw_ref[...] = jnp.zeros_like(dlnw_ref)
            dlnb_ref[...] = jnp.zeros_like(dlnb_ref)
            dws_ref[...] = jnp.zeros_like(dws_ref)
            dbs_ref[...] = jnp.zeros_like(dbs_ref)

        def block(n, carry):
            rows = pl.ds(pl.multiple_of(n * GMLP_BLOCK, GMLP_BLOCK), GMLP_BLOCK)
            for g in range(GMLP_GROUPS):
                cols = slice(g * GMLP_BLOCK, (g + 1) * GMLP_BLOCK)
                vcols = slice(width + g * GMLP_BLOCK, width + (g + 1) * GMLP_BLOCK)
                u, gelu_u_vjp = jax.vjp(jax.nn.gelu, za_ref[rows, cols].astype(F32))
                v, gelu_v_vjp = jax.vjp(jax.nn.gelu, za_ref[rows, vcols].astype(F32))
                vn, ln_vjp = jax.vjp(_layernorm, v, lnw_ref[g:g + 1, :], lnb_ref[g:g + 1, :])
                sv = _dot(wm_ref[g], vn) + bs_ref[g]
                d_o = do_ref[rows, cols].astype(F32)
                dsv = d_o * u
                d_wm = _dot(dsv, vn, 1, 1)
                dvn = _dot(wm_ref[g], dsv, 0, 0)
                dv, dlnw, dlnb = ln_vjp(dvn)
                dza_ref[rows, cols] = gelu_u_vjp(d_o * sv)[0].astype(dza_ref.dtype)
                dza_ref[rows, vcols] = gelu_v_vjp(dv)[0].astype(dza_ref.dtype)
                dlnw_ref[g:g + 1, :] += dlnw
                dlnb_ref[g:g + 1, :] += dlnb
                dws_ref[g] += jnp.where(mask, d_wm, 0.0)
                dbs_ref[g] += jnp.sum(dsv, axis=1, keepdims=True)
            return carry

        lax.fori_loop(0, tm // GMLP_BLOCK, block, 0)

    small = lambda a: pl.BlockSpec(a.shape, lambda i, nd=a.ndim: (0,) * nd)
    return pl.pallas_call(
        body, name=name, grid=(nrow // tm,),
        in_specs=[pl.BlockSpec((tm, 2 * width), lambda i: (i, 0)), pl.BlockSpec((tm, width), lambda i: (i, 0)),
                  small(ln_w), small(ln_b), small(ws), small(bs_col)],
        out_specs=[pl.BlockSpec((tm, 2 * width), lambda i: (i, 0)), small(ln_w), small(ln_b), small(ws), small(bs_col)],
        out_shape=[jax.ShapeDtypeStruct((nrow, 2 * width), BF16), jax.ShapeDtypeStruct(ln_w.shape, F32),
                   jax.ShapeDtypeStruct(ln_b.shape, F32), jax.ShapeDtypeStruct(ws.shape, F32),
                   jax.ShapeDtypeStruct(bs_col.shape, F32)],
        scratch_shapes=[pltpu.VMEM((GMLP_GROUPS, GMLP_BLOCK, GMLP_BLOCK), BF16)],
        compiler_params=_params(("arbitrary",)),
    )(za, dout, ln_w, ln_b, ws, bs_col)


CONV_TM = 256
CONV_RS = 32
HALO = 2 * SUBLANES


def _tap_rows(w_ref):
    return [w_ref[k:k + 1, :] for k in range(w_ref.shape[0])]


def _conv_rows(win, w, rs):
    taps = len(w)
    out = w[taps - 1] * win[HALO:, :]
    for k in range(taps - 1):
        back = taps - 1 - k
        out = out + w[k] * pltpu.roll(win, back, axis=0)[HALO:, :]
    return out


def _conv_t_rows(win, w, rs):
    taps = len(w)
    out = w[taps - 1] * win[:rs, :]
    for k in range(taps - 1):
        ahead = taps - 1 - k
        out = out + w[k] * pltpu.roll(win, rs + HALO - ahead, axis=0)[:rs, :]
    return out


def _conv_dw_rows(d, xwin, taps):
    rows = []
    for k in range(taps):
        back = taps - 1 - k
        xs = xwin[HALO:, :] if back == 0 else pltpu.roll(xwin, back, axis=0)[HALO:, :]
        rows.append(jnp.sum(d * xs, axis=0, keepdims=True))
    return rows


def _halo_specs(nrow, tm, tc):
    per = tm // HALO
    last = nrow // HALO - 1
    main = pl.BlockSpec((tm, tc), lambda j, i: (i, j))
    before = pl.BlockSpec((HALO, tc), lambda j, i: (jnp.maximum(i * per - 1, 0), j))
    after = pl.BlockSpec((HALO, tc), lambda j, i: (jnp.minimum((i + 1) * per, last), j))
    return main, before, after


def _col_spec(rows, tc):
    return pl.BlockSpec((rows, tc), lambda j, i: (0, j))


def _conv_fwd(name, x, w, b, *, tc):
    nrow, ncol = x.shape
    taps = w.shape[0]
    tm, rs = CONV_TM, CONV_RS
    main, before, _ = _halo_specs(nrow, tm, tc)

    def body(x_ref, xb_ref, w_ref, b_ref, o_ref):
        first = pl.program_id(1) == 0
        wv, bv = _tap_rows(w_ref), b_ref[...]

        def step(r, prev):
            sl = pl.ds(pl.multiple_of(r * rs, rs), rs)
            cur = x_ref[sl, :].astype(F32)
            o_ref[sl, :] = (_conv_rows(jnp.concatenate([prev, cur], axis=0), wv, rs) + bv).astype(o_ref.dtype)
            return cur[rs - HALO:, :]

        lax.fori_loop(0, tm // rs, step, jnp.where(first, 0.0, xb_ref[...].astype(F32)))

    return pl.pallas_call(
        body, name=name, grid=(ncol // tc, nrow // tm),
        in_specs=[main, before, _col_spec(taps, tc), _col_spec(1, tc)],
        out_specs=main, out_shape=jax.ShapeDtypeStruct((nrow, ncol), BF16),
        compiler_params=_params(("parallel", "arbitrary")),
    )(x, x, w, b)


def _conv_bwd(name, dpre, x, w, *, tc):
    nrow, ncol = x.shape
    taps = w.shape[0]
    tm, rs = CONV_TM, CONV_RS
    nsub = tm // rs
    main, before, after = _halo_specs(nrow, tm, tc)

    def body(d_ref, da_ref, x_ref, xb_ref, w_ref, dx_ref, dw_ref, db_ref):
        i = pl.program_id(1)
        first, last = i == 0, i == pl.num_programs(1) - 1
        wv = _tap_rows(w_ref)
        x_before = jnp.where(first, 0.0, xb_ref[...].astype(F32))

        @pl.when(first)
        def _():
            dw_ref[...] = jnp.zeros_like(dw_ref)
            db_ref[...] = jnp.zeros_like(db_ref)

        def step(q, carry):
            nxt, dw, db = carry
            r = nsub - 1 - q
            sl = pl.ds(pl.multiple_of(r * rs, rs), rs)
            cur = d_ref[sl, :].astype(F32)
            dx_ref[sl, :] = _conv_t_rows(jnp.concatenate([cur, nxt], axis=0), wv, rs).astype(dx_ref.dtype)
            inner = x_ref[pl.ds(pl.multiple_of(jnp.maximum(r * rs - HALO, 0), HALO), HALO), :].astype(F32)
            xwin = jnp.concatenate([jnp.where(r == 0, x_before, inner), x_ref[sl, :].astype(F32)], axis=0)
            dw = tuple(a + s for a, s in zip(dw, _conv_dw_rows(cur, xwin, taps)))
            return cur[:HALO, :], dw, db + _colsum(cur)

        zero_row = jnp.zeros((1, tc), F32)
        init = (jnp.where(last, 0.0, da_ref[...].astype(F32)), (zero_row,) * taps, zero_row)
        _, dw, db = lax.fori_loop(0, nsub, step, init)
        for k in range(taps):
            dw_ref[k:k + 1, :] += dw[k]
        db_ref[...] += db

    return pl.pallas_call(
        body, name=name, grid=(ncol // tc, nrow // tm),
        in_specs=[main, after, main, before, _col_spec(taps, tc)],
        out_specs=[main, _col_spec(taps, tc), _col_spec(1, tc)],
        out_shape=[jax.ShapeDtypeStruct((nrow, ncol), BF16), jax.ShapeDtypeStruct((taps, ncol), F32),
                   jax.ShapeDtypeStruct((1, ncol), F32)],
        compiler_params=_params(("parallel", "arbitrary")),
    )(dpre, dpre, x, x, w)


def _glu(gate, val):
    return jax.nn.silu(gate) * val


def _ffn_act_fwd(name, pg, pv, wg, wv, bg, bv, *, tc):
    nrow, ncol = pg.shape
    taps = wg.shape[0]
    tm, rs = CONV_TM, CONV_RS
    main, before, _ = _halo_specs(nrow, tm, tc)

    def body(pg_ref, pgb_ref, pv_ref, pvb_ref, wg_ref, wv_ref, bg_ref, bv_ref, g_ref, v_ref, a_ref):
        first = pl.program_id(1) == 0
        wgv, wvv, bgv, bvv = _tap_rows(wg_ref), _tap_rows(wv_ref), bg_ref[...], bv_ref[...]

        def step(r, carry):
            prev_g, prev_v = carry
            sl = pl.ds(pl.multiple_of(r * rs, rs), rs)
            cur_g, cur_v = pg_ref[sl, :].astype(F32), pv_ref[sl, :].astype(F32)
            gate = _conv_rows(jnp.concatenate([prev_g, cur_g], axis=0), wgv, rs) + bgv
            val = _conv_rows(jnp.concatenate([prev_v, cur_v], axis=0), wvv, rs) + bvv
            g_ref[sl, :] = gate.astype(g_ref.dtype)
            v_ref[sl, :] = val.astype(v_ref.dtype)
            a_ref[sl, :] = _glu(gate, val).astype(a_ref.dtype)
            return cur_g[rs - HALO:, :], cur_v[rs - HALO:, :]

        lax.fori_loop(0, tm // rs, step, (jnp.where(first, 0.0, pgb_ref[...].astype(F32)),
                                          jnp.where(first, 0.0, pvb_ref[...].astype(F32))))

    return pl.pallas_call(
        body, name=name, grid=(ncol // tc, nrow // tm),
        in_specs=[main, before, main, before, _col_spec(taps, tc), _col_spec(taps, tc), _col_spec(1, tc), _col_spec(1, tc)],
        out_specs=[main, main, main],
        out_shape=[jax.ShapeDtypeStruct((nrow, ncol), BF16)] * 3,
        compiler_params=_params(("parallel", "arbitrary")),
    )(pg, pg, pv, pv, wg, wv, bg, bv)


def _ffn_act_bwd(name, dact, gate, val):
    def fn(dv, gv, vv):
        _, vjp = jax.vjp(_glu, gv, vv)
        dg, dval = vjp(dv)
        return (dg, dval), ()
    width = dact.shape[1]
    return _rows(name, fn, [(dact, width, 0), (gate, width, 0), (val, width, 0)], [],
                 [(width, BF16), (width, BF16)], [], tm=256, rs=2 * SUBLANES)


SSD_TM = 256
SSD_CHUNKS = SSD_TM // CHUNK
X_OFF, B_OFF, C_OFF = 0, SSM_INNER, SSM_INNER + SSM_GROUPS * SSM_STATE
HP = SSM_HPG * SSM_HEAD_DIM


def _causal_tiled():
    row = lax.broadcasted_iota(jnp.int32, (CHUNK, HP), 0)
    src = lax.broadcasted_iota(jnp.int32, (CHUNK, HP), 1) & (CHUNK - 1)
    return src <= row


def _split3(v):
    hi = v.astype(BF16)
    rest = v - hi.astype(F32)
    mid = rest.astype(BF16)
    lo = (rest - mid.astype(F32)).astype(BF16)
    return hi, mid, lo


def _dot_exact(a, ind):
    parts = [lax.dot_general(p, ind, (((1,), (0,)), ((), ())), preferred_element_type=F32) for p in _split3(a)]
    return (parts[0] + parts[1]) + parts[2]


def _head_indicator():
    head = lax.broadcasted_iota(jnp.int32, (SSM_HEADS, SSM_INNER), 0)
    chan = lax.broadcasted_iota(jnp.int32, (SSM_HEADS, SSM_INNER), 1)
    return (chan // SSM_HEAD_DIM == head).astype(BF16)


def _chunk_decays(ci, dt_ref, ac_ref, ind, ax_ref, dtx_ref, eax_ref, eex_ref, tail_ref):
    rows = pl.ds(pl.multiple_of(ci * CHUNK, CHUNK), CHUNK)
    ax_ref[...] = _dot_exact(ac_ref[rows, :], ind)
    dtx_ref[...] = _dot_exact(dt_ref[rows, :], ind)
    eax_ref[...] = jnp.exp(ax_ref[...])
    eex_ref[...] = jnp.exp(ax_ref[CHUNK - 1:CHUNK, :] - ax_ref[...])
    tail = pl.ds(pl.multiple_of(ci * CHUNK + CHUNK - SUBLANES, SUBLANES), SUBLANES)
    tail_ref[...] = jnp.exp(ac_ref[tail, :])


def _group_decay(ci, g, ax_ref, af_ref, xbc_ref, causal):
    gcols = slice(g * HP, (g + 1) * HP)
    bm = xbc_ref[:, B_OFF + g * SSM_STATE:B_OFF + (g + 1) * SSM_STATE]
    cm = xbc_ref[:, C_OFF + g * SSM_STATE:C_OFF + (g + 1) * SSM_STATE]
    cb_tiled = _dot(cm, jnp.concatenate([bm] * SSM_HPG, axis=0), 1, 1)
    seg = ax_ref[:, gcols] - af_ref[ci, :, gcols]
    decay = jnp.where(causal, jnp.exp(jnp.where(causal, seg, 0.0)), 0.0)
    return bm, cm, cb_tiled * decay, decay


def _ssd_fwd(name, pre, dt, a_cum, a_flat, d_x, ind):
    nrow = pre.shape[0]
    tm = SSD_TM

    def body(pre_ref, dt_ref, ac_ref, af_ref, dx_ref, ind_ref, y_ref, st_ref,
             h_ref, xbc_ref, ax_ref, dtx_ref, eax_ref, eex_ref, m_ref, xd_ref, yd_ref, tail_ref):
        @pl.when(pl.program_id(0) == 0)
        def _():
            h_ref[...] = jnp.zeros_like(h_ref)

        causal = _causal_tiled()
        ind = ind_ref[...]

        def chunk(ci, carry):
            rows = pl.ds(pl.multiple_of(ci * CHUNK, CHUNK), CHUNK)
            xbc_ref[...] = jax.nn.silu(pre_ref[rows, :].astype(F32))
            _chunk_decays(ci, dt_ref, ac_ref, ind, ax_ref, dtx_ref, eax_ref, eex_ref, tail_ref)
            st_ref[ci] = h_ref[...].astype(st_ref.dtype)
            for g in range(SSM_GROUPS):
                gcols = slice(g * HP, (g + 1) * HP)
                bm, cm, m_all, _ = _group_decay(ci, g, ax_ref, af_ref, xbc_ref, causal)
                m_ref[...] = m_all
                x_g = xbc_ref[:, gcols]
                xd = x_g * dtx_ref[:, gcols]
                xd_ref[...] = xd
                h_g = h_ref[gcols, :]
                for hh in range(SSM_HPG):
                    lc = slice(hh * SSM_HEAD_DIM, (hh + 1) * SSM_HEAD_DIM)
                    yd_ref[:, lc] = _dot(m_ref[:, lc], xd_ref[:, lc])
                y_ref[rows, gcols] = (yd_ref[...] + _dot(cm, h_g, 1, 1) * eax_ref[:, gcols]
                                      + dx_ref[:, gcols] * x_g)
                new = _dot(xd * eex_ref[:, gcols], bm, 0, 0)
                for hh in range(SSM_HPG):
                    h = g * SSM_HPG + hh
                    hrows = slice(h * SSM_HEAD_DIM, (h + 1) * SSM_HEAD_DIM)
                    lrows = slice(hh * SSM_HEAD_DIM, (hh + 1) * SSM_HEAD_DIM)
                    h_ref[hrows, :] = tail_ref[SUBLANES - 1:SUBLANES, h:h + 1] * h_ref[hrows, :] + new[lrows, :]
            return carry

        lax.fori_loop(0, SSD_CHUNKS, chunk, 0)

    nchunk = nrow // CHUNK
    whole = lambda a: pl.BlockSpec(a.shape, lambda i, nd=a.ndim: (0,) * nd)
    wide = lambda: pltpu.VMEM((CHUNK, SSM_INNER), F32)
    group = lambda: pltpu.VMEM((CHUNK, HP), F32)
    return pl.pallas_call(
        body, name=name, grid=(nrow // tm,),
        in_specs=[pl.BlockSpec((tm, SSM_XBC), lambda i: (i, 0)), pl.BlockSpec((tm, SSM_HEADS), lambda i: (i, 0)),
                  pl.BlockSpec((tm, SSM_HEADS), lambda i: (i, 0)),
                  pl.BlockSpec((SSD_CHUNKS, 1, SSM_INNER), lambda i: (i, 0, 0)), whole(d_x), whole(ind)],
        out_specs=[pl.BlockSpec((tm, SSM_INNER), lambda i: (i, 0)),
                   pl.BlockSpec((SSD_CHUNKS, SSM_INNER, SSM_STATE), lambda i: (i, 0, 0))],
        out_shape=[jax.ShapeDtypeStruct((nrow, SSM_INNER), F32),
                   jax.ShapeDtypeStruct((nchunk, SSM_INNER, SSM_STATE), BF16)],
        scratch_shapes=[pltpu.VMEM((SSM_INNER, SSM_STATE), F32), pltpu.VMEM((CHUNK, SSM_XBC), F32),
                        wide(), wide(), wide(), wide(), group(), group(), group(),
                        pltpu.VMEM((SUBLANES, SSM_HEADS), F32)],
        compiler_params=_params(("arbitrary",)),
    )(pre, dt, a_cum, a_flat, d_x, ind)


def _ssd_bwd(name, pre, dt, a_cum, a_flat, d_x, ind, ind_t, states, dy):
    nrow = pre.shape[0]
    tm = SSD_TM
    ntile = nrow // tm

    def body(pre_ref, dt_ref, ac_ref, af_ref, dx_ref, ind_ref, indt_ref, st_ref, dy_ref,
             dpre_ref, ddt_ref, da_ref, daf_ref, dd_ref,
             dh_ref, xbc_ref, dxbc_ref, ax_ref, dtx_ref, eax_ref, eex_ref, ra_ref, ts_ref, r2_ref,
             m_ref, l_ref, xd_ref, dm_ref, dxd_ref, fold_ref, hd_ref, tail_ref):
        @pl.when(pl.program_id(0) == 0)
        def _():
            dh_ref[...] = jnp.zeros_like(dh_ref)
            dd_ref[...] = jnp.zeros_like(dd_ref)

        causal = _causal_tiled()
        ind, ind_t = ind_ref[...], indt_ref[...]
        is_last_row = lax.broadcasted_iota(jnp.int32, (CHUNK, 1), 0) == CHUNK - 1
        ones = jnp.ones((CHUNK, SSM_STATE), BF16)

        def chunk(k, ddx):
            ci = SSD_CHUNKS - 1 - k
            rows = pl.ds(pl.multiple_of(ci * CHUNK, CHUNK), CHUNK)
            pre_v = pre_ref[rows, :].astype(F32)
            xbc_ref[...] = jax.nn.silu(pre_v)
            _chunk_decays(ci, dt_ref, ac_ref, ind, ax_ref, dtx_ref, eax_ref, eex_ref, tail_ref)
            ddx_parts = []
            for g in range(SSM_GROUPS):
                gcols = slice(g * HP, (g + 1) * HP)
                bcols = slice(B_OFF + g * SSM_STATE, B_OFF + (g + 1) * SSM_STATE)
                ccols = slice(C_OFF + g * SSM_STATE, C_OFF + (g + 1) * SSM_STATE)
                bm, cm, m_all, decay = _group_decay(ci, g, ax_ref, af_ref, xbc_ref, causal)
                m_ref[...] = m_all
                l_ref[...] = decay
                x_g = xbc_ref[:, gcols]
                xd = x_g * dtx_ref[:, gcols]
                xd_ref[...] = xd
                h_g = st_ref[ci, gcols, :]
                dh_g = dh_ref[gcols, :]
                dy_g = dy_ref[rows, gcols]
                for hh in range(SSM_HPG):
                    h = g * SSM_HPG + hh
                    hcols = slice(h * SSM_HEAD_DIM, (h + 1) * SSM_HEAD_DIM)
                    lc = slice(hh * SSM_HEAD_DIM, (hh + 1) * SSM_HEAD_DIM)
                    dy_h = dy_ref[rows, hcols]
                    dm_ref[:, lc] = _dot(dy_h, xd_ref[:, lc], 1, 1)
                    dxd_ref[:, lc] = _dot(m_ref[:, lc], dy_h, 0, 0)
                ebdh = eex_ref[:, gcols] * _dot(bm, dh_g, 1, 1)
                dxd = dxd_ref[...] + ebdh
                dm = dm_ref[...]
                t = dm * l_ref[...]
                t128 = (t[:, 0:LANES] + t[:, LANES:2 * LANES]) + (t[:, 2 * LANES:3 * LANES] + t[:, 3 * LANES:])
                fold_ref[...] = t128 + pltpu.roll(t128, CHUNK, axis=1)
                dw_sum = fold_ref[:, 0:CHUNK]
                q = dm * m_ref[...]
                dyea = dy_g * eax_ref[:, gcols]
                ra_ref[:, gcols] = q + dyea * _dot(cm, h_g, 1, 1)
                ts_ref[:, gcols] = xd * ebdh
                r2_ref[:, gcols] = dxd * x_g
                daf_ref[ci, :, gcols] = -jnp.sum(q, axis=0, keepdims=True)
                ddx_parts.append(jnp.sum(dy_g * x_g, axis=0, keepdims=True))
                dxbc_ref[:, gcols] = dxd * dtx_ref[:, gcols] + dx_ref[:, gcols] * dy_g
                dxbc_ref[:, ccols] = _dot(dw_sum, bm) + _dot(dyea, h_g)
                dxbc_ref[:, bcols] = _dot(dw_sum, cm, 0, 0) + _dot(xd * eex_ref[:, gcols], dh_g)
                dh_new = _dot(dyea, cm, 0, 0)
                for hh in range(SSM_HPG):
                    h = g * SSM_HPG + hh
                    hrows = slice(h * SSM_HEAD_DIM, (h + 1) * SSM_HEAD_DIM)
                    lrows = slice(hh * SSM_HEAD_DIM, (hh + 1) * SSM_HEAD_DIM)
                    hd_ref[h:h + 1, :] = jnp.sum(st_ref[ci, hrows, :] * dh_ref[hrows, :], axis=0, keepdims=True)
                    dh_ref[hrows, :] = tail_ref[SUBLANES - 1:SUBLANES, h:h + 1] * dh_ref[hrows, :] + dh_new[lrows, :]
            ra = _dot_exact(ra_ref[...], ind_t)
            ts = _dot_exact(ts_ref[...], ind_t)
            hdh = sum(lax.dot_general(ones, p, (((1,), (1,)), ((), ())), preferred_element_type=F32)
                      for p in _split3(hd_ref[...]))
            da_last = jnp.sum(ts, axis=0, keepdims=True) + tail_ref[SUBLANES - 1:SUBLANES, :] * hdh
            da_ref[rows, :] = ra - ts + jnp.where(is_last_row, da_last, 0.0)
            ddt_ref[rows, :] = _dot_exact(r2_ref[...], ind_t)
            sig = jax.nn.sigmoid(pre_v)
            dpre_ref[rows, :] = (dxbc_ref[...] * (sig * (1.0 + pre_v * (1.0 - sig)))).astype(dpre_ref.dtype)
            return ddx + jnp.concatenate(ddx_parts, axis=1)

        ddx = lax.fori_loop(0, SSD_CHUNKS, chunk, jnp.zeros((1, SSM_INNER), F32))
        dd_ref[...] += _dot_exact(jnp.broadcast_to(ddx, (SUBLANES, SSM_INNER)), ind_t)

    rev = lambda i: ntile - 1 - i
    whole = lambda a: pl.BlockSpec(a.shape, lambda i, nd=a.ndim: (0,) * nd)
    wide = lambda: pltpu.VMEM((CHUNK, SSM_INNER), F32)
    group = lambda: pltpu.VMEM((CHUNK, HP), F32)
    return pl.pallas_call(
        body, name=name, grid=(ntile,),
        in_specs=[pl.BlockSpec((tm, SSM_XBC), lambda i: (rev(i), 0)), pl.BlockSpec((tm, SSM_HEADS), lambda i: (rev(i), 0)),
                  pl.BlockSpec((tm, SSM_HEADS), lambda i: (rev(i), 0)),
                  pl.BlockSpec((SSD_CHUNKS, 1, SSM_INNER), lambda i: (rev(i), 0, 0)),
                  whole(d_x), whole(ind), whole(ind_t),
                  pl.BlockSpec((SSD_CHUNKS, SSM_INNER, SSM_STATE), lambda i: (rev(i), 0, 0)),
                  pl.BlockSpec((tm, SSM_INNER), lambda i: (rev(i), 0))],
        out_specs=[pl.BlockSpec((tm, SSM_XBC), lambda i: (rev(i), 0)), pl.BlockSpec((tm, SSM_HEADS), lambda i: (rev(i), 0)),
                   pl.BlockSpec((tm, SSM_HEADS), lambda i: (rev(i), 0)),
                   pl.BlockSpec((SSD_CHUNKS, 1, SSM_INNER), lambda i: (rev(i), 0, 0)),
                   pl.BlockSpec((SUBLANES, SSM_HEADS), lambda i: (0, 0))],
        out_shape=[jax.ShapeDtypeStruct((nrow, SSM_XBC), BF16), jax.ShapeDtypeStruct((nrow, SSM_HEADS), F32),
                   jax.ShapeDtypeStruct((nrow, SSM_HEADS), F32), jax.ShapeDtypeStruct((nrow // CHUNK, 1, SSM_INNER), F32),
                   jax.ShapeDtypeStruct((SUBLANES, SSM_HEADS), F32)],
        scratch_shapes=[pltpu.VMEM((SSM_INNER, SSM_STATE), F32), pltpu.VMEM((CHUNK, SSM_XBC), F32),
                        pltpu.VMEM((CHUNK, SSM_XBC), F32), wide(), wide(), wide(), wide(), wide(), wide(), wide(),
                        group(), group(), group(), group(), group(), pltpu.VMEM((CHUNK, LANES), F32),
                        pltpu.VMEM((SSM_HEADS, SSM_STATE), F32), pltpu.VMEM((SUBLANES, SSM_HEADS), F32)],
        compiler_params=_params(("arbitrary",)),
    )(pre, dt, a_cum, a_flat, d_x, ind, ind_t, states, dy)


def _local_step(x, target, w):
    g = {}
    bs_col = w["gmlp_bs"].reshape(GMLP_GROUPS, GMLP_BLOCK, 1)
    b0, b1 = w["gate_bias"][0:1], w["gate_bias"][1:2]

    xn = _rms_fwd("mix_norm", x, w["mix_norm_w"])
    big = dict(bm=1024, bn=1024, bk=1024)
    act16 = dict(out_dtype=BF16, **big)
    gates = _mm("in_gates", xn, w["w_g"], **act16)
    za = _mm("in_gmlp", xn, w["w_za"], **act16)
    z = _mm("in_z", xn, w["w_z"], **act16)
    xbc = _mm("in_xbc", xn, w["w_xbc"], **act16)
    dt_raw = _mm("in_dt", xn, w["w_dt"], bm=1024, bn=SSM_HEADS, bk=1024)

    ya_pre = _gmlp_fwd("gmlp_fwd", za, w["gmlp_ln_w"], w["gmlp_ln_b"], w["gmlp_ws"], bs_col)
    y_a = _mm("proj_a", ya_pre, w["w_proj_a"], **act16)

    pre = _conv_fwd("ssm_conv_fwd", xbc, w["ssm_conv_w"], w["ssm_conv_b"], tc=1024)
    dt, a_cum = _dt_prep("dt_prep", dt_raw, w["ssm_dt_bias"], w["ssm_a_log"])
    a_flat = jnp.transpose(a_cum.reshape(-1, CHUNK, SSM_HEADS), (0, 2, 1)).reshape(-1, 1, SSM_INNER)
    d_x = jnp.repeat(w["ssm_d"], SSM_HEAD_DIM, axis=1)
    ind = _head_indicator()
    y_ssd, states = _ssd_fwd("ssd_fwd", pre, dt, a_cum, a_flat, d_x, ind)
    yb_pre = _gate_norm_fwd("gate_norm_fwd", y_ssd, z, w["ssm_norm_w"])
    y_b = _mm("proj_b", yb_pre, w["w_proj_b"], **act16)

    merged = _merge_fwd("merge_fwd", gates, y_a, y_b, b0, b1)
    h1 = _mm("out_proj", merged, w["w_out"], res=x, **big)

    hn = _rms_fwd("ffn_norm", h1, w["ffn_norm_w"])
    half = dict(bm=1024, bn=D_FF // 2, bk=1024, out_dtype=BF16)
    pg = _mm("ffn_up_gate", hn, w["w_up_g"], **half)
    pv = _mm("ffn_up_val", hn, w["w_up_v"], **half)
    cw, cb = w["ffn_conv_w"], w["ffn_conv_b"]
    gate, val, act = _ffn_act_fwd("ffn_act_fwd", pg, pv, cw[:, :D_FF], cw[:, D_FF:], cb[:, :D_FF], cb[:, D_FF:],
                                  tc=D_FF // 2)
    h2 = _mm("ffn_down", act, w["w_down"], res=h1, bm=512, bn=1024, bk=D_FF // 2)

    dh2, loss_part, g["final_norm_w"] = _final_loss("final_loss", h2, target, w["final_norm_w"].reshape(1, D_MODEL))

    dact = _mm("d_act", dh2, w["w_down"], tb=True, **half)
    wgrad = dict(ta=True, bk=512, out_dtype=BF16)
    g["w_down"] = _mm("dw_down", act, dh2, bm=D_FF // 2, bn=1024, **wgrad)
    dgate, dval = _ffn_act_bwd("ffn_act_bwd", dact, gate, val)
    dpg, dcwg, dcbg = _conv_bwd("ffn_conv_bwd_gate", dgate, pg, cw[:, :D_FF], tc=D_FF // 2)
    dpv, dcwv, dcbv = _conv_bwd("ffn_conv_bwd_val", dval, pv, cw[:, D_FF:], tc=D_FF // 2)
    g["ffn_conv_w"] = jnp.concatenate([dcwg, dcwv], axis=1)
    g["ffn_conv_b"] = jnp.concatenate([dcbg, dcbv], axis=1)
    back = dict(bm=1024, bn=1024, bk=D_FF // 2)
    dhn = _mm("d_hn_gate", dpg, w["w_up_g"], tb=True, **back)
    dhn = _mm("d_hn_val", dpv, w["w_up_v"], tb=True, res=dhn, **back)
    g["w_up_g"] = _mm("dw_up_gate", hn, dpg, bm=1024, bn=D_FF // 2, **wgrad)
    g["w_up_v"] = _mm("dw_up_val", hn, dpv, bm=1024, bn=D_FF // 2, **wgrad)
    dh1, g["ffn_norm_w"] = _rms_bwd("ffn_norm_bwd", h1, w["ffn_norm_w"], dhn, dh2)

    dmerged = _mm("d_merged", dh1, w["w_out"], tb=True, **act16)
    g["w_out"] = _mm("dw_out", merged, dh1, bm=1024, bn=1024, **wgrad)
    dgates, dya, dyb, db0, db1 = _merge_bwd("merge_bwd", gates, y_a, y_b, dmerged, b0, b1)
    g["gate_bias"] = jnp.concatenate([db0, db1], axis=0)

    dya_pre = _mm("d_ya_pre", dya, w["w_proj_a"], tb=True, **act16)
    g["w_proj_a"] = _mm("dw_proj_a", ya_pre, dya, bm=1024, bn=1024, **wgrad)
    dza, g["gmlp_ln_w"], g["gmlp_ln_b"], g["gmlp_ws"], dbs = _gmlp_bwd(
        "gmlp_bwd", za, dya_pre, w["gmlp_ln_w"], w["gmlp_ln_b"], w["gmlp_ws"], bs_col)
    g["gmlp_bs"] = dbs.reshape(GMLP_GROUPS, GMLP_BLOCK)

    dyb_pre = _mm("d_yb_pre", dyb, w["w_proj_b"], tb=True, **act16)
    g["w_proj_b"] = _mm("dw_proj_b", yb_pre, dyb, bm=1024, bn=1024, **wgrad)
    dy_ssd, dz, g["ssm_norm_w"] = _gate_norm_bwd("gate_norm_bwd", y_ssd, z, dyb_pre, w["ssm_norm_w"])
    dpre, ddt, da_tok, da_flat, dd = _ssd_bwd("ssd_bwd", pre, dt, a_cum, a_flat, d_x, ind, ind.T, states, dy_ssd)
    g["ssm_d"] = dd[0:1]
    da_src = jnp.transpose(da_flat.reshape(-1, SSM_HEADS, CHUNK), (0, 2, 1)).reshape(-1, SSM_HEADS)
    ddt_raw, g["ssm_dt_bias"], g["ssm_a_log"] = _dt_bwd("dt_bwd", dt_raw, ddt, da_tok, da_src,
                                                         w["ssm_dt_bias"], w["ssm_a_log"])
    dxbc, g["ssm_conv_w"], g["ssm_conv_b"] = _conv_bwd("ssm_conv_bwd", dpre, xbc, w["ssm_conv_w"], tc=1024)

    dxn = _mm("d_xn_gates", dgates, w["w_g"], tb=True, **big)
    dxn = _mm("d_xn_gmlp", dza, w["w_za"], tb=True, res=dxn, **big)
    dxn = _mm("d_xn_z", dz, w["w_z"], tb=True, res=dxn, **big)
    dxn = _mm("d_xn_xbc", dxbc, w["w_xbc"], tb=True, res=dxn, **big)
    dxn = _mm("d_xn_dt", ddt_raw, w["w_dt"], tb=True, res=dxn, bm=1024, bn=1024, bk=SSM_HEADS)
    g["w_g"] = _mm("dw_gates", xn, dgates, bm=1024, bn=1024, **wgrad)
    g["w_za"] = _mm("dw_gmlp", xn, dza, bm=1024, bn=1024, **wgrad)
    g["w_z"] = _mm("dw_z", xn, dz, bm=1024, bn=1024, **wgrad)
    g["w_xbc"] = _mm("dw_xbc", xn, dxbc, bm=1024, bn=1024, **wgrad)
    g["w_dt"] = _mm("dw_dt", xn, ddt_raw, bm=1024, bn=SSM_HEADS, **wgrad)
    grad_x, g["mix_norm_w"] = _rms_bwd("mix_norm_bwd", x, w["mix_norm_w"], dxn, dh1)
    return loss_part, grad_x, g


def _position():
    return lax.axis_index("x"), lax.axis_index("y"), lax.axis_index("c")


def _own_slot(stack, own):
    chip = 2 * lax.axis_index("x") + lax.axis_index("y")
    return lax.dynamic_update_index_in_dim(stack, own, chip, axis=0)


def _scatter_chips(name, arrs):
    n = len(arrs)

    def body(*refs):
        ins, outs = refs[:n], refs[n:2 * n]
        send_sems, recv_sems = refs[2 * n:]
        x, y, c = _position()
        me = 2 * x + y
        peers = [(1 - x, y), (x, 1 - y), (1 - x, 1 - y)]
        sends = []
        for i in range(n):
            for k, (px, py) in enumerate(peers):
                cp = pltpu.make_async_remote_copy(
                    src_ref=ins[i].at[2 * px + py], dst_ref=outs[i].at[me],
                    send_sem=send_sems.at[i, k], recv_sem=recv_sems.at[i, k],
                    device_id=(px, py, c), device_id_type=MESH)
                cp.start()
                sends.append(cp)
        for i in range(n):
            for k, (px, py) in enumerate(peers):
                pltpu.make_async_remote_copy(
                    src_ref=ins[i].at[me], dst_ref=outs[i].at[2 * px + py],
                    send_sem=send_sems.at[i, k], recv_sem=recv_sems.at[i, k],
                    device_id=(px, py, c), device_id_type=MESH).wait_recv()
        for cp in sends:
            cp.wait_send()

    hbm = pl.BlockSpec(memory_space=pl.ANY)
    return pl.pallas_call(
        body, name=name,
        in_specs=[hbm] * n, out_specs=[hbm] * n,
        out_shape=[jax.ShapeDtypeStruct(a.shape, a.dtype) for a in arrs],
        scratch_shapes=[pltpu.SemaphoreType.DMA((n, N_CHIPS - 1)), pltpu.SemaphoreType.DMA((n, N_CHIPS - 1))],
        compiler_params=pltpu.CompilerParams(has_side_effects=True),
    )(*arrs)


def _half_rows(ref_rows, which):
    half = ref_rows // 2
    return pl.ds(pl.multiple_of(which * half, 2 * SUBLANES), half)


def _gather_chips_split(name, split, whole):
    ns, nw = len(split), len(whole)
    arrs = list(split) + list(whole)
    n = ns + nw

    def body(*refs):
        ins, outs = refs[:n], refs[n:2 * n]
        send_sems, recv_sems = refs[2 * n:]
        x, y, c = _position()
        me = 2 * x + y
        sibling = (x, y, 1 - c)
        chips = [(1 - x, y), (x, 1 - y), (1 - x, 1 - y)]

        def remote(i, k, src, dst, to):
            return pltpu.make_async_remote_copy(src_ref=src, dst_ref=dst, send_sem=send_sems.at[i, k],
                                                recv_sem=recv_sems.at[i, k], device_id=to, device_id_type=MESH)

        sends = []
        for i in range(n):
            rows = _half_rows(arrs[i].shape[0], c) if i < ns else slice(None)
            for k, (px, py) in enumerate(chips):
                cp = remote(i, k, ins[i].at[rows], outs[i].at[me, rows], (px, py, c))
                cp.start()
                sends.append(cp)
        for i in range(ns):
            rows = _half_rows(arrs[i].shape[0], c)
            for k, (px, py) in enumerate(chips):
                landed = outs[i].at[2 * px + py, rows]
                remote(i, k, landed, landed, (px, py, c)).wait_recv()
                cp = remote(i, N_CHIPS - 1 + k, landed, landed, sibling)
                cp.start()
                sends.append(cp)
        for i in range(ns, n):
            for k, (px, py) in enumerate(chips):
                landed = outs[i].at[2 * px + py]
                remote(i, k, landed, landed, (px, py, c)).wait_recv()
        for i in range(ns):
            rows = _half_rows(arrs[i].shape[0], 1 - c)
            for k, (px, py) in enumerate(chips):
                landed = outs[i].at[2 * px + py, rows]
                remote(i, N_CHIPS - 1 + k, landed, landed, sibling).wait_recv()
        for cp in sends:
            cp.wait_send()

    hbm = pl.BlockSpec(memory_space=pl.ANY)
    nsem = 2 * (N_CHIPS - 1)
    return pl.pallas_call(
        body, name=name, in_specs=[hbm] * n, out_specs=[hbm] * n,
        out_shape=[jax.ShapeDtypeStruct((N_CHIPS,) + a.shape, a.dtype) for a in arrs],
        scratch_shapes=[pltpu.SemaphoreType.DMA((n, nsem)), pltpu.SemaphoreType.DMA((n, nsem))],
        compiler_params=pltpu.CompilerParams(has_side_effects=True),
    )(*arrs)


def _swap_cores(name, arrs):
    n = len(arrs)

    def body(*refs):
        ins, outs = refs[:n], refs[n:2 * n]
        send_sems, recv_sems = refs[2 * n:]
        x, y, c = _position()
        copies = [pltpu.make_async_remote_copy(src_ref=ins[i], dst_ref=outs[i], send_sem=send_sems.at[i],
                                               recv_sem=recv_sems.at[i], device_id=(x, y, 1 - c), device_id_type=MESH)
                  for i in range(n)]
        for cp in copies:
            cp.start()
        for cp in copies:
            cp.wait_recv()
        for cp in copies:
            cp.wait_send()

    hbm = pl.BlockSpec(memory_space=pl.ANY)
    return pl.pallas_call(
        body, name=name, in_specs=[hbm] * n, out_specs=[hbm] * n,
        out_shape=[jax.ShapeDtypeStruct(a.shape, a.dtype) for a in arrs],
        scratch_shapes=[pltpu.SemaphoreType.DMA((n,)), pltpu.SemaphoreType.DMA((n,))],
        compiler_params=pltpu.CompilerParams(has_side_effects=True),
    )(*arrs)


def _row_half(a, which, axis):
    half = a.shape[axis] // 2
    return lax.dynamic_slice_in_dim(a, which * half, half, axis=axis)


def _all_reduce(name, pack):
    def body(in_ref, out_ref, buf, send_sems, recv_sems):
        x, y, c = _position()
        me = 4 * x + 2 * y + c
        flips = [(dx, dy, dc) for dx in (0, 1) for dy in (0, 1) for dc in (0, 1) if (dx, dy, dc) != (0, 0, 0)]
        peers = [((1 - x) if dx else x, (1 - y) if dy else y, (1 - c) if dc else c) for dx, dy, dc in flips]
        buf[me] = in_ref[...]
        sends = []
        for k, peer in enumerate(peers):
            cp = pltpu.make_async_remote_copy(src_ref=in_ref, dst_ref=buf.at[me], send_sem=send_sems.at[k],
                                              recv_sem=recv_sems.at[k], device_id=peer, device_id_type=MESH)
            cp.start()
            sends.append(cp)
        for k, (px, py, pc) in enumerate(peers):
            pltpu.make_async_remote_copy(src_ref=in_ref, dst_ref=buf.at[4 * px + 2 * py + pc], send_sem=send_sems.at[k],
                                         recv_sem=recv_sems.at[k], device_id=(px, py, pc), device_id_type=MESH).wait_recv()
        total = buf[0]
        for j in range(1, N_DEV):
            total = total + buf[j]
        out_ref[...] = total
        for cp in sends:
            cp.wait_send()

    vmem = pl.BlockSpec(memory_space=pltpu.VMEM)
    return pl.pallas_call(
        body, name=name, in_specs=[vmem], out_specs=vmem,
        out_shape=jax.ShapeDtypeStruct(pack.shape, F32),
        scratch_shapes=[pltpu.VMEM((N_DEV,) + pack.shape, F32), pltpu.SemaphoreType.DMA((N_DEV - 1,)),
                        pltpu.SemaphoreType.DMA((N_DEV - 1,))],
        compiler_params=pltpu.CompilerParams(has_side_effects=True, vmem_limit_bytes=VMEM_LIMIT_V7X),
    )(pack)


PACK_UNIT = SUBLANES * LANES


def _pack(arrs):
    flat = []
    for a in arrs:
        v = a.reshape(-1).astype(F32)
        flat.append(jnp.pad(v, (0, -v.size % PACK_UNIT)))
    return jnp.concatenate(flat).reshape(-1, LANES)


def _unpack(pack, shapes):
    flat = pack.reshape(-1)
    out, off = [], 0
    for s in shapes:
        size = 1
        for d in s:
            size *= d
        out.append(flat[off:off + size].reshape(s))
        off += size + (-size % PACK_UNIT)
    return out


SMALL = ["mix_norm_w", "gate_bias", "gmlp_ln_w", "gmlp_ln_b", "gmlp_ws", "gmlp_bs", "ssm_conv_w", "ssm_conv_b",
         "ssm_dt_bias", "ssm_a_log", "ssm_d", "ssm_norm_w", "ffn_norm_w", "ffn_conv_w", "ffn_conv_b", "final_norm_w"]
SMALL_SHARDED = ("gate_bias", "ssm_conv_w", "ffn_conv_w")
BIG = ["w_in", "w_proj_a", "w_proj_b", "w_out", "ffn_w_up", "ffn_w_down"]
WEIGHTS = ["mix_norm_w", "w_in", "gate_bias", "gmlp_ln_w", "gmlp_ln_b", "gmlp_ws", "gmlp_bs", "ssm_conv_w",
           "ssm_conv_b", "ssm_dt_bias", "ssm_a_log", "ssm_d", "ssm_norm_w", "w_proj_a", "w_proj_b", "w_out",
           "ffn_norm_w", "ffn_w_up", "ffn_conv_w", "ffn_conv_b", "ffn_w_down", "final_norm_w"]
IN_SPLITS = [0, 2048, 4096, 6144, 9216, 9248]


def _columns_from_chips(stack):
    return jnp.transpose(stack, (1, 0, 2)).reshape(stack.shape[1], -1)


def _columns_to_chips(full, parts=N_CHIPS):
    rows, cols = full.shape
    return jnp.transpose(full.reshape(rows, parts, cols // parts), (1, 0, 2))


def kernel(x, mix_norm_w, w_in, gate_bias, gmlp_ln_w, gmlp_ln_b, gmlp_ws, gmlp_bs, ssm_conv_w, ssm_conv_b, ssm_dt_bias, ssm_a_log, ssm_d, ssm_norm_w, w_proj_a, w_proj_b, w_out, ffn_norm_w, ffn_w_up, ffn_conv_w, ffn_conv_b, ffn_w_down, final_norm_w, loss_target, m_mix_norm_w, m_w_in, m_gate_bias, m_gmlp_ln_w, m_gmlp_ln_b, m_gmlp_ws, m_gmlp_bs, m_ssm_conv_w, m_ssm_conv_b, m_ssm_dt_bias, m_ssm_a_log, m_ssm_d, m_ssm_norm_w, m_w_proj_a, m_w_proj_b, m_w_out, m_ffn_norm_w, m_ffn_w_up, m_ffn_conv_w, m_ffn_conv_b, m_ffn_w_down, m_final_norm_w, v_mix_norm_w, v_w_in, v_gate_bias, v_gmlp_ln_w, v_gmlp_ln_b, v_gmlp_ws, v_gmlp_bs, v_ssm_conv_w, v_ssm_conv_b, v_ssm_dt_bias, v_ssm_a_log, v_ssm_d, v_ssm_norm_w, v_w_proj_a, v_w_proj_b, v_w_out, v_ffn_norm_w, v_ffn_w_up, v_ffn_conv_w, v_ffn_conv_b, v_ffn_w_down, v_final_norm_w):
    args = dict(locals())
    weights = {n: args[n] for n in WEIGHTS}
    moments_m = {n: args["m_" + n] for n in WEIGHTS}
    moments_v = {n: args["v_" + n] for n in WEIGHTS}
    chip = 2 * lax.axis_index("x") + lax.axis_index("y")

    shards = [weights[n][0].astype(BF16) for n in BIG] + [weights[n][0] for n in SMALL_SHARDED]
    gathered = _gather_chips_split("gather_weights", shards[:len(BIG)], shards[len(BIG):])
    gathered = [_own_slot(stack, own) for stack, own in zip(gathered, shards)]
    w_in_s, w_pa_s, w_pb_s, w_out_s, w_up_s, w_down_s, gb_s, scw_s, fcw_s = gathered
    w_in_full = _columns_from_chips(w_in_s)
    full = {"w_" + nm: w_in_full[:, IN_SPLITS[k]:IN_SPLITS[k + 1]] for k, nm in enumerate(["g", "za", "z", "xbc", "dt"])}
    full["w_proj_a"] = w_pa_s.reshape(-1, D_MODEL)
    full["w_proj_b"] = w_pb_s.reshape(-1, D_MODEL)
    full["w_out"] = w_out_s.reshape(-1, D_MODEL)
    full["w_down"] = w_down_s.reshape(-1, D_MODEL)
    full["w_up_g"] = _columns_from_chips(w_up_s[:2])
    full["w_up_v"] = _columns_from_chips(w_up_s[2:])
    full["gate_bias"] = _columns_from_chips(gb_s)
    full["ssm_conv_w"] = _columns_from_chips(scw_s)
    full["ffn_conv_w"] = _columns_from_chips(fcw_s)
    for n in SMALL:
        if n not in SMALL_SHARDED:
            full[n] = weights[n] if n == "final_norm_w" else weights[n][0]
    for n in ("mix_norm_w", "ffn_norm_w", "ssm_conv_b", "ssm_dt_bias", "ssm_a_log", "ssm_d", "ssm_norm_w", "ffn_conv_b"):
        full[n] = full[n].reshape(1, -1)

    loss_part, grad_x, g = _local_step(x[0], loss_target[0], full)

    small_shapes = [(1, LANES)] + [g[n].shape for n in SMALL]
    reduced = _unpack(_all_reduce("reduce_small", _pack([loss_part] + [g[n] for n in SMALL])), small_shapes)
    loss = reduced[0][0, 0]
    small_grads = {}
    for n, r in zip(SMALL, reduced[1:]):
        if n in SMALL_SHARDED:
            width = weights[n].shape[2]
            r = lax.dynamic_slice_in_dim(r, chip * width, width, axis=1)
        small_grads[n] = r.reshape(weights[n].shape)
    packs = [_pack([d[n] for n in SMALL]) for d in (weights, moments_m, moments_v, small_grads)]
    upd = _adamw("adamw_small", packs[0], [packs[3]], packs[1], packs[2], tm=packs[0].shape[0], rs=SUBLANES)
    small_out = [_unpack(u, [weights[n].shape for n in SMALL]) for u in upd]

    dw_in = jnp.concatenate([g["w_g"], g["w_za"], g["w_z"], g["w_xbc"], g["w_dt"]], axis=1)
    stacks = [
        _columns_to_chips(dw_in),
        g["w_proj_a"].reshape(N_CHIPS, -1, D_MODEL),
        g["w_proj_b"].reshape(N_CHIPS, -1, D_MODEL),
        g["w_out"].reshape(N_CHIPS, -1, D_MODEL),
        jnp.concatenate([_columns_to_chips(g["w_up_g"], 2), _columns_to_chips(g["w_up_v"], 2)], axis=0),
        g["w_down"].reshape(N_CHIPS, -1, D_MODEL),
    ]
    half_tiles = {"w_in": 128, "w_proj_a": 128, "w_proj_b": 256, "w_out": 128, "ffn_w_up": 128, "ffn_w_down": 176}
    tiles = {"w_in": 128, "w_proj_a": 256, "w_proj_b": 256, "w_out": 256, "ffn_w_up": 128, "ffn_w_down": 176}
    core = lax.axis_index("c")
    own_half = [_row_half(s, core, 1) for s in stacks]
    other_half = _swap_cores("pair_grads", [_row_half(s, 1 - core, 1) for s in stacks])
    pair = [_pair_sum("pair_" + n, a, b, tm=half_tiles[n]) for n, a, b in zip(BIG, own_half, other_half)]
    received = _scatter_chips("scatter_grads", pair)
    received = [_own_slot(r, lax.dynamic_index_in_dim(p, chip, 0, keepdims=False)) for r, p in zip(received, pair)]
    halves = [_sum_slots("sum_" + n, r, tm=half_tiles[n], rs=2 * SUBLANES) for n, r in zip(BIG, received)]
    other = _swap_cores("join_grads", halves)
    reduced = [jnp.where(core == 0, jnp.concatenate([a, b], axis=0), jnp.concatenate([b, a], axis=0))
               for a, b in zip(halves, other)]
    big_out = {}
    for n, grad in zip(BIG, reduced):
        big_out[n] = _adamw("adamw_" + n, weights[n][0], [grad], moments_m[n][0], moments_v[n][0],
                            tm=tiles[n], rs=SUBLANES)

    per_kind = [[], [], [], []]
    for n in WEIGHTS:
        for kind in range(4):
            if n in big_out:
                per_kind[kind].append(big_out[n][kind].reshape(weights[n].shape))
            else:
                per_kind[kind].append(small_out[kind][SMALL.index(n)])
    return (loss, grad_x[None], *per_kind[0], *per_kind[1], *per_kind[2], *per_kind[3])
```

```python
import jax
import jax.numpy as jnp
from jax import lax
from jax.experimental import pallas as pl
from jax.experimental.pallas import tpu as pltpu

F32 = jnp.float32
BF16 = jnp.bfloat16
MESH = pl.DeviceIdType.MESH

EPS = 1e-5
D_MODEL = 1024
GMLP_BLOCK = 128
GMLP_GROUPS = 8
CHUNK = 64
SSM_INNER = 2048
SSM_HEADS = 32
SSM_HEAD_DIM = 64
SSM_GROUPS = 4
SSM_HPG = 8
SSM_STATE = 128
SSM_CONV = 4
SSM_XBC = 3072
D_FF = 2816
FFN_CONV = 3
N_CHIPS = 4
N_DEV = 8

ADAM_LR = 0.001
ADAM_B1 = 0.9
ADAM_B2 = 0.999
ADAM_EPS = 1e-08
ADAM_WD = 0.01
ADAM_STEP = 10

VMEM_LIMIT_V7X = 56 * 1024 * 1024
SUBLANES = 8
LANES = 128


def _params(sem=None):
    return pltpu.CompilerParams(dimension_semantics=sem, vmem_limit_bytes=VMEM_LIMIT_V7X)


def _dot(a, b, ca=1, cb=0):
    return lax.dot_general(a.astype(BF16), b.astype(BF16), (((ca,), (cb,)), ((), ())),
                           preferred_element_type=F32)


def _mm(name, a, b, *, ta=False, tb=False, out_dtype=F32, bm, bn, bk, res=None):
    m, k = (a.shape[1], a.shape[0]) if ta else a.shape
    k2, n = (b.shape[1], b.shape[0]) if tb else b.shape
    assert k == k2 and m % bm == 0 and n % bn == 0 and k % bk == 0, (name, a.shape, b.shape)
    nk = k // bk
    a_spec = (pl.BlockSpec((bk, bm), lambda i, j, kk: (kk, i)) if ta
              else pl.BlockSpec((bm, bk), lambda i, j, kk: (i, kk)))
    b_spec = (pl.BlockSpec((bn, bk), lambda i, j, kk: (j, kk)) if tb
              else pl.BlockSpec((bk, bn), lambda i, j, kk: (kk, j)))
    o_spec = pl.BlockSpec((bm, bn), lambda i, j, kk: (i, j))
    has_res = res is not None

    def body(*refs):
        a_ref, b_ref = refs[0], refs[1]
        r_ref = refs[2] if has_res else None
        o_ref = refs[3] if has_res else refs[2]
        p = _dot(a_ref[...], b_ref[...], 0 if ta else 1, 1 if tb else 0)

        def finish(total):
            if has_res:
                total = total + r_ref[...]
            o_ref[...] = total.astype(out_dtype)

        if nk == 1:
            finish(p)
        else:
            acc_ref = refs[-1]
            kk = pl.program_id(2)

            @pl.when(kk == 0)
            def _():
                acc_ref[...] = p

            @pl.when(kk > 0)
            def _():
                acc_ref[...] += p

            @pl.when(kk == nk - 1)
            def _():
                finish(acc_ref[...])

    return pl.pallas_call(
        body, name=name,
        grid=(m // bm, n // bn, nk),
        in_specs=[a_spec, b_spec] + ([o_spec] if has_res else []),
        out_specs=o_spec,
        out_shape=jax.ShapeDtypeStruct((m, n), out_dtype),
        scratch_shapes=[pltpu.VMEM((bm, bn), F32)] if nk > 1 else [],
        compiler_params=_params(("parallel", "parallel", "arbitrary")),
    )(*([a, b] + ([res] if has_res else [])))


def _rows(name, fn, ins, params, outs, accs, *, tm, rs, unroll=4):
    nrow = ins[0][0].shape[-2]
    while tm % (rs * unroll):
        unroll //= 2
    assert nrow % tm == 0 and tm % rs == 0, (name, nrow, tm, rs)
    n_in, n_p, n_out, n_acc = len(ins), len(params), len(outs), len(accs)
    in_specs = []
    for spec in ins:
        arr, width, cb = spec[:3]
        if len(spec) == 4:
            in_specs.append(pl.BlockSpec((None, tm, width), lambda i, cb=cb, lead=spec[3]: (lead, i, cb)))
        else:
            in_specs.append(pl.BlockSpec((tm, width), lambda i, cb=cb: (i, cb)))
    for p in params:
        in_specs.append(pl.BlockSpec(p.shape, lambda i, nd=p.ndim: (0,) * nd))
    out_specs = [pl.BlockSpec((tm, w), lambda i: (i, 0)) for w, _ in outs]
    out_specs += [pl.BlockSpec(s, lambda i: (0, 0)) for s in accs]
    out_shape = [jax.ShapeDtypeStruct((nrow, w), dt) for w, dt in outs]
    out_shape += [jax.ShapeDtypeStruct(s, F32) for s in accs]

    def body(*refs):
        in_refs = refs[:n_in]
        p_refs = refs[n_in:n_in + n_p]
        o_refs = refs[n_in + n_p:n_in + n_p + n_out]
        a_refs = refs[n_in + n_p + n_out:]
        pv = [p[...] for p in p_refs]

        if n_acc:
            @pl.when(pl.program_id(0) == 0)
            def _():
                for a_ref in a_refs:
                    a_ref[...] = jnp.zeros_like(a_ref)

        def step(r, carry):
            for u in range(unroll):
                sl = pl.ds(pl.multiple_of((r * unroll + u) * rs, rs), rs)
                vals = [ref[sl, :].astype(F32) for ref in in_refs]
                row_out, sums = fn(*vals, *pv)
                for o_ref, v in zip(o_refs, row_out):
                    o_ref[sl, :] = v.astype(o_ref.dtype)
                carry = tuple(c + s for c, s in zip(carry, sums))
            return carry

        init = tuple(jnp.zeros(s, F32) for s in accs)
        total = lax.fori_loop(0, tm // (rs * unroll), step, init)
        for a_ref, t in zip(a_refs, total):
            a_ref[...] += t

    res = pl.pallas_call(
        body, name=name, grid=(nrow // tm,),
        in_specs=in_specs, out_specs=out_specs, out_shape=out_shape,
        compiler_params=_params(("arbitrary",)),
    )(*([s[0] for s in ins] + list(params)))
    return res


def _rms(x, w):
    return x * lax.rsqrt(jnp.mean(x * x, axis=-1, keepdims=True) + EPS) * w


def _colsum(v):
    return jnp.sum(v, axis=0, keepdims=True)


def _rms_fwd(name, x, w):
    def fn(xv, wv):
        return (_rms(xv, wv),), ()
    return _rows(name, fn, [(x, D_MODEL, 0)], [w], [(D_MODEL, BF16)], [], tm=512, rs=16)[0]


def _rms_bwd(name, x, w, dy, dres):
    def fn(xv, dyv, drv, wv):
        _, vjp = jax.vjp(_rms, xv, wv)
        dx, dw = vjp(dyv)
        return (drv + dx,), (dw,)
    return _rows(name, fn, [(x, D_MODEL, 0), (dy, D_MODEL, 0), (dres, D_MODEL, 0)], [w],
                 [(D_MODEL, F32)], [(1, D_MODEL)], tm=512, rs=16)


def _final_loss(name, h, target, w):
    def fn(hv, tv, wv):
        y, vjp = jax.vjp(_rms, hv, wv)
        err = y - tv
        part = 0.5 * jnp.sum(jnp.mean(err * err, axis=-1, keepdims=True), axis=0, keepdims=True)
        dh, dw = vjp(err / D_MODEL)
        return (dh,), (jnp.broadcast_to(part, (1, LANES)), dw)
    return _rows(name, fn, [(h, D_MODEL, 0), (target, D_MODEL, 0)], [w],
                 [(D_MODEL, F32)], [(1, LANES), (1, D_MODEL)], tm=512, rs=16)


def _merge(ga, gb, ya, yb, b0, b1):
    return jax.nn.sigmoid(ga + b0) * ya + jax.nn.sigmoid(gb + b1) * yb


def _merge_fwd(name, g, ya, yb, b0, b1):
    def fn(ga, gb, yav, ybv, b0v, b1v):
        return (_merge(ga, gb, yav, ybv, b0v, b1v),), ()
    return _rows(name, fn, [(g, D_MODEL, 0), (g, D_MODEL, 1), (ya, D_MODEL, 0), (yb, D_MODEL, 0)],
                 [b0, b1], [(D_MODEL, BF16)], [], tm=512, rs=16)[0]


def _merge_bwd(name, g, ya, yb, dm, b0, b1):
    def fn(ga, gb, yav, ybv, dmv, b0v, b1v):
        _, vjp = jax.vjp(_merge, ga, gb, yav, ybv, b0v, b1v)
        dga, dgb, dya, dyb, db0, db1 = vjp(dmv)
        return (jnp.concatenate([dga, dgb], axis=1), dya, dyb), (db0, db1)
    return _rows(name, fn,
                 [(g, D_MODEL, 0), (g, D_MODEL, 1), (ya, D_MODEL, 0), (yb, D_MODEL, 0), (dm, D_MODEL, 0)],
                 [b0, b1], [(2 * D_MODEL, BF16), (D_MODEL, BF16), (D_MODEL, BF16)],
                 [(1, D_MODEL), (1, D_MODEL)], tm=512, rs=16)


GROUP_W = SSM_INNER // SSM_GROUPS


def _gate_norm_group(y, z, nw):
    v = y * jax.nn.silu(z)
    return v * lax.rsqrt(jnp.mean(v * v, axis=-1, keepdims=True) + EPS) * nw


def _gate_norm_fwd(name, y, z, nw):
    def fn(yv, zv, nwv):
        parts = [_gate_norm_group(yv[:, k * GROUP_W:(k + 1) * GROUP_W], zv[:, k * GROUP_W:(k + 1) * GROUP_W],
                                  nwv[:, k * GROUP_W:(k + 1) * GROUP_W]) for k in range(SSM_GROUPS)]
        return (jnp.concatenate(parts, axis=1),), ()
    return _rows(name, fn, [(y, SSM_INNER, 0), (z, SSM_INNER, 0)], [nw], [(SSM_INNER, BF16)], [],
                 tm=512, rs=16)[0]


def _gate_norm_bwd(name, y, z, dout, nw):
    def fn(yv, zv, dv, nwv):
        dys, dzs, dns = [], [], []
        for k in range(SSM_GROUPS):
            sl = slice(k * GROUP_W, (k + 1) * GROUP_W)
            _, vjp = jax.vjp(_gate_norm_group, yv[:, sl], zv[:, sl], nwv[:, sl])
            dy, dz, dn = vjp(dv[:, sl])
            dys.append(dy), dzs.append(dz), dns.append(dn)
        return (jnp.concatenate(dys, axis=1), jnp.concatenate(dzs, axis=1)), (jnp.concatenate(dns, axis=1),)
    return _rows(name, fn, [(y, SSM_INNER, 0), (z, SSM_INNER, 0), (dout, SSM_INNER, 0)], [nw],
                 [(SSM_INNER, F32), (SSM_INNER, BF16)], [(1, SSM_INNER)], tm=512, rs=16)


def _softplus(v):
    return jnp.maximum(v, 0.0) + jnp.log1p(jnp.exp(-jnp.abs(v)))


def _chunk_cumsum(v, reverse=False):
    row = lax.broadcasted_iota(jnp.int32, v.shape, 0)
    step = 1
    while step < CHUNK:
        if reverse:
            shifted = pltpu.roll(v, CHUNK - step, axis=0)
            v = v + jnp.where(row < CHUNK - step, shifted, 0.0)
        else:
            shifted = pltpu.roll(v, step, axis=0)
            v = v + jnp.where(row >= step, shifted, 0.0)
        step *= 2
    return v


def _dt_prep(name, dt_raw, dt_bias, a_log):
    def fn(rv, bv, alv):
        dt = _softplus(rv + bv)
        return (dt, _chunk_cumsum(dt * (-jnp.exp(alv)))), ()
    return _rows(name, fn, [(dt_raw, SSM_HEADS, 0)], [dt_bias, a_log],
                 [(SSM_HEADS, F32), (SSM_HEADS, F32)], [], tm=512, rs=CHUNK)


def _dt_bwd(name, dt_raw, ddt, da1, da2, dt_bias, a_log):
    def fn(rv, ddv, d1, d2, bv, alv):
        pre = rv + bv
        dt = _softplus(pre)
        a_neg = -jnp.exp(alv)
        back = _chunk_cumsum(d1 + d2, reverse=True)
        d_dt = ddv + back * a_neg
        d_raw = d_dt * jax.nn.sigmoid(pre)
        return (d_raw,), (_colsum(d_raw), _colsum(back * dt) * a_neg)
    return _rows(name, fn, [(dt_raw, SSM_HEADS, 0), (ddt, SSM_HEADS, 0), (da1, SSM_HEADS, 0), (da2, SSM_HEADS, 0)],
                 [dt_bias, a_log], [(SSM_HEADS, BF16)], [(1, SSM_HEADS), (1, SSM_HEADS)], tm=512, rs=CHUNK)


def _adamw_math(w, g, m, v):
    m_new = ADAM_B1 * m + (1.0 - ADAM_B1) * g
    v_new = ADAM_B2 * v + (1.0 - ADAM_B2) * jnp.square(g)
    m_hat = m_new / (1.0 - ADAM_B1 ** ADAM_STEP)
    v_hat = v_new / (1.0 - ADAM_B2 ** ADAM_STEP)
    delta = -ADAM_LR * (m_hat / (jnp.sqrt(v_hat) + ADAM_EPS) + ADAM_WD * w)
    return delta, m_new, v_new


def _adamw(name, w, g, m, v, *, tm, rs):
    width = w.shape[1]

    def fn(wv, mv, vv, gv):
        return (gv,) + _adamw_math(wv, gv, mv, vv), ()
    return _rows(name, fn, [(w, width, 0), (m, width, 0), (v, width, 0), (g, width, 0)],
                 [], [(width, F32)] * 4, [], tm=tm, rs=rs)


def _adamw_small(name, ws, gs, ms, vs):
    n = len(ws)

    def body(*refs):
        w_refs, g_refs, m_refs, v_refs = (refs[k * n:(k + 1) * n] for k in range(4))
        outs = refs[4 * n:]
        for i in range(n):
            res = _adamw_math(w_refs[i][...], g_refs[i][...], m_refs[i][...], v_refs[i][...])
            for k in range(3):
                outs[k * n + i][...] = res[k]

    vmem = pl.BlockSpec(memory_space=pltpu.VMEM)
    res = pl.pallas_call(
        body, name=name, in_specs=[vmem] * (4 * n), out_specs=[vmem] * (3 * n),
        out_shape=[jax.ShapeDtypeStruct(w.shape, F32) for w in ws] * 3,
        compiler_params=pltpu.CompilerParams(vmem_limit_bytes=VMEM_LIMIT_V7X),
    )(*ws, *gs, *ms, *vs)
    return res[:n], res[n:2 * n], res[2 * n:]


def _pair_sum(name, a, b, *, tm):
    shape = a.shape
    flat = (shape[0] * shape[1], shape[2])

    def fn(av, bv):
        return (av.astype(F32) + bv.astype(F32),), ()
    out = _rows(name, fn, [(a.reshape(flat), flat[1], 0), (b.reshape(flat), flat[1], 0)], [], [(flat[1], BF16)], [],
                tm=tm, rs=2 * SUBLANES)[0]
    return out.reshape(shape)


def _sum_slots(name, stack, *, tm, rs):
    width = stack.shape[2]

    def fn(*slots):
        s0, s1, s2, s3 = (s.astype(F32) for s in slots)
        return (((s0 + s1) + s2) + s3,), ()
    return _rows(name, fn, [(stack, width, 0, k) for k in range(N_CHIPS)], [], [(width, F32)], [],
                 tm=tm, rs=rs)[0]


def _layernorm(v, w, b):
    mu = jnp.mean(v, axis=-1, keepdims=True)
    var = jnp.mean(jnp.square(v - mu), axis=-1, keepdims=True)
    return (v - mu) * lax.rsqrt(var + EPS) * w + b


def _gmlp_mask():
    t = lax.broadcasted_iota(jnp.int32, (GMLP_BLOCK, GMLP_BLOCK), 0) // CHUNK
    s = lax.broadcasted_iota(jnp.int32, (GMLP_BLOCK, GMLP_BLOCK), 1) // CHUNK
    return s <= t


GMLP_TM = 512


def _gmlp_fwd(name, za, ln_w, ln_b, ws, bs_col):
    nrow = za.shape[0]
    tm = GMLP_TM
    width = GMLP_GROUPS * GMLP_BLOCK

    def body(za_ref, lnw_ref, lnb_ref, ws_ref, bs_ref, o_ref, wm_ref):
        mask = _gmlp_mask()
        for g in range(GMLP_GROUPS):
            wm_ref[g] = jnp.where(mask, ws_ref[g], 0.0).astype(BF16)

        def block(n, carry):
            rows = pl.ds(pl.multiple_of(n * GMLP_BLOCK, GMLP_BLOCK), GMLP_BLOCK)
            for g in range(GMLP_GROUPS):
                cols = slice(g * GMLP_BLOCK, (g + 1) * GMLP_BLOCK)
                vcols = slice(width + g * GMLP_BLOCK, width + (g + 1) * GMLP_BLOCK)
                u = jax.nn.gelu(za_ref[rows, cols].astype(F32))
                v = jax.nn.gelu(za_ref[rows, vcols].astype(F32))
                vn = _layernorm(v, lnw_ref[g:g + 1, :], lnb_ref[g:g + 1, :])
                sv = _dot(wm_ref[g], vn) + bs_ref[g]
                o_ref[rows, cols] = (u * sv).astype(o_ref.dtype)
            return carry

        lax.fori_loop(0, tm // GMLP_BLOCK, block, 0)

    small = lambda a: pl.BlockSpec(a.shape, lambda i, nd=a.ndim: (0,) * nd)
    return pl.pallas_call(
        body, name=name, grid=(nrow // tm,),
        in_specs=[pl.BlockSpec((tm, 2 * width), lambda i: (i, 0)), small(ln_w), small(ln_b), small(ws), small(bs_col)],
        out_specs=pl.BlockSpec((tm, width), lambda i: (i, 0)),
        out_shape=jax.ShapeDtypeStruct((nrow, width), BF16),
        scratch_shapes=[pltpu.VMEM((GMLP_GROUPS, GMLP_BLOCK, GMLP_BLOCK), BF16)],
        compiler_params=_params(("arbitrary",)),
    )(za, ln_w, ln_b, ws, bs_col)


def _gmlp_bwd(name, za, dout, ln_w, ln_b, ws, bs_col):
    nrow = za.shape[0]
    tm = GMLP_TM
    width = GMLP_GROUPS * GMLP_BLOCK

    def body(za_ref, do_ref, lnw_ref, lnb_ref, ws_ref, bs_ref, dza_ref, dlnw_ref, dlnb_ref, dws_ref, dbs_ref, wm_ref):
        mask = _gmlp_mask()
        for g in range(GMLP_GROUPS):
            wm_ref[g] = jnp.where(mask, ws_ref[g], 0.0).astype(BF16)

        @pl.when(pl.program_id(0) == 0)
        def _():
            dlnw_ref[...] = jnp.zeros_like(dlnw_ref)
            dlnb_ref[...] = jnp.zeros_like(dlnb_ref)
            dws_ref[...] = jnp.zeros_like(dws_ref)
            dbs_ref[...] = jnp.zeros_like(dbs_ref)

        def block(n, carry):
            rows = pl.ds(pl.multiple_of(n * GMLP_BLOCK, GMLP_BLOCK), GMLP_BLOCK)
            for g in range(GMLP_GROUPS):
                cols = slice(g * GMLP_BLOCK, (g + 1) * GMLP_BLOCK)
                vcols = slice(width + g * GMLP_BLOCK, width + (g + 1) * GMLP_BLOCK)
                u, gelu_u_vjp = jax.vjp(jax.nn.gelu, za_ref[rows, cols].astype(F32))
                v, gelu_v_vjp = jax.vjp(jax.nn.gelu, za_ref[rows, vcols].astype(F32))
                vn, ln_vjp = jax.vjp(_layernorm, v, lnw_ref[g:g + 1, :], lnb_ref[g:g + 1, :])
                sv = _dot(wm_ref[g], vn) + bs_ref[g]
                d_o = do_ref[rows, cols].astype(F32)
                dsv = d_o * u
                d_wm = _dot(dsv, vn, 1, 1)
                dvn = _dot(wm_ref[g], dsv, 0, 0)
                dv, dlnw, dlnb = ln_vjp(dvn)
                dza_ref[rows, cols] = gelu_u_vjp(d_o * sv)[0].astype(dza_ref.dtype)
                dza_ref[rows, vcols] = gelu_v_vjp(dv)[0].astype(dza_ref.dtype)
                dlnw_ref[g:g + 1, :] += dlnw
                dlnb_ref[g:g + 1, :] += dlnb
                dws_ref[g] += jnp.where(mask, d_wm, 0.0)
                dbs_ref[g] += jnp.sum(dsv, axis=1, keepdims=True)
            return carry

        lax.fori_loop(0, tm // GMLP_BLOCK, block, 0)

    small = lambda a: pl.BlockSpec(a.shape, lambda i, nd=a.ndim: (0,) * nd)
    return pl.pallas_call(
        body, name=name, grid=(nrow // tm,),
        in_specs=[pl.BlockSpec((tm, 2 * width), lambda i: (i, 0)), pl.BlockSpec((tm, width), lambda i: (i, 0)),
                  small(ln_w), small(ln_b), small(ws), small(bs_col)],
        out_specs=[pl.BlockSpec((tm, 2 * width), lambda i: (i, 0)), small(ln_w), small(ln_b), small(ws), small(bs_col)],
        out_shape=[jax.ShapeDtypeStruct((nrow, 2 * width), BF16), jax.ShapeDtypeStruct(ln_w.shape, F32),
                   jax.ShapeDtypeStruct(ln_b.shape, F32), jax.ShapeDtypeStruct(ws.shape, F32),
                   jax.ShapeDtypeStruct(bs_col.shape, F32)],
        scratch_shapes=[pltpu.VMEM((GMLP_GROUPS, GMLP_BLOCK, GMLP_BLOCK), BF16)],
        compiler_params=_params(("arbitrary",)),
    )(za, dout, ln_w, ln_b, ws, bs_col)


CONV_TM = 256
CONV_RS = 32
HALO = 2 * SUBLANES


def _tap_rows(w_ref):
    return [w_ref[k:k + 1, :] for k in range(w_ref.shape[0])]


def _conv_rows(win, w, rs):
    taps = len(w)
    out = w[taps - 1] * win[HALO:, :]
    for k in range(taps - 1):
        back = taps - 1 - k
        out = out + w[k] * pltpu.roll(win, back, axis=0)[HALO:, :]
    return out


def _conv_t_rows(win, w, rs):
    taps = len(w)
    out = w[taps - 1] * win[:rs, :]
    for k in range(taps - 1):
        ahead = taps - 1 - k
        out = out + w[k] * pltpu.roll(win, rs + HALO - ahead, axis=0)[:rs, :]
    return out


def _conv_dw_rows(d, xwin, taps):
    rows = []
    for k in range(taps):
        back = taps - 1 - k
        xs = xwin[HALO:, :] if back == 0 else pltpu.roll(xwin, back, axis=0)[HALO:, :]
        rows.append(jnp.sum(d * xs, axis=0, keepdims=True))
    return rows


def _halo_specs(nrow, tm, tc):
    per = tm // HALO
    last = nrow // HALO - 1
    main = pl.BlockSpec((tm, tc), lambda j, i: (i, j))
    before = pl.BlockSpec((HALO, tc), lambda j, i: (jnp.maximum(i * per - 1, 0), j))
    after = pl.BlockSpec((HALO, tc), lambda j, i: (jnp.minimum((i + 1) * per, last), j))
    return main, before, after


def _col_spec(rows, tc):
    return pl.BlockSpec((rows, tc), lambda j, i: (0, j))


def _conv_fwd(name, x, w, b, *, tc):
    nrow, ncol = x.shape
    taps = w.shape[0]
    tm, rs = CONV_TM, CONV_RS
    main, before, _ = _halo_specs(nrow, tm, tc)

    def body(x_ref, xb_ref, w_ref, b_ref, o_ref):
        first = pl.program_id(1) == 0
        wv, bv = _tap_rows(w_ref), b_ref[...]

        def step(r, prev):
            sl = pl.ds(pl.multiple_of(r * rs, rs), rs)
            cur = x_ref[sl, :].astype(F32)
            o_ref[sl, :] = (_conv_rows(jnp.concatenate([prev, cur], axis=0), wv, rs) + bv).astype(o_ref.dtype)
            return cur[rs - HALO:, :]

        lax.fori_loop(0, tm // rs, step, jnp.where(first, 0.0, xb_ref[...].astype(F32)))

    return pl.pallas_call(
        body, name=name, grid=(ncol // tc, nrow // tm),
        in_specs=[main, before, _col_spec(taps, tc), _col_spec(1, tc)],
        out_specs=main, out_shape=jax.ShapeDtypeStruct((nrow, ncol), BF16),
        compiler_params=_params(("parallel", "arbitrary")),
    )(x, x, w, b)


def _conv_bwd(name, dpre, x, w, *, tc):
    nrow, ncol = x.shape
    taps = w.shape[0]
    tm, rs = CONV_TM, CONV_RS
    nsub = tm // rs
    main, before, after = _halo_specs(nrow, tm, tc)

    def body(d_ref, da_ref, x_ref, xb_ref, w_ref, dx_ref, dw_ref, db_ref):
        i = pl.program_id(1)
        first, last = i == 0, i == pl.num_programs(1) - 1
        wv = _tap_rows(w_ref)
        x_before = jnp.where(first, 0.0, xb_ref[...].astype(F32))

        @pl.when(first)
        def _():
            dw_ref[...] = jnp.zeros_like(dw_ref)
            db_ref[...] = jnp.zeros_like(db_ref)

        def step(q, carry):
            nxt, dw, db = carry
            r = nsub - 1 - q
            sl = pl.ds(pl.multiple_of(r * rs, rs), rs)
            cur = d_ref[sl, :].astype(F32)
            dx_ref[sl, :] = _conv_t_rows(jnp.concatenate([cur, nxt], axis=0), wv, rs).astype(dx_ref.dtype)
            inner = x_ref[pl.ds(pl.multiple_of(jnp.maximum(r * rs - HALO, 0), HALO), HALO), :].astype(F32)
            xwin = jnp.concatenate([jnp.where(r == 0, x_before, inner), x_ref[sl, :].astype(F32)], axis=0)
            dw = tuple(a + s for a, s in zip(dw, _conv_dw_rows(cur, xwin, taps)))
            return cur[:HALO, :], dw, db + _colsum(cur)

        zero_row = jnp.zeros((1, tc), F32)
        init = (jnp.where(last, 0.0, da_ref[...].astype(F32)), (zero_row,) * taps, zero_row)
        _, dw, db = lax.fori_loop(0, nsub, step, init)
        for k in range(taps):
            dw_ref[k:k + 1, :] += dw[k]
        db_ref[...] += db

    return pl.pallas_call(
        body, name=name, grid=(ncol // tc, nrow // tm),
        in_specs=[main, after, main, before, _col_spec(taps, tc)],
        out_specs=[main, _col_spec(taps, tc), _col_spec(1, tc)],
        out_shape=[jax.ShapeDtypeStruct((nrow, ncol), BF16), jax.ShapeDtypeStruct((taps, ncol), F32),
                   jax.ShapeDtypeStruct((1, ncol), F32)],
        compiler_params=_params(("parallel", "arbitrary")),
    )(dpre, dpre, x, x, w)


def _glu(gate, val):
    return jax.nn.silu(gate) * val


def _ffn_act_fwd(name, pg, pv, wg, wv, bg, bv, *, tc):
    nrow, ncol = pg.shape
    taps = wg.shape[0]
    tm, rs = CONV_TM, CONV_RS
    main, before, _ = _halo_specs(nrow, tm, tc)

    def body(pg_ref, pgb_ref, pv_ref, pvb_ref, wg_ref, wv_ref, bg_ref, bv_ref, g_ref, v_ref, a_ref):
        first = pl.program_id(1) == 0
        wgv, wvv, bgv, bvv = _tap_rows(wg_ref), _tap_rows(wv_ref), bg_ref[...], bv_ref[...]

        def step(r, carry):
            prev_g, prev_v = carry
            sl = pl.ds(pl.multiple_of(r * rs, rs), rs)
            cur_g, cur_v = pg_ref[sl, :].astype(F32), pv_ref[sl, :].astype(F32)
            gate = _conv_rows(jnp.concatenate([prev_g, cur_g], axis=0), wgv, rs) + bgv
            val = _conv_rows(jnp.concatenate([prev_v, cur_v], axis=0), wvv, rs) + bvv
            g_ref[sl, :] = gate.astype(g_ref.dtype)
            v_ref[sl, :] = val.astype(v_ref.dtype)
            a_ref[sl, :] = _glu(gate, val).astype(a_ref.dtype)
            return cur_g[rs - HALO:, :], cur_v[rs - HALO:, :]

        lax.fori_loop(0, tm // rs, step, (jnp.where(first, 0.0, pgb_ref[...].astype(F32)),
                                          jnp.where(first, 0.0, pvb_ref[...].astype(F32))))

    return pl.pallas_call(
        body, name=name, grid=(ncol // tc, nrow // tm),
        in_specs=[main, before, main, before, _col_spec(taps, tc), _col_spec(taps, tc), _col_spec(1, tc), _col_spec(1, tc)],
        out_specs=[main, main, main],
        out_shape=[jax.ShapeDtypeStruct((nrow, ncol), BF16)] * 3,
        compiler_params=_params(("parallel", "arbitrary")),
    )(pg, pg, pv, pv, wg, wv, bg, bv)


def _ffn_act_bwd(name, dact, gate, val):
    def fn(dv, gv, vv):
        _, vjp = jax.vjp(_glu, gv, vv)
        dg, dval = vjp(dv)
        return (dg, dval), ()
    width = dact.shape[1]
    return _rows(name, fn, [(dact, width, 0), (gate, width, 0), (val, width, 0)], [],
                 [(width, BF16), (width, BF16)], [], tm=256, rs=2 * SUBLANES)


SSD_TM = 256
SSD_CHUNKS = SSD_TM // CHUNK
X_OFF, B_OFF, C_OFF = 0, SSM_INNER, SSM_INNER + SSM_GROUPS * SSM_STATE
HP = SSM_HPG * SSM_HEAD_DIM


def _causal_tiled():
    row = lax.broadcasted_iota(jnp.int32, (CHUNK, HP), 0)
    src = lax.broadcasted_iota(jnp.int32, (CHUNK, HP), 1) & (CHUNK - 1)
    return src <= row


def _split3(v):
    hi = v.astype(BF16)
    rest = v - hi.astype(F32)
    mid = rest.astype(BF16)
    lo = (rest - mid.astype(F32)).astype(BF16)
    return hi, mid, lo


def _dot_exact(a, ind):
    parts = [lax.dot_general(p, ind, (((1,), (0,)), ((), ())), preferred_element_type=F32) for p in _split3(a)]
    return (parts[0] + parts[1]) + parts[2]


def _head_indicator():
    head = lax.broadcasted_iota(jnp.int32, (SSM_HEADS, SSM_INNER), 0)
    chan = lax.broadcasted_iota(jnp.int32, (SSM_HEADS, SSM_INNER), 1)
    return (chan // SSM_HEAD_DIM == head).astype(BF16)


def _chunk_decays(ci, dt_ref, ac_ref, ind, ax_ref, dtx_ref, eax_ref, eex_ref, tail_ref):
    rows = pl.ds(pl.multiple_of(ci * CHUNK, CHUNK), CHUNK)
    ax_ref[...] = _dot_exact(ac_ref[rows, :], ind)
    dtx_ref[...] = _dot_exact(dt_ref[rows, :], ind)
    eax_ref[...] = jnp.exp(ax_ref[...])
    eex_ref[...] = jnp.exp(ax_ref[CHUNK - 1:CHUNK, :] - ax_ref[...])
    tail = pl.ds(pl.multiple_of(ci * CHUNK + CHUNK - SUBLANES, SUBLANES), SUBLANES)
    tail_ref[...] = jnp.exp(ac_ref[tail, :])


def _group_decay(ci, g, ax_ref, af_ref, xbc_ref, causal):
    gcols = slice(g * HP, (g + 1) * HP)
    bm = xbc_ref[:, B_OFF + g * SSM_STATE:B_OFF + (g + 1) * SSM_STATE]
    cm = xbc_ref[:, C_OFF + g * SSM_STATE:C_OFF + (g + 1) * SSM_STATE]
    cb_tiled = _dot(cm, jnp.concatenate([bm] * SSM_HPG, axis=0), 1, 1)
    seg = ax_ref[:, gcols] - af_ref[ci, :, gcols]
    decay = jnp.where(causal, jnp.exp(jnp.where(causal, seg, 0.0)), 0.0)
    return bm, cm, cb_tiled * decay, decay


def _ssd_fwd(name, pre, dt, a_cum, a_flat, d_x, ind):
    nrow = pre.shape[0]
    tm = SSD_TM

    def body(pre_ref, dt_ref, ac_ref, af_ref, dx_ref, ind_ref, y_ref, st_ref,
             h_ref, xbc_ref, ax_ref, dtx_ref, eax_ref, eex_ref, m_ref, xd_ref, yd_ref, tail_ref):
        @pl.when(pl.program_id(0) == 0)
        def _():
            h_ref[...] = jnp.zeros_like(h_ref)

        causal = _causal_tiled()
        ind = ind_ref[...]

        def chunk(ci, carry):
            rows = pl.ds(pl.multiple_of(ci * CHUNK, CHUNK), CHUNK)
            xbc_ref[...] = jax.nn.silu(pre_ref[rows, :].astype(F32))
            _chunk_decays(ci, dt_ref, ac_ref, ind, ax_ref, dtx_ref, eax_ref, eex_ref, tail_ref)
            st_ref[ci] = h_ref[...].astype(st_ref.dtype)
            for g in range(SSM_GROUPS):
                gcols = slice(g * HP, (g + 1) * HP)
                bm, cm, m_all, _ = _group_decay(ci, g, ax_ref, af_ref, xbc_ref, causal)
                m_ref[...] = m_all
                x_g = xbc_ref[:, gcols]
                xd = x_g * dtx_ref[:, gcols]
                xd_ref[...] = xd
                h_g = h_ref[gcols, :]
                for hh in range(SSM_HPG):
                    lc = slice(hh * SSM_HEAD_DIM, (hh + 1) * SSM_HEAD_DIM)
                    yd_ref[:, lc] = _dot(m_ref[:, lc], xd_ref[:, lc])
                y_ref[rows, gcols] = (yd_ref[...] + _dot(cm, h_g, 1, 1) * eax_ref[:, gcols]
                                      + dx_ref[:, gcols] * x_g)
                new = _dot(xd * eex_ref[:, gcols], bm, 0, 0)
                for hh in range(SSM_HPG):
                    h = g * SSM_HPG + hh
                    hrows = slice(h * SSM_HEAD_DIM, (h + 1) * SSM_HEAD_DIM)
                    lrows = slice(hh * SSM_HEAD_DIM, (hh + 1) * SSM_HEAD_DIM)
                    h_ref[hrows, :] = tail_ref[SUBLANES - 1:SUBLANES, h:h + 1] * h_ref[hrows, :] + new[lrows, :]
            return carry

        lax.fori_loop(0, SSD_CHUNKS, chunk, 0)

    nchunk = nrow // CHUNK
    whole = lambda a: pl.BlockSpec(a.shape, lambda i, nd=a.ndim: (0,) * nd)
    wide = lambda: pltpu.VMEM((CHUNK, SSM_INNER), F32)
    group = lambda: pltpu.VMEM((CHUNK, HP), F32)
    return pl.pallas_call(
        body, name=name, grid=(nrow // tm,),
        in_specs=[pl.BlockSpec((tm, SSM_XBC), lambda i: (i, 0)), pl.BlockSpec((tm, SSM_HEADS), lambda i: (i, 0)),
                  pl.BlockSpec((tm, SSM_HEADS), lambda i: (i, 0)),
                  pl.BlockSpec((SSD_CHUNKS, 1, SSM_INNER), lambda i: (i, 0, 0)), whole(d_x), whole(ind)],
        out_specs=[pl.BlockSpec((tm, SSM_INNER), lambda i: (i, 0)),
                   pl.BlockSpec((SSD_CHUNKS, SSM_INNER, SSM_STATE), lambda i: (i, 0, 0))],
        out_shape=[jax.ShapeDtypeStruct((nrow, SSM_INNER), F32),
                   jax.ShapeDtypeStruct((nchunk, SSM_INNER, SSM_STATE), BF16)],
        scratch_shapes=[pltpu.VMEM((SSM_INNER, SSM_STATE), F32), pltpu.VMEM((CHUNK, SSM_XBC), F32),
                        wide(), wide(), wide(), wide(), group(), group(), group(),
                        pltpu.VMEM((SUBLANES, SSM_HEADS), F32)],
        compiler_params=_params(("arbitrary",)),
    )(pre, dt, a_cum, a_flat, d_x, ind)


def _ssd_bwd(name, pre, dt, a_cum, a_flat, d_x, ind, ind_t, states, dy):
    nrow = pre.shape[0]
    tm = SSD_TM
    ntile = nrow // tm

    def body(pre_ref, dt_ref, ac_ref, af_ref, dx_ref, ind_ref, indt_ref, st_ref, dy_ref,
             dpre_ref, ddt_ref, da_ref, daf_ref, dd_ref,
             dh_ref, xbc_ref, dxbc_ref, ax_ref, dtx_ref, eax_ref, eex_ref, ra_ref, ts_ref, r2_ref,
             m_ref, l_ref, xd_ref, dm_ref, dxd_ref, fold_ref, hd_ref, tail_ref):
        @pl.when(pl.program_id(0) == 0)
        def _():
            dh_ref[...] = jnp.zeros_like(dh_ref)
            dd_ref[...] = jnp.zeros_like(dd_ref)

        causal = _causal_tiled()
        ind, ind_t = ind_ref[...], indt_ref[...]
        is_last_row = lax.broadcasted_iota(jnp.int32, (CHUNK, 1), 0) == CHUNK - 1
        ones = jnp.ones((CHUNK, SSM_STATE), BF16)

        def chunk(k, ddx):
            ci = SSD_CHUNKS - 1 - k
            rows = pl.ds(pl.multiple_of(ci * CHUNK, CHUNK), CHUNK)
            pre_v = pre_ref[rows, :].astype(F32)
            xbc_ref[...] = jax.nn.silu(pre_v)
            _chunk_decays(ci, dt_ref, ac_ref, ind, ax_ref, dtx_ref, eax_ref, eex_ref, tail_ref)
            ddx_parts = []
            for g in range(SSM_GROUPS):
                gcols = slice(g * HP, (g + 1) * HP)
                bcols = slice(B_OFF + g * SSM_STATE, B_OFF + (g + 1) * SSM_STATE)
                ccols = slice(C_OFF + g * SSM_STATE, C_OFF + (g + 1) * SSM_STATE)
                bm, cm, m_all, decay = _group_decay(ci, g, ax_ref, af_ref, xbc_ref, causal)
                m_ref[...] = m_all
                l_ref[...] = decay
                x_g = xbc_ref[:, gcols]
                xd = x_g * dtx_ref[:, gcols]
                xd_ref[...] = xd
                h_g = st_ref[ci, gcols, :]
                dh_g = dh_ref[gcols, :]
                dy_g = dy_ref[rows, gcols]
                for hh in range(SSM_HPG):
                    h = g * SSM_HPG + hh
                    hcols = slice(h * SSM_HEAD_DIM, (h + 1) * SSM_HEAD_DIM)
                    lc = slice(hh * SSM_HEAD_DIM, (hh + 1) * SSM_HEAD_DIM)
                    dy_h = dy_ref[rows, hcols]
                    dm_ref[:, lc] = _dot(dy_h, xd_ref[:, lc], 1, 1)
                    dxd_ref[:, lc] = _dot(m_ref[:, lc], dy_h, 0, 0)
                ebdh = eex_ref[:, gcols] * _dot(bm, dh_g, 1, 1)
                dxd = dxd_ref[...] + ebdh
                dm = dm_ref[...]
                t = dm * l_ref[...]
                t128 = (t[:, 0:LANES] + t[:, LANES:2 * LANES]) + (t[:, 2 * LANES:3 * LANES] + t[:, 3 * LANES:])
                fold_ref[...] = t128 + pltpu.roll(t128, CHUNK, axis=1)
                dw_sum = fold_ref[:, 0:CHUNK]
                q = dm * m_ref[...]
                dyea = dy_g * eax_ref[:, gcols]
                ra_ref[:, gcols] = q + dyea * _dot(cm, h_g, 1, 1)
                ts_ref[:, gcols] = xd * ebdh
                r2_ref[:, gcols] = dxd * x_g
                daf_ref[ci, :, gcols] = -jnp.sum(q, axis=0, keepdims=True)
                ddx_parts.append(jnp.sum(dy_g * x_g, axis=0, keepdims=True))
                dxbc_ref[:, gcols] = dxd * dtx_ref[:, gcols] + dx_ref[:, gcols] * dy_g
                dxbc_ref[:, ccols] = _dot(dw_sum, bm) + _dot(dyea, h_g)
                dxbc_ref[:, bcols] = _dot(dw_sum, cm, 0, 0) + _dot(xd * eex_ref[:, gcols], dh_g)
                dh_new = _dot(dyea, cm, 0, 0)
                for hh in range(SSM_HPG):
                    h = g * SSM_HPG + hh
                    hrows = slice(h * SSM_HEAD_DIM, (h + 1) * SSM_HEAD_DIM)
                    lrows = slice(hh * SSM_HEAD_DIM, (hh + 1) * SSM_HEAD_DIM)
                    hd_ref[h:h + 1, :] = jnp.sum(st_ref[ci, hrows, :] * dh_ref[hrows, :], axis=0, keepdims=True)
                    dh_ref[hrows, :] = tail_ref[SUBLANES - 1:SUBLANES, h:h + 1] * dh_ref[hrows, :] + dh_new[lrows, :]
            ra = _dot_exact(ra_ref[...], ind_t)
            ts = _dot_exact(ts_ref[...], ind_t)
            hdh = sum(lax.dot_general(ones, p, (((1,), (1,)), ((), ())), preferred_element_type=F32)
                      for p in _split3(hd_ref[...]))
            da_last = jnp.sum(ts, axis=0, keepdims=True) + tail_ref[SUBLANES - 1:SUBLANES, :] * hdh
            da_ref[rows, :] = ra - ts + jnp.where(is_last_row, da_last, 0.0)
            ddt_ref[rows, :] = _dot_exact(r2_ref[...], ind_t)
            sig = jax.nn.sigmoid(pre_v)
            dpre_ref[rows, :] = (dxbc_ref[...] * (sig * (1.0 + pre_v * (1.0 - sig)))).astype(dpre_ref.dtype)
            return ddx + jnp.concatenate(ddx_parts, axis=1)

        ddx = lax.fori_loop(0, SSD_CHUNKS, chunk, jnp.zeros((1, SSM_INNER), F32))
        dd_ref[...] += _dot_exact(jnp.broadcast_to(ddx, (SUBLANES, SSM_INNER)), ind_t)

    rev = lambda i: ntile - 1 - i
    whole = lambda a: pl.BlockSpec(a.shape, lambda i, nd=a.ndim: (0,) * nd)
    wide = lambda: pltpu.VMEM((CHUNK, SSM_INNER), F32)
    group = lambda: pltpu.VMEM((CHUNK, HP), F32)
    return pl.pallas_call(
        body, name=name, grid=(ntile,),
        in_specs=[pl.BlockSpec((tm, SSM_XBC), lambda i: (rev(i), 0)), pl.BlockSpec((tm, SSM_HEADS), lambda i: (rev(i), 0)),
                  pl.BlockSpec((tm, SSM_HEADS), lambda i: (rev(i), 0)),
                  pl.BlockSpec((SSD_CHUNKS, 1, SSM_INNER), lambda i: (rev(i), 0, 0)),
                  whole(d_x), whole(ind), whole(ind_t),
                  pl.BlockSpec((SSD_CHUNKS, SSM_INNER, SSM_STATE), lambda i: (rev(i), 0, 0)),
                  pl.BlockSpec((tm, SSM_INNER), lambda i: (rev(i), 0))],
        out_specs=[pl.BlockSpec((tm, SSM_XBC), lambda i: (rev(i), 0)), pl.BlockSpec((tm, SSM_HEADS), lambda i: (rev(i), 0)),
                   pl.BlockSpec((tm, SSM_HEADS), lambda i: (rev(i), 0)),
                   pl.BlockSpec((SSD_CHUNKS, 1, SSM_INNER), lambda i: (rev(i), 0, 0)),
                   pl.BlockSpec((SUBLANES, SSM_HEADS), lambda i: (0, 0))],
        out_shape=[jax.ShapeDtypeStruct((nrow, SSM_XBC), BF16), jax.ShapeDtypeStruct((nrow, SSM_HEADS), F32),
                   jax.ShapeDtypeStruct((nrow, SSM_HEADS), F32), jax.ShapeDtypeStruct((nrow // CHUNK, 1, SSM_INNER), F32),
                   jax.ShapeDtypeStruct((SUBLANES, SSM_HEADS), F32)],
        scratch_shapes=[pltpu.VMEM((SSM_INNER, SSM_STATE), F32), pltpu.VMEM((CHUNK, SSM_XBC), F32),
                        pltpu.VMEM((CHUNK, SSM_XBC), F32), wide(), wide(), wide(), wide(), wide(), wide(), wide(),
                        group(), group(), group(), group(), group(), pltpu.VMEM((CHUNK, LANES), F32),
                        pltpu.VMEM((SSM_HEADS, SSM_STATE), F32), pltpu.VMEM((SUBLANES, SSM_HEADS), F32)],
        compiler_params=_params(("arbitrary",)),
    )(pre, dt, a_cum, a_flat, d_x, ind, ind_t, states, dy)


def _local_step(x, target, w):
    g = {}
    bs_col = w["gmlp_bs"].reshape(GMLP_GROUPS, GMLP_BLOCK, 1)
    b0, b1 = w["gate_bias"][0:1], w["gate_bias"][1:2]

    xn = _rms_fwd("mix_norm", x, w["mix_norm_w"])
    big = dict(bm=1024, bn=1024, bk=1024)
    act16 = dict(out_dtype=BF16, **big)
    gates = _mm("in_gates", xn, w["w_g"], **act16)
    za = _mm("in_gmlp", xn, w["w_za"], **act16)
    z = _mm("in_z", xn, w["w_z"], **act16)
    xbc = _mm("in_xbc", xn, w["w_xbc"], **act16)
    dt_raw = _mm("in_dt", xn, w["w_dt"], bm=1024, bn=SSM_HEADS, bk=1024)

    ya_pre = _gmlp_fwd("gmlp_fwd", za, w["gmlp_ln_w"], w["gmlp_ln_b"], w["gmlp_ws"], bs_col)
    y_a = _mm("proj_a", ya_pre, w["w_proj_a"], **act16)

    pre = _conv_fwd("ssm_conv_fwd", xbc, w["ssm_conv_w"], w["ssm_conv_b"], tc=1024)
    dt, a_cum = _dt_prep("dt_prep", dt_raw, w["ssm_dt_bias"], w["ssm_a_log"])
    a_flat = jnp.transpose(a_cum.reshape(-1, CHUNK, SSM_HEADS), (0, 2, 1)).reshape(-1, 1, SSM_INNER)
    d_x = jnp.repeat(w["ssm_d"], SSM_HEAD_DIM, axis=1)
    ind = _head_indicator()
    y_ssd, states = _ssd_fwd("ssd_fwd", pre, dt, a_cum, a_flat, d_x, ind)
    yb_pre = _gate_norm_fwd("gate_norm_fwd", y_ssd, z, w["ssm_norm_w"])
    y_b = _mm("proj_b", yb_pre, w["w_proj_b"], **act16)

    merged = _merge_fwd("merge_fwd", gates, y_a, y_b, b0, b1)
    h1 = _mm("out_proj", merged, w["w_out"], res=x, **big)

    hn = _rms_fwd("ffn_norm", h1, w["ffn_norm_w"])
    half = dict(bm=1024, bn=D_FF // 2, bk=1024, out_dtype=BF16)
    pg = _mm("ffn_up_gate", hn, w["w_up_g"], **half)
    pv = _mm("ffn_up_val", hn, w["w_up_v"], **half)
    cw, cb = w["ffn_conv_w"], w["ffn_conv_b"]
    gate, val, act = _ffn_act_fwd("ffn_act_fwd", pg, pv, cw[:, :D_FF], cw[:, D_FF:], cb[:, :D_FF], cb[:, D_FF:],
                                  tc=D_FF // 2)
    h2 = _mm("ffn_down", act, w["w_down"], res=h1, bm=512, bn=1024, bk=D_FF // 2)

    dh2, loss_part, g["final_norm_w"] = _final_loss("final_loss", h2, target, w["final_norm_w"].reshape(1, D_MODEL))

    dact = _mm("d_act", dh2, w["w_down"], tb=True, **half)
    wgrad = dict(ta=True, bk=512, out_dtype=BF16)
    g["w_down"] = _mm("dw_down", act, dh2, bm=D_FF // 2, bn=1024, **wgrad)
    dgate, dval = _ffn_act_bwd("ffn_act_bwd", dact, gate, val)
    dpg, dcwg, dcbg = _conv_bwd("ffn_conv_bwd_gate", dgate, pg, cw[:, :D_FF], tc=D_FF // 2)
    dpv, dcwv, dcbv = _conv_bwd("ffn_conv_bwd_val", dval, pv, cw[:, D_FF:], tc=D_FF // 2)
    g["ffn_conv_w"] = jnp.concatenate([dcwg, dcwv], axis=1)
    g["ffn_conv_b"] = jnp.concatenate([dcbg, dcbv], axis=1)
    back = dict(bm=1024, bn=1024, bk=D_FF // 2)
    dhn = _mm("d_hn_gate", dpg, w["w_up_g"], tb=True, **back)
    dhn = _mm("d_hn_val", dpv, w["w_up_v"], tb=True, res=dhn, **back)
    g["w_up_g"] = _mm("dw_up_gate", hn, dpg, bm=1024, bn=D_FF // 2, **wgrad)
    g["w_up_v"] = _mm("dw_up_val", hn, dpv, bm=1024, bn=D_FF // 2, **wgrad)
    dh1, g["ffn_norm_w"] = _rms_bwd("ffn_norm_bwd", h1, w["ffn_norm_w"], dhn, dh2)

    dmerged = _mm("d_merged", dh1, w["w_out"], tb=True, **act16)
    g["w_out"] = _mm("dw_out", merged, dh1, bm=1024, bn=1024, **wgrad)
    dgates, dya, dyb, db0, db1 = _merge_bwd("merge_bwd", gates, y_a, y_b, dmerged, b0, b1)
    g["gate_bias"] = jnp.concatenate([db0, db1], axis=0)

    dya_pre = _mm("d_ya_pre", dya, w["w_proj_a"], tb=True, **act16)
    g["w_proj_a"] = _mm("dw_proj_a", ya_pre, dya, bm=1024, bn=1024, **wgrad)
    dza, g["gmlp_ln_w"], g["gmlp_ln_b"], g["gmlp_ws"], dbs = _gmlp_bwd(
        "gmlp_bwd", za, dya_pre, w["gmlp_ln_w"], w["gmlp_ln_b"], w["gmlp_ws"], bs_col)
    g["gmlp_bs"] = dbs.reshape(GMLP_GROUPS, GMLP_BLOCK)

    dyb_pre = _mm("d_yb_pre", dyb, w["w_proj_b"], tb=True, **act16)
    g["w_proj_b"] = _mm("dw_proj_b", yb_pre, dyb, bm=1024, bn=1024, **wgrad)
    dy_ssd, dz, g["ssm_norm_w"] = _gate_norm_bwd("gate_norm_bwd", y_ssd, z, dyb_pre, w["ssm_norm_w"])
    dpre, ddt, da_tok, da_flat, dd = _ssd_bwd("ssd_bwd", pre, dt, a_cum, a_flat, d_x, ind, ind.T, states, dy_ssd)
    g["ssm_d"] = dd[0:1]
    da_src = jnp.transpose(da_flat.reshape(-1, SSM_HEADS, CHUNK), (0, 2, 1)).reshape(-1, SSM_HEADS)
    ddt_raw, g["ssm_dt_bias"], g["ssm_a_log"] = _dt_bwd("dt_bwd", dt_raw, ddt, da_tok, da_src,
                                                         w["ssm_dt_bias"], w["ssm_a_log"])
    dxbc, g["ssm_conv_w"], g["ssm_conv_b"] = _conv_bwd("ssm_conv_bwd", dpre, xbc, w["ssm_conv_w"], tc=1024)

    dxn = _mm("d_xn_gates", dgates, w["w_g"], tb=True, **big)
    dxn = _mm("d_xn_gmlp", dza, w["w_za"], tb=True, res=dxn, **big)
    dxn = _mm("d_xn_z", dz, w["w_z"], tb=True, res=dxn, **big)
    dxn = _mm("d_xn_xbc", dxbc, w["w_xbc"], tb=True, res=dxn, **big)
    dxn = _mm("d_xn_dt", ddt_raw, w["w_dt"], tb=True, res=dxn, bm=1024, bn=1024, bk=SSM_HEADS)
    g["w_g"] = _mm("dw_gates", xn, dgates, bm=1024, bn=1024, **wgrad)
    g["w_za"] = _mm("dw_gmlp", xn, dza, bm=1024, bn=1024, **wgrad)
    g["w_z"] = _mm("dw_z", xn, dz, bm=1024, bn=1024, **wgrad)
    g["w_xbc"] = _mm("dw_xbc", xn, dxbc, bm=1024, bn=1024, **wgrad)
    g["w_dt"] = _mm("dw_dt", xn, ddt_raw, bm=1024, bn=SSM_HEADS, **wgrad)
    grad_x, g["mix_norm_w"] = _rms_bwd("mix_norm_bwd", x, w["mix_norm_w"], dxn, dh1)
    return loss_part, grad_x, g


def _position():
    return lax.axis_index("x"), lax.axis_index("y"), lax.axis_index("c")


def _own_slot(stack, own):
    chip = 2 * lax.axis_index("x") + lax.axis_index("y")
    return lax.dynamic_update_index_in_dim(stack, own, chip, axis=0)


def _scatter_chips(name, arrs):
    n = len(arrs)

    def body(*refs):
        ins, outs = refs[:n], refs[n:2 * n]
        send_sems, recv_sems = refs[2 * n:]
        x, y, c = _position()
        me = 2 * x + y
        peers = [(1 - x, y), (x, 1 - y), (1 - x, 1 - y)]
        sends = []
        for i in range(n):
            for k, (px, py) in enumerate(peers):
                cp = pltpu.make_async_remote_copy(
                    src_ref=ins[i].at[2 * px + py], dst_ref=outs[i].at[me],
                    send_sem=send_sems.at[i, k], recv_sem=recv_sems.at[i, k],
                    device_id=(px, py, c), device_id_type=MESH)
                cp.start()
                sends.append(cp)
        for i in range(n):
            for k, (px, py) in enumerate(peers):
                pltpu.make_async_remote_copy(
                    src_ref=ins[i].at[me], dst_ref=outs[i].at[2 * px + py],
                    send_sem=send_sems.at[i, k], recv_sem=recv_sems.at[i, k],
                    device_id=(px, py, c), device_id_type=MESH).wait_recv()
        for cp in sends:
            cp.wait_send()

    hbm = pl.BlockSpec(memory_space=pl.ANY)
    return pl.pallas_call(
        body, name=name,
        in_specs=[hbm] * n, out_specs=[hbm] * n,
        out_shape=[jax.ShapeDtypeStruct(a.shape, a.dtype) for a in arrs],
        scratch_shapes=[pltpu.SemaphoreType.DMA((n, N_CHIPS - 1)), pltpu.SemaphoreType.DMA((n, N_CHIPS - 1))],
        compiler_params=pltpu.CompilerParams(has_side_effects=True),
    )(*arrs)


def _half_rows(ref_rows, which):
    half = ref_rows // 2
    return pl.ds(pl.multiple_of(which * half, 2 * SUBLANES), half)


def _gather_chips_split(name, split, whole):
    ns, nw = len(split), len(whole)
    arrs = list(split) + list(whole)
    n = ns + nw

    def body(*refs):
        ins, outs = refs[:n], refs[n:2 * n]
        send_sems, recv_sems = refs[2 * n:]
        x, y, c = _position()
        me = 2 * x + y
        sibling = (x, y, 1 - c)
        chips = [(1 - x, y), (x, 1 - y), (1 - x, 1 - y)]

        def remote(i, k, src, dst, to):
            return pltpu.make_async_remote_copy(src_ref=src, dst_ref=dst, send_sem=send_sems.at[i, k],
                                                recv_sem=recv_sems.at[i, k], device_id=to, device_id_type=MESH)

        sends = []
        for i in range(n):
            rows = _half_rows(arrs[i].shape[0], c) if i < ns else slice(None)
            for k, (px, py) in enumerate(chips):
                cp = remote(i, k, ins[i].at[rows], outs[i].at[me, rows], (px, py, c))
                cp.start()
                sends.append(cp)
        for i in range(ns):
            rows = _half_rows(arrs[i].shape[0], c)
            for k, (px, py) in enumerate(chips):
                landed = outs[i].at[2 * px + py, rows]
                remote(i, k, landed, landed, (px, py, c)).wait_recv()
                cp = remote(i, N_CHIPS - 1 + k, landed, landed, sibling)
                cp.start()
                sends.append(cp)
        for i in range(ns, n):
            for k, (px, py) in enumerate(chips):
                landed = outs[i].at[2 * px + py]
                remote(i, k, landed, landed, (px, py, c)).wait_recv()
        for i in range(ns):
            rows = _half_rows(arrs[i].shape[0], 1 - c)
            for k, (px, py) in enumerate(chips):
                landed = outs[i].at[2 * px + py, rows]
                remote(i, N_CHIPS - 1 + k, landed, landed, sibling).wait_recv()
        for cp in sends:
            cp.wait_send()

    hbm = pl.BlockSpec(memory_space=pl.ANY)
    nsem = 2 * (N_CHIPS - 1)
    return pl.pallas_call(
        body, name=name, in_specs=[hbm] * n, out_specs=[hbm] * n,
        out_shape=[jax.ShapeDtypeStruct((N_CHIPS,) + a.shape, a.dtype) for a in arrs],
        scratch_shapes=[pltpu.SemaphoreType.DMA((n, nsem)), pltpu.SemaphoreType.DMA((n, nsem))],
        compiler_params=pltpu.CompilerParams(has_side_effects=True),
    )(*arrs)


def _swap_cores(name, arrs):
    n = len(arrs)

    def body(*refs):
        ins, outs = refs[:n], refs[n:2 * n]
        send_sems, recv_sems = refs[2 * n:]
        x, y, c = _position()
        copies = [pltpu.make_async_remote_copy(src_ref=ins[i], dst_ref=outs[i], send_sem=send_sems.at[i],
                                               recv_sem=recv_sems.at[i], device_id=(x, y, 1 - c), device_id_type=MESH)
                  for i in range(n)]
        for cp in copies:
            cp.start()
        for cp in copies:
            cp.wait_recv()
        for cp in copies:
            cp.wait_send()

    hbm = pl.BlockSpec(memory_space=pl.ANY)
    return pl.pallas_call(
        body, name=name, in_specs=[hbm] * n, out_specs=[hbm] * n,
        out_shape=[jax.ShapeDtypeStruct(a.shape, a.dtype) for a in arrs],
        scratch_shapes=[pltpu.SemaphoreType.DMA((n,)), pltpu.SemaphoreType.DMA((n,))],
        compiler_params=pltpu.CompilerParams(has_side_effects=True),
    )(*arrs)


def _row_half(a, which, axis):
    half = a.shape[axis] // 2
    return lax.dynamic_slice_in_dim(a, which * half, half, axis=axis)


def _all_reduce(name, pack):
    def body(in_ref, out_ref, buf, send_sems, recv_sems):
        x, y, c = _position()
        me = 4 * x + 2 * y + c
        flips = [(dx, dy, dc) for dx in (0, 1) for dy in (0, 1) for dc in (0, 1) if (dx, dy, dc) != (0, 0, 0)]
        peers = [((1 - x) if dx else x, (1 - y) if dy else y, (1 - c) if dc else c) for dx, dy, dc in flips]
        buf[me] = in_ref[...]
        sends = []
        for k, peer in enumerate(peers):
            cp = pltpu.make_async_remote_copy(src_ref=in_ref, dst_ref=buf.at[me], send_sem=send_sems.at[k],
                                              recv_sem=recv_sems.at[k], device_id=peer, device_id_type=MESH)
            cp.start()
            sends.append(cp)
        for k, (px, py, pc) in enumerate(peers):
            pltpu.make_async_remote_copy(src_ref=in_ref, dst_ref=buf.at[4 * px + 2 * py + pc], send_sem=send_sems.at[k],
                                         recv_sem=recv_sems.at[k], device_id=(px, py, pc), device_id_type=MESH).wait_recv()
        total = buf[0]
        for j in range(1, N_DEV):
            total = total + buf[j]
        out_ref[...] = total
        for cp in sends:
            cp.wait_send()

    vmem = pl.BlockSpec(memory_space=pltpu.VMEM)
    return pl.pallas_call(
        body, name=name, in_specs=[vmem], out_specs=vmem,
        out_shape=jax.ShapeDtypeStruct(pack.shape, F32),
        scratch_shapes=[pltpu.VMEM((N_DEV,) + pack.shape, F32), pltpu.SemaphoreType.DMA((N_DEV - 1,)),
                        pltpu.SemaphoreType.DMA((N_DEV - 1,))],
        compiler_params=pltpu.CompilerParams(has_side_effects=True, vmem_limit_bytes=VMEM_LIMIT_V7X),
    )(pack)


def _pack(arrs):
    rows = [a.reshape(-1, LANES) for a in arrs]
    total = sum(r.shape[0] for r in rows)
    rows.append(jnp.zeros((-total % SUBLANES, LANES), F32))
    return jnp.concatenate(rows, axis=0)


def _unpack(pack, shapes):
    out, off = [], 0
    for s in shapes:
        nrow = 1
        for d in s:
            nrow *= d
        nrow //= LANES
        out.append(pack[off:off + nrow].reshape(s))
        off += nrow
    return out


SMALL = ["mix_norm_w", "gate_bias", "gmlp_ln_w", "gmlp_ln_b", "gmlp_ws", "gmlp_bs", "ssm_conv_w", "ssm_conv_b",
         "ssm_dt_bias", "ssm_a_log", "ssm_d", "ssm_norm_w", "ffn_norm_w", "ffn_conv_w", "ffn_conv_b", "final_norm_w"]
SMALL_SHARDED = ("gate_bias", "ssm_conv_w", "ffn_conv_w")
BIG = ["w_in", "w_proj_a", "w_proj_b", "w_out", "ffn_w_up", "ffn_w_down"]
WEIGHTS = ["mix_norm_w", "w_in", "gate_bias", "gmlp_ln_w", "gmlp_ln_b", "gmlp_ws", "gmlp_bs", "ssm_conv_w",
           "ssm_conv_b", "ssm_dt_bias", "ssm_a_log", "ssm_d", "ssm_norm_w", "w_proj_a", "w_proj_b", "w_out",
           "ffn_norm_w", "ffn_w_up", "ffn_conv_w", "ffn_conv_b", "ffn_w_down", "final_norm_w"]
IN_SPLITS = [0, 2048, 4096, 6144, 9216, 9248]


def _columns_from_chips(stack):
    return jnp.transpose(stack, (1, 0, 2)).reshape(stack.shape[1], -1)


def _columns_to_chips(full, parts=N_CHIPS):
    rows, cols = full.shape
    return jnp.transpose(full.reshape(rows, parts, cols // parts), (1, 0, 2))


def kernel(x, mix_norm_w, w_in, gate_bias, gmlp_ln_w, gmlp_ln_b, gmlp_ws, gmlp_bs, ssm_conv_w, ssm_conv_b, ssm_dt_bias, ssm_a_log, ssm_d, ssm_norm_w, w_proj_a, w_proj_b, w_out, ffn_norm_w, ffn_w_up, ffn_conv_w, ffn_conv_b, ffn_w_down, final_norm_w, loss_target, m_mix_norm_w, m_w_in, m_gate_bias, m_gmlp_ln_w, m_gmlp_ln_b, m_gmlp_ws, m_gmlp_bs, m_ssm_conv_w, m_ssm_conv_b, m_ssm_dt_bias, m_ssm_a_log, m_ssm_d, m_ssm_norm_w, m_w_proj_a, m_w_proj_b, m_w_out, m_ffn_norm_w, m_ffn_w_up, m_ffn_conv_w, m_ffn_conv_b, m_ffn_w_down, m_final_norm_w, v_mix_norm_w, v_w_in, v_gate_bias, v_gmlp_ln_w, v_gmlp_ln_b, v_gmlp_ws, v_gmlp_bs, v_ssm_conv_w, v_ssm_conv_b, v_ssm_dt_bias, v_ssm_a_log, v_ssm_d, v_ssm_norm_w, v_w_proj_a, v_w_proj_b, v_w_out, v_ffn_norm_w, v_ffn_w_up, v_ffn_conv_w, v_ffn_conv_b, v_ffn_w_down, v_final_norm_w):
    args = dict(locals())
    weights = {n: args[n] for n in WEIGHTS}
    moments_m = {n: args["m_" + n] for n in WEIGHTS}
    moments_v = {n: args["v_" + n] for n in WEIGHTS}
    chip = 2 * lax.axis_index("x") + lax.axis_index("y")

    shards = [weights[n][0].astype(BF16) for n in BIG] + [weights[n][0] for n in SMALL_SHARDED]
    gathered = _gather_chips_split("gather_weights", shards[:len(BIG)], shards[len(BIG):])
    gathered = [_own_slot(stack, own) for stack, own in zip(gathered, shards)]
    w_in_s, w_pa_s, w_pb_s, w_out_s, w_up_s, w_down_s, gb_s, scw_s, fcw_s = gathered
    w_in_full = _columns_from_chips(w_in_s)
    full = {"w_" + nm: w_in_full[:, IN_SPLITS[k]:IN_SPLITS[k + 1]] for k, nm in enumerate(["g", "za", "z", "xbc", "dt"])}
    full["w_proj_a"] = w_pa_s.reshape(-1, D_MODEL)
    full["w_proj_b"] = w_pb_s.reshape(-1, D_MODEL)
    full["w_out"] = w_out_s.reshape(-1, D_MODEL)
    full["w_down"] = w_down_s.reshape(-1, D_MODEL)
    full["w_up_g"] = _columns_from_chips(w_up_s[:2])
    full["w_up_v"] = _columns_from_chips(w_up_s[2:])
    full["gate_bias"] = _columns_from_chips(gb_s)
    full["ssm_conv_w"] = _columns_from_chips(scw_s)
    full["ffn_conv_w"] = _columns_from_chips(fcw_s)
    for n in SMALL:
        if n not in SMALL_SHARDED:
            full[n] = weights[n] if n == "final_norm_w" else weights[n][0]
    for n in ("mix_norm_w", "ffn_norm_w", "ssm_conv_b", "ssm_dt_bias", "ssm_a_log", "ssm_d", "ssm_norm_w", "ffn_conv_b"):
        full[n] = full[n].reshape(1, -1)

    loss_part, grad_x, g = _local_step(x[0], loss_target[0], full)

    per_head = ["ssm_dt_bias", "ssm_a_log", "ssm_d"]
    rest = [n for n in SMALL if n not in per_head]
    head_row = jnp.concatenate([g[n] for n in per_head] + [jnp.zeros((1, LANES - 3 * SSM_HEADS), F32)], axis=1)
    pack = _pack([loss_part, head_row] + [g[n] for n in rest])
    reduced = _unpack(_all_reduce("reduce_small", pack), [(1, LANES), (1, LANES)] + [g[n].shape for n in rest])
    loss = reduced[0][0, 0]
    small_grads = {n: reduced[1][:, k * SSM_HEADS:(k + 1) * SSM_HEADS] for k, n in enumerate(per_head)}
    for n, r in zip(rest, reduced[2:]):
        if n in SMALL_SHARDED:
            width = weights[n].shape[2]
            r = lax.dynamic_slice_in_dim(r, chip * width, width, axis=1)
        small_grads[n] = r
    two_d = lambda a: a.reshape(-1, a.shape[-1])
    upd = _adamw_small("adamw_small", *[[two_d(d[n]) for n in SMALL]
                                        for d in (weights, small_grads, moments_m, moments_v)])
    small_out = [[small_grads[n] for n in SMALL]] + list(upd)
    small_out = [[a.reshape(weights[n].shape) for n, a in zip(SMALL, kind)] for kind in small_out]

    dw_in = jnp.concatenate([g["w_g"], g["w_za"], g["w_z"], g["w_xbc"], g["w_dt"]], axis=1)
    stacks = [
        _columns_to_chips(dw_in),
        g["w_proj_a"].reshape(N_CHIPS, -1, D_MODEL),
        g["w_proj_b"].reshape(N_CHIPS, -1, D_MODEL),
        g["w_out"].reshape(N_CHIPS, -1, D_MODEL),
        jnp.concatenate([_columns_to_chips(g["w_up_g"], 2), _columns_to_chips(g["w_up_v"], 2)], axis=0),
        g["w_down"].reshape(N_CHIPS, -1, D_MODEL),
    ]
    half_tiles = {"w_in": 128, "w_proj_a": 128, "w_proj_b": 256, "w_out": 128, "ffn_w_up": 128, "ffn_w_down": 176}
    tiles = {"w_in": 128, "w_proj_a": 256, "w_proj_b": 256, "w_out": 256, "ffn_w_up": 128, "ffn_w_down": 176}
    core = lax.axis_index("c")
    own_half = [_row_half(s, core, 1) for s in stacks]
    other_half = _swap_cores("pair_grads", [_row_half(s, 1 - core, 1) for s in stacks])
    pair = [_pair_sum("pair_" + n, a, b, tm=half_tiles[n]) for n, a, b in zip(BIG, own_half, other_half)]
    received = _scatter_chips("scatter_grads", pair)
    received = [_own_slot(r, lax.dynamic_index_in_dim(p, chip, 0, keepdims=False)) for r, p in zip(received, pair)]
    halves = [_sum_slots("sum_" + n, r, tm=half_tiles[n], rs=2 * SUBLANES) for n, r in zip(BIG, received)]
    other = _swap_cores("join_grads", halves)
    reduced = [jnp.where(core == 0, jnp.concatenate([a, b], axis=0), jnp.concatenate([b, a], axis=0))
               for a, b in zip(halves, other)]
    big_out = {}
    for n, grad in zip(BIG, reduced):
        big_out[n] = _adamw("adamw_" + n, weights[n][0], grad, moments_m[n][0], moments_v[n][0],
                            tm=tiles[n], rs=SUBLANES)

    per_kind = [[], [], [], []]
    for n in WEIGHTS:
        for kind in range(4):
            if n in big_out:
                per_kind[kind].append(big_out[n][kind].reshape(weights[n].shape))
            else:
                per_kind[kind].append(small_out[kind][SMALL.index(n)])
    return (loss, grad_x[None], *per_kind[0], *per_kind[1], *per_kind[2], *per_kind[3])
```

```python
import jax
import jax.numpy as jnp
from jax import lax
from jax.experimental import pallas as pl
from jax.experimental.pallas import tpu as pltpu

F32 = jnp.float32
BF16 = jnp.bfloat16
MESH = pl.DeviceIdType.MESH

EPS = 1e-5
D_MODEL = 1024
GMLP_BLOCK = 128
GMLP_GROUPS = 8
CHUNK = 64
SSM_INNER = 2048
SSM_HEADS = 32
SSM_HEAD_DIM = 64
SSM_GROUPS = 4
SSM_HPG = 8
SSM_STATE = 128
SSM_CONV = 4
SSM_XBC = 3072
D_FF = 2816
FFN_CONV = 3
N_CHIPS = 4
N_DEV = 8

ADAM_LR = 0.001
ADAM_B1 = 0.9
ADAM_B2 = 0.999
ADAM_EPS = 1e-08
ADAM_WD = 0.01
ADAM_STEP = 10

VMEM_LIMIT_V7X = 56 * 1024 * 1024
SUBLANES = 8
LANES = 128


def _params(sem=None):
    return pltpu.CompilerParams(dimension_semantics=sem, vmem_limit_bytes=VMEM_LIMIT_V7X)


def _dot(a, b, ca=1, cb=0):
    return lax.dot_general(a.astype(BF16), b.astype(BF16), (((ca,), (cb,)), ((), ())),
                           preferred_element_type=F32)


def _mm(name, a, b, *, ta=False, tb=False, out_dtype=F32, bm, bn, bk, res=None):
    m, k = (a.shape[1], a.shape[0]) if ta else a.shape
    k2, n = (b.shape[1], b.shape[0]) if tb else b.shape
    assert k == k2 and m % bm == 0 and n % bn == 0 and k % bk == 0, (name, a.shape, b.shape)
    nk = k // bk
    a_spec = (pl.BlockSpec((bk, bm), lambda i, j, kk: (kk, i)) if ta
              else pl.BlockSpec((bm, bk), lambda i, j, kk: (i, kk)))
    b_spec = (pl.BlockSpec((bn, bk), lambda i, j, kk: (j, kk)) if tb
              else pl.BlockSpec((bk, bn), lambda i, j, kk: (kk, j)))
    o_spec = pl.BlockSpec((bm, bn), lambda i, j, kk: (i, j))
    has_res = res is not None

    def body(*refs):
        a_ref, b_ref = refs[0], refs[1]
        r_ref = refs[2] if has_res else None
        o_ref = refs[3] if has_res else refs[2]
        p = _dot(a_ref[...], b_ref[...], 0 if ta else 1, 1 if tb else 0)

        def finish(total):
            if has_res:
                total = total + r_ref[...]
            o_ref[...] = total.astype(out_dtype)

        if nk == 1:
            finish(p)
        else:
            acc_ref = refs[-1]
            kk = pl.program_id(2)

            @pl.when(kk == 0)
            def _():
                acc_ref[...] = p

            @pl.when(kk > 0)
            def _():
                acc_ref[...] += p

            @pl.when(kk == nk - 1)
            def _():
                finish(acc_ref[...])

    return pl.pallas_call(
        body, name=name,
        grid=(m // bm, n // bn, nk),
        in_specs=[a_spec, b_spec] + ([o_spec] if has_res else []),
        out_specs=o_spec,
        out_shape=jax.ShapeDtypeStruct((m, n), out_dtype),
        scratch_shapes=[pltpu.VMEM((bm, bn), F32)] if nk > 1 else [],
        compiler_params=_params(("parallel", "parallel", "arbitrary")),
    )(*([a, b] + ([res] if has_res else [])))


def _rows(name, fn, ins, params, outs, accs, *, tm, rs, unroll=4):
    nrow = ins[0][0].shape[-2]
    while tm % (rs * unroll):
        unroll //= 2
    assert nrow % tm == 0 and tm % rs == 0, (name, nrow, tm, rs)
    n_in, n_p, n_out, n_acc = len(ins), len(params), len(outs), len(accs)
    in_specs = []
    for spec in ins:
        arr, width, cb = spec[:3]
        if len(spec) == 4:
            in_specs.append(pl.BlockSpec((None, tm, width), lambda i, cb=cb, lead=spec[3]: (lead, i, cb)))
        else:
            in_specs.append(pl.BlockSpec((tm, width), lambda i, cb=cb: (i, cb)))
    for p in params:
        in_specs.append(pl.BlockSpec(p.shape, lambda i, nd=p.ndim: (0,) * nd))
    out_specs = [pl.BlockSpec((tm, w), lambda i: (i, 0)) for w, _ in outs]
    out_specs += [pl.BlockSpec(s, lambda i: (0, 0)) for s in accs]
    out_shape = [jax.ShapeDtypeStruct((nrow, w), dt) for w, dt in outs]
    out_shape += [jax.ShapeDtypeStruct(s, F32) for s in accs]

    def body(*refs):
        in_refs = refs[:n_in]
        p_refs = refs[n_in:n_in + n_p]
        o_refs = refs[n_in + n_p:n_in + n_p + n_out]
        a_refs = refs[n_in + n_p + n_out:]
        pv = [p[...] for p in p_refs]

        if n_acc:
            @pl.when(pl.program_id(0) == 0)
            def _():
                for a_ref in a_refs:
                    a_ref[...] = jnp.zeros_like(a_ref)

        def step(r, carry):
            for u in range(unroll):
                sl = pl.ds(pl.multiple_of((r * unroll + u) * rs, rs), rs)
                vals = [ref[sl, :].astype(F32) for ref in in_refs]
                row_out, sums = fn(*vals, *pv)
                for o_ref, v in zip(o_refs, row_out):
                    o_ref[sl, :] = v.astype(o_ref.dtype)
                carry = tuple(c + s for c, s in zip(carry, sums))
            return carry

        init = tuple(jnp.zeros(s, F32) for s in accs)
        total = lax.fori_loop(0, tm // (rs * unroll), step, init)
        for a_ref, t in zip(a_refs, total):
            a_ref[...] += t

    res = pl.pallas_call(
        body, name=name, grid=(nrow // tm,),
        in_specs=in_specs, out_specs=out_specs, out_shape=out_shape,
        compiler_params=_params(("arbitrary",)),
    )(*([s[0] for s in ins] + list(params)))
    return res


def _rms(x, w):
    return x * lax.rsqrt(jnp.mean(x * x, axis=-1, keepdims=True) + EPS) * w


def _colsum(v):
    return jnp.sum(v, axis=0, keepdims=True)


def _rms_fwd(name, x, w):
    def fn(xv, wv):
        return (_rms(xv, wv),), ()
    return _rows(name, fn, [(x, D_MODEL, 0)], [w], [(D_MODEL, BF16)], [], tm=512, rs=16)[0]


def _rms_bwd(name, x, w, dy, dres):
    def fn(xv, dyv, drv, wv):
        _, vjp = jax.vjp(_rms, xv, wv)
        dx, dw = vjp(dyv)
        return (drv + dx,), (dw,)
    return _rows(name, fn, [(x, D_MODEL, 0), (dy, D_MODEL, 0), (dres, D_MODEL, 0)], [w],
                 [(D_MODEL, F32)], [(1, D_MODEL)], tm=512, rs=16)


def _final_loss(name, h, target, w):
    def fn(hv, tv, wv):
        y, vjp = jax.vjp(_rms, hv, wv)
        err = y - tv
        part = 0.5 * jnp.sum(jnp.mean(err * err, axis=-1, keepdims=True), axis=0, keepdims=True)
        dh, dw = vjp(err / D_MODEL)
        return (dh,), (jnp.broadcast_to(part, (1, LANES)), dw)
    return _rows(name, fn, [(h, D_MODEL, 0), (target, D_MODEL, 0)], [w],
                 [(D_MODEL, F32)], [(1, LANES), (1, D_MODEL)], tm=512, rs=16)


def _merge(ga, gb, ya, yb, b0, b1):
    return jax.nn.sigmoid(ga + b0) * ya + jax.nn.sigmoid(gb + b1) * yb


def _merge_fwd(name, g, ya, yb, b0, b1):
    def fn(ga, gb, yav, ybv, b0v, b1v):
        return (_merge(ga, gb, yav, ybv, b0v, b1v),), ()
    return _rows(name, fn, [(g, D_MODEL, 0), (g, D_MODEL, 1), (ya, D_MODEL, 0), (yb, D_MODEL, 0)],
                 [b0, b1], [(D_MODEL, BF16)], [], tm=512, rs=16)[0]


def _merge_bwd(name, g, ya, yb, dm, b0, b1):
    def fn(ga, gb, yav, ybv, dmv, b0v, b1v):
        _, vjp = jax.vjp(_merge, ga, gb, yav, ybv, b0v, b1v)
        dga, dgb, dya, dyb, db0, db1 = vjp(dmv)
        return (jnp.concatenate([dga, dgb], axis=1), dya, dyb), (db0, db1)
    return _rows(name, fn,
                 [(g, D_MODEL, 0), (g, D_MODEL, 1), (ya, D_MODEL, 0), (yb, D_MODEL, 0), (dm, D_MODEL, 0)],
                 [b0, b1], [(2 * D_MODEL, BF16), (D_MODEL, BF16), (D_MODEL, BF16)],
                 [(1, D_MODEL), (1, D_MODEL)], tm=512, rs=16)


GROUP_W = SSM_INNER // SSM_GROUPS


def _gate_norm_group(y, z, nw):
    v = y * jax.nn.silu(z)
    return v * lax.rsqrt(jnp.mean(v * v, axis=-1, keepdims=True) + EPS) * nw


def _gate_norm_fwd(name, y, z, nw):
    def fn(yv, zv, nwv):
        parts = [_gate_norm_group(yv[:, k * GROUP_W:(k + 1) * GROUP_W], zv[:, k * GROUP_W:(k + 1) * GROUP_W],
                                  nwv[:, k * GROUP_W:(k + 1) * GROUP_W]) for k in range(SSM_GROUPS)]
        return (jnp.concatenate(parts, axis=1),), ()
    return _rows(name, fn, [(y, SSM_INNER, 0), (z, SSM_INNER, 0)], [nw], [(SSM_INNER, BF16)], [],
                 tm=512, rs=16)[0]


def _gate_norm_bwd(name, y, z, dout, nw):
    def fn(yv, zv, dv, nwv):
        dys, dzs, dns = [], [], []
        for k in range(SSM_GROUPS):
            sl = slice(k * GROUP_W, (k + 1) * GROUP_W)
            _, vjp = jax.vjp(_gate_norm_group, yv[:, sl], zv[:, sl], nwv[:, sl])
            dy, dz, dn = vjp(dv[:, sl])
            dys.append(dy), dzs.append(dz), dns.append(dn)
        return (jnp.concatenate(dys, axis=1), jnp.concatenate(dzs, axis=1)), (jnp.concatenate(dns, axis=1),)
    return _rows(name, fn, [(y, SSM_INNER, 0), (z, SSM_INNER, 0), (dout, SSM_INNER, 0)], [nw],
                 [(SSM_INNER, F32), (SSM_INNER, BF16)], [(1, SSM_INNER)], tm=512, rs=16)


def _softplus(v):
    return jnp.maximum(v, 0.0) + jnp.log1p(jnp.exp(-jnp.abs(v)))


def _chunk_cumsum(v, reverse=False):
    row = lax.broadcasted_iota(jnp.int32, v.shape, 0)
    step = 1
    while step < CHUNK:
        if reverse:
            shifted = pltpu.roll(v, CHUNK - step, axis=0)
            v = v + jnp.where(row < CHUNK - step, shifted, 0.0)
        else:
            shifted = pltpu.roll(v, step, axis=0)
            v = v + jnp.where(row >= step, shifted, 0.0)
        step *= 2
    return v


def _dt_prep(name, dt_raw, dt_bias, a_log):
    def fn(rv, bv, alv):
        dt = _softplus(rv + bv)
        return (dt, _chunk_cumsum(dt * (-jnp.exp(alv)))), ()
    return _rows(name, fn, [(dt_raw, SSM_HEADS, 0)], [dt_bias, a_log],
                 [(SSM_HEADS, F32), (SSM_HEADS, F32)], [], tm=512, rs=CHUNK)


def _dt_bwd(name, dt_raw, ddt, da1, da2, dt_bias, a_log):
    def fn(rv, ddv, d1, d2, bv, alv):
        pre = rv + bv
        dt = _softplus(pre)
        a_neg = -jnp.exp(alv)
        back = _chunk_cumsum(d1 + d2, reverse=True)
        d_dt = ddv + back * a_neg
        d_raw = d_dt * jax.nn.sigmoid(pre)
        return (d_raw,), (_colsum(d_raw), _colsum(back * dt) * a_neg)
    return _rows(name, fn, [(dt_raw, SSM_HEADS, 0), (ddt, SSM_HEADS, 0), (da1, SSM_HEADS, 0), (da2, SSM_HEADS, 0)],
                 [dt_bias, a_log], [(SSM_HEADS, BF16)], [(1, SSM_HEADS), (1, SSM_HEADS)], tm=512, rs=CHUNK)


def _adamw_math(w, g, m, v):
    m_new = ADAM_B1 * m + (1.0 - ADAM_B1) * g
    v_new = ADAM_B2 * v + (1.0 - ADAM_B2) * jnp.square(g)
    m_hat = m_new / (1.0 - ADAM_B1 ** ADAM_STEP)
    v_hat = v_new / (1.0 - ADAM_B2 ** ADAM_STEP)
    delta = -ADAM_LR * (m_hat / (jnp.sqrt(v_hat) + ADAM_EPS) + ADAM_WD * w)
    return delta, m_new, v_new


def _adamw(name, w, g, m, v, *, tm, rs):
    width = w.shape[1]

    def fn(wv, mv, vv, gv):
        return (gv,) + _adamw_math(wv, gv, mv, vv), ()
    return _rows(name, fn, [(w, width, 0), (m, width, 0), (v, width, 0), (g, width, 0)],
                 [], [(width, F32)] * 4, [], tm=tm, rs=rs)


def _adamw_small(name, ws, gs, ms, vs):
    n = len(ws)

    def body(*refs):
        w_refs, g_refs, m_refs, v_refs = (refs[k * n:(k + 1) * n] for k in range(4))
        outs = refs[4 * n:]
        for i in range(n):
            res = _adamw_math(w_refs[i][...], g_refs[i][...], m_refs[i][...], v_refs[i][...])
            for k in range(3):
                outs[k * n + i][...] = res[k]

    vmem = pl.BlockSpec(memory_space=pltpu.VMEM)
    res = pl.pallas_call(
        body, name=name, in_specs=[vmem] * (4 * n), out_specs=[vmem] * (3 * n),
        out_shape=[jax.ShapeDtypeStruct(w.shape, F32) for w in ws] * 3,
        compiler_params=pltpu.CompilerParams(vmem_limit_bytes=VMEM_LIMIT_V7X),
    )(*ws, *gs, *ms, *vs)
    return res[:n], res[n:2 * n], res[2 * n:]


def _pair_sum(name, a, b, *, tm):
    shape = a.shape
    flat = (shape[0] * shape[1], shape[2])

    def fn(av, bv):
        return (av.astype(F32) + bv.astype(F32),), ()
    out = _rows(name, fn, [(a.reshape(flat), flat[1], 0), (b.reshape(flat), flat[1], 0)], [], [(flat[1], BF16)], [],
                tm=tm, rs=2 * SUBLANES)[0]
    return out.reshape(shape)


def _sum_slots(name, stack, *, tm, rs):
    width = stack.shape[2]

    def fn(*slots):
        s0, s1, s2, s3 = (s.astype(F32) for s in slots)
        return (((s0 + s1) + s2) + s3,), ()
    return _rows(name, fn, [(stack, width, 0, k) for k in range(N_CHIPS)], [], [(width, F32)], [],
                 tm=tm, rs=rs)[0]


def _layernorm(v, w, b):
    mu = jnp.mean(v, axis=-1, keepdims=True)
    var = jnp.mean(jnp.square(v - mu), axis=-1, keepdims=True)
    return (v - mu) * lax.rsqrt(var + EPS) * w + b


def _gmlp_mask():
    t = lax.broadcasted_iota(jnp.int32, (GMLP_BLOCK, GMLP_BLOCK), 0) // CHUNK
    s = lax.broadcasted_iota(jnp.int32, (GMLP_BLOCK, GMLP_BLOCK), 1) // CHUNK
    return s <= t


GMLP_TM = 512


def _gmlp_fwd(name, za, ln_w, ln_b, ws, bs_col):
    nrow = za.shape[0]
    tm = GMLP_TM
    width = GMLP_GROUPS * GMLP_BLOCK

    def body(za_ref, lnw_ref, lnb_ref, ws_ref, bs_ref, o_ref, wm_ref):
        mask = _gmlp_mask()
        for g in range(GMLP_GROUPS):
            wm_ref[g] = jnp.where(mask, ws_ref[g], 0.0).astype(BF16)

        def block(n, carry):
            rows = pl.ds(pl.multiple_of(n * GMLP_BLOCK, GMLP_BLOCK), GMLP_BLOCK)
            for g in range(GMLP_GROUPS):
                cols = slice(g * GMLP_BLOCK, (g + 1) * GMLP_BLOCK)
                vcols = slice(width + g * GMLP_BLOCK, width + (g + 1) * GMLP_BLOCK)
                u = jax.nn.gelu(za_ref[rows, cols].astype(F32))
                v = jax.nn.gelu(za_ref[rows, vcols].astype(F32))
                vn = _layernorm(v, lnw_ref[g:g + 1, :], lnb_ref[g:g + 1, :])
                sv = _dot(wm_ref[g], vn) + bs_ref[g]
                o_ref[rows, cols] = (u * sv).astype(o_ref.dtype)
            return carry

        lax.fori_loop(0, tm // GMLP_BLOCK, block, 0)

    small = lambda a: pl.BlockSpec(a.shape, lambda i, nd=a.ndim: (0,) * nd)
    return pl.pallas_call(
        body, name=name, grid=(nrow // tm,),
        in_specs=[pl.BlockSpec((tm, 2 * width), lambda i: (i, 0)), small(ln_w), small(ln_b), small(ws), small(bs_col)],
        out_specs=pl.BlockSpec((tm, width), lambda i: (i, 0)),
        out_shape=jax.ShapeDtypeStruct((nrow, width), BF16),
        scratch_shapes=[pltpu.VMEM((GMLP_GROUPS, GMLP_BLOCK, GMLP_BLOCK), BF16)],
        compiler_params=_params(("arbitrary",)),
    )(za, ln_w, ln_b, ws, bs_col)


def _gmlp_bwd(name, za, dout, ln_w, ln_b, ws, bs_col):
    nrow = za.shape[0]
    tm = GMLP_TM
    width = GMLP_GROUPS * GMLP_BLOCK

    def body(za_ref, do_ref, lnw_ref, lnb_ref, ws_ref, bs_ref, dza_ref, dlnw_ref, dlnb_ref, dws_ref, dbs_ref, wm_ref):
        mask = _gmlp_mask()
        for g in range(GMLP_GROUPS):
            wm_ref[g] = jnp.where(mask, ws_ref[g], 0.0).astype(BF16)

        @pl.when(pl.program_id(0) == 0)
        def _():
            dlnw_ref[...] = jnp.zeros_like(dlnw_ref)
            dlnb_ref[...] = jnp.zeros_like(dlnb_ref)
            dws_ref[...] = jnp.zeros_like(dws_ref)
            dbs_ref[...] = jnp.zeros_like(dbs_ref)

        def block(n, carry):
            rows = pl.ds(pl.multiple_of(n * GMLP_BLOCK, GMLP_BLOCK), GMLP_BLOCK)
            for g in range(GMLP_GROUPS):
                cols = slice(g * GMLP_BLOCK, (g + 1) * GMLP_BLOCK)
                vcols = slice(width + g * GMLP_BLOCK, width + (g + 1) * GMLP_BLOCK)
                u, gelu_u_vjp = jax.vjp(jax.nn.gelu, za_ref[rows, cols].astype(F32))
                v, gelu_v_vjp = jax.vjp(jax.nn.gelu, za_ref[rows, vcols].astype(F32))
                vn, ln_vjp = jax.vjp(_layernorm, v, lnw_ref[g:g + 1, :], lnb_ref[g:g + 1, :])
                sv = _dot(wm_ref[g], vn) + bs_ref[g]
                d_o = do_ref[rows, cols].astype(F32)
                dsv = d_o * u
                d_wm = _dot(dsv, vn, 1, 1)
                dvn = _dot(wm_ref[g], dsv, 0, 0)
                dv, dlnw, dlnb = ln_vjp(dvn)
                dza_ref[rows, cols] = gelu_u_vjp(d_o * sv)[0].astype(dza_ref.dtype)
                dza_ref[rows, vcols] = gelu_v_vjp(dv)[0].astype(dza_ref.dtype)
                dlnw_ref[g:g + 1, :] += dlnw
                dlnb_ref[g:g + 1, :] += dlnb
                dws_ref[g] += jnp.where(mask, d_wm, 0.0)
                dbs_ref[g] += jnp.sum(dsv, axis=1, keepdims=True)
            return carry

        lax.fori_loop(0, tm // GMLP_BLOCK, block, 0)

    small = lambda a: pl.BlockSpec(a.shape, lambda i, nd=a.ndim: (0,) * nd)
    return pl.pallas_call(
        body, name=name, grid=(nrow // tm,),
        in_specs=[pl.BlockSpec((tm, 2 * width), lambda i: (i, 0)), pl.BlockSpec((tm, width), lambda i: (i, 0)),
                  small(ln_w), small(ln_b), small(ws), small(bs_col)],
        out_specs=[pl.BlockSpec((tm, 2 * width), lambda i: (i, 0)), small(ln_w), small(ln_b), small(ws), small(bs_col)],
        out_shape=[jax.ShapeDtypeStruct((nrow, 2 * width), BF16), jax.ShapeDtypeStruct(ln_w.shape, F32),
                   jax.ShapeDtypeStruct(ln_b.shape, F32), jax.ShapeDtypeStruct(ws.shape, F32),
                   jax.ShapeDtypeStruct(bs_col.shape, F32)],
        scratch_shapes=[pltpu.VMEM((GMLP_GROUPS, GMLP_BLOCK, GMLP_BLOCK), BF16)],
        compiler_params=_params(("arbitrary",)),
    )(za, dout, ln_w, ln_b, ws, bs_col)


CONV_TM = 256
CONV_RS = 32
HALO = 2 * SUBLANES


def _tap_rows(w_ref):
    return [w_ref[k:k + 1, :] for k in range(w_ref.shape[0])]


def _conv_rows(win, w, rs):
    taps = len(w)
    out = w[taps - 1] * win[HALO:, :]
    for k in range(taps - 1):
        back = taps - 1 - k
        out = out + w[k] * pltpu.roll(win, back, axis=0)[HALO:, :]
    return out


def _conv_t_rows(win, w, rs):
    taps = len(w)
    out = w[taps - 1] * win[:rs, :]
    for k in range(taps - 1):
        ahead = taps - 1 - k
        out = out + w[k] * pltpu.roll(win, rs + HALO - ahead, axis=0)[:rs, :]
    return out


def _conv_dw_rows(d, xwin, taps):
    rows = []
    for k in range(taps):
        back = taps - 1 - k
        xs = xwin[HALO:, :] if back == 0 else pltpu.roll(xwin, back, axis=0)[HALO:, :]
        rows.append(jnp.sum(d * xs, axis=0, keepdims=True))
    return rows


def _halo_specs(nrow, tm, tc):
    per = tm // HALO
    last = nrow // HALO - 1
    main = pl.BlockSpec((tm, tc), lambda j, i: (i, j))
    before = pl.BlockSpec((HALO, tc), lambda j, i: (jnp.maximum(i * per - 1, 0), j))
    after = pl.BlockSpec((HALO, tc), lambda j, i: (jnp.minimum((i + 1) * per, last), j))
    return main, before, after


def _col_spec(rows, tc):
    return pl.BlockSpec((rows, tc), lambda j, i: (0, j))


def _conv_fwd(name, x, w, b, *, tc):
    nrow, ncol = x.shape
    taps = w.shape[0]
    tm, rs = CONV_TM, CONV_RS
    main, before, _ = _halo_specs(nrow, tm, tc)

    def body(x_ref, xb_ref, w_ref, b_ref, o_ref):
        first = pl.program_id(1) == 0
        wv, bv = _tap_rows(w_ref), b_ref[...]

        def step(r, prev):
            sl = pl.ds(pl.multiple_of(r * rs, rs), rs)
            cur = x_ref[sl, :].astype(F32)
            o_ref[sl, :] = (_conv_rows(jnp.concatenate([prev, cur], axis=0), wv, rs) + bv).astype(o_ref.dtype)
            return cur[rs - HALO:, :]

        lax.fori_loop(0, tm // rs, step, jnp.where(first, 0.0, xb_ref[...].astype(F32)))

    return pl.pallas_call(
        body, name=name, grid=(ncol // tc, nrow // tm),
        in_specs=[main, before, _col_spec(taps, tc), _col_spec(1, tc)],
        out_specs=main, out_shape=jax.ShapeDtypeStruct((nrow, ncol), BF16),
        compiler_params=_params(("parallel", "arbitrary")),
    )(x, x, w, b)


def _conv_bwd(name, dpre, x, w, *, tc):
    nrow, ncol = x.shape
    taps = w.shape[0]
    tm, rs = CONV_TM, CONV_RS
    nsub = tm // rs
    main, before, after = _halo_specs(nrow, tm, tc)

    def body(d_ref, da_ref, x_ref, xb_ref, w_ref, dx_ref, dw_ref, db_ref):
        i = pl.program_id(1)
        first, last = i == 0, i == pl.num_programs(1) - 1
        wv = _tap_rows(w_ref)
        x_before = jnp.where(first, 0.0, xb_ref[...].astype(F32))

        @pl.when(first)
        def _():
            dw_ref[...] = jnp.zeros_like(dw_ref)
            db_ref[...] = jnp.zeros_like(db_ref)

        def step(q, carry):
            nxt, dw, db = carry
            r = nsub - 1 - q
            sl = pl.ds(pl.multiple_of(r * rs, rs), rs)
            cur = d_ref[sl, :].astype(F32)
            dx_ref[sl, :] = _conv_t_rows(jnp.concatenate([cur, nxt], axis=0), wv, rs).astype(dx_ref.dtype)
            inner = x_ref[pl.ds(pl.multiple_of(jnp.maximum(r * rs - HALO, 0), HALO), HALO), :].astype(F32)
            xwin = jnp.concatenate([jnp.where(r == 0, x_before, inner), x_ref[sl, :].astype(F32)], axis=0)
            dw = tuple(a + s for a, s in zip(dw, _conv_dw_rows(cur, xwin, taps)))
            return cur[:HALO, :], dw, db + _colsum(cur)

        zero_row = jnp.zeros((1, tc), F32)
        init = (jnp.where(last, 0.0, da_ref[...].astype(F32)), (zero_row,) * taps, zero_row)
        _, dw, db = lax.fori_loop(0, nsub, step, init)
        for k in range(taps):
            dw_ref[k:k + 1, :] += dw[k]
        db_ref[...] += db

    return pl.pallas_call(
        body, name=name, grid=(ncol // tc, nrow // tm),
        in_specs=[main, after, main, before, _col_spec(taps, tc)],
        out_specs=[main, _col_spec(taps, tc), _col_spec(1, tc)],
        out_shape=[jax.ShapeDtypeStruct((nrow, ncol), BF16), jax.ShapeDtypeStruct((taps, ncol), F32),
                   jax.ShapeDtypeStruct((1, ncol), F32)],
        compiler_params=_params(("parallel", "arbitrary")),
    )(dpre, dpre, x, x, w)


def _glu(gate, val):
    return jax.nn.silu(gate) * val


def _ffn_act_fwd(name, pg, pv, wg, wv, bg, bv, *, tc):
    nrow, ncol = pg.shape
    taps = wg.shape[0]
    tm, rs = CONV_TM, CONV_RS
    main, before, _ = _halo_specs(nrow, tm, tc)

    def body(pg_ref, pgb_ref, pv_ref, pvb_ref, wg_ref, wv_ref, bg_ref, bv_ref, g_ref, v_ref, a_ref):
        first = pl.program_id(1) == 0
        wgv, wvv, bgv, bvv = _tap_rows(wg_ref), _tap_rows(wv_ref), bg_ref[...], bv_ref[...]

        def step(r, carry):
            prev_g, prev_v = carry
            sl = pl.ds(pl.multiple_of(r * rs, rs), rs)
            cur_g, cur_v = pg_ref[sl, :].astype(F32), pv_ref[sl, :].astype(F32)
            gate = _conv_rows(jnp.concatenate([prev_g, cur_g], axis=0), wgv, rs) + bgv
            val = _conv_rows(jnp.concatenate([prev_v, cur_v], axis=0), wvv, rs) + bvv
            g_ref[sl, :] = gate.astype(g_ref.dtype)
            v_ref[sl, :] = val.astype(v_ref.dtype)
            a_ref[sl, :] = _glu(gate, val).astype(a_ref.dtype)
            return cur_g[rs - HALO:, :], cur_v[rs - HALO:, :]

        lax.fori_loop(0, tm // rs, step, (jnp.where(first, 0.0, pgb_ref[...].astype(F32)),
                                          jnp.where(first, 0.0, pvb_ref[...].astype(F32))))

    return pl.pallas_call(
        body, name=name, grid=(ncol // tc, nrow // tm),
        in_specs=[main, before, main, before, _col_spec(taps, tc), _col_spec(taps, tc), _col_spec(1, tc), _col_spec(1, tc)],
        out_specs=[main, main, main],
        out_shape=[jax.ShapeDtypeStruct((nrow, ncol), BF16)] * 3,
        compiler_params=_params(("parallel", "arbitrary")),
    )(pg, pg, pv, pv, wg, wv, bg, bv)


def _ffn_act_bwd(name, dact, gate, val):
    def fn(dv, gv, vv):
        _, vjp = jax.vjp(_glu, gv, vv)
        dg, dval = vjp(dv)
        return (dg, dval), ()
    width = dact.shape[1]
    return _rows(name, fn, [(dact, width, 0), (gate, width, 0), (val, width, 0)], [],
                 [(width, BF16), (width, BF16)], [], tm=256, rs=2 * SUBLANES)


SSD_TM = 256
SSD_CHUNKS = SSD_TM // CHUNK
X_OFF, B_OFF, C_OFF = 0, SSM_INNER, SSM_INNER + SSM_GROUPS * SSM_STATE
HP = SSM_HPG * SSM_HEAD_DIM


def _causal_tiled():
    row = lax.broadcasted_iota(jnp.int32, (CHUNK, HP), 0)
    src = lax.broadcasted_iota(jnp.int32, (CHUNK, HP), 1) & (CHUNK - 1)
    return src <= row


def _split3(v):
    hi = v.astype(BF16)
    rest = v - hi.astype(F32)
    mid = rest.astype(BF16)
    lo = (rest - mid.astype(F32)).astype(BF16)
    return hi, mid, lo


def _dot_exact(a, ind):
    parts = [lax.dot_general(p, ind, (((1,), (0,)), ((), ())), preferred_element_type=F32) for p in _split3(a)]
    return (parts[0] + parts[1]) + parts[2]


def _head_indicator():
    head = lax.broadcasted_iota(jnp.int32, (SSM_HEADS, SSM_INNER), 0)
    chan = lax.broadcasted_iota(jnp.int32, (SSM_HEADS, SSM_INNER), 1)
    return (chan // SSM_HEAD_DIM == head).astype(BF16)


def _chunk_decays(ci, dt_ref, ac_ref, ind, ax_ref, dtx_ref, eax_ref, eex_ref, tail_ref):
    rows = pl.ds(pl.multiple_of(ci * CHUNK, CHUNK), CHUNK)
    ax_ref[...] = _dot_exact(ac_ref[rows, :], ind)
    dtx_ref[...] = _dot_exact(dt_ref[rows, :], ind)
    eax_ref[...] = jnp.exp(ax_ref[...])
    eex_ref[...] = jnp.exp(ax_ref[CHUNK - 1:CHUNK, :] - ax_ref[...])
    tail = pl.ds(pl.multiple_of(ci * CHUNK + CHUNK - SUBLANES, SUBLANES), SUBLANES)
    tail_ref[...] = jnp.exp(ac_ref[tail, :])


def _group_decay(ci, g, ax_ref, af_ref, xbc_ref, causal):
    gcols = slice(g * HP, (g + 1) * HP)
    bm = xbc_ref[:, B_OFF + g * SSM_STATE:B_OFF + (g + 1) * SSM_STATE]
    cm = xbc_ref[:, C_OFF + g * SSM_STATE:C_OFF + (g + 1) * SSM_STATE]
    cb_tiled = _dot(cm, jnp.concatenate([bm] * SSM_HPG, axis=0), 1, 1)
    seg = ax_ref[:, gcols] - af_ref[ci, :, gcols]
    decay = jnp.where(causal, jnp.exp(jnp.where(causal, seg, 0.0)), 0.0)
    return bm, cm, cb_tiled * decay, decay


def _ssd_fwd(name, pre, dt, a_cum, a_flat, d_x, ind, shards):
    nrow = pre.shape[0]
    tm = SSD_TM
    nstep = nrow // tm
    ng = len(shards)

    def body(pre_ref, dt_ref, ac_ref, af_ref, dx_ref, ind_ref, *rest):
        shard_refs, (y_ref, st_ref), stack_refs = rest[:ng], rest[ng:ng + 2], rest[ng + 2:2 * ng + 2]
        (h_ref, xbc_ref, ax_ref, dtx_ref, eax_ref, eex_ref, m_ref, xd_ref, yd_ref, tail_ref,
         send_sems, recv_sems) = rest[2 * ng + 2:]
        step = pl.program_id(0)
        start, forward, finish = _gather_phases([s.shape[0] for s in shards], ng, shard_refs, stack_refs,
                                                send_sems, recv_sems)

        @pl.when(step == 0)
        def _():
            h_ref[...] = jnp.zeros_like(h_ref)
            start()

        @pl.when(step == nstep // 2)
        def _():
            forward()

        causal = _causal_tiled()
        ind = ind_ref[...]

        def chunk(ci, carry):
            rows = pl.ds(pl.multiple_of(ci * CHUNK, CHUNK), CHUNK)
            xbc_ref[...] = jax.nn.silu(pre_ref[rows, :].astype(F32))
            _chunk_decays(ci, dt_ref, ac_ref, ind, ax_ref, dtx_ref, eax_ref, eex_ref, tail_ref)
            st_ref[ci] = h_ref[...].astype(st_ref.dtype)
            for g in range(SSM_GROUPS):
                gcols = slice(g * HP, (g + 1) * HP)
                bm, cm, m_all, _ = _group_decay(ci, g, ax_ref, af_ref, xbc_ref, causal)
                m_ref[...] = m_all
                x_g = xbc_ref[:, gcols]
                xd = x_g * dtx_ref[:, gcols]
                xd_ref[...] = xd
                h_g = h_ref[gcols, :]
                for hh in range(SSM_HPG):
                    lc = slice(hh * SSM_HEAD_DIM, (hh + 1) * SSM_HEAD_DIM)
                    yd_ref[:, lc] = _dot(m_ref[:, lc], xd_ref[:, lc])
                y_ref[rows, gcols] = (yd_ref[...] + _dot(cm, h_g, 1, 1) * eax_ref[:, gcols]
                                      + dx_ref[:, gcols] * x_g)
                new = _dot(xd * eex_ref[:, gcols], bm, 0, 0)
                for hh in range(SSM_HPG):
                    h = g * SSM_HPG + hh
                    hrows = slice(h * SSM_HEAD_DIM, (h + 1) * SSM_HEAD_DIM)
                    lrows = slice(hh * SSM_HEAD_DIM, (hh + 1) * SSM_HEAD_DIM)
                    h_ref[hrows, :] = tail_ref[SUBLANES - 1:SUBLANES, h:h + 1] * h_ref[hrows, :] + new[lrows, :]
            return carry

        lax.fori_loop(0, SSD_CHUNKS, chunk, 0)

        @pl.when(step == nstep - 1)
        def _():
            finish()

    nchunk = nrow // CHUNK
    whole = lambda a: pl.BlockSpec(a.shape, lambda i, nd=a.ndim: (0,) * nd)
    hbm = pl.BlockSpec(memory_space=pl.ANY)
    wide = lambda: pltpu.VMEM((CHUNK, SSM_INNER), F32)
    group = lambda: pltpu.VMEM((CHUNK, HP), F32)
    res = pl.pallas_call(
        body, name=name, grid=(nstep,),
        in_specs=[pl.BlockSpec((tm, SSM_XBC), lambda i: (i, 0)), pl.BlockSpec((tm, SSM_HEADS), lambda i: (i, 0)),
                  pl.BlockSpec((tm, SSM_HEADS), lambda i: (i, 0)),
                  pl.BlockSpec((SSD_CHUNKS, 1, SSM_INNER), lambda i: (i, 0, 0)), whole(d_x), whole(ind)] + [hbm] * ng,
        out_specs=[pl.BlockSpec((tm, SSM_INNER), lambda i: (i, 0)),
                   pl.BlockSpec((SSD_CHUNKS, SSM_INNER, SSM_STATE), lambda i: (i, 0, 0))] + [hbm] * ng,
        out_shape=[jax.ShapeDtypeStruct((nrow, SSM_INNER), F32),
                   jax.ShapeDtypeStruct((nchunk, SSM_INNER, SSM_STATE), BF16)]
        + [jax.ShapeDtypeStruct((N_CHIPS,) + s.shape, s.dtype) for s in shards],
        scratch_shapes=[pltpu.VMEM((SSM_INNER, SSM_STATE), F32), pltpu.VMEM((CHUNK, SSM_XBC), F32),
                        wide(), wide(), wide(), wide(), group(), group(), group(),
                        pltpu.VMEM((SUBLANES, SSM_HEADS), F32)] + _exchange_scratch(ng, GATHER_SEMS),
        compiler_params=_params(("arbitrary",)),
    )(pre, dt, a_cum, a_flat, d_x, ind, *shards)
    return res[0], res[1], res[2:]


def _ssd_bwd(name, pre, dt, a_cum, a_flat, d_x, ind, ind_t, states, dy, pairs):
    nrow = pre.shape[0]
    tm = SSD_TM
    ntile = nrow // tm
    npair = len(pairs)

    def body(pre_ref, dt_ref, ac_ref, af_ref, dx_ref, ind_ref, indt_ref, st_ref, dy_ref, *rest):
        pair_refs = rest[:npair]
        dpre_ref, ddt_ref, da_ref, daf_ref, dd_ref = rest[npair:npair + 5]
        recv_refs = rest[npair + 5:2 * npair + 5]
        (dh_ref, xbc_ref, dxbc_ref, ax_ref, dtx_ref, eax_ref, eex_ref, ra_ref, ts_ref, r2_ref,
         m_ref, l_ref, xd_ref, dm_ref, dxd_ref, fold_ref, hd_ref, tail_ref, send_sems, recv_sems) = rest[2 * npair + 5:]
        start, finish = _scatter_phases(pair_refs, recv_refs, send_sems, recv_sems)

        @pl.when(pl.program_id(0) == 0)
        def _():
            dh_ref[...] = jnp.zeros_like(dh_ref)
            dd_ref[...] = jnp.zeros_like(dd_ref)
            start()

        causal = _causal_tiled()
        ind, ind_t = ind_ref[...], indt_ref[...]
        is_last_row = lax.broadcasted_iota(jnp.int32, (CHUNK, 1), 0) == CHUNK - 1
        ones = jnp.ones((CHUNK, SSM_STATE), BF16)

        def chunk(k, ddx):
            ci = SSD_CHUNKS - 1 - k
            rows = pl.ds(pl.multiple_of(ci * CHUNK, CHUNK), CHUNK)
            pre_v = pre_ref[rows, :].astype(F32)
            xbc_ref[...] = jax.nn.silu(pre_v)
            _chunk_decays(ci, dt_ref, ac_ref, ind, ax_ref, dtx_ref, eax_ref, eex_ref, tail_ref)
            ddx_parts = []
            for g in range(SSM_GROUPS):
                gcols = slice(g * HP, (g + 1) * HP)
                bcols = slice(B_OFF + g * SSM_STATE, B_OFF + (g + 1) * SSM_STATE)
                ccols = slice(C_OFF + g * SSM_STATE, C_OFF + (g + 1) * SSM_STATE)
                bm, cm, m_all, decay = _group_decay(ci, g, ax_ref, af_ref, xbc_ref, causal)
                m_ref[...] = m_all
                l_ref[...] = decay
                x_g = xbc_ref[:, gcols]
                xd = x_g * dtx_ref[:, gcols]
                xd_ref[...] = xd
                h_g = st_ref[ci, gcols, :]
                dh_g = dh_ref[gcols, :]
                dy_g = dy_ref[rows, gcols]
                for hh in range(SSM_HPG):
                    h = g * SSM_HPG + hh
                    hcols = slice(h * SSM_HEAD_DIM, (h + 1) * SSM_HEAD_DIM)
                    lc = slice(hh * SSM_HEAD_DIM, (hh + 1) * SSM_HEAD_DIM)
                    dy_h = dy_ref[rows, hcols]
                    dm_ref[:, lc] = _dot(dy_h, xd_ref[:, lc], 1, 1)
                    dxd_ref[:, lc] = _dot(m_ref[:, lc], dy_h, 0, 0)
                ebdh = eex_ref[:, gcols] * _dot(bm, dh_g, 1, 1)
                dxd = dxd_ref[...] + ebdh
                dm = dm_ref[...]
                t = dm * l_ref[...]
                t128 = (t[:, 0:LANES] + t[:, LANES:2 * LANES]) + (t[:, 2 * LANES:3 * LANES] + t[:, 3 * LANES:])
                fold_ref[...] = t128 + pltpu.roll(t128, CHUNK, axis=1)
                dw_sum = fold_ref[:, 0:CHUNK]
                q = dm * m_ref[...]
                dyea = dy_g * eax_ref[:, gcols]
                ra_ref[:, gcols] = q + dyea * _dot(cm, h_g, 1, 1)
                ts_ref[:, gcols] = xd * ebdh
                r2_ref[:, gcols] = dxd * x_g
                daf_ref[ci, :, gcols] = -jnp.sum(q, axis=0, keepdims=True)
                ddx_parts.append(jnp.sum(dy_g * x_g, axis=0, keepdims=True))
                dxbc_ref[:, gcols] = dxd * dtx_ref[:, gcols] + dx_ref[:, gcols] * dy_g
                dxbc_ref[:, ccols] = _dot(dw_sum, bm) + _dot(dyea, h_g)
                dxbc_ref[:, bcols] = _dot(dw_sum, cm, 0, 0) + _dot(xd * eex_ref[:, gcols], dh_g)
                dh_new = _dot(dyea, cm, 0, 0)
                for hh in range(SSM_HPG):
                    h = g * SSM_HPG + hh
                    hrows = slice(h * SSM_HEAD_DIM, (h + 1) * SSM_HEAD_DIM)
                    lrows = slice(hh * SSM_HEAD_DIM, (hh + 1) * SSM_HEAD_DIM)
                    hd_ref[h:h + 1, :] = jnp.sum(st_ref[ci, hrows, :] * dh_ref[hrows, :], axis=0, keepdims=True)
                    dh_ref[hrows, :] = tail_ref[SUBLANES - 1:SUBLANES, h:h + 1] * dh_ref[hrows, :] + dh_new[lrows, :]
            ra = _dot_exact(ra_ref[...], ind_t)
            ts = _dot_exact(ts_ref[...], ind_t)
            hdh = sum(lax.dot_general(ones, p, (((1,), (1,)), ((), ())), preferred_element_type=F32)
                      for p in _split3(hd_ref[...]))
            da_last = jnp.sum(ts, axis=0, keepdims=True) + tail_ref[SUBLANES - 1:SUBLANES, :] * hdh
            da_ref[rows, :] = ra - ts + jnp.where(is_last_row, da_last, 0.0)
            ddt_ref[rows, :] = _dot_exact(r2_ref[...], ind_t)
            sig = jax.nn.sigmoid(pre_v)
            dpre_ref[rows, :] = (dxbc_ref[...] * (sig * (1.0 + pre_v * (1.0 - sig)))).astype(dpre_ref.dtype)
            return ddx + jnp.concatenate(ddx_parts, axis=1)

        ddx = lax.fori_loop(0, SSD_CHUNKS, chunk, jnp.zeros((1, SSM_INNER), F32))
        dd_ref[...] += _dot_exact(jnp.broadcast_to(ddx, (SUBLANES, SSM_INNER)), ind_t)

        @pl.when(pl.program_id(0) == ntile - 1)
        def _():
            finish()

    rev = lambda i: ntile - 1 - i
    whole = lambda a: pl.BlockSpec(a.shape, lambda i, nd=a.ndim: (0,) * nd)
    hbm = pl.BlockSpec(memory_space=pl.ANY)
    wide = lambda: pltpu.VMEM((CHUNK, SSM_INNER), F32)
    group = lambda: pltpu.VMEM((CHUNK, HP), F32)
    res = pl.pallas_call(
        body, name=name, grid=(ntile,),
        in_specs=[pl.BlockSpec((tm, SSM_XBC), lambda i: (rev(i), 0)), pl.BlockSpec((tm, SSM_HEADS), lambda i: (rev(i), 0)),
                  pl.BlockSpec((tm, SSM_HEADS), lambda i: (rev(i), 0)),
                  pl.BlockSpec((SSD_CHUNKS, 1, SSM_INNER), lambda i: (rev(i), 0, 0)),
                  whole(d_x), whole(ind), whole(ind_t),
                  pl.BlockSpec((SSD_CHUNKS, SSM_INNER, SSM_STATE), lambda i: (rev(i), 0, 0)),
                  pl.BlockSpec((tm, SSM_INNER), lambda i: (rev(i), 0))] + [hbm] * npair,
        out_specs=[pl.BlockSpec((tm, SSM_XBC), lambda i: (rev(i), 0)), pl.BlockSpec((tm, SSM_HEADS), lambda i: (rev(i), 0)),
                   pl.BlockSpec((tm, SSM_HEADS), lambda i: (rev(i), 0)),
                   pl.BlockSpec((SSD_CHUNKS, 1, SSM_INNER), lambda i: (rev(i), 0, 0)),
                   pl.BlockSpec((SUBLANES, SSM_HEADS), lambda i: (0, 0))] + [hbm] * npair,
        out_shape=[jax.ShapeDtypeStruct((nrow, SSM_XBC), BF16), jax.ShapeDtypeStruct((nrow, SSM_HEADS), F32),
                   jax.ShapeDtypeStruct((nrow, SSM_HEADS), F32), jax.ShapeDtypeStruct((nrow // CHUNK, 1, SSM_INNER), F32),
                   jax.ShapeDtypeStruct((SUBLANES, SSM_HEADS), F32)]
        + [jax.ShapeDtypeStruct(p.shape, p.dtype) for p in pairs],
        scratch_shapes=[pltpu.VMEM((SSM_INNER, SSM_STATE), F32), pltpu.VMEM((CHUNK, SSM_XBC), F32),
                        pltpu.VMEM((CHUNK, SSM_XBC), F32), wide(), wide(), wide(), wide(), wide(), wide(), wide(),
                        group(), group(), group(), group(), group(), pltpu.VMEM((CHUNK, LANES), F32),
                        pltpu.VMEM((SSM_HEADS, SSM_STATE), F32), pltpu.VMEM((SUBLANES, SSM_HEADS), F32)]
        + _exchange_scratch(npair, N_CHIPS - 1),
        compiler_params=_params(("arbitrary",)),
    )(pre, dt, a_cum, a_flat, d_x, ind, ind_t, states, dy, *pairs)
    return res[:5], res[5:]


LATE = ["w_proj_a", "w_proj_b", "w_out", "ffn_w_up", "ffn_w_down"]
HALF_TILES = {"w_in": 128, "w_proj_a": 128, "w_proj_b": 256, "w_out": 128, "ffn_w_up": 128, "ffn_w_down": 176}


def _late_weights(stacks, shards):
    pa, pb, out, up, down = [_own_slot(stack, own) for stack, own in zip(stacks, shards)]
    return {"w_proj_a": pa.reshape(-1, D_MODEL), "w_proj_b": pb.reshape(-1, D_MODEL), "w_out": out.reshape(-1, D_MODEL),
            "w_up_g": _columns_from_chips(up[:2]), "w_up_v": _columns_from_chips(up[2:]),
            "w_down": down.reshape(-1, D_MODEL)}


def _pair_reduce(tag, names, stacks):
    core = lax.axis_index("c")
    own_half = [_row_half(s, core, 1) for s in stacks]
    other_half = _swap_cores("pair_grads_" + tag, [_row_half(s, 1 - core, 1) for s in stacks])
    return [_pair_sum("pair_" + n, a, b, tm=HALF_TILES[n]) for n, a, b in zip(names, own_half, other_half)]


def _local_step(x, target, w, late_shards):
    w = dict(w)
    g = {}
    bs_col = w["gmlp_bs"].reshape(GMLP_GROUPS, GMLP_BLOCK, 1)
    b0, b1 = w["gate_bias"][0:1], w["gate_bias"][1:2]

    xn = _rms_fwd("mix_norm", x, w["mix_norm_w"])
    big = dict(bm=1024, bn=1024, bk=1024)
    act16 = dict(out_dtype=BF16, **big)
    gates = _mm("in_gates", xn, w["w_g"], **act16)
    za = _mm("in_gmlp", xn, w["w_za"], **act16)
    z = _mm("in_z", xn, w["w_z"], **act16)
    xbc = _mm("in_xbc", xn, w["w_xbc"], **act16)
    dt_raw = _mm("in_dt", xn, w["w_dt"], bm=1024, bn=SSM_HEADS, bk=1024)

    pre = _conv_fwd("ssm_conv_fwd", xbc, w["ssm_conv_w"], w["ssm_conv_b"], tc=1024)
    dt, a_cum = _dt_prep("dt_prep", dt_raw, w["ssm_dt_bias"], w["ssm_a_log"])
    a_flat = jnp.transpose(a_cum.reshape(-1, CHUNK, SSM_HEADS), (0, 2, 1)).reshape(-1, 1, SSM_INNER)
    d_x = jnp.repeat(w["ssm_d"], SSM_HEAD_DIM, axis=1)
    ind = _head_indicator()
    y_ssd, states, late_stacks = _ssd_fwd("ssd_fwd", pre, dt, a_cum, a_flat, d_x, ind, late_shards)
    w.update(_late_weights(late_stacks, late_shards))
    yb_pre = _gate_norm_fwd("gate_norm_fwd", y_ssd, z, w["ssm_norm_w"])
    y_b = _mm("proj_b", yb_pre, w["w_proj_b"], **act16)

    ya_pre = _gmlp_fwd("gmlp_fwd", za, w["gmlp_ln_w"], w["gmlp_ln_b"], w["gmlp_ws"], bs_col)
    y_a = _mm("proj_a", ya_pre, w["w_proj_a"], **act16)

    merged = _merge_fwd("merge_fwd", gates, y_a, y_b, b0, b1)
    h1 = _mm("out_proj", merged, w["w_out"], res=x, **big)

    hn = _rms_fwd("ffn_norm", h1, w["ffn_norm_w"])
    half = dict(bm=1024, bn=D_FF // 2, bk=1024, out_dtype=BF16)
    pg = _mm("ffn_up_gate", hn, w["w_up_g"], **half)
    pv = _mm("ffn_up_val", hn, w["w_up_v"], **half)
    cw, cb = w["ffn_conv_w"], w["ffn_conv_b"]
    gate, val, act = _ffn_act_fwd("ffn_act_fwd", pg, pv, cw[:, :D_FF], cw[:, D_FF:], cb[:, :D_FF], cb[:, D_FF:],
                                  tc=D_FF // 2)
    h2 = _mm("ffn_down", act, w["w_down"], res=h1, bm=512, bn=1024, bk=D_FF // 2)

    dh2, loss_part, g["final_norm_w"] = _final_loss("final_loss", h2, target, w["final_norm_w"].reshape(1, D_MODEL))

    dact = _mm("d_act", dh2, w["w_down"], tb=True, **half)
    wgrad = dict(ta=True, bk=512, out_dtype=BF16)
    g["w_down"] = _mm("dw_down", act, dh2, bm=D_FF // 2, bn=1024, **wgrad)
    dgate, dval = _ffn_act_bwd("ffn_act_bwd", dact, gate, val)
    dpg, dcwg, dcbg = _conv_bwd("ffn_conv_bwd_gate", dgate, pg, cw[:, :D_FF], tc=D_FF // 2)
    dpv, dcwv, dcbv = _conv_bwd("ffn_conv_bwd_val", dval, pv, cw[:, D_FF:], tc=D_FF // 2)
    g["ffn_conv_w"] = jnp.concatenate([dcwg, dcwv], axis=1)
    g["ffn_conv_b"] = jnp.concatenate([dcbg, dcbv], axis=1)
    back = dict(bm=1024, bn=1024, bk=D_FF // 2)
    dhn = _mm("d_hn_gate", dpg, w["w_up_g"], tb=True, **back)
    dhn = _mm("d_hn_val", dpv, w["w_up_v"], tb=True, res=dhn, **back)
    g["w_up_g"] = _mm("dw_up_gate", hn, dpg, bm=1024, bn=D_FF // 2, **wgrad)
    g["w_up_v"] = _mm("dw_up_val", hn, dpv, bm=1024, bn=D_FF // 2, **wgrad)
    dh1, g["ffn_norm_w"] = _rms_bwd("ffn_norm_bwd", h1, w["ffn_norm_w"], dhn, dh2)

    dmerged = _mm("d_merged", dh1, w["w_out"], tb=True, **act16)
    g["w_out"] = _mm("dw_out", merged, dh1, bm=1024, bn=1024, **wgrad)
    dgates, dya, dyb, db0, db1 = _merge_bwd("merge_bwd", gates, y_a, y_b, dmerged, b0, b1)
    g["gate_bias"] = jnp.concatenate([db0, db1], axis=0)

    dya_pre = _mm("d_ya_pre", dya, w["w_proj_a"], tb=True, **act16)
    g["w_proj_a"] = _mm("dw_proj_a", ya_pre, dya, bm=1024, bn=1024, **wgrad)
    dyb_pre = _mm("d_yb_pre", dyb, w["w_proj_b"], tb=True, **act16)
    g["w_proj_b"] = _mm("dw_proj_b", yb_pre, dyb, bm=1024, bn=1024, **wgrad)
    late_pairs = _pair_reduce("late", LATE, [
        g["w_proj_a"].reshape(N_CHIPS, -1, D_MODEL), g["w_proj_b"].reshape(N_CHIPS, -1, D_MODEL),
        g["w_out"].reshape(N_CHIPS, -1, D_MODEL),
        jnp.concatenate([_columns_to_chips(g["w_up_g"], 2), _columns_to_chips(g["w_up_v"], 2)], axis=0),
        g["w_down"].reshape(N_CHIPS, -1, D_MODEL)])

    dy_ssd, dz, g["ssm_norm_w"] = _gate_norm_bwd("gate_norm_bwd", y_ssd, z, dyb_pre, w["ssm_norm_w"])
    (dpre, ddt, da_tok, da_flat, dd), late_received = _ssd_bwd(
        "ssd_bwd", pre, dt, a_cum, a_flat, d_x, ind, ind.T, states, dy_ssd, late_pairs)
    g["ssm_d"] = dd[0:1]
    da_src = jnp.transpose(da_flat.reshape(-1, SSM_HEADS, CHUNK), (0, 2, 1)).reshape(-1, SSM_HEADS)
    ddt_raw, g["ssm_dt_bias"], g["ssm_a_log"] = _dt_bwd("dt_bwd", dt_raw, ddt, da_tok, da_src,
                                                         w["ssm_dt_bias"], w["ssm_a_log"])
    dxbc, g["ssm_conv_w"], g["ssm_conv_b"] = _conv_bwd("ssm_conv_bwd", dpre, xbc, w["ssm_conv_w"], tc=1024)

    dza, g["gmlp_ln_w"], g["gmlp_ln_b"], g["gmlp_ws"], dbs = _gmlp_bwd(
        "gmlp_bwd", za, dya_pre, w["gmlp_ln_w"], w["gmlp_ln_b"], w["gmlp_ws"], bs_col)
    g["gmlp_bs"] = dbs.reshape(GMLP_GROUPS, GMLP_BLOCK)

    dxn = _mm("d_xn_gates", dgates, w["w_g"], tb=True, **big)
    dxn = _mm("d_xn_gmlp", dza, w["w_za"], tb=True, res=dxn, **big)
    dxn = _mm("d_xn_z", dz, w["w_z"], tb=True, res=dxn, **big)
    dxn = _mm("d_xn_xbc", dxbc, w["w_xbc"], tb=True, res=dxn, **big)
    dxn = _mm("d_xn_dt", ddt_raw, w["w_dt"], tb=True, res=dxn, bm=1024, bn=1024, bk=SSM_HEADS)
    g["w_g"] = _mm("dw_gates", xn, dgates, bm=1024, bn=1024, **wgrad)
    g["w_za"] = _mm("dw_gmlp", xn, dza, bm=1024, bn=1024, **wgrad)
    g["w_z"] = _mm("dw_z", xn, dz, bm=1024, bn=1024, **wgrad)
    g["w_xbc"] = _mm("dw_xbc", xn, dxbc, bm=1024, bn=1024, **wgrad)
    g["w_dt"] = _mm("dw_dt", xn, ddt_raw, bm=1024, bn=SSM_HEADS, **wgrad)
    grad_x, g["mix_norm_w"] = _rms_bwd("mix_norm_bwd", x, w["mix_norm_w"], dxn, dh1)
    return loss_part, grad_x, g, late_pairs, late_received


def _position():
    return lax.axis_index("x"), lax.axis_index("y"), lax.axis_index("c")


def _own_slot(stack, own):
    chip = 2 * lax.axis_index("x") + lax.axis_index("y")
    return lax.dynamic_update_index_in_dim(stack, own, chip, axis=0)


def _scatter_phases(ins, outs, send_sems, recv_sems):
    n = len(ins)
    x, y, c = _position()
    me = 2 * x + y
    peers = [(1 - x, y), (x, 1 - y), (1 - x, 1 - y)]

    def copy(i, k, src_slot, dst_slot):
        px, py = peers[k]
        return pltpu.make_async_remote_copy(
            src_ref=ins[i].at[src_slot], dst_ref=outs[i].at[dst_slot],
            send_sem=send_sems.at[i, k], recv_sem=recv_sems.at[i, k],
            device_id=(px, py, c), device_id_type=MESH)

    def start():
        for i in range(n):
            for k, (px, py) in enumerate(peers):
                copy(i, k, 2 * px + py, me).start()

    def finish():
        for i in range(n):
            for k, (px, py) in enumerate(peers):
                copy(i, k, me, 2 * px + py).wait_recv()
        for i in range(n):
            for k, (px, py) in enumerate(peers):
                copy(i, k, 2 * px + py, me).wait_send()

    return start, finish


def _exchange_scratch(n, per_array):
    return [pltpu.SemaphoreType.DMA((n, per_array)), pltpu.SemaphoreType.DMA((n, per_array))]


def _scatter_chips(name, arrs):
    n = len(arrs)

    def body(*refs):
        start, finish = _scatter_phases(refs[:n], refs[n:2 * n], *refs[2 * n:])
        start()
        finish()

    hbm = pl.BlockSpec(memory_space=pl.ANY)
    return pl.pallas_call(
        body, name=name,
        in_specs=[hbm] * n, out_specs=[hbm] * n,
        out_shape=[jax.ShapeDtypeStruct(a.shape, a.dtype) for a in arrs],
        scratch_shapes=_exchange_scratch(n, N_CHIPS - 1),
        compiler_params=pltpu.CompilerParams(has_side_effects=True),
    )(*arrs)


def _half_rows(ref_rows, which):
    half = ref_rows // 2
    return pl.ds(pl.multiple_of(which * half, 2 * SUBLANES), half)


GATHER_SEMS = 2 * (N_CHIPS - 1)


def _gather_phases(nrows, ns, ins, outs, send_sems, recv_sems):
    n = len(ins)
    x, y, c = _position()
    me = 2 * x + y
    sibling = (x, y, 1 - c)
    chips = [(1 - x, y), (x, 1 - y), (1 - x, 1 - y)]

    def remote(i, k, src, dst, to):
        return pltpu.make_async_remote_copy(src_ref=src, dst_ref=dst, send_sem=send_sems.at[i, k],
                                            recv_sem=recv_sems.at[i, k], device_id=to, device_id_type=MESH)

    def over_ici(i, k):
        px, py = chips[k]
        rows = _half_rows(nrows[i], c) if i < ns else slice(None)
        return remote(i, k, ins[i].at[rows], outs[i].at[me, rows], (px, py, c))

    def landed(i, k, which):
        px, py = chips[k]
        return outs[i].at[2 * px + py, _half_rows(nrows[i], which)] if i < ns else outs[i].at[2 * px + py]

    def start():
        for i in range(n):
            for k in range(N_CHIPS - 1):
                over_ici(i, k).start()

    def forward():
        for i in range(n):
            for k in range(N_CHIPS - 1):
                piece = landed(i, k, c)
                remote(i, k, piece, piece, (*chips[k], c)).wait_recv()
                if i < ns:
                    remote(i, N_CHIPS - 1 + k, piece, piece, sibling).start()

    def finish():
        for i in range(ns):
            for k in range(N_CHIPS - 1):
                piece = landed(i, k, 1 - c)
                remote(i, N_CHIPS - 1 + k, piece, piece, sibling).wait_recv()
        for i in range(n):
            for k in range(N_CHIPS - 1):
                over_ici(i, k).wait_send()
                if i < ns:
                    piece = landed(i, k, c)
                    remote(i, N_CHIPS - 1 + k, piece, piece, sibling).wait_send()

    return start, forward, finish


def _gather_chips_split(name, split, whole):
    arrs = list(split) + list(whole)
    n = len(arrs)

    def body(*refs):
        phases = _gather_phases([a.shape[0] for a in arrs], len(split), refs[:n], refs[n:2 * n], *refs[2 * n:])
        for phase in phases:
            phase()

    hbm = pl.BlockSpec(memory_space=pl.ANY)
    return pl.pallas_call(
        body, name=name, in_specs=[hbm] * n, out_specs=[hbm] * n,
        out_shape=[jax.ShapeDtypeStruct((N_CHIPS,) + a.shape, a.dtype) for a in arrs],
        scratch_shapes=_exchange_scratch(n, GATHER_SEMS),
        compiler_params=pltpu.CompilerParams(has_side_effects=True),
    )(*arrs)


def _swap_cores(name, arrs):
    n = len(arrs)

    def body(*refs):
        ins, outs = refs[:n], refs[n:2 * n]
        send_sems, recv_sems = refs[2 * n:]
        x, y, c = _position()
        copies = [pltpu.make_async_remote_copy(src_ref=ins[i], dst_ref=outs[i], send_sem=send_sems.at[i],
                                               recv_sem=recv_sems.at[i], device_id=(x, y, 1 - c), device_id_type=MESH)
                  for i in range(n)]
        for cp in copies:
            cp.start()
        for cp in copies:
            cp.wait_recv()
        for cp in copies:
            cp.wait_send()

    hbm = pl.BlockSpec(memory_space=pl.ANY)
    return pl.pallas_call(
        body, name=name, in_specs=[hbm] * n, out_specs=[hbm] * n,
        out_shape=[jax.ShapeDtypeStruct(a.shape, a.dtype) for a in arrs],
        scratch_shapes=[pltpu.SemaphoreType.DMA((n,)), pltpu.SemaphoreType.DMA((n,))],
        compiler_params=pltpu.CompilerParams(has_side_effects=True),
    )(*arrs)


def _row_half(a, which, axis):
    half = a.shape[axis] // 2
    return lax.dynamic_slice_in_dim(a, which * half, half, axis=axis)


def _all_reduce(name, pack):
    def body(in_ref, out_ref, buf, send_sems, recv_sems):
        x, y, c = _position()
        me = 4 * x + 2 * y + c
        flips = [(dx, dy, dc) for dx in (0, 1) for dy in (0, 1) for dc in (0, 1) if (dx, dy, dc) != (0, 0, 0)]
        peers = [((1 - x) if dx else x, (1 - y) if dy else y, (1 - c) if dc else c) for dx, dy, dc in flips]
        buf[me] = in_ref[...]
        sends = []
        for k, peer in enumerate(peers):
            cp = pltpu.make_async_remote_copy(src_ref=in_ref, dst_ref=buf.at[me], send_sem=send_sems.at[k],
                                              recv_sem=recv_sems.at[k], device_id=peer, device_id_type=MESH)
            cp.start()
            sends.append(cp)
        for k, (px, py, pc) in enumerate(peers):
            pltpu.make_async_remote_copy(src_ref=in_ref, dst_ref=buf.at[4 * px + 2 * py + pc], send_sem=send_sems.at[k],
                                         recv_sem=recv_sems.at[k], device_id=(px, py, pc), device_id_type=MESH).wait_recv()
        total = buf[0]
        for j in range(1, N_DEV):
            total = total + buf[j]
        out_ref[...] = total
        for cp in sends:
            cp.wait_send()

    vmem = pl.BlockSpec(memory_space=pltpu.VMEM)
    return pl.pallas_call(
        body, name=name, in_specs=[vmem], out_specs=vmem,
        out_shape=jax.ShapeDtypeStruct(pack.shape, F32),
        scratch_shapes=[pltpu.VMEM((N_DEV,) + pack.shape, F32), pltpu.SemaphoreType.DMA((N_DEV - 1,)),
                        pltpu.SemaphoreType.DMA((N_DEV - 1,))],
        compiler_params=pltpu.CompilerParams(has_side_effects=True, vmem_limit_bytes=VMEM_LIMIT_V7X),
    )(pack)


def _pack(arrs):
    rows = [a.reshape(-1, LANES) for a in arrs]
    total = sum(r.shape[0] for r in rows)
    rows.append(jnp.zeros((-total % SUBLANES, LANES), F32))
    return jnp.concatenate(rows, axis=0)


def _unpack(pack, shapes):
    out, off = [], 0
    for s in shapes:
        nrow = 1
        for d in s:
            nrow *= d
        nrow //= LANES
        out.append(pack[off:off + nrow].reshape(s))
        off += nrow
    return out


SMALL = ["mix_norm_w", "gate_bias", "gmlp_ln_w", "gmlp_ln_b", "gmlp_ws", "gmlp_bs", "ssm_conv_w", "ssm_conv_b",
         "ssm_dt_bias", "ssm_a_log", "ssm_d", "ssm_norm_w", "ffn_norm_w", "ffn_conv_w", "ffn_conv_b", "final_norm_w"]
SMALL_SHARDED = ("gate_bias", "ssm_conv_w", "ffn_conv_w")
BIG = ["w_in", "w_proj_a", "w_proj_b", "w_out", "ffn_w_up", "ffn_w_down"]
WEIGHTS = ["mix_norm_w", "w_in", "gate_bias", "gmlp_ln_w", "gmlp_ln_b", "gmlp_ws", "gmlp_bs", "ssm_conv_w",
           "ssm_conv_b", "ssm_dt_bias", "ssm_a_log", "ssm_d", "ssm_norm_w", "w_proj_a", "w_proj_b", "w_out",
           "ffn_norm_w", "ffn_w_up", "ffn_conv_w", "ffn_conv_b", "ffn_w_down", "final_norm_w"]
IN_SPLITS = [0, 2048, 4096, 6144, 9216, 9248]


def _columns_from_chips(stack):
    return jnp.transpose(stack, (1, 0, 2)).reshape(stack.shape[1], -1)


def _columns_to_chips(full, parts=N_CHIPS):
    rows, cols = full.shape
    return jnp.transpose(full.reshape(rows, parts, cols // parts), (1, 0, 2))


def kernel(x, mix_norm_w, w_in, gate_bias, gmlp_ln_w, gmlp_ln_b, gmlp_ws, gmlp_bs, ssm_conv_w, ssm_conv_b, ssm_dt_bias, ssm_a_log, ssm_d, ssm_norm_w, w_proj_a, w_proj_b, w_out, ffn_norm_w, ffn_w_up, ffn_conv_w, ffn_conv_b, ffn_w_down, final_norm_w, loss_target, m_mix_norm_w, m_w_in, m_gate_bias, m_gmlp_ln_w, m_gmlp_ln_b, m_gmlp_ws, m_gmlp_bs, m_ssm_conv_w, m_ssm_conv_b, m_ssm_dt_bias, m_ssm_a_log, m_ssm_d, m_ssm_norm_w, m_w_proj_a, m_w_proj_b, m_w_out, m_ffn_norm_w, m_ffn_w_up, m_ffn_conv_w, m_ffn_conv_b, m_ffn_w_down, m_final_norm_w, v_mix_norm_w, v_w_in, v_gate_bias, v_gmlp_ln_w, v_gmlp_ln_b, v_gmlp_ws, v_gmlp_bs, v_ssm_conv_w, v_ssm_conv_b, v_ssm_dt_bias, v_ssm_a_log, v_ssm_d, v_ssm_norm_w, v_w_proj_a, v_w_proj_b, v_w_out, v_ffn_norm_w, v_ffn_w_up, v_ffn_conv_w, v_ffn_conv_b, v_ffn_w_down, v_final_norm_w):
    args = dict(locals())
    weights = {n: args[n] for n in WEIGHTS}
    moments_m = {n: args["m_" + n] for n in WEIGHTS}
    moments_v = {n: args["v_" + n] for n in WEIGHTS}
    chip = 2 * lax.axis_index("x") + lax.axis_index("y")

    shards = [weights["w_in"][0].astype(BF16)] + [weights[n][0] for n in SMALL_SHARDED]
    gathered = _gather_chips_split("gather_weights", shards[:1], shards[1:])
    w_in_s, gb_s, scw_s, fcw_s = [_own_slot(stack, own) for stack, own in zip(gathered, shards)]
    late_shards = [weights[n][0].astype(BF16) for n in LATE]
    w_in_full = _columns_from_chips(w_in_s)
    full = {"w_" + nm: w_in_full[:, IN_SPLITS[k]:IN_SPLITS[k + 1]] for k, nm in enumerate(["g", "za", "z", "xbc", "dt"])}
    full["gate_bias"] = _columns_from_chips(gb_s)
    full["ssm_conv_w"] = _columns_from_chips(scw_s)
    full["ffn_conv_w"] = _columns_from_chips(fcw_s)
    for n in SMALL:
        if n not in SMALL_SHARDED:
            full[n] = weights[n] if n == "final_norm_w" else weights[n][0]
    for n in ("mix_norm_w", "ffn_norm_w", "ssm_conv_b", "ssm_dt_bias", "ssm_a_log", "ssm_d", "ssm_norm_w", "ffn_conv_b"):
        full[n] = full[n].reshape(1, -1)

    loss_part, grad_x, g, late_pairs, late_received = _local_step(x[0], loss_target[0], full, late_shards)

    per_head = ["ssm_dt_bias", "ssm_a_log", "ssm_d"]
    rest = [n for n in SMALL if n not in per_head]
    head_row = jnp.concatenate([g[n] for n in per_head] + [jnp.zeros((1, LANES - 3 * SSM_HEADS), F32)], axis=1)
    pack = _pack([loss_part, head_row] + [g[n] for n in rest])
    reduced = _unpack(_all_reduce("reduce_small", pack), [(1, LANES), (1, LANES)] + [g[n].shape for n in rest])
    loss = reduced[0][0, 0]
    small_grads = {n: reduced[1][:, k * SSM_HEADS:(k + 1) * SSM_HEADS] for k, n in enumerate(per_head)}
    for n, r in zip(rest, reduced[2:]):
        if n in SMALL_SHARDED:
            width = weights[n].shape[2]
            r = lax.dynamic_slice_in_dim(r, chip * width, width, axis=1)
        small_grads[n] = r
    two_d = lambda a: a.reshape(-1, a.shape[-1])
    upd = _adamw_small("adamw_small", *[[two_d(d[n]) for n in SMALL]
                                        for d in (weights, small_grads, moments_m, moments_v)])
    small_out = [[small_grads[n] for n in SMALL]] + list(upd)
    small_out = [[a.reshape(weights[n].shape) for n, a in zip(SMALL, kind)] for kind in small_out]

    dw_in = jnp.concatenate([g["w_g"], g["w_za"], g["w_z"], g["w_xbc"], g["w_dt"]], axis=1)
    in_pairs = _pair_reduce("in", ["w_in"], [_columns_to_chips(dw_in)])
    pair = in_pairs + list(late_pairs)
    received = list(_scatter_chips("scatter_grads", in_pairs)) + list(late_received)
    received = [_own_slot(r, lax.dynamic_index_in_dim(p, chip, 0, keepdims=False)) for r, p in zip(received, pair)]
    halves = [_sum_slots("sum_" + n, r, tm=HALF_TILES[n], rs=2 * SUBLANES) for n, r in zip(BIG, received)]
    tiles = {"w_in": 128, "w_proj_a": 256, "w_proj_b": 256, "w_out": 256, "ffn_w_up": 128, "ffn_w_down": 176}
    core = lax.axis_index("c")
    other = _swap_cores("join_grads", halves)
    reduced = [jnp.where(core == 0, jnp.concatenate([a, b], axis=0), jnp.concatenate([b, a], axis=0))
               for a, b in zip(halves, other)]
    big_out = {}
    for n, grad in zip(BIG, reduced):
        big_out[n] = _adamw("adamw_" + n, weights[n][0], grad, moments_m[n][0], moments_v[n][0],
                            tm=tiles[n], rs=SUBLANES)

    per_kind = [[], [], [], []]
    for n in WEIGHTS:
        for kind in range(4):
            if n in big_out:
                per_kind[kind].append(big_out[n][kind].reshape(weights[n].shape))
            else:
                per_kind[kind].append(small_out[kind][SMALL.index(n)])
    return (loss, grad_x[None], *per_kind[0], *per_kind[1], *per_kind[2], *per_kind[3])
```

```python
import jax
import jax.numpy as jnp
from jax import lax
from jax.experimental import pallas as pl
from jax.experimental.pallas import tpu as pltpu

F32 = jnp.float32
BF16 = jnp.bfloat16
MESH = pl.DeviceIdType.MESH

EPS = 1e-5
D_MODEL = 1024
GMLP_BLOCK = 128
GMLP_GROUPS = 8
CHUNK = 64
SSM_INNER = 2048
SSM_HEADS = 32
SSM_HEAD_DIM = 64
SSM_GROUPS = 4
SSM_HPG = 8
SSM_STATE = 128
SSM_CONV = 4
SSM_XBC = 3072
D_FF = 2816
FFN_CONV = 3
N_CHIPS = 4
N_DEV = 8

ADAM_LR = 0.001
ADAM_B1 = 0.9
ADAM_B2 = 0.999
ADAM_EPS = 1e-08
ADAM_WD = 0.01
ADAM_STEP = 10

VMEM_LIMIT_V7X = 56 * 1024 * 1024
SUBLANES = 8
LANES = 128


def _params(sem=None):
    return pltpu.CompilerParams(dimension_semantics=sem, vmem_limit_bytes=VMEM_LIMIT_V7X)


def _dot(a, b, ca=1, cb=0):
    return lax.dot_general(a.astype(BF16), b.astype(BF16), (((ca,), (cb,)), ((), ())),
                           preferred_element_type=F32)


def _mm(name, a, b, *, ta=False, tb=False, out_dtype=F32, bm, bn, bk, res=None):
    m, k = (a.shape[1], a.shape[0]) if ta else a.shape
    k2, n = (b.shape[1], b.shape[0]) if tb else b.shape
    assert k == k2 and m % bm == 0 and n % bn == 0 and k % bk == 0, (name, a.shape, b.shape)
    nk = k // bk
    a_spec = (pl.BlockSpec((bk, bm), lambda i, j, kk: (kk, i)) if ta
              else pl.BlockSpec((bm, bk), lambda i, j, kk: (i, kk)))
    b_spec = (pl.BlockSpec((bn, bk), lambda i, j, kk: (j, kk)) if tb
              else pl.BlockSpec((bk, bn), lambda i, j, kk: (kk, j)))
    o_spec = pl.BlockSpec((bm, bn), lambda i, j, kk: (i, j))
    has_res = res is not None

    def body(*refs):
        a_ref, b_ref = refs[0], refs[1]
        r_ref = refs[2] if has_res else None
        o_ref = refs[3] if has_res else refs[2]
        p = _dot(a_ref[...], b_ref[...], 0 if ta else 1, 1 if tb else 0)

        def finish(total):
            if has_res:
                total = total + r_ref[...]
            o_ref[...] = total.astype(out_dtype)

        if nk == 1:
            finish(p)
        else:
            acc_ref = refs[-1]
            kk = pl.program_id(2)

            @pl.when(kk == 0)
            def _():
                acc_ref[...] = p

            @pl.when(kk > 0)
            def _():
                acc_ref[...] += p

            @pl.when(kk == nk - 1)
            def _():
                finish(acc_ref[...])

    return pl.pallas_call(
        body, name=name,
        grid=(m // bm, n // bn, nk),
        in_specs=[a_spec, b_spec] + ([o_spec] if has_res else []),
        out_specs=o_spec,
        out_shape=jax.ShapeDtypeStruct((m, n), out_dtype),
        scratch_shapes=[pltpu.VMEM((bm, bn), F32)] if nk > 1 else [],
        compiler_params=_params(("parallel", "parallel", "arbitrary")),
    )(*([a, b] + ([res] if has_res else [])))


def _rows(name, fn, ins, params, outs, accs, *, tm, rs, unroll=4):
    nrow = ins[0][0].shape[-2]
    while tm % (rs * unroll):
        unroll //= 2
    assert nrow % tm == 0 and tm % rs == 0, (name, nrow, tm, rs)
    n_in, n_p, n_out, n_acc = len(ins), len(params), len(outs), len(accs)
    in_specs = []
    for spec in ins:
        arr, width, cb = spec[:3]
        if len(spec) == 4:
            in_specs.append(pl.BlockSpec((None, tm, width), lambda i, cb=cb, lead=spec[3]: (lead, i, cb)))
        else:
            in_specs.append(pl.BlockSpec((tm, width), lambda i, cb=cb: (i, cb)))
    for p in params:
        in_specs.append(pl.BlockSpec(p.shape, lambda i, nd=p.ndim: (0,) * nd))
    out_specs = [pl.BlockSpec((tm, w), lambda i: (i, 0)) for w, _ in outs]
    out_specs += [pl.BlockSpec(s, lambda i: (0, 0)) for s in accs]
    out_shape = [jax.ShapeDtypeStruct((nrow, w), dt) for w, dt in outs]
    out_shape += [jax.ShapeDtypeStruct(s, F32) for s in accs]

    def body(*refs):
        in_refs = refs[:n_in]
        p_refs = refs[n_in:n_in + n_p]
        o_refs = refs[n_in + n_p:n_in + n_p + n_out]
        a_refs = refs[n_in + n_p + n_out:]
        pv = [p[...] for p in p_refs]

        if n_acc:
            @pl.when(pl.program_id(0) == 0)
            def _():
                for a_ref in a_refs:
                    a_ref[...] = jnp.zeros_like(a_ref)

        def step(r, carry):
            for u in range(unroll):
                sl = pl.ds(pl.multiple_of((r * unroll + u) * rs, rs), rs)
                vals = [ref[sl, :].astype(F32) for ref in in_refs]
                row_out, sums = fn(*vals, *pv)
                for o_ref, v in zip(o_refs, row_out):
                    o_ref[sl, :] = v.astype(o_ref.dtype)
                carry = tuple(c + s for c, s in zip(carry, sums))
            return carry

        init = tuple(jnp.zeros(s, F32) for s in accs)
        total = lax.fori_loop(0, tm // (rs * unroll), step, init)
        for a_ref, t in zip(a_refs, total):
            a_ref[...] += t

    res = pl.pallas_call(
        body, name=name, grid=(nrow // tm,),
        in_specs=in_specs, out_specs=out_specs, out_shape=out_shape,
        compiler_params=_params(("arbitrary",)),
    )(*([s[0] for s in ins] + list(params)))
    return res


def _rms(x, w):
    return x * lax.rsqrt(jnp.mean(x * x, axis=-1, keepdims=True) + EPS) * w


def _colsum(v):
    return jnp.sum(v, axis=0, keepdims=True)


def _rms_fwd(name, x, w):
    def fn(xv, wv):
        return (_rms(xv, wv),), ()
    return _rows(name, fn, [(x, D_MODEL, 0)], [w], [(D_MODEL, BF16)], [], tm=512, rs=16)[0]


def _rms_bwd(name, x, w, dy, dres):
    def fn(xv, dyv, drv, wv):
        _, vjp = jax.vjp(_rms, xv, wv)
        dx, dw = vjp(dyv)
        return (drv + dx,), (dw,)
    return _rows(name, fn, [(x, D_MODEL, 0), (dy, D_MODEL, 0), (dres, D_MODEL, 0)], [w],
                 [(D_MODEL, F32)], [(1, D_MODEL)], tm=512, rs=16)


def _final_loss(name, h, target, w):
    def fn(hv, tv, wv):
        y, vjp = jax.vjp(_rms, hv, wv)
        err = y - tv
        part = 0.5 * jnp.sum(jnp.mean(err * err, axis=-1, keepdims=True), axis=0, keepdims=True)
        dh, dw = vjp(err / D_MODEL)
        return (dh,), (jnp.broadcast_to(part, (1, LANES)), dw)
    return _rows(name, fn, [(h, D_MODEL, 0), (target, D_MODEL, 0)], [w],
                 [(D_MODEL, F32)], [(1, LANES), (1, D_MODEL)], tm=512, rs=16)


def _merge(ga, gb, ya, yb, b0, b1):
    return jax.nn.sigmoid(ga + b0) * ya + jax.nn.sigmoid(gb + b1) * yb


def _merge_fwd(name, g, ya, yb, b0, b1):
    def fn(ga, gb, yav, ybv, b0v, b1v):
        return (_merge(ga, gb, yav, ybv, b0v, b1v),), ()
    return _rows(name, fn, [(g, D_MODEL, 0), (g, D_MODEL, 1), (ya, D_MODEL, 0), (yb, D_MODEL, 0)],
                 [b0, b1], [(D_MODEL, BF16)], [], tm=512, rs=16)[0]


def _merge_bwd(name, g, ya, yb, dm, b0, b1):
    def fn(ga, gb, yav, ybv, dmv, b0v, b1v):
        _, vjp = jax.vjp(_merge, ga, gb, yav, ybv, b0v, b1v)
        dga, dgb, dya, dyb, db0, db1 = vjp(dmv)
        return (jnp.concatenate([dga, dgb], axis=1), dya, dyb), (db0, db1)
    return _rows(name, fn,
                 [(g, D_MODEL, 0), (g, D_MODEL, 1), (ya, D_MODEL, 0), (yb, D_MODEL, 0), (dm, D_MODEL, 0)],
                 [b0, b1], [(2 * D_MODEL, BF16), (D_MODEL, BF16), (D_MODEL, BF16)],
                 [(1, D_MODEL), (1, D_MODEL)], tm=512, rs=16)


GROUP_W = SSM_INNER // SSM_GROUPS


def _gate_norm_group(y, z, nw):
    v = y * jax.nn.silu(z)
    return v * lax.rsqrt(jnp.mean(v * v, axis=-1, keepdims=True) + EPS) * nw


def _gate_norm_fwd(name, y, z, nw):
    def fn(yv, zv, nwv):
        parts = [_gate_norm_group(yv[:, k * GROUP_W:(k + 1) * GROUP_W], zv[:, k * GROUP_W:(k + 1) * GROUP_W],
                                  nwv[:, k * GROUP_W:(k + 1) * GROUP_W]) for k in range(SSM_GROUPS)]
        return (jnp.concatenate(parts, axis=1),), ()
    return _rows(name, fn, [(y, SSM_INNER, 0), (z, SSM_INNER, 0)], [nw], [(SSM_INNER, BF16)], [],
                 tm=512, rs=16)[0]


def _gate_norm_bwd(name, y, z, dout, nw):
    def fn(yv, zv, dv, nwv):
        dys, dzs, dns = [], [], []
        for k in range(SSM_GROUPS):
            sl = slice(k * GROUP_W, (k + 1) * GROUP_W)
            _, vjp = jax.vjp(_gate_norm_group, yv[:, sl], zv[:, sl], nwv[:, sl])
            dy, dz, dn = vjp(dv[:, sl])
            dys.append(dy), dzs.append(dz), dns.append(dn)
        return (jnp.concatenate(dys, axis=1), jnp.concatenate(dzs, axis=1)), (jnp.concatenate(dns, axis=1),)
    return _rows(name, fn, [(y, SSM_INNER, 0), (z, SSM_INNER, 0), (dout, SSM_INNER, 0)], [nw],
                 [(SSM_INNER, F32), (SSM_INNER, BF16)], [(1, SSM_INNER)], tm=512, rs=16)


def _softplus(v):
    return jnp.maximum(v, 0.0) + jnp.log1p(jnp.exp(-jnp.abs(v)))


def _chunk_cumsum(v, reverse=False):
    row = lax.broadcasted_iota(jnp.int32, v.shape, 0)
    step = 1
    while step < CHUNK:
        if reverse:
            shifted = pltpu.roll(v, CHUNK - step, axis=0)
            v = v + jnp.where(row < CHUNK - step, shifted, 0.0)
        else:
            shifted = pltpu.roll(v, step, axis=0)
            v = v + jnp.where(row >= step, shifted, 0.0)
        step *= 2
    return v


def _dt_prep(name, dt_raw, dt_bias, a_log):
    def fn(rv, bv, alv):
        dt = _softplus(rv + bv)
        return (dt, _chunk_cumsum(dt * (-jnp.exp(alv)))), ()
    return _rows(name, fn, [(dt_raw, SSM_HEADS, 0)], [dt_bias, a_log],
                 [(SSM_HEADS, F32), (SSM_HEADS, F32)], [], tm=512, rs=CHUNK)


def _dt_bwd(name, dt_raw, ddt, da1, da2, dt_bias, a_log):
    def fn(rv, ddv, d1, d2, bv, alv):
        pre = rv + bv
        dt = _softplus(pre)
        a_neg = -jnp.exp(alv)
        back = _chunk_cumsum(d1 + d2, reverse=True)
        d_dt = ddv + back * a_neg
        d_raw = d_dt * jax.nn.sigmoid(pre)
        return (d_raw,), (_colsum(d_raw), _colsum(back * dt) * a_neg)
    return _rows(name, fn, [(dt_raw, SSM_HEADS, 0), (ddt, SSM_HEADS, 0), (da1, SSM_HEADS, 0), (da2, SSM_HEADS, 0)],
                 [dt_bias, a_log], [(SSM_HEADS, BF16)], [(1, SSM_HEADS), (1, SSM_HEADS)], tm=512, rs=CHUNK)


def _adamw_math(w, g, m, v):
    m_new = ADAM_B1 * m + (1.0 - ADAM_B1) * g
    v_new = ADAM_B2 * v + (1.0 - ADAM_B2) * jnp.square(g)
    m_hat = m_new / (1.0 - ADAM_B1 ** ADAM_STEP)
    v_hat = v_new / (1.0 - ADAM_B2 ** ADAM_STEP)
    delta = -ADAM_LR * (m_hat / (jnp.sqrt(v_hat) + ADAM_EPS) + ADAM_WD * w)
    return delta, m_new, v_new


def _adamw(name, w, g, m, v, *, tm, rs):
    width = w.shape[1]

    def fn(wv, mv, vv, gv):
        return (gv,) + _adamw_math(wv, gv, mv, vv), ()
    return _rows(name, fn, [(w, width, 0), (m, width, 0), (v, width, 0), (g, width, 0)],
                 [], [(width, F32)] * 4, [], tm=tm, rs=rs)


def _adamw_small(name, ws, gs, ms, vs):
    n = len(ws)

    def body(*refs):
        w_refs, g_refs, m_refs, v_refs = (refs[k * n:(k + 1) * n] for k in range(4))
        outs = refs[4 * n:]
        for i in range(n):
            res = _adamw_math(w_refs[i][...], g_refs[i][...], m_refs[i][...], v_refs[i][...])
            for k in range(3):
                outs[k * n + i][...] = res[k]

    vmem = pl.BlockSpec(memory_space=pltpu.VMEM)
    res = pl.pallas_call(
        body, name=name, in_specs=[vmem] * (4 * n), out_specs=[vmem] * (3 * n),
        out_shape=[jax.ShapeDtypeStruct(w.shape, F32) for w in ws] * 3,
        compiler_params=pltpu.CompilerParams(vmem_limit_bytes=VMEM_LIMIT_V7X),
    )(*ws, *gs, *ms, *vs)
    return res[:n], res[n:2 * n], res[2 * n:]


def _pair_sum(name, a, b, *, tm):
    shape = a.shape
    flat = (shape[0] * shape[1], shape[2])

    def fn(av, bv):
        return (av.astype(F32) + bv.astype(F32),), ()
    out = _rows(name, fn, [(a.reshape(flat), flat[1], 0), (b.reshape(flat), flat[1], 0)], [], [(flat[1], BF16)], [],
                tm=tm, rs=2 * SUBLANES)[0]
    return out.reshape(shape)


def _sum_slots(name, stack, *, tm, rs):
    width = stack.shape[2]

    def fn(*slots):
        s0, s1, s2, s3 = (s.astype(F32) for s in slots)
        return (((s0 + s1) + s2) + s3,), ()
    return _rows(name, fn, [(stack, width, 0, k) for k in range(N_CHIPS)], [], [(width, F32)], [],
                 tm=tm, rs=rs)[0]


def _layernorm(v, w, b):
    mu = jnp.mean(v, axis=-1, keepdims=True)
    var = jnp.mean(jnp.square(v - mu), axis=-1, keepdims=True)
    return (v - mu) * lax.rsqrt(var + EPS) * w + b


def _gmlp_mask():
    t = lax.broadcasted_iota(jnp.int32, (GMLP_BLOCK, GMLP_BLOCK), 0) // CHUNK
    s = lax.broadcasted_iota(jnp.int32, (GMLP_BLOCK, GMLP_BLOCK), 1) // CHUNK
    return s <= t


GMLP_TM = 512


def _gmlp_fwd(name, za, ln_w, ln_b, ws, bs_col):
    nrow = za.shape[0]
    tm = GMLP_TM
    width = GMLP_GROUPS * GMLP_BLOCK

    def body(za_ref, lnw_ref, lnb_ref, ws_ref, bs_ref, o_ref, wm_ref):
        mask = _gmlp_mask()
        for g in range(GMLP_GROUPS):
            wm_ref[g] = jnp.where(mask, ws_ref[g], 0.0).astype(BF16)

        def block(n, carry):
            rows = pl.ds(pl.multiple_of(n * GMLP_BLOCK, GMLP_BLOCK), GMLP_BLOCK)
            for g in range(GMLP_GROUPS):
                cols = slice(g * GMLP_BLOCK, (g + 1) * GMLP_BLOCK)
                vcols = slice(width + g * GMLP_BLOCK, width + (g + 1) * GMLP_BLOCK)
                u = jax.nn.gelu(za_ref[rows, cols].astype(F32))
                v = jax.nn.gelu(za_ref[rows, vcols].astype(F32))
                vn = _layernorm(v, lnw_ref[g:g + 1, :], lnb_ref[g:g + 1, :])
                sv = _dot(wm_ref[g], vn) + bs_ref[g]
                o_ref[rows, cols] = (u * sv).astype(o_ref.dtype)
            return carry

        lax.fori_loop(0, tm // GMLP_BLOCK, block, 0)

    small = lambda a: pl.BlockSpec(a.shape, lambda i, nd=a.ndim: (0,) * nd)
    return pl.pallas_call(
        body, name=name, grid=(nrow // tm,),
        in_specs=[pl.BlockSpec((tm, 2 * width), lambda i: (i, 0)), small(ln_w), small(ln_b), small(ws), small(bs_col)],
        out_specs=pl.BlockSpec((tm, width), lambda i: (i, 0)),
        out_shape=jax.ShapeDtypeStruct((nrow, width), BF16),
        scratch_shapes=[pltpu.VMEM((GMLP_GROUPS, GMLP_BLOCK, GMLP_BLOCK), BF16)],
        compiler_params=_params(("arbitrary",)),
    )(za, ln_w, ln_b, ws, bs_col)


def _gmlp_bwd(name, za, dout, ln_w, ln_b, ws, bs_col):
    nrow = za.shape[0]
    tm = GMLP_TM
    width = GMLP_GROUPS * GMLP_BLOCK

    def body(za_ref, do_ref, lnw_ref, lnb_ref, ws_ref, bs_ref, dza_ref, dlnw_ref, dlnb_ref, dws_ref, dbs_ref, wm_ref):
        mask = _gmlp_mask()
        for g in range(GMLP_GROUPS):
            wm_ref[g] = jnp.where(mask, ws_ref[g], 0.0).astype(BF16)

        @pl.when(pl.program_id(0) == 0)
        def _():
            dlnw_ref[...] = jnp.zeros_like(dlnw_ref)
            dlnb_ref[...] = jnp.zeros_like(dlnb_ref)
            dws_ref[...] = jnp.zeros_like(dws_ref)
            dbs_ref[...] = jnp.zeros_like(dbs_ref)

        def block(n, carry):
            rows = pl.ds(pl.multiple_of(n * GMLP_BLOCK, GMLP_BLOCK), GMLP_BLOCK)
            for g in range(GMLP_GROUPS):
                cols = slice(g * GMLP_BLOCK, (g + 1) * GMLP_BLOCK)
                vcols = slice(width + g * GMLP_BLOCK, width + (g + 1) * GMLP_BLOCK)
                u, gelu_u_vjp = jax.vjp(jax.nn.gelu, za_ref[rows, cols].astype(F32))
                v, gelu_v_vjp = jax.vjp(jax.nn.gelu, za_ref[rows, vcols].astype(F32))
                vn, ln_vjp = jax.vjp(_layernorm, v, lnw_ref[g:g + 1, :], lnb_ref[g:g + 1, :])
                sv = _dot(wm_ref[g], vn) + bs_ref[g]
                d_o = do_ref[rows, cols].astype(F32)
                dsv = d_o * u
                d_wm = _dot(dsv, vn, 1, 1)
                dvn = _dot(wm_ref[g], dsv, 0, 0)
                dv, dlnw, dlnb = ln_vjp(dvn)
                dza_ref[rows, cols] = gelu_u_vjp(d_o * sv)[0].astype(dza_ref.dtype)
                dza_ref[rows, vcols] = gelu_v_vjp(dv)[0].astype(dza_ref.dtype)
                dlnw_ref[g:g + 1, :] += dlnw
                dlnb_ref[g:g + 1, :] += dlnb
                dws_ref[g] += jnp.where(mask, d_wm, 0.0)
                dbs_ref[g] += jnp.sum(dsv, axis=1, keepdims=True)
            return carry

        lax.fori_loop(0, tm // GMLP_BLOCK, block, 0)

    small = lambda a: pl.BlockSpec(a.shape, lambda i, nd=a.ndim: (0,) * nd)
    return pl.pallas_call(
        body, name=name, grid=(nrow // tm,),
        in_specs=[pl.BlockSpec((tm, 2 * width), lambda i: (i, 0)), pl.BlockSpec((tm, width), lambda i: (i, 0)),
                  small(ln_w), small(ln_b), small(ws), small(bs_col)],
        out_specs=[pl.BlockSpec((tm, 2 * width), lambda i: (i, 0)), small(ln_w), small(ln_b), small(ws), small(bs_col)],
        out_shape=[jax.ShapeDtypeStruct((nrow, 2 * width), BF16), jax.ShapeDtypeStruct(ln_w.shape, F32),
                   jax.ShapeDtypeStruct(ln_b.shape, F32), jax.ShapeDtypeStruct(ws.shape, F32),
                   jax.ShapeDtypeStruct(bs_col.shape, F32)],
        scratch_shapes=[pltpu.VMEM((GMLP_GROUPS, GMLP_BLOCK, GMLP_BLOCK), BF16)],
        compiler_params=_params(("arbitrary",)),
    )(za, dout, ln_w, ln_b, ws, bs_col)


CONV_TM = 256
CONV_RS = 32
HALO = 2 * SUBLANES


def _tap_rows(w_ref):
    return [w_ref[k:k + 1, :] for k in range(w_ref.shape[0])]


def _conv_rows(win, w, rs):
    taps = len(w)
    out = w[taps - 1] * win[HALO:, :]
    for k in range(taps - 1):
        back = taps - 1 - k
        out = out + w[k] * pltpu.roll(win, back, axis=0)[HALO:, :]
    return out


def _conv_t_rows(win, w, rs):
    taps = len(w)
    out = w[taps - 1] * win[:rs, :]
    for k in range(taps - 1):
        ahead = taps - 1 - k
        out = out + w[k] * pltpu.roll(win, rs + HALO - ahead, axis=0)[:rs, :]
    return out


def _conv_dw_rows(d, xwin, taps):
    rows = []
    for k in range(taps):
        back = taps - 1 - k
        xs = xwin[HALO:, :] if back == 0 else pltpu.roll(xwin, back, axis=0)[HALO:, :]
        rows.append(jnp.sum(d * xs, axis=0, keepdims=True))
    return rows


def _halo_specs(nrow, tm, tc):
    per = tm // HALO
    last = nrow // HALO - 1
    main = pl.BlockSpec((tm, tc), lambda j, i: (i, j))
    before = pl.BlockSpec((HALO, tc), lambda j, i: (jnp.maximum(i * per - 1, 0), j))
    after = pl.BlockSpec((HALO, tc), lambda j, i: (jnp.minimum((i + 1) * per, last), j))
    return main, before, after


def _col_spec(rows, tc):
    return pl.BlockSpec((rows, tc), lambda j, i: (0, j))


def _conv_fwd(name, x, w, b, *, tc):
    nrow, ncol = x.shape
    taps = w.shape[0]
    tm, rs = CONV_TM, CONV_RS
    main, before, _ = _halo_specs(nrow, tm, tc)

    def body(x_ref, xb_ref, w_ref, b_ref, o_ref):
        first = pl.program_id(1) == 0
        wv, bv = _tap_rows(w_ref), b_ref[...]

        def step(r, prev):
            sl = pl.ds(pl.multiple_of(r * rs, rs), rs)
            cur = x_ref[sl, :].astype(F32)
            o_ref[sl, :] = (_conv_rows(jnp.concatenate([prev, cur], axis=0), wv, rs) + bv).astype(o_ref.dtype)
            return cur[rs - HALO:, :]

        lax.fori_loop(0, tm // rs, step, jnp.where(first, 0.0, xb_ref[...].astype(F32)))

    return pl.pallas_call(
        body, name=name, grid=(ncol // tc, nrow // tm),
        in_specs=[main, before, _col_spec(taps, tc), _col_spec(1, tc)],
        out_specs=main, out_shape=jax.ShapeDtypeStruct((nrow, ncol), BF16),
        compiler_params=_params(("parallel", "arbitrary")),
    )(x, x, w, b)


def _conv_bwd(name, dpre, x, w, *, tc):
    nrow, ncol = x.shape
    taps = w.shape[0]
    tm, rs = CONV_TM, CONV_RS
    nsub = tm // rs
    main, before, after = _halo_specs(nrow, tm, tc)

    def body(d_ref, da_ref, x_ref, xb_ref, w_ref, dx_ref, dw_ref, db_ref):
        i = pl.program_id(1)
        first, last = i == 0, i == pl.num_programs(1) - 1
        wv = _tap_rows(w_ref)
        x_before = jnp.where(first, 0.0, xb_ref[...].astype(F32))

        @pl.when(first)
        def _():
            dw_ref[...] = jnp.zeros_like(dw_ref)
            db_ref[...] = jnp.zeros_like(db_ref)

        def step(q, carry):
            nxt, dw, db = carry
            r = nsub - 1 - q
            sl = pl.ds(pl.multiple_of(r * rs, rs), rs)
            cur = d_ref[sl, :].astype(F32)
            dx_ref[sl, :] = _conv_t_rows(jnp.concatenate([cur, nxt], axis=0), wv, rs).astype(dx_ref.dtype)
            inner = x_ref[pl.ds(pl.multiple_of(jnp.maximum(r * rs - HALO, 0), HALO), HALO), :].astype(F32)
            xwin = jnp.concatenate([jnp.where(r == 0, x_before, inner), x_ref[sl, :].astype(F32)], axis=0)
            dw = tuple(a + s for a, s in zip(dw, _conv_dw_rows(cur, xwin, taps)))
            return cur[:HALO, :], dw, db + _colsum(cur)

        zero_row = jnp.zeros((1, tc), F32)
        init = (jnp.where(last, 0.0, da_ref[...].astype(F32)), (zero_row,) * taps, zero_row)
        _, dw, db = lax.fori_loop(0, nsub, step, init)
        for k in range(taps):
            dw_ref[k:k + 1, :] += dw[k]
        db_ref[...] += db

    return pl.pallas_call(
        body, name=name, grid=(ncol // tc, nrow // tm),
        in_specs=[main, after, main, before, _col_spec(taps, tc)],
        out_specs=[main, _col_spec(taps, tc), _col_spec(1, tc)],
        out_shape=[jax.ShapeDtypeStruct((nrow, ncol), BF16), jax.ShapeDtypeStruct((taps, ncol), F32),
                   jax.ShapeDtypeStruct((1, ncol), F32)],
        compiler_params=_params(("parallel", "arbitrary")),
    )(dpre, dpre, x, x, w)


def _glu(gate, val):
    return jax.nn.silu(gate) * val


def _ffn_act_fwd(name, pg, pv, wg, wv, bg, bv, *, tc):
    nrow, ncol = pg.shape
    taps = wg.shape[0]
    tm, rs = CONV_TM, CONV_RS
    main, before, _ = _halo_specs(nrow, tm, tc)

    def body(pg_ref, pgb_ref, pv_ref, pvb_ref, wg_ref, wv_ref, bg_ref, bv_ref, g_ref, v_ref, a_ref):
        first = pl.program_id(1) == 0
        wgv, wvv, bgv, bvv = _tap_rows(wg_ref), _tap_rows(wv_ref), bg_ref[...], bv_ref[...]

        def step(r, carry):
            prev_g, prev_v = carry
            sl = pl.ds(pl.multiple_of(r * rs, rs), rs)
            cur_g, cur_v = pg_ref[sl, :].astype(F32), pv_ref[sl, :].astype(F32)
            gate = _conv_rows(jnp.concatenate([prev_g, cur_g], axis=0), wgv, rs) + bgv
            val = _conv_rows(jnp.concatenate([prev_v, cur_v], axis=0), wvv, rs) + bvv
            g_ref[sl, :] = gate.astype(g_ref.dtype)
            v_ref[sl, :] = val.astype(v_ref.dtype)
            a_ref[sl, :] = _glu(gate, val).astype(a_ref.dtype)
            return cur_g[rs - HALO:, :], cur_v[rs - HALO:, :]

        lax.fori_loop(0, tm // rs, step, (jnp.where(first, 0.0, pgb_ref[...].astype(F32)),
                                          jnp.where(first, 0.0, pvb_ref[...].astype(F32))))

    return pl.pallas_call(
        body, name=name, grid=(ncol // tc, nrow // tm),
        in_specs=[main, before, main, before, _col_spec(taps, tc), _col_spec(taps, tc), _col_spec(1, tc), _col_spec(1, tc)],
        out_specs=[main, main, main],
        out_shape=[jax.ShapeDtypeStruct((nrow, ncol), BF16)] * 3,
        compiler_params=_params(("parallel", "arbitrary")),
    )(pg, pg, pv, pv, wg, wv, bg, bv)


def _ffn_act_bwd(name, dact, gate, val):
    def fn(dv, gv, vv):
        _, vjp = jax.vjp(_glu, gv, vv)
        dg, dval = vjp(dv)
        return (dg, dval), ()
    width = dact.shape[1]
    return _rows(name, fn, [(dact, width, 0), (gate, width, 0), (val, width, 0)], [],
                 [(width, BF16), (width, BF16)], [], tm=256, rs=2 * SUBLANES)


SSD_TM = 256
SSD_CHUNKS = SSD_TM // CHUNK
X_OFF, B_OFF, C_OFF = 0, SSM_INNER, SSM_INNER + SSM_GROUPS * SSM_STATE
HP = SSM_HPG * SSM_HEAD_DIM


def _causal_tiled():
    row = lax.broadcasted_iota(jnp.int32, (CHUNK, HP), 0)
    src = lax.broadcasted_iota(jnp.int32, (CHUNK, HP), 1) & (CHUNK - 1)
    return src <= row


def _split2(v):
    hi = v.astype(BF16)
    return hi, (v - hi.astype(F32)).astype(BF16)


def _dot_exact(a, ind):
    hi, lo = (lax.dot_general(p, ind, (((1,), (0,)), ((), ())), preferred_element_type=F32) for p in _split2(a))
    return hi + lo


def _head_indicator():
    head = lax.broadcasted_iota(jnp.int32, (SSM_HEADS, SSM_INNER), 0)
    chan = lax.broadcasted_iota(jnp.int32, (SSM_HEADS, SSM_INNER), 1)
    return (chan // SSM_HEAD_DIM == head).astype(BF16)


def _chunk_decays(ci, dt_ref, ac_ref, ind, ax_ref, dtx_ref, eax_ref, eex_ref, tail_ref):
    rows = pl.ds(pl.multiple_of(ci * CHUNK, CHUNK), CHUNK)
    ax_ref[...] = _dot_exact(ac_ref[rows, :], ind)
    dtx_ref[...] = _dot_exact(dt_ref[rows, :], ind)
    eax_ref[...] = jnp.exp(ax_ref[...])
    eex_ref[...] = jnp.exp(ax_ref[CHUNK - 1:CHUNK, :] - ax_ref[...])
    tail = pl.ds(pl.multiple_of(ci * CHUNK + CHUNK - SUBLANES, SUBLANES), SUBLANES)
    tail_ref[...] = jnp.exp(ac_ref[tail, :])


def _group_decay(ci, g, ax_ref, af_ref, xbc_ref, causal):
    gcols = slice(g * HP, (g + 1) * HP)
    bm = xbc_ref[:, B_OFF + g * SSM_STATE:B_OFF + (g + 1) * SSM_STATE]
    cm = xbc_ref[:, C_OFF + g * SSM_STATE:C_OFF + (g + 1) * SSM_STATE]
    cb_tiled = _dot(cm, jnp.concatenate([bm] * SSM_HPG, axis=0), 1, 1)
    seg = ax_ref[:, gcols] - af_ref[ci, :, gcols]
    decay = jnp.where(causal, jnp.exp(jnp.where(causal, seg, 0.0)), 0.0)
    return bm, cm, cb_tiled * decay, decay


def _ssd_fwd(name, pre, dt, a_cum, a_flat, d_x, ind, shards):
    nrow = pre.shape[0]
    tm = SSD_TM
    nstep = nrow // tm
    ng = len(shards)

    def body(pre_ref, dt_ref, ac_ref, af_ref, dx_ref, ind_ref, *rest):
        shard_refs, (y_ref, st_ref), stack_refs = rest[:ng], rest[ng:ng + 2], rest[ng + 2:2 * ng + 2]
        (h_ref, xbc_ref, ax_ref, dtx_ref, eax_ref, eex_ref, m_ref, xd_ref, yd_ref, tail_ref,
         send_sems, recv_sems) = rest[2 * ng + 2:]
        step = pl.program_id(0)
        start, forward, finish = _gather_phases([s.shape[0] for s in shards], ng, shard_refs, stack_refs,
                                                send_sems, recv_sems)

        @pl.when(step == 0)
        def _():
            h_ref[...] = jnp.zeros_like(h_ref)
            start()

        @pl.when(step == nstep // 2)
        def _():
            forward()

        causal = _causal_tiled()
        ind = ind_ref[...]

        def chunk(ci, carry):
            rows = pl.ds(pl.multiple_of(ci * CHUNK, CHUNK), CHUNK)
            xbc_ref[...] = jax.nn.silu(pre_ref[rows, :].astype(F32))
            _chunk_decays(ci, dt_ref, ac_ref, ind, ax_ref, dtx_ref, eax_ref, eex_ref, tail_ref)
            st_ref[ci] = h_ref[...].astype(st_ref.dtype)
            for g in range(SSM_GROUPS):
                gcols = slice(g * HP, (g + 1) * HP)
                bm, cm, m_all, _ = _group_decay(ci, g, ax_ref, af_ref, xbc_ref, causal)
                m_ref[...] = m_all
                x_g = xbc_ref[:, gcols]
                xd = x_g * dtx_ref[:, gcols]
                xd_ref[...] = xd
                h_g = h_ref[gcols, :]
                for hh in range(SSM_HPG):
                    lc = slice(hh * SSM_HEAD_DIM, (hh + 1) * SSM_HEAD_DIM)
                    yd_ref[:, lc] = _dot(m_ref[:, lc], xd_ref[:, lc])
                y_ref[rows, gcols] = (yd_ref[...] + _dot(cm, h_g, 1, 1) * eax_ref[:, gcols]
                                      + dx_ref[:, gcols] * x_g)
                new = _dot(xd * eex_ref[:, gcols], bm, 0, 0)
                for hh in range(SSM_HPG):
                    h = g * SSM_HPG + hh
                    hrows = slice(h * SSM_HEAD_DIM, (h + 1) * SSM_HEAD_DIM)
                    lrows = slice(hh * SSM_HEAD_DIM, (hh + 1) * SSM_HEAD_DIM)
                    h_ref[hrows, :] = tail_ref[SUBLANES - 1:SUBLANES, h:h + 1] * h_ref[hrows, :] + new[lrows, :]
            return carry

        lax.fori_loop(0, SSD_CHUNKS, chunk, 0)

        @pl.when(step == nstep - 1)
        def _():
            finish()

    nchunk = nrow // CHUNK
    whole = lambda a: pl.BlockSpec(a.shape, lambda i, nd=a.ndim: (0,) * nd)
    hbm = pl.BlockSpec(memory_space=pl.ANY)
    wide = lambda: pltpu.VMEM((CHUNK, SSM_INNER), F32)
    group = lambda: pltpu.VMEM((CHUNK, HP), F32)
    res = pl.pallas_call(
        body, name=name, grid=(nstep,),
        in_specs=[pl.BlockSpec((tm, SSM_XBC), lambda i: (i, 0)), pl.BlockSpec((tm, SSM_HEADS), lambda i: (i, 0)),
                  pl.BlockSpec((tm, SSM_HEADS), lambda i: (i, 0)),
                  pl.BlockSpec((SSD_CHUNKS, 1, SSM_INNER), lambda i: (i, 0, 0)), whole(d_x), whole(ind)] + [hbm] * ng,
        out_specs=[pl.BlockSpec((tm, SSM_INNER), lambda i: (i, 0)),
                   pl.BlockSpec((SSD_CHUNKS, SSM_INNER, SSM_STATE), lambda i: (i, 0, 0))] + [hbm] * ng,
        out_shape=[jax.ShapeDtypeStruct((nrow, SSM_INNER), F32),
                   jax.ShapeDtypeStruct((nchunk, SSM_INNER, SSM_STATE), BF16)]
        + [jax.ShapeDtypeStruct((N_CHIPS,) + s.shape, s.dtype) for s in shards],
        scratch_shapes=[pltpu.VMEM((SSM_INNER, SSM_STATE), F32), pltpu.VMEM((CHUNK, SSM_XBC), F32),
                        wide(), wide(), wide(), wide(), group(), group(), group(),
                        pltpu.VMEM((SUBLANES, SSM_HEADS), F32)] + _exchange_scratch(ng, GATHER_SEMS),
        compiler_params=_params(("arbitrary",)),
    )(pre, dt, a_cum, a_flat, d_x, ind, *shards)
    return res[0], res[1], res[2:]


def _ssd_bwd(name, pre, dt, a_cum, a_flat, d_x, ind, ind_t, states, dy, pairs):
    nrow = pre.shape[0]
    tm = SSD_TM
    ntile = nrow // tm
    npair = len(pairs)

    def body(pre_ref, dt_ref, ac_ref, af_ref, dx_ref, ind_ref, indt_ref, st_ref, dy_ref, *rest):
        pair_refs = rest[:npair]
        dpre_ref, ddt_ref, da_ref, daf_ref, dd_ref = rest[npair:npair + 5]
        recv_refs = rest[npair + 5:2 * npair + 5]
        (dh_ref, xbc_ref, dxbc_ref, ax_ref, dtx_ref, eax_ref, eex_ref, red_ref,
         m_ref, l_ref, xd_ref, dm_ref, dxd_ref, fold_ref, hd_ref, tail_ref, send_sems, recv_sems) = rest[2 * npair + 5:]
        start, finish = _scatter_phases(pair_refs, recv_refs, send_sems, recv_sems)

        @pl.when(pl.program_id(0) == 0)
        def _():
            dh_ref[...] = jnp.zeros_like(dh_ref)
            dd_ref[...] = jnp.zeros_like(dd_ref)
            start()

        causal = _causal_tiled()
        ind, ind_t = ind_ref[...], indt_ref[...]
        is_last_row = lax.broadcasted_iota(jnp.int32, (CHUNK, 1), 0) == CHUNK - 1
        ones = jnp.ones((CHUNK, SSM_STATE), BF16)

        def chunk(k, ddx):
            ci = SSD_CHUNKS - 1 - k
            rows = pl.ds(pl.multiple_of(ci * CHUNK, CHUNK), CHUNK)
            pre_v = pre_ref[rows, :].astype(F32)
            xbc_ref[...] = jax.nn.silu(pre_v)
            _chunk_decays(ci, dt_ref, ac_ref, ind, ax_ref, dtx_ref, eax_ref, eex_ref, tail_ref)
            ddx_parts = []
            for g in range(SSM_GROUPS):
                gcols = slice(g * HP, (g + 1) * HP)
                bcols = slice(B_OFF + g * SSM_STATE, B_OFF + (g + 1) * SSM_STATE)
                ccols = slice(C_OFF + g * SSM_STATE, C_OFF + (g + 1) * SSM_STATE)
                bm, cm, m_all, decay = _group_decay(ci, g, ax_ref, af_ref, xbc_ref, causal)
                m_ref[...] = m_all
                l_ref[...] = decay
                x_g = xbc_ref[:, gcols]
                xd = x_g * dtx_ref[:, gcols]
                xd_ref[...] = xd
                h_g = st_ref[ci, gcols, :]
                dh_g = dh_ref[gcols, :]
                dy_g = dy_ref[rows, gcols]
                for hh in range(SSM_HPG):
                    h = g * SSM_HPG + hh
                    hcols = slice(h * SSM_HEAD_DIM, (h + 1) * SSM_HEAD_DIM)
                    lc = slice(hh * SSM_HEAD_DIM, (hh + 1) * SSM_HEAD_DIM)
                    dy_h = dy_ref[rows, hcols]
                    dm_ref[:, lc] = _dot(dy_h, xd_ref[:, lc], 1, 1)
                    dxd_ref[:, lc] = _dot(m_ref[:, lc], dy_h, 0, 0)
                ebdh = eex_ref[:, gcols] * _dot(bm, dh_g, 1, 1)
                dxd = dxd_ref[...] + ebdh
                dm = dm_ref[...]
                t = dm * l_ref[...]
                t128 = (t[:, 0:LANES] + t[:, LANES:2 * LANES]) + (t[:, 2 * LANES:3 * LANES] + t[:, 3 * LANES:])
                fold_ref[...] = t128 + pltpu.roll(t128, CHUNK, axis=1)
                dw_sum = fold_ref[:, 0:CHUNK]
                q = dm * m_ref[...]
                dyea = dy_g * eax_ref[:, gcols]
                red_ref[0:CHUNK, gcols] = q + dyea * _dot(cm, h_g, 1, 1)
                red_ref[CHUNK:2 * CHUNK, gcols] = xd * ebdh
                red_ref[2 * CHUNK:3 * CHUNK, gcols] = dxd * x_g
                daf_ref[ci, :, gcols] = -jnp.sum(q, axis=0, keepdims=True)
                ddx_parts.append(jnp.sum(dy_g * x_g, axis=0, keepdims=True))
                dxbc_ref[:, gcols] = dxd * dtx_ref[:, gcols] + dx_ref[:, gcols] * dy_g
                dxbc_ref[:, ccols] = _dot(dw_sum, bm) + _dot(dyea, h_g)
                dxbc_ref[:, bcols] = _dot(dw_sum, cm, 0, 0) + _dot(xd * eex_ref[:, gcols], dh_g)
                dh_new = _dot(dyea, cm, 0, 0)
                for hh in range(SSM_HPG):
                    h = g * SSM_HPG + hh
                    hrows = slice(h * SSM_HEAD_DIM, (h + 1) * SSM_HEAD_DIM)
                    lrows = slice(hh * SSM_HEAD_DIM, (hh + 1) * SSM_HEAD_DIM)
                    hd_ref[h:h + 1, :] = jnp.sum(st_ref[ci, hrows, :] * dh_ref[hrows, :], axis=0, keepdims=True)
                    dh_ref[hrows, :] = tail_ref[SUBLANES - 1:SUBLANES, h:h + 1] * dh_ref[hrows, :] + dh_new[lrows, :]
            sums = _dot_exact(red_ref[...], ind_t)
            ra, ts = sums[:CHUNK], sums[CHUNK:2 * CHUNK]
            hdh = sum(lax.dot_general(ones, p, (((1,), (1,)), ((), ())), preferred_element_type=F32)
                      for p in _split2(hd_ref[...]))
            da_last = jnp.sum(ts, axis=0, keepdims=True) + tail_ref[SUBLANES - 1:SUBLANES, :] * hdh
            da_ref[rows, :] = ra - ts + jnp.where(is_last_row, da_last, 0.0)
            ddt_ref[rows, :] = sums[2 * CHUNK:]
            sig = jax.nn.sigmoid(pre_v)
            dpre_ref[rows, :] = (dxbc_ref[...] * (sig * (1.0 + pre_v * (1.0 - sig)))).astype(dpre_ref.dtype)
            return ddx + jnp.concatenate(ddx_parts, axis=1)

        ddx = lax.fori_loop(0, SSD_CHUNKS, chunk, jnp.zeros((1, SSM_INNER), F32))
        dd_ref[...] += _dot_exact(jnp.broadcast_to(ddx, (SUBLANES, SSM_INNER)), ind_t)

        @pl.when(pl.program_id(0) == ntile - 1)
        def _():
            finish()

    rev = lambda i: ntile - 1 - i
    whole = lambda a: pl.BlockSpec(a.shape, lambda i, nd=a.ndim: (0,) * nd)
    hbm = pl.BlockSpec(memory_space=pl.ANY)
    wide = lambda: pltpu.VMEM((CHUNK, SSM_INNER), F32)
    group = lambda: pltpu.VMEM((CHUNK, HP), F32)
    res = pl.pallas_call(
        body, name=name, grid=(ntile,),
        in_specs=[pl.BlockSpec((tm, SSM_XBC), lambda i: (rev(i), 0)), pl.BlockSpec((tm, SSM_HEADS), lambda i: (rev(i), 0)),
                  pl.BlockSpec((tm, SSM_HEADS), lambda i: (rev(i), 0)),
                  pl.BlockSpec((SSD_CHUNKS, 1, SSM_INNER), lambda i: (rev(i), 0, 0)),
                  whole(d_x), whole(ind), whole(ind_t),
                  pl.BlockSpec((SSD_CHUNKS, SSM_INNER, SSM_STATE), lambda i: (rev(i), 0, 0)),
                  pl.BlockSpec((tm, SSM_INNER), lambda i: (rev(i), 0))] + [hbm] * npair,
        out_specs=[pl.BlockSpec((tm, SSM_XBC), lambda i: (rev(i), 0)), pl.BlockSpec((tm, SSM_HEADS), lambda i: (rev(i), 0)),
                   pl.BlockSpec((tm, SSM_HEADS), lambda i: (rev(i), 0)),
                   pl.BlockSpec((SSD_CHUNKS, 1, SSM_INNER), lambda i: (rev(i), 0, 0)),
                   pl.BlockSpec((SUBLANES, SSM_HEADS), lambda i: (0, 0))] + [hbm] * npair,
        out_shape=[jax.ShapeDtypeStruct((nrow, SSM_XBC), BF16), jax.ShapeDtypeStruct((nrow, SSM_HEADS), F32),
                   jax.ShapeDtypeStruct((nrow, SSM_HEADS), F32), jax.ShapeDtypeStruct((nrow // CHUNK, 1, SSM_INNER), F32),
                   jax.ShapeDtypeStruct((SUBLANES, SSM_HEADS), F32)]
        + [jax.ShapeDtypeStruct(p.shape, p.dtype) for p in pairs],
        scratch_shapes=[pltpu.VMEM((SSM_INNER, SSM_STATE), F32), pltpu.VMEM((CHUNK, SSM_XBC), F32),
                        pltpu.VMEM((CHUNK, SSM_XBC), F32), wide(), wide(), wide(), wide(),
                        pltpu.VMEM((3 * CHUNK, SSM_INNER), F32),
                        group(), group(), group(), group(), group(), pltpu.VMEM((CHUNK, LANES), F32),
                        pltpu.VMEM((SSM_HEADS, SSM_STATE), F32), pltpu.VMEM((SUBLANES, SSM_HEADS), F32)]
        + _exchange_scratch(npair, N_CHIPS - 1),
        compiler_params=_params(("arbitrary",)),
    )(pre, dt, a_cum, a_flat, d_x, ind, ind_t, states, dy, *pairs)
    return res[:5], res[5:]


LATE = ["w_proj_a", "w_proj_b", "w_out", "ffn_w_up", "ffn_w_down"]
HALF_TILES = {"w_in": 128, "w_proj_a": 128, "w_proj_b": 256, "w_out": 128, "ffn_w_up": 128, "ffn_w_down": 176}


def _late_weights(stacks, shards):
    pa, pb, out, up, down = [_own_slot(stack, own) for stack, own in zip(stacks, shards)]
    return {"w_proj_a": pa.reshape(-1, D_MODEL), "w_proj_b": pb.reshape(-1, D_MODEL), "w_out": out.reshape(-1, D_MODEL),
            "w_up_g": _columns_from_chips(up[:2]), "w_up_v": _columns_from_chips(up[2:]),
            "w_down": down.reshape(-1, D_MODEL)}


def _pair_reduce(tag, names, stacks):
    core = lax.axis_index("c")
    own_half = [_row_half(s, core, 1) for s in stacks]
    other_half = _swap_cores("pair_grads_" + tag, [_row_half(s, 1 - core, 1) for s in stacks])
    return [_pair_sum("pair_" + n, a, b, tm=HALF_TILES[n]) for n, a, b in zip(names, own_half, other_half)]


def _local_step(x, target, w, late_shards):
    w = dict(w)
    g = {}
    bs_col = w["gmlp_bs"].reshape(GMLP_GROUPS, GMLP_BLOCK, 1)
    b0, b1 = w["gate_bias"][0:1], w["gate_bias"][1:2]

    xn = _rms_fwd("mix_norm", x, w["mix_norm_w"])
    big = dict(bm=1024, bn=1024, bk=1024)
    act16 = dict(out_dtype=BF16, **big)
    gates = _mm("in_gates", xn, w["w_g"], **act16)
    za = _mm("in_gmlp", xn, w["w_za"], **act16)
    z = _mm("in_z", xn, w["w_z"], **act16)
    xbc = _mm("in_xbc", xn, w["w_xbc"], **act16)
    dt_raw = _mm("in_dt", xn, w["w_dt"], bm=1024, bn=SSM_HEADS, bk=1024)

    pre = _conv_fwd("ssm_conv_fwd", xbc, w["ssm_conv_w"], w["ssm_conv_b"], tc=1024)
    dt, a_cum = _dt_prep("dt_prep", dt_raw, w["ssm_dt_bias"], w["ssm_a_log"])
    a_flat = jnp.transpose(a_cum.reshape(-1, CHUNK, SSM_HEADS), (0, 2, 1)).reshape(-1, 1, SSM_INNER)
    d_x = jnp.repeat(w["ssm_d"], SSM_HEAD_DIM, axis=1)
    ind = _head_indicator()
    y_ssd, states, late_stacks = _ssd_fwd("ssd_fwd", pre, dt, a_cum, a_flat, d_x, ind, late_shards)
    w.update(_late_weights(late_stacks, late_shards))
    yb_pre = _gate_norm_fwd("gate_norm_fwd", y_ssd, z, w["ssm_norm_w"])
    y_b = _mm("proj_b", yb_pre, w["w_proj_b"], **act16)

    ya_pre = _gmlp_fwd("gmlp_fwd", za, w["gmlp_ln_w"], w["gmlp_ln_b"], w["gmlp_ws"], bs_col)
    y_a = _mm("proj_a", ya_pre, w["w_proj_a"], **act16)

    merged = _merge_fwd("merge_fwd", gates, y_a, y_b, b0, b1)
    h1 = _mm("out_proj", merged, w["w_out"], res=x, **big)

    hn = _rms_fwd("ffn_norm", h1, w["ffn_norm_w"])
    half = dict(bm=1024, bn=D_FF // 2, bk=1024, out_dtype=BF16)
    pg = _mm("ffn_up_gate", hn, w["w_up_g"], **half)
    pv = _mm("ffn_up_val", hn, w["w_up_v"], **half)
    cw, cb = w["ffn_conv_w"], w["ffn_conv_b"]
    gate, val, act = _ffn_act_fwd("ffn_act_fwd", pg, pv, cw[:, :D_FF], cw[:, D_FF:], cb[:, :D_FF], cb[:, D_FF:],
                                  tc=D_FF // 2)
    h2 = _mm("ffn_down", act, w["w_down"], res=h1, bm=512, bn=1024, bk=D_FF // 2)

    dh2, loss_part, g["final_norm_w"] = _final_loss("final_loss", h2, target, w["final_norm_w"].reshape(1, D_MODEL))

    dact = _mm("d_act", dh2, w["w_down"], tb=True, **half)
    wgrad = dict(ta=True, bk=min(2048, x.shape[0]), out_dtype=BF16)
    g["w_down"] = _mm("dw_down", act, dh2, bm=D_FF // 2, bn=1024, **wgrad)
    dgate, dval = _ffn_act_bwd("ffn_act_bwd", dact, gate, val)
    dpg, dcwg, dcbg = _conv_bwd("ffn_conv_bwd_gate", dgate, pg, cw[:, :D_FF], tc=D_FF // 2)
    dpv, dcwv, dcbv = _conv_bwd("ffn_conv_bwd_val", dval, pv, cw[:, D_FF:], tc=D_FF // 2)
    g["ffn_conv_w"] = jnp.concatenate([dcwg, dcwv], axis=1)
    g["ffn_conv_b"] = jnp.concatenate([dcbg, dcbv], axis=1)
    back = dict(bm=1024, bn=1024, bk=D_FF // 2)
    dhn = _mm("d_hn_gate", dpg, w["w_up_g"], tb=True, **back)
    dhn = _mm("d_hn_val", dpv, w["w_up_v"], tb=True, res=dhn, **back)
    g["w_up_g"] = _mm("dw_up_gate", hn, dpg, bm=1024, bn=D_FF // 2, **wgrad)
    g["w_up_v"] = _mm("dw_up_val", hn, dpv, bm=1024, bn=D_FF // 2, **wgrad)
    dh1, g["ffn_norm_w"] = _rms_bwd("ffn_norm_bwd", h1, w["ffn_norm_w"], dhn, dh2)

    dmerged = _mm("d_merged", dh1, w["w_out"], tb=True, **act16)
    g["w_out"] = _mm("dw_out", merged, dh1, bm=1024, bn=1024, **wgrad)
    dgates, dya, dyb, db0, db1 = _merge_bwd("merge_bwd", gates, y_a, y_b, dmerged, b0, b1)
    g["gate_bias"] = jnp.concatenate([db0, db1], axis=0)

    dya_pre = _mm("d_ya_pre", dya, w["w_proj_a"], tb=True, **act16)
    g["w_proj_a"] = _mm("dw_proj_a", ya_pre, dya, bm=1024, bn=1024, **wgrad)
    dyb_pre = _mm("d_yb_pre", dyb, w["w_proj_b"], tb=True, **act16)
    g["w_proj_b"] = _mm("dw_proj_b", yb_pre, dyb, bm=1024, bn=1024, **wgrad)
    late_pairs = _pair_reduce("late", LATE, [
        g["w_proj_a"].reshape(N_CHIPS, -1, D_MODEL), g["w_proj_b"].reshape(N_CHIPS, -1, D_MODEL),
        g["w_out"].reshape(N_CHIPS, -1, D_MODEL),
        jnp.concatenate([_columns_to_chips(g["w_up_g"], 2), _columns_to_chips(g["w_up_v"], 2)], axis=0),
        g["w_down"].reshape(N_CHIPS, -1, D_MODEL)])

    dy_ssd, dz, g["ssm_norm_w"] = _gate_norm_bwd("gate_norm_bwd", y_ssd, z, dyb_pre, w["ssm_norm_w"])
    (dpre, ddt, da_tok, da_flat, dd), late_received = _ssd_bwd(
        "ssd_bwd", pre, dt, a_cum, a_flat, d_x, ind, ind.T, states, dy_ssd, late_pairs)
    g["ssm_d"] = dd[0:1]
    da_src = jnp.transpose(da_flat.reshape(-1, SSM_HEADS, CHUNK), (0, 2, 1)).reshape(-1, SSM_HEADS)
    ddt_raw, g["ssm_dt_bias"], g["ssm_a_log"] = _dt_bwd("dt_bwd", dt_raw, ddt, da_tok, da_src,
                                                         w["ssm_dt_bias"], w["ssm_a_log"])
    dxbc, g["ssm_conv_w"], g["ssm_conv_b"] = _conv_bwd("ssm_conv_bwd", dpre, xbc, w["ssm_conv_w"], tc=1024)

    dza, g["gmlp_ln_w"], g["gmlp_ln_b"], g["gmlp_ws"], dbs = _gmlp_bwd(
        "gmlp_bwd", za, dya_pre, w["gmlp_ln_w"], w["gmlp_ln_b"], w["gmlp_ws"], bs_col)
    g["gmlp_bs"] = dbs.reshape(GMLP_GROUPS, GMLP_BLOCK)

    dxn = _mm("d_xn_gates", dgates, w["w_g"], tb=True, **big)
    dxn = _mm("d_xn_gmlp", dza, w["w_za"], tb=True, res=dxn, **big)
    dxn = _mm("d_xn_z", dz, w["w_z"], tb=True, res=dxn, **big)
    dxn = _mm("d_xn_xbc", dxbc, w["w_xbc"], tb=True, res=dxn, **big)
    dxn = _mm("d_xn_dt", ddt_raw, w["w_dt"], tb=True, res=dxn, bm=1024, bn=1024, bk=SSM_HEADS)
    g["w_g"] = _mm("dw_gates", xn, dgates, bm=1024, bn=1024, **wgrad)
    g["w_za"] = _mm("dw_gmlp", xn, dza, bm=1024, bn=1024, **wgrad)
    g["w_z"] = _mm("dw_z", xn, dz, bm=1024, bn=1024, **wgrad)
    g["w_xbc"] = _mm("dw_xbc", xn, dxbc, bm=1024, bn=1024, **wgrad)
    g["w_dt"] = _mm("dw_dt", xn, ddt_raw, bm=1024, bn=SSM_HEADS, **wgrad)
    grad_x, g["mix_norm_w"] = _rms_bwd("mix_norm_bwd", x, w["mix_norm_w"], dxn, dh1)
    return loss_part, grad_x, g, late_pairs, late_received


def _position():
    return lax.axis_index("x"), lax.axis_index("y"), lax.axis_index("c")


def _own_slot(stack, own):
    chip = 2 * lax.axis_index("x") + lax.axis_index("y")
    return lax.dynamic_update_index_in_dim(stack, own, chip, axis=0)


def _scatter_phases(ins, outs, send_sems, recv_sems):
    n = len(ins)
    x, y, c = _position()
    me = 2 * x + y
    peers = [(1 - x, y), (x, 1 - y), (1 - x, 1 - y)]

    def copy(i, k, src_slot, dst_slot):
        px, py = peers[k]
        return pltpu.make_async_remote_copy(
            src_ref=ins[i].at[src_slot], dst_ref=outs[i].at[dst_slot],
            send_sem=send_sems.at[i, k], recv_sem=recv_sems.at[i, k],
            device_id=(px, py, c), device_id_type=MESH)

    def start():
        for i in range(n):
            for k, (px, py) in enumerate(peers):
                copy(i, k, 2 * px + py, me).start()

    def finish():
        for i in range(n):
            for k, (px, py) in enumerate(peers):
                copy(i, k, me, 2 * px + py).wait_recv()
        for i in range(n):
            for k, (px, py) in enumerate(peers):
                copy(i, k, 2 * px + py, me).wait_send()

    return start, finish


def _exchange_scratch(n, per_array):
    return [pltpu.SemaphoreType.DMA((n, per_array)), pltpu.SemaphoreType.DMA((n, per_array))]


def _scatter_chips(name, arrs):
    n = len(arrs)

    def body(*refs):
        start, finish = _scatter_phases(refs[:n], refs[n:2 * n], *refs[2 * n:])
        start()
        finish()

    hbm = pl.BlockSpec(memory_space=pl.ANY)
    return pl.pallas_call(
        body, name=name,
        in_specs=[hbm] * n, out_specs=[hbm] * n,
        out_shape=[jax.ShapeDtypeStruct(a.shape, a.dtype) for a in arrs],
        scratch_shapes=_exchange_scratch(n, N_CHIPS - 1),
        compiler_params=pltpu.CompilerParams(has_side_effects=True),
    )(*arrs)


def _half_rows(ref_rows, which):
    half = ref_rows // 2
    return pl.ds(pl.multiple_of(which * half, 2 * SUBLANES), half)


GATHER_SEMS = 2 * (N_CHIPS - 1)


def _gather_phases(nrows, ns, ins, outs, send_sems, recv_sems):
    n = len(ins)
    x, y, c = _position()
    me = 2 * x + y
    sibling = (x, y, 1 - c)
    chips = [(1 - x, y), (x, 1 - y), (1 - x, 1 - y)]

    def remote(i, k, src, dst, to):
        return pltpu.make_async_remote_copy(src_ref=src, dst_ref=dst, send_sem=send_sems.at[i, k],
                                            recv_sem=recv_sems.at[i, k], device_id=to, device_id_type=MESH)

    def over_ici(i, k):
        px, py = chips[k]
        rows = _half_rows(nrows[i], c) if i < ns else slice(None)
        return remote(i, k, ins[i].at[rows], outs[i].at[me, rows], (px, py, c))

    def landed(i, k, which):
        px, py = chips[k]
        return outs[i].at[2 * px + py, _half_rows(nrows[i], which)] if i < ns else outs[i].at[2 * px + py]

    def start():
        for i in range(n):
            for k in range(N_CHIPS - 1):
                over_ici(i, k).start()

    def forward():
        for i in range(n):
            for k in range(N_CHIPS - 1):
                piece = landed(i, k, c)
                remote(i, k, piece, piece, (*chips[k], c)).wait_recv()
                if i < ns:
                    remote(i, N_CHIPS - 1 + k, piece, piece, sibling).start()

    def finish():
        for i in range(ns):
            for k in range(N_CHIPS - 1):
                piece = landed(i, k, 1 - c)
                remote(i, N_CHIPS - 1 + k, piece, piece, sibling).wait_recv()
        for i in range(n):
            for k in range(N_CHIPS - 1):
                over_ici(i, k).wait_send()
                if i < ns:
                    piece = landed(i, k, c)
                    remote(i, N_CHIPS - 1 + k, piece, piece, sibling).wait_send()

    return start, forward, finish


def _gather_chips_split(name, split, whole):
    arrs = list(split) + list(whole)
    n = len(arrs)

    def body(*refs):
        phases = _gather_phases([a.shape[0] for a in arrs], len(split), refs[:n], refs[n:2 * n], *refs[2 * n:])
        for phase in phases:
            phase()

    hbm = pl.BlockSpec(memory_space=pl.ANY)
    return pl.pallas_call(
        body, name=name, in_specs=[hbm] * n, out_specs=[hbm] * n,
        out_shape=[jax.ShapeDtypeStruct((N_CHIPS,) + a.shape, a.dtype) for a in arrs],
        scratch_shapes=_exchange_scratch(n, GATHER_SEMS),
        compiler_params=pltpu.CompilerParams(has_side_effects=True),
    )(*arrs)


def _swap_cores(name, arrs):
    n = len(arrs)

    def body(*refs):
        ins, outs = refs[:n], refs[n:2 * n]
        send_sems, recv_sems = refs[2 * n:]
        x, y, c = _position()
        copies = [pltpu.make_async_remote_copy(src_ref=ins[i], dst_ref=outs[i], send_sem=send_sems.at[i],
                                               recv_sem=recv_sems.at[i], device_id=(x, y, 1 - c), device_id_type=MESH)
                  for i in range(n)]
        for cp in copies:
            cp.start()
        for cp in copies:
            cp.wait_recv()
        for cp in copies:
            cp.wait_send()

    hbm = pl.BlockSpec(memory_space=pl.ANY)
    return pl.pallas_call(
        body, name=name, in_specs=[hbm] * n, out_specs=[hbm] * n,
        out_shape=[jax.ShapeDtypeStruct(a.shape, a.dtype) for a in arrs],
        scratch_shapes=[pltpu.SemaphoreType.DMA((n,)), pltpu.SemaphoreType.DMA((n,))],
        compiler_params=pltpu.CompilerParams(has_side_effects=True),
    )(*arrs)


def _row_half(a, which, axis):
    half = a.shape[axis] // 2
    return lax.dynamic_slice_in_dim(a, which * half, half, axis=axis)


def _all_reduce(name, pack):
    def body(in_ref, out_ref, buf, send_sems, recv_sems):
        x, y, c = _position()
        me = 4 * x + 2 * y + c
        flips = [(dx, dy, dc) for dx in (0, 1) for dy in (0, 1) for dc in (0, 1) if (dx, dy, dc) != (0, 0, 0)]
        peers = [((1 - x) if dx else x, (1 - y) if dy else y, (1 - c) if dc else c) for dx, dy, dc in flips]
        buf[me] = in_ref[...]
        sends = []
        for k, peer in enumerate(peers):
            cp = pltpu.make_async_remote_copy(src_ref=in_ref, dst_ref=buf.at[me], send_sem=send_sems.at[k],
                                              recv_sem=recv_sems.at[k], device_id=peer, device_id_type=MESH)
            cp.start()
            sends.append(cp)
        for k, (px, py, pc) in enumerate(peers):
            pltpu.make_async_remote_copy(src_ref=in_ref, dst_ref=buf.at[4 * px + 2 * py + pc], send_sem=send_sems.at[k],
                                         recv_sem=recv_sems.at[k], device_id=(px, py, pc), device_id_type=MESH).wait_recv()
        total = buf[0]
        for j in range(1, N_DEV):
            total = total + buf[j]
        out_ref[...] = total
        for cp in sends:
            cp.wait_send()

    vmem = pl.BlockSpec(memory_space=pltpu.VMEM)
    return pl.pallas_call(
        body, name=name, in_specs=[vmem], out_specs=vmem,
        out_shape=jax.ShapeDtypeStruct(pack.shape, F32),
        scratch_shapes=[pltpu.VMEM((N_DEV,) + pack.shape, F32), pltpu.SemaphoreType.DMA((N_DEV - 1,)),
                        pltpu.SemaphoreType.DMA((N_DEV - 1,))],
        compiler_params=pltpu.CompilerParams(has_side_effects=True, vmem_limit_bytes=VMEM_LIMIT_V7X),
    )(pack)


def _pack(arrs):
    rows = [a.reshape(-1, LANES) for a in arrs]
    total = sum(r.shape[0] for r in rows)
    rows.append(jnp.zeros((-total % SUBLANES, LANES), F32))
    return jnp.concatenate(rows, axis=0)


def _unpack(pack, shapes):
    out, off = [], 0
    for s in shapes:
        nrow = 1
        for d in s:
            nrow *= d
        nrow //= LANES
        out.append(pack[off:off + nrow].reshape(s))
        off += nrow
    return out


SMALL = ["mix_norm_w", "gate_bias", "gmlp_ln_w", "gmlp_ln_b", "gmlp_ws", "gmlp_bs", "ssm_conv_w", "ssm_conv_b",
         "ssm_dt_bias", "ssm_a_log", "ssm_d", "ssm_norm_w", "ffn_norm_w", "ffn_conv_w", "ffn_conv_b", "final_norm_w"]
SMALL_SHARDED = ("gate_bias", "ssm_conv_w", "ffn_conv_w")
BIG = ["w_in", "w_proj_a", "w_proj_b", "w_out", "ffn_w_up", "ffn_w_down"]
WEIGHTS = ["mix_norm_w", "w_in", "gate_bias", "gmlp_ln_w", "gmlp_ln_b", "gmlp_ws", "gmlp_bs", "ssm_conv_w",
           "ssm_conv_b", "ssm_dt_bias", "ssm_a_log", "ssm_d", "ssm_norm_w", "w_proj_a", "w_proj_b", "w_out",
           "ffn_norm_w", "ffn_w_up", "ffn_conv_w", "ffn_conv_b", "ffn_w_down", "final_norm_w"]
IN_SPLITS = [0, 2048, 4096, 6144, 9216, 9248]


def _columns_from_chips(stack):
    return jnp.transpose(stack, (1, 0, 2)).reshape(stack.shape[1], -1)


def _columns_to_chips(full, parts=N_CHIPS):
    rows, cols = full.shape
    return jnp.transpose(full.reshape(rows, parts, cols // parts), (1, 0, 2))


def kernel(x, mix_norm_w, w_in, gate_bias, gmlp_ln_w, gmlp_ln_b, gmlp_ws, gmlp_bs, ssm_conv_w, ssm_conv_b, ssm_dt_bias, ssm_a_log, ssm_d, ssm_norm_w, w_proj_a, w_proj_b, w_out, ffn_norm_w, ffn_w_up, ffn_conv_w, ffn_conv_b, ffn_w_down, final_norm_w, loss_target, m_mix_norm_w, m_w_in, m_gate_bias, m_gmlp_ln_w, m_gmlp_ln_b, m_gmlp_ws, m_gmlp_bs, m_ssm_conv_w, m_ssm_conv_b, m_ssm_dt_bias, m_ssm_a_log, m_ssm_d, m_ssm_norm_w, m_w_proj_a, m_w_proj_b, m_w_out, m_ffn_norm_w, m_ffn_w_up, m_ffn_conv_w, m_ffn_conv_b, m_ffn_w_down, m_final_norm_w, v_mix_norm_w, v_w_in, v_gate_bias, v_gmlp_ln_w, v_gmlp_ln_b, v_gmlp_ws, v_gmlp_bs, v_ssm_conv_w, v_ssm_conv_b, v_ssm_dt_bias, v_ssm_a_log, v_ssm_d, v_ssm_norm_w, v_w_proj_a, v_w_proj_b, v_w_out, v_ffn_norm_w, v_ffn_w_up, v_ffn_conv_w, v_ffn_conv_b, v_ffn_w_down, v_final_norm_w):
    args = dict(locals())
    weights = {n: args[n] for n in WEIGHTS}
    moments_m = {n: args["m_" + n] for n in WEIGHTS}
    moments_v = {n: args["v_" + n] for n in WEIGHTS}
    chip = 2 * lax.axis_index("x") + lax.axis_index("y")

    shards = [weights["w_in"][0].astype(BF16)] + [weights[n][0] for n in SMALL_SHARDED]
    gathered = _gather_chips_split("gather_weights", shards[:1], shards[1:])
    w_in_s, gb_s, scw_s, fcw_s = [_own_slot(stack, own) for stack, own in zip(gathered, shards)]
    late_shards = [weights[n][0].astype(BF16) for n in LATE]
    w_in_full = _columns_from_chips(w_in_s)
    full = {"w_" + nm: w_in_full[:, IN_SPLITS[k]:IN_SPLITS[k + 1]] for k, nm in enumerate(["g", "za", "z", "xbc", "dt"])}
    full["gate_bias"] = _columns_from_chips(gb_s)
    full["ssm_conv_w"] = _columns_from_chips(scw_s)
    full["ffn_conv_w"] = _columns_from_chips(fcw_s)
    for n in SMALL:
        if n not in SMALL_SHARDED:
            full[n] = weights[n] if n == "final_norm_w" else weights[n][0]
    for n in ("mix_norm_w", "ffn_norm_w", "ssm_conv_b", "ssm_dt_bias", "ssm_a_log", "ssm_d", "ssm_norm_w", "ffn_conv_b"):
        full[n] = full[n].reshape(1, -1)

    loss_part, grad_x, g, late_pairs, late_received = _local_step(x[0], loss_target[0], full, late_shards)

    per_head = ["ssm_dt_bias", "ssm_a_log", "ssm_d"]
    rest = [n for n in SMALL if n not in per_head]
    head_row = jnp.concatenate([g[n] for n in per_head] + [jnp.zeros((1, LANES - 3 * SSM_HEADS), F32)], axis=1)
    pack = _pack([loss_part, head_row] + [g[n] for n in rest])
    reduced = _unpack(_all_reduce("reduce_small", pack), [(1, LANES), (1, LANES)] + [g[n].shape for n in rest])
    loss = reduced[0][0, 0]
    small_grads = {n: reduced[1][:, k * SSM_HEADS:(k + 1) * SSM_HEADS] for k, n in enumerate(per_head)}
    for n, r in zip(rest, reduced[2:]):
        if n in SMALL_SHARDED:
            width = weights[n].shape[2]
            r = lax.dynamic_slice_in_dim(r, chip * width, width, axis=1)
        small_grads[n] = r
    two_d = lambda a: a.reshape(-1, a.shape[-1])
    upd = _adamw_small("adamw_small", *[[two_d(d[n]) for n in SMALL]
                                        for d in (weights, small_grads, moments_m, moments_v)])
    small_out = [[small_grads[n] for n in SMALL]] + list(upd)
    small_out = [[a.reshape(weights[n].shape) for n, a in zip(SMALL, kind)] for kind in small_out]

    dw_in = jnp.concatenate([g["w_g"], g["w_za"], g["w_z"], g["w_xbc"], g["w_dt"]], axis=1)
    in_pairs = _pair_reduce("in", ["w_in"], [_columns_to_chips(dw_in)])
    pair = in_pairs + list(late_pairs)
    received = list(_scatter_chips("scatter_grads", in_pairs)) + list(late_received)
    received = [_own_slot(r, lax.dynamic_index_in_dim(p, chip, 0, keepdims=False)) for r, p in zip(received, pair)]
    halves = [_sum_slots("sum_" + n, r, tm=HALF_TILES[n], rs=2 * SUBLANES) for n, r in zip(BIG, received)]
    tiles = {"w_in": 128, "w_proj_a": 256, "w_proj_b": 256, "w_out": 256, "ffn_w_up": 128, "ffn_w_down": 176}
    core = lax.axis_index("c")
    other = _swap_cores("join_grads", halves)
    reduced = [jnp.concatenate([jnp.where(core == 0, a, b), jnp.where(core == 0, b, a)], axis=0)
               for a, b in zip(halves, other)]
    big_out = {}
    for n, grad in zip(BIG, reduced):
        big_out[n] = _adamw("adamw_" + n, weights[n][0], grad, moments_m[n][0], moments_v[n][0],
                            tm=tiles[n], rs=SUBLANES)

    per_kind = [[], [], [], []]
    for n in WEIGHTS:
        for kind in range(4):
            if n in big_out:
                per_kind[kind].append(big_out[n][kind].reshape(weights[n].shape))
            else:
                per_kind[kind].append(small_out[kind][SMALL.index(n)])
    return (loss, grad_x[None], *per_kind[0], *per_kind[1], *per_kind[2], *per_kind[3])
```

```python
import jax
import jax.numpy as jnp
from jax import lax
from jax.experimental import pallas as pl
from jax.experimental.pallas import tpu as pltpu

F32 = jnp.float32
BF16 = jnp.bfloat16
MESH = pl.DeviceIdType.MESH

EPS = 1e-5
D_MODEL = 1024
GMLP_BLOCK = 128
GMLP_GROUPS = 8
CHUNK = 64
SSM_INNER = 2048
SSM_HEADS = 32
SSM_HEAD_DIM = 64
SSM_GROUPS = 4
SSM_HPG = 8
SSM_STATE = 128
SSM_CONV = 4
SSM_XBC = 3072
D_FF = 2816
FFN_CONV = 3
N_CHIPS = 4
N_DEV = 8

ADAM_LR = 0.001
ADAM_B1 = 0.9
ADAM_B2 = 0.999
ADAM_EPS = 1e-08
ADAM_WD = 0.01
ADAM_STEP = 10

VMEM_LIMIT_V7X = 56 * 1024 * 1024
SUBLANES = 8
LANES = 128


def _params(sem=None):
    return pltpu.CompilerParams(dimension_semantics=sem, vmem_limit_bytes=VMEM_LIMIT_V7X)


def _dot(a, b, ca=1, cb=0):
    return lax.dot_general(a.astype(BF16), b.astype(BF16), (((ca,), (cb,)), ((), ())),
                           preferred_element_type=F32)


def _mm(name, a, b, *, ta=False, tb=False, out_dtype=F32, bm, bn, bk, res=None):
    m, k = (a.shape[1], a.shape[0]) if ta else a.shape
    k2, n = (b.shape[1], b.shape[0]) if tb else b.shape
    assert k == k2 and m % bm == 0 and n % bn == 0 and k % bk == 0, (name, a.shape, b.shape)
    nk = k // bk
    a_spec = (pl.BlockSpec((bk, bm), lambda i, j, kk: (kk, i)) if ta
              else pl.BlockSpec((bm, bk), lambda i, j, kk: (i, kk)))
    b_spec = (pl.BlockSpec((bn, bk), lambda i, j, kk: (j, kk)) if tb
              else pl.BlockSpec((bk, bn), lambda i, j, kk: (kk, j)))
    o_spec = pl.BlockSpec((bm, bn), lambda i, j, kk: (i, j))
    has_res = res is not None

    def body(*refs):
        a_ref, b_ref = refs[0], refs[1]
        r_ref = refs[2] if has_res else None
        o_ref = refs[3] if has_res else refs[2]
        p = _dot(a_ref[...], b_ref[...], 0 if ta else 1, 1 if tb else 0)

        def finish(total):
            if has_res:
                total = total + r_ref[...]
            o_ref[...] = total.astype(out_dtype)

        if nk == 1:
            finish(p)
        else:
            acc_ref = refs[-1]
            kk = pl.program_id(2)

            @pl.when(kk == 0)
            def _():
                acc_ref[...] = p

            @pl.when(kk > 0)
            def _():
                acc_ref[...] += p

            @pl.when(kk == nk - 1)
            def _():
                finish(acc_ref[...])

    return pl.pallas_call(
        body, name=name,
        grid=(m // bm, n // bn, nk),
        in_specs=[a_spec, b_spec] + ([o_spec] if has_res else []),
        out_specs=o_spec,
        out_shape=jax.ShapeDtypeStruct((m, n), out_dtype),
        scratch_shapes=[pltpu.VMEM((bm, bn), F32)] if nk > 1 else [],
        compiler_params=_params(("parallel", "parallel", "arbitrary")),
    )(*([a, b] + ([res] if has_res else [])))


def _mm_sum(name, pairs, *, bm, bk):
    m, n = pairs[0][0].shape[0], pairs[0][1].shape[0]
    steps, first = [], []
    for a, b in pairs:
        k = a.shape[1]
        assert a.shape[0] == m and b.shape == (n, k) and m % bm == 0 and (k % bk == 0 or k < bk), (name, a.shape, b.shape)
        first.append(sum(steps))
        steps.append(max(k // bk, 1))
    total = sum(steps)
    in_specs = []
    for (a, b), off, cnt in zip(pairs, first, steps):
        width = min(bk, a.shape[1])
        in_specs.append(pl.BlockSpec((bm, width), lambda i, kk, off=off, cnt=cnt: (i, jnp.clip(kk - off, 0, cnt - 1))))
        in_specs.append(pl.BlockSpec((n, width), lambda i, kk, off=off, cnt=cnt: (0, jnp.clip(kk - off, 0, cnt - 1))))

    def body(*refs):
        o_ref, acc_ref = refs[-2], refs[-1]
        kk = pl.program_id(1)
        for s, (off, cnt) in enumerate(zip(first, steps)):
            @pl.when((kk >= off) & (kk < off + cnt))
            def _(s=s, off=off):
                p = _dot(refs[2 * s][...], refs[2 * s + 1][...], 1, 1)
                if off == 0:
                    @pl.when(kk == 0)
                    def _():
                        acc_ref[...] = p

                    @pl.when(kk > 0)
                    def _():
                        acc_ref[...] += p
                else:
                    acc_ref[...] += p

        @pl.when(kk == total - 1)
        def _():
            o_ref[...] = acc_ref[...]

    return pl.pallas_call(
        body, name=name, grid=(m // bm, total),
        in_specs=in_specs, out_specs=pl.BlockSpec((bm, n), lambda i, kk: (i, 0)),
        out_shape=jax.ShapeDtypeStruct((m, n), F32),
        scratch_shapes=[pltpu.VMEM((bm, n), F32)],
        compiler_params=_params(("parallel", "arbitrary")),
    )(*[t for pair in pairs for t in pair])


def _rows(name, fn, ins, params, outs, accs, *, tm, rs, unroll=4):
    nrow = ins[0][0].shape[-2]
    while tm % (rs * unroll):
        unroll //= 2
    assert nrow % tm == 0 and tm % rs == 0, (name, nrow, tm, rs)
    n_in, n_p, n_out, n_acc = len(ins), len(params), len(outs), len(accs)
    in_specs = []
    for spec in ins:
        arr, width, cb = spec[:3]
        if len(spec) == 4:
            in_specs.append(pl.BlockSpec((None, tm, width), lambda i, cb=cb, lead=spec[3]: (lead, i, cb)))
        else:
            in_specs.append(pl.BlockSpec((tm, width), lambda i, cb=cb: (i, cb)))
    for p in params:
        in_specs.append(pl.BlockSpec(p.shape, lambda i, nd=p.ndim: (0,) * nd))
    out_specs = [pl.BlockSpec((tm, w), lambda i: (i, 0)) for w, _ in outs]
    out_specs += [pl.BlockSpec(s, lambda i: (0, 0)) for s in accs]
    out_shape = [jax.ShapeDtypeStruct((nrow, w), dt) for w, dt in outs]
    out_shape += [jax.ShapeDtypeStruct(s, F32) for s in accs]

    def body(*refs):
        in_refs = refs[:n_in]
        p_refs = refs[n_in:n_in + n_p]
        o_refs = refs[n_in + n_p:n_in + n_p + n_out]
        a_refs = refs[n_in + n_p + n_out:]
        pv = [p[...] for p in p_refs]

        if n_acc:
            @pl.when(pl.program_id(0) == 0)
            def _():
                for a_ref in a_refs:
                    a_ref[...] = jnp.zeros_like(a_ref)

        def step(r, carry):
            for u in range(unroll):
                sl = pl.ds(pl.multiple_of((r * unroll + u) * rs, rs), rs)
                vals = [ref[sl, :].astype(F32) for ref in in_refs]
                row_out, sums = fn(*vals, *pv)
                for o_ref, v in zip(o_refs, row_out):
                    o_ref[sl, :] = v.astype(o_ref.dtype)
                carry = tuple(c + s for c, s in zip(carry, sums))
            return carry

        init = tuple(jnp.zeros(s, F32) for s in accs)
        total = lax.fori_loop(0, tm // (rs * unroll), step, init)
        for a_ref, t in zip(a_refs, total):
            a_ref[...] += t

    res = pl.pallas_call(
        body, name=name, grid=(nrow // tm,),
        in_specs=in_specs, out_specs=out_specs, out_shape=out_shape,
        compiler_params=_params(("arbitrary",)),
    )(*([s[0] for s in ins] + list(params)))
    return res


def _rms(x, w):
    return x * lax.rsqrt(jnp.mean(x * x, axis=-1, keepdims=True) + EPS) * w


def _colsum(v):
    return jnp.sum(v, axis=0, keepdims=True)


def _rms_fwd(name, x, w):
    def fn(xv, wv):
        return (_rms(xv, wv),), ()
    return _rows(name, fn, [(x, D_MODEL, 0)], [w], [(D_MODEL, BF16)], [], tm=512, rs=16)[0]


def _rms_bwd(name, x, w, dy, dres):
    def fn(xv, dyv, drv, wv):
        _, vjp = jax.vjp(_rms, xv, wv)
        dx, dw = vjp(dyv)
        return (drv + dx,), (dw,)
    return _rows(name, fn, [(x, D_MODEL, 0), (dy, D_MODEL, 0), (dres, D_MODEL, 0)], [w],
                 [(D_MODEL, F32)], [(1, D_MODEL)], tm=512, rs=16)


def _down_loss(name, act, w_down, h1, target, w):
    nrow, kdim = act.shape
    bm, bk, rs = 512, D_FF // 2, 2 * SUBLANES
    nk = kdim // bk

    def loss_rows(hv, tv, wv):
        y, vjp = jax.vjp(_rms, hv, wv)
        err = y - tv
        part = 0.5 * jnp.sum(jnp.mean(err * err, axis=-1, keepdims=True), axis=0, keepdims=True)
        dh, dw = vjp(err / D_MODEL)
        return dh, jnp.broadcast_to(part, (1, LANES)), dw

    def body(a_ref, b_ref, h1_ref, t_ref, w_ref, dh_ref, loss_ref, dw_ref, acc_ref):
        i, kk = pl.program_id(0), pl.program_id(1)
        p = _dot(a_ref[...], b_ref[...])

        @pl.when((i == 0) & (kk == 0))
        def _():
            loss_ref[...] = jnp.zeros_like(loss_ref)
            dw_ref[...] = jnp.zeros_like(dw_ref)

        @pl.when(kk == 0)
        def _():
            acc_ref[...] = p

        @pl.when(kk > 0)
        def _():
            acc_ref[...] += p

        @pl.when(kk == nk - 1)
        def _():
            wv = w_ref[...]

            def step(r, carry):
                sl = pl.ds(pl.multiple_of(r * rs, rs), rs)
                dh, part, dw = loss_rows(acc_ref[sl, :] + h1_ref[sl, :], t_ref[sl, :], wv)
                dh_ref[sl, :] = dh
                return carry[0] + part, carry[1] + dw

            part, dw = lax.fori_loop(0, bm // rs, step, (jnp.zeros((1, LANES), F32), jnp.zeros((1, D_MODEL), F32)))
            loss_ref[...] += part
            dw_ref[...] += dw

    rows = pl.BlockSpec((bm, D_MODEL), lambda i, kk: (i, 0))
    whole = lambda shape: pl.BlockSpec(shape, lambda i, kk: (0, 0))
    return pl.pallas_call(
        body, name=name, grid=(nrow // bm, nk),
        in_specs=[pl.BlockSpec((bm, bk), lambda i, kk: (i, kk)), pl.BlockSpec((bk, D_MODEL), lambda i, kk: (kk, 0)),
                  rows, rows, whole((1, D_MODEL))],
        out_specs=[rows, whole((1, LANES)), whole((1, D_MODEL))],
        out_shape=[jax.ShapeDtypeStruct((nrow, D_MODEL), F32), jax.ShapeDtypeStruct((1, LANES), F32),
                   jax.ShapeDtypeStruct((1, D_MODEL), F32)],
        scratch_shapes=[pltpu.VMEM((bm, D_MODEL), F32)],
        compiler_params=_params(("arbitrary", "arbitrary")),
    )(act, w_down, h1, target, w)


def _merge(ga, gb, ya, yb, b0, b1):
    return jax.nn.sigmoid(ga + b0) * ya + jax.nn.sigmoid(gb + b1) * yb


def _merge_fwd(name, g, ya, yb, b0, b1):
    def fn(ga, gb, yav, ybv, b0v, b1v):
        return (_merge(ga, gb, yav, ybv, b0v, b1v),), ()
    return _rows(name, fn, [(g, D_MODEL, 0), (g, D_MODEL, 1), (ya, D_MODEL, 0), (yb, D_MODEL, 0)],
                 [b0, b1], [(D_MODEL, BF16)], [], tm=512, rs=16)[0]


def _merge_bwd(name, g, ya, yb, dm, b0, b1):
    def fn(ga, gb, yav, ybv, dmv, b0v, b1v):
        _, vjp = jax.vjp(_merge, ga, gb, yav, ybv, b0v, b1v)
        dga, dgb, dya, dyb, db0, db1 = vjp(dmv)
        return (jnp.concatenate([dga, dgb], axis=1), dya, dyb), (db0, db1)
    return _rows(name, fn,
                 [(g, D_MODEL, 0), (g, D_MODEL, 1), (ya, D_MODEL, 0), (yb, D_MODEL, 0), (dm, D_MODEL, 0)],
                 [b0, b1], [(2 * D_MODEL, BF16), (D_MODEL, BF16), (D_MODEL, BF16)],
                 [(1, D_MODEL), (1, D_MODEL)], tm=512, rs=16)


GROUP_W = SSM_INNER // SSM_GROUPS


def _gate_norm_group(y, z, nw):
    v = y * jax.nn.silu(z)
    return v * lax.rsqrt(jnp.mean(v * v, axis=-1, keepdims=True) + EPS) * nw


def _gate_norm_fwd(name, y, z, nw):
    def fn(yv, zv, nwv):
        parts = [_gate_norm_group(yv[:, k * GROUP_W:(k + 1) * GROUP_W], zv[:, k * GROUP_W:(k + 1) * GROUP_W],
                                  nwv[:, k * GROUP_W:(k + 1) * GROUP_W]) for k in range(SSM_GROUPS)]
        return (jnp.concatenate(parts, axis=1),), ()
    return _rows(name, fn, [(y, SSM_INNER, 0), (z, SSM_INNER, 0)], [nw], [(SSM_INNER, BF16)], [],
                 tm=512, rs=16)[0]


def _gate_norm_bwd(name, y, z, dout, nw):
    def fn(yv, zv, dv, nwv):
        dys, dzs, dns = [], [], []
        for k in range(SSM_GROUPS):
            sl = slice(k * GROUP_W, (k + 1) * GROUP_W)
            _, vjp = jax.vjp(_gate_norm_group, yv[:, sl], zv[:, sl], nwv[:, sl])
            dy, dz, dn = vjp(dv[:, sl])
            dys.append(dy), dzs.append(dz), dns.append(dn)
        return (jnp.concatenate(dys, axis=1), jnp.concatenate(dzs, axis=1)), (jnp.concatenate(dns, axis=1),)
    return _rows(name, fn, [(y, SSM_INNER, 0), (z, SSM_INNER, 0), (dout, SSM_INNER, 0)], [nw],
                 [(SSM_INNER, F32), (SSM_INNER, BF16)], [(1, SSM_INNER)], tm=512, rs=16)


def _softplus(v):
    return jnp.maximum(v, 0.0) + jnp.log1p(jnp.exp(-jnp.abs(v)))


def _chunk_cumsum(v, reverse=False):
    row = lax.broadcasted_iota(jnp.int32, v.shape, 0)
    step = 1
    while step < CHUNK:
        if reverse:
            shifted = pltpu.roll(v, CHUNK - step, axis=0)
            v = v + jnp.where(row < CHUNK - step, shifted, 0.0)
        else:
            shifted = pltpu.roll(v, step, axis=0)
            v = v + jnp.where(row >= step, shifted, 0.0)
        step *= 2
    return v


def _dt_prep(name, dt_raw, dt_bias, a_log):
    def fn(rv, bv, alv):
        dt = _softplus(rv + bv)
        return (dt, _chunk_cumsum(dt * (-jnp.exp(alv)))), ()
    return _rows(name, fn, [(dt_raw, SSM_HEADS, 0)], [dt_bias, a_log],
                 [(SSM_HEADS, F32), (SSM_HEADS, F32)], [], tm=512, rs=CHUNK)


def _dt_bwd(name, dt_raw, ddt, da1, da2, dt_bias, a_log):
    def fn(rv, ddv, d1, d2, bv, alv):
        pre = rv + bv
        dt = _softplus(pre)
        a_neg = -jnp.exp(alv)
        back = _chunk_cumsum(d1 + d2, reverse=True)
        d_dt = ddv + back * a_neg
        d_raw = d_dt * jax.nn.sigmoid(pre)
        return (d_raw,), (_colsum(d_raw), _colsum(back * dt) * a_neg)
    return _rows(name, fn, [(dt_raw, SSM_HEADS, 0), (ddt, SSM_HEADS, 0), (da1, SSM_HEADS, 0), (da2, SSM_HEADS, 0)],
                 [dt_bias, a_log], [(SSM_HEADS, BF16)], [(1, SSM_HEADS), (1, SSM_HEADS)], tm=512, rs=CHUNK)


def _adamw_math(w, g, m, v):
    m_new = ADAM_B1 * m + (1.0 - ADAM_B1) * g
    v_new = ADAM_B2 * v + (1.0 - ADAM_B2) * jnp.square(g)
    m_hat = m_new / (1.0 - ADAM_B1 ** ADAM_STEP)
    v_hat = v_new / (1.0 - ADAM_B2 ** ADAM_STEP)
    delta = -ADAM_LR * (m_hat / (jnp.sqrt(v_hat) + ADAM_EPS) + ADAM_WD * w)
    return delta, m_new, v_new


def _adamw(name, w, g, m, v, *, tm, rs):
    width = w.shape[1]

    def fn(wv, mv, vv, gv):
        return (gv,) + _adamw_math(wv, gv, mv, vv), ()
    return _rows(name, fn, [(w, width, 0), (m, width, 0), (v, width, 0), (g, width, 0)],
                 [], [(width, F32)] * 4, [], tm=tm, rs=rs)


def _adamw_small(name, ws, gs, ms, vs):
    n = len(ws)

    def body(*refs):
        w_refs, g_refs, m_refs, v_refs = (refs[k * n:(k + 1) * n] for k in range(4))
        outs = refs[4 * n:]
        for i in range(n):
            res = _adamw_math(w_refs[i][...], g_refs[i][...], m_refs[i][...], v_refs[i][...])
            for k in range(3):
                outs[k * n + i][...] = res[k]

    vmem = pl.BlockSpec(memory_space=pltpu.VMEM)
    res = pl.pallas_call(
        body, name=name, in_specs=[vmem] * (4 * n), out_specs=[vmem] * (3 * n),
        out_shape=[jax.ShapeDtypeStruct(w.shape, F32) for w in ws] * 3,
        compiler_params=pltpu.CompilerParams(vmem_limit_bytes=VMEM_LIMIT_V7X),
    )(*ws, *gs, *ms, *vs)
    return res[:n], res[n:2 * n], res[2 * n:]


def _pair_sum(name, a, b, *, tm):
    shape = a.shape
    flat = (shape[0] * shape[1], shape[2])

    def fn(av, bv):
        return (av.astype(F32) + bv.astype(F32),), ()
    out = _rows(name, fn, [(a.reshape(flat), flat[1], 0), (b.reshape(flat), flat[1], 0)], [], [(flat[1], BF16)], [],
                tm=tm, rs=2 * SUBLANES)[0]
    return out.reshape(shape)


def _sum_slots(name, stack, *, tm, rs):
    width = stack.shape[2]

    def fn(*slots):
        s0, s1, s2, s3 = (s.astype(F32) for s in slots)
        return (((s0 + s1) + s2) + s3,), ()
    return _rows(name, fn, [(stack, width, 0, k) for k in range(N_CHIPS)], [], [(width, F32)], [],
                 tm=tm, rs=rs)[0]


def _layernorm(v, w, b):
    mu = jnp.mean(v, axis=-1, keepdims=True)
    var = jnp.mean(jnp.square(v - mu), axis=-1, keepdims=True)
    return (v - mu) * lax.rsqrt(var + EPS) * w + b


def _gmlp_mask():
    t = lax.broadcasted_iota(jnp.int32, (GMLP_BLOCK, GMLP_BLOCK), 0) // CHUNK
    s = lax.broadcasted_iota(jnp.int32, (GMLP_BLOCK, GMLP_BLOCK), 1) // CHUNK
    return s <= t


GMLP_TM = 512


def _gmlp_fwd(name, za, ln_w, ln_b, ws, bs_col):
    nrow = za.shape[0]
    tm = GMLP_TM
    width = GMLP_GROUPS * GMLP_BLOCK

    def body(za_ref, lnw_ref, lnb_ref, ws_ref, bs_ref, o_ref, wm_ref):
        mask = _gmlp_mask()
        for g in range(GMLP_GROUPS):
            wm_ref[g] = jnp.where(mask, ws_ref[g], 0.0).astype(BF16)

        def block(n, carry):
            rows = pl.ds(pl.multiple_of(n * GMLP_BLOCK, GMLP_BLOCK), GMLP_BLOCK)
            for g in range(GMLP_GROUPS):
                cols = slice(g * GMLP_BLOCK, (g + 1) * GMLP_BLOCK)
                vcols = slice(width + g * GMLP_BLOCK, width + (g + 1) * GMLP_BLOCK)
                u = jax.nn.gelu(za_ref[rows, cols].astype(F32))
                v = jax.nn.gelu(za_ref[rows, vcols].astype(F32))
                vn = _layernorm(v, lnw_ref[g:g + 1, :], lnb_ref[g:g + 1, :])
                sv = _dot(wm_ref[g], vn) + bs_ref[g]
                o_ref[rows, cols] = (u * sv).astype(o_ref.dtype)
            return carry

        lax.fori_loop(0, tm // GMLP_BLOCK, block, 0)

    small = lambda a: pl.BlockSpec(a.shape, lambda i, nd=a.ndim: (0,) * nd)
    return pl.pallas_call(
        body, name=name, grid=(nrow // tm,),
        in_specs=[pl.BlockSpec((tm, 2 * width), lambda i: (i, 0)), small(ln_w), small(ln_b), small(ws), small(bs_col)],
        out_specs=pl.BlockSpec((tm, width), lambda i: (i, 0)),
        out_shape=jax.ShapeDtypeStruct((nrow, width), BF16),
        scratch_shapes=[pltpu.VMEM((GMLP_GROUPS, GMLP_BLOCK, GMLP_BLOCK), BF16)],
        compiler_params=_params(("arbitrary",)),
    )(za, ln_w, ln_b, ws, bs_col)


def _gmlp_bwd(name, za, dout, ln_w, ln_b, ws, bs_col):
    nrow = za.shape[0]
    tm = GMLP_TM
    width = GMLP_GROUPS * GMLP_BLOCK

    def body(za_ref, do_ref, lnw_ref, lnb_ref, ws_ref, bs_ref, dza_ref, dlnw_ref, dlnb_ref, dws_ref, dbs_ref, wm_ref):
        mask = _gmlp_mask()
        for g in range(GMLP_GROUPS):
            wm_ref[g] = jnp.where(mask, ws_ref[g], 0.0).astype(BF16)

        @pl.when(pl.program_id(0) == 0)
        def _():
            dlnw_ref[...] = jnp.zeros_like(dlnw_ref)
            dlnb_ref[...] = jnp.zeros_like(dlnb_ref)
            dws_ref[...] = jnp.zeros_like(dws_ref)
            dbs_ref[...] = jnp.zeros_like(dbs_ref)

        def block(n, carry):
            rows = pl.ds(pl.multiple_of(n * GMLP_BLOCK, GMLP_BLOCK), GMLP_BLOCK)
            for g in range(GMLP_GROUPS):
                cols = slice(g * GMLP_BLOCK, (g + 1) * GMLP_BLOCK)
                vcols = slice(width + g * GMLP_BLOCK, width + (g + 1) * GMLP_BLOCK)
                u, gelu_u_vjp = jax.vjp(jax.nn.gelu, za_ref[rows, cols].astype(F32))
                v, gelu_v_vjp = jax.vjp(jax.nn.gelu, za_ref[rows, vcols].astype(F32))
                vn, ln_vjp = jax.vjp(_layernorm, v, lnw_ref[g:g + 1, :], lnb_ref[g:g + 1, :])
                sv = _dot(wm_ref[g], vn) + bs_ref[g]
                d_o = do_ref[rows, cols].astype(F32)
                dsv = d_o * u
                d_wm = _dot(dsv, vn, 1, 1)
                dvn = _dot(wm_ref[g], dsv, 0, 0)
                dv, dlnw, dlnb = ln_vjp(dvn)
                dza_ref[rows, cols] = gelu_u_vjp(d_o * sv)[0].astype(dza_ref.dtype)
                dza_ref[rows, vcols] = gelu_v_vjp(dv)[0].astype(dza_ref.dtype)
                dlnw_ref[g:g + 1, :] += dlnw
                dlnb_ref[g:g + 1, :] += dlnb
                dws_ref[g] += jnp.where(mask, d_wm, 0.0)
                dbs_ref[g] += jnp.sum(dsv, axis=1, keepdims=True)
            return carry

        lax.fori_loop(0, tm // GMLP_BLOCK, block, 0)

    small = lambda a: pl.BlockSpec(a.shape, lambda i, nd=a.ndim: (0,) * nd)
    return pl.pallas_call(
        body, name=name, grid=(nrow // tm,),
        in_specs=[pl.BlockSpec((tm, 2 * width), lambda i: (i, 0)), pl.BlockSpec((tm, width), lambda i: (i, 0)),
                  small(ln_w), small(ln_b), small(ws), small(bs_col)],
        out_specs=[pl.BlockSpec((tm, 2 * width), lambda i: (i, 0)), small(ln_w), small(ln_b), small(ws), small(bs_col)],
        out_shape=[jax.ShapeDtypeStruct((nrow, 2 * width), BF16), jax.ShapeDtypeStruct(ln_w.shape, F32),
                   jax.ShapeDtypeStruct(ln_b.shape, F32), jax.ShapeDtypeStruct(ws.shape, F32),
                   jax.ShapeDtypeStruct(bs_col.shape, F32)],
        scratch_shapes=[pltpu.VMEM((GMLP_GROUPS, GMLP_BLOCK, GMLP_BLOCK), BF16)],
        compiler_params=_params(("arbitrary",)),
    )(za, dout, ln_w, ln_b, ws, bs_col)


CONV_TM = 256
CONV_RS = 32
HALO = 2 * SUBLANES


def _tap_rows(w_ref):
    return [w_ref[k:k + 1, :] for k in range(w_ref.shape[0])]


def _conv_rows(win, w, rs):
    taps = len(w)
    out = w[taps - 1] * win[HALO:, :]
    for k in range(taps - 1):
        back = taps - 1 - k
        out = out + w[k] * pltpu.roll(win, back, axis=0)[HALO:, :]
    return out


def _conv_t_rows(win, w, rs):
    taps = len(w)
    out = w[taps - 1] * win[:rs, :]
    for k in range(taps - 1):
        ahead = taps - 1 - k
        out = out + w[k] * pltpu.roll(win, rs + HALO - ahead, axis=0)[:rs, :]
    return out


def _conv_dw_rows(d, xwin, taps):
    rows = []
    for k in range(taps):
        back = taps - 1 - k
        xs = xwin[HALO:, :] if back == 0 else pltpu.roll(xwin, back, axis=0)[HALO:, :]
        rows.append(jnp.sum(d * xs, axis=0, keepdims=True))
    return rows


def _halo_specs(nrow, tm, tc):
    per = tm // HALO
    last = nrow // HALO - 1
    main = pl.BlockSpec((tm, tc), lambda j, i: (i, j))
    before = pl.BlockSpec((HALO, tc), lambda j, i: (jnp.maximum(i * per - 1, 0), j))
    after = pl.BlockSpec((HALO, tc), lambda j, i: (jnp.minimum((i + 1) * per, last), j))
    return main, before, after


def _col_spec(rows, tc):
    return pl.BlockSpec((rows, tc), lambda j, i: (0, j))


def _conv_fwd(name, x, w, b, *, tc):
    nrow, ncol = x.shape
    taps = w.shape[0]
    tm, rs = CONV_TM, CONV_RS
    main, before, _ = _halo_specs(nrow, tm, tc)

    def body(x_ref, xb_ref, w_ref, b_ref, o_ref):
        first = pl.program_id(1) == 0
        wv, bv = _tap_rows(w_ref), b_ref[...]

        def step(r, prev):
            sl = pl.ds(pl.multiple_of(r * rs, rs), rs)
            cur = x_ref[sl, :].astype(F32)
            o_ref[sl, :] = (_conv_rows(jnp.concatenate([prev, cur], axis=0), wv, rs) + bv).astype(o_ref.dtype)
            return cur[rs - HALO:, :]

        lax.fori_loop(0, tm // rs, step, jnp.where(first, 0.0, xb_ref[...].astype(F32)))

    return pl.pallas_call(
        body, name=name, grid=(ncol // tc, nrow // tm),
        in_specs=[main, before, _col_spec(taps, tc), _col_spec(1, tc)],
        out_specs=main, out_shape=jax.ShapeDtypeStruct((nrow, ncol), BF16),
        compiler_params=_params(("parallel", "arbitrary")),
    )(x, x, w, b)


def _conv_bwd(name, dpre, x, w, *, tc):
    nrow, ncol = x.shape
    taps = w.shape[0]
    tm, rs = CONV_TM, CONV_RS
    nsub = tm // rs
    main, before, after = _halo_specs(nrow, tm, tc)

    def body(d_ref, da_ref, x_ref, xb_ref, w_ref, dx_ref, dw_ref, db_ref):
        i = pl.program_id(1)
        first, last = i == 0, i == pl.num_programs(1) - 1
        wv = _tap_rows(w_ref)
        x_before = jnp.where(first, 0.0, xb_ref[...].astype(F32))

        @pl.when(first)
        def _():
            dw_ref[...] = jnp.zeros_like(dw_ref)
            db_ref[...] = jnp.zeros_like(db_ref)

        def step(q, carry):
            nxt, dw, db = carry
            r = nsub - 1 - q
            sl = pl.ds(pl.multiple_of(r * rs, rs), rs)
            cur = d_ref[sl, :].astype(F32)
            dx_ref[sl, :] = _conv_t_rows(jnp.concatenate([cur, nxt], axis=0), wv, rs).astype(dx_ref.dtype)
            inner = x_ref[pl.ds(pl.multiple_of(jnp.maximum(r * rs - HALO, 0), HALO), HALO), :].astype(F32)
            xwin = jnp.concatenate([jnp.where(r == 0, x_before, inner), x_ref[sl, :].astype(F32)], axis=0)
            dw = tuple(a + s for a, s in zip(dw, _conv_dw_rows(cur, xwin, taps)))
            return cur[:HALO, :], dw, db + _colsum(cur)

        zero_row = jnp.zeros((1, tc), F32)
        init = (jnp.where(last, 0.0, da_ref[...].astype(F32)), (zero_row,) * taps, zero_row)
        _, dw, db = lax.fori_loop(0, nsub, step, init)
        for k in range(taps):
            dw_ref[k:k + 1, :] += dw[k]
        db_ref[...] += db

    return pl.pallas_call(
        body, name=name, grid=(ncol // tc, nrow // tm),
        in_specs=[main, after, main, before, _col_spec(taps, tc)],
        out_specs=[main, _col_spec(taps, tc), _col_spec(1, tc)],
        out_shape=[jax.ShapeDtypeStruct((nrow, ncol), BF16), jax.ShapeDtypeStruct((taps, ncol), F32),
                   jax.ShapeDtypeStruct((1, ncol), F32)],
        compiler_params=_params(("parallel", "arbitrary")),
    )(dpre, dpre, x, x, w)


def _glu(gate, val):
    return jax.nn.silu(gate) * val


def _ffn_act_fwd(name, pg, pv, wg, wv, bg, bv, *, tc):
    nrow, ncol = pg.shape
    taps = wg.shape[0]
    tm, rs = CONV_TM, CONV_RS
    main, before, _ = _halo_specs(nrow, tm, tc)

    def body(pg_ref, pgb_ref, pv_ref, pvb_ref, wg_ref, wv_ref, bg_ref, bv_ref, g_ref, v_ref, a_ref):
        first = pl.program_id(1) == 0
        wgv, wvv, bgv, bvv = _tap_rows(wg_ref), _tap_rows(wv_ref), bg_ref[...], bv_ref[...]

        def step(r, carry):
            prev_g, prev_v = carry
            sl = pl.ds(pl.multiple_of(r * rs, rs), rs)
            cur_g, cur_v = pg_ref[sl, :].astype(F32), pv_ref[sl, :].astype(F32)
            gate = _conv_rows(jnp.concatenate([prev_g, cur_g], axis=0), wgv, rs) + bgv
            val = _conv_rows(jnp.concatenate([prev_v, cur_v], axis=0), wvv, rs) + bvv
            g_ref[sl, :] = gate.astype(g_ref.dtype)
            v_ref[sl, :] = val.astype(v_ref.dtype)
            a_ref[sl, :] = _glu(gate, val).astype(a_ref.dtype)
            return cur_g[rs - HALO:, :], cur_v[rs - HALO:, :]

        lax.fori_loop(0, tm // rs, step, (jnp.where(first, 0.0, pgb_ref[...].astype(F32)),
                                          jnp.where(first, 0.0, pvb_ref[...].astype(F32))))

    return pl.pallas_call(
        body, name=name, grid=(ncol // tc, nrow // tm),
        in_specs=[main, before, main, before, _col_spec(taps, tc), _col_spec(taps, tc), _col_spec(1, tc), _col_spec(1, tc)],
        out_specs=[main, main, main],
        out_shape=[jax.ShapeDtypeStruct((nrow, ncol), BF16)] * 3,
        compiler_params=_params(("parallel", "arbitrary")),
    )(pg, pg, pv, pv, wg, wv, bg, bv)


def _ffn_act_bwd(name, dact, gate, val):
    def fn(dv, gv, vv):
        _, vjp = jax.vjp(_glu, gv, vv)
        dg, dval = vjp(dv)
        return (dg, dval), ()
    width = dact.shape[1]
    return _rows(name, fn, [(dact, width, 0), (gate, width, 0), (val, width, 0)], [],
                 [(width, BF16), (width, BF16)], [], tm=256, rs=2 * SUBLANES)


SSD_TM = 256
SSD_CHUNKS = SSD_TM // CHUNK
X_OFF, B_OFF, C_OFF = 0, SSM_INNER, SSM_INNER + SSM_GROUPS * SSM_STATE
HP = SSM_HPG * SSM_HEAD_DIM


def _causal_tiled():
    row = lax.broadcasted_iota(jnp.int32, (CHUNK, HP), 0)
    src = lax.broadcasted_iota(jnp.int32, (CHUNK, HP), 1) & (CHUNK - 1)
    return src <= row


def _split2(v):
    hi = v.astype(BF16)
    return hi, (v - hi.astype(F32)).astype(BF16)


def _dot_exact(a, ind):
    hi, lo = (lax.dot_general(p, ind, (((1,), (0,)), ((), ())), preferred_element_type=F32) for p in _split2(a))
    return hi + lo


def _head_indicator():
    head = lax.broadcasted_iota(jnp.int32, (SSM_HEADS, SSM_INNER), 0)
    chan = lax.broadcasted_iota(jnp.int32, (SSM_HEADS, SSM_INNER), 1)
    return (chan // SSM_HEAD_DIM == head).astype(BF16)


def _chunk_decays(ci, dt_ref, ac_ref, ind, ax_ref, dtx_ref, eax_ref, eex_ref, tail_ref):
    rows = pl.ds(pl.multiple_of(ci * CHUNK, CHUNK), CHUNK)
    ax_ref[...] = _dot_exact(ac_ref[rows, :], ind)
    dtx_ref[...] = _dot_exact(dt_ref[rows, :], ind)
    eax_ref[...] = jnp.exp(ax_ref[...])
    eex_ref[...] = jnp.exp(ax_ref[CHUNK - 1:CHUNK, :] - ax_ref[...])
    tail = pl.ds(pl.multiple_of(ci * CHUNK + CHUNK - SUBLANES, SUBLANES), SUBLANES)
    tail_ref[...] = jnp.exp(ac_ref[tail, :])


def _group_decay(ci, g, ax_ref, af_ref, xbc_ref, causal):
    gcols = slice(g * HP, (g + 1) * HP)
    bm = xbc_ref[:, B_OFF + g * SSM_STATE:B_OFF + (g + 1) * SSM_STATE]
    cm = xbc_ref[:, C_OFF + g * SSM_STATE:C_OFF + (g + 1) * SSM_STATE]
    cb_tiled = _dot(cm, jnp.concatenate([bm] * SSM_HPG, axis=0), 1, 1)
    seg = ax_ref[:, gcols] - af_ref[ci, :, gcols]
    decay = jnp.where(causal, jnp.exp(jnp.where(causal, seg, 0.0)), 0.0)
    return bm, cm, cb_tiled * decay, decay


def _ssd_fwd(name, pre, dt, a_cum, a_flat, d_x, ind, shards):
    nrow = pre.shape[0]
    tm = SSD_TM
    nstep = nrow // tm
    ng = len(shards)

    def body(pre_ref, dt_ref, ac_ref, af_ref, dx_ref, ind_ref, *rest):
        shard_refs, (y_ref, st_ref), stack_refs = rest[:ng], rest[ng:ng + 2], rest[ng + 2:2 * ng + 2]
        (h_ref, xbc_ref, ax_ref, dtx_ref, eax_ref, eex_ref, m_ref, xd_ref, yd_ref, tail_ref,
         send_sems, recv_sems) = rest[2 * ng + 2:]
        step = pl.program_id(0)
        start, forward, finish = _gather_phases([s.shape[0] for s in shards], ng, shard_refs, stack_refs,
                                                send_sems, recv_sems)

        @pl.when(step == 0)
        def _():
            h_ref[...] = jnp.zeros_like(h_ref)
            start()

        @pl.when(step == nstep // 2)
        def _():
            forward()

        causal = _causal_tiled()
        ind = ind_ref[...]

        def chunk(ci, carry):
            rows = pl.ds(pl.multiple_of(ci * CHUNK, CHUNK), CHUNK)
            xbc_ref[...] = jax.nn.silu(pre_ref[rows, :].astype(F32))
            _chunk_decays(ci, dt_ref, ac_ref, ind, ax_ref, dtx_ref, eax_ref, eex_ref, tail_ref)
            st_ref[ci] = h_ref[...].astype(st_ref.dtype)
            for g in range(SSM_GROUPS):
                gcols = slice(g * HP, (g + 1) * HP)
                bm, cm, m_all, _ = _group_decay(ci, g, ax_ref, af_ref, xbc_ref, causal)
                m_ref[...] = m_all
                x_g = xbc_ref[:, gcols]
                xd = x_g * dtx_ref[:, gcols]
                xd_ref[...] = xd
                h_g = h_ref[gcols, :]
                for hh in range(SSM_HPG):
                    lc = slice(hh * SSM_HEAD_DIM, (hh + 1) * SSM_HEAD_DIM)
                    yd_ref[:, lc] = _dot(m_ref[:, lc], xd_ref[:, lc])
                y_ref[rows, gcols] = (yd_ref[...] + _dot(cm, h_g, 1, 1) * eax_ref[:, gcols]
                                      + dx_ref[:, gcols] * x_g)
                new = _dot(xd * eex_ref[:, gcols], bm, 0, 0)
                for hh in range(SSM_HPG):
                    h = g * SSM_HPG + hh
                    hrows = slice(h * SSM_HEAD_DIM, (h + 1) * SSM_HEAD_DIM)
                    lrows = slice(hh * SSM_HEAD_DIM, (hh + 1) * SSM_HEAD_DIM)
                    h_ref[hrows, :] = tail_ref[SUBLANES - 1:SUBLANES, h:h + 1] * h_ref[hrows, :] + new[lrows, :]
            return carry

        lax.fori_loop(0, SSD_CHUNKS, chunk, 0)

        @pl.when(step == nstep - 1)
        def _():
            finish()

    nchunk = nrow // CHUNK
    whole = lambda a: pl.BlockSpec(a.shape, lambda i, nd=a.ndim: (0,) * nd)
    hbm = pl.BlockSpec(memory_space=pl.ANY)
    wide = lambda: pltpu.VMEM((CHUNK, SSM_INNER), F32)
    group = lambda: pltpu.VMEM((CHUNK, HP), F32)
    res = pl.pallas_call(
        body, name=name, grid=(nstep,),
        in_specs=[pl.BlockSpec((tm, SSM_XBC), lambda i: (i, 0)), pl.BlockSpec((tm, SSM_HEADS), lambda i: (i, 0)),
                  pl.BlockSpec((tm, SSM_HEADS), lambda i: (i, 0)),
                  pl.BlockSpec((SSD_CHUNKS, 1, SSM_INNER), lambda i: (i, 0, 0)), whole(d_x), whole(ind)] + [hbm] * ng,
        out_specs=[pl.BlockSpec((tm, SSM_INNER), lambda i: (i, 0)),
                   pl.BlockSpec((SSD_CHUNKS, SSM_INNER, SSM_STATE), lambda i: (i, 0, 0))] + [hbm] * ng,
        out_shape=[jax.ShapeDtypeStruct((nrow, SSM_INNER), F32),
                   jax.ShapeDtypeStruct((nchunk, SSM_INNER, SSM_STATE), BF16)]
        + [jax.ShapeDtypeStruct((N_CHIPS,) + s.shape, s.dtype) for s in shards],
        scratch_shapes=[pltpu.VMEM((SSM_INNER, SSM_STATE), F32), pltpu.VMEM((CHUNK, SSM_XBC), F32),
                        wide(), wide(), wide(), wide(), group(), group(), group(),
                        pltpu.VMEM((SUBLANES, SSM_HEADS), F32)] + _exchange_scratch(ng, GATHER_SEMS),
        compiler_params=_params(("arbitrary",)),
    )(pre, dt, a_cum, a_flat, d_x, ind, *shards)
    return res[0], res[1], res[2:]


def _ssd_bwd(name, pre, dt, a_cum, a_flat, d_x, ind, ind_t, states, dy, pairs):
    nrow = pre.shape[0]
    tm = SSD_TM
    ntile = nrow // tm
    npair = len(pairs)

    def body(pre_ref, dt_ref, ac_ref, af_ref, dx_ref, ind_ref, indt_ref, st_ref, dy_ref, *rest):
        pair_refs = rest[:npair]
        dpre_ref, ddt_ref, da_ref, daf_ref, dd_ref = rest[npair:npair + 5]
        recv_refs = rest[npair + 5:2 * npair + 5]
        (dh_ref, xbc_ref, dxbc_ref, ax_ref, dtx_ref, eax_ref, eex_ref, red_ref,
         m_ref, l_ref, xd_ref, dm_ref, dxd_ref, fold_ref, hd_ref, tail_ref, send_sems, recv_sems) = rest[2 * npair + 5:]
        start, finish = _scatter_phases(pair_refs, recv_refs, send_sems, recv_sems)

        @pl.when(pl.program_id(0) == 0)
        def _():
            dh_ref[...] = jnp.zeros_like(dh_ref)
            dd_ref[...] = jnp.zeros_like(dd_ref)
            start()

        causal = _causal_tiled()
        ind, ind_t = ind_ref[...], indt_ref[...]
        is_last_row = lax.broadcasted_iota(jnp.int32, (CHUNK, 1), 0) == CHUNK - 1
        ones = jnp.ones((CHUNK, SSM_STATE), BF16)

        def chunk(k, ddx):
            ci = SSD_CHUNKS - 1 - k
            rows = pl.ds(pl.multiple_of(ci * CHUNK, CHUNK), CHUNK)
            pre_v = pre_ref[rows, :].astype(F32)
            xbc_ref[...] = jax.nn.silu(pre_v)
            _chunk_decays(ci, dt_ref, ac_ref, ind, ax_ref, dtx_ref, eax_ref, eex_ref, tail_ref)
            ddx_parts = []
            for g in range(SSM_GROUPS):
                gcols = slice(g * HP, (g + 1) * HP)
                bcols = slice(B_OFF + g * SSM_STATE, B_OFF + (g + 1) * SSM_STATE)
                ccols = slice(C_OFF + g * SSM_STATE, C_OFF + (g + 1) * SSM_STATE)
                bm, cm, m_all, decay = _group_decay(ci, g, ax_ref, af_ref, xbc_ref, causal)
                m_ref[...] = m_all
                l_ref[...] = decay
                x_g = xbc_ref[:, gcols]
                xd = x_g * dtx_ref[:, gcols]
                xd_ref[...] = xd
                h_g = st_ref[ci, gcols, :]
                dh_g = dh_ref[gcols, :]
                dy_g = dy_ref[rows, gcols]
                for hh in range(SSM_HPG):
                    h = g * SSM_HPG + hh
                    hcols = slice(h * SSM_HEAD_DIM, (h + 1) * SSM_HEAD_DIM)
                    lc = slice(hh * SSM_HEAD_DIM, (hh + 1) * SSM_HEAD_DIM)
                    dy_h = dy_ref[rows, hcols]
                    dm_ref[:, lc] = _dot(dy_h, xd_ref[:, lc], 1, 1)
                    dxd_ref[:, lc] = _dot(m_ref[:, lc], dy_h, 0, 0)
                ebdh = eex_ref[:, gcols] * _dot(bm, dh_g, 1, 1)
                dxd = dxd_ref[...] + ebdh
                dm = dm_ref[...]
                t = dm * l_ref[...]
                t128 = (t[:, 0:LANES] + t[:, LANES:2 * LANES]) + (t[:, 2 * LANES:3 * LANES] + t[:, 3 * LANES:])
                fold_ref[...] = t128 + pltpu.roll(t128, CHUNK, axis=1)
                dw_sum = fold_ref[:, 0:CHUNK]
                q = dm * m_ref[...]
                dyea = dy_g * eax_ref[:, gcols]
                red_ref[0:CHUNK, gcols] = q + dyea * _dot(cm, h_g, 1, 1)
                red_ref[CHUNK:2 * CHUNK, gcols] = xd * ebdh
                red_ref[2 * CHUNK:3 * CHUNK, gcols] = dxd * x_g
                daf_ref[ci, :, gcols] = -jnp.sum(q, axis=0, keepdims=True)
                ddx_parts.append(jnp.sum(dy_g * x_g, axis=0, keepdims=True))
                dxbc_ref[:, gcols] = dxd * dtx_ref[:, gcols] + dx_ref[:, gcols] * dy_g
                dxbc_ref[:, ccols] = _dot(dw_sum, bm) + _dot(dyea, h_g)
                dxbc_ref[:, bcols] = _dot(dw_sum, cm, 0, 0) + _dot(xd * eex_ref[:, gcols], dh_g)
                dh_new = _dot(dyea, cm, 0, 0)
                for hh in range(SSM_HPG):
                    h = g * SSM_HPG + hh
                    hrows = slice(h * SSM_HEAD_DIM, (h + 1) * SSM_HEAD_DIM)
                    lrows = slice(hh * SSM_HEAD_DIM, (hh + 1) * SSM_HEAD_DIM)
                    hd_ref[h:h + 1, :] = jnp.sum(st_ref[ci, hrows, :] * dh_ref[hrows, :], axis=0, keepdims=True)
                    dh_ref[hrows, :] = tail_ref[SUBLANES - 1:SUBLANES, h:h + 1] * dh_ref[hrows, :] + dh_new[lrows, :]
            sums = _dot_exact(red_ref[...], ind_t)
            ra, ts = sums[:CHUNK], sums[CHUNK:2 * CHUNK]
            hdh = sum(lax.dot_general(ones, p, (((1,), (1,)), ((), ())), preferred_element_type=F32)
                      for p in _split2(hd_ref[...]))
            da_last = jnp.sum(ts, axis=0, keepdims=True) + tail_ref[SUBLANES - 1:SUBLANES, :] * hdh
            da_ref[rows, :] = ra - ts + jnp.where(is_last_row, da_last, 0.0)
            ddt_ref[rows, :] = sums[2 * CHUNK:]
            sig = jax.nn.sigmoid(pre_v)
            dpre_ref[rows, :] = (dxbc_ref[...] * (sig * (1.0 + pre_v * (1.0 - sig)))).astype(dpre_ref.dtype)
            return ddx + jnp.concatenate(ddx_parts, axis=1)

        ddx = lax.fori_loop(0, SSD_CHUNKS, chunk, jnp.zeros((1, SSM_INNER), F32))
        dd_ref[...] += _dot_exact(jnp.broadcast_to(ddx, (SUBLANES, SSM_INNER)), ind_t)

        @pl.when(pl.program_id(0) == ntile - 1)
        def _():
            finish()

    rev = lambda i: ntile - 1 - i
    whole = lambda a: pl.BlockSpec(a.shape, lambda i, nd=a.ndim: (0,) * nd)
    hbm = pl.BlockSpec(memory_space=pl.ANY)
    wide = lambda: pltpu.VMEM((CHUNK, SSM_INNER), F32)
    group = lambda: pltpu.VMEM((CHUNK, HP), F32)
    res = pl.pallas_call(
        body, name=name, grid=(ntile,),
        in_specs=[pl.BlockSpec((tm, SSM_XBC), lambda i: (rev(i), 0)), pl.BlockSpec((tm, SSM_HEADS), lambda i: (rev(i), 0)),
                  pl.BlockSpec((tm, SSM_HEADS), lambda i: (rev(i), 0)),
                  pl.BlockSpec((SSD_CHUNKS, 1, SSM_INNER), lambda i: (rev(i), 0, 0)),
                  whole(d_x), whole(ind), whole(ind_t),
                  pl.BlockSpec((SSD_CHUNKS, SSM_INNER, SSM_STATE), lambda i: (rev(i), 0, 0)),
                  pl.BlockSpec((tm, SSM_INNER), lambda i: (rev(i), 0))] + [hbm] * npair,
        out_specs=[pl.BlockSpec((tm, SSM_XBC), lambda i: (rev(i), 0)), pl.BlockSpec((tm, SSM_HEADS), lambda i: (rev(i), 0)),
                   pl.BlockSpec((tm, SSM_HEADS), lambda i: (rev(i), 0)),
                   pl.BlockSpec((SSD_CHUNKS, 1, SSM_INNER), lambda i: (rev(i), 0, 0)),
                   pl.BlockSpec((SUBLANES, SSM_HEADS), lambda i: (0, 0))] + [hbm] * npair,
        out_shape=[jax.ShapeDtypeStruct((nrow, SSM_XBC), BF16), jax.ShapeDtypeStruct((nrow, SSM_HEADS), F32),
                   jax.ShapeDtypeStruct((nrow, SSM_HEADS), F32), jax.ShapeDtypeStruct((nrow // CHUNK, 1, SSM_INNER), F32),
                   jax.ShapeDtypeStruct((SUBLANES, SSM_HEADS), F32)]
        + [jax.ShapeDtypeStruct(p.shape, p.dtype) for p in pairs],
        scratch_shapes=[pltpu.VMEM((SSM_INNER, SSM_STATE), F32), pltpu.VMEM((CHUNK, SSM_XBC), F32),
                        pltpu.VMEM((CHUNK, SSM_XBC), F32), wide(), wide(), wide(), wide(),
                        pltpu.VMEM((3 * CHUNK, SSM_INNER), F32),
                        group(), group(), group(), group(), group(), pltpu.VMEM((CHUNK, LANES), F32),
                        pltpu.VMEM((SSM_HEADS, SSM_STATE), F32), pltpu.VMEM((SUBLANES, SSM_HEADS), F32)]
        + _exchange_scratch(npair, N_CHIPS - 1),
        compiler_params=_params(("arbitrary",)),
    )(pre, dt, a_cum, a_flat, d_x, ind, ind_t, states, dy, *pairs)
    return res[:5], res[5:]


LATE = ["w_proj_a", "w_proj_b", "w_out", "ffn_w_up", "ffn_w_down"]
HALF_TILES = {"w_in": 128, "w_proj_a": 128, "w_proj_b": 256, "w_out": 128, "ffn_w_up": 128, "ffn_w_down": 176}


def _late_weights(stacks, shards):
    pa, pb, out, up, down = [_own_slot(stack, own) for stack, own in zip(stacks, shards)]
    return {"w_proj_a": pa.reshape(-1, D_MODEL), "w_proj_b": pb.reshape(-1, D_MODEL), "w_out": out.reshape(-1, D_MODEL),
            "w_up_g": _columns_from_chips(up[:2]), "w_up_v": _columns_from_chips(up[2:]),
            "w_down": down.reshape(-1, D_MODEL)}


def _pair_reduce(tag, names, stacks):
    core = lax.axis_index("c")
    own_half = [_row_half(s, core, 1) for s in stacks]
    other_half = _swap_cores("pair_grads_" + tag, [_row_half(s, 1 - core, 1) for s in stacks])
    return [_pair_sum("pair_" + n, a, b, tm=HALF_TILES[n]) for n, a, b in zip(names, own_half, other_half)]


def _local_step(x, target, w, late_shards):
    w = dict(w)
    g = {}
    bs_col = w["gmlp_bs"].reshape(GMLP_GROUPS, GMLP_BLOCK, 1)
    b0, b1 = w["gate_bias"][0:1], w["gate_bias"][1:2]

    xn = _rms_fwd("mix_norm", x, w["mix_norm_w"])
    big = dict(bm=1024, bn=1024, bk=1024)
    act16 = dict(out_dtype=BF16, **big)
    gates = _mm("in_gates", xn, w["w_g"], **act16)
    za = _mm("in_gmlp", xn, w["w_za"], **act16)
    z = _mm("in_z", xn, w["w_z"], **act16)
    xbc = _mm("in_xbc", xn, w["w_xbc"], **act16)
    dt_raw = _mm("in_dt", xn, w["w_dt"], bm=1024, bn=SSM_HEADS, bk=1024)

    pre = _conv_fwd("ssm_conv_fwd", xbc, w["ssm_conv_w"], w["ssm_conv_b"], tc=1024)
    dt, a_cum = _dt_prep("dt_prep", dt_raw, w["ssm_dt_bias"], w["ssm_a_log"])
    a_flat = jnp.transpose(a_cum.reshape(-1, CHUNK, SSM_HEADS), (0, 2, 1)).reshape(-1, 1, SSM_INNER)
    d_x = jnp.repeat(w["ssm_d"], SSM_HEAD_DIM, axis=1)
    ind = _head_indicator()
    y_ssd, states, late_stacks = _ssd_fwd("ssd_fwd", pre, dt, a_cum, a_flat, d_x, ind, late_shards)
    w.update(_late_weights(late_stacks, late_shards))
    yb_pre = _gate_norm_fwd("gate_norm_fwd", y_ssd, z, w["ssm_norm_w"])
    y_b = _mm("proj_b", yb_pre, w["w_proj_b"], **act16)

    ya_pre = _gmlp_fwd("gmlp_fwd", za, w["gmlp_ln_w"], w["gmlp_ln_b"], w["gmlp_ws"], bs_col)
    y_a = _mm("proj_a", ya_pre, w["w_proj_a"], **act16)

    merged = _merge_fwd("merge_fwd", gates, y_a, y_b, b0, b1)
    h1 = _mm("out_proj", merged, w["w_out"], res=x, **big)

    hn = _rms_fwd("ffn_norm", h1, w["ffn_norm_w"])
    half = dict(bm=1024, bn=D_FF // 2, bk=1024, out_dtype=BF16)
    pg = _mm("ffn_up_gate", hn, w["w_up_g"], **half)
    pv = _mm("ffn_up_val", hn, w["w_up_v"], **half)
    cw, cb = w["ffn_conv_w"], w["ffn_conv_b"]
    gate, val, act = _ffn_act_fwd("ffn_act_fwd", pg, pv, cw[:, :D_FF], cw[:, D_FF:], cb[:, :D_FF], cb[:, D_FF:],
                                  tc=D_FF // 2)
    dh2, loss_part, g["final_norm_w"] = _down_loss("ffn_down_loss", act, w["w_down"], h1, target,
                                                   w["final_norm_w"].reshape(1, D_MODEL))

    dact = _mm("d_act", dh2, w["w_down"], tb=True, **half)
    wgrad = dict(ta=True, bk=min(2048, x.shape[0]), out_dtype=BF16)
    g["w_down"] = _mm("dw_down", act, dh2, bm=D_FF // 2, bn=1024, **wgrad)
    dgate, dval = _ffn_act_bwd("ffn_act_bwd", dact, gate, val)
    dpg, dcwg, dcbg = _conv_bwd("ffn_conv_bwd_gate", dgate, pg, cw[:, :D_FF], tc=D_FF // 2)
    dpv, dcwv, dcbv = _conv_bwd("ffn_conv_bwd_val", dval, pv, cw[:, D_FF:], tc=D_FF // 2)
    g["ffn_conv_w"] = jnp.concatenate([dcwg, dcwv], axis=1)
    g["ffn_conv_b"] = jnp.concatenate([dcbg, dcbv], axis=1)
    dhn = _mm_sum("d_hn", [(dpg, w["w_up_g"]), (dpv, w["w_up_v"])], bm=1024, bk=D_FF // 2)
    g["w_up_g"] = _mm("dw_up_gate", hn, dpg, bm=1024, bn=D_FF // 2, **wgrad)
    g["w_up_v"] = _mm("dw_up_val", hn, dpv, bm=1024, bn=D_FF // 2, **wgrad)
    dh1, g["ffn_norm_w"] = _rms_bwd("ffn_norm_bwd", h1, w["ffn_norm_w"], dhn, dh2)

    dmerged = _mm("d_merged", dh1, w["w_out"], tb=True, **act16)
    g["w_out"] = _mm("dw_out", merged, dh1, bm=1024, bn=1024, **wgrad)
    dgates, dya, dyb, db0, db1 = _merge_bwd("merge_bwd", gates, y_a, y_b, dmerged, b0, b1)
    g["gate_bias"] = jnp.concatenate([db0, db1], axis=0)

    dya_pre = _mm("d_ya_pre", dya, w["w_proj_a"], tb=True, **act16)
    g["w_proj_a"] = _mm("dw_proj_a", ya_pre, dya, bm=1024, bn=1024, **wgrad)
    dyb_pre = _mm("d_yb_pre", dyb, w["w_proj_b"], tb=True, **act16)
    g["w_proj_b"] = _mm("dw_proj_b", yb_pre, dyb, bm=1024, bn=1024, **wgrad)
    late_pairs = _pair_reduce("late", LATE, [
        g["w_proj_a"].reshape(N_CHIPS, -1, D_MODEL), g["w_proj_b"].reshape(N_CHIPS, -1, D_MODEL),
        g["w_out"].reshape(N_CHIPS, -1, D_MODEL),
        jnp.concatenate([_columns_to_chips(g["w_up_g"], 2), _columns_to_chips(g["w_up_v"], 2)], axis=0),
        g["w_down"].reshape(N_CHIPS, -1, D_MODEL)])

    dy_ssd, dz, g["ssm_norm_w"] = _gate_norm_bwd("gate_norm_bwd", y_ssd, z, dyb_pre, w["ssm_norm_w"])
    (dpre, ddt, da_tok, da_flat, dd), late_received = _ssd_bwd(
        "ssd_bwd", pre, dt, a_cum, a_flat, d_x, ind, ind.T, states, dy_ssd, late_pairs)
    g["ssm_d"] = dd[0:1]
    da_src = jnp.transpose(da_flat.reshape(-1, SSM_HEADS, CHUNK), (0, 2, 1)).reshape(-1, SSM_HEADS)
    ddt_raw, g["ssm_dt_bias"], g["ssm_a_log"] = _dt_bwd("dt_bwd", dt_raw, ddt, da_tok, da_src,
                                                         w["ssm_dt_bias"], w["ssm_a_log"])
    dxbc, g["ssm_conv_w"], g["ssm_conv_b"] = _conv_bwd("ssm_conv_bwd", dpre, xbc, w["ssm_conv_w"], tc=1024)

    dza, g["gmlp_ln_w"], g["gmlp_ln_b"], g["gmlp_ws"], dbs = _gmlp_bwd(
        "gmlp_bwd", za, dya_pre, w["gmlp_ln_w"], w["gmlp_ln_b"], w["gmlp_ws"], bs_col)
    g["gmlp_bs"] = dbs.reshape(GMLP_GROUPS, GMLP_BLOCK)

    dxn = _mm_sum("d_xn", [(dgates, w["w_g"]), (dza, w["w_za"]), (dz, w["w_z"]), (dxbc, w["w_xbc"]),
                           (ddt_raw, w["w_dt"])], bm=1024, bk=512)
    g["w_g"] = _mm("dw_gates", xn, dgates, bm=1024, bn=1024, **wgrad)
    g["w_za"] = _mm("dw_gmlp", xn, dza, bm=1024, bn=1024, **wgrad)
    g["w_z"] = _mm("dw_z", xn, dz, bm=1024, bn=1024, **wgrad)
    g["w_xbc"] = _mm("dw_xbc", xn, dxbc, bm=1024, bn=1024, **wgrad)
    g["w_dt"] = _mm("dw_dt", xn, ddt_raw, bm=1024, bn=SSM_HEADS, **wgrad)
    grad_x, g["mix_norm_w"] = _rms_bwd("mix_norm_bwd", x, w["mix_norm_w"], dxn, dh1)
    return loss_part, grad_x, g, late_pairs, late_received


def _position():
    return lax.axis_index("x"), lax.axis_index("y"), lax.axis_index("c")


def _own_slot(stack, own):
    chip = 2 * lax.axis_index("x") + lax.axis_index("y")
    return lax.dynamic_update_index_in_dim(stack, own, chip, axis=0)


def _scatter_phases(ins, outs, send_sems, recv_sems):
    n = len(ins)
    x, y, c = _position()
    me = 2 * x + y
    peers = [(1 - x, y), (x, 1 - y), (1 - x, 1 - y)]

    def copy(i, k, src_slot, dst_slot):
        px, py = peers[k]
        return pltpu.make_async_remote_copy(
            src_ref=ins[i].at[src_slot], dst_ref=outs[i].at[dst_slot],
            send_sem=send_sems.at[i, k], recv_sem=recv_sems.at[i, k],
            device_id=(px, py, c), device_id_type=MESH)

    def start():
        for i in range(n):
            for k, (px, py) in enumerate(peers):
                copy(i, k, 2 * px + py, me).start()

    def finish():
        for i in range(n):
            for k, (px, py) in enumerate(peers):
                copy(i, k, me, 2 * px + py).wait_recv()
        for i in range(n):
            for k, (px, py) in enumerate(peers):
                copy(i, k, 2 * px + py, me).wait_send()

    return start, finish


def _exchange_scratch(n, per_array):
    return [pltpu.SemaphoreType.DMA((n, per_array)), pltpu.SemaphoreType.DMA((n, per_array))]


def _scatter_chips(name, arrs):
    n = len(arrs)

    def body(*refs):
        start, finish = _scatter_phases(refs[:n], refs[n:2 * n], *refs[2 * n:])
        start()
        finish()

    hbm = pl.BlockSpec(memory_space=pl.ANY)
    return pl.pallas_call(
        body, name=name,
        in_specs=[hbm] * n, out_specs=[hbm] * n,
        out_shape=[jax.ShapeDtypeStruct(a.shape, a.dtype) for a in arrs],
        scratch_shapes=_exchange_scratch(n, N_CHIPS - 1),
        compiler_params=pltpu.CompilerParams(has_side_effects=True),
    )(*arrs)


def _half_rows(ref_rows, which):
    half = ref_rows // 2
    return pl.ds(pl.multiple_of(which * half, 2 * SUBLANES), half)


GATHER_SEMS = 2 * (N_CHIPS - 1)


def _gather_phases(nrows, ns, ins, outs, send_sems, recv_sems):
    n = len(ins)
    x, y, c = _position()
    me = 2 * x + y
    sibling = (x, y, 1 - c)
    chips = [(1 - x, y), (x, 1 - y), (1 - x, 1 - y)]

    def remote(i, k, src, dst, to):
        return pltpu.make_async_remote_copy(src_ref=src, dst_ref=dst, send_sem=send_sems.at[i, k],
                                            recv_sem=recv_sems.at[i, k], device_id=to, device_id_type=MESH)

    def over_ici(i, k):
        px, py = chips[k]
        rows = _half_rows(nrows[i], c) if i < ns else slice(None)
        return remote(i, k, ins[i].at[rows], outs[i].at[me, rows], (px, py, c))

    def landed(i, k, which):
        px, py = chips[k]
        return outs[i].at[2 * px + py, _half_rows(nrows[i], which)] if i < ns else outs[i].at[2 * px + py]

    def start():
        for i in range(n):
            for k in range(N_CHIPS - 1):
                over_ici(i, k).start()

    def forward():
        for i in range(n):
            for k in range(N_CHIPS - 1):
                piece = landed(i, k, c)
                remote(i, k, piece, piece, (*chips[k], c)).wait_recv()
                if i < ns:
                    remote(i, N_CHIPS - 1 + k, piece, piece, sibling).start()

    def finish():
        for i in range(ns):
            for k in range(N_CHIPS - 1):
                piece = landed(i, k, 1 - c)
                remote(i, N_CHIPS - 1 + k, piece, piece, sibling).wait_recv()
        for i in range(n):
            for k in range(N_CHIPS - 1):
                over_ici(i, k).wait_send()
                if i < ns:
                    piece = landed(i, k, c)
                    remote(i, N_CHIPS - 1 + k, piece, piece, sibling).wait_send()

    return start, forward, finish


def _gather_chips_split(name, split, whole):
    arrs = list(split) + list(whole)
    n = len(arrs)

    def body(*refs):
        phases = _gather_phases([a.shape[0] for a in arrs], len(split), refs[:n], refs[n:2 * n], *refs[2 * n:])
        for phase in phases:
            phase()

    hbm = pl.BlockSpec(memory_space=pl.ANY)
    return pl.pallas_call(
        body, name=name, in_specs=[hbm] * n, out_specs=[hbm] * n,
        out_shape=[jax.ShapeDtypeStruct((N_CHIPS,) + a.shape, a.dtype) for a in arrs],
        scratch_shapes=_exchange_scratch(n, GATHER_SEMS),
        compiler_params=pltpu.CompilerParams(has_side_effects=True),
    )(*arrs)


def _swap_cores(name, arrs):
    n = len(arrs)

    def body(*refs):
        ins, outs = refs[:n], refs[n:2 * n]
        send_sems, recv_sems = refs[2 * n:]
        x, y, c = _position()
        copies = [pltpu.make_async_remote_copy(src_ref=ins[i], dst_ref=outs[i], send_sem=send_sems.at[i],
                                               recv_sem=recv_sems.at[i], device_id=(x, y, 1 - c), device_id_type=MESH)
                  for i in range(n)]
        for cp in copies:
            cp.start()
        for cp in copies:
            cp.wait_recv()
        for cp in copies:
            cp.wait_send()

    hbm = pl.BlockSpec(memory_space=pl.ANY)
    return pl.pallas_call(
        body, name=name, in_specs=[hbm] * n, out_specs=[hbm] * n,
        out_shape=[jax.ShapeDtypeStruct(a.shape, a.dtype) for a in arrs],
        scratch_shapes=[pltpu.SemaphoreType.DMA((n,)), pltpu.SemaphoreType.DMA((n,))],
        compiler_params=pltpu.CompilerParams(has_side_effects=True),
    )(*arrs)


def _row_half(a, which, axis):
    half = a.shape[axis] // 2
    return lax.dynamic_slice_in_dim(a, which * half, half, axis=axis)


def _all_reduce(name, pack):
    def body(in_ref, out_ref, buf, send_sems, recv_sems):
        x, y, c = _position()
        me = 4 * x + 2 * y + c
        flips = [(dx, dy, dc) for dx in (0, 1) for dy in (0, 1) for dc in (0, 1) if (dx, dy, dc) != (0, 0, 0)]
        peers = [((1 - x) if dx else x, (1 - y) if dy else y, (1 - c) if dc else c) for dx, dy, dc in flips]
        buf[me] = in_ref[...]
        sends = []
        for k, peer in enumerate(peers):
            cp = pltpu.make_async_remote_copy(src_ref=in_ref, dst_ref=buf.at[me], send_sem=send_sems.at[k],
                                              recv_sem=recv_sems.at[k], device_id=peer, device_id_type=MESH)
            cp.start()
            sends.append(cp)
        for k, (px, py, pc) in enumerate(peers):
            pltpu.make_async_remote_copy(src_ref=in_ref, dst_ref=buf.at[4 * px + 2 * py + pc], send_sem=send_sems.at[k],
                                         recv_sem=recv_sems.at[k], device_id=(px, py, pc), device_id_type=MESH).wait_recv()
        total = buf[0]
        for j in range(1, N_DEV):
            total = total + buf[j]
        out_ref[...] = total
        for cp in sends:
            cp.wait_send()

    vmem = pl.BlockSpec(memory_space=pltpu.VMEM)
    return pl.pallas_call(
        body, name=name, in_specs=[vmem], out_specs=vmem,
        out_shape=jax.ShapeDtypeStruct(pack.shape, F32),
        scratch_shapes=[pltpu.VMEM((N_DEV,) + pack.shape, F32), pltpu.SemaphoreType.DMA((N_DEV - 1,)),
                        pltpu.SemaphoreType.DMA((N_DEV - 1,))],
        compiler_params=pltpu.CompilerParams(has_side_effects=True, vmem_limit_bytes=VMEM_LIMIT_V7X),
    )(pack)


def _pack(arrs):
    rows = [a.reshape(-1, LANES) for a in arrs]
    total = sum(r.shape[0] for r in rows)
    rows.append(jnp.zeros((-total % SUBLANES, LANES), F32))
    return jnp.concatenate(rows, axis=0)


def _unpack(pack, shapes):
    out, off = [], 0
    for s in shapes:
        nrow = 1
        for d in s:
            nrow *= d
        nrow //= LANES
        out.append(pack[off:off + nrow].reshape(s))
        off += nrow
    return out


SMALL = ["mix_norm_w", "gate_bias", "gmlp_ln_w", "gmlp_ln_b", "gmlp_ws", "gmlp_bs", "ssm_conv_w", "ssm_conv_b",
         "ssm_dt_bias", "ssm_a_log", "ssm_d", "ssm_norm_w", "ffn_norm_w", "ffn_conv_w", "ffn_conv_b", "final_norm_w"]
SMALL_SHARDED = ("gate_bias", "ssm_conv_w", "ffn_conv_w")
BIG = ["w_in", "w_proj_a", "w_proj_b", "w_out", "ffn_w_up", "ffn_w_down"]
WEIGHTS = ["mix_norm_w", "w_in", "gate_bias", "gmlp_ln_w", "gmlp_ln_b", "gmlp_ws", "gmlp_bs", "ssm_conv_w",
           "ssm_conv_b", "ssm_dt_bias", "ssm_a_log", "ssm_d", "ssm_norm_w", "w_proj_a", "w_proj_b", "w_out",
           "ffn_norm_w", "ffn_w_up", "ffn_conv_w", "ffn_conv_b", "ffn_w_down", "final_norm_w"]
IN_SPLITS = [0, 2048, 4096, 6144, 9216, 9248]


def _columns_from_chips(stack):
    return jnp.transpose(stack, (1, 0, 2)).reshape(stack.shape[1], -1)


def _columns_to_chips(full, parts=N_CHIPS):
    rows, cols = full.shape
    return jnp.transpose(full.reshape(rows, parts, cols // parts), (1, 0, 2))


def kernel(x, mix_norm_w, w_in, gate_bias, gmlp_ln_w, gmlp_ln_b, gmlp_ws, gmlp_bs, ssm_conv_w, ssm_conv_b, ssm_dt_bias, ssm_a_log, ssm_d, ssm_norm_w, w_proj_a, w_proj_b, w_out, ffn_norm_w, ffn_w_up, ffn_conv_w, ffn_conv_b, ffn_w_down, final_norm_w, loss_target, m_mix_norm_w, m_w_in, m_gate_bias, m_gmlp_ln_w, m_gmlp_ln_b, m_gmlp_ws, m_gmlp_bs, m_ssm_conv_w, m_ssm_conv_b, m_ssm_dt_bias, m_ssm_a_log, m_ssm_d, m_ssm_norm_w, m_w_proj_a, m_w_proj_b, m_w_out, m_ffn_norm_w, m_ffn_w_up, m_ffn_conv_w, m_ffn_conv_b, m_ffn_w_down, m_final_norm_w, v_mix_norm_w, v_w_in, v_gate_bias, v_gmlp_ln_w, v_gmlp_ln_b, v_gmlp_ws, v_gmlp_bs, v_ssm_conv_w, v_ssm_conv_b, v_ssm_dt_bias, v_ssm_a_log, v_ssm_d, v_ssm_norm_w, v_w_proj_a, v_w_proj_b, v_w_out, v_ffn_norm_w, v_ffn_w_up, v_ffn_conv_w, v_ffn_conv_b, v_ffn_w_down, v_final_norm_w):
    args = dict(locals())
    weights = {n: args[n] for n in WEIGHTS}
    moments_m = {n: args["m_" + n] for n in WEIGHTS}
    moments_v = {n: args["v_" + n] for n in WEIGHTS}
    chip = 2 * lax.axis_index("x") + lax.axis_index("y")

    shards = [weights["w_in"][0].astype(BF16)] + [weights[n][0] for n in SMALL_SHARDED]
    gathered = _gather_chips_split("gather_weights", shards[:1], shards[1:])
    w_in_s, gb_s, scw_s, fcw_s = [_own_slot(stack, own) for stack, own in zip(gathered, shards)]
    late_shards = [weights[n][0].astype(BF16) for n in LATE]
    w_in_full = _columns_from_chips(w_in_s)
    full = {"w_" + nm: w_in_full[:, IN_SPLITS[k]:IN_SPLITS[k + 1]] for k, nm in enumerate(["g", "za", "z", "xbc", "dt"])}
    full["gate_bias"] = _columns_from_chips(gb_s)
    full["ssm_conv_w"] = _columns_from_chips(scw_s)
    full["ffn_conv_w"] = _columns_from_chips(fcw_s)
    for n in SMALL:
        if n not in SMALL_SHARDED:
            full[n] = weights[n] if n == "final_norm_w" else weights[n][0]
    for n in ("mix_norm_w", "ffn_norm_w", "ssm_conv_b", "ssm_dt_bias", "ssm_a_log", "ssm_d", "ssm_norm_w", "ffn_conv_b"):
        full[n] = full[n].reshape(1, -1)

    loss_part, grad_x, g, late_pairs, late_received = _local_step(x[0], loss_target[0], full, late_shards)

    per_head = ["ssm_dt_bias", "ssm_a_log", "ssm_d"]
    rest = [n for n in SMALL if n not in per_head]
    head_row = jnp.concatenate([g[n] for n in per_head] + [jnp.zeros((1, LANES - 3 * SSM_HEADS), F32)], axis=1)
    pack = _pack([loss_part, head_row] + [g[n] for n in rest])
    reduced = _unpack(_all_reduce("reduce_small", pack), [(1, LANES), (1, LANES)] + [g[n].shape for n in rest])
    loss = reduced[0][0, 0]
    small_grads = {n: reduced[1][:, k * SSM_HEADS:(k + 1) * SSM_HEADS] for k, n in enumerate(per_head)}
    for n, r in zip(rest, reduced[2:]):
        if n in SMALL_SHARDED:
            width = weights[n].shape[2]
            r = lax.dynamic_slice_in_dim(r, chip * width, width, axis=1)
        small_grads[n] = r
    two_d = lambda a: a.reshape(-1, a.shape[-1])
    upd = _adamw_small("adamw_small", *[[two_d(d[n]) for n in SMALL]
                                        for d in (weights, small_grads, moments_m, moments_v)])
    small_out = [[small_grads[n] for n in SMALL]] + list(upd)
    small_out = [[a.reshape(weights[n].shape) for n, a in zip(SMALL, kind)] for kind in small_out]

    dw_in = jnp.concatenate([g["w_g"], g["w_za"], g["w_z"], g["w_xbc"], g["w_dt"]], axis=1)
    in_pairs = _pair_reduce("in", ["w_in"], [_columns_to_chips(dw_in)])
    pair = in_pairs + list(late_pairs)
    received = list(_scatter_chips("scatter_grads", in_pairs)) + list(late_received)
    received = [_own_slot(r, lax.dynamic_index_in_dim(p, chip, 0, keepdims=False)) for r, p in zip(received, pair)]
    halves = [_sum_slots("sum_" + n, r, tm=HALF_TILES[n], rs=2 * SUBLANES) for n, r in zip(BIG, received)]
    tiles = {"w_in": 128, "w_proj_a": 256, "w_proj_b": 256, "w_out": 256, "ffn_w_up": 128, "ffn_w_down": 176}
    core = lax.axis_index("c")
    other = _swap_cores("join_grads", halves)
    reduced = [jnp.concatenate([jnp.where(core == 0, a, b), jnp.where(core == 0, b, a)], axis=0)
               for a, b in zip(halves, other)]
    big_out = {}
    for n, grad in zip(BIG, reduced):
        big_out[n] = _adamw("adamw_" + n, weights[n][0], grad, moments_m[n][0], moments_v[n][0],
                            tm=tiles[n], rs=SUBLANES)

    per_kind = [[], [], [], []]
    for n in WEIGHTS:
        for kind in range(4):
            if n in big_out:
                per_kind[kind].append(big_out[n][kind].reshape(weights[n].shape))
            else:
                per_kind[kind].append(small_out[kind][SMALL.index(n)])
    return (loss, grad_x[None], *per_kind[0], *per_kind[1], *per_kind[2], *per_kind[3])
```

```python
import jax
import jax.numpy as jnp
from jax import lax
from jax.experimental import pallas as pl
from jax.experimental.pallas import tpu as pltpu

F32 = jnp.float32
BF16 = jnp.bfloat16
MESH = pl.DeviceIdType.MESH

EPS = 1e-5
D_MODEL = 1024
GMLP_BLOCK = 128
GMLP_GROUPS = 8
CHUNK = 64
SSM_INNER = 2048
SSM_HEADS = 32
SSM_HEAD_DIM = 64
SSM_GROUPS = 4
SSM_HPG = 8
SSM_STATE = 128
SSM_CONV = 4
SSM_XBC = 3072
D_FF = 2816
FFN_CONV = 3
N_CHIPS = 4
N_DEV = 8

ADAM_LR = 0.001
ADAM_B1 = 0.9
ADAM_B2 = 0.999
ADAM_EPS = 1e-08
ADAM_WD = 0.01
ADAM_STEP = 10

VMEM_LIMIT_V7X = 56 * 1024 * 1024
SUBLANES = 8
LANES = 128


def _params(sem=None):
    return pltpu.CompilerParams(dimension_semantics=sem, vmem_limit_bytes=VMEM_LIMIT_V7X)


def _dot(a, b, ca=1, cb=0):
    return lax.dot_general(a.astype(BF16), b.astype(BF16), (((ca,), (cb,)), ((), ())),
                           preferred_element_type=F32)


def _mm(name, a, b, *, ta=False, tb=False, out_dtype=F32, bm, bn, bk, res=None):
    m, k = (a.shape[1], a.shape[0]) if ta else a.shape
    k2, n = (b.shape[1], b.shape[0]) if tb else b.shape
    assert k == k2 and m % bm == 0 and n % bn == 0 and k % bk == 0, (name, a.shape, b.shape)
    nk = k // bk
    a_spec = (pl.BlockSpec((bk, bm), lambda i, j, kk: (kk, i)) if ta
              else pl.BlockSpec((bm, bk), lambda i, j, kk: (i, kk)))
    b_spec = (pl.BlockSpec((bn, bk), lambda i, j, kk: (j, kk)) if tb
              else pl.BlockSpec((bk, bn), lambda i, j, kk: (kk, j)))
    o_spec = pl.BlockSpec((bm, bn), lambda i, j, kk: (i, j))
    has_res = res is not None

    def body(*refs):
        a_ref, b_ref = refs[0], refs[1]
        r_ref = refs[2] if has_res else None
        o_ref = refs[3] if has_res else refs[2]
        p = _dot(a_ref[...], b_ref[...], 0 if ta else 1, 1 if tb else 0)

        def finish(total):
            if has_res:
                total = total + r_ref[...]
            o_ref[...] = total.astype(out_dtype)

        if nk == 1:
            finish(p)
        else:
            acc_ref = refs[-1]
            kk = pl.program_id(2)

            @pl.when(kk == 0)
            def _():
                acc_ref[...] = p

            @pl.when(kk > 0)
            def _():
                acc_ref[...] += p

            @pl.when(kk == nk - 1)
            def _():
                finish(acc_ref[...])

    return pl.pallas_call(
        body, name=name,
        grid=(m // bm, n // bn, nk),
        in_specs=[a_spec, b_spec] + ([o_spec] if has_res else []),
        out_specs=o_spec,
        out_shape=jax.ShapeDtypeStruct((m, n), out_dtype),
        scratch_shapes=[pltpu.VMEM((bm, bn), F32)] if nk > 1 else [],
        compiler_params=_params(("parallel", "parallel", "arbitrary")),
    )(*([a, b] + ([res] if has_res else [])))


def _mm_sum(name, pairs, *, bm, bk):
    m, n = pairs[0][0].shape[0], pairs[0][1].shape[0]
    steps, first = [], []
    for a, b in pairs:
        k = a.shape[1]
        assert a.shape[0] == m and b.shape == (n, k) and m % bm == 0 and (k % bk == 0 or k < bk), (name, a.shape, b.shape)
        first.append(sum(steps))
        steps.append(max(k // bk, 1))
    total = sum(steps)
    in_specs = []
    for (a, b), off, cnt in zip(pairs, first, steps):
        width = min(bk, a.shape[1])
        in_specs.append(pl.BlockSpec((bm, width), lambda i, kk, off=off, cnt=cnt: (i, jnp.clip(kk - off, 0, cnt - 1))))
        in_specs.append(pl.BlockSpec((n, width), lambda i, kk, off=off, cnt=cnt: (0, jnp.clip(kk - off, 0, cnt - 1))))

    def body(*refs):
        o_ref, acc_ref = refs[-2], refs[-1]
        kk = pl.program_id(1)
        for s, (off, cnt) in enumerate(zip(first, steps)):
            @pl.when((kk >= off) & (kk < off + cnt))
            def _(s=s, off=off):
                p = _dot(refs[2 * s][...], refs[2 * s + 1][...], 1, 1)
                if off == 0:
                    @pl.when(kk == 0)
                    def _():
                        acc_ref[...] = p

                    @pl.when(kk > 0)
                    def _():
                        acc_ref[...] += p
                else:
                    acc_ref[...] += p

        @pl.when(kk == total - 1)
        def _():
            o_ref[...] = acc_ref[...]

    return pl.pallas_call(
        body, name=name, grid=(m // bm, total),
        in_specs=in_specs, out_specs=pl.BlockSpec((bm, n), lambda i, kk: (i, 0)),
        out_shape=jax.ShapeDtypeStruct((m, n), F32),
        scratch_shapes=[pltpu.VMEM((bm, n), F32)],
        compiler_params=_params(("parallel", "arbitrary")),
    )(*[t for pair in pairs for t in pair])


def _rows(name, fn, ins, params, outs, accs, *, tm, rs, unroll=4):
    nrow = ins[0][0].shape[-2]
    while tm % (rs * unroll):
        unroll //= 2
    assert nrow % tm == 0 and tm % rs == 0, (name, nrow, tm, rs)
    n_in, n_p, n_out, n_acc = len(ins), len(params), len(outs), len(accs)
    in_specs = []
    for spec in ins:
        arr, width, cb = spec[:3]
        if len(spec) == 4:
            in_specs.append(pl.BlockSpec((None, tm, width), lambda i, cb=cb, lead=spec[3]: (lead, i, cb)))
        else:
            in_specs.append(pl.BlockSpec((tm, width), lambda i, cb=cb: (i, cb)))
    for p in params:
        in_specs.append(pl.BlockSpec(p.shape, lambda i, nd=p.ndim: (0,) * nd))
    out_specs = [pl.BlockSpec((tm, w), lambda i: (i, 0)) for w, _ in outs]
    out_specs += [pl.BlockSpec(s, lambda i: (0, 0)) for s in accs]
    out_shape = [jax.ShapeDtypeStruct((nrow, w), dt) for w, dt in outs]
    out_shape += [jax.ShapeDtypeStruct(s, F32) for s in accs]

    def body(*refs):
        in_refs = refs[:n_in]
        p_refs = refs[n_in:n_in + n_p]
        o_refs = refs[n_in + n_p:n_in + n_p + n_out]
        a_refs = refs[n_in + n_p + n_out:]
        pv = [p[...] for p in p_refs]

        if n_acc:
            @pl.when(pl.program_id(0) == 0)
            def _():
                for a_ref in a_refs:
                    a_ref[...] = jnp.zeros_like(a_ref)

        def step(r, carry):
            for u in range(unroll):
                sl = pl.ds(pl.multiple_of((r * unroll + u) * rs, rs), rs)
                vals = [ref[sl, :].astype(F32) for ref in in_refs]
                row_out, sums = fn(*vals, *pv)
                for o_ref, v in zip(o_refs, row_out):
                    o_ref[sl, :] = v.astype(o_ref.dtype)
                carry = tuple(c + s for c, s in zip(carry, sums))
            return carry

        init = tuple(jnp.zeros(s, F32) for s in accs)
        total = lax.fori_loop(0, tm // (rs * unroll), step, init)
        for a_ref, t in zip(a_refs, total):
            a_ref[...] += t

    res = pl.pallas_call(
        body, name=name, grid=(nrow // tm,),
        in_specs=in_specs, out_specs=out_specs, out_shape=out_shape,
        compiler_params=_params(("arbitrary",)),
    )(*([s[0] for s in ins] + list(params)))
    return res


def _rms(x, w):
    return x * lax.rsqrt(jnp.mean(x * x, axis=-1, keepdims=True) + EPS) * w


def _colsum(v):
    return jnp.sum(v, axis=0, keepdims=True)


def _rms_fwd(name, x, w):
    def fn(xv, wv):
        return (_rms(xv, wv),), ()
    return _rows(name, fn, [(x, D_MODEL, 0)], [w], [(D_MODEL, BF16)], [], tm=512, rs=16)[0]


def _rms_bwd(name, x, w, dy, dres):
    def fn(xv, dyv, drv, wv):
        _, vjp = jax.vjp(_rms, xv, wv)
        dx, dw = vjp(dyv)
        return (drv + dx,), (dw,)
    return _rows(name, fn, [(x, D_MODEL, 0), (dy, D_MODEL, 0), (dres, D_MODEL, 0)], [w],
                 [(D_MODEL, F32)], [(1, D_MODEL)], tm=512, rs=16)


def _final_loss(name, h, target, w):
    def fn(hv, tv, wv):
        y, vjp = jax.vjp(_rms, hv, wv)
        err = y - tv
        part = 0.5 * jnp.sum(jnp.mean(err * err, axis=-1, keepdims=True), axis=0, keepdims=True)
        dh, dw = vjp(err / D_MODEL)
        return (dh,), (jnp.broadcast_to(part, (1, LANES)), dw)
    return _rows(name, fn, [(h, D_MODEL, 0), (target, D_MODEL, 0)], [w],
                 [(D_MODEL, F32)], [(1, LANES), (1, D_MODEL)], tm=512, rs=16)


def _merge(ga, gb, ya, yb, b0, b1):
    return jax.nn.sigmoid(ga + b0) * ya + jax.nn.sigmoid(gb + b1) * yb


def _merge_fwd(name, g, ya, yb, b0, b1):
    def fn(ga, gb, yav, ybv, b0v, b1v):
        return (_merge(ga, gb, yav, ybv, b0v, b1v),), ()
    return _rows(name, fn, [(g, D_MODEL, 0), (g, D_MODEL, 1), (ya, D_MODEL, 0), (yb, D_MODEL, 0)],
                 [b0, b1], [(D_MODEL, BF16)], [], tm=512, rs=16)[0]


def _merge_bwd(name, g, ya, yb, dm, b0, b1):
    def fn(ga, gb, yav, ybv, dmv, b0v, b1v):
        _, vjp = jax.vjp(_merge, ga, gb, yav, ybv, b0v, b1v)
        dga, dgb, dya, dyb, db0, db1 = vjp(dmv)
        return (jnp.concatenate([dga, dgb], axis=1), dya, dyb), (db0, db1)
    return _rows(name, fn,
                 [(g, D_MODEL, 0), (g, D_MODEL, 1), (ya, D_MODEL, 0), (yb, D_MODEL, 0), (dm, D_MODEL, 0)],
                 [b0, b1], [(2 * D_MODEL, BF16), (D_MODEL, BF16), (D_MODEL, BF16)],
                 [(1, D_MODEL), (1, D_MODEL)], tm=512, rs=16)


GROUP_W = SSM_INNER // SSM_GROUPS


def _gate_norm_group(y, z, nw):
    v = y * jax.nn.silu(z)
    return v * lax.rsqrt(jnp.mean(v * v, axis=-1, keepdims=True) + EPS) * nw


def _gate_norm_fwd(name, y, z, nw):
    def fn(yv, zv, nwv):
        parts = [_gate_norm_group(yv[:, k * GROUP_W:(k + 1) * GROUP_W], zv[:, k * GROUP_W:(k + 1) * GROUP_W],
                                  nwv[:, k * GROUP_W:(k + 1) * GROUP_W]) for k in range(SSM_GROUPS)]
        return (jnp.concatenate(parts, axis=1),), ()
    return _rows(name, fn, [(y, SSM_INNER, 0), (z, SSM_INNER, 0)], [nw], [(SSM_INNER, BF16)], [],
                 tm=512, rs=16)[0]


def _gate_norm_bwd(name, y, z, dout, nw):
    def fn(yv, zv, dv, nwv):
        dys, dzs, dns = [], [], []
        for k in range(SSM_GROUPS):
            sl = slice(k * GROUP_W, (k + 1) * GROUP_W)
            _, vjp = jax.vjp(_gate_norm_group, yv[:, sl], zv[:, sl], nwv[:, sl])
            dy, dz, dn = vjp(dv[:, sl])
            dys.append(dy), dzs.append(dz), dns.append(dn)
        return (jnp.concatenate(dys, axis=1), jnp.concatenate(dzs, axis=1)), (jnp.concatenate(dns, axis=1),)
    return _rows(name, fn, [(y, SSM_INNER, 0), (z, SSM_INNER, 0), (dout, SSM_INNER, 0)], [nw],
                 [(SSM_INNER, F32), (SSM_INNER, BF16)], [(1, SSM_INNER)], tm=512, rs=16)


def _softplus(v):
    return jnp.maximum(v, 0.0) + jnp.log1p(jnp.exp(-jnp.abs(v)))


def _chunk_cumsum(v, reverse=False):
    row = lax.broadcasted_iota(jnp.int32, v.shape, 0)
    step = 1
    while step < CHUNK:
        if reverse:
            shifted = pltpu.roll(v, CHUNK - step, axis=0)
            v = v + jnp.where(row < CHUNK - step, shifted, 0.0)
        else:
            shifted = pltpu.roll(v, step, axis=0)
            v = v + jnp.where(row >= step, shifted, 0.0)
        step *= 2
    return v


def _dt_prep(name, dt_raw, dt_bias, a_log):
    def fn(rv, bv, alv):
        dt = _softplus(rv + bv)
        return (dt, _chunk_cumsum(dt * (-jnp.exp(alv)))), ()
    return _rows(name, fn, [(dt_raw, SSM_HEADS, 0)], [dt_bias, a_log],
                 [(SSM_HEADS, F32), (SSM_HEADS, F32)], [], tm=512, rs=CHUNK)


def _dt_bwd(name, dt_raw, ddt, da1, da2, dt_bias, a_log):
    def fn(rv, ddv, d1, d2, bv, alv):
        pre = rv + bv
        dt = _softplus(pre)
        a_neg = -jnp.exp(alv)
        back = _chunk_cumsum(d1 + d2, reverse=True)
        d_dt = ddv + back * a_neg
        d_raw = d_dt * jax.nn.sigmoid(pre)
        return (d_raw,), (_colsum(d_raw), _colsum(back * dt) * a_neg)
    return _rows(name, fn, [(dt_raw, SSM_HEADS, 0), (ddt, SSM_HEADS, 0), (da1, SSM_HEADS, 0), (da2, SSM_HEADS, 0)],
                 [dt_bias, a_log], [(SSM_HEADS, BF16)], [(1, SSM_HEADS), (1, SSM_HEADS)], tm=512, rs=CHUNK)


def _adamw_math(w, g, m, v):
    m_new = ADAM_B1 * m + (1.0 - ADAM_B1) * g
    v_new = ADAM_B2 * v + (1.0 - ADAM_B2) * jnp.square(g)
    m_hat = m_new / (1.0 - ADAM_B1 ** ADAM_STEP)
    v_hat = v_new / (1.0 - ADAM_B2 ** ADAM_STEP)
    delta = -ADAM_LR * (m_hat / (jnp.sqrt(v_hat) + ADAM_EPS) + ADAM_WD * w)
    return delta, m_new, v_new


def _adamw(name, w, g, m, v, *, tm, rs):
    width = w.shape[1]

    def fn(wv, mv, vv, gv):
        return (gv,) + _adamw_math(wv, gv, mv, vv), ()
    return _rows(name, fn, [(w, width, 0), (m, width, 0), (v, width, 0), (g, width, 0)],
                 [], [(width, F32)] * 4, [], tm=tm, rs=rs)


def _adamw_small(name, ws, gs, ms, vs):
    n = len(ws)

    def body(*refs):
        w_refs, g_refs, m_refs, v_refs = (refs[k * n:(k + 1) * n] for k in range(4))
        outs = refs[4 * n:]
        for i in range(n):
            res = _adamw_math(w_refs[i][...], g_refs[i][...], m_refs[i][...], v_refs[i][...])
            for k in range(3):
                outs[k * n + i][...] = res[k]

    vmem = pl.BlockSpec(memory_space=pltpu.VMEM)
    res = pl.pallas_call(
        body, name=name, in_specs=[vmem] * (4 * n), out_specs=[vmem] * (3 * n),
        out_shape=[jax.ShapeDtypeStruct(w.shape, F32) for w in ws] * 3,
        compiler_params=pltpu.CompilerParams(vmem_limit_bytes=VMEM_LIMIT_V7X),
    )(*ws, *gs, *ms, *vs)
    return res[:n], res[n:2 * n], res[2 * n:]


def _pair_sum(name, a, b, *, tm):
    shape = a.shape
    flat = (shape[0] * shape[1], shape[2])

    def fn(av, bv):
        return (av.astype(F32) + bv.astype(F32),), ()
    out = _rows(name, fn, [(a.reshape(flat), flat[1], 0), (b.reshape(flat), flat[1], 0)], [], [(flat[1], BF16)], [],
                tm=tm, rs=2 * SUBLANES)[0]
    return out.reshape(shape)


def _sum_slots(name, stack, *, tm, rs):
    width = stack.shape[2]

    def fn(*slots):
        s0, s1, s2, s3 = (s.astype(F32) for s in slots)
        return (((s0 + s1) + s2) + s3,), ()
    return _rows(name, fn, [(stack, width, 0, k) for k in range(N_CHIPS)], [], [(width, F32)], [],
                 tm=tm, rs=rs)[0]


def _layernorm(v, w, b):
    mu = jnp.mean(v, axis=-1, keepdims=True)
    var = jnp.mean(jnp.square(v - mu), axis=-1, keepdims=True)
    return (v - mu) * lax.rsqrt(var + EPS) * w + b


def _gmlp_mask():
    t = lax.broadcasted_iota(jnp.int32, (GMLP_BLOCK, GMLP_BLOCK), 0) // CHUNK
    s = lax.broadcasted_iota(jnp.int32, (GMLP_BLOCK, GMLP_BLOCK), 1) // CHUNK
    return s <= t


GMLP_TM = 512


def _gmlp_fwd(name, za, ln_w, ln_b, ws, bs_col):
    nrow = za.shape[0]
    tm = GMLP_TM
    width = GMLP_GROUPS * GMLP_BLOCK

    def body(za_ref, lnw_ref, lnb_ref, ws_ref, bs_ref, o_ref, wm_ref):
        mask = _gmlp_mask()
        for g in range(GMLP_GROUPS):
            wm_ref[g] = jnp.where(mask, ws_ref[g], 0.0).astype(BF16)

        def block(n, carry):
            rows = pl.ds(pl.multiple_of(n * GMLP_BLOCK, GMLP_BLOCK), GMLP_BLOCK)
            for g in range(GMLP_GROUPS):
                cols = slice(g * GMLP_BLOCK, (g + 1) * GMLP_BLOCK)
                vcols = slice(width + g * GMLP_BLOCK, width + (g + 1) * GMLP_BLOCK)
                u = jax.nn.gelu(za_ref[rows, cols].astype(F32))
                v = jax.nn.gelu(za_ref[rows, vcols].astype(F32))
                vn = _layernorm(v, lnw_ref[g:g + 1, :], lnb_ref[g:g + 1, :])
                sv = _dot(wm_ref[g], vn) + bs_ref[g]
                o_ref[rows, cols] = (u * sv).astype(o_ref.dtype)
            return carry

        lax.fori_loop(0, tm // GMLP_BLOCK, block, 0)

    small = lambda a: pl.BlockSpec(a.shape, lambda i, nd=a.ndim: (0,) * nd)
    return pl.pallas_call(
        body, name=name, grid=(nrow // tm,),
        in_specs=[pl.BlockSpec((tm, 2 * width), lambda i: (i, 0)), small(ln_w), small(ln_b), small(ws), small(bs_col)],
        out_specs=pl.BlockSpec((tm, width), lambda i: (i, 0)),
        out_shape=jax.ShapeDtypeStruct((nrow, width), BF16),
        scratch_shapes=[pltpu.VMEM((GMLP_GROUPS, GMLP_BLOCK, GMLP_BLOCK), BF16)],
        compiler_params=_params(("arbitrary",)),
    )(za, ln_w, ln_b, ws, bs_col)


def _gmlp_bwd(name, za, dout, ln_w, ln_b, ws, bs_col):
    nrow = za.shape[0]
    tm = GMLP_TM
    width = GMLP_GROUPS * GMLP_BLOCK

    def body(za_ref, do_ref, lnw_ref, lnb_ref, ws_ref, bs_ref, dza_ref, dlnw_ref, dlnb_ref, dws_ref, dbs_ref, wm_ref):
        mask = _gmlp_mask()
        for g in range(GMLP_GROUPS):
            wm_ref[g] = jnp.where(mask, ws_ref[g], 0.0).astype(BF16)

        @pl.when(pl.program_id(0) == 0)
        def _():
            dlnw_ref[...] = jnp.zeros_like(dlnw_ref)
            dlnb_ref[...] = jnp.zeros_like(dlnb_ref)
            dws_ref[...] = jnp.zeros_like(dws_ref)
            dbs_ref[...] = jnp.zeros_like(dbs_ref)

        def block(n, carry):
            rows = pl.ds(pl.multiple_of(n * GMLP_BLOCK, GMLP_BLOCK), GMLP_BLOCK)
            for g in range(GMLP_GROUPS):
                cols = slice(g * GMLP_BLOCK, (g + 1) * GMLP_BLOCK)
                vcols = slice(width + g * GMLP_BLOCK, width + (g + 1) * GMLP_BLOCK)
                u, gelu_u_vjp = jax.vjp(jax.nn.gelu, za_ref[rows, cols].astype(F32))
                v, gelu_v_vjp = jax.vjp(jax.nn.gelu, za_ref[rows, vcols].astype(F32))
                vn, ln_vjp = jax.vjp(_layernorm, v, lnw_ref[g:g + 1, :], lnb_ref[g:g + 1, :])
                sv = _dot(wm_ref[g], vn) + bs_ref[g]
                d_o = do_ref[rows, cols].astype(F32)
                dsv = d_o * u
                d_wm = _dot(dsv, vn, 1, 1)
                dvn = _dot(wm_ref[g], dsv, 0, 0)
                dv, dlnw, dlnb = ln_vjp(dvn)
                dza_ref[rows, cols] = gelu_u_vjp(d_o * sv)[0].astype(dza_ref.dtype)
                dza_ref[rows, vcols] = gelu_v_vjp(dv)[0].astype(dza_ref.dtype)
                dlnw_ref[g:g + 1, :] += dlnw
                dlnb_ref[g:g + 1, :] += dlnb
                dws_ref[g] += jnp.where(mask, d_wm, 0.0)
                dbs_ref[g] += jnp.sum(dsv, axis=1, keepdims=True)
            return carry

        lax.fori_loop(0, tm // GMLP_BLOCK, block, 0)

    small = lambda a: pl.BlockSpec(a.shape, lambda i, nd=a.ndim: (0,) * nd)
    return pl.pallas_call(
        body, name=name, grid=(nrow // tm,),
        in_specs=[pl.BlockSpec((tm, 2 * width), lambda i: (i, 0)), pl.BlockSpec((tm, width), lambda i: (i, 0)),
                  small(ln_w), small(ln_b), small(ws), small(bs_col)],
        out_specs=[pl.BlockSpec((tm, 2 * width), lambda i: (i, 0)), small(ln_w), small(ln_b), small(ws), small(bs_col)],
        out_shape=[jax.ShapeDtypeStruct((nrow, 2 * width), BF16), jax.ShapeDtypeStruct(ln_w.shape, F32),
                   jax.ShapeDtypeStruct(ln_b.shape, F32), jax.ShapeDtypeStruct(ws.shape, F32),
                   jax.ShapeDtypeStruct(bs_col.shape, F32)],
        scratch_shapes=[pltpu.VMEM((GMLP_GROUPS, GMLP_BLOCK, GMLP_BLOCK), BF16)],
        compiler_params=_params(("arbitrary",)),
    )(za, dout, ln_w, ln_b, ws, bs_col)


CONV_TM = 256
CONV_RS = 32
HALO = 2 * SUBLANES


def _tap_rows(w_ref):
    return [w_ref[k:k + 1, :] for k in range(w_ref.shape[0])]


def _conv_rows(win, w, rs):
    taps = len(w)
    out = w[taps - 1] * win[HALO:, :]
    for k in range(taps - 1):
        back = taps - 1 - k
        out = out + w[k] * pltpu.roll(win, back, axis=0)[HALO:, :]
    return out


def _conv_t_rows(win, w, rs):
    taps = len(w)
    out = w[taps - 1] * win[:rs, :]
    for k in range(taps - 1):
        ahead = taps - 1 - k
        out = out + w[k] * pltpu.roll(win, rs + HALO - ahead, axis=0)[:rs, :]
    return out


def _conv_dw_rows(d, xwin, taps):
    rows = []
    for k in range(taps):
        back = taps - 1 - k
        xs = xwin[HALO:, :] if back == 0 else pltpu.roll(xwin, back, axis=0)[HALO:, :]
        rows.append(jnp.sum(d * xs, axis=0, keepdims=True))
    return rows


def _halo_specs(nrow, tm, tc):
    per = tm // HALO
    last = nrow // HALO - 1
    main = pl.BlockSpec((tm, tc), lambda j, i: (i, j))
    before = pl.BlockSpec((HALO, tc), lambda j, i: (jnp.maximum(i * per - 1, 0), j))
    after = pl.BlockSpec((HALO, tc), lambda j, i: (jnp.minimum((i + 1) * per, last), j))
    return main, before, after


def _col_spec(rows, tc):
    return pl.BlockSpec((rows, tc), lambda j, i: (0, j))


def _conv_fwd(name, x, w, b, *, tc):
    nrow, ncol = x.shape
    taps = w.shape[0]
    tm, rs = CONV_TM, CONV_RS
    main, before, _ = _halo_specs(nrow, tm, tc)

    def body(x_ref, xb_ref, w_ref, b_ref, o_ref):
        first = pl.program_id(1) == 0
        wv, bv = _tap_rows(w_ref), b_ref[...]

        def step(r, prev):
            sl = pl.ds(pl.multiple_of(r * rs, rs), rs)
            cur = x_ref[sl, :].astype(F32)
            o_ref[sl, :] = (_conv_rows(jnp.concatenate([prev, cur], axis=0), wv, rs) + bv).astype(o_ref.dtype)
            return cur[rs - HALO:, :]

        lax.fori_loop(0, tm // rs, step, jnp.where(first, 0.0, xb_ref[...].astype(F32)))

    return pl.pallas_call(
        body, name=name, grid=(ncol // tc, nrow // tm),
        in_specs=[main, before, _col_spec(taps, tc), _col_spec(1, tc)],
        out_specs=main, out_shape=jax.ShapeDtypeStruct((nrow, ncol), BF16),
        compiler_params=_params(("parallel", "arbitrary")),
    )(x, x, w, b)


def _conv_bwd(name, dpre, x, w, *, tc):
    nrow, ncol = x.shape
    taps = w.shape[0]
    tm, rs = CONV_TM, CONV_RS
    nsub = tm // rs
    main, before, after = _halo_specs(nrow, tm, tc)

    def body(d_ref, da_ref, x_ref, xb_ref, w_ref, dx_ref, dw_ref, db_ref):
        i = pl.program_id(1)
        first, last = i == 0, i == pl.num_programs(1) - 1
        wv = _tap_rows(w_ref)
        x_before = jnp.where(first, 0.0, xb_ref[...].astype(F32))

        @pl.when(first)
        def _():
            dw_ref[...] = jnp.zeros_like(dw_ref)
            db_ref[...] = jnp.zeros_like(db_ref)

        def step(q, carry):
            nxt, dw, db = carry
            r = nsub - 1 - q
            sl = pl.ds(pl.multiple_of(r * rs, rs), rs)
            cur = d_ref[sl, :].astype(F32)
            dx_ref[sl, :] = _conv_t_rows(jnp.concatenate([cur, nxt], axis=0), wv, rs).astype(dx_ref.dtype)
            inner = x_ref[pl.ds(pl.multiple_of(jnp.maximum(r * rs - HALO, 0), HALO), HALO), :].astype(F32)
            xwin = jnp.concatenate([jnp.where(r == 0, x_before, inner), x_ref[sl, :].astype(F32)], axis=0)
            dw = tuple(a + s for a, s in zip(dw, _conv_dw_rows(cur, xwin, taps)))
            return cur[:HALO, :], dw, db + _colsum(cur)

        zero_row = jnp.zeros((1, tc), F32)
        init = (jnp.where(last, 0.0, da_ref[...].astype(F32)), (zero_row,) * taps, zero_row)
        _, dw, db = lax.fori_loop(0, nsub, step, init)
        for k in range(taps):
            dw_ref[k:k + 1, :] += dw[k]
        db_ref[...] += db

    return pl.pallas_call(
        body, name=name, grid=(ncol // tc, nrow // tm),
        in_specs=[main, after, main, before, _col_spec(taps, tc)],
        out_specs=[main, _col_spec(taps, tc), _col_spec(1, tc)],
        out_shape=[jax.ShapeDtypeStruct((nrow, ncol), BF16), jax.ShapeDtypeStruct((taps, ncol), F32),
                   jax.ShapeDtypeStruct((1, ncol), F32)],
        compiler_params=_params(("parallel", "arbitrary")),
    )(dpre, dpre, x, x, w)


def _glu(gate, val):
    return jax.nn.silu(gate) * val


def _ffn_act_fwd(name, pg, pv, wg, wv, bg, bv, *, tc):
    nrow, ncol = pg.shape
    taps = wg.shape[0]
    tm, rs = CONV_TM, CONV_RS
    main, before, _ = _halo_specs(nrow, tm, tc)

    def body(pg_ref, pgb_ref, pv_ref, pvb_ref, wg_ref, wv_ref, bg_ref, bv_ref, g_ref, v_ref, a_ref):
        first = pl.program_id(1) == 0
        wgv, wvv, bgv, bvv = _tap_rows(wg_ref), _tap_rows(wv_ref), bg_ref[...], bv_ref[...]

        def step(r, carry):
            prev_g, prev_v = carry
            sl = pl.ds(pl.multiple_of(r * rs, rs), rs)
            cur_g, cur_v = pg_ref[sl, :].astype(F32), pv_ref[sl, :].astype(F32)
            gate = _conv_rows(jnp.concatenate([prev_g, cur_g], axis=0), wgv, rs) + bgv
            val = _conv_rows(jnp.concatenate([prev_v, cur_v], axis=0), wvv, rs) + bvv
            g_ref[sl, :] = gate.astype(g_ref.dtype)
            v_ref[sl, :] = val.astype(v_ref.dtype)
            a_ref[sl, :] = _glu(gate, val).astype(a_ref.dtype)
            return cur_g[rs - HALO:, :], cur_v[rs - HALO:, :]

        lax.fori_loop(0, tm // rs, step, (jnp.where(first, 0.0, pgb_ref[...].astype(F32)),
                                          jnp.where(first, 0.0, pvb_ref[...].astype(F32))))

    return pl.pallas_call(
        body, name=name, grid=(ncol // tc, nrow // tm),
        in_specs=[main, before, main, before, _col_spec(taps, tc), _col_spec(taps, tc), _col_spec(1, tc), _col_spec(1, tc)],
        out_specs=[main, main, main],
        out_shape=[jax.ShapeDtypeStruct((nrow, ncol), BF16)] * 3,
        compiler_params=_params(("parallel", "arbitrary")),
    )(pg, pg, pv, pv, wg, wv, bg, bv)


def _ffn_act_bwd(name, dact, gate, val):
    def fn(dv, gv, vv):
        _, vjp = jax.vjp(_glu, gv, vv)
        dg, dval = vjp(dv)
        return (dg, dval), ()
    width = dact.shape[1]
    return _rows(name, fn, [(dact, width, 0), (gate, width, 0), (val, width, 0)], [],
                 [(width, BF16), (width, BF16)], [], tm=256, rs=2 * SUBLANES)


SSD_TM = 256
SSD_CHUNKS = SSD_TM // CHUNK
X_OFF, B_OFF, C_OFF = 0, SSM_INNER, SSM_INNER + SSM_GROUPS * SSM_STATE
HP = SSM_HPG * SSM_HEAD_DIM


def _causal_tiled():
    row = lax.broadcasted_iota(jnp.int32, (CHUNK, HP), 0)
    src = lax.broadcasted_iota(jnp.int32, (CHUNK, HP), 1) & (CHUNK - 1)
    return src <= row


def _split2(v):
    hi = v.astype(BF16)
    return hi, (v - hi.astype(F32)).astype(BF16)


def _dot_exact(a, ind):
    hi, lo = (lax.dot_general(p, ind, (((1,), (0,)), ((), ())), preferred_element_type=F32) for p in _split2(a))
    return hi + lo


def _head_indicator():
    head = lax.broadcasted_iota(jnp.int32, (SSM_HEADS, SSM_INNER), 0)
    chan = lax.broadcasted_iota(jnp.int32, (SSM_HEADS, SSM_INNER), 1)
    return (chan // SSM_HEAD_DIM == head).astype(BF16)


def _chunk_decays(ci, dt_ref, ac_ref, ind, ax_ref, dtx_ref, eax_ref, eex_ref, tail_ref):
    rows = pl.ds(pl.multiple_of(ci * CHUNK, CHUNK), CHUNK)
    ax_ref[...] = _dot_exact(ac_ref[rows, :], ind)
    dtx_ref[...] = _dot_exact(dt_ref[rows, :], ind)
    eax_ref[...] = jnp.exp(ax_ref[...])
    eex_ref[...] = jnp.exp(ax_ref[CHUNK - 1:CHUNK, :] - ax_ref[...])
    tail = pl.ds(pl.multiple_of(ci * CHUNK + CHUNK - SUBLANES, SUBLANES), SUBLANES)
    tail_ref[...] = jnp.exp(ac_ref[tail, :])


def _group_decay(ci, g, ax_ref, af_ref, xbc_ref, causal):
    gcols = slice(g * HP, (g + 1) * HP)
    bm = xbc_ref[:, B_OFF + g * SSM_STATE:B_OFF + (g + 1) * SSM_STATE]
    cm = xbc_ref[:, C_OFF + g * SSM_STATE:C_OFF + (g + 1) * SSM_STATE]
    cb_tiled = _dot(cm, jnp.concatenate([bm] * SSM_HPG, axis=0), 1, 1)
    seg = ax_ref[:, gcols] - af_ref[ci, :, gcols]
    decay = jnp.where(causal, jnp.exp(jnp.where(causal, seg, 0.0)), 0.0)
    return bm, cm, cb_tiled * decay, decay


def _ssd_fwd(name, pre, dt, a_cum, a_flat, d_x, ind, shards):
    nrow = pre.shape[0]
    tm = SSD_TM
    nstep = nrow // tm
    ng = len(shards)

    def body(pre_ref, dt_ref, ac_ref, af_ref, dx_ref, ind_ref, *rest):
        shard_refs, (y_ref, st_ref), stack_refs = rest[:ng], rest[ng:ng + 2], rest[ng + 2:2 * ng + 2]
        (h_ref, xbc_ref, ax_ref, dtx_ref, eax_ref, eex_ref, m_ref, xd_ref, yd_ref, tail_ref,
         send_sems, recv_sems) = rest[2 * ng + 2:]
        step = pl.program_id(0)
        start, forward, finish = _gather_phases([s.shape[0] for s in shards], ng, shard_refs, stack_refs,
                                                send_sems, recv_sems)

        @pl.when(step == 0)
        def _():
            h_ref[...] = jnp.zeros_like(h_ref)
            start()

        @pl.when(step == nstep // 2)
        def _():
            forward()

        causal = _causal_tiled()
        ind = ind_ref[...]

        def chunk(ci, carry):
            rows = pl.ds(pl.multiple_of(ci * CHUNK, CHUNK), CHUNK)
            xbc_ref[...] = jax.nn.silu(pre_ref[rows, :].astype(F32))
            _chunk_decays(ci, dt_ref, ac_ref, ind, ax_ref, dtx_ref, eax_ref, eex_ref, tail_ref)
            st_ref[ci] = h_ref[...].astype(st_ref.dtype)
            for g in range(SSM_GROUPS):
                gcols = slice(g * HP, (g + 1) * HP)
                bm, cm, m_all, _ = _group_decay(ci, g, ax_ref, af_ref, xbc_ref, causal)
                m_ref[...] = m_all
                x_g = xbc_ref[:, gcols]
                xd = x_g * dtx_ref[:, gcols]
                xd_ref[...] = xd
                h_g = h_ref[gcols, :]
                for hh in range(SSM_HPG):
                    lc = slice(hh * SSM_HEAD_DIM, (hh + 1) * SSM_HEAD_DIM)
                    yd_ref[:, lc] = _dot(m_ref[:, lc], xd_ref[:, lc])
                y_ref[rows, gcols] = (yd_ref[...] + _dot(cm, h_g, 1, 1) * eax_ref[:, gcols]
                                      + dx_ref[:, gcols] * x_g)
                new = _dot(xd * eex_ref[:, gcols], bm, 0, 0)
                for hh in range(SSM_HPG):
                    h = g * SSM_HPG + hh
                    hrows = slice(h * SSM_HEAD_DIM, (h + 1) * SSM_HEAD_DIM)
                    lrows = slice(hh * SSM_HEAD_DIM, (hh + 1) * SSM_HEAD_DIM)
                    h_ref[hrows, :] = tail_ref[SUBLANES - 1:SUBLANES, h:h + 1] * h_ref[hrows, :] + new[lrows, :]
            return carry

        lax.fori_loop(0, SSD_CHUNKS, chunk, 0)

        @pl.when(step == nstep - 1)
        def _():
            finish()

    nchunk = nrow // CHUNK
    whole = lambda a: pl.BlockSpec(a.shape, lambda i, nd=a.ndim: (0,) * nd)
    hbm = pl.BlockSpec(memory_space=pl.ANY)
    wide = lambda: pltpu.VMEM((CHUNK, SSM_INNER), F32)
    group = lambda: pltpu.VMEM((CHUNK, HP), F32)
    res = pl.pallas_call(
        body, name=name, grid=(nstep,),
        in_specs=[pl.BlockSpec((tm, SSM_XBC), lambda i: (i, 0)), pl.BlockSpec((tm, SSM_HEADS), lambda i: (i, 0)),
                  pl.BlockSpec((tm, SSM_HEADS), lambda i: (i, 0)),
                  pl.BlockSpec((SSD_CHUNKS, 1, SSM_INNER), lambda i: (i, 0, 0)), whole(d_x), whole(ind)] + [hbm] * ng,
        out_specs=[pl.BlockSpec((tm, SSM_INNER), lambda i: (i, 0)),
                   pl.BlockSpec((SSD_CHUNKS, SSM_INNER, SSM_STATE), lambda i: (i, 0, 0))] + [hbm] * ng,
        out_shape=[jax.ShapeDtypeStruct((nrow, SSM_INNER), F32),
                   jax.ShapeDtypeStruct((nchunk, SSM_INNER, SSM_STATE), BF16)]
        + [jax.ShapeDtypeStruct((N_CHIPS,) + s.shape, s.dtype) for s in shards],
        scratch_shapes=[pltpu.VMEM((SSM_INNER, SSM_STATE), F32), pltpu.VMEM((CHUNK, SSM_XBC), F32),
                        wide(), wide(), wide(), wide(), group(), group(), group(),
                        pltpu.VMEM((SUBLANES, SSM_HEADS), F32)] + _exchange_scratch(ng, GATHER_SEMS),
        compiler_params=_params(("arbitrary",)),
    )(pre, dt, a_cum, a_flat, d_x, ind, *shards)
    return res[0], res[1], res[2:]


def _ssd_bwd(name, pre, dt, a_cum, a_flat, d_x, ind, ind_t, states, dy, pairs):
    nrow = pre.shape[0]
    tm = SSD_TM
    ntile = nrow // tm
    npair = len(pairs)

    def body(pre_ref, dt_ref, ac_ref, af_ref, dx_ref, ind_ref, indt_ref, st_ref, dy_ref, *rest):
        pair_refs = rest[:npair]
        dpre_ref, ddt_ref, da_ref, daf_ref, dd_ref = rest[npair:npair + 5]
        recv_refs = rest[npair + 5:2 * npair + 5]
        (dh_ref, xbc_ref, dxbc_ref, ax_ref, dtx_ref, eax_ref, eex_ref, red_ref,
         m_ref, l_ref, xd_ref, dm_ref, dxd_ref, fold_ref, hd_ref, tail_ref, send_sems, recv_sems) = rest[2 * npair + 5:]
        start, finish = _scatter_phases(pair_refs, recv_refs, send_sems, recv_sems)

        @pl.when(pl.program_id(0) == 0)
        def _():
            dh_ref[...] = jnp.zeros_like(dh_ref)
            dd_ref[...] = jnp.zeros_like(dd_ref)
            start()

        causal = _causal_tiled()
        ind, ind_t = ind_ref[...], indt_ref[...]
        is_last_row = lax.broadcasted_iota(jnp.int32, (CHUNK, 1), 0) == CHUNK - 1
        ones = jnp.ones((CHUNK, SSM_STATE), BF16)

        def chunk(k, ddx):
            ci = SSD_CHUNKS - 1 - k
            rows = pl.ds(pl.multiple_of(ci * CHUNK, CHUNK), CHUNK)
            pre_v = pre_ref[rows, :].astype(F32)
            xbc_ref[...] = jax.nn.silu(pre_v)
            _chunk_decays(ci, dt_ref, ac_ref, ind, ax_ref, dtx_ref, eax_ref, eex_ref, tail_ref)
            ddx_parts = []
            for g in range(SSM_GROUPS):
                gcols = slice(g * HP, (g + 1) * HP)
                bcols = slice(B_OFF + g * SSM_STATE, B_OFF + (g + 1) * SSM_STATE)
                ccols = slice(C_OFF + g * SSM_STATE, C_OFF + (g + 1) * SSM_STATE)
                bm, cm, m_all, decay = _group_decay(ci, g, ax_ref, af_ref, xbc_ref, causal)
                m_ref[...] = m_all
                l_ref[...] = decay
                x_g = xbc_ref[:, gcols]
                xd = x_g * dtx_ref[:, gcols]
                xd_ref[...] = xd
                h_g = st_ref[ci, gcols, :]
                dh_g = dh_ref[gcols, :]
                dy_g = dy_ref[rows, gcols]
                for hh in range(SSM_HPG):
                    h = g * SSM_HPG + hh
                    hcols = slice(h * SSM_HEAD_DIM, (h + 1) * SSM_HEAD_DIM)
                    lc = slice(hh * SSM_HEAD_DIM, (hh + 1) * SSM_HEAD_DIM)
                    dy_h = dy_ref[rows, hcols]
                    dm_ref[:, lc] = _dot(dy_h, xd_ref[:, lc], 1, 1)
                    dxd_ref[:, lc] = _dot(m_ref[:, lc], dy_h, 0, 0)
                ebdh = eex_ref[:, gcols] * _dot(bm, dh_g, 1, 1)
                dxd = dxd_ref[...] + ebdh
                dm = dm_ref[...]
                t = dm * l_ref[...]
                t128 = (t[:, 0:LANES] + t[:, LANES:2 * LANES]) + (t[:, 2 * LANES:3 * LANES] + t[:, 3 * LANES:])
                fold_ref[...] = t128 + pltpu.roll(t128, CHUNK, axis=1)
                dw_sum = fold_ref[:, 0:CHUNK]
                q = dm * m_ref[...]
                dyea = dy_g * eax_ref[:, gcols]
                red_ref[0:CHUNK, gcols] = q + dyea * _dot(cm, h_g, 1, 1)
                red_ref[CHUNK:2 * CHUNK, gcols] = xd * ebdh
                red_ref[2 * CHUNK:3 * CHUNK, gcols] = dxd * x_g
                daf_ref[ci, :, gcols] = -jnp.sum(q, axis=0, keepdims=True)
                ddx_parts.append(jnp.sum(dy_g * x_g, axis=0, keepdims=True))
                dxbc_ref[:, gcols] = dxd * dtx_ref[:, gcols] + dx_ref[:, gcols] * dy_g
                dxbc_ref[:, ccols] = _dot(dw_sum, bm) + _dot(dyea, h_g)
                dxbc_ref[:, bcols] = _dot(dw_sum, cm, 0, 0) + _dot(xd * eex_ref[:, gcols], dh_g)
                dh_new = _dot(dyea, cm, 0, 0)
                for hh in range(SSM_HPG):
                    h = g * SSM_HPG + hh
                    hrows = slice(h * SSM_HEAD_DIM, (h + 1) * SSM_HEAD_DIM)
                    lrows = slice(hh * SSM_HEAD_DIM, (hh + 1) * SSM_HEAD_DIM)
                    hd_ref[h:h + 1, :] = jnp.sum(st_ref[ci, hrows, :] * dh_ref[hrows, :], axis=0, keepdims=True)
                    dh_ref[hrows, :] = tail_ref[SUBLANES - 1:SUBLANES, h:h + 1] * dh_ref[hrows, :] + dh_new[lrows, :]
            sums = _dot_exact(red_ref[...], ind_t)
            ra, ts = sums[:CHUNK], sums[CHUNK:2 * CHUNK]
            hdh = sum(lax.dot_general(ones, p, (((1,), (1,)), ((), ())), preferred_element_type=F32)
                      for p in _split2(hd_ref[...]))
            da_last = jnp.sum(ts, axis=0, keepdims=True) + tail_ref[SUBLANES - 1:SUBLANES, :] * hdh
            da_ref[rows, :] = ra - ts + jnp.where(is_last_row, da_last, 0.0)
            ddt_ref[rows, :] = sums[2 * CHUNK:]
            sig = jax.nn.sigmoid(pre_v)
            dpre_ref[rows, :] = (dxbc_ref[...] * (sig * (1.0 + pre_v * (1.0 - sig)))).astype(dpre_ref.dtype)
            return ddx + jnp.concatenate(ddx_parts, axis=1)

        ddx = lax.fori_loop(0, SSD_CHUNKS, chunk, jnp.zeros((1, SSM_INNER), F32))
        dd_ref[...] += _dot_exact(jnp.broadcast_to(ddx, (SUBLANES, SSM_INNER)), ind_t)

        @pl.when(pl.program_id(0) == ntile - 1)
        def _():
            finish()

    rev = lambda i: ntile - 1 - i
    whole = lambda a: pl.BlockSpec(a.shape, lambda i, nd=a.ndim: (0,) * nd)
    hbm = pl.BlockSpec(memory_space=pl.ANY)
    wide = lambda: pltpu.VMEM((CHUNK, SSM_INNER), F32)
    group = lambda: pltpu.VMEM((CHUNK, HP), F32)
    res = pl.pallas_call(
        body, name=name, grid=(ntile,),
        in_specs=[pl.BlockSpec((tm, SSM_XBC), lambda i: (rev(i), 0)), pl.BlockSpec((tm, SSM_HEADS), lambda i: (rev(i), 0)),
                  pl.BlockSpec((tm, SSM_HEADS), lambda i: (rev(i), 0)),
                  pl.BlockSpec((SSD_CHUNKS, 1, SSM_INNER), lambda i: (rev(i), 0, 0)),
                  whole(d_x), whole(ind), whole(ind_t),
                  pl.BlockSpec((SSD_CHUNKS, SSM_INNER, SSM_STATE), lambda i: (rev(i), 0, 0)),
                  pl.BlockSpec((tm, SSM_INNER), lambda i: (rev(i), 0))] + [hbm] * npair,
        out_specs=[pl.BlockSpec((tm, SSM_XBC), lambda i: (rev(i), 0)), pl.BlockSpec((tm, SSM_HEADS), lambda i: (rev(i), 0)),
                   pl.BlockSpec((tm, SSM_HEADS), lambda i: (rev(i), 0)),
                   pl.BlockSpec((SSD_CHUNKS, 1, SSM_INNER), lambda i: (rev(i), 0, 0)),
                   pl.BlockSpec((SUBLANES, SSM_HEADS), lambda i: (0, 0))] + [hbm] * npair,
        out_shape=[jax.ShapeDtypeStruct((nrow, SSM_XBC), BF16), jax.ShapeDtypeStruct((nrow, SSM_HEADS), F32),
                   jax.ShapeDtypeStruct((nrow, SSM_HEADS), F32), jax.ShapeDtypeStruct((nrow // CHUNK, 1, SSM_INNER), F32),
                   jax.ShapeDtypeStruct((SUBLANES, SSM_HEADS), F32)]
        + [jax.ShapeDtypeStruct(p.shape, p.dtype) for p in pairs],
        scratch_shapes=[pltpu.VMEM((SSM_INNER, SSM_STATE), F32), pltpu.VMEM((CHUNK, SSM_XBC), F32),
                        pltpu.VMEM((CHUNK, SSM_XBC), F32), wide(), wide(), wide(), wide(),
                        pltpu.VMEM((3 * CHUNK, SSM_INNER), F32),
                        group(), group(), group(), group(), group(), pltpu.VMEM((CHUNK, LANES), F32),
                        pltpu.VMEM((SSM_HEADS, SSM_STATE), F32), pltpu.VMEM((SUBLANES, SSM_HEADS), F32)]
        + _exchange_scratch(npair, N_CHIPS - 1),
        compiler_params=_params(("arbitrary",)),
    )(pre, dt, a_cum, a_flat, d_x, ind, ind_t, states, dy, *pairs)
    return res[:5], res[5:]


LATE = ["w_proj_a", "w_proj_b", "w_out", "ffn_w_up", "ffn_w_down"]
HALF_TILES = {"w_in": 128, "w_proj_a": 128, "w_proj_b": 256, "w_out": 128, "ffn_w_up": 128, "ffn_w_down": 176}


def _late_weights(stacks, shards):
    pa, pb, out, up, down = [_own_slot(stack, own) for stack, own in zip(stacks, shards)]
    return {"w_proj_a": pa.reshape(-1, D_MODEL), "w_proj_b": pb.reshape(-1, D_MODEL), "w_out": out.reshape(-1, D_MODEL),
            "w_up_g": _columns_from_chips(up[:2]), "w_up_v": _columns_from_chips(up[2:]),
            "w_down": down.reshape(-1, D_MODEL)}


def _pair_reduce(tag, names, stacks):
    core = lax.axis_index("c")
    own_half = [_row_half(s, core, 1) for s in stacks]
    other_half = _swap_cores("pair_grads_" + tag, [_row_half(s, 1 - core, 1) for s in stacks])
    return [_pair_sum("pair_" + n, a, b, tm=HALF_TILES[n]) for n, a, b in zip(names, own_half, other_half)]


def _local_step(x, target, w, late_shards):
    w = dict(w)
    g = {}
    bs_col = w["gmlp_bs"].reshape(GMLP_GROUPS, GMLP_BLOCK, 1)
    b0, b1 = w["gate_bias"][0:1], w["gate_bias"][1:2]

    xn = _rms_fwd("mix_norm", x, w["mix_norm_w"])
    big = dict(bm=1024, bn=1024, bk=1024)
    act16 = dict(out_dtype=BF16, **big)
    gates = _mm("in_gates", xn, w["w_g"], **act16)
    za = _mm("in_gmlp", xn, w["w_za"], **act16)
    z = _mm("in_z", xn, w["w_z"], **act16)
    xbc = _mm("in_xbc", xn, w["w_xbc"], **act16)
    dt_raw = _mm("in_dt", xn, w["w_dt"], bm=1024, bn=SSM_HEADS, bk=1024)

    pre = _conv_fwd("ssm_conv_fwd", xbc, w["ssm_conv_w"], w["ssm_conv_b"], tc=1024)
    dt, a_cum = _dt_prep("dt_prep", dt_raw, w["ssm_dt_bias"], w["ssm_a_log"])
    a_flat = jnp.transpose(a_cum.reshape(-1, CHUNK, SSM_HEADS), (0, 2, 1)).reshape(-1, 1, SSM_INNER)
    d_x = jnp.repeat(w["ssm_d"], SSM_HEAD_DIM, axis=1)
    ind = _head_indicator()
    y_ssd, states, late_stacks = _ssd_fwd("ssd_fwd", pre, dt, a_cum, a_flat, d_x, ind, late_shards)
    w.update(_late_weights(late_stacks, late_shards))
    yb_pre = _gate_norm_fwd("gate_norm_fwd", y_ssd, z, w["ssm_norm_w"])
    y_b = _mm("proj_b", yb_pre, w["w_proj_b"], bm=1024, bn=1024, bk=SSM_INNER, out_dtype=BF16)

    ya_pre = _gmlp_fwd("gmlp_fwd", za, w["gmlp_ln_w"], w["gmlp_ln_b"], w["gmlp_ws"], bs_col)
    y_a = _mm("proj_a", ya_pre, w["w_proj_a"], **act16)

    merged = _merge_fwd("merge_fwd", gates, y_a, y_b, b0, b1)
    h1 = _mm("out_proj", merged, w["w_out"], res=x, **big)

    hn = _rms_fwd("ffn_norm", h1, w["ffn_norm_w"])
    half = dict(bm=1024, bn=D_FF // 2, bk=1024, out_dtype=BF16)
    pg = _mm("ffn_up_gate", hn, w["w_up_g"], **half)
    pv = _mm("ffn_up_val", hn, w["w_up_v"], **half)
    cw, cb = w["ffn_conv_w"], w["ffn_conv_b"]
    gate, val, act = _ffn_act_fwd("ffn_act_fwd", pg, pv, cw[:, :D_FF], cw[:, D_FF:], cb[:, :D_FF], cb[:, D_FF:],
                                  tc=D_FF // 2)
    h2 = _mm("ffn_down", act, w["w_down"], res=h1, bm=1024, bn=1024, bk=D_FF // 2)

    dh2, loss_part, g["final_norm_w"] = _final_loss("final_loss", h2, target, w["final_norm_w"].reshape(1, D_MODEL))

    dact = _mm("d_act", dh2, w["w_down"], tb=True, **half)
    wgrad = dict(ta=True, bk=min(2048, x.shape[0]), out_dtype=BF16)
    g["w_down"] = _mm("dw_down", act, dh2, bm=D_FF // 2, bn=1024, **wgrad)
    dgate, dval = _ffn_act_bwd("ffn_act_bwd", dact, gate, val)
    dpg, dcwg, dcbg = _conv_bwd("ffn_conv_bwd_gate", dgate, pg, cw[:, :D_FF], tc=D_FF // 2)
    dpv, dcwv, dcbv = _conv_bwd("ffn_conv_bwd_val", dval, pv, cw[:, D_FF:], tc=D_FF // 2)
    g["ffn_conv_w"] = jnp.concatenate([dcwg, dcwv], axis=1)
    g["ffn_conv_b"] = jnp.concatenate([dcbg, dcbv], axis=1)
    dhn = _mm_sum("d_hn", [(dpg, w["w_up_g"]), (dpv, w["w_up_v"])], bm=1024, bk=D_FF // 2)
    g["w_up_g"] = _mm("dw_up_gate", hn, dpg, bm=1024, bn=D_FF // 2, **wgrad)
    g["w_up_v"] = _mm("dw_up_val", hn, dpv, bm=1024, bn=D_FF // 2, **wgrad)
    dh1, g["ffn_norm_w"] = _rms_bwd("ffn_norm_bwd", h1, w["ffn_norm_w"], dhn, dh2)

    dmerged = _mm("d_merged", dh1, w["w_out"], tb=True, **act16)
    g["w_out"] = _mm("dw_out", merged, dh1, bm=1024, bn=1024, **wgrad)
    dgates, dya, dyb, db0, db1 = _merge_bwd("merge_bwd", gates, y_a, y_b, dmerged, b0, b1)
    g["gate_bias"] = jnp.concatenate([db0, db1], axis=0)

    dya_pre = _mm("d_ya_pre", dya, w["w_proj_a"], tb=True, **act16)
    g["w_proj_a"] = _mm("dw_proj_a", ya_pre, dya, bm=1024, bn=1024, **wgrad)
    dyb_pre = _mm("d_yb_pre", dyb, w["w_proj_b"], tb=True, **act16)
    g["w_proj_b"] = _mm("dw_proj_b", yb_pre, dyb, bm=1024, bn=1024, **wgrad)
    late_pairs = _pair_reduce("late", LATE, [
        g["w_proj_a"].reshape(N_CHIPS, -1, D_MODEL), g["w_proj_b"].reshape(N_CHIPS, -1, D_MODEL),
        g["w_out"].reshape(N_CHIPS, -1, D_MODEL),
        jnp.concatenate([_columns_to_chips(g["w_up_g"], 2), _columns_to_chips(g["w_up_v"], 2)], axis=0),
        g["w_down"].reshape(N_CHIPS, -1, D_MODEL)])

    dy_ssd, dz, g["ssm_norm_w"] = _gate_norm_bwd("gate_norm_bwd", y_ssd, z, dyb_pre, w["ssm_norm_w"])
    (dpre, ddt, da_tok, da_flat, dd), late_received = _ssd_bwd(
        "ssd_bwd", pre, dt, a_cum, a_flat, d_x, ind, ind.T, states, dy_ssd, late_pairs)
    g["ssm_d"] = dd[0:1]
    da_src = jnp.transpose(da_flat.reshape(-1, SSM_HEADS, CHUNK), (0, 2, 1)).reshape(-1, SSM_HEADS)
    ddt_raw, g["ssm_dt_bias"], g["ssm_a_log"] = _dt_bwd("dt_bwd", dt_raw, ddt, da_tok, da_src,
                                                         w["ssm_dt_bias"], w["ssm_a_log"])
    dxbc, g["ssm_conv_w"], g["ssm_conv_b"] = _conv_bwd("ssm_conv_bwd", dpre, xbc, w["ssm_conv_w"], tc=1024)

    dza, g["gmlp_ln_w"], g["gmlp_ln_b"], g["gmlp_ws"], dbs = _gmlp_bwd(
        "gmlp_bwd", za, dya_pre, w["gmlp_ln_w"], w["gmlp_ln_b"], w["gmlp_ws"], bs_col)
    g["gmlp_bs"] = dbs.reshape(GMLP_GROUPS, GMLP_BLOCK)

    dxn = _mm_sum("d_xn", [(dgates, w["w_g"]), (dza, w["w_za"]), (dz, w["w_z"]), (dxbc, w["w_xbc"]),
                           (ddt_raw, w["w_dt"])], bm=1024, bk=1024)
    g["w_g"] = _mm("dw_gates", xn, dgates, bm=1024, bn=1024, **wgrad)
    g["w_za"] = _mm("dw_gmlp", xn, dza, bm=1024, bn=1024, **wgrad)
    g["w_z"] = _mm("dw_z", xn, dz, bm=1024, bn=1024, **wgrad)
    g["w_xbc"] = _mm("dw_xbc", xn, dxbc, bm=1024, bn=1024, **wgrad)
    g["w_dt"] = _mm("dw_dt", xn, ddt_raw, bm=1024, bn=SSM_HEADS, **wgrad)
    grad_x, g["mix_norm_w"] = _rms_bwd("mix_norm_bwd", x, w["mix_norm_w"], dxn, dh1)
    return loss_part, grad_x, g, late_pairs, late_received


def _position():
    return lax.axis_index("x"), lax.axis_index("y"), lax.axis_index("c")


def _own_slot(stack, own):
    chip = 2 * lax.axis_index("x") + lax.axis_index("y")
    return lax.dynamic_update_index_in_dim(stack, own, chip, axis=0)


def _scatter_phases(ins, outs, send_sems, recv_sems):
    n = len(ins)
    x, y, c = _position()
    me = 2 * x + y
    peers = [(1 - x, y), (x, 1 - y), (1 - x, 1 - y)]

    def copy(i, k, src_slot, dst_slot):
        px, py = peers[k]
        return pltpu.make_async_remote_copy(
            src_ref=ins[i].at[src_slot], dst_ref=outs[i].at[dst_slot],
            send_sem=send_sems.at[i, k], recv_sem=recv_sems.at[i, k],
            device_id=(px, py, c), device_id_type=MESH)

    def start():
        for i in range(n):
            for k, (px, py) in enumerate(peers):
                copy(i, k, 2 * px + py, me).start()

    def finish():
        for i in range(n):
            for k, (px, py) in enumerate(peers):
                copy(i, k, me, 2 * px + py).wait_recv()
        for i in range(n):
            for k, (px, py) in enumerate(peers):
                copy(i, k, 2 * px + py, me).wait_send()

    return start, finish


def _exchange_scratch(n, per_array):
    return [pltpu.SemaphoreType.DMA((n, per_array)), pltpu.SemaphoreType.DMA((n, per_array))]


def _scatter_chips(name, arrs):
    n = len(arrs)

    def body(*refs):
        start, finish = _scatter_phases(refs[:n], refs[n:2 * n], *refs[2 * n:])
        start()
        finish()

    hbm = pl.BlockSpec(memory_space=pl.ANY)
    return pl.pallas_call(
        body, name=name,
        in_specs=[hbm] * n, out_specs=[hbm] * n,
        out_shape=[jax.ShapeDtypeStruct(a.shape, a.dtype) for a in arrs],
        scratch_shapes=_exchange_scratch(n, N_CHIPS - 1),
        compiler_params=pltpu.CompilerParams(has_side_effects=True),
    )(*arrs)


def _half_rows(ref_rows, which):
    half = ref_rows // 2
    return pl.ds(pl.multiple_of(which * half, 2 * SUBLANES), half)


GATHER_SEMS = 2 * (N_CHIPS - 1)


def _gather_phases(nrows, ns, ins, outs, send_sems, recv_sems):
    n = len(ins)
    x, y, c = _position()
    me = 2 * x + y
    sibling = (x, y, 1 - c)
    chips = [(1 - x, y), (x, 1 - y), (1 - x, 1 - y)]

    def remote(i, k, src, dst, to):
        return pltpu.make_async_remote_copy(src_ref=src, dst_ref=dst, send_sem=send_sems.at[i, k],
                                            recv_sem=recv_sems.at[i, k], device_id=to, device_id_type=MESH)

    def over_ici(i, k):
        px, py = chips[k]
        rows = _half_rows(nrows[i], c) if i < ns else slice(None)
        return remote(i, k, ins[i].at[rows], outs[i].at[me, rows], (px, py, c))

    def landed(i, k, which):
        px, py = chips[k]
        return outs[i].at[2 * px + py, _half_rows(nrows[i], which)] if i < ns else outs[i].at[2 * px + py]

    def start():
        for i in range(n):
            for k in range(N_CHIPS - 1):
                over_ici(i, k).start()

    def forward():
        for i in range(n):
            for k in range(N_CHIPS - 1):
                piece = landed(i, k, c)
                remote(i, k, piece, piece, (*chips[k], c)).wait_recv()
                if i < ns:
                    remote(i, N_CHIPS - 1 + k, piece, piece, sibling).start()

    def finish():
        for i in range(ns):
            for k in range(N_CHIPS - 1):
                piece = landed(i, k, 1 - c)
                remote(i, N_CHIPS - 1 + k, piece, piece, sibling).wait_recv()
        for i in range(n):
            for k in range(N_CHIPS - 1):
                over_ici(i, k).wait_send()
                if i < ns:
                    piece = landed(i, k, c)
                    remote(i, N_CHIPS - 1 + k, piece, piece, sibling).wait_send()

    return start, forward, finish


def _gather_chips_split(name, split, whole):
    arrs = list(split) + list(whole)
    n = len(arrs)

    def body(*refs):
        phases = _gather_phases([a.shape[0] for a in arrs], len(split), refs[:n], refs[n:2 * n], *refs[2 * n:])
        for phase in phases:
            phase()

    hbm = pl.BlockSpec(memory_space=pl.ANY)
    return pl.pallas_call(
        body, name=name, in_specs=[hbm] * n, out_specs=[hbm] * n,
        out_shape=[jax.ShapeDtypeStruct((N_CHIPS,) + a.shape, a.dtype) for a in arrs],
        scratch_shapes=_exchange_scratch(n, GATHER_SEMS),
        compiler_params=pltpu.CompilerParams(has_side_effects=True),
    )(*arrs)


def _swap_cores(name, arrs):
    n = len(arrs)

    def body(*refs):
        ins, outs = refs[:n], refs[n:2 * n]
        send_sems, recv_sems = refs[2 * n:]
        x, y, c = _position()
        copies = [pltpu.make_async_remote_copy(src_ref=ins[i], dst_ref=outs[i], send_sem=send_sems.at[i],
                                               recv_sem=recv_sems.at[i], device_id=(x, y, 1 - c), device_id_type=MESH)
                  for i in range(n)]
        for cp in copies:
            cp.start()
        for cp in copies:
            cp.wait_recv()
        for cp in copies:
            cp.wait_send()

    hbm = pl.BlockSpec(memory_space=pl.ANY)
    return pl.pallas_call(
        body, name=name, in_specs=[hbm] * n, out_specs=[hbm] * n,
        out_shape=[jax.ShapeDtypeStruct(a.shape, a.dtype) for a in arrs],
        scratch_shapes=[pltpu.SemaphoreType.DMA((n,)), pltpu.SemaphoreType.DMA((n,))],
        compiler_params=pltpu.CompilerParams(has_side_effects=True),
    )(*arrs)


def _row_half(a, which, axis):
    half = a.shape[axis] // 2
    return lax.dynamic_slice_in_dim(a, which * half, half, axis=axis)


def _all_reduce(name, pack):
    def body(in_ref, out_ref, buf, send_sems, recv_sems):
        x, y, c = _position()
        me = 4 * x + 2 * y + c
        flips = [(dx, dy, dc) for dx in (0, 1) for dy in (0, 1) for dc in (0, 1) if (dx, dy, dc) != (0, 0, 0)]
        peers = [((1 - x) if dx else x, (1 - y) if dy else y, (1 - c) if dc else c) for dx, dy, dc in flips]
        buf[me] = in_ref[...]
        sends = []
        for k, peer in enumerate(peers):
            cp = pltpu.make_async_remote_copy(src_ref=in_ref, dst_ref=buf.at[me], send_sem=send_sems.at[k],
                                              recv_sem=recv_sems.at[k], device_id=peer, device_id_type=MESH)
            cp.start()
            sends.append(cp)
        for k, (px, py, pc) in enumerate(peers):
            pltpu.make_async_remote_copy(src_ref=in_ref, dst_ref=buf.at[4 * px + 2 * py + pc], send_sem=send_sems.at[k],
                                         recv_sem=recv_sems.at[k], device_id=(px, py, pc), device_id_type=MESH).wait_recv()
        total = buf[0]
        for j in range(1, N_DEV):
            total = total + buf[j]
        out_ref[...] = total
        for cp in sends:
            cp.wait_send()

    vmem = pl.BlockSpec(memory_space=pltpu.VMEM)
    return pl.pallas_call(
        body, name=name, in_specs=[vmem], out_specs=vmem,
        out_shape=jax.ShapeDtypeStruct(pack.shape, F32),
        scratch_shapes=[pltpu.VMEM((N_DEV,) + pack.shape, F32), pltpu.SemaphoreType.DMA((N_DEV - 1,)),
                        pltpu.SemaphoreType.DMA((N_DEV - 1,))],
        compiler_params=pltpu.CompilerParams(has_side_effects=True, vmem_limit_bytes=VMEM_LIMIT_V7X),
    )(pack)


def _pack(arrs):
    rows = [a.reshape(-1, LANES) for a in arrs]
    total = sum(r.shape[0] for r in rows)
    rows.append(jnp.zeros((-total % SUBLANES, LANES), F32))
    return jnp.concatenate(rows, axis=0)


def _unpack(pack, shapes):
    out, off = [], 0
    for s in shapes:
        nrow = 1
        for d in s:
            nrow *= d
        nrow //= LANES
        out.append(pack[off:off + nrow].reshape(s))
        off += nrow
    return out


SMALL = ["mix_norm_w", "gate_bias", "gmlp_ln_w", "gmlp_ln_b", "gmlp_ws", "gmlp_bs", "ssm_conv_w", "ssm_conv_b",
         "ssm_dt_bias", "ssm_a_log", "ssm_d", "ssm_norm_w", "ffn_norm_w", "ffn_conv_w", "ffn_conv_b", "final_norm_w"]
SMALL_SHARDED = ("gate_bias", "ssm_conv_w", "ffn_conv_w")
BIG = ["w_in", "w_proj_a", "w_proj_b", "w_out", "ffn_w_up", "ffn_w_down"]
WEIGHTS = ["mix_norm_w", "w_in", "gate_bias", "gmlp_ln_w", "gmlp_ln_b", "gmlp_ws", "gmlp_bs", "ssm_conv_w",
           "ssm_conv_b", "ssm_dt_bias", "ssm_a_log", "ssm_d", "ssm_norm_w", "w_proj_a", "w_proj_b", "w_out",
           "ffn_norm_w", "ffn_w_up", "ffn_conv_w", "ffn_conv_b", "ffn_w_down", "final_norm_w"]
IN_SPLITS = [0, 2048, 4096, 6144, 9216, 9248]


def _columns_from_chips(stack):
    return jnp.transpose(stack, (1, 0, 2)).reshape(stack.shape[1], -1)


def _columns_to_chips(full, parts=N_CHIPS):
    rows, cols = full.shape
    return jnp.transpose(full.reshape(rows, parts, cols // parts), (1, 0, 2))


def kernel(x, mix_norm_w, w_in, gate_bias, gmlp_ln_w, gmlp_ln_b, gmlp_ws, gmlp_bs, ssm_conv_w, ssm_conv_b, ssm_dt_bias, ssm_a_log, ssm_d, ssm_norm_w, w_proj_a, w_proj_b, w_out, ffn_norm_w, ffn_w_up, ffn_conv_w, ffn_conv_b, ffn_w_down, final_norm_w, loss_target, m_mix_norm_w, m_w_in, m_gate_bias, m_gmlp_ln_w, m_gmlp_ln_b, m_gmlp_ws, m_gmlp_bs, m_ssm_conv_w, m_ssm_conv_b, m_ssm_dt_bias, m_ssm_a_log, m_ssm_d, m_ssm_norm_w, m_w_proj_a, m_w_proj_b, m_w_out, m_ffn_norm_w, m_ffn_w_up, m_ffn_conv_w, m_ffn_conv_b, m_ffn_w_down, m_final_norm_w, v_mix_norm_w, v_w_in, v_gate_bias, v_gmlp_ln_w, v_gmlp_ln_b, v_gmlp_ws, v_gmlp_bs, v_ssm_conv_w, v_ssm_conv_b, v_ssm_dt_bias, v_ssm_a_log, v_ssm_d, v_ssm_norm_w, v_w_proj_a, v_w_proj_b, v_w_out, v_ffn_norm_w, v_ffn_w_up, v_ffn_conv_w, v_ffn_conv_b, v_ffn_w_down, v_final_norm_w):
    args = dict(locals())
    weights = {n: args[n] for n in WEIGHTS}
    moments_m = {n: args["m_" + n] for n in WEIGHTS}
    moments_v = {n: args["v_" + n] for n in WEIGHTS}
    chip = 2 * lax.axis_index("x") + lax.axis_index("y")

    shards = [weights["w_in"][0].astype(BF16)] + [weights[n][0] for n in SMALL_SHARDED]
    gathered = _gather_chips_split("gather_weights", shards[:1], shards[1:])
    w_in_s, gb_s, scw_s, fcw_s = [_own_slot(stack, own) for stack, own in zip(gathered, shards)]
    late_shards = [weights[n][0].astype(BF16) for n in LATE]
    w_in_full = _columns_from_chips(w_in_s)
    full = {"w_" + nm: w_in_full[:, IN_SPLITS[k]:IN_SPLITS[k + 1]] for k, nm in enumerate(["g", "za", "z", "xbc", "dt"])}
    full["gate_bias"] = _columns_from_chips(gb_s)
    full["ssm_conv_w"] = _columns_from_chips(scw_s)
    full["ffn_conv_w"] = _columns_from_chips(fcw_s)
    for n in SMALL:
        if n not in SMALL_SHARDED:
            full[n] = weights[n] if n == "final_norm_w" else weights[n][0]
    for n in ("mix_norm_w", "ffn_norm_w", "ssm_conv_b", "ssm_dt_bias", "ssm_a_log", "ssm_d", "ssm_norm_w", "ffn_conv_b"):
        full[n] = full[n].reshape(1, -1)

    loss_part, grad_x, g, late_pairs, late_received = _local_step(x[0], loss_target[0], full, late_shards)

    per_head = ["ssm_dt_bias", "ssm_a_log", "ssm_d"]
    rest = [n for n in SMALL if n not in per_head]
    head_row = jnp.concatenate([g[n] for n in per_head] + [jnp.zeros((1, LANES - 3 * SSM_HEADS), F32)], axis=1)
    pack = _pack([loss_part, head_row] + [g[n] for n in rest])
    reduced = _unpack(_all_reduce("reduce_small", pack), [(1, LANES), (1, LANES)] + [g[n].shape for n in rest])
    loss = reduced[0][0, 0]
    small_grads = {n: reduced[1][:, k * SSM_HEADS:(k + 1) * SSM_HEADS] for k, n in enumerate(per_head)}
    for n, r in zip(rest, reduced[2:]):
        if n in SMALL_SHARDED:
            width = weights[n].shape[2]
            r = lax.dynamic_slice_in_dim(r, chip * width, width, axis=1)
        small_grads[n] = r
    two_d = lambda a: a.reshape(-1, a.shape[-1])
    upd = _adamw_small("adamw_small", *[[two_d(d[n]) for n in SMALL]
                                        for d in (weights, small_grads, moments_m, moments_v)])
    small_out = [[small_grads[n] for n in SMALL]] + list(upd)
    small_out = [[a.reshape(weights[n].shape) for n, a in zip(SMALL, kind)] for kind in small_out]

    dw_in = jnp.concatenate([g["w_g"], g["w_za"], g["w_z"], g["w_xbc"], g["w_dt"]], axis=1)
    in_pairs = _pair_reduce("in", ["w_in"], [_columns_to_chips(dw_in)])
    pair = in_pairs + list(late_pairs)
    received = list(_scatter_chips("scatter_grads", in_pairs)) + list(late_received)
    received = [_own_slot(r, lax.dynamic_index_in_dim(p, chip, 0, keepdims=False)) for r, p in zip(received, pair)]
    halves = [_sum_slots("sum_" + n, r, tm=HALF_TILES[n], rs=2 * SUBLANES) for n, r in zip(BIG, received)]
    tiles = {"w_in": 128, "w_proj_a": 256, "w_proj_b": 256, "w_out": 256, "ffn_w_up": 128, "ffn_w_down": 176}
    core = lax.axis_index("c")
    other = _swap_cores("join_grads", halves)
    reduced = [jnp.concatenate([jnp.where(core == 0, a, b), jnp.where(core == 0, b, a)], axis=0)
               for a, b in zip(halves, other)]
    big_out = {}
    for n, grad in zip(BIG, reduced):
        big_out[n] = _adamw("adamw_" + n, weights[n][0], grad, moments_m[n][0], moments_v[n][0],
                            tm=tiles[n], rs=SUBLANES)

    per_kind = [[], [], [], []]
    for n in WEIGHTS:
        for kind in range(4):
            if n in big_out:
                per_kind[kind].append(big_out[n][kind].reshape(weights[n].shape))
            else:
                per_kind[kind].append(small_out[kind][SMALL.index(n)])
    return (loss, grad_x[None], *per_kind[0], *per_kind[1], *per_kind[2], *per_kind[3])
```

```python
import jax
import jax.numpy as jnp
from jax import lax
from jax.experimental import pallas as pl
from jax.experimental.pallas import tpu as pltpu

F32 = jnp.float32
BF16 = jnp.bfloat16
MESH = pl.DeviceIdType.MESH

EPS = 1e-5
D_MODEL = 1024
GMLP_BLOCK = 128
GMLP_GROUPS = 8
CHUNK = 64
SSM_INNER = 2048
SSM_HEADS = 32
SSM_HEAD_DIM = 64
SSM_GROUPS = 4
SSM_HPG = 8
SSM_STATE = 128
SSM_CONV = 4
SSM_XBC = 3072
D_FF = 2816
FFN_CONV = 3
N_CHIPS = 4
N_DEV = 8

ADAM_LR = 0.001
ADAM_B1 = 0.9
ADAM_B2 = 0.999
ADAM_EPS = 1e-08
ADAM_WD = 0.01
ADAM_STEP = 10

VMEM_LIMIT_V7X = 56 * 1024 * 1024
SUBLANES = 8
LANES = 128


def _params(sem=None):
    return pltpu.CompilerParams(dimension_semantics=sem, vmem_limit_bytes=VMEM_LIMIT_V7X)


def _dot(a, b, ca=1, cb=0):
    return lax.dot_general(a.astype(BF16), b.astype(BF16), (((ca,), (cb,)), ((), ())),
                           preferred_element_type=F32)


def _mm(name, a, b, *, ta=False, tb=False, out_dtype=F32, bm, bn, bk, res=None):
    m, k = (a.shape[1], a.shape[0]) if ta else a.shape
    k2, n = (b.shape[1], b.shape[0]) if tb else b.shape
    assert k == k2 and m % bm == 0 and n % bn == 0 and k % bk == 0, (name, a.shape, b.shape)
    nk = k // bk
    a_spec = (pl.BlockSpec((bk, bm), lambda i, j, kk: (kk, i)) if ta
              else pl.BlockSpec((bm, bk), lambda i, j, kk: (i, kk)))
    b_spec = (pl.BlockSpec((bn, bk), lambda i, j, kk: (j, kk)) if tb
              else pl.BlockSpec((bk, bn), lambda i, j, kk: (kk, j)))
    o_spec = pl.BlockSpec((bm, bn), lambda i, j, kk: (i, j))
    has_res = res is not None

    def body(*refs):
        a_ref, b_ref = refs[0], refs[1]
        r_ref = refs[2] if has_res else None
        o_ref = refs[3] if has_res else refs[2]
        p = _dot(a_ref[...], b_ref[...], 0 if ta else 1, 1 if tb else 0)

        def finish(total):
            if has_res:
                total = total + r_ref[...]
            o_ref[...] = total.astype(out_dtype)

        if nk == 1:
            finish(p)
        else:
            acc_ref = refs[-1]
            kk = pl.program_id(2)

            @pl.when(kk == 0)
            def _():
                acc_ref[...] = p

            @pl.when(kk > 0)
            def _():
                acc_ref[...] += p

            @pl.when(kk == nk - 1)
            def _():
                finish(acc_ref[...])

    return pl.pallas_call(
        body, name=name,
        grid=(m // bm, n // bn, nk),
        in_specs=[a_spec, b_spec] + ([o_spec] if has_res else []),
        out_specs=o_spec,
        out_shape=jax.ShapeDtypeStruct((m, n), out_dtype),
        scratch_shapes=[pltpu.VMEM((bm, bn), F32)] if nk > 1 else [],
        compiler_params=_params(("parallel", "parallel", "arbitrary")),
    )(*([a, b] + ([res] if has_res else [])))


def _mm_sum(name, pairs, *, bm, bk):
    m, n = pairs[0][0].shape[0], pairs[0][1].shape[0]
    steps, first = [], []
    for a, b in pairs:
        k = a.shape[1]
        assert a.shape[0] == m and b.shape == (n, k) and m % bm == 0 and (k % bk == 0 or k < bk), (name, a.shape, b.shape)
        first.append(sum(steps))
        steps.append(max(k // bk, 1))
    total = sum(steps)
    in_specs = []
    for (a, b), off, cnt in zip(pairs, first, steps):
        width = min(bk, a.shape[1])
        in_specs.append(pl.BlockSpec((bm, width), lambda i, kk, off=off, cnt=cnt: (i, jnp.clip(kk - off, 0, cnt - 1))))
        in_specs.append(pl.BlockSpec((n, width), lambda i, kk, off=off, cnt=cnt: (0, jnp.clip(kk - off, 0, cnt - 1))))

    def body(*refs):
        o_ref, acc_ref = refs[-2], refs[-1]
        kk = pl.program_id(1)
        for s, (off, cnt) in enumerate(zip(first, steps)):
            @pl.when((kk >= off) & (kk < off + cnt))
            def _(s=s, off=off):
                p = _dot(refs[2 * s][...], refs[2 * s + 1][...], 1, 1)
                if off == 0:
                    @pl.when(kk == 0)
                    def _():
                        acc_ref[...] = p

                    @pl.when(kk > 0)
                    def _():
                        acc_ref[...] += p
                else:
                    acc_ref[...] += p

        @pl.when(kk == total - 1)
        def _():
            o_ref[...] = acc_ref[...]

    return pl.pallas_call(
        body, name=name, grid=(m // bm, total),
        in_specs=in_specs, out_specs=pl.BlockSpec((bm, n), lambda i, kk: (i, 0)),
        out_shape=jax.ShapeDtypeStruct((m, n), F32),
        scratch_shapes=[pltpu.VMEM((bm, n), F32)],
        compiler_params=_params(("parallel", "arbitrary")),
    )(*[t for pair in pairs for t in pair])


def _rows(name, fn, ins, params, outs, accs, *, tm, rs, unroll=4):
    nrow = ins[0][0].shape[-2]
    while tm % (rs * unroll):
        unroll //= 2
    assert nrow % tm == 0 and tm % rs == 0, (name, nrow, tm, rs)
    n_in, n_p, n_out, n_acc = len(ins), len(params), len(outs), len(accs)
    in_specs = []
    for spec in ins:
        arr, width, cb = spec[:3]
        if len(spec) == 4:
            in_specs.append(pl.BlockSpec((None, tm, width), lambda i, cb=cb, lead=spec[3]: (lead, i, cb)))
        else:
            in_specs.append(pl.BlockSpec((tm, width), lambda i, cb=cb: (i, cb)))
    for p in params:
        in_specs.append(pl.BlockSpec(p.shape, lambda i, nd=p.ndim: (0,) * nd))
    out_specs = [pl.BlockSpec((tm, w), lambda i: (i, 0)) for w, _ in outs]
    out_specs += [pl.BlockSpec(s, lambda i: (0, 0)) for s in accs]
    out_shape = [jax.ShapeDtypeStruct((nrow, w), dt) for w, dt in outs]
    out_shape += [jax.ShapeDtypeStruct(s, F32) for s in accs]

    def body(*refs):
        in_refs = refs[:n_in]
        p_refs = refs[n_in:n_in + n_p]
        o_refs = refs[n_in + n_p:n_in + n_p + n_out]
        a_refs = refs[n_in + n_p + n_out:]
        pv = [p[...] for p in p_refs]

        if n_acc:
            @pl.when(pl.program_id(0) == 0)
            def _():
                for a_ref in a_refs:
                    a_ref[...] = jnp.zeros_like(a_ref)

        def step(r, carry):
            for u in range(unroll):
                sl = pl.ds(pl.multiple_of((r * unroll + u) * rs, rs), rs)
                vals = [ref[sl, :].astype(F32) for ref in in_refs]
                row_out, sums = fn(*vals, *pv)
                for o_ref, v in zip(o_refs, row_out):
                    o_ref[sl, :] = v.astype(o_ref.dtype)
                carry = tuple(c + s for c, s in zip(carry, sums))
            return carry

        init = tuple(jnp.zeros(s, F32) for s in accs)
        total = lax.fori_loop(0, tm // (rs * unroll), step, init)
        for a_ref, t in zip(a_refs, total):
            a_ref[...] += t

    res = pl.pallas_call(
        body, name=name, grid=(nrow // tm,),
        in_specs=in_specs, out_specs=out_specs, out_shape=out_shape,
        compiler_params=_params(("arbitrary",)),
    )(*([s[0] for s in ins] + list(params)))
    return res


def _rms(x, w):
    return x * lax.rsqrt(jnp.mean(x * x, axis=-1, keepdims=True) + EPS) * w


def _colsum(v):
    return jnp.sum(v, axis=0, keepdims=True)


def _rms_fwd(name, x, w):
    def fn(xv, wv):
        return (_rms(xv, wv),), ()
    return _rows(name, fn, [(x, D_MODEL, 0)], [w], [(D_MODEL, BF16)], [], tm=512, rs=16)[0]


def _rms_bwd(name, x, w, dy, dres):
    def fn(xv, dyv, drv, wv):
        _, vjp = jax.vjp(_rms, xv, wv)
        dx, dw = vjp(dyv)
        return (drv + dx,), (dw,)
    return _rows(name, fn, [(x, D_MODEL, 0), (dy, D_MODEL, 0), (dres, D_MODEL, 0)], [w],
                 [(D_MODEL, F32)], [(1, D_MODEL)], tm=512, rs=16)


def _final_loss(name, h, target, w):
    def fn(hv, tv, wv):
        y, vjp = jax.vjp(_rms, hv, wv)
        err = y - tv
        part = 0.5 * jnp.sum(jnp.mean(err * err, axis=-1, keepdims=True), axis=0, keepdims=True)
        dh, dw = vjp(err / D_MODEL)
        return (dh,), (jnp.broadcast_to(part, (1, LANES)), dw)
    return _rows(name, fn, [(h, D_MODEL, 0), (target, D_MODEL, 0)], [w],
                 [(D_MODEL, F32)], [(1, LANES), (1, D_MODEL)], tm=512, rs=16)


def _merge(ga, gb, ya, yb, b0, b1):
    return jax.nn.sigmoid(ga + b0) * ya + jax.nn.sigmoid(gb + b1) * yb


def _merge_fwd(name, g, ya, yb, b0, b1):
    def fn(ga, gb, yav, ybv, b0v, b1v):
        return (_merge(ga, gb, yav, ybv, b0v, b1v),), ()
    return _rows(name, fn, [(g, D_MODEL, 0), (g, D_MODEL, 1), (ya, D_MODEL, 0), (yb, D_MODEL, 0)],
                 [b0, b1], [(D_MODEL, BF16)], [], tm=512, rs=16)[0]


def _merge_bwd(name, g, ya, yb, dm, b0, b1):
    def fn(ga, gb, yav, ybv, dmv, b0v, b1v):
        _, vjp = jax.vjp(_merge, ga, gb, yav, ybv, b0v, b1v)
        dga, dgb, dya, dyb, db0, db1 = vjp(dmv)
        return (jnp.concatenate([dga, dgb], axis=1), dya, dyb), (db0, db1)
    return _rows(name, fn,
                 [(g, D_MODEL, 0), (g, D_MODEL, 1), (ya, D_MODEL, 0), (yb, D_MODEL, 0), (dm, D_MODEL, 0)],
                 [b0, b1], [(2 * D_MODEL, BF16), (D_MODEL, BF16), (D_MODEL, BF16)],
                 [(1, D_MODEL), (1, D_MODEL)], tm=512, rs=16)


GROUP_W = SSM_INNER // SSM_GROUPS


def _gate_norm_group(y, z, nw):
    v = y * jax.nn.silu(z)
    return v * lax.rsqrt(jnp.mean(v * v, axis=-1, keepdims=True) + EPS) * nw


def _gate_norm_fwd(name, y, z, nw):
    def fn(yv, zv, nwv):
        parts = [_gate_norm_group(yv[:, k * GROUP_W:(k + 1) * GROUP_W], zv[:, k * GROUP_W:(k + 1) * GROUP_W],
                                  nwv[:, k * GROUP_W:(k + 1) * GROUP_W]) for k in range(SSM_GROUPS)]
        return (jnp.concatenate(parts, axis=1),), ()
    return _rows(name, fn, [(y, SSM_INNER, 0), (z, SSM_INNER, 0)], [nw], [(SSM_INNER, BF16)], [],
                 tm=512, rs=16)[0]


def _gate_norm_bwd(name, y, z, dout, nw):
    def fn(yv, zv, dv, nwv):
        dys, dzs, dns = [], [], []
        for k in range(SSM_GROUPS):
            sl = slice(k * GROUP_W, (k + 1) * GROUP_W)
            _, vjp = jax.vjp(_gate_norm_group, yv[:, sl], zv[:, sl], nwv[:, sl])
            dy, dz, dn = vjp(dv[:, sl])
            dys.append(dy), dzs.append(dz), dns.append(dn)
        return (jnp.concatenate(dys, axis=1), jnp.concatenate(dzs, axis=1)), (jnp.concatenate(dns, axis=1),)
    return _rows(name, fn, [(y, SSM_INNER, 0), (z, SSM_INNER, 0), (dout, SSM_INNER, 0)], [nw],
                 [(SSM_INNER, F32), (SSM_INNER, BF16)], [(1, SSM_INNER)], tm=512, rs=16)


def _softplus(v):
    return jnp.maximum(v, 0.0) + jnp.log1p(jnp.exp(-jnp.abs(v)))


def _chunk_cumsum(v, reverse=False):
    row = lax.broadcasted_iota(jnp.int32, v.shape, 0)
    step = 1
    while step < CHUNK:
        if reverse:
            shifted = pltpu.roll(v, CHUNK - step, axis=0)
            v = v + jnp.where(row < CHUNK - step, shifted, 0.0)
        else:
            shifted = pltpu.roll(v, step, axis=0)
            v = v + jnp.where(row >= step, shifted, 0.0)
        step *= 2
    return v


def _dt_prep(name, dt_raw, dt_bias, a_log):
    def fn(rv, bv, alv):
        dt = _softplus(rv + bv)
        return (dt, _chunk_cumsum(dt * (-jnp.exp(alv)))), ()
    return _rows(name, fn, [(dt_raw, SSM_HEADS, 0)], [dt_bias, a_log],
                 [(SSM_HEADS, F32), (SSM_HEADS, F32)], [], tm=512, rs=CHUNK)


def _dt_bwd(name, dt_raw, ddt, da1, da2, dt_bias, a_log):
    def fn(rv, ddv, d1, d2, bv, alv):
        pre = rv + bv
        dt = _softplus(pre)
        a_neg = -jnp.exp(alv)
        back = _chunk_cumsum(d1 + d2, reverse=True)
        d_dt = ddv + back * a_neg
        d_raw = d_dt * jax.nn.sigmoid(pre)
        return (d_raw,), (_colsum(d_raw), _colsum(back * dt) * a_neg)
    return _rows(name, fn, [(dt_raw, SSM_HEADS, 0), (ddt, SSM_HEADS, 0), (da1, SSM_HEADS, 0), (da2, SSM_HEADS, 0)],
                 [dt_bias, a_log], [(SSM_HEADS, BF16)], [(1, SSM_HEADS), (1, SSM_HEADS)], tm=512, rs=CHUNK)


def _adamw_math(w, g, m, v):
    m_new = ADAM_B1 * m + (1.0 - ADAM_B1) * g
    v_new = ADAM_B2 * v + (1.0 - ADAM_B2) * jnp.square(g)
    m_hat = m_new / (1.0 - ADAM_B1 ** ADAM_STEP)
    v_hat = v_new / (1.0 - ADAM_B2 ** ADAM_STEP)
    delta = -ADAM_LR * (m_hat / (jnp.sqrt(v_hat) + ADAM_EPS) + ADAM_WD * w)
    return delta, m_new, v_new


def _adamw(name, w, g, m, v, *, tm, rs):
    width = w.shape[1]

    def fn(wv, mv, vv, gv):
        return (gv,) + _adamw_math(wv, gv, mv, vv), ()
    return _rows(name, fn, [(w, width, 0), (m, width, 0), (v, width, 0), (g, width, 0)],
                 [], [(width, F32)] * 4, [], tm=tm, rs=rs)


def _adamw_small(name, ws, gs, ms, vs):
    n = len(ws)

    def body(*refs):
        w_refs, g_refs, m_refs, v_refs = (refs[k * n:(k + 1) * n] for k in range(4))
        outs = refs[4 * n:]
        for i in range(n):
            res = _adamw_math(w_refs[i][...], g_refs[i][...], m_refs[i][...], v_refs[i][...])
            for k in range(3):
                outs[k * n + i][...] = res[k]

    vmem = pl.BlockSpec(memory_space=pltpu.VMEM)
    res = pl.pallas_call(
        body, name=name, in_specs=[vmem] * (4 * n), out_specs=[vmem] * (3 * n),
        out_shape=[jax.ShapeDtypeStruct(w.shape, F32) for w in ws] * 3,
        compiler_params=pltpu.CompilerParams(vmem_limit_bytes=VMEM_LIMIT_V7X),
    )(*ws, *gs, *ms, *vs)
    return res[:n], res[n:2 * n], res[2 * n:]


def _pair_sum(name, a, b, *, tm):
    shape = a.shape
    flat = (shape[0] * shape[1], shape[2])

    def fn(av, bv):
        return (av.astype(F32) + bv.astype(F32),), ()
    out = _rows(name, fn, [(a.reshape(flat), flat[1], 0), (b.reshape(flat), flat[1], 0)], [], [(flat[1], BF16)], [],
                tm=tm, rs=2 * SUBLANES)[0]
    return out.reshape(shape)


def _sum_slots(name, stack, *, tm, rs):
    width = stack.shape[2]

    def fn(*slots):
        s0, s1, s2, s3 = (s.astype(F32) for s in slots)
        return (((s0 + s1) + s2) + s3,), ()
    return _rows(name, fn, [(stack, width, 0, k) for k in range(N_CHIPS)], [], [(width, F32)], [],
                 tm=tm, rs=rs)[0]


def _layernorm(v, w, b):
    mu = jnp.mean(v, axis=-1, keepdims=True)
    var = jnp.mean(jnp.square(v - mu), axis=-1, keepdims=True)
    return (v - mu) * lax.rsqrt(var + EPS) * w + b


def _gmlp_mask():
    t = lax.broadcasted_iota(jnp.int32, (GMLP_BLOCK, GMLP_BLOCK), 0) // CHUNK
    s = lax.broadcasted_iota(jnp.int32, (GMLP_BLOCK, GMLP_BLOCK), 1) // CHUNK
    return s <= t


GMLP_TM = 512


def _gmlp_fwd(name, za, ln_w, ln_b, ws, bs_col):
    nrow = za.shape[0]
    tm = GMLP_TM
    width = GMLP_GROUPS * GMLP_BLOCK

    def body(za_ref, lnw_ref, lnb_ref, ws_ref, bs_ref, o_ref, wm_ref):
        mask = _gmlp_mask()
        for g in range(GMLP_GROUPS):
            wm_ref[g] = jnp.where(mask, ws_ref[g], 0.0).astype(BF16)

        def block(n, carry):
            rows = pl.ds(pl.multiple_of(n * GMLP_BLOCK, GMLP_BLOCK), GMLP_BLOCK)
            for g in range(GMLP_GROUPS):
                cols = slice(g * GMLP_BLOCK, (g + 1) * GMLP_BLOCK)
                vcols = slice(width + g * GMLP_BLOCK, width + (g + 1) * GMLP_BLOCK)
                u = jax.nn.gelu(za_ref[rows, cols].astype(F32))
                v = jax.nn.gelu(za_ref[rows, vcols].astype(F32))
                vn = _layernorm(v, lnw_ref[g:g + 1, :], lnb_ref[g:g + 1, :])
                sv = _dot(wm_ref[g], vn) + bs_ref[g]
                o_ref[rows, cols] = (u * sv).astype(o_ref.dtype)
            return carry

        lax.fori_loop(0, tm // GMLP_BLOCK, block, 0)

    small = lambda a: pl.BlockSpec(a.shape, lambda i, nd=a.ndim: (0,) * nd)
    return pl.pallas_call(
        body, name=name, grid=(nrow // tm,),
        in_specs=[pl.BlockSpec((tm, 2 * width), lambda i: (i, 0)), small(ln_w), small(ln_b), small(ws), small(bs_col)],
        out_specs=pl.BlockSpec((tm, width), lambda i: (i, 0)),
        out_shape=jax.ShapeDtypeStruct((nrow, width), BF16),
        scratch_shapes=[pltpu.VMEM((GMLP_GROUPS, GMLP_BLOCK, GMLP_BLOCK), BF16)],
        compiler_params=_params(("arbitrary",)),
    )(za, ln_w, ln_b, ws, bs_col)


def _gmlp_bwd(name, za, dout, ln_w, ln_b, ws, bs_col):
    nrow = za.shape[0]
    tm = GMLP_TM
    width = GMLP_GROUPS * GMLP_BLOCK

    def body(za_ref, do_ref, lnw_ref, lnb_ref, ws_ref, bs_ref, dza_ref, dlnw_ref, dlnb_ref, dws_ref, dbs_ref, wm_ref):
        mask = _gmlp_mask()
        for g in range(GMLP_GROUPS):
            wm_ref[g] = jnp.where(mask, ws_ref[g], 0.0).astype(BF16)

        @pl.when(pl.program_id(0) == 0)
        def _():
            dlnw_ref[...] = jnp.zeros_like(dlnw_ref)
            dlnb_ref[...] = jnp.zeros_like(dlnb_ref)
            dws_ref[...] = jnp.zeros_like(dws_ref)
            dbs_ref[...] = jnp.zeros_like(dbs_ref)

        def block(n, carry):
            rows = pl.ds(pl.multiple_of(n * GMLP_BLOCK, GMLP_BLOCK), GMLP_BLOCK)
            for g in range(GMLP_GROUPS):
                cols = slice(g * GMLP_BLOCK, (g + 1) * GMLP_BLOCK)
                vcols = slice(width + g * GMLP_BLOCK, width + (g + 1) * GMLP_BLOCK)
                u, gelu_u_vjp = jax.vjp(jax.nn.gelu, za_ref[rows, cols].astype(F32))
                v, gelu_v_vjp = jax.vjp(jax.nn.gelu, za_ref[rows, vcols].astype(F32))
                vn, ln_vjp = jax.vjp(_layernorm, v, lnw_ref[g:g + 1, :], lnb_ref[g:g + 1, :])
                sv = _dot(wm_ref[g], vn) + bs_ref[g]
                d_o = do_ref[rows, cols].astype(F32)
                dsv = d_o * u
                d_wm = _dot(dsv, vn, 1, 1)
                dvn = _dot(wm_ref[g], dsv, 0, 0)
                dv, dlnw, dlnb = ln_vjp(dvn)
                dza_ref[rows, cols] = gelu_u_vjp(d_o * sv)[0].astype(dza_ref.dtype)
                dza_ref[rows, vcols] = gelu_v_vjp(dv)[0].astype(dza_ref.dtype)
                dlnw_ref[g:g + 1, :] += dlnw
                dlnb_ref[g:g + 1, :] += dlnb
                dws_ref[g] += jnp.where(mask, d_wm, 0.0)
                dbs_ref[g] += jnp.sum(dsv, axis=1, keepdims=True)
            return carry

        lax.fori_loop(0, tm // GMLP_BLOCK, block, 0)

    small = lambda a: pl.BlockSpec(a.shape, lambda i, nd=a.ndim: (0,) * nd)
    return pl.pallas_call(
        body, name=name, grid=(nrow // tm,),
        in_specs=[pl.BlockSpec((tm, 2 * width), lambda i: (i, 0)), pl.BlockSpec((tm, width), lambda i: (i, 0)),
                  small(ln_w), small(ln_b), small(ws), small(bs_col)],
        out_specs=[pl.BlockSpec((tm, 2 * width), lambda i: (i, 0)), small(ln_w), small(ln_b), small(ws), small(bs_col)],
        out_shape=[jax.ShapeDtypeStruct((nrow, 2 * width), BF16), jax.ShapeDtypeStruct(ln_w.shape, F32),
                   jax.ShapeDtypeStruct(ln_b.shape, F32), jax.ShapeDtypeStruct(ws.shape, F32),
                   jax.ShapeDtypeStruct(bs_col.shape, F32)],
        scratch_shapes=[pltpu.VMEM((GMLP_GROUPS, GMLP_BLOCK, GMLP_BLOCK), BF16)],
        compiler_params=_params(("arbitrary",)),
    )(za, dout, ln_w, ln_b, ws, bs_col)


CONV_TM = 256
CONV_RS = 32
HALO = 2 * SUBLANES


def _tap_rows(w_ref):
    return [w_ref[k:k + 1, :] for k in range(w_ref.shape[0])]


def _halo_specs(nrow, tm, tc):
    per = tm // HALO
    last = nrow // HALO - 1
    main = pl.BlockSpec((tm, tc), lambda j, i: (i, j))
    before = pl.BlockSpec((HALO, tc), lambda j, i: (jnp.maximum(i * per - 1, 0), j))
    after = pl.BlockSpec((HALO, tc), lambda j, i: (jnp.minimum((i + 1) * per, last), j))
    return main, before, after


def _col_spec(rows, tc):
    return pl.BlockSpec((rows, tc), lambda j, i: (0, j))


def _conv_fwd(name, x, w, b, *, tc):
    nrow, ncol = x.shape
    taps = w.shape[0]
    tm, rs = CONV_TM, CONV_RS
    main, before, _ = _halo_specs(nrow, tm, tc)

    def body(x_ref, xb_ref, w_ref, b_ref, o_ref, xw_ref):
        first = pl.program_id(1) == 0
        wv, bv = _tap_rows(w_ref), b_ref[...]
        xw_ref[0:HALO, :] = jnp.where(first, 0.0, xb_ref[...].astype(F32))
        for r in range(tm // rs):
            xw_ref[HALO + r * rs:HALO + (r + 1) * rs, :] = x_ref[r * rs:(r + 1) * rs, :].astype(F32)
        for r in range(tm // rs):
            base = HALO + r * rs
            out = bv + wv[taps - 1] * xw_ref[base:base + rs, :]
            for k in range(taps - 1):
                back = taps - 1 - k
                out = out + wv[k] * xw_ref[base - back:base - back + rs, :]
            o_ref[r * rs:(r + 1) * rs, :] = out.astype(o_ref.dtype)

    return pl.pallas_call(
        body, name=name, grid=(ncol // tc, nrow // tm),
        in_specs=[main, before, _col_spec(taps, tc), _col_spec(1, tc)],
        out_specs=main, out_shape=jax.ShapeDtypeStruct((nrow, ncol), BF16),
        scratch_shapes=[pltpu.VMEM((HALO + tm, tc), F32)],
        compiler_params=_params(("parallel", "arbitrary")),
    )(x, x, w, b)


def _conv_bwd(name, dpre, x, w, *, tc):
    nrow, ncol = x.shape
    taps = w.shape[0]
    tm, rs = CONV_TM, CONV_RS
    nsub = tm // rs
    main, before, after = _halo_specs(nrow, tm, tc)

    def fold(v):
        total = v[0:SUBLANES]
        for q in range(1, rs // SUBLANES):
            total = total + v[q * SUBLANES:(q + 1) * SUBLANES]
        return total

    def body(d_ref, da_ref, x_ref, xb_ref, w_ref, dx_ref, dw_ref, db_ref, dwin_ref, xwin_ref):
        i = pl.program_id(1)
        first, last = i == 0, i == pl.num_programs(1) - 1
        wv = _tap_rows(w_ref)

        @pl.when(first)
        def _():
            dw_ref[...] = jnp.zeros_like(dw_ref)
            db_ref[...] = jnp.zeros_like(db_ref)

        xwin_ref[0:HALO, :] = jnp.where(first, 0.0, xb_ref[...].astype(F32))
        dwin_ref[tm:, :] = jnp.where(last, 0.0, da_ref[...].astype(F32))
        for r in range(nsub):
            dwin_ref[r * rs:(r + 1) * rs, :] = d_ref[r * rs:(r + 1) * rs, :].astype(F32)
            xwin_ref[HALO + r * rs:HALO + (r + 1) * rs, :] = x_ref[r * rs:(r + 1) * rs, :].astype(F32)
        dw = [jnp.zeros((SUBLANES, tc), F32)] * taps
        db = jnp.zeros((SUBLANES, tc), F32)
        for r in range(nsub):
            cur = dwin_ref[r * rs:(r + 1) * rs, :]
            dx = wv[taps - 1] * cur
            for k in range(taps - 1):
                ahead = taps - 1 - k
                dx = dx + wv[k] * dwin_ref[r * rs + ahead:(r + 1) * rs + ahead, :]
            dx_ref[r * rs:(r + 1) * rs, :] = dx.astype(dx_ref.dtype)
            for k in range(taps):
                back = taps - 1 - k
                dw[k] = dw[k] + fold(cur * xwin_ref[HALO + r * rs - back:HALO + (r + 1) * rs - back, :])
            db = db + fold(cur)
        for k in range(taps):
            dw_ref[k:k + 1, :] += _colsum(dw[k])
        db_ref[...] += _colsum(db)

    return pl.pallas_call(
        body, name=name, grid=(ncol // tc, nrow // tm),
        in_specs=[main, after, main, before, _col_spec(taps, tc)],
        out_specs=[main, _col_spec(taps, tc), _col_spec(1, tc)],
        out_shape=[jax.ShapeDtypeStruct((nrow, ncol), BF16), jax.ShapeDtypeStruct((taps, ncol), F32),
                   jax.ShapeDtypeStruct((1, ncol), F32)],
        scratch_shapes=[pltpu.VMEM((tm + HALO, tc), F32), pltpu.VMEM((HALO + tm, tc), F32)],
        compiler_params=_params(("parallel", "arbitrary")),
    )(dpre, dpre, x, x, w)


def _glu(gate, val):
    return jax.nn.silu(gate) * val


def _ffn_act_fwd(name, pg, pv, wg, wv, bg, bv, *, tc):
    nrow, ncol = pg.shape
    taps = wg.shape[0]
    tm, rs = CONV_TM, CONV_RS
    main, before, _ = _halo_specs(nrow, tm, tc)

    def body(pg_ref, pgb_ref, pv_ref, pvb_ref, wg_ref, wv_ref, bg_ref, bv_ref, g_ref, v_ref, a_ref, gwin_ref, vwin_ref):
        first = pl.program_id(1) == 0
        taps_g, taps_v, bgv, bvv = _tap_rows(wg_ref), _tap_rows(wv_ref), bg_ref[...], bv_ref[...]
        gwin_ref[0:HALO, :] = jnp.where(first, 0.0, pgb_ref[...].astype(F32))
        vwin_ref[0:HALO, :] = jnp.where(first, 0.0, pvb_ref[...].astype(F32))
        for r in range(tm // rs):
            gwin_ref[HALO + r * rs:HALO + (r + 1) * rs, :] = pg_ref[r * rs:(r + 1) * rs, :].astype(F32)
            vwin_ref[HALO + r * rs:HALO + (r + 1) * rs, :] = pv_ref[r * rs:(r + 1) * rs, :].astype(F32)

        def conv(win_ref, tap_rows, bias, r):
            base = HALO + r * rs
            out = bias + tap_rows[taps - 1] * win_ref[base:base + rs, :]
            for k in range(taps - 1):
                back = taps - 1 - k
                out = out + tap_rows[k] * win_ref[base - back:base - back + rs, :]
            return out

        for r in range(tm // rs):
            sl = slice(r * rs, (r + 1) * rs)
            gate, val = conv(gwin_ref, taps_g, bgv, r), conv(vwin_ref, taps_v, bvv, r)
            g_ref[sl, :] = gate.astype(g_ref.dtype)
            v_ref[sl, :] = val.astype(v_ref.dtype)
            a_ref[sl, :] = _glu(gate, val).astype(a_ref.dtype)

    return pl.pallas_call(
        body, name=name, grid=(ncol // tc, nrow // tm),
        in_specs=[main, before, main, before, _col_spec(taps, tc), _col_spec(taps, tc), _col_spec(1, tc), _col_spec(1, tc)],
        out_specs=[main, main, main],
        out_shape=[jax.ShapeDtypeStruct((nrow, ncol), BF16)] * 3,
        scratch_shapes=[pltpu.VMEM((HALO + tm, tc), F32), pltpu.VMEM((HALO + tm, tc), F32)],
        compiler_params=_params(("parallel", "arbitrary")),
    )(pg, pg, pv, pv, wg, wv, bg, bv)


def _ffn_act_bwd(name, dact, gate, val):
    def fn(dv, gv, vv):
        _, vjp = jax.vjp(_glu, gv, vv)
        dg, dval = vjp(dv)
        return (dg, dval), ()
    width = dact.shape[1]
    return _rows(name, fn, [(dact, width, 0), (gate, width, 0), (val, width, 0)], [],
                 [(width, BF16), (width, BF16)], [], tm=256, rs=2 * SUBLANES)


SSD_TM = 256
SSD_CHUNKS = SSD_TM // CHUNK
X_OFF, B_OFF, C_OFF = 0, SSM_INNER, SSM_INNER + SSM_GROUPS * SSM_STATE
HP = SSM_HPG * SSM_HEAD_DIM


def _causal_tiled():
    row = lax.broadcasted_iota(jnp.int32, (CHUNK, HP), 0)
    src = lax.broadcasted_iota(jnp.int32, (CHUNK, HP), 1) & (CHUNK - 1)
    return src <= row


def _split2(v):
    hi = v.astype(BF16)
    return hi, (v - hi.astype(F32)).astype(BF16)


def _dot_exact(a, ind):
    hi, lo = (lax.dot_general(p, ind, (((1,), (0,)), ((), ())), preferred_element_type=F32) for p in _split2(a))
    return hi + lo


def _head_indicator():
    head = lax.broadcasted_iota(jnp.int32, (SSM_HEADS, SSM_INNER), 0)
    chan = lax.broadcasted_iota(jnp.int32, (SSM_HEADS, SSM_INNER), 1)
    return (chan // SSM_HEAD_DIM == head).astype(BF16)


def _chunk_decays(ci, dt_ref, ac_ref, ind, ax_ref, dtx_ref, eax_ref, eex_ref, tail_ref):
    rows = pl.ds(pl.multiple_of(ci * CHUNK, CHUNK), CHUNK)
    ax_ref[...] = _dot_exact(ac_ref[rows, :], ind)
    dtx_ref[...] = _dot_exact(dt_ref[rows, :], ind)
    eax_ref[...] = jnp.exp(ax_ref[...])
    eex_ref[...] = jnp.exp(ax_ref[CHUNK - 1:CHUNK, :] - ax_ref[...])
    tail = pl.ds(pl.multiple_of(ci * CHUNK + CHUNK - SUBLANES, SUBLANES), SUBLANES)
    tail_ref[...] = jnp.exp(ac_ref[tail, :])


def _group_decay(ci, g, ax_ref, af_ref, xbc_ref, causal):
    gcols = slice(g * HP, (g + 1) * HP)
    bm = xbc_ref[:, B_OFF + g * SSM_STATE:B_OFF + (g + 1) * SSM_STATE]
    cm = xbc_ref[:, C_OFF + g * SSM_STATE:C_OFF + (g + 1) * SSM_STATE]
    cb_tiled = _dot(cm, jnp.concatenate([bm] * SSM_HPG, axis=0), 1, 1)
    seg = ax_ref[:, gcols] - af_ref[ci, :, gcols]
    decay = jnp.where(causal, jnp.exp(jnp.where(causal, seg, 0.0)), 0.0)
    return bm, cm, cb_tiled * decay, decay


def _ssd_fwd(name, pre, dt, a_cum, a_flat, d_x, ind, shards):
    nrow = pre.shape[0]
    tm = SSD_TM
    nstep = nrow // tm
    ng = len(shards)

    def body(pre_ref, dt_ref, ac_ref, af_ref, dx_ref, ind_ref, *rest):
        shard_refs, (y_ref, st_ref), stack_refs = rest[:ng], rest[ng:ng + 2], rest[ng + 2:2 * ng + 2]
        (h_ref, xbc_ref, ax_ref, dtx_ref, eax_ref, eex_ref, m_ref, xd_ref, yd_ref, tail_ref,
         send_sems, recv_sems) = rest[2 * ng + 2:]
        step = pl.program_id(0)
        start, forward, finish = _gather_phases([s.shape[0] for s in shards], ng, shard_refs, stack_refs,
                                                send_sems, recv_sems)

        @pl.when(step == 0)
        def _():
            h_ref[...] = jnp.zeros_like(h_ref)
            start()

        @pl.when(step == nstep // 2)
        def _():
            forward()

        causal = _causal_tiled()
        ind = ind_ref[...]

        def chunk(ci, carry):
            rows = pl.ds(pl.multiple_of(ci * CHUNK, CHUNK), CHUNK)
            xbc_ref[...] = jax.nn.silu(pre_ref[rows, :].astype(F32))
            _chunk_decays(ci, dt_ref, ac_ref, ind, ax_ref, dtx_ref, eax_ref, eex_ref, tail_ref)
            st_ref[ci] = h_ref[...].astype(st_ref.dtype)
            for g in range(SSM_GROUPS):
                gcols = slice(g * HP, (g + 1) * HP)
                bm, cm, m_all, _ = _group_decay(ci, g, ax_ref, af_ref, xbc_ref, causal)
                m_ref[...] = m_all
                x_g = xbc_ref[:, gcols]
                xd = x_g * dtx_ref[:, gcols]
                xd_ref[...] = xd
                h_g = h_ref[gcols, :]
                for hh in range(SSM_HPG):
                    lc = slice(hh * SSM_HEAD_DIM, (hh + 1) * SSM_HEAD_DIM)
                    yd_ref[:, lc] = _dot(m_ref[:, lc], xd_ref[:, lc])
                y_ref[rows, gcols] = (yd_ref[...] + _dot(cm, h_g, 1, 1) * eax_ref[:, gcols]
                                      + dx_ref[:, gcols] * x_g)
                new = _dot(xd * eex_ref[:, gcols], bm, 0, 0)
                for hh in range(SSM_HPG):
                    h = g * SSM_HPG + hh
                    hrows = slice(h * SSM_HEAD_DIM, (h + 1) * SSM_HEAD_DIM)
                    lrows = slice(hh * SSM_HEAD_DIM, (hh + 1) * SSM_HEAD_DIM)
                    h_ref[hrows, :] = tail_ref[SUBLANES - 1:SUBLANES, h:h + 1] * h_ref[hrows, :] + new[lrows, :]
            return carry

        lax.fori_loop(0, SSD_CHUNKS, chunk, 0)

        @pl.when(step == nstep - 1)
        def _():
            finish()

    nchunk = nrow // CHUNK
    whole = lambda a: pl.BlockSpec(a.shape, lambda i, nd=a.ndim: (0,) * nd)
    hbm = pl.BlockSpec(memory_space=pl.ANY)
    wide = lambda: pltpu.VMEM((CHUNK, SSM_INNER), F32)
    group = lambda: pltpu.VMEM((CHUNK, HP), F32)
    res = pl.pallas_call(
        body, name=name, grid=(nstep,),
        in_specs=[pl.BlockSpec((tm, SSM_XBC), lambda i: (i, 0)), pl.BlockSpec((tm, SSM_HEADS), lambda i: (i, 0)),
                  pl.BlockSpec((tm, SSM_HEADS), lambda i: (i, 0)),
                  pl.BlockSpec((SSD_CHUNKS, 1, SSM_INNER), lambda i: (i, 0, 0)), whole(d_x), whole(ind)] + [hbm] * ng,
        out_specs=[pl.BlockSpec((tm, SSM_INNER), lambda i: (i, 0)),
                   pl.BlockSpec((SSD_CHUNKS, SSM_INNER, SSM_STATE), lambda i: (i, 0, 0))] + [hbm] * ng,
        out_shape=[jax.ShapeDtypeStruct((nrow, SSM_INNER), F32),
                   jax.ShapeDtypeStruct((nchunk, SSM_INNER, SSM_STATE), BF16)]
        + [jax.ShapeDtypeStruct((N_CHIPS,) + s.shape, s.dtype) for s in shards],
        scratch_shapes=[pltpu.VMEM((SSM_INNER, SSM_STATE), F32), pltpu.VMEM((CHUNK, SSM_XBC), F32),
                        wide(), wide(), wide(), wide(), group(), group(), group(),
                        pltpu.VMEM((SUBLANES, SSM_HEADS), F32)] + _exchange_scratch(ng, GATHER_SEMS),
        compiler_params=_params(("arbitrary",)),
    )(pre, dt, a_cum, a_flat, d_x, ind, *shards)
    return res[0], res[1], res[2:]


def _ssd_bwd(name, pre, dt, a_cum, a_flat, d_x, ind, ind_t, states, dy, pairs):
    nrow = pre.shape[0]
    tm = SSD_TM
    ntile = nrow // tm
    npair = len(pairs)

    def body(pre_ref, dt_ref, ac_ref, af_ref, dx_ref, ind_ref, indt_ref, st_ref, dy_ref, *rest):
        pair_refs = rest[:npair]
        dpre_ref, ddt_ref, da_ref, daf_ref, dd_ref = rest[npair:npair + 5]
        recv_refs = rest[npair + 5:2 * npair + 5]
        (dh_ref, xbc_ref, dxbc_ref, ax_ref, dtx_ref, eax_ref, eex_ref, red_ref,
         m_ref, l_ref, xd_ref, dm_ref, dxd_ref, fold_ref, hd_ref, tail_ref, send_sems, recv_sems) = rest[2 * npair + 5:]
        start, finish = _scatter_phases(pair_refs, recv_refs, send_sems, recv_sems)

        @pl.when(pl.program_id(0) == 0)
        def _():
            dh_ref[...] = jnp.zeros_like(dh_ref)
            dd_ref[...] = jnp.zeros_like(dd_ref)
            start()

        causal = _causal_tiled()
        ind, ind_t = ind_ref[...], indt_ref[...]
        is_last_row = lax.broadcasted_iota(jnp.int32, (CHUNK, 1), 0) == CHUNK - 1
        ones = jnp.ones((CHUNK, SSM_STATE), BF16)

        def chunk(k, ddx):
            ci = SSD_CHUNKS - 1 - k
            rows = pl.ds(pl.multiple_of(ci * CHUNK, CHUNK), CHUNK)
            pre_v = pre_ref[rows, :].astype(F32)
            xbc_ref[...] = jax.nn.silu(pre_v)
            _chunk_decays(ci, dt_ref, ac_ref, ind, ax_ref, dtx_ref, eax_ref, eex_ref, tail_ref)
            ddx_parts = []
            for g in range(SSM_GROUPS):
                gcols = slice(g * HP, (g + 1) * HP)
                bcols = slice(B_OFF + g * SSM_STATE, B_OFF + (g + 1) * SSM_STATE)
                ccols = slice(C_OFF + g * SSM_STATE, C_OFF + (g + 1) * SSM_STATE)
                bm, cm, m_all, decay = _group_decay(ci, g, ax_ref, af_ref, xbc_ref, causal)
                m_ref[...] = m_all
                l_ref[...] = decay
                x_g = xbc_ref[:, gcols]
                xd = x_g * dtx_ref[:, gcols]
                xd_ref[...] = xd
                h_g = st_ref[ci, gcols, :]
                dh_g = dh_ref[gcols, :]
                dy_g = dy_ref[rows, gcols]
                for hh in range(SSM_HPG):
                    h = g * SSM_HPG + hh
                    hcols = slice(h * SSM_HEAD_DIM, (h + 1) * SSM_HEAD_DIM)
                    lc = slice(hh * SSM_HEAD_DIM, (hh + 1) * SSM_HEAD_DIM)
                    dy_h = dy_ref[rows, hcols]
                    dm_ref[:, lc] = _dot(dy_h, xd_ref[:, lc], 1, 1)
                    dxd_ref[:, lc] = _dot(m_ref[:, lc], dy_h, 0, 0)
                ebdh = eex_ref[:, gcols] * _dot(bm, dh_g, 1, 1)
                dxd = dxd_ref[...] + ebdh
                dm = dm_ref[...]
                t = dm * l_ref[...]
                t128 = (t[:, 0:LANES] + t[:, LANES:2 * LANES]) + (t[:, 2 * LANES:3 * LANES] + t[:, 3 * LANES:])
                fold_ref[...] = t128 + pltpu.roll(t128, CHUNK, axis=1)
                dw_sum = fold_ref[:, 0:CHUNK]
                q = dm * m_ref[...]
                dyea = dy_g * eax_ref[:, gcols]
                red_ref[0:CHUNK, gcols] = q + dyea * _dot(cm, h_g, 1, 1)
                red_ref[CHUNK:2 * CHUNK, gcols] = xd * ebdh
                red_ref[2 * CHUNK:3 * CHUNK, gcols] = dxd * x_g
                daf_ref[ci, :, gcols] = -jnp.sum(q, axis=0, keepdims=True)
                ddx_parts.append(jnp.sum(dy_g * x_g, axis=0, keepdims=True))
                dxbc_ref[:, gcols] = dxd * dtx_ref[:, gcols] + dx_ref[:, gcols] * dy_g
                dxbc_ref[:, ccols] = _dot(dw_sum, bm) + _dot(dyea, h_g)
                dxbc_ref[:, bcols] = _dot(dw_sum, cm, 0, 0) + _dot(xd * eex_ref[:, gcols], dh_g)
                dh_new = _dot(dyea, cm, 0, 0)
                for hh in range(SSM_HPG):
                    h = g * SSM_HPG + hh
                    hrows = slice(h * SSM_HEAD_DIM, (h + 1) * SSM_HEAD_DIM)
                    lrows = slice(hh * SSM_HEAD_DIM, (hh + 1) * SSM_HEAD_DIM)
                    hd_ref[h:h + 1, :] = jnp.sum(st_ref[ci, hrows, :] * dh_ref[hrows, :], axis=0, keepdims=True)
                    dh_ref[hrows, :] = tail_ref[SUBLANES - 1:SUBLANES, h:h + 1] * dh_ref[hrows, :] + dh_new[lrows, :]
            sums = _dot_exact(red_ref[...], ind_t)
            ra, ts = sums[:CHUNK], sums[CHUNK:2 * CHUNK]
            hdh = sum(lax.dot_general(ones, p, (((1,), (1,)), ((), ())), preferred_element_type=F32)
                      for p in _split2(hd_ref[...]))
            da_last = jnp.sum(ts, axis=0, keepdims=True) + tail_ref[SUBLANES - 1:SUBLANES, :] * hdh
            da_ref[rows, :] = ra - ts + jnp.where(is_last_row, da_last, 0.0)
            ddt_ref[rows, :] = sums[2 * CHUNK:]
            sig = jax.nn.sigmoid(pre_v)
            dpre_ref[rows, :] = (dxbc_ref[...] * (sig * (1.0 + pre_v * (1.0 - sig)))).astype(dpre_ref.dtype)
            return ddx + jnp.concatenate(ddx_parts, axis=1)

        ddx = lax.fori_loop(0, SSD_CHUNKS, chunk, jnp.zeros((1, SSM_INNER), F32))
        dd_ref[...] += _dot_exact(jnp.broadcast_to(ddx, (SUBLANES, SSM_INNER)), ind_t)

        @pl.when(pl.program_id(0) == ntile - 1)
        def _():
            finish()

    rev = lambda i: ntile - 1 - i
    whole = lambda a: pl.BlockSpec(a.shape, lambda i, nd=a.ndim: (0,) * nd)
    hbm = pl.BlockSpec(memory_space=pl.ANY)
    wide = lambda: pltpu.VMEM((CHUNK, SSM_INNER), F32)
    group = lambda: pltpu.VMEM((CHUNK, HP), F32)
    res = pl.pallas_call(
        body, name=name, grid=(ntile,),
        in_specs=[pl.BlockSpec((tm, SSM_XBC), lambda i: (rev(i), 0)), pl.BlockSpec((tm, SSM_HEADS), lambda i: (rev(i), 0)),
                  pl.BlockSpec((tm, SSM_HEADS), lambda i: (rev(i), 0)),
                  pl.BlockSpec((SSD_CHUNKS, 1, SSM_INNER), lambda i: (rev(i), 0, 0)),
                  whole(d_x), whole(ind), whole(ind_t),
                  pl.BlockSpec((SSD_CHUNKS, SSM_INNER, SSM_STATE), lambda i: (rev(i), 0, 0)),
                  pl.BlockSpec((tm, SSM_INNER), lambda i: (rev(i), 0))] + [hbm] * npair,
        out_specs=[pl.BlockSpec((tm, SSM_XBC), lambda i: (rev(i), 0)), pl.BlockSpec((tm, SSM_HEADS), lambda i: (rev(i), 0)),
                   pl.BlockSpec((tm, SSM_HEADS), lambda i: (rev(i), 0)),
                   pl.BlockSpec((SSD_CHUNKS, 1, SSM_INNER), lambda i: (rev(i), 0, 0)),
                   pl.BlockSpec((SUBLANES, SSM_HEADS), lambda i: (0, 0))] + [hbm] * npair,
        out_shape=[jax.ShapeDtypeStruct((nrow, SSM_XBC), BF16), jax.ShapeDtypeStruct((nrow, SSM_HEADS), F32),
                   jax.ShapeDtypeStruct((nrow, SSM_HEADS), F32), jax.ShapeDtypeStruct((nrow // CHUNK, 1, SSM_INNER), F32),
                   jax.ShapeDtypeStruct((SUBLANES, SSM_HEADS), F32)]
        + [jax.ShapeDtypeStruct(p.shape, p.dtype) for p in pairs],
        scratch_shapes=[pltpu.VMEM((SSM_INNER, SSM_STATE), F32), pltpu.VMEM((CHUNK, SSM_XBC), F32),
                        pltpu.VMEM((CHUNK, SSM_XBC), F32), wide(), wide(), wide(), wide(),
                        pltpu.VMEM((3 * CHUNK, SSM_INNER), F32),
                        group(), group(), group(), group(), group(), pltpu.VMEM((CHUNK, LANES), F32),
                        pltpu.VMEM((SSM_HEADS, SSM_STATE), F32), pltpu.VMEM((SUBLANES, SSM_HEADS), F32)]
        + _exchange_scratch(npair, N_CHIPS - 1),
        compiler_params=_params(("arbitrary",)),
    )(pre, dt, a_cum, a_flat, d_x, ind, ind_t, states, dy, *pairs)
    return res[:5], res[5:]


LATE = ["w_proj_a", "w_proj_b", "w_out", "ffn_w_up", "ffn_w_down"]
HALF_TILES = {"w_in": 128, "w_proj_a": 128, "w_proj_b": 256, "w_out": 128, "ffn_w_up": 128, "ffn_w_down": 176}


def _late_weights(stacks, shards):
    pa, pb, out, up, down = [_own_slot(stack, own) for stack, own in zip(stacks, shards)]
    return {"w_proj_a": pa.reshape(-1, D_MODEL), "w_proj_b": pb.reshape(-1, D_MODEL), "w_out": out.reshape(-1, D_MODEL),
            "w_up_g": _columns_from_chips(up[:2]), "w_up_v": _columns_from_chips(up[2:]),
            "w_down": down.reshape(-1, D_MODEL)}


def _pair_reduce(tag, names, stacks):
    core = lax.axis_index("c")
    own_half = [_row_half(s, core, 1) for s in stacks]
    other_half = _swap_cores("pair_grads_" + tag, [_row_half(s, 1 - core, 1) for s in stacks])
    return [_pair_sum("pair_" + n, a, b, tm=HALF_TILES[n]) for n, a, b in zip(names, own_half, other_half)]


def _local_step(x, target, w, late_shards):
    w = dict(w)
    g = {}
    bs_col = w["gmlp_bs"].reshape(GMLP_GROUPS, GMLP_BLOCK, 1)
    b0, b1 = w["gate_bias"][0:1], w["gate_bias"][1:2]

    xn = _rms_fwd("mix_norm", x, w["mix_norm_w"])
    big = dict(bm=1024, bn=1024, bk=1024)
    act16 = dict(out_dtype=BF16, **big)
    gates = _mm("in_gates", xn, w["w_g"], **act16)
    za = _mm("in_gmlp", xn, w["w_za"], **act16)
    z = _mm("in_z", xn, w["w_z"], **act16)
    xbc = _mm("in_xbc", xn, w["w_xbc"], **act16)
    dt_raw = _mm("in_dt", xn, w["w_dt"], bm=1024, bn=SSM_HEADS, bk=1024)

    pre = _conv_fwd("ssm_conv_fwd", xbc, w["ssm_conv_w"], w["ssm_conv_b"], tc=1024)
    dt, a_cum = _dt_prep("dt_prep", dt_raw, w["ssm_dt_bias"], w["ssm_a_log"])
    a_flat = jnp.transpose(a_cum.reshape(-1, CHUNK, SSM_HEADS), (0, 2, 1)).reshape(-1, 1, SSM_INNER)
    d_x = jnp.repeat(w["ssm_d"], SSM_HEAD_DIM, axis=1)
    ind = _head_indicator()
    y_ssd, states, late_stacks = _ssd_fwd("ssd_fwd", pre, dt, a_cum, a_flat, d_x, ind, late_shards)
    w.update(_late_weights(late_stacks, late_shards))
    yb_pre = _gate_norm_fwd("gate_norm_fwd", y_ssd, z, w["ssm_norm_w"])
    y_b = _mm("proj_b", yb_pre, w["w_proj_b"], bm=1024, bn=1024, bk=SSM_INNER, out_dtype=BF16)

    ya_pre = _gmlp_fwd("gmlp_fwd", za, w["gmlp_ln_w"], w["gmlp_ln_b"], w["gmlp_ws"], bs_col)
    y_a = _mm("proj_a", ya_pre, w["w_proj_a"], **act16)

    merged = _merge_fwd("merge_fwd", gates, y_a, y_b, b0, b1)
    h1 = _mm("out_proj", merged, w["w_out"], res=x, **big)

    hn = _rms_fwd("ffn_norm", h1, w["ffn_norm_w"])
    half = dict(bm=1024, bn=D_FF // 2, bk=1024, out_dtype=BF16)
    pg = _mm("ffn_up_gate", hn, w["w_up_g"], **half)
    pv = _mm("ffn_up_val", hn, w["w_up_v"], **half)
    cw, cb = w["ffn_conv_w"], w["ffn_conv_b"]
    gate, val, act = _ffn_act_fwd("ffn_act_fwd", pg, pv, cw[:, :D_FF], cw[:, D_FF:], cb[:, :D_FF], cb[:, D_FF:],
                                  tc=D_FF // 2)
    h2 = _mm("ffn_down", act, w["w_down"], res=h1, bm=1024, bn=1024, bk=D_FF // 2)

    dh2, loss_part, g["final_norm_w"] = _final_loss("final_loss", h2, target, w["final_norm_w"].reshape(1, D_MODEL))

    dact = _mm("d_act", dh2, w["w_down"], tb=True, **half)
    wgrad = dict(ta=True, bk=min(2048, x.shape[0]), out_dtype=BF16)
    g["w_down"] = _mm("dw_down", act, dh2, bm=D_FF // 2, bn=1024, **wgrad)
    dgate, dval = _ffn_act_bwd("ffn_act_bwd", dact, gate, val)
    dpg, dcwg, dcbg = _conv_bwd("ffn_conv_bwd_gate", dgate, pg, cw[:, :D_FF], tc=D_FF // 2)
    dpv, dcwv, dcbv = _conv_bwd("ffn_conv_bwd_val", dval, pv, cw[:, D_FF:], tc=D_FF // 2)
    g["ffn_conv_w"] = jnp.concatenate([dcwg, dcwv], axis=1)
    g["ffn_conv_b"] = jnp.concatenate([dcbg, dcbv], axis=1)
    dhn = _mm_sum("d_hn", [(dpg, w["w_up_g"]), (dpv, w["w_up_v"])], bm=1024, bk=D_FF // 2)
    g["w_up_g"] = _mm("dw_up_gate", hn, dpg, bm=1024, bn=D_FF // 2, **wgrad)
    g["w_up_v"] = _mm("dw_up_val", hn, dpv, bm=1024, bn=D_FF // 2, **wgrad)
    dh1, g["ffn_norm_w"] = _rms_bwd("ffn_norm_bwd", h1, w["ffn_norm_w"], dhn, dh2)

    dmerged = _mm("d_merged", dh1, w["w_out"], tb=True, **act16)
    g["w_out"] = _mm("dw_out", merged, dh1, bm=1024, bn=1024, **wgrad)
    dgates, dya, dyb, db0, db1 = _merge_bwd("merge_bwd", gates, y_a, y_b, dmerged, b0, b1)
    g["gate_bias"] = jnp.concatenate([db0, db1], axis=0)

    dya_pre = _mm("d_ya_pre", dya, w["w_proj_a"], tb=True, **act16)
    g["w_proj_a"] = _mm("dw_proj_a", ya_pre, dya, bm=1024, bn=1024, **wgrad)
    dyb_pre = _mm("d_yb_pre", dyb, w["w_proj_b"], tb=True, **act16)
    g["w_proj_b"] = _mm("dw_proj_b", yb_pre, dyb, bm=1024, bn=1024, **wgrad)
    late_pairs = _pair_reduce("late", LATE, [
        g["w_proj_a"].reshape(N_CHIPS, -1, D_MODEL), g["w_proj_b"].reshape(N_CHIPS, -1, D_MODEL),
        g["w_out"].reshape(N_CHIPS, -1, D_MODEL),
        jnp.concatenate([_columns_to_chips(g["w_up_g"], 2), _columns_to_chips(g["w_up_v"], 2)], axis=0),
        g["w_down"].reshape(N_CHIPS, -1, D_MODEL)])

    dy_ssd, dz, g["ssm_norm_w"] = _gate_norm_bwd("gate_norm_bwd", y_ssd, z, dyb_pre, w["ssm_norm_w"])
    (dpre, ddt, da_tok, da_flat, dd), late_received = _ssd_bwd(
        "ssd_bwd", pre, dt, a_cum, a_flat, d_x, ind, ind.T, states, dy_ssd, late_pairs)
    g["ssm_d"] = dd[0:1]
    da_src = jnp.transpose(da_flat.reshape(-1, SSM_HEADS, CHUNK), (0, 2, 1)).reshape(-1, SSM_HEADS)
    ddt_raw, g["ssm_dt_bias"], g["ssm_a_log"] = _dt_bwd("dt_bwd", dt_raw, ddt, da_tok, da_src,
                                                         w["ssm_dt_bias"], w["ssm_a_log"])
    dxbc, g["ssm_conv_w"], g["ssm_conv_b"] = _conv_bwd("ssm_conv_bwd", dpre, xbc, w["ssm_conv_w"], tc=1024)

    dza, g["gmlp_ln_w"], g["gmlp_ln_b"], g["gmlp_ws"], dbs = _gmlp_bwd(
        "gmlp_bwd", za, dya_pre, w["gmlp_ln_w"], w["gmlp_ln_b"], w["gmlp_ws"], bs_col)
    g["gmlp_bs"] = dbs.reshape(GMLP_GROUPS, GMLP_BLOCK)

    dxn = _mm_sum("d_xn", [(dgates, w["w_g"]), (dza, w["w_za"]), (dz, w["w_z"]), (dxbc, w["w_xbc"]),
                           (ddt_raw, w["w_dt"])], bm=1024, bk=1024)
    g["w_g"] = _mm("dw_gates", xn, dgates, bm=1024, bn=1024, **wgrad)
    g["w_za"] = _mm("dw_gmlp", xn, dza, bm=1024, bn=1024, **wgrad)
    g["w_z"] = _mm("dw_z", xn, dz, bm=1024, bn=1024, **wgrad)
    g["w_xbc"] = _mm("dw_xbc", xn, dxbc, bm=1024, bn=1024, **wgrad)
    g["w_dt"] = _mm("dw_dt", xn, ddt_raw, bm=1024, bn=SSM_HEADS, **wgrad)
    grad_x, g["mix_norm_w"] = _rms_bwd("mix_norm_bwd", x, w["mix_norm_w"], dxn, dh1)
    return loss_part, grad_x, g, late_pairs, late_received


def _position():
    return lax.axis_index("x"), lax.axis_index("y"), lax.axis_index("c")


def _own_slot(stack, own):
    chip = 2 * lax.axis_index("x") + lax.axis_index("y")
    return lax.dynamic_update_index_in_dim(stack, own, chip, axis=0)


def _scatter_phases(ins, outs, send_sems, recv_sems):
    n = len(ins)
    x, y, c = _position()
    me = 2 * x + y
    peers = [(1 - x, y), (x, 1 - y), (1 - x, 1 - y)]

    def copy(i, k, src_slot, dst_slot):
        px, py = peers[k]
        return pltpu.make_async_remote_copy(
            src_ref=ins[i].at[src_slot], dst_ref=outs[i].at[dst_slot],
            send_sem=send_sems.at[i, k], recv_sem=recv_sems.at[i, k],
            device_id=(px, py, c), device_id_type=MESH)

    def start():
        for i in range(n):
            for k, (px, py) in enumerate(peers):
                copy(i, k, 2 * px + py, me).start()

    def finish():
        for i in range(n):
            for k, (px, py) in enumerate(peers):
                copy(i, k, me, 2 * px + py).wait_recv()
        for i in range(n):
            for k, (px, py) in enumerate(peers):
                copy(i, k, 2 * px + py, me).wait_send()

    return start, finish


def _exchange_scratch(n, per_array):
    return [pltpu.SemaphoreType.DMA((n, per_array)), pltpu.SemaphoreType.DMA((n, per_array))]


def _scatter_chips(name, arrs):
    n = len(arrs)

    def body(*refs):
        start, finish = _scatter_phases(refs[:n], refs[n:2 * n], *refs[2 * n:])
        start()
        finish()

    hbm = pl.BlockSpec(memory_space=pl.ANY)
    return pl.pallas_call(
        body, name=name,
        in_specs=[hbm] * n, out_specs=[hbm] * n,
        out_shape=[jax.ShapeDtypeStruct(a.shape, a.dtype) for a in arrs],
        scratch_shapes=_exchange_scratch(n, N_CHIPS - 1),
        compiler_params=pltpu.CompilerParams(has_side_effects=True),
    )(*arrs)


def _half_rows(ref_rows, which):
    half = ref_rows // 2
    return pl.ds(pl.multiple_of(which * half, 2 * SUBLANES), half)


GATHER_SEMS = 2 * (N_CHIPS - 1)


def _gather_phases(nrows, ns, ins, outs, send_sems, recv_sems):
    n = len(ins)
    x, y, c = _position()
    me = 2 * x + y
    sibling = (x, y, 1 - c)
    chips = [(1 - x, y), (x, 1 - y), (1 - x, 1 - y)]

    def remote(i, k, src, dst, to):
        return pltpu.make_async_remote_copy(src_ref=src, dst_ref=dst, send_sem=send_sems.at[i, k],
                                            recv_sem=recv_sems.at[i, k], device_id=to, device_id_type=MESH)

    def over_ici(i, k):
        px, py = chips[k]
        rows = _half_rows(nrows[i], c) if i < ns else slice(None)
        return remote(i, k, ins[i].at[rows], outs[i].at[me, rows], (px, py, c))

    def landed(i, k, which):
        px, py = chips[k]
        return outs[i].at[2 * px + py, _half_rows(nrows[i], which)] if i < ns else outs[i].at[2 * px + py]

    def start():
        for i in range(n):
            for k in range(N_CHIPS - 1):
                over_ici(i, k).start()

    def forward():
        for i in range(n):
            for k in range(N_CHIPS - 1):
                piece = landed(i, k, c)
                remote(i, k, piece, piece, (*chips[k], c)).wait_recv()
                if i < ns:
                    remote(i, N_CHIPS - 1 + k, piece, piece, sibling).start()

    def finish():
        for i in range(ns):
            for k in range(N_CHIPS - 1):
                piece = landed(i, k, 1 - c)
                remote(i, N_CHIPS - 1 + k, piece, piece, sibling).wait_recv()
        for i in range(n):
            for k in range(N_CHIPS - 1):
                over_ici(i, k).wait_send()
                if i < ns:
                    piece = landed(i, k, c)
                    remote(i, N_CHIPS - 1 + k, piece, piece, sibling).wait_send()

    return start, forward, finish


def _gather_chips_split(name, split, whole):
    arrs = list(split) + list(whole)
    n = len(arrs)

    def body(*refs):
        phases = _gather_phases([a.shape[0] for a in arrs], len(split), refs[:n], refs[n:2 * n], *refs[2 * n:])
        for phase in phases:
            phase()

    hbm = pl.BlockSpec(memory_space=pl.ANY)
    return pl.pallas_call(
        body, name=name, in_specs=[hbm] * n, out_specs=[hbm] * n,
        out_shape=[jax.ShapeDtypeStruct((N_CHIPS,) + a.shape, a.dtype) for a in arrs],
        scratch_shapes=_exchange_scratch(n, GATHER_SEMS),
        compiler_params=pltpu.CompilerParams(has_side_effects=True),
    )(*arrs)


def _swap_cores(name, arrs):
    n = len(arrs)

    def body(*refs):
        ins, outs = refs[:n], refs[n:2 * n]
        send_sems, recv_sems = refs[2 * n:]
        x, y, c = _position()
        copies = [pltpu.make_async_remote_copy(src_ref=ins[i], dst_ref=outs[i], send_sem=send_sems.at[i],
                                               recv_sem=recv_sems.at[i], device_id=(x, y, 1 - c), device_id_type=MESH)
                  for i in range(n)]
        for cp in copies:
            cp.start()
        for cp in copies:
            cp.wait_recv()
        for cp in copies:
            cp.wait_send()

    hbm = pl.BlockSpec(memory_space=pl.ANY)
    return pl.pallas_call(
        body, name=name, in_specs=[hbm] * n, out_specs=[hbm] * n,
        out_shape=[jax.ShapeDtypeStruct(a.shape, a.dtype) for a in arrs],
        scratch_shapes=[pltpu.SemaphoreType.DMA((n,)), pltpu.SemaphoreType.DMA((n,))],
        compiler_params=pltpu.CompilerParams(has_side_effects=True),
    )(*arrs)


def _row_half(a, which, axis):
    half = a.shape[axis] // 2
    return lax.dynamic_slice_in_dim(a, which * half, half, axis=axis)


def _all_reduce(name, pack):
    def body(in_ref, out_ref, buf, send_sems, recv_sems):
        x, y, c = _position()
        me = 4 * x + 2 * y + c
        flips = [(dx, dy, dc) for dx in (0, 1) for dy in (0, 1) for dc in (0, 1) if (dx, dy, dc) != (0, 0, 0)]
        peers = [((1 - x) if dx else x, (1 - y) if dy else y, (1 - c) if dc else c) for dx, dy, dc in flips]
        buf[me] = in_ref[...]
        sends = []
        for k, peer in enumerate(peers):
            cp = pltpu.make_async_remote_copy(src_ref=in_ref, dst_ref=buf.at[me], send_sem=send_sems.at[k],
                                              recv_sem=recv_sems.at[k], device_id=peer, device_id_type=MESH)
            cp.start()
            sends.append(cp)
        for k, (px, py, pc) in enumerate(peers):
            pltpu.make_async_remote_copy(src_ref=in_ref, dst_ref=buf.at[4 * px + 2 * py + pc], send_sem=send_sems.at[k],
                                         recv_sem=recv_sems.at[k], device_id=(px, py, pc), device_id_type=MESH).wait_recv()
        total = buf[0]
        for j in range(1, N_DEV):
            total = total + buf[j]
        out_ref[...] = total
        for cp in sends:
            cp.wait_send()

    vmem = pl.BlockSpec(memory_space=pltpu.VMEM)
    return pl.pallas_call(
        body, name=name, in_specs=[vmem], out_specs=vmem,
        out_shape=jax.ShapeDtypeStruct(pack.shape, F32),
        scratch_shapes=[pltpu.VMEM((N_DEV,) + pack.shape, F32), pltpu.SemaphoreType.DMA((N_DEV - 1,)),
                        pltpu.SemaphoreType.DMA((N_DEV - 1,))],
        compiler_params=pltpu.CompilerParams(has_side_effects=True, vmem_limit_bytes=VMEM_LIMIT_V7X),
    )(pack)


def _pack(arrs):
    rows = [a.reshape(-1, LANES) for a in arrs]
    total = sum(r.shape[0] for r in rows)
    rows.append(jnp.zeros((-total % SUBLANES, LANES), F32))
    return jnp.concatenate(rows, axis=0)


def _unpack(pack, shapes):
    out, off = [], 0
    for s in shapes:
        nrow = 1
        for d in s:
            nrow *= d
        nrow //= LANES
        out.append(pack[off:off + nrow].reshape(s))
        off += nrow
    return out


SMALL = ["mix_norm_w", "gate_bias", "gmlp_ln_w", "gmlp_ln_b", "gmlp_ws", "gmlp_bs", "ssm_conv_w", "ssm_conv_b",
         "ssm_dt_bias", "ssm_a_log", "ssm_d", "ssm_norm_w", "ffn_norm_w", "ffn_conv_w", "ffn_conv_b", "final_norm_w"]
SMALL_SHARDED = ("gate_bias", "ssm_conv_w", "ffn_conv_w")
BIG = ["w_in", "w_proj_a", "w_proj_b", "w_out", "ffn_w_up", "ffn_w_down"]
WEIGHTS = ["mix_norm_w", "w_in", "gate_bias", "gmlp_ln_w", "gmlp_ln_b", "gmlp_ws", "gmlp_bs", "ssm_conv_w",
           "ssm_conv_b", "ssm_dt_bias", "ssm_a_log", "ssm_d", "ssm_norm_w", "w_proj_a", "w_proj_b", "w_out",
           "ffn_norm_w", "ffn_w_up", "ffn_conv_w", "ffn_conv_b", "ffn_w_down", "final_norm_w"]
IN_SPLITS = [0, 2048, 4096, 6144, 9216, 9248]


def _columns_from_chips(stack):
    return jnp.transpose(stack, (1, 0, 2)).reshape(stack.shape[1], -1)


def _columns_to_chips(full, parts=N_CHIPS):
    rows, cols = full.shape
    return jnp.transpose(full.reshape(rows, parts, cols // parts), (1, 0, 2))


def kernel(x, mix_norm_w, w_in, gate_bias, gmlp_ln_w, gmlp_ln_b, gmlp_ws, gmlp_bs, ssm_conv_w, ssm_conv_b, ssm_dt_bias, ssm_a_log, ssm_d, ssm_norm_w, w_proj_a, w_proj_b, w_out, ffn_norm_w, ffn_w_up, ffn_conv_w, ffn_conv_b, ffn_w_down, final_norm_w, loss_target, m_mix_norm_w, m_w_in, m_gate_bias, m_gmlp_ln_w, m_gmlp_ln_b, m_gmlp_ws, m_gmlp_bs, m_ssm_conv_w, m_ssm_conv_b, m_ssm_dt_bias, m_ssm_a_log, m_ssm_d, m_ssm_norm_w, m_w_proj_a, m_w_proj_b, m_w_out, m_ffn_norm_w, m_ffn_w_up, m_ffn_conv_w, m_ffn_conv_b, m_ffn_w_down, m_final_norm_w, v_mix_norm_w, v_w_in, v_gate_bias, v_gmlp_ln_w, v_gmlp_ln_b, v_gmlp_ws, v_gmlp_bs, v_ssm_conv_w, v_ssm_conv_b, v_ssm_dt_bias, v_ssm_a_log, v_ssm_d, v_ssm_norm_w, v_w_proj_a, v_w_proj_b, v_w_out, v_ffn_norm_w, v_ffn_w_up, v_ffn_conv_w, v_ffn_conv_b, v_ffn_w_down, v_final_norm_w):
    args = dict(locals())
    weights = {n: args[n] for n in WEIGHTS}
    moments_m = {n: args["m_" + n] for n in WEIGHTS}
    moments_v = {n: args["v_" + n] for n in WEIGHTS}
    chip = 2 * lax.axis_index("x") + lax.axis_index("y")

    shards = [weights["w_in"][0].astype(BF16)] + [weights[n][0] for n in SMALL_SHARDED]
    gathered = _gather_chips_split("gather_weights", shards[:1], shards[1:])
    w_in_s, gb_s, scw_s, fcw_s = [_own_slot(stack, own) for stack, own in zip(gathered, shards)]
    late_shards = [weights[n][0].astype(BF16) for n in LATE]
    w_in_full = _columns_from_chips(w_in_s)
    full = {"w_" + nm: w_in_full[:, IN_SPLITS[k]:IN_SPLITS[k + 1]] for k, nm in enumerate(["g", "za", "z", "xbc", "dt"])}
    full["gate_bias"] = _columns_from_chips(gb_s)
    full["ssm_conv_w"] = _columns_from_chips(scw_s)
    full["ffn_conv_w"] = _columns_from_chips(fcw_s)
    for n in SMALL:
        if n not in SMALL_SHARDED:
            full[n] = weights[n] if n == "final_norm_w" else weights[n][0]
    for n in ("mix_norm_w", "ffn_norm_w", "ssm_conv_b", "ssm_dt_bias", "ssm_a_log", "ssm_d", "ssm_norm_w", "ffn_conv_b"):
        full[n] = full[n].reshape(1, -1)

    loss_part, grad_x, g, late_pairs, late_received = _local_step(x[0], loss_target[0], full, late_shards)

    per_head = ["ssm_dt_bias", "ssm_a_log", "ssm_d"]
    rest = [n for n in SMALL if n not in per_head]
    head_row = jnp.concatenate([g[n] for n in per_head] + [jnp.zeros((1, LANES - 3 * SSM_HEADS), F32)], axis=1)
    pack = _pack([loss_part, head_row] + [g[n] for n in rest])
    reduced = _unpack(_all_reduce("reduce_small", pack), [(1, LANES), (1, LANES)] + [g[n].shape for n in rest])
    loss = reduced[0][0, 0]
    small_grads = {n: reduced[1][:, k * SSM_HEADS:(k + 1) * SSM_HEADS] for k, n in enumerate(per_head)}
    for n, r in zip(rest, reduced[2:]):
        if n in SMALL_SHARDED:
            width = weights[n].shape[2]
            r = lax.dynamic_slice_in_dim(r, chip * width, width, axis=1)
        small_grads[n] = r
    two_d = lambda a: a.reshape(-1, a.shape[-1])
    upd = _adamw_small("adamw_small", *[[two_d(d[n]) for n in SMALL]
                                        for d in (weights, small_grads, moments_m, moments_v)])
    small_out = [[small_grads[n] for n in SMALL]] + list(upd)
    small_out = [[a.reshape(weights[n].shape) for n, a in zip(SMALL, kind)] for kind in small_out]

    dw_in = jnp.concatenate([g["w_g"], g["w_za"], g["w_z"], g["w_xbc"], g["w_dt"]], axis=1)
    in_pairs = _pair_reduce("in", ["w_in"], [_columns_to_chips(dw_in)])
    pair = in_pairs + list(late_pairs)
    received = list(_scatter_chips("scatter_grads", in_pairs)) + list(late_received)
    received = [_own_slot(r, lax.dynamic_index_in_dim(p, chip, 0, keepdims=False)) for r, p in zip(received, pair)]
    halves = [_sum_slots("sum_" + n, r, tm=HALF_TILES[n], rs=2 * SUBLANES) for n, r in zip(BIG, received)]
    tiles = {"w_in": 128, "w_proj_a": 256, "w_proj_b": 256, "w_out": 256, "ffn_w_up": 128, "ffn_w_down": 176}
    core = lax.axis_index("c")
    other = _swap_cores("join_grads", halves)
    reduced = [jnp.concatenate([jnp.where(core == 0, a, b), jnp.where(core == 0, b, a)], axis=0)
               for a, b in zip(halves, other)]
    big_out = {}
    for n, grad in zip(BIG, reduced):
        big_out[n] = _adamw("adamw_" + n, weights[n][0], grad, moments_m[n][0], moments_v[n][0],
                            tm=tiles[n], rs=SUBLANES)

    per_kind = [[], [], [], []]
    for n in WEIGHTS:
        for kind in range(4):
            if n in big_out:
                per_kind[kind].append(big_out[n][kind].reshape(weights[n].shape))
            else:
                per_kind[kind].append(small_out[kind][SMALL.index(n)])
    return (loss, grad_x[None], *per_kind[0], *per_kind[1], *per_kind[2], *per_kind[3])
```

```python
import jax
import jax.numpy as jnp
from jax import lax
from jax.experimental import pallas as pl
from jax.experimental.pallas import tpu as pltpu

F32 = jnp.float32
BF16 = jnp.bfloat16
MESH = pl.DeviceIdType.MESH

EPS = 1e-5
D_MODEL = 1024
GMLP_BLOCK = 128
GMLP_GROUPS = 8
CHUNK = 64
SSM_INNER = 2048
SSM_HEADS = 32
SSM_HEAD_DIM = 64
SSM_GROUPS = 4
SSM_HPG = 8
SSM_STATE = 128
SSM_CONV = 4
SSM_XBC = 3072
D_FF = 2816
FFN_CONV = 3
N_CHIPS = 4
N_DEV = 8

ADAM_LR = 0.001
ADAM_B1 = 0.9
ADAM_B2 = 0.999
ADAM_EPS = 1e-08
ADAM_WD = 0.01
ADAM_STEP = 10

VMEM_LIMIT_V7X = 56 * 1024 * 1024
SUBLANES = 8
LANES = 128


def _params(sem=None):
    return pltpu.CompilerParams(dimension_semantics=sem, vmem_limit_bytes=VMEM_LIMIT_V7X)


def _dot(a, b, ca=1, cb=0):
    return lax.dot_general(a.astype(BF16), b.astype(BF16), (((ca,), (cb,)), ((), ())),
                           preferred_element_type=F32)


def _mm(name, a, b, *, ta=False, tb=False, out_dtype=F32, bm, bn, bk, res=None):
    m, k = (a.shape[1], a.shape[0]) if ta else a.shape
    k2, n = (b.shape[1], b.shape[0]) if tb else b.shape
    assert k == k2 and m % bm == 0 and n % bn == 0 and k % bk == 0, (name, a.shape, b.shape)
    nk = k // bk
    a_spec = (pl.BlockSpec((bk, bm), lambda i, j, kk: (kk, i)) if ta
              else pl.BlockSpec((bm, bk), lambda i, j, kk: (i, kk)))
    b_spec = (pl.BlockSpec((bn, bk), lambda i, j, kk: (j, kk)) if tb
              else pl.BlockSpec((bk, bn), lambda i, j, kk: (kk, j)))
    o_spec = pl.BlockSpec((bm, bn), lambda i, j, kk: (i, j))
    has_res = res is not None

    def body(*refs):
        a_ref, b_ref = refs[0], refs[1]
        r_ref = refs[2] if has_res else None
        o_ref = refs[3] if has_res else refs[2]
        p = _dot(a_ref[...], b_ref[...], 0 if ta else 1, 1 if tb else 0)

        def finish(total):
            if has_res:
                total = total + r_ref[...]
            o_ref[...] = total.astype(out_dtype)

        if nk == 1:
            finish(p)
        else:
            acc_ref = refs[-1]
            kk = pl.program_id(2)

            @pl.when(kk == 0)
            def _():
                acc_ref[...] = p

            @pl.when(kk > 0)
            def _():
                acc_ref[...] += p

            @pl.when(kk == nk - 1)
            def _():
                finish(acc_ref[...])

    return pl.pallas_call(
        body, name=name,
        grid=(m // bm, n // bn, nk),
        in_specs=[a_spec, b_spec] + ([o_spec] if has_res else []),
        out_specs=o_spec,
        out_shape=jax.ShapeDtypeStruct((m, n), out_dtype),
        scratch_shapes=[pltpu.VMEM((bm, bn), F32)] if nk > 1 else [],
        compiler_params=_params(("parallel", "parallel", "arbitrary")),
    )(*([a, b] + ([res] if has_res else [])))


def _mm_sum(name, pairs, *, bm, bk, exchange=()):
    nx = len(exchange)
    npair = len(pairs)
    m, n = pairs[0][0].shape[0], pairs[0][1].shape[0]
    steps, first = [], []
    for a, b in pairs:
        k = a.shape[1]
        assert a.shape[0] == m and b.shape == (n, k) and m % bm == 0 and (k % bk == 0 or k < bk), (name, a.shape, b.shape)
        first.append(sum(steps))
        steps.append(max(k // bk, 1))
    total = sum(steps)
    in_specs = []
    for (a, b), off, cnt in zip(pairs, first, steps):
        width = min(bk, a.shape[1])
        in_specs.append(pl.BlockSpec((bm, width), lambda i, kk, off=off, cnt=cnt: (i, jnp.clip(kk - off, 0, cnt - 1))))
        in_specs.append(pl.BlockSpec((n, width), lambda i, kk, off=off, cnt=cnt: (0, jnp.clip(kk - off, 0, cnt - 1))))

    def body(*refs):
        send_refs = refs[2 * npair:2 * npair + nx]
        o_ref = refs[2 * npair + nx]
        recv_refs = refs[2 * npair + nx + 1:2 * npair + 2 * nx + 1]
        acc_ref = refs[2 * npair + 2 * nx + 1]
        i, kk = pl.program_id(0), pl.program_id(1)
        if nx:
            start, finish = _scatter_phases(send_refs, recv_refs, *refs[2 * npair + 2 * nx + 2:])

            @pl.when((i == 0) & (kk == 0))
            def _():
                start()

        for s, (off, cnt) in enumerate(zip(first, steps)):
            @pl.when((kk >= off) & (kk < off + cnt))
            def _(s=s, off=off):
                p = _dot(refs[2 * s][...], refs[2 * s + 1][...], 1, 1)
                if off == 0:
                    @pl.when(kk == 0)
                    def _():
                        acc_ref[...] = p

                    @pl.when(kk > 0)
                    def _():
                        acc_ref[...] += p
                else:
                    acc_ref[...] += p

        @pl.when(kk == total - 1)
        def _():
            o_ref[...] = acc_ref[...]

        if nx:
            @pl.when((i == m // bm - 1) & (kk == total - 1))
            def _():
                finish()

    hbm = pl.BlockSpec(memory_space=pl.ANY)
    res = pl.pallas_call(
        body, name=name, grid=(m // bm, total),
        in_specs=in_specs + [hbm] * nx, out_specs=[pl.BlockSpec((bm, n), lambda i, kk: (i, 0))] + [hbm] * nx,
        out_shape=[jax.ShapeDtypeStruct((m, n), F32)] + [jax.ShapeDtypeStruct(e.shape, e.dtype) for e in exchange],
        scratch_shapes=[pltpu.VMEM((bm, n), F32)] + (_exchange_scratch(nx, N_CHIPS - 1) if nx else []),
        compiler_params=_params(("arbitrary", "arbitrary")),
    )(*[t for pair in pairs for t in pair], *exchange)
    return (res[0], res[1:]) if nx else res[0]


def _rows(name, fn, ins, params, outs, accs, *, tm, rs, unroll=4):
    nrow = ins[0][0].shape[-2]
    while tm % (rs * unroll):
        unroll //= 2
    assert nrow % tm == 0 and tm % rs == 0, (name, nrow, tm, rs)
    n_in, n_p, n_out, n_acc = len(ins), len(params), len(outs), len(accs)
    in_specs = []
    for spec in ins:
        arr, width, cb = spec[:3]
        if len(spec) == 4:
            in_specs.append(pl.BlockSpec((None, tm, width), lambda i, cb=cb, lead=spec[3]: (lead, i, cb)))
        else:
            in_specs.append(pl.BlockSpec((tm, width), lambda i, cb=cb: (i, cb)))
    for p in params:
        in_specs.append(pl.BlockSpec(p.shape, lambda i, nd=p.ndim: (0,) * nd))
    out_specs = [pl.BlockSpec((tm, w), lambda i: (i, 0)) for w, _ in outs]
    out_specs += [pl.BlockSpec(s, lambda i: (0, 0)) for s in accs]
    out_shape = [jax.ShapeDtypeStruct((nrow, w), dt) for w, dt in outs]
    out_shape += [jax.ShapeDtypeStruct(s, F32) for s in accs]

    def body(*refs):
        in_refs = refs[:n_in]
        p_refs = refs[n_in:n_in + n_p]
        o_refs = refs[n_in + n_p:n_in + n_p + n_out]
        a_refs = refs[n_in + n_p + n_out:]
        pv = [p[...] for p in p_refs]

        if n_acc:
            @pl.when(pl.program_id(0) == 0)
            def _():
                for a_ref in a_refs:
                    a_ref[...] = jnp.zeros_like(a_ref)

        def step(r, carry):
            for u in range(unroll):
                sl = pl.ds(pl.multiple_of((r * unroll + u) * rs, rs), rs)
                vals = [ref[sl, :].astype(F32) for ref in in_refs]
                row_out, sums = fn(*vals, *pv)
                for o_ref, v in zip(o_refs, row_out):
                    o_ref[sl, :] = v.astype(o_ref.dtype)
                carry = tuple(c + s for c, s in zip(carry, sums))
            return carry

        init = tuple(jnp.zeros(s, F32) for s in accs)
        total = lax.fori_loop(0, tm // (rs * unroll), step, init)
        for a_ref, t in zip(a_refs, total):
            a_ref[...] += t

    res = pl.pallas_call(
        body, name=name, grid=(nrow // tm,),
        in_specs=in_specs, out_specs=out_specs, out_shape=out_shape,
        compiler_params=_params(("arbitrary",)),
    )(*([s[0] for s in ins] + list(params)))
    return res


def _rms(x, w):
    return x * lax.rsqrt(jnp.mean(x * x, axis=-1, keepdims=True) + EPS) * w


def _colsum(v):
    return jnp.sum(v, axis=0, keepdims=True)


def _rms_fwd(name, x, w):
    def fn(xv, wv):
        return (_rms(xv, wv),), ()
    return _rows(name, fn, [(x, D_MODEL, 0)], [w], [(D_MODEL, BF16)], [], tm=512, rs=16)[0]


def _rms_bwd(name, x, w, dy, dres):
    def fn(xv, dyv, drv, wv):
        _, vjp = jax.vjp(_rms, xv, wv)
        dx, dw = vjp(dyv)
        return (drv + dx,), (dw,)
    return _rows(name, fn, [(x, D_MODEL, 0), (dy, D_MODEL, 0), (dres, D_MODEL, 0)], [w],
                 [(D_MODEL, F32)], [(1, D_MODEL)], tm=512, rs=16)


def _final_loss(name, h, target, w):
    def fn(hv, tv, wv):
        y, vjp = jax.vjp(_rms, hv, wv)
        err = y - tv
        part = 0.5 * jnp.sum(jnp.mean(err * err, axis=-1, keepdims=True), axis=0, keepdims=True)
        dh, dw = vjp(err / D_MODEL)
        return (dh,), (jnp.broadcast_to(part, (1, LANES)), dw)
    return _rows(name, fn, [(h, D_MODEL, 0), (target, D_MODEL, 0)], [w],
                 [(D_MODEL, F32)], [(1, LANES), (1, D_MODEL)], tm=512, rs=16)


def _merge(ga, gb, ya, yb, b0, b1):
    return jax.nn.sigmoid(ga + b0) * ya + jax.nn.sigmoid(gb + b1) * yb


def _merge_fwd(name, g, ya, yb, b0, b1):
    def fn(ga, gb, yav, ybv, b0v, b1v):
        return (_merge(ga, gb, yav, ybv, b0v, b1v),), ()
    return _rows(name, fn, [(g, D_MODEL, 0), (g, D_MODEL, 1), (ya, D_MODEL, 0), (yb, D_MODEL, 0)],
                 [b0, b1], [(D_MODEL, BF16)], [], tm=512, rs=16)[0]


def _merge_bwd(name, g, ya, yb, dm, b0, b1):
    def fn(ga, gb, yav, ybv, dmv, b0v, b1v):
        _, vjp = jax.vjp(_merge, ga, gb, yav, ybv, b0v, b1v)
        dga, dgb, dya, dyb, db0, db1 = vjp(dmv)
        return (jnp.concatenate([dga, dgb], axis=1), dya, dyb), (db0, db1)
    return _rows(name, fn,
                 [(g, D_MODEL, 0), (g, D_MODEL, 1), (ya, D_MODEL, 0), (yb, D_MODEL, 0), (dm, D_MODEL, 0)],
                 [b0, b1], [(2 * D_MODEL, BF16), (D_MODEL, BF16), (D_MODEL, BF16)],
                 [(1, D_MODEL), (1, D_MODEL)], tm=512, rs=16)


GROUP_W = SSM_INNER // SSM_GROUPS


def _gate_norm_group(y, z, nw):
    v = y * jax.nn.silu(z)
    return v * lax.rsqrt(jnp.mean(v * v, axis=-1, keepdims=True) + EPS) * nw


def _gate_norm_fwd(name, y, z, nw):
    def fn(yv, zv, nwv):
        parts = [_gate_norm_group(yv[:, k * GROUP_W:(k + 1) * GROUP_W], zv[:, k * GROUP_W:(k + 1) * GROUP_W],
                                  nwv[:, k * GROUP_W:(k + 1) * GROUP_W]) for k in range(SSM_GROUPS)]
        return (jnp.concatenate(parts, axis=1),), ()
    return _rows(name, fn, [(y, SSM_INNER, 0), (z, SSM_INNER, 0)], [nw], [(SSM_INNER, BF16)], [],
                 tm=512, rs=16)[0]


def _gate_norm_bwd(name, y, z, dout, nw):
    def fn(yv, zv, dv, nwv):
        dys, dzs, dns = [], [], []
        for k in range(SSM_GROUPS):
            sl = slice(k * GROUP_W, (k + 1) * GROUP_W)
            _, vjp = jax.vjp(_gate_norm_group, yv[:, sl], zv[:, sl], nwv[:, sl])
            dy, dz, dn = vjp(dv[:, sl])
            dys.append(dy), dzs.append(dz), dns.append(dn)
        return (jnp.concatenate(dys, axis=1), jnp.concatenate(dzs, axis=1)), (jnp.concatenate(dns, axis=1),)
    return _rows(name, fn, [(y, SSM_INNER, 0), (z, SSM_INNER, 0), (dout, SSM_INNER, 0)], [nw],
                 [(SSM_INNER, F32), (SSM_INNER, BF16)], [(1, SSM_INNER)], tm=512, rs=16)


def _softplus(v):
    return jnp.maximum(v, 0.0) + jnp.log1p(jnp.exp(-jnp.abs(v)))


def _chunk_cumsum(v, reverse=False):
    row = lax.broadcasted_iota(jnp.int32, v.shape, 0)
    step = 1
    while step < CHUNK:
        if reverse:
            shifted = pltpu.roll(v, CHUNK - step, axis=0)
            v = v + jnp.where(row < CHUNK - step, shifted, 0.0)
        else:
            shifted = pltpu.roll(v, step, axis=0)
            v = v + jnp.where(row >= step, shifted, 0.0)
        step *= 2
    return v


def _dt_prep(name, dt_raw, dt_bias, a_log):
    def fn(rv, bv, alv):
        dt = _softplus(rv + bv)
        return (dt, _chunk_cumsum(dt * (-jnp.exp(alv)))), ()
    return _rows(name, fn, [(dt_raw, SSM_HEADS, 0)], [dt_bias, a_log],
                 [(SSM_HEADS, F32), (SSM_HEADS, F32)], [], tm=512, rs=CHUNK)


def _dt_bwd(name, dt_raw, ddt, da1, da2, dt_bias, a_log):
    def fn(rv, ddv, d1, d2, bv, alv):
        pre = rv + bv
        dt = _softplus(pre)
        a_neg = -jnp.exp(alv)
        back = _chunk_cumsum(d1 + d2, reverse=True)
        d_dt = ddv + back * a_neg
        d_raw = d_dt * jax.nn.sigmoid(pre)
        return (d_raw,), (_colsum(d_raw), _colsum(back * dt) * a_neg)
    return _rows(name, fn, [(dt_raw, SSM_HEADS, 0), (ddt, SSM_HEADS, 0), (da1, SSM_HEADS, 0), (da2, SSM_HEADS, 0)],
                 [dt_bias, a_log], [(SSM_HEADS, BF16)], [(1, SSM_HEADS), (1, SSM_HEADS)], tm=512, rs=CHUNK)


def _adamw_math(w, g, m, v):
    m_new = ADAM_B1 * m + (1.0 - ADAM_B1) * g
    v_new = ADAM_B2 * v + (1.0 - ADAM_B2) * jnp.square(g)
    m_hat = m_new / (1.0 - ADAM_B1 ** ADAM_STEP)
    v_hat = v_new / (1.0 - ADAM_B2 ** ADAM_STEP)
    delta = -ADAM_LR * (m_hat / (jnp.sqrt(v_hat) + ADAM_EPS) + ADAM_WD * w)
    return delta, m_new, v_new


def _adamw(name, w, g, m, v, *, tm, rs):
    width = w.shape[1]

    def fn(wv, mv, vv, gv):
        return (gv,) + _adamw_math(wv, gv, mv, vv), ()
    return _rows(name, fn, [(w, width, 0), (m, width, 0), (v, width, 0), (g, width, 0)],
                 [], [(width, F32)] * 4, [], tm=tm, rs=rs)


def _adamw_small(name, ws, gs, ms, vs):
    n = len(ws)

    def body(*refs):
        w_refs, g_refs, m_refs, v_refs = (refs[k * n:(k + 1) * n] for k in range(4))
        outs = refs[4 * n:]
        for i in range(n):
            res = _adamw_math(w_refs[i][...], g_refs[i][...], m_refs[i][...], v_refs[i][...])
            for k in range(3):
                outs[k * n + i][...] = res[k]

    vmem = pl.BlockSpec(memory_space=pltpu.VMEM)
    res = pl.pallas_call(
        body, name=name, in_specs=[vmem] * (4 * n), out_specs=[vmem] * (3 * n),
        out_shape=[jax.ShapeDtypeStruct(w.shape, F32) for w in ws] * 3,
        compiler_params=pltpu.CompilerParams(vmem_limit_bytes=VMEM_LIMIT_V7X),
    )(*ws, *gs, *ms, *vs)
    return res[:n], res[n:2 * n], res[2 * n:]


def _pair_sum(name, a, b, *, tm):
    shape = a.shape
    flat = (shape[0] * shape[1], shape[2])

    def fn(av, bv):
        return (av.astype(F32) + bv.astype(F32),), ()
    out = _rows(name, fn, [(a.reshape(flat), flat[1], 0), (b.reshape(flat), flat[1], 0)], [], [(flat[1], BF16)], [],
                tm=tm, rs=2 * SUBLANES)[0]
    return out.reshape(shape)


def _sum_slots(name, stack, *, tm, rs):
    width = stack.shape[2]

    def fn(*slots):
        s0, s1, s2, s3 = (s.astype(F32) for s in slots)
        return (((s0 + s1) + s2) + s3,), ()
    return _rows(name, fn, [(stack, width, 0, k) for k in range(N_CHIPS)], [], [(width, F32)], [],
                 tm=tm, rs=rs)[0]


def _layernorm(v, w, b):
    mu = jnp.mean(v, axis=-1, keepdims=True)
    var = jnp.mean(jnp.square(v - mu), axis=-1, keepdims=True)
    return (v - mu) * lax.rsqrt(var + EPS) * w + b


def _gmlp_mask():
    t = lax.broadcasted_iota(jnp.int32, (GMLP_BLOCK, GMLP_BLOCK), 0) // CHUNK
    s = lax.broadcasted_iota(jnp.int32, (GMLP_BLOCK, GMLP_BLOCK), 1) // CHUNK
    return s <= t


GMLP_TM = 512


def _gmlp_fwd(name, za, ln_w, ln_b, ws, bs_col):
    nrow = za.shape[0]
    tm = GMLP_TM
    width = GMLP_GROUPS * GMLP_BLOCK

    def body(za_ref, lnw_ref, lnb_ref, ws_ref, bs_ref, o_ref, wm_ref):
        mask = _gmlp_mask()
        for g in range(GMLP_GROUPS):
            wm_ref[g] = jnp.where(mask, ws_ref[g], 0.0).astype(BF16)

        def block(n, carry):
            rows = pl.ds(pl.multiple_of(n * GMLP_BLOCK, GMLP_BLOCK), GMLP_BLOCK)
            for g in range(GMLP_GROUPS):
                cols = slice(g * GMLP_BLOCK, (g + 1) * GMLP_BLOCK)
                vcols = slice(width + g * GMLP_BLOCK, width + (g + 1) * GMLP_BLOCK)
                u = jax.nn.gelu(za_ref[rows, cols].astype(F32))
                v = jax.nn.gelu(za_ref[rows, vcols].astype(F32))
                vn = _layernorm(v, lnw_ref[g:g + 1, :], lnb_ref[g:g + 1, :])
                sv = _dot(wm_ref[g], vn) + bs_ref[g]
                o_ref[rows, cols] = (u * sv).astype(o_ref.dtype)
            return carry

        lax.fori_loop(0, tm // GMLP_BLOCK, block, 0)

    small = lambda a: pl.BlockSpec(a.shape, lambda i, nd=a.ndim: (0,) * nd)
    return pl.pallas_call(
        body, name=name, grid=(nrow // tm,),
        in_specs=[pl.BlockSpec((tm, 2 * width), lambda i: (i, 0)), small(ln_w), small(ln_b), small(ws), small(bs_col)],
        out_specs=pl.BlockSpec((tm, width), lambda i: (i, 0)),
        out_shape=jax.ShapeDtypeStruct((nrow, width), BF16),
        scratch_shapes=[pltpu.VMEM((GMLP_GROUPS, GMLP_BLOCK, GMLP_BLOCK), BF16)],
        compiler_params=_params(("arbitrary",)),
    )(za, ln_w, ln_b, ws, bs_col)


def _gmlp_bwd(name, za, dout, ln_w, ln_b, ws, bs_col):
    nrow = za.shape[0]
    tm = GMLP_TM
    width = GMLP_GROUPS * GMLP_BLOCK

    def body(za_ref, do_ref, lnw_ref, lnb_ref, ws_ref, bs_ref, dza_ref, dlnw_ref, dlnb_ref, dws_ref, dbs_ref, wm_ref):
        mask = _gmlp_mask()
        for g in range(GMLP_GROUPS):
            wm_ref[g] = jnp.where(mask, ws_ref[g], 0.0).astype(BF16)

        @pl.when(pl.program_id(0) == 0)
        def _():
            dlnw_ref[...] = jnp.zeros_like(dlnw_ref)
            dlnb_ref[...] = jnp.zeros_like(dlnb_ref)
            dws_ref[...] = jnp.zeros_like(dws_ref)
            dbs_ref[...] = jnp.zeros_like(dbs_ref)

        def block(n, carry):
            rows = pl.ds(pl.multiple_of(n * GMLP_BLOCK, GMLP_BLOCK), GMLP_BLOCK)
            for g in range(GMLP_GROUPS):
                cols = slice(g * GMLP_BLOCK, (g + 1) * GMLP_BLOCK)
                vcols = slice(width + g * GMLP_BLOCK, width + (g + 1) * GMLP_BLOCK)
                u, gelu_u_vjp = jax.vjp(jax.nn.gelu, za_ref[rows, cols].astype(F32))
                v, gelu_v_vjp = jax.vjp(jax.nn.gelu, za_ref[rows, vcols].astype(F32))
                vn, ln_vjp = jax.vjp(_layernorm, v, lnw_ref[g:g + 1, :], lnb_ref[g:g + 1, :])
                sv = _dot(wm_ref[g], vn) + bs_ref[g]
                d_o = do_ref[rows, cols].astype(F32)
                dsv = d_o * u
                d_wm = _dot(dsv, vn, 1, 1)
                dvn = _dot(wm_ref[g], dsv, 0, 0)
                dv, dlnw, dlnb = ln_vjp(dvn)
                dza_ref[rows, cols] = gelu_u_vjp(d_o * sv)[0].astype(dza_ref.dtype)
                dza_ref[rows, vcols] = gelu_v_vjp(dv)[0].astype(dza_ref.dtype)
                dlnw_ref[g:g + 1, :] += dlnw
                dlnb_ref[g:g + 1, :] += dlnb
                dws_ref[g] += jnp.where(mask, d_wm, 0.0)
                dbs_ref[g] += jnp.sum(dsv, axis=1, keepdims=True)
            return carry

        lax.fori_loop(0, tm // GMLP_BLOCK, block, 0)

    small = lambda a: pl.BlockSpec(a.shape, lambda i, nd=a.ndim: (0,) * nd)
    return pl.pallas_call(
        body, name=name, grid=(nrow // tm,),
        in_specs=[pl.BlockSpec((tm, 2 * width), lambda i: (i, 0)), pl.BlockSpec((tm, width), lambda i: (i, 0)),
                  small(ln_w), small(ln_b), small(ws), small(bs_col)],
        out_specs=[pl.BlockSpec((tm, 2 * width), lambda i: (i, 0)), small(ln_w), small(ln_b), small(ws), small(bs_col)],
        out_shape=[jax.ShapeDtypeStruct((nrow, 2 * width), BF16), jax.ShapeDtypeStruct(ln_w.shape, F32),
                   jax.ShapeDtypeStruct(ln_b.shape, F32), jax.ShapeDtypeStruct(ws.shape, F32),
                   jax.ShapeDtypeStruct(bs_col.shape, F32)],
        scratch_shapes=[pltpu.VMEM((GMLP_GROUPS, GMLP_BLOCK, GMLP_BLOCK), BF16)],
        compiler_params=_params(("arbitrary",)),
    )(za, dout, ln_w, ln_b, ws, bs_col)


CONV_TM = 256
CONV_RS = 32
HALO = 2 * SUBLANES


def _tap_rows(w_ref):
    return [w_ref[k:k + 1, :] for k in range(w_ref.shape[0])]


def _halo_specs(nrow, tm, tc):
    per = tm // HALO
    last = nrow // HALO - 1
    main = pl.BlockSpec((tm, tc), lambda j, i: (i, j))
    before = pl.BlockSpec((HALO, tc), lambda j, i: (jnp.maximum(i * per - 1, 0), j))
    after = pl.BlockSpec((HALO, tc), lambda j, i: (jnp.minimum((i + 1) * per, last), j))
    return main, before, after


def _col_spec(rows, tc):
    return pl.BlockSpec((rows, tc), lambda j, i: (0, j))


def _conv_fwd(name, x, w, b, *, tc):
    nrow, ncol = x.shape
    taps = w.shape[0]
    tm, rs = CONV_TM, CONV_RS
    main, before, _ = _halo_specs(nrow, tm, tc)

    def body(x_ref, xb_ref, w_ref, b_ref, o_ref, xw_ref):
        first = pl.program_id(1) == 0
        wv, bv = _tap_rows(w_ref), b_ref[...]
        xw_ref[0:HALO, :] = jnp.where(first, 0.0, xb_ref[...].astype(F32))
        for r in range(tm // rs):
            xw_ref[HALO + r * rs:HALO + (r + 1) * rs, :] = x_ref[r * rs:(r + 1) * rs, :].astype(F32)
        for r in range(tm // rs):
            base = HALO + r * rs
            out = bv + wv[taps - 1] * xw_ref[base:base + rs, :]
            for k in range(taps - 1):
                back = taps - 1 - k
                out = out + wv[k] * xw_ref[base - back:base - back + rs, :]
            o_ref[r * rs:(r + 1) * rs, :] = out.astype(o_ref.dtype)

    return pl.pallas_call(
        body, name=name, grid=(ncol // tc, nrow // tm),
        in_specs=[main, before, _col_spec(taps, tc), _col_spec(1, tc)],
        out_specs=main, out_shape=jax.ShapeDtypeStruct((nrow, ncol), BF16),
        scratch_shapes=[pltpu.VMEM((HALO + tm, tc), F32)],
        compiler_params=_params(("parallel", "arbitrary")),
    )(x, x, w, b)


def _conv_bwd(name, dpre, x, w, *, tc):
    nrow, ncol = x.shape
    taps = w.shape[0]
    tm, rs = CONV_TM, CONV_RS
    nsub = tm // rs
    main, before, after = _halo_specs(nrow, tm, tc)

    def fold(v):
        total = v[0:SUBLANES]
        for q in range(1, rs // SUBLANES):
            total = total + v[q * SUBLANES:(q + 1) * SUBLANES]
        return total

    def body(d_ref, da_ref, x_ref, xb_ref, w_ref, dx_ref, dw_ref, db_ref, dwin_ref, xwin_ref):
        i = pl.program_id(1)
        first, last = i == 0, i == pl.num_programs(1) - 1
        wv = _tap_rows(w_ref)

        @pl.when(first)
        def _():
            dw_ref[...] = jnp.zeros_like(dw_ref)
            db_ref[...] = jnp.zeros_like(db_ref)

        xwin_ref[0:HALO, :] = jnp.where(first, 0.0, xb_ref[...].astype(F32))
        dwin_ref[tm:, :] = jnp.where(last, 0.0, da_ref[...].astype(F32))
        for r in range(nsub):
            dwin_ref[r * rs:(r + 1) * rs, :] = d_ref[r * rs:(r + 1) * rs, :].astype(F32)
            xwin_ref[HALO + r * rs:HALO + (r + 1) * rs, :] = x_ref[r * rs:(r + 1) * rs, :].astype(F32)
        dw = [jnp.zeros((SUBLANES, tc), F32)] * taps
        db = jnp.zeros((SUBLANES, tc), F32)
        for r in range(nsub):
            cur = dwin_ref[r * rs:(r + 1) * rs, :]
            dx = wv[taps - 1] * cur
            for k in range(taps - 1):
                ahead = taps - 1 - k
                dx = dx + wv[k] * dwin_ref[r * rs + ahead:(r + 1) * rs + ahead, :]
            dx_ref[r * rs:(r + 1) * rs, :] = dx.astype(dx_ref.dtype)
            for k in range(taps):
                back = taps - 1 - k
                dw[k] = dw[k] + fold(cur * xwin_ref[HALO + r * rs - back:HALO + (r + 1) * rs - back, :])
            db = db + fold(cur)
        for k in range(taps):
            dw_ref[k:k + 1, :] += _colsum(dw[k])
        db_ref[...] += _colsum(db)

    return pl.pallas_call(
        body, name=name, grid=(ncol // tc, nrow // tm),
        in_specs=[main, after, main, before, _col_spec(taps, tc)],
        out_specs=[main, _col_spec(taps, tc), _col_spec(1, tc)],
        out_shape=[jax.ShapeDtypeStruct((nrow, ncol), BF16), jax.ShapeDtypeStruct((taps, ncol), F32),
                   jax.ShapeDtypeStruct((1, ncol), F32)],
        scratch_shapes=[pltpu.VMEM((tm + HALO, tc), F32), pltpu.VMEM((HALO + tm, tc), F32)],
        compiler_params=_params(("parallel", "arbitrary")),
    )(dpre, dpre, x, x, w)


def _glu(gate, val):
    return jax.nn.silu(gate) * val


def _ffn_act_fwd(name, pg, pv, wg, wv, bg, bv, *, tc):
    nrow, ncol = pg.shape
    taps = wg.shape[0]
    tm, rs = CONV_TM, CONV_RS
    main, before, _ = _halo_specs(nrow, tm, tc)

    def body(pg_ref, pgb_ref, pv_ref, pvb_ref, wg_ref, wv_ref, bg_ref, bv_ref, g_ref, v_ref, a_ref, gwin_ref, vwin_ref):
        first = pl.program_id(1) == 0
        taps_g, taps_v, bgv, bvv = _tap_rows(wg_ref), _tap_rows(wv_ref), bg_ref[...], bv_ref[...]
        gwin_ref[0:HALO, :] = jnp.where(first, 0.0, pgb_ref[...].astype(F32))
        vwin_ref[0:HALO, :] = jnp.where(first, 0.0, pvb_ref[...].astype(F32))
        for r in range(tm // rs):
            gwin_ref[HALO + r * rs:HALO + (r + 1) * rs, :] = pg_ref[r * rs:(r + 1) * rs, :].astype(F32)
            vwin_ref[HALO + r * rs:HALO + (r + 1) * rs, :] = pv_ref[r * rs:(r + 1) * rs, :].astype(F32)

        def conv(win_ref, tap_rows, bias, r):
            base = HALO + r * rs
            out = bias + tap_rows[taps - 1] * win_ref[base:base + rs, :]
            for k in range(taps - 1):
                back = taps - 1 - k
                out = out + tap_rows[k] * win_ref[base - back:base - back + rs, :]
            return out

        for r in range(tm // rs):
            sl = slice(r * rs, (r + 1) * rs)
            gate, val = conv(gwin_ref, taps_g, bgv, r), conv(vwin_ref, taps_v, bvv, r)
            g_ref[sl, :] = gate.astype(g_ref.dtype)
            v_ref[sl, :] = val.astype(v_ref.dtype)
            a_ref[sl, :] = _glu(gate, val).astype(a_ref.dtype)

    return pl.pallas_call(
        body, name=name, grid=(ncol // tc, nrow // tm),
        in_specs=[main, before, main, before, _col_spec(taps, tc), _col_spec(taps, tc), _col_spec(1, tc), _col_spec(1, tc)],
        out_specs=[main, main, main],
        out_shape=[jax.ShapeDtypeStruct((nrow, ncol), BF16)] * 3,
        scratch_shapes=[pltpu.VMEM((HALO + tm, tc), F32), pltpu.VMEM((HALO + tm, tc), F32)],
        compiler_params=_params(("parallel", "arbitrary")),
    )(pg, pg, pv, pv, wg, wv, bg, bv)


def _ffn_act_bwd(name, dact, gate, val):
    def fn(dv, gv, vv):
        _, vjp = jax.vjp(_glu, gv, vv)
        dg, dval = vjp(dv)
        return (dg, dval), ()
    width = dact.shape[1]
    return _rows(name, fn, [(dact, width, 0), (gate, width, 0), (val, width, 0)], [],
                 [(width, BF16), (width, BF16)], [], tm=256, rs=2 * SUBLANES)


SSD_TM = 256
SSD_CHUNKS = SSD_TM // CHUNK
X_OFF, B_OFF, C_OFF = 0, SSM_INNER, SSM_INNER + SSM_GROUPS * SSM_STATE
HP = SSM_HPG * SSM_HEAD_DIM


def _causal_tiled():
    row = lax.broadcasted_iota(jnp.int32, (CHUNK, HP), 0)
    src = lax.broadcasted_iota(jnp.int32, (CHUNK, HP), 1) & (CHUNK - 1)
    return src <= row


def _split2(v):
    hi = v.astype(BF16)
    return hi, (v - hi.astype(F32)).astype(BF16)


def _dot_exact(a, ind):
    hi, lo = (lax.dot_general(p, ind, (((1,), (0,)), ((), ())), preferred_element_type=F32) for p in _split2(a))
    return hi + lo


def _head_indicator():
    head = lax.broadcasted_iota(jnp.int32, (SSM_HEADS, SSM_INNER), 0)
    chan = lax.broadcasted_iota(jnp.int32, (SSM_HEADS, SSM_INNER), 1)
    return (chan // SSM_HEAD_DIM == head).astype(BF16)


def _chunk_decays(ci, dt_ref, ac_ref, ind, ax_ref, dtx_ref, eax_ref, eex_ref, tail_ref):
    rows = pl.ds(pl.multiple_of(ci * CHUNK, CHUNK), CHUNK)
    ax_ref[...] = _dot_exact(ac_ref[rows, :], ind)
    dtx_ref[...] = _dot_exact(dt_ref[rows, :], ind)
    eax_ref[...] = jnp.exp(ax_ref[...])
    eex_ref[...] = jnp.exp(ax_ref[CHUNK - 1:CHUNK, :] - ax_ref[...])
    tail = pl.ds(pl.multiple_of(ci * CHUNK + CHUNK - SUBLANES, SUBLANES), SUBLANES)
    tail_ref[...] = jnp.exp(ac_ref[tail, :])


def _group_decay(ci, g, ax_ref, af_ref, xbc_ref, causal):
    gcols = slice(g * HP, (g + 1) * HP)
    bm = xbc_ref[:, B_OFF + g * SSM_STATE:B_OFF + (g + 1) * SSM_STATE]
    cm = xbc_ref[:, C_OFF + g * SSM_STATE:C_OFF + (g + 1) * SSM_STATE]
    cb_tiled = _dot(cm, jnp.concatenate([bm] * SSM_HPG, axis=0), 1, 1)
    seg = ax_ref[:, gcols] - af_ref[ci, :, gcols]
    decay = jnp.where(causal, jnp.exp(jnp.where(causal, seg, 0.0)), 0.0)
    return bm, cm, cb_tiled * decay, decay


def _ssd_fwd(name, pre, dt, a_cum, a_flat, d_x, ind, shards):
    nrow = pre.shape[0]
    tm = SSD_TM
    nstep = nrow // tm
    ng = len(shards)

    def body(pre_ref, dt_ref, ac_ref, af_ref, dx_ref, ind_ref, *rest):
        shard_refs, (y_ref, st_ref), stack_refs = rest[:ng], rest[ng:ng + 2], rest[ng + 2:2 * ng + 2]
        (h_ref, xbc_ref, ax_ref, dtx_ref, eax_ref, eex_ref, m_ref, xd_ref, yd_ref, tail_ref,
         send_sems, recv_sems) = rest[2 * ng + 2:]
        step = pl.program_id(0)
        start, forward, finish = _gather_phases([s.shape[0] for s in shards], ng, shard_refs, stack_refs,
                                                send_sems, recv_sems)

        @pl.when(step == 0)
        def _():
            h_ref[...] = jnp.zeros_like(h_ref)
            start()

        @pl.when(step == nstep // 2)
        def _():
            forward()

        causal = _causal_tiled()
        ind = ind_ref[...]

        def chunk(ci, carry):
            rows = pl.ds(pl.multiple_of(ci * CHUNK, CHUNK), CHUNK)
            xbc_ref[...] = jax.nn.silu(pre_ref[rows, :].astype(F32))
            _chunk_decays(ci, dt_ref, ac_ref, ind, ax_ref, dtx_ref, eax_ref, eex_ref, tail_ref)
            st_ref[ci] = h_ref[...].astype(st_ref.dtype)
            for g in range(SSM_GROUPS):
                gcols = slice(g * HP, (g + 1) * HP)
                bm, cm, m_all, _ = _group_decay(ci, g, ax_ref, af_ref, xbc_ref, causal)
                m_ref[...] = m_all
                x_g = xbc_ref[:, gcols]
                xd = x_g * dtx_ref[:, gcols]
                xd_ref[...] = xd
                h_g = h_ref[gcols, :]
                for hh in range(SSM_HPG):
                    lc = slice(hh * SSM_HEAD_DIM, (hh + 1) * SSM_HEAD_DIM)
                    yd_ref[:, lc] = _dot(m_ref[:, lc], xd_ref[:, lc])
                y_ref[rows, gcols] = (yd_ref[...] + _dot(cm, h_g, 1, 1) * eax_ref[:, gcols]
                                      + dx_ref[:, gcols] * x_g)
                new = _dot(xd * eex_ref[:, gcols], bm, 0, 0)
                for hh in range(SSM_HPG):
                    h = g * SSM_HPG + hh
                    hrows = slice(h * SSM_HEAD_DIM, (h + 1) * SSM_HEAD_DIM)
                    lrows = slice(hh * SSM_HEAD_DIM, (hh + 1) * SSM_HEAD_DIM)
                    h_ref[hrows, :] = tail_ref[SUBLANES - 1:SUBLANES, h:h + 1] * h_ref[hrows, :] + new[lrows, :]
            return carry

        lax.fori_loop(0, SSD_CHUNKS, chunk, 0)

        @pl.when(step == nstep - 1)
        def _():
            finish()

    nchunk = nrow // CHUNK
    whole = lambda a: pl.BlockSpec(a.shape, lambda i, nd=a.ndim: (0,) * nd)
    hbm = pl.BlockSpec(memory_space=pl.ANY)
    wide = lambda: pltpu.VMEM((CHUNK, SSM_INNER), F32)
    group = lambda: pltpu.VMEM((CHUNK, HP), F32)
    res = pl.pallas_call(
        body, name=name, grid=(nstep,),
        in_specs=[pl.BlockSpec((tm, SSM_XBC), lambda i: (i, 0)), pl.BlockSpec((tm, SSM_HEADS), lambda i: (i, 0)),
                  pl.BlockSpec((tm, SSM_HEADS), lambda i: (i, 0)),
                  pl.BlockSpec((SSD_CHUNKS, 1, SSM_INNER), lambda i: (i, 0, 0)), whole(d_x), whole(ind)] + [hbm] * ng,
        out_specs=[pl.BlockSpec((tm, SSM_INNER), lambda i: (i, 0)),
                   pl.BlockSpec((SSD_CHUNKS, SSM_INNER, SSM_STATE), lambda i: (i, 0, 0))] + [hbm] * ng,
        out_shape=[jax.ShapeDtypeStruct((nrow, SSM_INNER), F32),
                   jax.ShapeDtypeStruct((nchunk, SSM_INNER, SSM_STATE), BF16)]
        + [jax.ShapeDtypeStruct((N_CHIPS,) + s.shape, s.dtype) for s in shards],
        scratch_shapes=[pltpu.VMEM((SSM_INNER, SSM_STATE), F32), pltpu.VMEM((CHUNK, SSM_XBC), F32),
                        wide(), wide(), wide(), wide(), group(), group(), group(),
                        pltpu.VMEM((SUBLANES, SSM_HEADS), F32)] + _exchange_scratch(ng, GATHER_SEMS),
        compiler_params=_params(("arbitrary",)),
    )(pre, dt, a_cum, a_flat, d_x, ind, *shards)
    return res[0], res[1], res[2:]


def _ssd_bwd(name, pre, dt, a_cum, a_flat, d_x, ind, ind_t, states, dy, pairs):
    nrow = pre.shape[0]
    tm = SSD_TM
    ntile = nrow // tm
    npair = len(pairs)

    def body(pre_ref, dt_ref, ac_ref, af_ref, dx_ref, ind_ref, indt_ref, st_ref, dy_ref, *rest):
        pair_refs = rest[:npair]
        dpre_ref, ddt_ref, da_ref, daf_ref, dd_ref = rest[npair:npair + 5]
        recv_refs = rest[npair + 5:2 * npair + 5]
        (dh_ref, xbc_ref, dxbc_ref, ax_ref, dtx_ref, eax_ref, eex_ref, red_ref,
         m_ref, l_ref, xd_ref, dm_ref, dxd_ref, fold_ref, hd_ref, tail_ref, send_sems, recv_sems) = rest[2 * npair + 5:]
        start, finish = _scatter_phases(pair_refs, recv_refs, send_sems, recv_sems)

        @pl.when(pl.program_id(0) == 0)
        def _():
            dh_ref[...] = jnp.zeros_like(dh_ref)
            dd_ref[...] = jnp.zeros_like(dd_ref)
            start()

        causal = _causal_tiled()
        ind, ind_t = ind_ref[...], indt_ref[...]
        is_last_row = lax.broadcasted_iota(jnp.int32, (CHUNK, 1), 0) == CHUNK - 1
        ones = jnp.ones((CHUNK, SSM_STATE), BF16)

        def chunk(k, ddx):
            ci = SSD_CHUNKS - 1 - k
            rows = pl.ds(pl.multiple_of(ci * CHUNK, CHUNK), CHUNK)
            pre_v = pre_ref[rows, :].astype(F32)
            xbc_ref[...] = jax.nn.silu(pre_v)
            _chunk_decays(ci, dt_ref, ac_ref, ind, ax_ref, dtx_ref, eax_ref, eex_ref, tail_ref)
            ddx_parts = []
            for g in range(SSM_GROUPS):
                gcols = slice(g * HP, (g + 1) * HP)
                bcols = slice(B_OFF + g * SSM_STATE, B_OFF + (g + 1) * SSM_STATE)
                ccols = slice(C_OFF + g * SSM_STATE, C_OFF + (g + 1) * SSM_STATE)
                bm, cm, m_all, decay = _group_decay(ci, g, ax_ref, af_ref, xbc_ref, causal)
                m_ref[...] = m_all
                l_ref[...] = decay
                x_g = xbc_ref[:, gcols]
                xd = x_g * dtx_ref[:, gcols]
                xd_ref[...] = xd
                h_g = st_ref[ci, gcols, :]
                dh_g = dh_ref[gcols, :]
                dy_g = dy_ref[rows, gcols]
                for hh in range(SSM_HPG):
                    h = g * SSM_HPG + hh
                    hcols = slice(h * SSM_HEAD_DIM, (h + 1) * SSM_HEAD_DIM)
                    lc = slice(hh * SSM_HEAD_DIM, (hh + 1) * SSM_HEAD_DIM)
                    dy_h = dy_ref[rows, hcols]
                    dm_ref[:, lc] = _dot(dy_h, xd_ref[:, lc], 1, 1)
                    dxd_ref[:, lc] = _dot(m_ref[:, lc], dy_h, 0, 0)
                ebdh = eex_ref[:, gcols] * _dot(bm, dh_g, 1, 1)
                dxd = dxd_ref[...] + ebdh
                dm = dm_ref[...]
                t = dm * l_ref[...]
                t128 = (t[:, 0:LANES] + t[:, LANES:2 * LANES]) + (t[:, 2 * LANES:3 * LANES] + t[:, 3 * LANES:])
                fold_ref[...] = t128 + pltpu.roll(t128, CHUNK, axis=1)
                dw_sum = fold_ref[:, 0:CHUNK]
                q = dm * m_ref[...]
                dyea = dy_g * eax_ref[:, gcols]
                red_ref[0:CHUNK, gcols] = q + dyea * _dot(cm, h_g, 1, 1)
                red_ref[CHUNK:2 * CHUNK, gcols] = xd * ebdh
                red_ref[2 * CHUNK:3 * CHUNK, gcols] = dxd * x_g
                daf_ref[ci, :, gcols] = -jnp.sum(q, axis=0, keepdims=True)
                ddx_parts.append(jnp.sum(dy_g * x_g, axis=0, keepdims=True))
                dxbc_ref[:, gcols] = dxd * dtx_ref[:, gcols] + dx_ref[:, gcols] * dy_g
                dxbc_ref[:, ccols] = _dot(dw_sum, bm) + _dot(dyea, h_g)
                dxbc_ref[:, bcols] = _dot(dw_sum, cm, 0, 0) + _dot(xd * eex_ref[:, gcols], dh_g)
                dh_new = _dot(dyea, cm, 0, 0)
                for hh in range(SSM_HPG):
                    h = g * SSM_HPG + hh
                    hrows = slice(h * SSM_HEAD_DIM, (h + 1) * SSM_HEAD_DIM)
                    lrows = slice(hh * SSM_HEAD_DIM, (hh + 1) * SSM_HEAD_DIM)
                    hd_ref[h:h + 1, :] = jnp.sum(st_ref[ci, hrows, :] * dh_ref[hrows, :], axis=0, keepdims=True)
                    dh_ref[hrows, :] = tail_ref[SUBLANES - 1:SUBLANES, h:h + 1] * dh_ref[hrows, :] + dh_new[lrows, :]
            sums = _dot_exact(red_ref[...], ind_t)
            ra, ts = sums[:CHUNK], sums[CHUNK:2 * CHUNK]
            hdh = sum(lax.dot_general(ones, p, (((1,), (1,)), ((), ())), preferred_element_type=F32)
                      for p in _split2(hd_ref[...]))
            da_last = jnp.sum(ts, axis=0, keepdims=True) + tail_ref[SUBLANES - 1:SUBLANES, :] * hdh
            da_ref[rows, :] = ra - ts + jnp.where(is_last_row, da_last, 0.0)
            ddt_ref[rows, :] = sums[2 * CHUNK:]
            sig = jax.nn.sigmoid(pre_v)
            dpre_ref[rows, :] = (dxbc_ref[...] * (sig * (1.0 + pre_v * (1.0 - sig)))).astype(dpre_ref.dtype)
            return ddx + jnp.concatenate(ddx_parts, axis=1)

        ddx = lax.fori_loop(0, SSD_CHUNKS, chunk, jnp.zeros((1, SSM_INNER), F32))
        dd_ref[...] += _dot_exact(jnp.broadcast_to(ddx, (SUBLANES, SSM_INNER)), ind_t)

        @pl.when(pl.program_id(0) == ntile - 1)
        def _():
            finish()

    rev = lambda i: ntile - 1 - i
    whole = lambda a: pl.BlockSpec(a.shape, lambda i, nd=a.ndim: (0,) * nd)
    hbm = pl.BlockSpec(memory_space=pl.ANY)
    wide = lambda: pltpu.VMEM((CHUNK, SSM_INNER), F32)
    group = lambda: pltpu.VMEM((CHUNK, HP), F32)
    res = pl.pallas_call(
        body, name=name, grid=(ntile,),
        in_specs=[pl.BlockSpec((tm, SSM_XBC), lambda i: (rev(i), 0)), pl.BlockSpec((tm, SSM_HEADS), lambda i: (rev(i), 0)),
                  pl.BlockSpec((tm, SSM_HEADS), lambda i: (rev(i), 0)),
                  pl.BlockSpec((SSD_CHUNKS, 1, SSM_INNER), lambda i: (rev(i), 0, 0)),
                  whole(d_x), whole(ind), whole(ind_t),
                  pl.BlockSpec((SSD_CHUNKS, SSM_INNER, SSM_STATE), lambda i: (rev(i), 0, 0)),
                  pl.BlockSpec((tm, SSM_INNER), lambda i: (rev(i), 0))] + [hbm] * npair,
        out_specs=[pl.BlockSpec((tm, SSM_XBC), lambda i: (rev(i), 0)), pl.BlockSpec((tm, SSM_HEADS), lambda i: (rev(i), 0)),
                   pl.BlockSpec((tm, SSM_HEADS), lambda i: (rev(i), 0)),
                   pl.BlockSpec((SSD_CHUNKS, 1, SSM_INNER), lambda i: (rev(i), 0, 0)),
                   pl.BlockSpec((SUBLANES, SSM_HEADS), lambda i: (0, 0))] + [hbm] * npair,
        out_shape=[jax.ShapeDtypeStruct((nrow, SSM_XBC), BF16), jax.ShapeDtypeStruct((nrow, SSM_HEADS), F32),
                   jax.ShapeDtypeStruct((nrow, SSM_HEADS), F32), jax.ShapeDtypeStruct((nrow // CHUNK, 1, SSM_INNER), F32),
                   jax.ShapeDtypeStruct((SUBLANES, SSM_HEADS), F32)]
        + [jax.ShapeDtypeStruct(p.shape, p.dtype) for p in pairs],
        scratch_shapes=[pltpu.VMEM((SSM_INNER, SSM_STATE), F32), pltpu.VMEM((CHUNK, SSM_XBC), F32),
                        pltpu.VMEM((CHUNK, SSM_XBC), F32), wide(), wide(), wide(), wide(),
                        pltpu.VMEM((3 * CHUNK, SSM_INNER), F32),
                        group(), group(), group(), group(), group(), pltpu.VMEM((CHUNK, LANES), F32),
                        pltpu.VMEM((SSM_HEADS, SSM_STATE), F32), pltpu.VMEM((SUBLANES, SSM_HEADS), F32)]
        + _exchange_scratch(npair, N_CHIPS - 1),
        compiler_params=_params(("arbitrary",)),
    )(pre, dt, a_cum, a_flat, d_x, ind, ind_t, states, dy, *pairs)
    return res[:5], res[5:]


LATE = ["w_proj_a", "w_proj_b", "w_out", "ffn_w_up", "ffn_w_down"]
HALF_TILES = {"w_in": 128, "w_proj_a": 128, "w_proj_b": 256, "w_out": 128, "ffn_w_up": 128, "ffn_w_down": 176}


def _late_weights(stacks, shards):
    pa, pb, out, up, down = [_own_slot(stack, own) for stack, own in zip(stacks, shards)]
    return {"w_proj_a": pa.reshape(-1, D_MODEL), "w_proj_b": pb.reshape(-1, D_MODEL), "w_out": out.reshape(-1, D_MODEL),
            "w_up_g": _columns_from_chips(up[:2]), "w_up_v": _columns_from_chips(up[2:]),
            "w_down": down.reshape(-1, D_MODEL)}


def _pair_reduce(tag, names, stacks):
    core = lax.axis_index("c")
    own_half = [_row_half(s, core, 1) for s in stacks]
    other_half = _swap_cores("pair_grads_" + tag, [_row_half(s, 1 - core, 1) for s in stacks])
    return [_pair_sum("pair_" + n, a, b, tm=HALF_TILES[n]) for n, a, b in zip(names, own_half, other_half)]


def _local_step(x, target, w, late_shards):
    w = dict(w)
    g = {}
    bs_col = w["gmlp_bs"].reshape(GMLP_GROUPS, GMLP_BLOCK, 1)
    b0, b1 = w["gate_bias"][0:1], w["gate_bias"][1:2]

    xn = _rms_fwd("mix_norm", x, w["mix_norm_w"])
    big = dict(bm=1024, bn=1024, bk=1024)
    act16 = dict(out_dtype=BF16, **big)
    gates = _mm("in_gates", xn, w["w_g"], **act16)
    za = _mm("in_gmlp", xn, w["w_za"], **act16)
    z = _mm("in_z", xn, w["w_z"], **act16)
    xbc = _mm("in_xbc", xn, w["w_xbc"], **act16)
    dt_raw = _mm("in_dt", xn, w["w_dt"], bm=1024, bn=SSM_HEADS, bk=1024)

    pre = _conv_fwd("ssm_conv_fwd", xbc, w["ssm_conv_w"], w["ssm_conv_b"], tc=1024)
    dt, a_cum = _dt_prep("dt_prep", dt_raw, w["ssm_dt_bias"], w["ssm_a_log"])
    a_flat = jnp.transpose(a_cum.reshape(-1, CHUNK, SSM_HEADS), (0, 2, 1)).reshape(-1, 1, SSM_INNER)
    d_x = jnp.repeat(w["ssm_d"], SSM_HEAD_DIM, axis=1)
    ind = _head_indicator()
    y_ssd, states, late_stacks = _ssd_fwd("ssd_fwd", pre, dt, a_cum, a_flat, d_x, ind, late_shards)
    w.update(_late_weights(late_stacks, late_shards))
    yb_pre = _gate_norm_fwd("gate_norm_fwd", y_ssd, z, w["ssm_norm_w"])
    y_b = _mm("proj_b", yb_pre, w["w_proj_b"], bm=1024, bn=1024, bk=SSM_INNER, out_dtype=BF16)

    ya_pre = _gmlp_fwd("gmlp_fwd", za, w["gmlp_ln_w"], w["gmlp_ln_b"], w["gmlp_ws"], bs_col)
    y_a = _mm("proj_a", ya_pre, w["w_proj_a"], **act16)

    merged = _merge_fwd("merge_fwd", gates, y_a, y_b, b0, b1)
    h1 = _mm("out_proj", merged, w["w_out"], res=x, **big)

    hn = _rms_fwd("ffn_norm", h1, w["ffn_norm_w"])
    half = dict(bm=1024, bn=D_FF // 2, bk=1024, out_dtype=BF16)
    pg = _mm("ffn_up_gate", hn, w["w_up_g"], **half)
    pv = _mm("ffn_up_val", hn, w["w_up_v"], **half)
    cw, cb = w["ffn_conv_w"], w["ffn_conv_b"]
    gate, val, act = _ffn_act_fwd("ffn_act_fwd", pg, pv, cw[:, :D_FF], cw[:, D_FF:], cb[:, :D_FF], cb[:, D_FF:],
                                  tc=D_FF // 2)
    h2 = _mm("ffn_down", act, w["w_down"], res=h1, bm=1024, bn=1024, bk=D_FF // 2)

    dh2, loss_part, g["final_norm_w"] = _final_loss("final_loss", h2, target, w["final_norm_w"].reshape(1, D_MODEL))

    dact = _mm("d_act", dh2, w["w_down"], tb=True, **half)
    wgrad = dict(ta=True, bk=min(2048, x.shape[0]), out_dtype=BF16)
    g["w_down"] = _mm("dw_down", act, dh2, bm=D_FF // 2, bn=1024, **wgrad)
    dgate, dval = _ffn_act_bwd("ffn_act_bwd", dact, gate, val)
    dpg, dcwg, dcbg = _conv_bwd("ffn_conv_bwd_gate", dgate, pg, cw[:, :D_FF], tc=D_FF // 2)
    dpv, dcwv, dcbv = _conv_bwd("ffn_conv_bwd_val", dval, pv, cw[:, D_FF:], tc=D_FF // 2)
    g["ffn_conv_w"] = jnp.concatenate([dcwg, dcwv], axis=1)
    g["ffn_conv_b"] = jnp.concatenate([dcbg, dcbv], axis=1)
    dhn = _mm_sum("d_hn", [(dpg, w["w_up_g"]), (dpv, w["w_up_v"])], bm=1024, bk=D_FF // 2)
    g["w_up_g"] = _mm("dw_up_gate", hn, dpg, bm=1024, bn=D_FF // 2, **wgrad)
    g["w_up_v"] = _mm("dw_up_val", hn, dpv, bm=1024, bn=D_FF // 2, **wgrad)
    dh1, g["ffn_norm_w"] = _rms_bwd("ffn_norm_bwd", h1, w["ffn_norm_w"], dhn, dh2)

    dmerged = _mm("d_merged", dh1, w["w_out"], tb=True, **act16)
    g["w_out"] = _mm("dw_out", merged, dh1, bm=1024, bn=1024, **wgrad)
    dgates, dya, dyb, db0, db1 = _merge_bwd("merge_bwd", gates, y_a, y_b, dmerged, b0, b1)
    g["gate_bias"] = jnp.concatenate([db0, db1], axis=0)

    dya_pre = _mm("d_ya_pre", dya, w["w_proj_a"], tb=True, **act16)
    g["w_proj_a"] = _mm("dw_proj_a", ya_pre, dya, bm=1024, bn=1024, **wgrad)
    dyb_pre = _mm("d_yb_pre", dyb, w["w_proj_b"], tb=True, **act16)
    g["w_proj_b"] = _mm("dw_proj_b", yb_pre, dyb, bm=1024, bn=1024, **wgrad)
    late_pairs = _pair_reduce("late", LATE, [
        g["w_proj_a"].reshape(N_CHIPS, -1, D_MODEL), g["w_proj_b"].reshape(N_CHIPS, -1, D_MODEL),
        g["w_out"].reshape(N_CHIPS, -1, D_MODEL),
        jnp.concatenate([_columns_to_chips(g["w_up_g"], 2), _columns_to_chips(g["w_up_v"], 2)], axis=0),
        g["w_down"].reshape(N_CHIPS, -1, D_MODEL)])

    dy_ssd, dz, g["ssm_norm_w"] = _gate_norm_bwd("gate_norm_bwd", y_ssd, z, dyb_pre, w["ssm_norm_w"])
    (dpre, ddt, da_tok, da_flat, dd), late_received = _ssd_bwd(
        "ssd_bwd", pre, dt, a_cum, a_flat, d_x, ind, ind.T, states, dy_ssd, late_pairs)
    g["ssm_d"] = dd[0:1]
    da_src = jnp.transpose(da_flat.reshape(-1, SSM_HEADS, CHUNK), (0, 2, 1)).reshape(-1, SSM_HEADS)
    ddt_raw, g["ssm_dt_bias"], g["ssm_a_log"] = _dt_bwd("dt_bwd", dt_raw, ddt, da_tok, da_src,
                                                         w["ssm_dt_bias"], w["ssm_a_log"])
    dxbc, g["ssm_conv_w"], g["ssm_conv_b"] = _conv_bwd("ssm_conv_bwd", dpre, xbc, w["ssm_conv_w"], tc=1024)

    dza, g["gmlp_ln_w"], g["gmlp_ln_b"], g["gmlp_ws"], dbs = _gmlp_bwd(
        "gmlp_bwd", za, dya_pre, w["gmlp_ln_w"], w["gmlp_ln_b"], w["gmlp_ws"], bs_col)
    g["gmlp_bs"] = dbs.reshape(GMLP_GROUPS, GMLP_BLOCK)

    dw_in = jnp.concatenate([
        _mm("dw_gates", xn, dgates, bm=1024, bn=1024, **wgrad), _mm("dw_gmlp", xn, dza, bm=1024, bn=1024, **wgrad),
        _mm("dw_z", xn, dz, bm=1024, bn=1024, **wgrad), _mm("dw_xbc", xn, dxbc, bm=1024, bn=1024, **wgrad),
        _mm("dw_dt", xn, ddt_raw, bm=1024, bn=SSM_HEADS, **wgrad)], axis=1)
    in_pairs = _pair_reduce("in", ["w_in"], [_columns_to_chips(dw_in)])
    dxn, in_received = _mm_sum("d_xn", [(dgates, w["w_g"]), (dza, w["w_za"]), (dz, w["w_z"]), (dxbc, w["w_xbc"]),
                                        (ddt_raw, w["w_dt"])], bm=1024, bk=1024, exchange=in_pairs)
    grad_x, g["mix_norm_w"] = _rms_bwd("mix_norm_bwd", x, w["mix_norm_w"], dxn, dh1)
    return loss_part, grad_x, g, in_pairs + list(late_pairs), list(in_received) + list(late_received)


def _position():
    return lax.axis_index("x"), lax.axis_index("y"), lax.axis_index("c")


def _own_slot(stack, own):
    chip = 2 * lax.axis_index("x") + lax.axis_index("y")
    return lax.dynamic_update_index_in_dim(stack, own, chip, axis=0)


def _scatter_phases(ins, outs, send_sems, recv_sems):
    n = len(ins)
    x, y, c = _position()
    me = 2 * x + y
    peers = [(1 - x, y), (x, 1 - y), (1 - x, 1 - y)]

    def copy(i, k, src_slot, dst_slot):
        px, py = peers[k]
        return pltpu.make_async_remote_copy(
            src_ref=ins[i].at[src_slot], dst_ref=outs[i].at[dst_slot],
            send_sem=send_sems.at[i, k], recv_sem=recv_sems.at[i, k],
            device_id=(px, py, c), device_id_type=MESH)

    def start():
        for i in range(n):
            for k, (px, py) in enumerate(peers):
                copy(i, k, 2 * px + py, me).start()

    def finish():
        for i in range(n):
            for k, (px, py) in enumerate(peers):
                copy(i, k, me, 2 * px + py).wait_recv()
        for i in range(n):
            for k, (px, py) in enumerate(peers):
                copy(i, k, 2 * px + py, me).wait_send()

    return start, finish


def _exchange_scratch(n, per_array):
    return [pltpu.SemaphoreType.DMA((n, per_array)), pltpu.SemaphoreType.DMA((n, per_array))]


def _half_rows(ref_rows, which):
    half = ref_rows // 2
    return pl.ds(pl.multiple_of(which * half, 2 * SUBLANES), half)


GATHER_SEMS = 2 * (N_CHIPS - 1)


def _gather_phases(nrows, ns, ins, outs, send_sems, recv_sems):
    n = len(ins)
    x, y, c = _position()
    me = 2 * x + y
    sibling = (x, y, 1 - c)
    chips = [(1 - x, y), (x, 1 - y), (1 - x, 1 - y)]

    def remote(i, k, src, dst, to):
        return pltpu.make_async_remote_copy(src_ref=src, dst_ref=dst, send_sem=send_sems.at[i, k],
                                            recv_sem=recv_sems.at[i, k], device_id=to, device_id_type=MESH)

    def over_ici(i, k):
        px, py = chips[k]
        rows = _half_rows(nrows[i], c) if i < ns else slice(None)
        return remote(i, k, ins[i].at[rows], outs[i].at[me, rows], (px, py, c))

    def landed(i, k, which):
        px, py = chips[k]
        return outs[i].at[2 * px + py, _half_rows(nrows[i], which)] if i < ns else outs[i].at[2 * px + py]

    def start():
        for i in range(n):
            for k in range(N_CHIPS - 1):
                over_ici(i, k).start()

    def forward():
        for i in range(n):
            for k in range(N_CHIPS - 1):
                piece = landed(i, k, c)
                remote(i, k, piece, piece, (*chips[k], c)).wait_recv()
                if i < ns:
                    remote(i, N_CHIPS - 1 + k, piece, piece, sibling).start()

    def finish():
        for i in range(ns):
            for k in range(N_CHIPS - 1):
                piece = landed(i, k, 1 - c)
                remote(i, N_CHIPS - 1 + k, piece, piece, sibling).wait_recv()
        for i in range(n):
            for k in range(N_CHIPS - 1):
                over_ici(i, k).wait_send()
                if i < ns:
                    piece = landed(i, k, c)
                    remote(i, N_CHIPS - 1 + k, piece, piece, sibling).wait_send()

    return start, forward, finish


def _gather_chips_split(name, split, whole):
    arrs = list(split) + list(whole)
    n = len(arrs)

    def body(*refs):
        phases = _gather_phases([a.shape[0] for a in arrs], len(split), refs[:n], refs[n:2 * n], *refs[2 * n:])
        for phase in phases:
            phase()

    hbm = pl.BlockSpec(memory_space=pl.ANY)
    return pl.pallas_call(
        body, name=name, in_specs=[hbm] * n, out_specs=[hbm] * n,
        out_shape=[jax.ShapeDtypeStruct((N_CHIPS,) + a.shape, a.dtype) for a in arrs],
        scratch_shapes=_exchange_scratch(n, GATHER_SEMS),
        compiler_params=pltpu.CompilerParams(has_side_effects=True),
    )(*arrs)


def _swap_cores(name, arrs):
    n = len(arrs)

    def body(*refs):
        ins, outs = refs[:n], refs[n:2 * n]
        send_sems, recv_sems = refs[2 * n:]
        x, y, c = _position()
        copies = [pltpu.make_async_remote_copy(src_ref=ins[i], dst_ref=outs[i], send_sem=send_sems.at[i],
                                               recv_sem=recv_sems.at[i], device_id=(x, y, 1 - c), device_id_type=MESH)
                  for i in range(n)]
        for cp in copies:
            cp.start()
        for cp in copies:
            cp.wait_recv()
        for cp in copies:
            cp.wait_send()

    hbm = pl.BlockSpec(memory_space=pl.ANY)
    return pl.pallas_call(
        body, name=name, in_specs=[hbm] * n, out_specs=[hbm] * n,
        out_shape=[jax.ShapeDtypeStruct(a.shape, a.dtype) for a in arrs],
        scratch_shapes=[pltpu.SemaphoreType.DMA((n,)), pltpu.SemaphoreType.DMA((n,))],
        compiler_params=pltpu.CompilerParams(has_side_effects=True),
    )(*arrs)


def _row_half(a, which, axis):
    half = a.shape[axis] // 2
    return lax.dynamic_slice_in_dim(a, which * half, half, axis=axis)


def _all_reduce(name, pack):
    def body(in_ref, out_ref, buf, send_sems, recv_sems):
        x, y, c = _position()
        me = 4 * x + 2 * y + c
        flips = [(dx, dy, dc) for dx in (0, 1) for dy in (0, 1) for dc in (0, 1) if (dx, dy, dc) != (0, 0, 0)]
        peers = [((1 - x) if dx else x, (1 - y) if dy else y, (1 - c) if dc else c) for dx, dy, dc in flips]
        buf[me] = in_ref[...]
        sends = []
        for k, peer in enumerate(peers):
            cp = pltpu.make_async_remote_copy(src_ref=in_ref, dst_ref=buf.at[me], send_sem=send_sems.at[k],
                                              recv_sem=recv_sems.at[k], device_id=peer, device_id_type=MESH)
            cp.start()
            sends.append(cp)
        for k, (px, py, pc) in enumerate(peers):
            pltpu.make_async_remote_copy(src_ref=in_ref, dst_ref=buf.at[4 * px + 2 * py + pc], send_sem=send_sems.at[k],
                                         recv_sem=recv_sems.at[k], device_id=(px, py, pc), device_id_type=MESH).wait_recv()
        total = buf[0]
        for j in range(1, N_DEV):
            total = total + buf[j]
        out_ref[...] = total
        for cp in sends:
            cp.wait_send()

    vmem = pl.BlockSpec(memory_space=pltpu.VMEM)
    return pl.pallas_call(
        body, name=name, in_specs=[vmem], out_specs=vmem,
        out_shape=jax.ShapeDtypeStruct(pack.shape, F32),
        scratch_shapes=[pltpu.VMEM((N_DEV,) + pack.shape, F32), pltpu.SemaphoreType.DMA((N_DEV - 1,)),
                        pltpu.SemaphoreType.DMA((N_DEV - 1,))],
        compiler_params=pltpu.CompilerParams(has_side_effects=True, vmem_limit_bytes=VMEM_LIMIT_V7X),
    )(pack)


def _pack(arrs):
    rows = [a.reshape(-1, LANES) for a in arrs]
    total = sum(r.shape[0] for r in rows)
    rows.append(jnp.zeros((-total % SUBLANES, LANES), F32))
    return jnp.concatenate(rows, axis=0)


def _unpack(pack, shapes):
    out, off = [], 0
    for s in shapes:
        nrow = 1
        for d in s:
            nrow *= d
        nrow //= LANES
        out.append(pack[off:off + nrow].reshape(s))
        off += nrow
    return out


SMALL = ["mix_norm_w", "gate_bias", "gmlp_ln_w", "gmlp_ln_b", "gmlp_ws", "gmlp_bs", "ssm_conv_w", "ssm_conv_b",
         "ssm_dt_bias", "ssm_a_log", "ssm_d", "ssm_norm_w", "ffn_norm_w", "ffn_conv_w", "ffn_conv_b", "final_norm_w"]
SMALL_SHARDED = ("gate_bias", "ssm_conv_w", "ffn_conv_w")
BIG = ["w_in", "w_proj_a", "w_proj_b", "w_out", "ffn_w_up", "ffn_w_down"]
WEIGHTS = ["mix_norm_w", "w_in", "gate_bias", "gmlp_ln_w", "gmlp_ln_b", "gmlp_ws", "gmlp_bs", "ssm_conv_w",
           "ssm_conv_b", "ssm_dt_bias", "ssm_a_log", "ssm_d", "ssm_norm_w", "w_proj_a", "w_proj_b", "w_out",
           "ffn_norm_w", "ffn_w_up", "ffn_conv_w", "ffn_conv_b", "ffn_w_down", "final_norm_w"]
IN_SPLITS = [0, 2048, 4096, 6144, 9216, 9248]


def _columns_from_chips(stack):
    return jnp.transpose(stack, (1, 0, 2)).reshape(stack.shape[1], -1)


def _columns_to_chips(full, parts=N_CHIPS):
    rows, cols = full.shape
    return jnp.transpose(full.reshape(rows, parts, cols // parts), (1, 0, 2))


def kernel(x, mix_norm_w, w_in, gate_bias, gmlp_ln_w, gmlp_ln_b, gmlp_ws, gmlp_bs, ssm_conv_w, ssm_conv_b, ssm_dt_bias, ssm_a_log, ssm_d, ssm_norm_w, w_proj_a, w_proj_b, w_out, ffn_norm_w, ffn_w_up, ffn_conv_w, ffn_conv_b, ffn_w_down, final_norm_w, loss_target, m_mix_norm_w, m_w_in, m_gate_bias, m_gmlp_ln_w, m_gmlp_ln_b, m_gmlp_ws, m_gmlp_bs, m_ssm_conv_w, m_ssm_conv_b, m_ssm_dt_bias, m_ssm_a_log, m_ssm_d, m_ssm_norm_w, m_w_proj_a, m_w_proj_b, m_w_out, m_ffn_norm_w, m_ffn_w_up, m_ffn_conv_w, m_ffn_conv_b, m_ffn_w_down, m_final_norm_w, v_mix_norm_w, v_w_in, v_gate_bias, v_gmlp_ln_w, v_gmlp_ln_b, v_gmlp_ws, v_gmlp_bs, v_ssm_conv_w, v_ssm_conv_b, v_ssm_dt_bias, v_ssm_a_log, v_ssm_d, v_ssm_norm_w, v_w_proj_a, v_w_proj_b, v_w_out, v_ffn_norm_w, v_ffn_w_up, v_ffn_conv_w, v_ffn_conv_b, v_ffn_w_down, v_final_norm_w):
    args = dict(locals())
    weights = {n: args[n] for n in WEIGHTS}
    moments_m = {n: args["m_" + n] for n in WEIGHTS}
    moments_v = {n: args["v_" + n] for n in WEIGHTS}
    chip = 2 * lax.axis_index("x") + lax.axis_index("y")

    shards = [weights["w_in"][0].astype(BF16)] + [weights[n][0] for n in SMALL_SHARDED]
    gathered = _gather_chips_split("gather_weights", shards[:1], shards[1:])
    w_in_s, gb_s, scw_s, fcw_s = [_own_slot(stack, own) for stack, own in zip(gathered, shards)]
    late_shards = [weights[n][0].astype(BF16) for n in LATE]
    w_in_full = _columns_from_chips(w_in_s)
    full = {"w_" + nm: w_in_full[:, IN_SPLITS[k]:IN_SPLITS[k + 1]] for k, nm in enumerate(["g", "za", "z", "xbc", "dt"])}
    full["gate_bias"] = _columns_from_chips(gb_s)
    full["ssm_conv_w"] = _columns_from_chips(scw_s)
    full["ffn_conv_w"] = _columns_from_chips(fcw_s)
    for n in SMALL:
        if n not in SMALL_SHARDED:
            full[n] = weights[n] if n == "final_norm_w" else weights[n][0]
    for n in ("mix_norm_w", "ffn_norm_w", "ssm_conv_b", "ssm_dt_bias", "ssm_a_log", "ssm_d", "ssm_norm_w", "ffn_conv_b"):
        full[n] = full[n].reshape(1, -1)

    loss_part, grad_x, g, pair, received = _local_step(x[0], loss_target[0], full, late_shards)

    per_head = ["ssm_dt_bias", "ssm_a_log", "ssm_d"]
    rest = [n for n in SMALL if n not in per_head]
    head_row = jnp.concatenate([g[n] for n in per_head] + [jnp.zeros((1, LANES - 3 * SSM_HEADS), F32)], axis=1)
    pack = _pack([loss_part, head_row] + [g[n] for n in rest])
    reduced = _unpack(_all_reduce("reduce_small", pack), [(1, LANES), (1, LANES)] + [g[n].shape for n in rest])
    loss = reduced[0][0, 0]
    small_grads = {n: reduced[1][:, k * SSM_HEADS:(k + 1) * SSM_HEADS] for k, n in enumerate(per_head)}
    for n, r in zip(rest, reduced[2:]):
        if n in SMALL_SHARDED:
            width = weights[n].shape[2]
            r = lax.dynamic_slice_in_dim(r, chip * width, width, axis=1)
        small_grads[n] = r
    two_d = lambda a: a.reshape(-1, a.shape[-1])
    upd = _adamw_small("adamw_small", *[[two_d(d[n]) for n in SMALL]
                                        for d in (weights, small_grads, moments_m, moments_v)])
    small_out = [[small_grads[n] for n in SMALL]] + list(upd)
    small_out = [[a.reshape(weights[n].shape) for n, a in zip(SMALL, kind)] for kind in small_out]

    received = [_own_slot(r, lax.dynamic_index_in_dim(p, chip, 0, keepdims=False)) for r, p in zip(received, pair)]
    halves = [_sum_slots("sum_" + n, r, tm=HALF_TILES[n], rs=2 * SUBLANES) for n, r in zip(BIG, received)]
    tiles = {"w_in": 128, "w_proj_a": 256, "w_proj_b": 256, "w_out": 256, "ffn_w_up": 128, "ffn_w_down": 176}
    core = lax.axis_index("c")
    other = _swap_cores("join_grads", halves)
    reduced = [jnp.concatenate([jnp.where(core == 0, a, b), jnp.where(core == 0, b, a)], axis=0)
               for a, b in zip(halves, other)]
    big_out = {}
    for n, grad in zip(BIG, reduced):
        big_out[n] = _adamw("adamw_" + n, weights[n][0], grad, moments_m[n][0], moments_v[n][0],
                            tm=tiles[n], rs=SUBLANES)

    per_kind = [[], [], [], []]
    for n in WEIGHTS:
        for kind in range(4):
            if n in big_out:
                per_kind[kind].append(big_out[n][kind].reshape(weights[n].shape))
            else:
                per_kind[kind].append(small_out[kind][SMALL.index(n)])
    return (loss, grad_x[None], *per_kind[0], *per_kind[1], *per_kind[2], *per_kind[3])
```

```python
import jax
import jax.numpy as jnp
from jax import lax
from jax.experimental import pallas as pl
from jax.experimental.pallas import tpu as pltpu

F32 = jnp.float32
BF16 = jnp.bfloat16
MESH = pl.DeviceIdType.MESH

EPS = 1e-5
D_MODEL = 1024
GMLP_BLOCK = 128
GMLP_GROUPS = 8
CHUNK = 64
SSM_INNER = 2048
SSM_HEADS = 32
SSM_HEAD_DIM = 64
SSM_GROUPS = 4
SSM_HPG = 8
SSM_STATE = 128
SSM_CONV = 4
SSM_XBC = 3072
D_FF = 2816
FFN_CONV = 3
N_CHIPS = 4
N_DEV = 8

ADAM_LR = 0.001
ADAM_B1 = 0.9
ADAM_B2 = 0.999
ADAM_EPS = 1e-08
ADAM_WD = 0.01
ADAM_STEP = 10

VMEM_LIMIT_V7X = 56 * 1024 * 1024
SUBLANES = 8
LANES = 128


def _params(sem=None):
    return pltpu.CompilerParams(dimension_semantics=sem, vmem_limit_bytes=VMEM_LIMIT_V7X)


def _dot(a, b, ca=1, cb=0):
    return lax.dot_general(a.astype(BF16), b.astype(BF16), (((ca,), (cb,)), ((), ())),
                           preferred_element_type=F32)


def _mm(name, a, b, *, ta=False, tb=False, out_dtype=F32, bm, bn, bk, res=None):
    m, k = (a.shape[1], a.shape[0]) if ta else a.shape
    k2, n = (b.shape[1], b.shape[0]) if tb else b.shape
    assert k == k2 and m % bm == 0 and n % bn == 0 and k % bk == 0, (name, a.shape, b.shape)
    nk = k // bk
    a_spec = (pl.BlockSpec((bk, bm), lambda i, j, kk: (kk, i)) if ta
              else pl.BlockSpec((bm, bk), lambda i, j, kk: (i, kk)))
    b_spec = (pl.BlockSpec((bn, bk), lambda i, j, kk: (j, kk)) if tb
              else pl.BlockSpec((bk, bn), lambda i, j, kk: (kk, j)))
    o_spec = pl.BlockSpec((bm, bn), lambda i, j, kk: (i, j))
    has_res = res is not None

    def body(*refs):
        a_ref, b_ref = refs[0], refs[1]
        r_ref = refs[2] if has_res else None
        o_ref = refs[3] if has_res else refs[2]
        p = _dot(a_ref[...], b_ref[...], 0 if ta else 1, 1 if tb else 0)

        def finish(total):
            if has_res:
                total = total + r_ref[...]
            o_ref[...] = total.astype(out_dtype)

        if nk == 1:
            finish(p)
        else:
            acc_ref = refs[-1]
            kk = pl.program_id(2)

            @pl.when(kk == 0)
            def _():
                acc_ref[...] = p

            @pl.when(kk > 0)
            def _():
                acc_ref[...] += p

            @pl.when(kk == nk - 1)
            def _():
                finish(acc_ref[...])

    return pl.pallas_call(
        body, name=name,
        grid=(m // bm, n // bn, nk),
        in_specs=[a_spec, b_spec] + ([o_spec] if has_res else []),
        out_specs=o_spec,
        out_shape=jax.ShapeDtypeStruct((m, n), out_dtype),
        scratch_shapes=[pltpu.VMEM((bm, bn), F32)] if nk > 1 else [],
        compiler_params=_params(("parallel", "parallel", "arbitrary")),
    )(*([a, b] + ([res] if has_res else [])))


def _mm_sum(name, pairs, *, bm, bk, exchange=()):
    nx = len(exchange)
    npair = len(pairs)
    m, n = pairs[0][0].shape[0], pairs[0][1].shape[0]
    steps, first = [], []
    for a, b in pairs:
        k = a.shape[1]
        assert a.shape[0] == m and b.shape == (n, k) and m % bm == 0 and (k % bk == 0 or k < bk), (name, a.shape, b.shape)
        first.append(sum(steps))
        steps.append(max(k // bk, 1))
    total = sum(steps)
    in_specs = []
    for (a, b), off, cnt in zip(pairs, first, steps):
        width = min(bk, a.shape[1])
        in_specs.append(pl.BlockSpec((bm, width), lambda i, kk, off=off, cnt=cnt: (i, jnp.clip(kk - off, 0, cnt - 1))))
        in_specs.append(pl.BlockSpec((n, width), lambda i, kk, off=off, cnt=cnt: (0, jnp.clip(kk - off, 0, cnt - 1))))

    def body(*refs):
        send_refs = refs[2 * npair:2 * npair + nx]
        o_ref = refs[2 * npair + nx]
        recv_refs = refs[2 * npair + nx + 1:2 * npair + 2 * nx + 1]
        acc_ref = refs[2 * npair + 2 * nx + 1]
        i, kk = pl.program_id(0), pl.program_id(1)
        if nx:
            start, finish = _scatter_phases(send_refs, recv_refs, *refs[2 * npair + 2 * nx + 2:])

            @pl.when((i == 0) & (kk == 0))
            def _():
                start()

        for s, (off, cnt) in enumerate(zip(first, steps)):
            @pl.when((kk >= off) & (kk < off + cnt))
            def _(s=s, off=off):
                p = _dot(refs[2 * s][...], refs[2 * s + 1][...], 1, 1)
                if off == 0:
                    @pl.when(kk == 0)
                    def _():
                        acc_ref[...] = p

                    @pl.when(kk > 0)
                    def _():
                        acc_ref[...] += p
                else:
                    acc_ref[...] += p

        @pl.when(kk == total - 1)
        def _():
            o_ref[...] = acc_ref[...]

        if nx:
            @pl.when((i == m // bm - 1) & (kk == total - 1))
            def _():
                finish()

    hbm = pl.BlockSpec(memory_space=pl.ANY)
    res = pl.pallas_call(
        body, name=name, grid=(m // bm, total),
        in_specs=in_specs + [hbm] * nx, out_specs=[pl.BlockSpec((bm, n), lambda i, kk: (i, 0))] + [hbm] * nx,
        out_shape=[jax.ShapeDtypeStruct((m, n), F32)] + [jax.ShapeDtypeStruct(e.shape, e.dtype) for e in exchange],
        scratch_shapes=[pltpu.VMEM((bm, n), F32)] + (_exchange_scratch(nx, N_CHIPS - 1) if nx else []),
        compiler_params=_params(("arbitrary", "arbitrary")),
    )(*[t for pair in pairs for t in pair], *exchange)
    return (res[0], res[1:]) if nx else res[0]


def _rows(name, fn, ins, params, outs, accs, *, tm, rs, unroll=4):
    nrow = ins[0][0].shape[-2]
    while tm % (rs * unroll):
        unroll //= 2
    assert nrow % tm == 0 and tm % rs == 0, (name, nrow, tm, rs)
    n_in, n_p, n_out, n_acc = len(ins), len(params), len(outs), len(accs)
    in_specs = []
    for spec in ins:
        arr, width, cb = spec[:3]
        if len(spec) == 4:
            in_specs.append(pl.BlockSpec((None, tm, width), lambda i, cb=cb, lead=spec[3]: (lead, i, cb)))
        else:
            in_specs.append(pl.BlockSpec((tm, width), lambda i, cb=cb: (i, cb)))
    for p in params:
        in_specs.append(pl.BlockSpec(p.shape, lambda i, nd=p.ndim: (0,) * nd))
    out_specs = [pl.BlockSpec((tm, w), lambda i: (i, 0)) for w, _ in outs]
    out_specs += [pl.BlockSpec(s, lambda i: (0, 0)) for s in accs]
    out_shape = [jax.ShapeDtypeStruct((nrow, w), dt) for w, dt in outs]
    out_shape += [jax.ShapeDtypeStruct(s, F32) for s in accs]

    def body(*refs):
        in_refs = refs[:n_in]
        p_refs = refs[n_in:n_in + n_p]
        o_refs = refs[n_in + n_p:n_in + n_p + n_out]
        a_refs = refs[n_in + n_p + n_out:]
        pv = [p[...] for p in p_refs]

        if n_acc:
            @pl.when(pl.program_id(0) == 0)
            def _():
                for a_ref in a_refs:
                    a_ref[...] = jnp.zeros_like(a_ref)

        def step(r, carry):
            for u in range(unroll):
                sl = pl.ds(pl.multiple_of((r * unroll + u) * rs, rs), rs)
                vals = [ref[sl, :].astype(F32) for ref in in_refs]
                row_out, sums = fn(*vals, *pv)
                for o_ref, v in zip(o_refs, row_out):
                    o_ref[sl, :] = v.astype(o_ref.dtype)
                carry = tuple(c + s for c, s in zip(carry, sums))
            return carry

        init = tuple(jnp.zeros(s, F32) for s in accs)
        total = lax.fori_loop(0, tm // (rs * unroll), step, init)
        for a_ref, t in zip(a_refs, total):
            a_ref[...] += t

    res = pl.pallas_call(
        body, name=name, grid=(nrow // tm,),
        in_specs=in_specs, out_specs=out_specs, out_shape=out_shape,
        compiler_params=_params(("arbitrary",)),
    )(*([s[0] for s in ins] + list(params)))
    return res


def _rms(x, w):
    return x * lax.rsqrt(jnp.mean(x * x, axis=-1, keepdims=True) + EPS) * w


def _colsum(v):
    return jnp.sum(v, axis=0, keepdims=True)


def _rms_fwd(name, x, w):
    def fn(xv, wv):
        return (_rms(xv, wv),), ()
    return _rows(name, fn, [(x, D_MODEL, 0)], [w], [(D_MODEL, BF16)], [], tm=512, rs=16)[0]


def _rms_bwd(name, x, w, dy, dres):
    def fn(xv, dyv, drv, wv):
        _, vjp = jax.vjp(_rms, xv, wv)
        dx, dw = vjp(dyv)
        return (drv + dx,), (dw,)
    return _rows(name, fn, [(x, D_MODEL, 0), (dy, D_MODEL, 0), (dres, D_MODEL, 0)], [w],
                 [(D_MODEL, F32)], [(1, D_MODEL)], tm=512, rs=16)


def _final_loss(name, h, target, w):
    def fn(hv, tv, wv):
        y, vjp = jax.vjp(_rms, hv, wv)
        err = y - tv
        part = 0.5 * jnp.sum(jnp.mean(err * err, axis=-1, keepdims=True), axis=0, keepdims=True)
        dh, dw = vjp(err / D_MODEL)
        return (dh,), (jnp.broadcast_to(part, (1, LANES)), dw)
    return _rows(name, fn, [(h, D_MODEL, 0), (target, D_MODEL, 0)], [w],
                 [(D_MODEL, F32)], [(1, LANES), (1, D_MODEL)], tm=512, rs=16)


def _merge(ga, gb, ya, yb, b0, b1):
    return jax.nn.sigmoid(ga + b0) * ya + jax.nn.sigmoid(gb + b1) * yb


def _merge_fwd(name, g, ya, yb, b0, b1):
    def fn(ga, gb, yav, ybv, b0v, b1v):
        return (_merge(ga, gb, yav, ybv, b0v, b1v),), ()
    return _rows(name, fn, [(g, D_MODEL, 0), (g, D_MODEL, 1), (ya, D_MODEL, 0), (yb, D_MODEL, 0)],
                 [b0, b1], [(D_MODEL, BF16)], [], tm=512, rs=16)[0]


def _merge_bwd(name, g, ya, yb, dm, b0, b1):
    def fn(ga, gb, yav, ybv, dmv, b0v, b1v):
        _, vjp = jax.vjp(_merge, ga, gb, yav, ybv, b0v, b1v)
        dga, dgb, dya, dyb, db0, db1 = vjp(dmv)
        return (jnp.concatenate([dga, dgb], axis=1), dya, dyb), (db0, db1)
    return _rows(name, fn,
                 [(g, D_MODEL, 0), (g, D_MODEL, 1), (ya, D_MODEL, 0), (yb, D_MODEL, 0), (dm, D_MODEL, 0)],
                 [b0, b1], [(2 * D_MODEL, BF16), (D_MODEL, BF16), (D_MODEL, BF16)],
                 [(1, D_MODEL), (1, D_MODEL)], tm=512, rs=16)


GROUP_W = SSM_INNER // SSM_GROUPS


def _gate_norm_group(y, z, nw):
    v = y * jax.nn.silu(z)
    return v * lax.rsqrt(jnp.mean(v * v, axis=-1, keepdims=True) + EPS) * nw


def _gate_norm_fwd(name, y, z, nw):
    def fn(yv, zv, nwv):
        parts = [_gate_norm_group(yv[:, k * GROUP_W:(k + 1) * GROUP_W], zv[:, k * GROUP_W:(k + 1) * GROUP_W],
                                  nwv[:, k * GROUP_W:(k + 1) * GROUP_W]) for k in range(SSM_GROUPS)]
        return (jnp.concatenate(parts, axis=1),), ()
    return _rows(name, fn, [(y, SSM_INNER, 0), (z, SSM_INNER, 0)], [nw], [(SSM_INNER, BF16)], [],
                 tm=512, rs=16)[0]


def _gate_norm_bwd(name, y, z, dout, nw):
    def fn(yv, zv, dv, nwv):
        dys, dzs, dns = [], [], []
        for k in range(SSM_GROUPS):
            sl = slice(k * GROUP_W, (k + 1) * GROUP_W)
            _, vjp = jax.vjp(_gate_norm_group, yv[:, sl], zv[:, sl], nwv[:, sl])
            dy, dz, dn = vjp(dv[:, sl])
            dys.append(dy), dzs.append(dz), dns.append(dn)
        return (jnp.concatenate(dys, axis=1), jnp.concatenate(dzs, axis=1)), (jnp.concatenate(dns, axis=1),)
    return _rows(name, fn, [(y, SSM_INNER, 0), (z, SSM_INNER, 0), (dout, SSM_INNER, 0)], [nw],
                 [(SSM_INNER, BF16), (SSM_INNER, BF16)], [(1, SSM_INNER)], tm=512, rs=16)


def _softplus(v):
    return jnp.maximum(v, 0.0) + jnp.log1p(jnp.exp(-jnp.abs(v)))


def _chunk_cumsum(v, reverse=False):
    row = lax.broadcasted_iota(jnp.int32, v.shape, 0)
    step = 1
    while step < CHUNK:
        if reverse:
            shifted = pltpu.roll(v, CHUNK - step, axis=0)
            v = v + jnp.where(row < CHUNK - step, shifted, 0.0)
        else:
            shifted = pltpu.roll(v, step, axis=0)
            v = v + jnp.where(row >= step, shifted, 0.0)
        step *= 2
    return v


def _dt_prep(name, dt_raw, dt_bias, a_log):
    def fn(rv, bv, alv):
        dt = _softplus(rv + bv)
        return (dt, _chunk_cumsum(dt * (-jnp.exp(alv)))), ()
    return _rows(name, fn, [(dt_raw, SSM_HEADS, 0)], [dt_bias, a_log],
                 [(SSM_HEADS, F32), (SSM_HEADS, F32)], [], tm=512, rs=CHUNK)


def _dt_bwd(name, dt_raw, ddt, da1, da2, dt_bias, a_log):
    def fn(rv, ddv, d1, d2, bv, alv):
        pre = rv + bv
        dt = _softplus(pre)
        a_neg = -jnp.exp(alv)
        back = _chunk_cumsum(d1 + d2, reverse=True)
        d_dt = ddv + back * a_neg
        d_raw = d_dt * jax.nn.sigmoid(pre)
        return (d_raw,), (_colsum(d_raw), _colsum(back * dt) * a_neg)
    return _rows(name, fn, [(dt_raw, SSM_HEADS, 0), (ddt, SSM_HEADS, 0), (da1, SSM_HEADS, 0), (da2, SSM_HEADS, 0)],
                 [dt_bias, a_log], [(SSM_HEADS, BF16)], [(1, SSM_HEADS), (1, SSM_HEADS)], tm=512, rs=CHUNK)


def _adamw_math(w, g, m, v):
    m_new = ADAM_B1 * m + (1.0 - ADAM_B1) * g
    v_new = ADAM_B2 * v + (1.0 - ADAM_B2) * jnp.square(g)
    m_hat = m_new / (1.0 - ADAM_B1 ** ADAM_STEP)
    v_hat = v_new / (1.0 - ADAM_B2 ** ADAM_STEP)
    delta = -ADAM_LR * (m_hat / (jnp.sqrt(v_hat) + ADAM_EPS) + ADAM_WD * w)
    return delta, m_new, v_new


def _adamw(name, w, g, m, v, *, tm, rs):
    width = w.shape[1]

    def fn(wv, mv, vv, gv):
        return (gv,) + _adamw_math(wv, gv, mv, vv), ()
    return _rows(name, fn, [(w, width, 0), (m, width, 0), (v, width, 0), (g, width, 0)],
                 [], [(width, F32)] * 4, [], tm=tm, rs=rs)


def _adamw_small(name, ws, gs, ms, vs):
    n = len(ws)

    def body(*refs):
        w_refs, g_refs, m_refs, v_refs = (refs[k * n:(k + 1) * n] for k in range(4))
        outs = refs[4 * n:]
        for i in range(n):
            res = _adamw_math(w_refs[i][...], g_refs[i][...], m_refs[i][...], v_refs[i][...])
            for k in range(3):
                outs[k * n + i][...] = res[k]

    vmem = pl.BlockSpec(memory_space=pltpu.VMEM)
    res = pl.pallas_call(
        body, name=name, in_specs=[vmem] * (4 * n), out_specs=[vmem] * (3 * n),
        out_shape=[jax.ShapeDtypeStruct(w.shape, F32) for w in ws] * 3,
        compiler_params=pltpu.CompilerParams(vmem_limit_bytes=VMEM_LIMIT_V7X),
    )(*ws, *gs, *ms, *vs)
    return res[:n], res[n:2 * n], res[2 * n:]


def _pair_sum(name, a, b, *, tm):
    shape = a.shape
    flat = (shape[0] * shape[1], shape[2])

    def fn(av, bv):
        return (av.astype(F32) + bv.astype(F32),), ()
    out = _rows(name, fn, [(a.reshape(flat), flat[1], 0), (b.reshape(flat), flat[1], 0)], [], [(flat[1], BF16)], [],
                tm=tm, rs=2 * SUBLANES)[0]
    return out.reshape(shape)


def _sum_slots(name, stack, *, tm, rs):
    width = stack.shape[2]

    def fn(*slots):
        s0, s1, s2, s3 = (s.astype(F32) for s in slots)
        return (((s0 + s1) + s2) + s3,), ()
    return _rows(name, fn, [(stack, width, 0, k) for k in range(N_CHIPS)], [], [(width, F32)], [],
                 tm=tm, rs=rs)[0]


def _layernorm(v, w, b):
    mu = jnp.mean(v, axis=-1, keepdims=True)
    var = jnp.mean(jnp.square(v - mu), axis=-1, keepdims=True)
    return (v - mu) * lax.rsqrt(var + EPS) * w + b


def _gmlp_mask():
    t = lax.broadcasted_iota(jnp.int32, (GMLP_BLOCK, GMLP_BLOCK), 0) // CHUNK
    s = lax.broadcasted_iota(jnp.int32, (GMLP_BLOCK, GMLP_BLOCK), 1) // CHUNK
    return s <= t


GMLP_TM = 512


def _gmlp_fwd(name, za, ln_w, ln_b, ws, bs_col):
    nrow = za.shape[0]
    tm = GMLP_TM
    width = GMLP_GROUPS * GMLP_BLOCK

    def body(za_ref, lnw_ref, lnb_ref, ws_ref, bs_ref, o_ref, wm_ref):
        mask = _gmlp_mask()
        for g in range(GMLP_GROUPS):
            wm_ref[g] = jnp.where(mask, ws_ref[g], 0.0).astype(BF16)

        def block(n, carry):
            rows = pl.ds(pl.multiple_of(n * GMLP_BLOCK, GMLP_BLOCK), GMLP_BLOCK)
            for g in range(GMLP_GROUPS):
                cols = slice(g * GMLP_BLOCK, (g + 1) * GMLP_BLOCK)
                vcols = slice(width + g * GMLP_BLOCK, width + (g + 1) * GMLP_BLOCK)
                u = jax.nn.gelu(za_ref[rows, cols].astype(F32))
                v = jax.nn.gelu(za_ref[rows, vcols].astype(F32))
                vn = _layernorm(v, lnw_ref[g:g + 1, :], lnb_ref[g:g + 1, :])
                sv = _dot(wm_ref[g], vn) + bs_ref[g]
                o_ref[rows, cols] = (u * sv).astype(o_ref.dtype)
            return carry

        lax.fori_loop(0, tm // GMLP_BLOCK, block, 0)

    small = lambda a: pl.BlockSpec(a.shape, lambda i, nd=a.ndim: (0,) * nd)
    return pl.pallas_call(
        body, name=name, grid=(nrow // tm,),
        in_specs=[pl.BlockSpec((tm, 2 * width), lambda i: (i, 0)), small(ln_w), small(ln_b), small(ws), small(bs_col)],
        out_specs=pl.BlockSpec((tm, width), lambda i: (i, 0)),
        out_shape=jax.ShapeDtypeStruct((nrow, width), BF16),
        scratch_shapes=[pltpu.VMEM((GMLP_GROUPS, GMLP_BLOCK, GMLP_BLOCK), BF16)],
        compiler_params=_params(("arbitrary",)),
    )(za, ln_w, ln_b, ws, bs_col)


def _gmlp_bwd(name, za, dout, ln_w, ln_b, ws, bs_col):
    nrow = za.shape[0]
    tm = GMLP_TM
    width = GMLP_GROUPS * GMLP_BLOCK

    def body(za_ref, do_ref, lnw_ref, lnb_ref, ws_ref, bs_ref, dza_ref, dlnw_ref, dlnb_ref, dws_ref, dbs_ref, wm_ref):
        mask = _gmlp_mask()
        for g in range(GMLP_GROUPS):
            wm_ref[g] = jnp.where(mask, ws_ref[g], 0.0).astype(BF16)

        @pl.when(pl.program_id(0) == 0)
        def _():
            dlnw_ref[...] = jnp.zeros_like(dlnw_ref)
            dlnb_ref[...] = jnp.zeros_like(dlnb_ref)
            dws_ref[...] = jnp.zeros_like(dws_ref)
            dbs_ref[...] = jnp.zeros_like(dbs_ref)

        def block(n, carry):
            rows = pl.ds(pl.multiple_of(n * GMLP_BLOCK, GMLP_BLOCK), GMLP_BLOCK)
            for g in range(GMLP_GROUPS):
                cols = slice(g * GMLP_BLOCK, (g + 1) * GMLP_BLOCK)
                vcols = slice(width + g * GMLP_BLOCK, width + (g + 1) * GMLP_BLOCK)
                u, gelu_u_vjp = jax.vjp(jax.nn.gelu, za_ref[rows, cols].astype(F32))
                v, gelu_v_vjp = jax.vjp(jax.nn.gelu, za_ref[rows, vcols].astype(F32))
                vn, ln_vjp = jax.vjp(_layernorm, v, lnw_ref[g:g + 1, :], lnb_ref[g:g + 1, :])
                sv = _dot(wm_ref[g], vn) + bs_ref[g]
                d_o = do_ref[rows, cols].astype(F32)
                dsv = d_o * u
                d_wm = _dot(dsv, vn, 1, 1)
                dvn = _dot(wm_ref[g], dsv, 0, 0)
                dv, dlnw, dlnb = ln_vjp(dvn)
                dza_ref[rows, cols] = gelu_u_vjp(d_o * sv)[0].astype(dza_ref.dtype)
                dza_ref[rows, vcols] = gelu_v_vjp(dv)[0].astype(dza_ref.dtype)
                dlnw_ref[g:g + 1, :] += dlnw
                dlnb_ref[g:g + 1, :] += dlnb
                dws_ref[g] += jnp.where(mask, d_wm, 0.0)
                dbs_ref[g] += jnp.sum(dsv, axis=1, keepdims=True)
            return carry

        lax.fori_loop(0, tm // GMLP_BLOCK, block, 0)

    small = lambda a: pl.BlockSpec(a.shape, lambda i, nd=a.ndim: (0,) * nd)
    return pl.pallas_call(
        body, name=name, grid=(nrow // tm,),
        in_specs=[pl.BlockSpec((tm, 2 * width), lambda i: (i, 0)), pl.BlockSpec((tm, width), lambda i: (i, 0)),
                  small(ln_w), small(ln_b), small(ws), small(bs_col)],
        out_specs=[pl.BlockSpec((tm, 2 * width), lambda i: (i, 0)), small(ln_w), small(ln_b), small(ws), small(bs_col)],
        out_shape=[jax.ShapeDtypeStruct((nrow, 2 * width), BF16), jax.ShapeDtypeStruct(ln_w.shape, F32),
                   jax.ShapeDtypeStruct(ln_b.shape, F32), jax.ShapeDtypeStruct(ws.shape, F32),
                   jax.ShapeDtypeStruct(bs_col.shape, F32)],
        scratch_shapes=[pltpu.VMEM((GMLP_GROUPS, GMLP_BLOCK, GMLP_BLOCK), BF16)],
        compiler_params=_params(("arbitrary",)),
    )(za, dout, ln_w, ln_b, ws, bs_col)


CONV_TM = 256
CONV_RS = 32
HALO = 2 * SUBLANES


def _tap_rows(w_ref):
    return [w_ref[k:k + 1, :] for k in range(w_ref.shape[0])]


def _halo_specs(nrow, tm, tc):
    per = tm // HALO
    last = nrow // HALO - 1
    main = pl.BlockSpec((tm, tc), lambda j, i: (i, j))
    before = pl.BlockSpec((HALO, tc), lambda j, i: (jnp.maximum(i * per - 1, 0), j))
    after = pl.BlockSpec((HALO, tc), lambda j, i: (jnp.minimum((i + 1) * per, last), j))
    return main, before, after


def _col_spec(rows, tc):
    return pl.BlockSpec((rows, tc), lambda j, i: (0, j))


def _conv_fwd(name, x, w, b, *, tc):
    nrow, ncol = x.shape
    taps = w.shape[0]
    tm, rs = CONV_TM, CONV_RS
    main, before, _ = _halo_specs(nrow, tm, tc)

    def body(x_ref, xb_ref, w_ref, b_ref, o_ref, xw_ref):
        first = pl.program_id(1) == 0
        wv, bv = _tap_rows(w_ref), b_ref[...]
        xw_ref[0:HALO, :] = jnp.where(first, 0.0, xb_ref[...].astype(F32))
        for r in range(tm // rs):
            xw_ref[HALO + r * rs:HALO + (r + 1) * rs, :] = x_ref[r * rs:(r + 1) * rs, :].astype(F32)
        for r in range(tm // rs):
            base = HALO + r * rs
            out = bv + wv[taps - 1] * xw_ref[base:base + rs, :]
            for k in range(taps - 1):
                back = taps - 1 - k
                out = out + wv[k] * xw_ref[base - back:base - back + rs, :]
            o_ref[r * rs:(r + 1) * rs, :] = out.astype(o_ref.dtype)

    return pl.pallas_call(
        body, name=name, grid=(ncol // tc, nrow // tm),
        in_specs=[main, before, _col_spec(taps, tc), _col_spec(1, tc)],
        out_specs=main, out_shape=jax.ShapeDtypeStruct((nrow, ncol), BF16),
        scratch_shapes=[pltpu.VMEM((HALO + tm, tc), F32)],
        compiler_params=_params(("parallel", "arbitrary")),
    )(x, x, w, b)


def _conv_bwd(name, dpre, x, w, *, tc):
    nrow, ncol = x.shape
    taps = w.shape[0]
    tm, rs = CONV_TM, CONV_RS
    nsub = tm // rs
    main, before, after = _halo_specs(nrow, tm, tc)

    def fold(v):
        total = v[0:SUBLANES]
        for q in range(1, rs // SUBLANES):
            total = total + v[q * SUBLANES:(q + 1) * SUBLANES]
        return total

    def body(d_ref, da_ref, x_ref, xb_ref, w_ref, dx_ref, dw_ref, db_ref, dwin_ref, xwin_ref):
        i = pl.program_id(1)
        first, last = i == 0, i == pl.num_programs(1) - 1
        wv = _tap_rows(w_ref)

        @pl.when(first)
        def _():
            dw_ref[...] = jnp.zeros_like(dw_ref)
            db_ref[...] = jnp.zeros_like(db_ref)

        xwin_ref[0:HALO, :] = jnp.where(first, 0.0, xb_ref[...].astype(F32))
        dwin_ref[tm:, :] = jnp.where(last, 0.0, da_ref[...].astype(F32))
        for r in range(nsub):
            dwin_ref[r * rs:(r + 1) * rs, :] = d_ref[r * rs:(r + 1) * rs, :].astype(F32)
            xwin_ref[HALO + r * rs:HALO + (r + 1) * rs, :] = x_ref[r * rs:(r + 1) * rs, :].astype(F32)
        dw = [jnp.zeros((SUBLANES, tc), F32)] * taps
        db = jnp.zeros((SUBLANES, tc), F32)
        for r in range(nsub):
            cur = dwin_ref[r * rs:(r + 1) * rs, :]
            dx = wv[taps - 1] * cur
            for k in range(taps - 1):
                ahead = taps - 1 - k
                dx = dx + wv[k] * dwin_ref[r * rs + ahead:(r + 1) * rs + ahead, :]
            dx_ref[r * rs:(r + 1) * rs, :] = dx.astype(dx_ref.dtype)
            for k in range(taps):
                back = taps - 1 - k
                dw[k] = dw[k] + fold(cur * xwin_ref[HALO + r * rs - back:HALO + (r + 1) * rs - back, :])
            db = db + fold(cur)
        for k in range(taps):
            dw_ref[k:k + 1, :] += _colsum(dw[k])
        db_ref[...] += _colsum(db)

    return pl.pallas_call(
        body, name=name, grid=(ncol // tc, nrow // tm),
        in_specs=[main, after, main, before, _col_spec(taps, tc)],
        out_specs=[main, _col_spec(taps, tc), _col_spec(1, tc)],
        out_shape=[jax.ShapeDtypeStruct((nrow, ncol), BF16), jax.ShapeDtypeStruct((taps, ncol), F32),
                   jax.ShapeDtypeStruct((1, ncol), F32)],
        scratch_shapes=[pltpu.VMEM((tm + HALO, tc), F32), pltpu.VMEM((HALO + tm, tc), F32)],
        compiler_params=_params(("parallel", "arbitrary")),
    )(dpre, dpre, x, x, w)


def _glu(gate, val):
    return jax.nn.silu(gate) * val


def _ffn_act_fwd(name, pg, pv, wg, wv, bg, bv, *, tc):
    nrow, ncol = pg.shape
    taps = wg.shape[0]
    tm, rs = CONV_TM, CONV_RS
    main, before, _ = _halo_specs(nrow, tm, tc)

    def body(pg_ref, pgb_ref, pv_ref, pvb_ref, wg_ref, wv_ref, bg_ref, bv_ref, g_ref, v_ref, a_ref, gwin_ref, vwin_ref):
        first = pl.program_id(1) == 0
        taps_g, taps_v, bgv, bvv = _tap_rows(wg_ref), _tap_rows(wv_ref), bg_ref[...], bv_ref[...]
        gwin_ref[0:HALO, :] = jnp.where(first, 0.0, pgb_ref[...].astype(F32))
        vwin_ref[0:HALO, :] = jnp.where(first, 0.0, pvb_ref[...].astype(F32))
        for r in range(tm // rs):
            gwin_ref[HALO + r * rs:HALO + (r + 1) * rs, :] = pg_ref[r * rs:(r + 1) * rs, :].astype(F32)
            vwin_ref[HALO + r * rs:HALO + (r + 1) * rs, :] = pv_ref[r * rs:(r + 1) * rs, :].astype(F32)

        def conv(win_ref, tap_rows, bias, r):
            base = HALO + r * rs
            out = bias + tap_rows[taps - 1] * win_ref[base:base + rs, :]
            for k in range(taps - 1):
                back = taps - 1 - k
                out = out + tap_rows[k] * win_ref[base - back:base - back + rs, :]
            return out

        for r in range(tm // rs):
            sl = slice(r * rs, (r + 1) * rs)
            gate, val = conv(gwin_ref, taps_g, bgv, r), conv(vwin_ref, taps_v, bvv, r)
            g_ref[sl, :] = gate.astype(g_ref.dtype)
            v_ref[sl, :] = val.astype(v_ref.dtype)
            a_ref[sl, :] = _glu(gate, val).astype(a_ref.dtype)

    return pl.pallas_call(
        body, name=name, grid=(ncol // tc, nrow // tm),
        in_specs=[main, before, main, before, _col_spec(taps, tc), _col_spec(taps, tc), _col_spec(1, tc), _col_spec(1, tc)],
        out_specs=[main, main, main],
        out_shape=[jax.ShapeDtypeStruct((nrow, ncol), BF16)] * 3,
        scratch_shapes=[pltpu.VMEM((HALO + tm, tc), F32), pltpu.VMEM((HALO + tm, tc), F32)],
        compiler_params=_params(("parallel", "arbitrary")),
    )(pg, pg, pv, pv, wg, wv, bg, bv)


def _ffn_act_bwd(name, dact, gate, val):
    def fn(dv, gv, vv):
        _, vjp = jax.vjp(_glu, gv, vv)
        dg, dval = vjp(dv)
        return (dg, dval), ()
    width = dact.shape[1]
    return _rows(name, fn, [(dact, width, 0), (gate, width, 0), (val, width, 0)], [],
                 [(width, BF16), (width, BF16)], [], tm=256, rs=2 * SUBLANES)


SSD_TM = 256
SSD_CHUNKS = SSD_TM // CHUNK
X_OFF, B_OFF, C_OFF = 0, SSM_INNER, SSM_INNER + SSM_GROUPS * SSM_STATE
HP = SSM_HPG * SSM_HEAD_DIM


def _causal_tiled():
    row = lax.broadcasted_iota(jnp.int32, (CHUNK, HP), 0)
    src = lax.broadcasted_iota(jnp.int32, (CHUNK, HP), 1) & (CHUNK - 1)
    return src <= row


def _split2(v):
    hi = v.astype(BF16)
    return hi, (v - hi.astype(F32)).astype(BF16)


def _dot_exact(a, ind):
    hi, lo = (lax.dot_general(p, ind, (((1,), (0,)), ((), ())), preferred_element_type=F32) for p in _split2(a))
    return hi + lo


def _head_indicator():
    head = lax.broadcasted_iota(jnp.int32, (SSM_HEADS, SSM_INNER), 0)
    chan = lax.broadcasted_iota(jnp.int32, (SSM_HEADS, SSM_INNER), 1)
    return (chan // SSM_HEAD_DIM == head).astype(BF16)


def _chunk_decays(ci, dt_ref, ac_ref, ind, ax_ref, dtx_ref, eax_ref, eex_ref, tail_ref):
    rows = pl.ds(pl.multiple_of(ci * CHUNK, CHUNK), CHUNK)
    ax_ref[...] = _dot_exact(ac_ref[rows, :], ind)
    dtx_ref[...] = _dot_exact(dt_ref[rows, :], ind)
    eax_ref[...] = jnp.exp(ax_ref[...])
    eex_ref[...] = jnp.exp(ax_ref[CHUNK - 1:CHUNK, :] - ax_ref[...])
    tail = pl.ds(pl.multiple_of(ci * CHUNK + CHUNK - SUBLANES, SUBLANES), SUBLANES)
    tail_ref[...] = jnp.exp(ac_ref[tail, :])


def _group_decay(ci, g, ax_ref, af_ref, xbc_ref, causal):
    gcols = slice(g * HP, (g + 1) * HP)
    bm = xbc_ref[:, B_OFF + g * SSM_STATE:B_OFF + (g + 1) * SSM_STATE]
    cm = xbc_ref[:, C_OFF + g * SSM_STATE:C_OFF + (g + 1) * SSM_STATE]
    cb_tiled = _dot(cm, jnp.concatenate([bm] * SSM_HPG, axis=0), 1, 1)
    seg = ax_ref[:, gcols] - af_ref[ci, :, gcols]
    decay = jnp.where(causal, jnp.exp(jnp.where(causal, seg, 0.0)), 0.0)
    return bm, cm, cb_tiled * decay, decay


def _ssd_fwd(name, pre, dt, a_cum, a_flat, d_x, ind, shards):
    nrow = pre.shape[0]
    tm = SSD_TM
    nstep = nrow // tm
    ng = len(shards)

    def body(pre_ref, dt_ref, ac_ref, af_ref, dx_ref, ind_ref, *rest):
        shard_refs, (y_ref, st_ref), stack_refs = rest[:ng], rest[ng:ng + 2], rest[ng + 2:2 * ng + 2]
        (h_ref, xbc_ref, ax_ref, dtx_ref, eax_ref, eex_ref, m_ref, xd_ref, yd_ref, tail_ref,
         send_sems, recv_sems) = rest[2 * ng + 2:]
        step = pl.program_id(0)
        start, forward, finish = _gather_phases([s.shape[0] for s in shards], ng, shard_refs, stack_refs,
                                                send_sems, recv_sems)

        @pl.when(step == 0)
        def _():
            h_ref[...] = jnp.zeros_like(h_ref)
            start()

        @pl.when(step == nstep // 2)
        def _():
            forward()

        causal = _causal_tiled()
        ind = ind_ref[...]

        def chunk(ci, carry):
            rows = pl.ds(pl.multiple_of(ci * CHUNK, CHUNK), CHUNK)
            xbc_ref[...] = jax.nn.silu(pre_ref[rows, :].astype(F32))
            _chunk_decays(ci, dt_ref, ac_ref, ind, ax_ref, dtx_ref, eax_ref, eex_ref, tail_ref)
            st_ref[ci] = h_ref[...].astype(st_ref.dtype)
            for g in range(SSM_GROUPS):
                gcols = slice(g * HP, (g + 1) * HP)
                bm, cm, m_all, _ = _group_decay(ci, g, ax_ref, af_ref, xbc_ref, causal)
                m_ref[...] = m_all
                x_g = xbc_ref[:, gcols]
                xd = x_g * dtx_ref[:, gcols]
                xd_ref[...] = xd
                h_g = h_ref[gcols, :]
                for hh in range(SSM_HPG):
                    lc = slice(hh * SSM_HEAD_DIM, (hh + 1) * SSM_HEAD_DIM)
                    yd_ref[:, lc] = _dot(m_ref[:, lc], xd_ref[:, lc])
                y_ref[rows, gcols] = (yd_ref[...] + _dot(cm, h_g, 1, 1) * eax_ref[:, gcols]
                                      + dx_ref[:, gcols] * x_g).astype(y_ref.dtype)
                new = _dot(xd * eex_ref[:, gcols], bm, 0, 0)
                for hh in range(SSM_HPG):
                    h = g * SSM_HPG + hh
                    hrows = slice(h * SSM_HEAD_DIM, (h + 1) * SSM_HEAD_DIM)
                    lrows = slice(hh * SSM_HEAD_DIM, (hh + 1) * SSM_HEAD_DIM)
                    h_ref[hrows, :] = tail_ref[SUBLANES - 1:SUBLANES, h:h + 1] * h_ref[hrows, :] + new[lrows, :]
            return carry

        lax.fori_loop(0, SSD_CHUNKS, chunk, 0)

        @pl.when(step == nstep - 1)
        def _():
            finish()

    nchunk = nrow // CHUNK
    whole = lambda a: pl.BlockSpec(a.shape, lambda i, nd=a.ndim: (0,) * nd)
    hbm = pl.BlockSpec(memory_space=pl.ANY)
    wide = lambda: pltpu.VMEM((CHUNK, SSM_INNER), F32)
    group = lambda: pltpu.VMEM((CHUNK, HP), F32)
    res = pl.pallas_call(
        body, name=name, grid=(nstep,),
        in_specs=[pl.BlockSpec((tm, SSM_XBC), lambda i: (i, 0)), pl.BlockSpec((tm, SSM_HEADS), lambda i: (i, 0)),
                  pl.BlockSpec((tm, SSM_HEADS), lambda i: (i, 0)),
                  pl.BlockSpec((SSD_CHUNKS, 1, SSM_INNER), lambda i: (i, 0, 0)), whole(d_x), whole(ind)] + [hbm] * ng,
        out_specs=[pl.BlockSpec((tm, SSM_INNER), lambda i: (i, 0)),
                   pl.BlockSpec((SSD_CHUNKS, SSM_INNER, SSM_STATE), lambda i: (i, 0, 0))] + [hbm] * ng,
        out_shape=[jax.ShapeDtypeStruct((nrow, SSM_INNER), BF16),
                   jax.ShapeDtypeStruct((nchunk, SSM_INNER, SSM_STATE), BF16)]
        + [jax.ShapeDtypeStruct((N_CHIPS,) + s.shape, s.dtype) for s in shards],
        scratch_shapes=[pltpu.VMEM((SSM_INNER, SSM_STATE), F32), pltpu.VMEM((CHUNK, SSM_XBC), F32),
                        wide(), wide(), wide(), wide(), group(), group(), group(),
                        pltpu.VMEM((SUBLANES, SSM_HEADS), F32)] + _exchange_scratch(ng, GATHER_SEMS),
        compiler_params=_params(("arbitrary",)),
    )(pre, dt, a_cum, a_flat, d_x, ind, *shards)
    return res[0], res[1], res[2:]


def _ssd_bwd(name, pre, dt, a_cum, a_flat, d_x, ind, ind_t, states, dy, pairs):
    nrow = pre.shape[0]
    tm = SSD_TM
    ntile = nrow // tm
    npair = len(pairs)

    def body(pre_ref, dt_ref, ac_ref, af_ref, dx_ref, ind_ref, indt_ref, st_ref, dy_ref, *rest):
        pair_refs = rest[:npair]
        dpre_ref, ddt_ref, da_ref, daf_ref, dd_ref = rest[npair:npair + 5]
        recv_refs = rest[npair + 5:2 * npair + 5]
        (dh_ref, xbc_ref, dxbc_ref, ax_ref, dtx_ref, eax_ref, eex_ref, red_ref,
         m_ref, l_ref, xd_ref, dm_ref, dxd_ref, fold_ref, hd_ref, tail_ref, send_sems, recv_sems) = rest[2 * npair + 5:]
        start, finish = _scatter_phases(pair_refs, recv_refs, send_sems, recv_sems)

        @pl.when(pl.program_id(0) == 0)
        def _():
            dh_ref[...] = jnp.zeros_like(dh_ref)
            dd_ref[...] = jnp.zeros_like(dd_ref)
            start()

        causal = _causal_tiled()
        ind, ind_t = ind_ref[...], indt_ref[...]
        is_last_row = lax.broadcasted_iota(jnp.int32, (CHUNK, 1), 0) == CHUNK - 1
        ones = jnp.ones((CHUNK, SSM_STATE), BF16)

        def chunk(k, ddx):
            ci = SSD_CHUNKS - 1 - k
            rows = pl.ds(pl.multiple_of(ci * CHUNK, CHUNK), CHUNK)
            pre_v = pre_ref[rows, :].astype(F32)
            xbc_ref[...] = jax.nn.silu(pre_v)
            _chunk_decays(ci, dt_ref, ac_ref, ind, ax_ref, dtx_ref, eax_ref, eex_ref, tail_ref)
            ddx_parts = []
            for g in range(SSM_GROUPS):
                gcols = slice(g * HP, (g + 1) * HP)
                bcols = slice(B_OFF + g * SSM_STATE, B_OFF + (g + 1) * SSM_STATE)
                ccols = slice(C_OFF + g * SSM_STATE, C_OFF + (g + 1) * SSM_STATE)
                bm, cm, m_all, decay = _group_decay(ci, g, ax_ref, af_ref, xbc_ref, causal)
                m_ref[...] = m_all
                l_ref[...] = decay
                x_g = xbc_ref[:, gcols]
                xd = x_g * dtx_ref[:, gcols]
                xd_ref[...] = xd
                h_g = st_ref[ci, gcols, :]
                dh_g = dh_ref[gcols, :]
                dy_g = dy_ref[rows, gcols]
                for hh in range(SSM_HPG):
                    h = g * SSM_HPG + hh
                    hcols = slice(h * SSM_HEAD_DIM, (h + 1) * SSM_HEAD_DIM)
                    lc = slice(hh * SSM_HEAD_DIM, (hh + 1) * SSM_HEAD_DIM)
                    dy_h = dy_ref[rows, hcols]
                    dm_ref[:, lc] = _dot(dy_h, xd_ref[:, lc], 1, 1)
                    dxd_ref[:, lc] = _dot(m_ref[:, lc], dy_h, 0, 0)
                ebdh = eex_ref[:, gcols] * _dot(bm, dh_g, 1, 1)
                dxd = dxd_ref[...] + ebdh
                dm = dm_ref[...]
                t = dm * l_ref[...]
                t128 = (t[:, 0:LANES] + t[:, LANES:2 * LANES]) + (t[:, 2 * LANES:3 * LANES] + t[:, 3 * LANES:])
                fold_ref[...] = t128 + pltpu.roll(t128, CHUNK, axis=1)
                dw_sum = fold_ref[:, 0:CHUNK]
                q = dm * m_ref[...]
                dyea = dy_g * eax_ref[:, gcols]
                red_ref[0:CHUNK, gcols] = q + dyea * _dot(cm, h_g, 1, 1)
                red_ref[CHUNK:2 * CHUNK, gcols] = xd * ebdh
                red_ref[2 * CHUNK:3 * CHUNK, gcols] = dxd * x_g
                daf_ref[ci, :, gcols] = -jnp.sum(q, axis=0, keepdims=True)
                ddx_parts.append(jnp.sum(dy_g * x_g, axis=0, keepdims=True))
                dxbc_ref[:, gcols] = dxd * dtx_ref[:, gcols] + dx_ref[:, gcols] * dy_g
                dxbc_ref[:, ccols] = _dot(dw_sum, bm) + _dot(dyea, h_g)
                dxbc_ref[:, bcols] = _dot(dw_sum, cm, 0, 0) + _dot(xd * eex_ref[:, gcols], dh_g)
                dh_new = _dot(dyea, cm, 0, 0)
                for hh in range(SSM_HPG):
                    h = g * SSM_HPG + hh
                    hrows = slice(h * SSM_HEAD_DIM, (h + 1) * SSM_HEAD_DIM)
                    lrows = slice(hh * SSM_HEAD_DIM, (hh + 1) * SSM_HEAD_DIM)
                    hd_ref[h:h + 1, :] = jnp.sum(st_ref[ci, hrows, :] * dh_ref[hrows, :], axis=0, keepdims=True)
                    dh_ref[hrows, :] = tail_ref[SUBLANES - 1:SUBLANES, h:h + 1] * dh_ref[hrows, :] + dh_new[lrows, :]
            sums = _dot_exact(red_ref[...], ind_t)
            ra, ts = sums[:CHUNK], sums[CHUNK:2 * CHUNK]
            hdh = sum(lax.dot_general(ones, p, (((1,), (1,)), ((), ())), preferred_element_type=F32)
                      for p in _split2(hd_ref[...]))
            da_last = jnp.sum(ts, axis=0, keepdims=True) + tail_ref[SUBLANES - 1:SUBLANES, :] * hdh
            da_ref[rows, :] = ra - ts + jnp.where(is_last_row, da_last, 0.0)
            ddt_ref[rows, :] = sums[2 * CHUNK:]
            sig = jax.nn.sigmoid(pre_v)
            dpre_ref[rows, :] = (dxbc_ref[...] * (sig * (1.0 + pre_v * (1.0 - sig)))).astype(dpre_ref.dtype)
            return ddx + jnp.concatenate(ddx_parts, axis=1)

        ddx = lax.fori_loop(0, SSD_CHUNKS, chunk, jnp.zeros((1, SSM_INNER), F32))
        dd_ref[...] += _dot_exact(jnp.broadcast_to(ddx, (SUBLANES, SSM_INNER)), ind_t)

        @pl.when(pl.program_id(0) == ntile - 1)
        def _():
            finish()

    rev = lambda i: ntile - 1 - i
    whole = lambda a: pl.BlockSpec(a.shape, lambda i, nd=a.ndim: (0,) * nd)
    hbm = pl.BlockSpec(memory_space=pl.ANY)
    wide = lambda: pltpu.VMEM((CHUNK, SSM_INNER), F32)
    group = lambda: pltpu.VMEM((CHUNK, HP), F32)
    res = pl.pallas_call(
        body, name=name, grid=(ntile,),
        in_specs=[pl.BlockSpec((tm, SSM_XBC), lambda i: (rev(i), 0)), pl.BlockSpec((tm, SSM_HEADS), lambda i: (rev(i), 0)),
                  pl.BlockSpec((tm, SSM_HEADS), lambda i: (rev(i), 0)),
                  pl.BlockSpec((SSD_CHUNKS, 1, SSM_INNER), lambda i: (rev(i), 0, 0)),
                  whole(d_x), whole(ind), whole(ind_t),
                  pl.BlockSpec((SSD_CHUNKS, SSM_INNER, SSM_STATE), lambda i: (rev(i), 0, 0)),
                  pl.BlockSpec((tm, SSM_INNER), lambda i: (rev(i), 0))] + [hbm] * npair,
        out_specs=[pl.BlockSpec((tm, SSM_XBC), lambda i: (rev(i), 0)), pl.BlockSpec((tm, SSM_HEADS), lambda i: (rev(i), 0)),
                   pl.BlockSpec((tm, SSM_HEADS), lambda i: (rev(i), 0)),
                   pl.BlockSpec((SSD_CHUNKS, 1, SSM_INNER), lambda i: (rev(i), 0, 0)),
                   pl.BlockSpec((SUBLANES, SSM_HEADS), lambda i: (0, 0))] + [hbm] * npair,
        out_shape=[jax.ShapeDtypeStruct((nrow, SSM_XBC), BF16), jax.ShapeDtypeStruct((nrow, SSM_HEADS), F32),
                   jax.ShapeDtypeStruct((nrow, SSM_HEADS), F32), jax.ShapeDtypeStruct((nrow // CHUNK, 1, SSM_INNER), F32),
                   jax.ShapeDtypeStruct((SUBLANES, SSM_HEADS), F32)]
        + [jax.ShapeDtypeStruct(p.shape, p.dtype) for p in pairs],
        scratch_shapes=[pltpu.VMEM((SSM_INNER, SSM_STATE), F32), pltpu.VMEM((CHUNK, SSM_XBC), F32),
                        pltpu.VMEM((CHUNK, SSM_XBC), F32), wide(), wide(), wide(), wide(),
                        pltpu.VMEM((3 * CHUNK, SSM_INNER), F32),
                        group(), group(), group(), group(), group(), pltpu.VMEM((CHUNK, LANES), F32),
                        pltpu.VMEM((SSM_HEADS, SSM_STATE), F32), pltpu.VMEM((SUBLANES, SSM_HEADS), F32)]
        + _exchange_scratch(npair, N_CHIPS - 1),
        compiler_params=_params(("arbitrary",)),
    )(pre, dt, a_cum, a_flat, d_x, ind, ind_t, states, dy, *pairs)
    return res[:5], res[5:]


LATE = ["w_proj_a", "w_proj_b", "w_out", "ffn_w_up", "ffn_w_down"]
HALF_TILES = {"w_in": 128, "w_proj_a": 128, "w_proj_b": 256, "w_out": 128, "ffn_w_up": 128, "ffn_w_down": 176}


def _late_weights(stacks, shards):
    pa, pb, out, up, down = [_own_slot(stack, own) for stack, own in zip(stacks, shards)]
    return {"w_proj_a": pa.reshape(-1, D_MODEL), "w_proj_b": pb.reshape(-1, D_MODEL), "w_out": out.reshape(-1, D_MODEL),
            "w_up_g": _columns_from_chips(up[:2]), "w_up_v": _columns_from_chips(up[2:]),
            "w_down": down.reshape(-1, D_MODEL)}


def _pair_reduce(tag, names, stacks):
    core = lax.axis_index("c")
    own_half = [_row_half(s, core, 1) for s in stacks]
    other_half = _swap_cores("pair_grads_" + tag, [_row_half(s, 1 - core, 1) for s in stacks])
    return [_pair_sum("pair_" + n, a, b, tm=HALF_TILES[n]) for n, a, b in zip(names, own_half, other_half)]


def _local_step(x, target, w, late_shards):
    w = dict(w)
    g = {}
    bs_col = w["gmlp_bs"].reshape(GMLP_GROUPS, GMLP_BLOCK, 1)
    b0, b1 = w["gate_bias"][0:1], w["gate_bias"][1:2]

    xn = _rms_fwd("mix_norm", x, w["mix_norm_w"])
    big = dict(bm=1024, bn=1024, bk=1024)
    act16 = dict(out_dtype=BF16, **big)
    gates = _mm("in_gates", xn, w["w_g"], **act16)
    za = _mm("in_gmlp", xn, w["w_za"], **act16)
    z = _mm("in_z", xn, w["w_z"], **act16)
    xbc = _mm("in_xbc", xn, w["w_xbc"], **act16)
    dt_raw = _mm("in_dt", xn, w["w_dt"], bm=1024, bn=SSM_HEADS, bk=1024)

    pre = _conv_fwd("ssm_conv_fwd", xbc, w["ssm_conv_w"], w["ssm_conv_b"], tc=1024)
    dt, a_cum = _dt_prep("dt_prep", dt_raw, w["ssm_dt_bias"], w["ssm_a_log"])
    a_flat = jnp.transpose(a_cum.reshape(-1, CHUNK, SSM_HEADS), (0, 2, 1)).reshape(-1, 1, SSM_INNER)
    d_x = jnp.repeat(w["ssm_d"], SSM_HEAD_DIM, axis=1)
    ind = _head_indicator()
    y_ssd, states, late_stacks = _ssd_fwd("ssd_fwd", pre, dt, a_cum, a_flat, d_x, ind, late_shards)
    w.update(_late_weights(late_stacks, late_shards))
    yb_pre = _gate_norm_fwd("gate_norm_fwd", y_ssd, z, w["ssm_norm_w"])
    y_b = _mm("proj_b", yb_pre, w["w_proj_b"], bm=1024, bn=1024, bk=SSM_INNER, out_dtype=BF16)

    ya_pre = _gmlp_fwd("gmlp_fwd", za, w["gmlp_ln_w"], w["gmlp_ln_b"], w["gmlp_ws"], bs_col)
    y_a = _mm("proj_a", ya_pre, w["w_proj_a"], **act16)

    merged = _merge_fwd("merge_fwd", gates, y_a, y_b, b0, b1)
    h1 = _mm("out_proj", merged, w["w_out"], res=x, **big)

    hn = _rms_fwd("ffn_norm", h1, w["ffn_norm_w"])
    half = dict(bm=1024, bn=D_FF // 2, bk=1024, out_dtype=BF16)
    pg = _mm("ffn_up_gate", hn, w["w_up_g"], **half)
    pv = _mm("ffn_up_val", hn, w["w_up_v"], **half)
    cw, cb = w["ffn_conv_w"], w["ffn_conv_b"]
    gate, val, act = _ffn_act_fwd("ffn_act_fwd", pg, pv, cw[:, :D_FF], cw[:, D_FF:], cb[:, :D_FF], cb[:, D_FF:],
                                  tc=D_FF // 2)
    h2 = _mm("ffn_down", act, w["w_down"], res=h1, bm=1024, bn=1024, bk=D_FF // 2)

    dh2, loss_part, g["final_norm_w"] = _final_loss("final_loss", h2, target, w["final_norm_w"].reshape(1, D_MODEL))

    dact = _mm("d_act", dh2, w["w_down"], tb=True, **half)
    wgrad = dict(ta=True, bk=min(2048, x.shape[0]), out_dtype=BF16)
    g["w_down"] = _mm("dw_down", act, dh2, bm=D_FF // 2, bn=1024, **wgrad)
    dgate, dval = _ffn_act_bwd("ffn_act_bwd", dact, gate, val)
    dpg, dcwg, dcbg = _conv_bwd("ffn_conv_bwd_gate", dgate, pg, cw[:, :D_FF], tc=D_FF // 2)
    dpv, dcwv, dcbv = _conv_bwd("ffn_conv_bwd_val", dval, pv, cw[:, D_FF:], tc=D_FF // 2)
    g["ffn_conv_w"] = jnp.concatenate([dcwg, dcwv], axis=1)
    g["ffn_conv_b"] = jnp.concatenate([dcbg, dcbv], axis=1)
    dhn = _mm_sum("d_hn", [(dpg, w["w_up_g"]), (dpv, w["w_up_v"])], bm=1024, bk=D_FF // 2)
    g["w_up_g"] = _mm("dw_up_gate", hn, dpg, bm=1024, bn=D_FF // 2, **wgrad)
    g["w_up_v"] = _mm("dw_up_val", hn, dpv, bm=1024, bn=D_FF // 2, **wgrad)
    dh1, g["ffn_norm_w"] = _rms_bwd("ffn_norm_bwd", h1, w["ffn_norm_w"], dhn, dh2)

    dmerged = _mm("d_merged", dh1, w["w_out"], tb=True, **act16)
    g["w_out"] = _mm("dw_out", merged, dh1, bm=1024, bn=1024, **wgrad)
    dgates, dya, dyb, db0, db1 = _merge_bwd("merge_bwd", gates, y_a, y_b, dmerged, b0, b1)
    g["gate_bias"] = jnp.concatenate([db0, db1], axis=0)

    dya_pre = _mm("d_ya_pre", dya, w["w_proj_a"], tb=True, **act16)
    g["w_proj_a"] = _mm("dw_proj_a", ya_pre, dya, bm=1024, bn=1024, **wgrad)
    dyb_pre = _mm("d_yb_pre", dyb, w["w_proj_b"], tb=True, **act16)
    g["w_proj_b"] = _mm("dw_proj_b", yb_pre, dyb, bm=1024, bn=1024, **wgrad)
    late_pairs = _pair_reduce("late", LATE, [
        g["w_proj_a"].reshape(N_CHIPS, -1, D_MODEL), g["w_proj_b"].reshape(N_CHIPS, -1, D_MODEL),
        g["w_out"].reshape(N_CHIPS, -1, D_MODEL),
        jnp.concatenate([_columns_to_chips(g["w_up_g"], 2), _columns_to_chips(g["w_up_v"], 2)], axis=0),
        g["w_down"].reshape(N_CHIPS, -1, D_MODEL)])

    dy_ssd, dz, g["ssm_norm_w"] = _gate_norm_bwd("gate_norm_bwd", y_ssd, z, dyb_pre, w["ssm_norm_w"])
    (dpre, ddt, da_tok, da_flat, dd), late_received = _ssd_bwd(
        "ssd_bwd", pre, dt, a_cum, a_flat, d_x, ind, ind.T, states, dy_ssd, late_pairs)
    g["ssm_d"] = dd[0:1]
    da_src = jnp.transpose(da_flat.reshape(-1, SSM_HEADS, CHUNK), (0, 2, 1)).reshape(-1, SSM_HEADS)
    ddt_raw, g["ssm_dt_bias"], g["ssm_a_log"] = _dt_bwd("dt_bwd", dt_raw, ddt, da_tok, da_src,
                                                         w["ssm_dt_bias"], w["ssm_a_log"])
    dxbc, g["ssm_conv_w"], g["ssm_conv_b"] = _conv_bwd("ssm_conv_bwd", dpre, xbc, w["ssm_conv_w"], tc=1024)

    dza, g["gmlp_ln_w"], g["gmlp_ln_b"], g["gmlp_ws"], dbs = _gmlp_bwd(
        "gmlp_bwd", za, dya_pre, w["gmlp_ln_w"], w["gmlp_ln_b"], w["gmlp_ws"], bs_col)
    g["gmlp_bs"] = dbs.reshape(GMLP_GROUPS, GMLP_BLOCK)

    dw_in = jnp.concatenate([
        _mm("dw_gates", xn, dgates, bm=1024, bn=1024, **wgrad), _mm("dw_gmlp", xn, dza, bm=1024, bn=1024, **wgrad),
        _mm("dw_z", xn, dz, bm=1024, bn=1024, **wgrad), _mm("dw_xbc", xn, dxbc, bm=1024, bn=1024, **wgrad),
        _mm("dw_dt", xn, ddt_raw, bm=1024, bn=SSM_HEADS, **wgrad)], axis=1)
    in_pairs = _pair_reduce("in", ["w_in"], [_columns_to_chips(dw_in)])
    dxn, in_received = _mm_sum("d_xn", [(dgates, w["w_g"]), (dza, w["w_za"]), (dz, w["w_z"]), (dxbc, w["w_xbc"]),
                                        (ddt_raw, w["w_dt"])], bm=1024, bk=1024, exchange=in_pairs)
    grad_x, g["mix_norm_w"] = _rms_bwd("mix_norm_bwd", x, w["mix_norm_w"], dxn, dh1)
    return loss_part, grad_x, g, in_pairs + list(late_pairs), list(in_received) + list(late_received)


def _position():
    return lax.axis_index("x"), lax.axis_index("y"), lax.axis_index("c")


def _own_slot(stack, own):
    chip = 2 * lax.axis_index("x") + lax.axis_index("y")
    return lax.dynamic_update_index_in_dim(stack, own, chip, axis=0)


def _scatter_phases(ins, outs, send_sems, recv_sems):
    n = len(ins)
    x, y, c = _position()
    me = 2 * x + y
    peers = [(1 - x, y), (x, 1 - y), (1 - x, 1 - y)]

    def copy(i, k, src_slot, dst_slot):
        px, py = peers[k]
        return pltpu.make_async_remote_copy(
            src_ref=ins[i].at[src_slot], dst_ref=outs[i].at[dst_slot],
            send_sem=send_sems.at[i, k], recv_sem=recv_sems.at[i, k],
            device_id=(px, py, c), device_id_type=MESH)

    def start():
        for i in range(n):
            for k, (px, py) in enumerate(peers):
                copy(i, k, 2 * px + py, me).start()

    def finish():
        for i in range(n):
            for k, (px, py) in enumerate(peers):
                copy(i, k, me, 2 * px + py).wait_recv()
        for i in range(n):
            for k, (px, py) in enumerate(peers):
                copy(i, k, 2 * px + py, me).wait_send()

    return start, finish


def _exchange_scratch(n, per_array):
    return [pltpu.SemaphoreType.DMA((n, per_array)), pltpu.SemaphoreType.DMA((n, per_array))]


def _half_rows(ref_rows, which):
    half = ref_rows // 2
    return pl.ds(pl.multiple_of(which * half, 2 * SUBLANES), half)


GATHER_SEMS = 2 * (N_CHIPS - 1)


def _gather_phases(nrows, ns, ins, outs, send_sems, recv_sems):
    n = len(ins)
    x, y, c = _position()
    me = 2 * x + y
    sibling = (x, y, 1 - c)
    chips = [(1 - x, y), (x, 1 - y), (1 - x, 1 - y)]

    def remote(i, k, src, dst, to):
        return pltpu.make_async_remote_copy(src_ref=src, dst_ref=dst, send_sem=send_sems.at[i, k],
                                            recv_sem=recv_sems.at[i, k], device_id=to, device_id_type=MESH)

    def over_ici(i, k):
        px, py = chips[k]
        rows = _half_rows(nrows[i], c) if i < ns else slice(None)
        return remote(i, k, ins[i].at[rows], outs[i].at[me, rows], (px, py, c))

    def landed(i, k, which):
        px, py = chips[k]
        return outs[i].at[2 * px + py, _half_rows(nrows[i], which)] if i < ns else outs[i].at[2 * px + py]

    def start():
        for i in range(n):
            for k in range(N_CHIPS - 1):
                over_ici(i, k).start()

    def forward():
        for i in range(n):
            for k in range(N_CHIPS - 1):
                piece = landed(i, k, c)
                remote(i, k, piece, piece, (*chips[k], c)).wait_recv()
                if i < ns:
                    remote(i, N_CHIPS - 1 + k, piece, piece, sibling).start()

    def finish():
        for i in range(ns):
            for k in range(N_CHIPS - 1):
                piece = landed(i, k, 1 - c)
                remote(i, N_CHIPS - 1 + k, piece, piece, sibling).wait_recv()
        for i in range(n):
            for k in range(N_CHIPS - 1):
                over_ici(i, k).wait_send()
                if i < ns:
                    piece = landed(i, k, c)
                    remote(i, N_CHIPS - 1 + k, piece, piece, sibling).wait_send()

    return start, forward, finish


def _gather_chips_split(name, split, whole):
    arrs = list(split) + list(whole)
    n = len(arrs)

    def body(*refs):
        phases = _gather_phases([a.shape[0] for a in arrs], len(split), refs[:n], refs[n:2 * n], *refs[2 * n:])
        for phase in phases:
            phase()

    hbm = pl.BlockSpec(memory_space=pl.ANY)
    return pl.pallas_call(
        body, name=name, in_specs=[hbm] * n, out_specs=[hbm] * n,
        out_shape=[jax.ShapeDtypeStruct((N_CHIPS,) + a.shape, a.dtype) for a in arrs],
        scratch_shapes=_exchange_scratch(n, GATHER_SEMS),
        compiler_params=pltpu.CompilerParams(has_side_effects=True),
    )(*arrs)


def _swap_cores(name, arrs):
    n = len(arrs)

    def body(*refs):
        ins, outs = refs[:n], refs[n:2 * n]
        send_sems, recv_sems = refs[2 * n:]
        x, y, c = _position()
        copies = [pltpu.make_async_remote_copy(src_ref=ins[i], dst_ref=outs[i], send_sem=send_sems.at[i],
                                               recv_sem=recv_sems.at[i], device_id=(x, y, 1 - c), device_id_type=MESH)
                  for i in range(n)]
        for cp in copies:
            cp.start()
        for cp in copies:
            cp.wait_recv()
        for cp in copies:
            cp.wait_send()

    hbm = pl.BlockSpec(memory_space=pl.ANY)
    return pl.pallas_call(
        body, name=name, in_specs=[hbm] * n, out_specs=[hbm] * n,
        out_shape=[jax.ShapeDtypeStruct(a.shape, a.dtype) for a in arrs],
        scratch_shapes=[pltpu.SemaphoreType.DMA((n,)), pltpu.SemaphoreType.DMA((n,))],
        compiler_params=pltpu.CompilerParams(has_side_effects=True),
    )(*arrs)


def _row_half(a, which, axis):
    half = a.shape[axis] // 2
    return lax.dynamic_slice_in_dim(a, which * half, half, axis=axis)


def _all_reduce(name, pack):
    def body(in_ref, out_ref, buf, send_sems, recv_sems):
        x, y, c = _position()
        me = 4 * x + 2 * y + c
        flips = [(dx, dy, dc) for dx in (0, 1) for dy in (0, 1) for dc in (0, 1) if (dx, dy, dc) != (0, 0, 0)]
        peers = [((1 - x) if dx else x, (1 - y) if dy else y, (1 - c) if dc else c) for dx, dy, dc in flips]
        buf[me] = in_ref[...]
        sends = []
        for k, peer in enumerate(peers):
            cp = pltpu.make_async_remote_copy(src_ref=in_ref, dst_ref=buf.at[me], send_sem=send_sems.at[k],
                                              recv_sem=recv_sems.at[k], device_id=peer, device_id_type=MESH)
            cp.start()
            sends.append(cp)
        for k, (px, py, pc) in enumerate(peers):
            pltpu.make_async_remote_copy(src_ref=in_ref, dst_ref=buf.at[4 * px + 2 * py + pc], send_sem=send_sems.at[k],
                                         recv_sem=recv_sems.at[k], device_id=(px, py, pc), device_id_type=MESH).wait_recv()
        total = buf[0]
        for j in range(1, N_DEV):
            total = total + buf[j]
        out_ref[...] = total
        for cp in sends:
            cp.wait_send()

    vmem = pl.BlockSpec(memory_space=pltpu.VMEM)
    return pl.pallas_call(
        body, name=name, in_specs=[vmem], out_specs=vmem,
        out_shape=jax.ShapeDtypeStruct(pack.shape, F32),
        scratch_shapes=[pltpu.VMEM((N_DEV,) + pack.shape, F32), pltpu.SemaphoreType.DMA((N_DEV - 1,)),
                        pltpu.SemaphoreType.DMA((N_DEV - 1,))],
        compiler_params=pltpu.CompilerParams(has_side_effects=True, vmem_limit_bytes=VMEM_LIMIT_V7X),
    )(pack)


def _pack(arrs):
    rows = [a.reshape(-1, LANES) for a in arrs]
    total = sum(r.shape[0] for r in rows)
    rows.append(jnp.zeros((-total % SUBLANES, LANES), F32))
    return jnp.concatenate(rows, axis=0)


def _unpack(pack, shapes):
    out, off = [], 0
    for s in shapes:
        nrow = 1
        for d in s:
            nrow *= d
        nrow //= LANES
        out.append(pack[off:off + nrow].reshape(s))
        off += nrow
    return out


SMALL = ["mix_norm_w", "gate_bias", "gmlp_ln_w", "gmlp_ln_b", "gmlp_ws", "gmlp_bs", "ssm_conv_w", "ssm_conv_b",
         "ssm_dt_bias", "ssm_a_log", "ssm_d", "ssm_norm_w", "ffn_norm_w", "ffn_conv_w", "ffn_conv_b", "final_norm_w"]
SMALL_SHARDED = ("gate_bias", "ssm_conv_w", "ffn_conv_w")
BIG = ["w_in", "w_proj_a", "w_proj_b", "w_out", "ffn_w_up", "ffn_w_down"]
WEIGHTS = ["mix_norm_w", "w_in", "gate_bias", "gmlp_ln_w", "gmlp_ln_b", "gmlp_ws", "gmlp_bs", "ssm_conv_w",
           "ssm_conv_b", "ssm_dt_bias", "ssm_a_log", "ssm_d", "ssm_norm_w", "w_proj_a", "w_proj_b", "w_out",
           "ffn_norm_w", "ffn_w_up", "ffn_conv_w", "ffn_conv_b", "ffn_w_down", "final_norm_w"]
IN_SPLITS = [0, 2048, 4096, 6144, 9216, 9248]


def _columns_from_chips(stack):
    return jnp.transpose(stack, (1, 0, 2)).reshape(stack.shape[1], -1)


def _columns_to_chips(full, parts=N_CHIPS):
    rows, cols = full.shape
    return jnp.transpose(full.reshape(rows, parts, cols // parts), (1, 0, 2))


def kernel(x, mix_norm_w, w_in, gate_bias, gmlp_ln_w, gmlp_ln_b, gmlp_ws, gmlp_bs, ssm_conv_w, ssm_conv_b, ssm_dt_bias, ssm_a_log, ssm_d, ssm_norm_w, w_proj_a, w_proj_b, w_out, ffn_norm_w, ffn_w_up, ffn_conv_w, ffn_conv_b, ffn_w_down, final_norm_w, loss_target, m_mix_norm_w, m_w_in, m_gate_bias, m_gmlp_ln_w, m_gmlp_ln_b, m_gmlp_ws, m_gmlp_bs, m_ssm_conv_w, m_ssm_conv_b, m_ssm_dt_bias, m_ssm_a_log, m_ssm_d, m_ssm_norm_w, m_w_proj_a, m_w_proj_b, m_w_out, m_ffn_norm_w, m_ffn_w_up, m_ffn_conv_w, m_ffn_conv_b, m_ffn_w_down, m_final_norm_w, v_mix_norm_w, v_w_in, v_gate_bias, v_gmlp_ln_w, v_gmlp_ln_b, v_gmlp_ws, v_gmlp_bs, v_ssm_conv_w, v_ssm_conv_b, v_ssm_dt_bias, v_ssm_a_log, v_ssm_d, v_ssm_norm_w, v_w_proj_a, v_w_proj_b, v_w_out, v_ffn_norm_w, v_ffn_w_up, v_ffn_conv_w, v_ffn_conv_b, v_ffn_w_down, v_final_norm_w):
    args = dict(locals())
    weights = {n: args[n] for n in WEIGHTS}
    moments_m = {n: args["m_" + n] for n in WEIGHTS}
    moments_v = {n: args["v_" + n] for n in WEIGHTS}
    chip = 2 * lax.axis_index("x") + lax.axis_index("y")

    shards = [weights["w_in"][0].astype(BF16)] + [weights[n][0] for n in SMALL_SHARDED]
    gathered = _gather_chips_split("gather_weights", shards[:1], shards[1:])
    w_in_s, gb_s, scw_s, fcw_s = [_own_slot(stack, own) for stack, own in zip(gathered, shards)]
    late_shards = [weights[n][0].astype(BF16) for n in LATE]
    w_in_full = _columns_from_chips(w_in_s)
    full = {"w_" + nm: w_in_full[:, IN_SPLITS[k]:IN_SPLITS[k + 1]] for k, nm in enumerate(["g", "za", "z", "xbc", "dt"])}
    full["gate_bias"] = _columns_from_chips(gb_s)
    full["ssm_conv_w"] = _columns_from_chips(scw_s)
    full["ffn_conv_w"] = _columns_from_chips(fcw_s)
    for n in SMALL:
        if n not in SMALL_SHARDED:
            full[n] = weights[n] if n == "final_norm_w" else weights[n][0]
    for n in ("mix_norm_w", "ffn_norm_w", "ssm_conv_b", "ssm_dt_bias", "ssm_a_log", "ssm_d", "ssm_norm_w", "ffn_conv_b"):
        full[n] = full[n].reshape(1, -1)

    loss_part, grad_x, g, pair, received = _local_step(x[0], loss_target[0], full, late_shards)

    per_head = ["ssm_dt_bias", "ssm_a_log", "ssm_d"]
    rest = [n for n in SMALL if n not in per_head]
    head_row = jnp.concatenate([g[n] for n in per_head] + [jnp.zeros((1, LANES - 3 * SSM_HEADS), F32)], axis=1)
    pack = _pack([loss_part, head_row] + [g[n] for n in rest])
    reduced = _unpack(_all_reduce("reduce_small", pack), [(1, LANES), (1, LANES)] + [g[n].shape for n in rest])
    loss = reduced[0][0, 0]
    small_grads = {n: reduced[1][:, k * SSM_HEADS:(k + 1) * SSM_HEADS] for k, n in enumerate(per_head)}
    for n, r in zip(rest, reduced[2:]):
        if n in SMALL_SHARDED:
            width = weights[n].shape[2]
            r = lax.dynamic_slice_in_dim(r, chip * width, width, axis=1)
        small_grads[n] = r
    two_d = lambda a: a.reshape(-1, a.shape[-1])
    upd = _adamw_small("adamw_small", *[[two_d(d[n]) for n in SMALL]
                                        for d in (weights, small_grads, moments_m, moments_v)])
    small_out = [[small_grads[n] for n in SMALL]] + list(upd)
    small_out = [[a.reshape(weights[n].shape) for n, a in zip(SMALL, kind)] for kind in small_out]

    received = [_own_slot(r, lax.dynamic_index_in_dim(p, chip, 0, keepdims=False)) for r, p in zip(received, pair)]
    halves = [_sum_slots("sum_" + n, r, tm=HALF_TILES[n], rs=2 * SUBLANES) for n, r in zip(BIG, received)]
    tiles = {"w_in": 128, "w_proj_a": 256, "w_proj_b": 256, "w_out": 256, "ffn_w_up": 128, "ffn_w_down": 176}
    core = lax.axis_index("c")
    other = _swap_cores("join_grads", halves)
    reduced = [jnp.concatenate([jnp.where(core == 0, a, b), jnp.where(core == 0, b, a)], axis=0)
               for a, b in zip(halves, other)]
    big_out = {}
    for n, grad in zip(BIG, reduced):
        big_out[n] = _adamw("adamw_" + n, weights[n][0], grad, moments_m[n][0], moments_v[n][0],
                            tm=tiles[n], rs=SUBLANES)

    per_kind = [[], [], [], []]
    for n in WEIGHTS:
        for kind in range(4):
            if n in big_out:
                per_kind[kind].append(big_out[n][kind].reshape(weights[n].shape))
            else:
                per_kind[kind].append(small_out[kind][SMALL.index(n)])
    return (loss, grad_x[None], *per_kind[0], *per_kind[1], *per_kind[2], *per_kind[3])
```

```python
import jax
import jax.numpy as jnp
from jax import lax
from jax.experimental import pallas as pl
from jax.experimental.pallas import tpu as pltpu

F32 = jnp.float32
BF16 = jnp.bfloat16
MESH = pl.DeviceIdType.MESH

EPS = 1e-5
D_MODEL = 1024
GMLP_BLOCK = 128
GMLP_GROUPS = 8
CHUNK = 64
SSM_INNER = 2048
SSM_HEADS = 32
SSM_HEAD_DIM = 64
SSM_GROUPS = 4
SSM_HPG = 8
SSM_STATE = 128
SSM_CONV = 4
SSM_XBC = 3072
D_FF = 2816
FFN_CONV = 3
N_CHIPS = 4
N_DEV = 8

ADAM_LR = 0.001
ADAM_B1 = 0.9
ADAM_B2 = 0.999
ADAM_EPS = 1e-08
ADAM_WD = 0.01
ADAM_STEP = 10

VMEM_LIMIT_V7X = 56 * 1024 * 1024
SUBLANES = 8
LANES = 128


def _params(sem=None):
    return pltpu.CompilerParams(dimension_semantics=sem, vmem_limit_bytes=VMEM_LIMIT_V7X)


def _dot(a, b, ca=1, cb=0):
    return lax.dot_general(a.astype(BF16), b.astype(BF16), (((ca,), (cb,)), ((), ())),
                           preferred_element_type=F32)


def _mm(name, a, b, *, ta=False, tb=False, out_dtype=F32, bm, bn, bk, res=None):
    m, k = (a.shape[1], a.shape[0]) if ta else a.shape
    k2, n = (b.shape[1], b.shape[0]) if tb else b.shape
    assert k == k2 and m % bm == 0 and n % bn == 0 and k % bk == 0, (name, a.shape, b.shape)
    nk = k // bk
    a_spec = (pl.BlockSpec((bk, bm), lambda i, j, kk: (kk, i)) if ta
              else pl.BlockSpec((bm, bk), lambda i, j, kk: (i, kk)))
    b_spec = (pl.BlockSpec((bn, bk), lambda i, j, kk: (j, kk)) if tb
              else pl.BlockSpec((bk, bn), lambda i, j, kk: (kk, j)))
    o_spec = pl.BlockSpec((bm, bn), lambda i, j, kk: (i, j))
    has_res = res is not None

    def body(*refs):
        a_ref, b_ref = refs[0], refs[1]
        r_ref = refs[2] if has_res else None
        o_ref = refs[3] if has_res else refs[2]
        p = _dot(a_ref[...], b_ref[...], 0 if ta else 1, 1 if tb else 0)

        def finish(total):
            if has_res:
                total = total + r_ref[...]
            o_ref[...] = total.astype(out_dtype)

        if nk == 1:
            finish(p)
        else:
            acc_ref = refs[-1]
            kk = pl.program_id(2)

            @pl.when(kk == 0)
            def _():
                acc_ref[...] = p

            @pl.when(kk > 0)
            def _():
                acc_ref[...] += p

            @pl.when(kk == nk - 1)
            def _():
                finish(acc_ref[...])

    return pl.pallas_call(
        body, name=name,
        grid=(m // bm, n // bn, nk),
        in_specs=[a_spec, b_spec] + ([o_spec] if has_res else []),
        out_specs=o_spec,
        out_shape=jax.ShapeDtypeStruct((m, n), out_dtype),
        scratch_shapes=[pltpu.VMEM((bm, bn), F32)] if nk > 1 else [],
        compiler_params=_params(("parallel", "parallel", "arbitrary")),
    )(*([a, b] + ([res] if has_res else [])))


def _mm_sum(name, pairs, *, bm, bk, exchange=()):
    nx = len(exchange)
    npair = len(pairs)
    m, n = pairs[0][0].shape[0], pairs[0][1].shape[0]
    steps, first = [], []
    for a, b in pairs:
        k = a.shape[1]
        assert a.shape[0] == m and b.shape == (n, k) and m % bm == 0 and (k % bk == 0 or k < bk), (name, a.shape, b.shape)
        first.append(sum(steps))
        steps.append(max(k // bk, 1))
    total = sum(steps)
    in_specs = []
    for (a, b), off, cnt in zip(pairs, first, steps):
        width = min(bk, a.shape[1])
        in_specs.append(pl.BlockSpec((bm, width), lambda i, kk, off=off, cnt=cnt: (i, jnp.clip(kk - off, 0, cnt - 1))))
        in_specs.append(pl.BlockSpec((n, width), lambda i, kk, off=off, cnt=cnt: (0, jnp.clip(kk - off, 0, cnt - 1))))

    def body(*refs):
        send_refs = refs[2 * npair:2 * npair + nx]
        o_ref = refs[2 * npair + nx]
        recv_refs = refs[2 * npair + nx + 1:2 * npair + 2 * nx + 1]
        acc_ref = refs[2 * npair + 2 * nx + 1]
        i, kk = pl.program_id(0), pl.program_id(1)
        if nx:
            start, finish = _scatter_phases(send_refs, recv_refs, *refs[2 * npair + 2 * nx + 2:])

            @pl.when((i == 0) & (kk == 0))
            def _():
                start()

        for s, (off, cnt) in enumerate(zip(first, steps)):
            @pl.when((kk >= off) & (kk < off + cnt))
            def _(s=s, off=off):
                p = _dot(refs[2 * s][...], refs[2 * s + 1][...], 1, 1)
                if off == 0:
                    @pl.when(kk == 0)
                    def _():
                        acc_ref[...] = p

                    @pl.when(kk > 0)
                    def _():
                        acc_ref[...] += p
                else:
                    acc_ref[...] += p

        @pl.when(kk == total - 1)
        def _():
            o_ref[...] = acc_ref[...]

        if nx:
            @pl.when((i == m // bm - 1) & (kk == total - 1))
            def _():
                finish()

    hbm = pl.BlockSpec(memory_space=pl.ANY)
    res = pl.pallas_call(
        body, name=name, grid=(m // bm, total),
        in_specs=in_specs + [hbm] * nx, out_specs=[pl.BlockSpec((bm, n), lambda i, kk: (i, 0))] + [hbm] * nx,
        out_shape=[jax.ShapeDtypeStruct((m, n), F32)] + [jax.ShapeDtypeStruct(e.shape, e.dtype) for e in exchange],
        scratch_shapes=[pltpu.VMEM((bm, n), F32)] + (_exchange_scratch(nx, N_CHIPS - 1) if nx else []),
        compiler_params=_params(("arbitrary", "arbitrary")),
    )(*[t for pair in pairs for t in pair], *exchange)
    return (res[0], res[1:]) if nx else res[0]


def _rows(name, fn, ins, params, outs, accs, *, tm, rs, unroll=4):
    nrow = ins[0][0].shape[-2]
    while tm % (rs * unroll):
        unroll //= 2
    assert nrow % tm == 0 and tm % rs == 0, (name, nrow, tm, rs)
    n_in, n_p, n_out, n_acc = len(ins), len(params), len(outs), len(accs)
    in_specs = []
    for spec in ins:
        arr, width, cb = spec[:3]
        if len(spec) == 4:
            in_specs.append(pl.BlockSpec((None, tm, width), lambda i, cb=cb, lead=spec[3]: (lead, i, cb)))
        else:
            in_specs.append(pl.BlockSpec((tm, width), lambda i, cb=cb: (i, cb)))
    for p in params:
        in_specs.append(pl.BlockSpec(p.shape, lambda i, nd=p.ndim: (0,) * nd))
    out_specs = [pl.BlockSpec((tm, w), lambda i: (i, 0)) for w, _ in outs]
    out_specs += [pl.BlockSpec(s, lambda i: (0, 0)) for s in accs]
    out_shape = [jax.ShapeDtypeStruct((nrow, w), dt) for w, dt in outs]
    out_shape += [jax.ShapeDtypeStruct(s, F32) for s in accs]

    def body(*refs):
        in_refs = refs[:n_in]
        p_refs = refs[n_in:n_in + n_p]
        o_refs = refs[n_in + n_p:n_in + n_p + n_out]
        a_refs = refs[n_in + n_p + n_out:]
        pv = [p[...] for p in p_refs]

        if n_acc:
            @pl.when(pl.program_id(0) == 0)
            def _():
                for a_ref in a_refs:
                    a_ref[...] = jnp.zeros_like(a_ref)

        def step(r, carry):
            for u in range(unroll):
                sl = pl.ds(pl.multiple_of((r * unroll + u) * rs, rs), rs)
                vals = [ref[sl, :].astype(F32) for ref in in_refs]
                row_out, sums = fn(*vals, *pv)
                for o_ref, v in zip(o_refs, row_out):
                    o_ref[sl, :] = v.astype(o_ref.dtype)
                carry = tuple(c + s for c, s in zip(carry, sums))
            return carry

        init = tuple(jnp.zeros(s, F32) for s in accs)
        total = lax.fori_loop(0, tm // (rs * unroll), step, init)
        for a_ref, t in zip(a_refs, total):
            a_ref[...] += t

    res = pl.pallas_call(
        body, name=name, grid=(nrow // tm,),
        in_specs=in_specs, out_specs=out_specs, out_shape=out_shape,
        compiler_params=_params(("arbitrary",)),
    )(*([s[0] for s in ins] + list(params)))
    return res


def _rms(x, w):
    return x * lax.rsqrt(jnp.mean(x * x, axis=-1, keepdims=True) + EPS) * w


def _colsum(v):
    return jnp.sum(v, axis=0, keepdims=True)


def _rms_fwd(name, x, w):
    def fn(xv, wv):
        return (_rms(xv, wv),), ()
    return _rows(name, fn, [(x, D_MODEL, 0)], [w], [(D_MODEL, BF16)], [], tm=512, rs=16)[0]


def _rms_bwd(name, x, w, dy, dres):
    def fn(xv, dyv, drv, wv):
        _, vjp = jax.vjp(_rms, xv, wv)
        dx, dw = vjp(dyv)
        return (drv + dx,), (dw,)
    return _rows(name, fn, [(x, D_MODEL, 0), (dy, D_MODEL, 0), (dres, D_MODEL, 0)], [w],
                 [(D_MODEL, F32)], [(1, D_MODEL)], tm=512, rs=16)


def _final_loss(name, h, target, w):
    def fn(hv, tv, wv):
        y, vjp = jax.vjp(_rms, hv, wv)
        err = y - tv
        part = 0.5 * jnp.sum(jnp.mean(err * err, axis=-1, keepdims=True), axis=0, keepdims=True)
        dh, dw = vjp(err / D_MODEL)
        return (dh,), (jnp.broadcast_to(part, (1, LANES)), dw)
    return _rows(name, fn, [(h, D_MODEL, 0), (target, D_MODEL, 0)], [w],
                 [(D_MODEL, F32)], [(1, LANES), (1, D_MODEL)], tm=512, rs=16)


def _merge(ga, gb, ya, yb, b0, b1):
    return jax.nn.sigmoid(ga + b0) * ya + jax.nn.sigmoid(gb + b1) * yb


def _merge_fwd(name, g, ya, yb, b0, b1):
    def fn(ga, gb, yav, ybv, b0v, b1v):
        return (_merge(ga, gb, yav, ybv, b0v, b1v),), ()
    return _rows(name, fn, [(g, D_MODEL, 0), (g, D_MODEL, 1), (ya, D_MODEL, 0), (yb, D_MODEL, 0)],
                 [b0, b1], [(D_MODEL, BF16)], [], tm=512, rs=16)[0]


def _merge_bwd(name, g, ya, yb, dm, b0, b1):
    def fn(ga, gb, yav, ybv, dmv, b0v, b1v):
        _, vjp = jax.vjp(_merge, ga, gb, yav, ybv, b0v, b1v)
        dga, dgb, dya, dyb, db0, db1 = vjp(dmv)
        return (jnp.concatenate([dga, dgb], axis=1), dya, dyb), (db0, db1)
    return _rows(name, fn,
                 [(g, D_MODEL, 0), (g, D_MODEL, 1), (ya, D_MODEL, 0), (yb, D_MODEL, 0), (dm, D_MODEL, 0)],
                 [b0, b1], [(2 * D_MODEL, BF16), (D_MODEL, BF16), (D_MODEL, BF16)],
                 [(1, D_MODEL), (1, D_MODEL)], tm=512, rs=16)


GROUP_W = SSM_INNER // SSM_GROUPS


def _gate_norm_group(y, z, nw):
    v = y * jax.nn.silu(z)
    return v * lax.rsqrt(jnp.mean(v * v, axis=-1, keepdims=True) + EPS) * nw


def _gate_norm_fwd(name, y, z, nw):
    def fn(yv, zv, nwv):
        parts = [_gate_norm_group(yv[:, k * GROUP_W:(k + 1) * GROUP_W], zv[:, k * GROUP_W:(k + 1) * GROUP_W],
                                  nwv[:, k * GROUP_W:(k + 1) * GROUP_W]) for k in range(SSM_GROUPS)]
        return (jnp.concatenate(parts, axis=1),), ()
    return _rows(name, fn, [(y, SSM_INNER, 0), (z, SSM_INNER, 0)], [nw], [(SSM_INNER, BF16)], [],
                 tm=512, rs=16)[0]


def _gate_norm_bwd(name, y, z, dout, nw):
    def fn(yv, zv, dv, nwv):
        dys, dzs, dns = [], [], []
        for k in range(SSM_GROUPS):
            sl = slice(k * GROUP_W, (k + 1) * GROUP_W)
            _, vjp = jax.vjp(_gate_norm_group, yv[:, sl], zv[:, sl], nwv[:, sl])
            dy, dz, dn = vjp(dv[:, sl])
            dys.append(dy), dzs.append(dz), dns.append(dn)
        return (jnp.concatenate(dys, axis=1), jnp.concatenate(dzs, axis=1)), (jnp.concatenate(dns, axis=1),)
    return _rows(name, fn, [(y, SSM_INNER, 0), (z, SSM_INNER, 0), (dout, SSM_INNER, 0)], [nw],
                 [(SSM_INNER, BF16), (SSM_INNER, BF16)], [(1, SSM_INNER)], tm=512, rs=16)


def _softplus(v):
    return jnp.maximum(v, 0.0) + jnp.log1p(jnp.exp(-jnp.abs(v)))


def _chunk_cumsum(v, reverse=False):
    row = lax.broadcasted_iota(jnp.int32, v.shape, 0)
    step = 1
    while step < CHUNK:
        if reverse:
            shifted = pltpu.roll(v, CHUNK - step, axis=0)
            v = v + jnp.where(row < CHUNK - step, shifted, 0.0)
        else:
            shifted = pltpu.roll(v, step, axis=0)
            v = v + jnp.where(row >= step, shifted, 0.0)
        step *= 2
    return v


def _dt_prep(name, dt_raw, dt_bias, a_log):
    def fn(rv, bv, alv):
        dt = _softplus(rv + bv)
        return (dt, _chunk_cumsum(dt * (-jnp.exp(alv)))), ()
    return _rows(name, fn, [(dt_raw, SSM_HEADS, 0)], [dt_bias, a_log],
                 [(SSM_HEADS, F32), (SSM_HEADS, F32)], [], tm=512, rs=CHUNK)


def _dt_bwd(name, dt_raw, ddt, da1, da2, dt_bias, a_log):
    def fn(rv, ddv, d1, d2, bv, alv):
        pre = rv + bv
        dt = _softplus(pre)
        a_neg = -jnp.exp(alv)
        back = _chunk_cumsum(d1 + d2, reverse=True)
        d_dt = ddv + back * a_neg
        d_raw = d_dt * jax.nn.sigmoid(pre)
        return (d_raw,), (_colsum(d_raw), _colsum(back * dt) * a_neg)
    return _rows(name, fn, [(dt_raw, SSM_HEADS, 0), (ddt, SSM_HEADS, 0), (da1, SSM_HEADS, 0), (da2, SSM_HEADS, 0)],
                 [dt_bias, a_log], [(SSM_HEADS, BF16)], [(1, SSM_HEADS), (1, SSM_HEADS)], tm=512, rs=CHUNK)


def _adamw_math(w, g, m, v):
    m_new = ADAM_B1 * m + (1.0 - ADAM_B1) * g
    v_new = ADAM_B2 * v + (1.0 - ADAM_B2) * jnp.square(g)
    m_hat = m_new / (1.0 - ADAM_B1 ** ADAM_STEP)
    v_hat = v_new / (1.0 - ADAM_B2 ** ADAM_STEP)
    delta = -ADAM_LR * (m_hat / (jnp.sqrt(v_hat) + ADAM_EPS) + ADAM_WD * w)
    return delta, m_new, v_new


def _adamw(name, w, g, m, v, *, tm, rs):
    width = w.shape[1]

    def fn(wv, mv, vv, gv):
        return (gv,) + _adamw_math(wv, gv, mv, vv), ()
    return _rows(name, fn, [(w, width, 0), (m, width, 0), (v, width, 0), (g, width, 0)],
                 [], [(width, F32)] * 4, [], tm=tm, rs=rs)


def _adamw_small(name, ws, gs, ms, vs):
    n = len(ws)

    def body(*refs):
        w_refs, g_refs, m_refs, v_refs = (refs[k * n:(k + 1) * n] for k in range(4))
        outs = refs[4 * n:]
        for i in range(n):
            res = _adamw_math(w_refs[i][...], g_refs[i][...], m_refs[i][...], v_refs[i][...])
            for k in range(3):
                outs[k * n + i][...] = res[k]

    vmem = pl.BlockSpec(memory_space=pltpu.VMEM)
    res = pl.pallas_call(
        body, name=name, in_specs=[vmem] * (4 * n), out_specs=[vmem] * (3 * n),
        out_shape=[jax.ShapeDtypeStruct(w.shape, F32) for w in ws] * 3,
        compiler_params=pltpu.CompilerParams(vmem_limit_bytes=VMEM_LIMIT_V7X),
    )(*ws, *gs, *ms, *vs)
    return res[:n], res[n:2 * n], res[2 * n:]


def _pair_sum(name, a, b, *, tm):
    shape = a.shape
    flat = (shape[0] * shape[1], shape[2])

    def fn(av, bv):
        return (av.astype(F32) + bv.astype(F32),), ()
    out = _rows(name, fn, [(a.reshape(flat), flat[1], 0), (b.reshape(flat), flat[1], 0)], [], [(flat[1], BF16)], [],
                tm=tm, rs=2 * SUBLANES)[0]
    return out.reshape(shape)


def _sum_slots(name, stack, *, tm, rs):
    width = stack.shape[2]

    def fn(*slots):
        s0, s1, s2, s3 = (s.astype(F32) for s in slots)
        return (((s0 + s1) + s2) + s3,), ()
    return _rows(name, fn, [(stack, width, 0, k) for k in range(N_CHIPS)], [], [(width, F32)], [],
                 tm=tm, rs=rs)[0]


def _layernorm(v, w, b):
    mu = jnp.mean(v, axis=-1, keepdims=True)
    var = jnp.mean(jnp.square(v - mu), axis=-1, keepdims=True)
    return (v - mu) * lax.rsqrt(var + EPS) * w + b


GELU_C = 0.7978845608028654
GELU_A = 0.044715


def _gelu_and_slope(x):
    x2 = x * x
    t = jnp.tanh(GELU_C * x * (1.0 + GELU_A * x2))
    half = 0.5 * (1.0 + t)
    slope = half + 0.5 * x * (1.0 - t * t) * (GELU_C * (1.0 + 3.0 * GELU_A * x2))
    return x * half, slope


def _layernorm_and_back(v, w, b):
    mu = jnp.mean(v, axis=-1, keepdims=True)
    cen = v - mu
    rstd = lax.rsqrt(jnp.mean(cen * cen, axis=-1, keepdims=True) + EPS)
    vhat = cen * rstd

    def back(dout):
        dhat = dout * w
        dv = rstd * (dhat - jnp.mean(dhat, axis=-1, keepdims=True)
                     - vhat * jnp.mean(dhat * vhat, axis=-1, keepdims=True))
        return dv, _colsum(dout * vhat), _colsum(dout)

    return vhat * w + b, back


def _gmlp_mask():
    t = lax.broadcasted_iota(jnp.int32, (GMLP_BLOCK, GMLP_BLOCK), 0) // CHUNK
    s = lax.broadcasted_iota(jnp.int32, (GMLP_BLOCK, GMLP_BLOCK), 1) // CHUNK
    return s <= t


GMLP_TM = 512


def _gmlp_fwd(name, za, ln_w, ln_b, ws, bs_col):
    nrow = za.shape[0]
    tm = GMLP_TM
    width = GMLP_GROUPS * GMLP_BLOCK

    def body(za_ref, lnw_ref, lnb_ref, ws_ref, bs_ref, o_ref, wm_ref):
        mask = _gmlp_mask()
        for g in range(GMLP_GROUPS):
            wm_ref[g] = jnp.where(mask, ws_ref[g], 0.0).astype(BF16)

        def block(n, carry):
            rows = pl.ds(pl.multiple_of(n * GMLP_BLOCK, GMLP_BLOCK), GMLP_BLOCK)
            for g in range(GMLP_GROUPS):
                cols = slice(g * GMLP_BLOCK, (g + 1) * GMLP_BLOCK)
                vcols = slice(width + g * GMLP_BLOCK, width + (g + 1) * GMLP_BLOCK)
                u = jax.nn.gelu(za_ref[rows, cols].astype(F32))
                v = jax.nn.gelu(za_ref[rows, vcols].astype(F32))
                vn = _layernorm(v, lnw_ref[g:g + 1, :], lnb_ref[g:g + 1, :])
                sv = _dot(wm_ref[g], vn) + bs_ref[g]
                o_ref[rows, cols] = (u * sv).astype(o_ref.dtype)
            return carry

        lax.fori_loop(0, tm // GMLP_BLOCK, block, 0)

    small = lambda a: pl.BlockSpec(a.shape, lambda i, nd=a.ndim: (0,) * nd)
    return pl.pallas_call(
        body, name=name, grid=(nrow // tm,),
        in_specs=[pl.BlockSpec((tm, 2 * width), lambda i: (i, 0)), small(ln_w), small(ln_b), small(ws), small(bs_col)],
        out_specs=pl.BlockSpec((tm, width), lambda i: (i, 0)),
        out_shape=jax.ShapeDtypeStruct((nrow, width), BF16),
        scratch_shapes=[pltpu.VMEM((GMLP_GROUPS, GMLP_BLOCK, GMLP_BLOCK), BF16)],
        compiler_params=_params(("arbitrary",)),
    )(za, ln_w, ln_b, ws, bs_col)


def _gmlp_bwd(name, za, dout, ln_w, ln_b, ws, bs_col):
    nrow = za.shape[0]
    tm = GMLP_TM
    width = GMLP_GROUPS * GMLP_BLOCK

    def body(za_ref, do_ref, lnw_ref, lnb_ref, ws_ref, bs_ref, dza_ref, dlnw_ref, dlnb_ref, dws_ref, dbs_ref, wm_ref):
        mask = _gmlp_mask()
        for g in range(GMLP_GROUPS):
            wm_ref[g] = jnp.where(mask, ws_ref[g], 0.0).astype(BF16)

        @pl.when(pl.program_id(0) == 0)
        def _():
            dlnw_ref[...] = jnp.zeros_like(dlnw_ref)
            dlnb_ref[...] = jnp.zeros_like(dlnb_ref)
            dws_ref[...] = jnp.zeros_like(dws_ref)
            dbs_ref[...] = jnp.zeros_like(dbs_ref)

        def block(n, carry):
            rows = pl.ds(pl.multiple_of(n * GMLP_BLOCK, GMLP_BLOCK), GMLP_BLOCK)
            for g in range(GMLP_GROUPS):
                cols = slice(g * GMLP_BLOCK, (g + 1) * GMLP_BLOCK)
                vcols = slice(width + g * GMLP_BLOCK, width + (g + 1) * GMLP_BLOCK)
                u, slope_u = _gelu_and_slope(za_ref[rows, cols].astype(F32))
                v, slope_v = _gelu_and_slope(za_ref[rows, vcols].astype(F32))
                vn, ln_back = _layernorm_and_back(v, lnw_ref[g:g + 1, :], lnb_ref[g:g + 1, :])
                sv = _dot(wm_ref[g], vn) + bs_ref[g]
                d_o = do_ref[rows, cols].astype(F32)
                dsv = d_o * u
                d_wm = _dot(dsv, vn, 1, 1)
                dvn = _dot(wm_ref[g], dsv, 0, 0)
                dv, dlnw, dlnb = ln_back(dvn)
                dza_ref[rows, cols] = (d_o * sv * slope_u).astype(dza_ref.dtype)
                dza_ref[rows, vcols] = (dv * slope_v).astype(dza_ref.dtype)
                dlnw_ref[g:g + 1, :] += dlnw
                dlnb_ref[g:g + 1, :] += dlnb
                dws_ref[g] += jnp.where(mask, d_wm, 0.0)
                dbs_ref[g] += jnp.sum(dsv, axis=1, keepdims=True)
            return carry

        lax.fori_loop(0, tm // GMLP_BLOCK, block, 0)

    small = lambda a: pl.BlockSpec(a.shape, lambda i, nd=a.ndim: (0,) * nd)
    return pl.pallas_call(
        body, name=name, grid=(nrow // tm,),
        in_specs=[pl.BlockSpec((tm, 2 * width), lambda i: (i, 0)), pl.BlockSpec((tm, width), lambda i: (i, 0)),
                  small(ln_w), small(ln_b), small(ws), small(bs_col)],
        out_specs=[pl.BlockSpec((tm, 2 * width), lambda i: (i, 0)), small(ln_w), small(ln_b), small(ws), small(bs_col)],
        out_shape=[jax.ShapeDtypeStruct((nrow, 2 * width), BF16), jax.ShapeDtypeStruct(ln_w.shape, F32),
                   jax.ShapeDtypeStruct(ln_b.shape, F32), jax.ShapeDtypeStruct(ws.shape, F32),
                   jax.ShapeDtypeStruct(bs_col.shape, F32)],
        scratch_shapes=[pltpu.VMEM((GMLP_GROUPS, GMLP_BLOCK, GMLP_BLOCK), BF16)],
        compiler_params=_params(("arbitrary",)),
    )(za, dout, ln_w, ln_b, ws, bs_col)


CONV_TM = 256
CONV_RS = 32
HALO = 2 * SUBLANES


def _tap_rows(w_ref):
    return [w_ref[k:k + 1, :] for k in range(w_ref.shape[0])]


def _halo_specs(nrow, tm, tc):
    per = tm // HALO
    last = nrow // HALO - 1
    main = pl.BlockSpec((tm, tc), lambda j, i: (i, j))
    before = pl.BlockSpec((HALO, tc), lambda j, i: (jnp.maximum(i * per - 1, 0), j))
    after = pl.BlockSpec((HALO, tc), lambda j, i: (jnp.minimum((i + 1) * per, last), j))
    return main, before, after


def _col_spec(rows, tc):
    return pl.BlockSpec((rows, tc), lambda j, i: (0, j))


def _conv_fwd(name, x, w, b, *, tc):
    nrow, ncol = x.shape
    taps = w.shape[0]
    tm, rs = CONV_TM, CONV_RS
    main, before, _ = _halo_specs(nrow, tm, tc)

    def body(x_ref, xb_ref, w_ref, b_ref, o_ref, xw_ref):
        first = pl.program_id(1) == 0
        wv, bv = _tap_rows(w_ref), b_ref[...]
        xw_ref[0:HALO, :] = jnp.where(first, 0.0, xb_ref[...].astype(F32))
        for r in range(tm // rs):
            xw_ref[HALO + r * rs:HALO + (r + 1) * rs, :] = x_ref[r * rs:(r + 1) * rs, :].astype(F32)
        for r in range(tm // rs):
            base = HALO + r * rs
            out = bv + wv[taps - 1] * xw_ref[base:base + rs, :]
            for k in range(taps - 1):
                back = taps - 1 - k
                out = out + wv[k] * xw_ref[base - back:base - back + rs, :]
            o_ref[r * rs:(r + 1) * rs, :] = out.astype(o_ref.dtype)

    return pl.pallas_call(
        body, name=name, grid=(ncol // tc, nrow // tm),
        in_specs=[main, before, _col_spec(taps, tc), _col_spec(1, tc)],
        out_specs=main, out_shape=jax.ShapeDtypeStruct((nrow, ncol), BF16),
        scratch_shapes=[pltpu.VMEM((HALO + tm, tc), F32)],
        compiler_params=_params(("parallel", "arbitrary")),
    )(x, x, w, b)


def _conv_bwd(name, dpre, x, w, *, tc):
    nrow, ncol = x.shape
    taps = w.shape[0]
    tm, rs = CONV_TM, CONV_RS
    nsub = tm // rs
    main, before, after = _halo_specs(nrow, tm, tc)

    def fold(v):
        total = v[0:SUBLANES]
        for q in range(1, rs // SUBLANES):
            total = total + v[q * SUBLANES:(q + 1) * SUBLANES]
        return total

    def body(d_ref, da_ref, x_ref, xb_ref, w_ref, dx_ref, dw_ref, db_ref, dwin_ref, xwin_ref):
        i = pl.program_id(1)
        first, last = i == 0, i == pl.num_programs(1) - 1
        wv = _tap_rows(w_ref)

        @pl.when(first)
        def _():
            dw_ref[...] = jnp.zeros_like(dw_ref)
            db_ref[...] = jnp.zeros_like(db_ref)

        xwin_ref[0:HALO, :] = jnp.where(first, 0.0, xb_ref[...].astype(F32))
        dwin_ref[tm:, :] = jnp.where(last, 0.0, da_ref[...].astype(F32))
        for r in range(nsub):
            dwin_ref[r * rs:(r + 1) * rs, :] = d_ref[r * rs:(r + 1) * rs, :].astype(F32)
            xwin_ref[HALO + r * rs:HALO + (r + 1) * rs, :] = x_ref[r * rs:(r + 1) * rs, :].astype(F32)
        dw = [jnp.zeros((SUBLANES, tc), F32)] * taps
        db = jnp.zeros((SUBLANES, tc), F32)
        for r in range(nsub):
            cur = dwin_ref[r * rs:(r + 1) * rs, :]
            dx = wv[taps - 1] * cur
            for k in range(taps - 1):
                ahead = taps - 1 - k
                dx = dx + wv[k] * dwin_ref[r * rs + ahead:(r + 1) * rs + ahead, :]
            dx_ref[r * rs:(r + 1) * rs, :] = dx.astype(dx_ref.dtype)
            for k in range(taps):
                back = taps - 1 - k
                dw[k] = dw[k] + fold(cur * xwin_ref[HALO + r * rs - back:HALO + (r + 1) * rs - back, :])
            db = db + fold(cur)
        for k in range(taps):
            dw_ref[k:k + 1, :] += _colsum(dw[k])
        db_ref[...] += _colsum(db)

    return pl.pallas_call(
        body, name=name, grid=(ncol // tc, nrow // tm),
        in_specs=[main, after, main, before, _col_spec(taps, tc)],
        out_specs=[main, _col_spec(taps, tc), _col_spec(1, tc)],
        out_shape=[jax.ShapeDtypeStruct((nrow, ncol), BF16), jax.ShapeDtypeStruct((taps, ncol), F32),
                   jax.ShapeDtypeStruct((1, ncol), F32)],
        scratch_shapes=[pltpu.VMEM((tm + HALO, tc), F32), pltpu.VMEM((HALO + tm, tc), F32)],
        compiler_params=_params(("parallel", "arbitrary")),
    )(dpre, dpre, x, x, w)


def _glu(gate, val):
    return jax.nn.silu(gate) * val


def _ffn_act_fwd(name, pg, pv, wg, wv, bg, bv, *, tc):
    nrow, ncol = pg.shape
    taps = wg.shape[0]
    tm, rs = CONV_TM, CONV_RS
    main, before, _ = _halo_specs(nrow, tm, tc)

    def body(pg_ref, pgb_ref, pv_ref, pvb_ref, wg_ref, wv_ref, bg_ref, bv_ref, g_ref, v_ref, a_ref, gwin_ref, vwin_ref):
        first = pl.program_id(1) == 0
        taps_g, taps_v, bgv, bvv = _tap_rows(wg_ref), _tap_rows(wv_ref), bg_ref[...], bv_ref[...]
        gwin_ref[0:HALO, :] = jnp.where(first, 0.0, pgb_ref[...].astype(F32))
        vwin_ref[0:HALO, :] = jnp.where(first, 0.0, pvb_ref[...].astype(F32))
        for r in range(tm // rs):
            gwin_ref[HALO + r * rs:HALO + (r + 1) * rs, :] = pg_ref[r * rs:(r + 1) * rs, :].astype(F32)
            vwin_ref[HALO + r * rs:HALO + (r + 1) * rs, :] = pv_ref[r * rs:(r + 1) * rs, :].astype(F32)

        def conv(win_ref, tap_rows, bias, r):
            base = HALO + r * rs
            out = bias + tap_rows[taps - 1] * win_ref[base:base + rs, :]
            for k in range(taps - 1):
                back = taps - 1 - k
                out = out + tap_rows[k] * win_ref[base - back:base - back + rs, :]
            return out

        for r in range(tm // rs):
            sl = slice(r * rs, (r + 1) * rs)
            gate, val = conv(gwin_ref, taps_g, bgv, r), conv(vwin_ref, taps_v, bvv, r)
            g_ref[sl, :] = gate.astype(g_ref.dtype)
            v_ref[sl, :] = val.astype(v_ref.dtype)
            a_ref[sl, :] = _glu(gate, val).astype(a_ref.dtype)

    return pl.pallas_call(
        body, name=name, grid=(ncol // tc, nrow // tm),
        in_specs=[main, before, main, before, _col_spec(taps, tc), _col_spec(taps, tc), _col_spec(1, tc), _col_spec(1, tc)],
        out_specs=[main, main, main],
        out_shape=[jax.ShapeDtypeStruct((nrow, ncol), BF16)] * 3,
        scratch_shapes=[pltpu.VMEM((HALO + tm, tc), F32), pltpu.VMEM((HALO + tm, tc), F32)],
        compiler_params=_params(("parallel", "arbitrary")),
    )(pg, pg, pv, pv, wg, wv, bg, bv)


def _ffn_act_bwd(name, dact, gate, val):
    def fn(dv, gv, vv):
        _, vjp = jax.vjp(_glu, gv, vv)
        dg, dval = vjp(dv)
        return (dg, dval), ()
    width = dact.shape[1]
    return _rows(name, fn, [(dact, width, 0), (gate, width, 0), (val, width, 0)], [],
                 [(width, BF16), (width, BF16)], [], tm=256, rs=2 * SUBLANES)


SSD_TM = 256
SSD_CHUNKS = SSD_TM // CHUNK
X_OFF, B_OFF, C_OFF = 0, SSM_INNER, SSM_INNER + SSM_GROUPS * SSM_STATE
HP = SSM_HPG * SSM_HEAD_DIM


def _causal_tiled():
    row = lax.broadcasted_iota(jnp.int32, (CHUNK, HP), 0)
    src = lax.broadcasted_iota(jnp.int32, (CHUNK, HP), 1) & (CHUNK - 1)
    return src <= row


def _split2(v):
    hi = v.astype(BF16)
    return hi, (v - hi.astype(F32)).astype(BF16)


def _dot_exact(a, ind):
    hi, lo = (lax.dot_general(p, ind, (((1,), (0,)), ((), ())), preferred_element_type=F32) for p in _split2(a))
    return hi + lo


def _head_indicator():
    head = lax.broadcasted_iota(jnp.int32, (SSM_HEADS, SSM_INNER), 0)
    chan = lax.broadcasted_iota(jnp.int32, (SSM_HEADS, SSM_INNER), 1)
    return (chan // SSM_HEAD_DIM == head).astype(BF16)


def _chunk_decays(ci, dt_ref, ac_ref, ind, ax_ref, dtx_ref, eax_ref, eex_ref, tail_ref):
    rows = pl.ds(pl.multiple_of(ci * CHUNK, CHUNK), CHUNK)
    ax_ref[...] = _dot_exact(ac_ref[rows, :], ind)
    dtx_ref[...] = _dot_exact(dt_ref[rows, :], ind)
    eax_ref[...] = jnp.exp(ax_ref[...])
    eex_ref[...] = jnp.exp(ax_ref[CHUNK - 1:CHUNK, :] - ax_ref[...])
    tail = pl.ds(pl.multiple_of(ci * CHUNK + CHUNK - SUBLANES, SUBLANES), SUBLANES)
    tail_ref[...] = jnp.exp(ac_ref[tail, :])


def _group_decay(ci, g, ax_ref, af_ref, xbc_ref, causal):
    gcols = slice(g * HP, (g + 1) * HP)
    bm = xbc_ref[:, B_OFF + g * SSM_STATE:B_OFF + (g + 1) * SSM_STATE]
    cm = xbc_ref[:, C_OFF + g * SSM_STATE:C_OFF + (g + 1) * SSM_STATE]
    cb_tiled = _dot(cm, jnp.concatenate([bm] * SSM_HPG, axis=0), 1, 1)
    seg = ax_ref[:, gcols] - af_ref[ci, :, gcols]
    decay = jnp.where(causal, jnp.exp(jnp.where(causal, seg, 0.0)), 0.0)
    return bm, cm, cb_tiled * decay, decay


def _ssd_fwd(name, pre, dt, a_cum, a_flat, d_x, ind, shards):
    nrow = pre.shape[0]
    tm = SSD_TM
    nstep = nrow // tm
    ng = len(shards)

    def body(pre_ref, dt_ref, ac_ref, af_ref, dx_ref, ind_ref, *rest):
        shard_refs, (y_ref, st_ref), stack_refs = rest[:ng], rest[ng:ng + 2], rest[ng + 2:2 * ng + 2]
        (h_ref, xbc_ref, ax_ref, dtx_ref, eax_ref, eex_ref, m_ref, xd_ref, yd_ref, tail_ref,
         send_sems, recv_sems) = rest[2 * ng + 2:]
        step = pl.program_id(0)
        start, forward, finish = _gather_phases([s.shape[0] for s in shards], ng, shard_refs, stack_refs,
                                                send_sems, recv_sems)

        @pl.when(step == 0)
        def _():
            h_ref[...] = jnp.zeros_like(h_ref)
            start()

        @pl.when(step == nstep // 2)
        def _():
            forward()

        causal = _causal_tiled()
        ind = ind_ref[...]

        def chunk(ci, carry):
            rows = pl.ds(pl.multiple_of(ci * CHUNK, CHUNK), CHUNK)
            xbc_ref[...] = jax.nn.silu(pre_ref[rows, :].astype(F32))
            _chunk_decays(ci, dt_ref, ac_ref, ind, ax_ref, dtx_ref, eax_ref, eex_ref, tail_ref)
            st_ref[ci] = h_ref[...].astype(st_ref.dtype)
            for g in range(SSM_GROUPS):
                gcols = slice(g * HP, (g + 1) * HP)
                bm, cm, m_all, _ = _group_decay(ci, g, ax_ref, af_ref, xbc_ref, causal)
                m_ref[...] = m_all
                x_g = xbc_ref[:, gcols]
                xd = x_g * dtx_ref[:, gcols]
                xd_ref[...] = xd
                h_g = h_ref[gcols, :]
                for hh in range(SSM_HPG):
                    lc = slice(hh * SSM_HEAD_DIM, (hh + 1) * SSM_HEAD_DIM)
                    yd_ref[:, lc] = _dot(m_ref[:, lc], xd_ref[:, lc])
                y_ref[rows, gcols] = (yd_ref[...] + _dot(cm, h_g, 1, 1) * eax_ref[:, gcols]
                                      + dx_ref[:, gcols] * x_g).astype(y_ref.dtype)
                new = _dot(xd * eex_ref[:, gcols], bm, 0, 0)
                for hh in range(SSM_HPG):
                    h = g * SSM_HPG + hh
                    hrows = slice(h * SSM_HEAD_DIM, (h + 1) * SSM_HEAD_DIM)
                    lrows = slice(hh * SSM_HEAD_DIM, (hh + 1) * SSM_HEAD_DIM)
                    h_ref[hrows, :] = tail_ref[SUBLANES - 1:SUBLANES, h:h + 1] * h_ref[hrows, :] + new[lrows, :]
            return carry

        lax.fori_loop(0, SSD_CHUNKS, chunk, 0)

        @pl.when(step == nstep - 1)
        def _():
            finish()

    nchunk = nrow // CHUNK
    whole = lambda a: pl.BlockSpec(a.shape, lambda i, nd=a.ndim: (0,) * nd)
    hbm = pl.BlockSpec(memory_space=pl.ANY)
    wide = lambda: pltpu.VMEM((CHUNK, SSM_INNER), F32)
    group = lambda: pltpu.VMEM((CHUNK, HP), F32)
    res = pl.pallas_call(
        body, name=name, grid=(nstep,),
        in_specs=[pl.BlockSpec((tm, SSM_XBC), lambda i: (i, 0)), pl.BlockSpec((tm, SSM_HEADS), lambda i: (i, 0)),
                  pl.BlockSpec((tm, SSM_HEADS), lambda i: (i, 0)),
                  pl.BlockSpec((SSD_CHUNKS, 1, SSM_INNER), lambda i: (i, 0, 0)), whole(d_x), whole(ind)] + [hbm] * ng,
        out_specs=[pl.BlockSpec((tm, SSM_INNER), lambda i: (i, 0)),
                   pl.BlockSpec((SSD_CHUNKS, SSM_INNER, SSM_STATE), lambda i: (i, 0, 0))] + [hbm] * ng,
        out_shape=[jax.ShapeDtypeStruct((nrow, SSM_INNER), BF16),
                   jax.ShapeDtypeStruct((nchunk, SSM_INNER, SSM_STATE), BF16)]
        + [jax.ShapeDtypeStruct((N_CHIPS,) + s.shape, s.dtype) for s in shards],
        scratch_shapes=[pltpu.VMEM((SSM_INNER, SSM_STATE), F32), pltpu.VMEM((CHUNK, SSM_XBC), F32),
                        wide(), wide(), wide(), wide(), group(), group(), group(),
                        pltpu.VMEM((SUBLANES, SSM_HEADS), F32)] + _exchange_scratch(ng, GATHER_SEMS),
        compiler_params=_params(("arbitrary",)),
    )(pre, dt, a_cum, a_flat, d_x, ind, *shards)
    return res[0], res[1], res[2:]


def _ssd_bwd(name, pre, dt, a_cum, a_flat, d_x, ind, ind_t, states, dy, pairs):
    nrow = pre.shape[0]
    tm = SSD_TM
    ntile = nrow // tm
    npair = len(pairs)

    def body(pre_ref, dt_ref, ac_ref, af_ref, dx_ref, ind_ref, indt_ref, st_ref, dy_ref, *rest):
        pair_refs = rest[:npair]
        dpre_ref, ddt_ref, da_ref, daf_ref, dd_ref = rest[npair:npair + 5]
        recv_refs = rest[npair + 5:2 * npair + 5]
        (dh_ref, xbc_ref, dxbc_ref, ax_ref, dtx_ref, eax_ref, eex_ref, red_ref,
         m_ref, l_ref, xd_ref, dm_ref, dxd_ref, fold_ref, hd_ref, tail_ref, send_sems, recv_sems) = rest[2 * npair + 5:]
        start, finish = _scatter_phases(pair_refs, recv_refs, send_sems, recv_sems)

        @pl.when(pl.program_id(0) == 0)
        def _():
            dh_ref[...] = jnp.zeros_like(dh_ref)
            dd_ref[...] = jnp.zeros_like(dd_ref)
            start()

        causal = _causal_tiled()
        ind, ind_t = ind_ref[...], indt_ref[...]
        is_last_row = lax.broadcasted_iota(jnp.int32, (CHUNK, 1), 0) == CHUNK - 1
        ones = jnp.ones((CHUNK, SSM_STATE), BF16)

        def chunk(k, ddx):
            ci = SSD_CHUNKS - 1 - k
            rows = pl.ds(pl.multiple_of(ci * CHUNK, CHUNK), CHUNK)
            pre_v = pre_ref[rows, :].astype(F32)
            xbc_ref[...] = jax.nn.silu(pre_v)
            _chunk_decays(ci, dt_ref, ac_ref, ind, ax_ref, dtx_ref, eax_ref, eex_ref, tail_ref)
            ddx_parts = []
            for g in range(SSM_GROUPS):
                gcols = slice(g * HP, (g + 1) * HP)
                bcols = slice(B_OFF + g * SSM_STATE, B_OFF + (g + 1) * SSM_STATE)
                ccols = slice(C_OFF + g * SSM_STATE, C_OFF + (g + 1) * SSM_STATE)
                bm, cm, m_all, decay = _group_decay(ci, g, ax_ref, af_ref, xbc_ref, causal)
                m_ref[...] = m_all
                l_ref[...] = decay
                x_g = xbc_ref[:, gcols]
                xd = x_g * dtx_ref[:, gcols]
                xd_ref[...] = xd
                h_g = st_ref[ci, gcols, :]
                dh_g = dh_ref[gcols, :]
                dy_g = dy_ref[rows, gcols]
                for hh in range(SSM_HPG):
                    h = g * SSM_HPG + hh
                    hcols = slice(h * SSM_HEAD_DIM, (h + 1) * SSM_HEAD_DIM)
                    lc = slice(hh * SSM_HEAD_DIM, (hh + 1) * SSM_HEAD_DIM)
                    dy_h = dy_ref[rows, hcols]
                    dm_ref[:, lc] = _dot(dy_h, xd_ref[:, lc], 1, 1)
                    dxd_ref[:, lc] = _dot(m_ref[:, lc], dy_h, 0, 0)
                ebdh = eex_ref[:, gcols] * _dot(bm, dh_g, 1, 1)
                dxd = dxd_ref[...] + ebdh
                dm = dm_ref[...]
                t = dm * l_ref[...]
                t128 = (t[:, 0:LANES] + t[:, LANES:2 * LANES]) + (t[:, 2 * LANES:3 * LANES] + t[:, 3 * LANES:])
                fold_ref[...] = t128 + pltpu.roll(t128, CHUNK, axis=1)
                dw_sum = fold_ref[:, 0:CHUNK]
                q = dm * m_ref[...]
                dyea = dy_g * eax_ref[:, gcols]
                red_ref[0:CHUNK, gcols] = q + dyea * _dot(cm, h_g, 1, 1)
                red_ref[CHUNK:2 * CHUNK, gcols] = xd * ebdh
                red_ref[2 * CHUNK:3 * CHUNK, gcols] = dxd * x_g
                daf_ref[ci, :, gcols] = -jnp.sum(q, axis=0, keepdims=True)
                ddx_parts.append(jnp.sum(dy_g * x_g, axis=0, keepdims=True))
                dxbc_ref[:, gcols] = dxd * dtx_ref[:, gcols] + dx_ref[:, gcols] * dy_g
                dxbc_ref[:, ccols] = _dot(dw_sum, bm) + _dot(dyea, h_g)
                dxbc_ref[:, bcols] = _dot(dw_sum, cm, 0, 0) + _dot(xd * eex_ref[:, gcols], dh_g)
                dh_new = _dot(dyea, cm, 0, 0)
                for hh in range(SSM_HPG):
                    h = g * SSM_HPG + hh
                    hrows = slice(h * SSM_HEAD_DIM, (h + 1) * SSM_HEAD_DIM)
                    lrows = slice(hh * SSM_HEAD_DIM, (hh + 1) * SSM_HEAD_DIM)
                    hd_ref[h:h + 1, :] = jnp.sum(st_ref[ci, hrows, :] * dh_ref[hrows, :], axis=0, keepdims=True)
                    dh_ref[hrows, :] = tail_ref[SUBLANES - 1:SUBLANES, h:h + 1] * dh_ref[hrows, :] + dh_new[lrows, :]
            sums = _dot_exact(red_ref[...], ind_t)
            ra, ts = sums[:CHUNK], sums[CHUNK:2 * CHUNK]
            hdh = sum(lax.dot_general(ones, p, (((1,), (1,)), ((), ())), preferred_element_type=F32)
                      for p in _split2(hd_ref[...]))
            da_last = jnp.sum(ts, axis=0, keepdims=True) + tail_ref[SUBLANES - 1:SUBLANES, :] * hdh
            da_ref[rows, :] = ra - ts + jnp.where(is_last_row, da_last, 0.0)
            ddt_ref[rows, :] = sums[2 * CHUNK:]
            sig = jax.nn.sigmoid(pre_v)
            dpre_ref[rows, :] = (dxbc_ref[...] * (sig * (1.0 + pre_v * (1.0 - sig)))).astype(dpre_ref.dtype)
            return ddx + jnp.concatenate(ddx_parts, axis=1)

        ddx = lax.fori_loop(0, SSD_CHUNKS, chunk, jnp.zeros((1, SSM_INNER), F32))
        dd_ref[...] += _dot_exact(jnp.broadcast_to(ddx, (SUBLANES, SSM_INNER)), ind_t)

        @pl.when(pl.program_id(0) == ntile - 1)
        def _():
            finish()

    rev = lambda i: ntile - 1 - i
    whole = lambda a: pl.BlockSpec(a.shape, lambda i, nd=a.ndim: (0,) * nd)
    hbm = pl.BlockSpec(memory_space=pl.ANY)
    wide = lambda: pltpu.VMEM((CHUNK, SSM_INNER), F32)
    group = lambda: pltpu.VMEM((CHUNK, HP), F32)
    res = pl.pallas_call(
        body, name=name, grid=(ntile,),
        in_specs=[pl.BlockSpec((tm, SSM_XBC), lambda i: (rev(i), 0)), pl.BlockSpec((tm, SSM_HEADS), lambda i: (rev(i), 0)),
                  pl.BlockSpec((tm, SSM_HEADS), lambda i: (rev(i), 0)),
                  pl.BlockSpec((SSD_CHUNKS, 1, SSM_INNER), lambda i: (rev(i), 0, 0)),
                  whole(d_x), whole(ind), whole(ind_t),
                  pl.BlockSpec((SSD_CHUNKS, SSM_INNER, SSM_STATE), lambda i: (rev(i), 0, 0)),
                  pl.BlockSpec((tm, SSM_INNER), lambda i: (rev(i), 0))] + [hbm] * npair,
        out_specs=[pl.BlockSpec((tm, SSM_XBC), lambda i: (rev(i), 0)), pl.BlockSpec((tm, SSM_HEADS), lambda i: (rev(i), 0)),
                   pl.BlockSpec((tm, SSM_HEADS), lambda i: (rev(i), 0)),
                   pl.BlockSpec((SSD_CHUNKS, 1, SSM_INNER), lambda i: (rev(i), 0, 0)),
                   pl.BlockSpec((SUBLANES, SSM_HEADS), lambda i: (0, 0))] + [hbm] * npair,
        out_shape=[jax.ShapeDtypeStruct((nrow, SSM_XBC), BF16), jax.ShapeDtypeStruct((nrow, SSM_HEADS), F32),
                   jax.ShapeDtypeStruct((nrow, SSM_HEADS), F32), jax.ShapeDtypeStruct((nrow // CHUNK, 1, SSM_INNER), F32),
                   jax.ShapeDtypeStruct((SUBLANES, SSM_HEADS), F32)]
        + [jax.ShapeDtypeStruct(p.shape, p.dtype) for p in pairs],
        scratch_shapes=[pltpu.VMEM((SSM_INNER, SSM_STATE), F32), pltpu.VMEM((CHUNK, SSM_XBC), F32),
                        pltpu.VMEM((CHUNK, SSM_XBC), F32), wide(), wide(), wide(), wide(),
                        pltpu.VMEM((3 * CHUNK, SSM_INNER), F32),
                        group(), group(), group(), group(), group(), pltpu.VMEM((CHUNK, LANES), F32),
                        pltpu.VMEM((SSM_HEADS, SSM_STATE), F32), pltpu.VMEM((SUBLANES, SSM_HEADS), F32)]
        + _exchange_scratch(npair, N_CHIPS - 1),
        compiler_params=_params(("arbitrary",)),
    )(pre, dt, a_cum, a_flat, d_x, ind, ind_t, states, dy, *pairs)
    return res[:5], res[5:]


LATE = ["w_proj_a", "w_proj_b", "w_out", "ffn_w_up", "ffn_w_down"]
HALF_TILES = {"w_in": 128, "w_proj_a": 128, "w_proj_b": 256, "w_out": 128, "ffn_w_up": 128, "ffn_w_down": 176}


def _late_weights(stacks, shards):
    pa, pb, out, up, down = [_own_slot(stack, own) for stack, own in zip(stacks, shards)]
    return {"w_proj_a": pa.reshape(-1, D_MODEL), "w_proj_b": pb.reshape(-1, D_MODEL), "w_out": out.reshape(-1, D_MODEL),
            "w_up_g": _columns_from_chips(up[:2]), "w_up_v": _columns_from_chips(up[2:]),
            "w_down": down.reshape(-1, D_MODEL)}


def _pair_reduce(tag, names, stacks):
    core = lax.axis_index("c")
    own_half = [_row_half(s, core, 1) for s in stacks]
    other_half = _swap_cores("pair_grads_" + tag, [_row_half(s, 1 - core, 1) for s in stacks])
    return [_pair_sum("pair_" + n, a, b, tm=HALF_TILES[n]) for n, a, b in zip(names, own_half, other_half)]


def _local_step(x, target, w, late_shards):
    w = dict(w)
    g = {}
    bs_col = w["gmlp_bs"].reshape(GMLP_GROUPS, GMLP_BLOCK, 1)
    b0, b1 = w["gate_bias"][0:1], w["gate_bias"][1:2]

    xn = _rms_fwd("mix_norm", x, w["mix_norm_w"])
    big = dict(bm=1024, bn=1024, bk=1024)
    act16 = dict(out_dtype=BF16, **big)
    gates = _mm("in_gates", xn, w["w_g"], **act16)
    za = _mm("in_gmlp", xn, w["w_za"], **act16)
    z = _mm("in_z", xn, w["w_z"], **act16)
    xbc = _mm("in_xbc", xn, w["w_xbc"], **act16)
    dt_raw = _mm("in_dt", xn, w["w_dt"], bm=1024, bn=SSM_HEADS, bk=1024)

    pre = _conv_fwd("ssm_conv_fwd", xbc, w["ssm_conv_w"], w["ssm_conv_b"], tc=1024)
    dt, a_cum = _dt_prep("dt_prep", dt_raw, w["ssm_dt_bias"], w["ssm_a_log"])
    a_flat = jnp.transpose(a_cum.reshape(-1, CHUNK, SSM_HEADS), (0, 2, 1)).reshape(-1, 1, SSM_INNER)
    d_x = jnp.repeat(w["ssm_d"], SSM_HEAD_DIM, axis=1)
    ind = _head_indicator()
    y_ssd, states, late_stacks = _ssd_fwd("ssd_fwd", pre, dt, a_cum, a_flat, d_x, ind, late_shards)
    w.update(_late_weights(late_stacks, late_shards))
    yb_pre = _gate_norm_fwd("gate_norm_fwd", y_ssd, z, w["ssm_norm_w"])
    y_b = _mm("proj_b", yb_pre, w["w_proj_b"], bm=1024, bn=1024, bk=SSM_INNER, out_dtype=BF16)

    ya_pre = _gmlp_fwd("gmlp_fwd", za, w["gmlp_ln_w"], w["gmlp_ln_b"], w["gmlp_ws"], bs_col)
    y_a = _mm("proj_a", ya_pre, w["w_proj_a"], **act16)

    merged = _merge_fwd("merge_fwd", gates, y_a, y_b, b0, b1)
    h1 = _mm("out_proj", merged, w["w_out"], res=x, **big)

    hn = _rms_fwd("ffn_norm", h1, w["ffn_norm_w"])
    half = dict(bm=1024, bn=D_FF // 2, bk=1024, out_dtype=BF16)
    pg = _mm("ffn_up_gate", hn, w["w_up_g"], **half)
    pv = _mm("ffn_up_val", hn, w["w_up_v"], **half)
    cw, cb = w["ffn_conv_w"], w["ffn_conv_b"]
    gate, val, act = _ffn_act_fwd("ffn_act_fwd", pg, pv, cw[:, :D_FF], cw[:, D_FF:], cb[:, :D_FF], cb[:, D_FF:],
                                  tc=D_FF // 2)
    h2 = _mm("ffn_down", act, w["w_down"], res=h1, bm=1024, bn=1024, bk=D_FF // 2)

    dh2, loss_part, g["final_norm_w"] = _final_loss("final_loss", h2, target, w["final_norm_w"].reshape(1, D_MODEL))

    dact = _mm("d_act", dh2, w["w_down"], tb=True, **half)
    wgrad = dict(ta=True, bk=min(2048, x.shape[0]), out_dtype=BF16)
    g["w_down"] = _mm("dw_down", act, dh2, bm=D_FF // 2, bn=1024, **wgrad)
    dgate, dval = _ffn_act_bwd("ffn_act_bwd", dact, gate, val)
    dpg, dcwg, dcbg = _conv_bwd("ffn_conv_bwd_gate", dgate, pg, cw[:, :D_FF], tc=D_FF // 2)
    dpv, dcwv, dcbv = _conv_bwd("ffn_conv_bwd_val", dval, pv, cw[:, D_FF:], tc=D_FF // 2)
    g["ffn_conv_w"] = jnp.concatenate([dcwg, dcwv], axis=1)
    g["ffn_conv_b"] = jnp.concatenate([dcbg, dcbv], axis=1)
    dhn = _mm_sum("d_hn", [(dpg, w["w_up_g"]), (dpv, w["w_up_v"])], bm=1024, bk=D_FF // 2)
    g["w_up_g"] = _mm("dw_up_gate", hn, dpg, bm=1024, bn=D_FF // 2, **wgrad)
    g["w_up_v"] = _mm("dw_up_val", hn, dpv, bm=1024, bn=D_FF // 2, **wgrad)
    dh1, g["ffn_norm_w"] = _rms_bwd("ffn_norm_bwd", h1, w["ffn_norm_w"], dhn, dh2)

    dmerged = _mm("d_merged", dh1, w["w_out"], tb=True, **act16)
    g["w_out"] = _mm("dw_out", merged, dh1, bm=1024, bn=1024, **wgrad)
    dgates, dya, dyb, db0, db1 = _merge_bwd("merge_bwd", gates, y_a, y_b, dmerged, b0, b1)
    g["gate_bias"] = jnp.concatenate([db0, db1], axis=0)

    dya_pre = _mm("d_ya_pre", dya, w["w_proj_a"], tb=True, **act16)
    g["w_proj_a"] = _mm("dw_proj_a", ya_pre, dya, bm=1024, bn=1024, **wgrad)
    dyb_pre = _mm("d_yb_pre", dyb, w["w_proj_b"], tb=True, **act16)
    g["w_proj_b"] = _mm("dw_proj_b", yb_pre, dyb, bm=1024, bn=1024, **wgrad)
    late_pairs = _pair_reduce("late", LATE, [
        g["w_proj_a"].reshape(N_CHIPS, -1, D_MODEL), g["w_proj_b"].reshape(N_CHIPS, -1, D_MODEL),
        g["w_out"].reshape(N_CHIPS, -1, D_MODEL),
        jnp.concatenate([_columns_to_chips(g["w_up_g"], 2), _columns_to_chips(g["w_up_v"], 2)], axis=0),
        g["w_down"].reshape(N_CHIPS, -1, D_MODEL)])

    dy_ssd, dz, g["ssm_norm_w"] = _gate_norm_bwd("gate_norm_bwd", y_ssd, z, dyb_pre, w["ssm_norm_w"])
    (dpre, ddt, da_tok, da_flat, dd), late_received = _ssd_bwd(
        "ssd_bwd", pre, dt, a_cum, a_flat, d_x, ind, ind.T, states, dy_ssd, late_pairs)
    g["ssm_d"] = dd[0:1]
    da_src = jnp.transpose(da_flat.reshape(-1, SSM_HEADS, CHUNK), (0, 2, 1)).reshape(-1, SSM_HEADS)
    ddt_raw, g["ssm_dt_bias"], g["ssm_a_log"] = _dt_bwd("dt_bwd", dt_raw, ddt, da_tok, da_src,
                                                         w["ssm_dt_bias"], w["ssm_a_log"])
    dxbc, g["ssm_conv_w"], g["ssm_conv_b"] = _conv_bwd("ssm_conv_bwd", dpre, xbc, w["ssm_conv_w"], tc=1024)

    dza, g["gmlp_ln_w"], g["gmlp_ln_b"], g["gmlp_ws"], dbs = _gmlp_bwd(
        "gmlp_bwd", za, dya_pre, w["gmlp_ln_w"], w["gmlp_ln_b"], w["gmlp_ws"], bs_col)
    g["gmlp_bs"] = dbs.reshape(GMLP_GROUPS, GMLP_BLOCK)

    dw_in = jnp.concatenate([
        _mm("dw_gates", xn, dgates, bm=1024, bn=1024, **wgrad), _mm("dw_gmlp", xn, dza, bm=1024, bn=1024, **wgrad),
        _mm("dw_z", xn, dz, bm=1024, bn=1024, **wgrad), _mm("dw_xbc", xn, dxbc, bm=1024, bn=1024, **wgrad),
        _mm("dw_dt", xn, ddt_raw, bm=1024, bn=SSM_HEADS, **wgrad)], axis=1)
    in_pairs = _pair_reduce("in", ["w_in"], [_columns_to_chips(dw_in)])
    dxn, in_received = _mm_sum("d_xn", [(dgates, w["w_g"]), (dza, w["w_za"]), (dz, w["w_z"]), (dxbc, w["w_xbc"]),
                                        (ddt_raw, w["w_dt"])], bm=1024, bk=1024, exchange=in_pairs)
    grad_x, g["mix_norm_w"] = _rms_bwd("mix_norm_bwd", x, w["mix_norm_w"], dxn, dh1)
    return loss_part, grad_x, g, in_pairs + list(late_pairs), list(in_received) + list(late_received)


def _position():
    return lax.axis_index("x"), lax.axis_index("y"), lax.axis_index("c")


def _own_slot(stack, own):
    chip = 2 * lax.axis_index("x") + lax.axis_index("y")
    return lax.dynamic_update_index_in_dim(stack, own, chip, axis=0)


def _scatter_phases(ins, outs, send_sems, recv_sems):
    n = len(ins)
    x, y, c = _position()
    me = 2 * x + y
    peers = [(1 - x, y), (x, 1 - y), (1 - x, 1 - y)]

    def copy(i, k, src_slot, dst_slot):
        px, py = peers[k]
        return pltpu.make_async_remote_copy(
            src_ref=ins[i].at[src_slot], dst_ref=outs[i].at[dst_slot],
            send_sem=send_sems.at[i, k], recv_sem=recv_sems.at[i, k],
            device_id=(px, py, c), device_id_type=MESH)

    def start():
        for i in range(n):
            for k, (px, py) in enumerate(peers):
                copy(i, k, 2 * px + py, me).start()

    def finish():
        for i in range(n):
            for k, (px, py) in enumerate(peers):
                copy(i, k, me, 2 * px + py).wait_recv()
        for i in range(n):
            for k, (px, py) in enumerate(peers):
                copy(i, k, 2 * px + py, me).wait_send()

    return start, finish


def _exchange_scratch(n, per_array):
    return [pltpu.SemaphoreType.DMA((n, per_array)), pltpu.SemaphoreType.DMA((n, per_array))]


def _half_rows(ref_rows, which):
    half = ref_rows // 2
    return pl.ds(pl.multiple_of(which * half, 2 * SUBLANES), half)


GATHER_SEMS = 2 * (N_CHIPS - 1)


def _gather_phases(nrows, ns, ins, outs, send_sems, recv_sems):
    n = len(ins)
    x, y, c = _position()
    me = 2 * x + y
    sibling = (x, y, 1 - c)
    chips = [(1 - x, y), (x, 1 - y), (1 - x, 1 - y)]

    def remote(i, k, src, dst, to):
        return pltpu.make_async_remote_copy(src_ref=src, dst_ref=dst, send_sem=send_sems.at[i, k],
                                            recv_sem=recv_sems.at[i, k], device_id=to, device_id_type=MESH)

    def over_ici(i, k):
        px, py = chips[k]
        rows = _half_rows(nrows[i], c) if i < ns else slice(None)
        return remote(i, k, ins[i].at[rows], outs[i].at[me, rows], (px, py, c))

    def landed(i, k, which):
        px, py = chips[k]
        return outs[i].at[2 * px + py, _half_rows(nrows[i], which)] if i < ns else outs[i].at[2 * px + py]

    def start():
        for i in range(n):
            for k in range(N_CHIPS - 1):
                over_ici(i, k).start()

    def forward():
        for i in range(n):
            for k in range(N_CHIPS - 1):
                piece = landed(i, k, c)
                remote(i, k, piece, piece, (*chips[k], c)).wait_recv()
                if i < ns:
                    remote(i, N_CHIPS - 1 + k, piece, piece, sibling).start()

    def finish():
        for i in range(ns):
            for k in range(N_CHIPS - 1):
                piece = landed(i, k, 1 - c)
                remote(i, N_CHIPS - 1 + k, piece, piece, sibling).wait_recv()
        for i in range(n):
            for k in range(N_CHIPS - 1):
                over_ici(i, k).wait_send()
                if i < ns:
                    piece = landed(i, k, c)
                    remote(i, N_CHIPS - 1 + k, piece, piece, sibling).wait_send()

    return start, forward, finish


def _gather_chips_split(name, split, whole):
    arrs = list(split) + list(whole)
    n = len(arrs)

    def body(*refs):
        phases = _gather_phases([a.shape[0] for a in arrs], len(split), refs[:n], refs[n:2 * n], *refs[2 * n:])
        for phase in phases:
            phase()

    hbm = pl.BlockSpec(memory_space=pl.ANY)
    return pl.pallas_call(
        body, name=name, in_specs=[hbm] * n, out_specs=[hbm] * n,
        out_shape=[jax.ShapeDtypeStruct((N_CHIPS,) + a.shape, a.dtype) for a in arrs],
        scratch_shapes=_exchange_scratch(n, GATHER_SEMS),
        compiler_params=pltpu.CompilerParams(has_side_effects=True),
    )(*arrs)


def _swap_cores(name, arrs):
    n = len(arrs)

    def body(*refs):
        ins, outs = refs[:n], refs[n:2 * n]
        send_sems, recv_sems = refs[2 * n:]
        x, y, c = _position()
        copies = [pltpu.make_async_remote_copy(src_ref=ins[i], dst_ref=outs[i], send_sem=send_sems.at[i],
                                               recv_sem=recv_sems.at[i], device_id=(x, y, 1 - c), device_id_type=MESH)
                  for i in range(n)]
        for cp in copies:
            cp.start()
        for cp in copies:
            cp.wait_recv()
        for cp in copies:
            cp.wait_send()

    hbm = pl.BlockSpec(memory_space=pl.ANY)
    return pl.pallas_call(
        body, name=name, in_specs=[hbm] * n, out_specs=[hbm] * n,
        out_shape=[jax.ShapeDtypeStruct(a.shape, a.dtype) for a in arrs],
        scratch_shapes=[pltpu.SemaphoreType.DMA((n,)), pltpu.SemaphoreType.DMA((n,))],
        compiler_params=pltpu.CompilerParams(has_side_effects=True),
    )(*arrs)


def _row_half(a, which, axis):
    half = a.shape[axis] // 2
    return lax.dynamic_slice_in_dim(a, which * half, half, axis=axis)


def _all_reduce(name, pack):
    def body(in_ref, out_ref, buf, send_sems, recv_sems):
        x, y, c = _position()
        me = 4 * x + 2 * y + c
        flips = [(dx, dy, dc) for dx in (0, 1) for dy in (0, 1) for dc in (0, 1) if (dx, dy, dc) != (0, 0, 0)]
        peers = [((1 - x) if dx else x, (1 - y) if dy else y, (1 - c) if dc else c) for dx, dy, dc in flips]
        buf[me] = in_ref[...]
        sends = []
        for k, peer in enumerate(peers):
            cp = pltpu.make_async_remote_copy(src_ref=in_ref, dst_ref=buf.at[me], send_sem=send_sems.at[k],
                                              recv_sem=recv_sems.at[k], device_id=peer, device_id_type=MESH)
            cp.start()
            sends.append(cp)
        for k, (px, py, pc) in enumerate(peers):
            pltpu.make_async_remote_copy(src_ref=in_ref, dst_ref=buf.at[4 * px + 2 * py + pc], send_sem=send_sems.at[k],
                                         recv_sem=recv_sems.at[k], device_id=(px, py, pc), device_id_type=MESH).wait_recv()
        total = buf[0]
        for j in range(1, N_DEV):
            total = total + buf[j]
        out_ref[...] = total
        for cp in sends:
            cp.wait_send()

    vmem = pl.BlockSpec(memory_space=pltpu.VMEM)
    return pl.pallas_call(
        body, name=name, in_specs=[vmem], out_specs=vmem,
        out_shape=jax.ShapeDtypeStruct(pack.shape, F32),
        scratch_shapes=[pltpu.VMEM((N_DEV,) + pack.shape, F32), pltpu.SemaphoreType.DMA((N_DEV - 1,)),
                        pltpu.SemaphoreType.DMA((N_DEV - 1,))],
        compiler_params=pltpu.CompilerParams(has_side_effects=True, vmem_limit_bytes=VMEM_LIMIT_V7X),
    )(pack)


def _pack(arrs):
    rows = [a.reshape(-1, LANES) for a in arrs]
    total = sum(r.shape[0] for r in rows)
    rows.append(jnp.zeros((-total % SUBLANES, LANES), F32))
    return jnp.concatenate(rows, axis=0)


def _unpack(pack, shapes):
    out, off = [], 0
    for s in shapes:
        nrow = 1
        for d in s:
            nrow *= d
        nrow //= LANES
        out.append(pack[off:off + nrow].reshape(s))
        off += nrow
    return out


SMALL = ["mix_norm_w", "gate_bias", "gmlp_ln_w", "gmlp_ln_b", "gmlp_ws", "gmlp_bs", "ssm_conv_w", "ssm_conv_b",
         "ssm_dt_bias", "ssm_a_log", "ssm_d", "ssm_norm_w", "ffn_norm_w", "ffn_conv_w", "ffn_conv_b", "final_norm_w"]
SMALL_SHARDED = ("gate_bias", "ssm_conv_w", "ffn_conv_w")
BIG = ["w_in", "w_proj_a", "w_proj_b", "w_out", "ffn_w_up", "ffn_w_down"]
WEIGHTS = ["mix_norm_w", "w_in", "gate_bias", "gmlp_ln_w", "gmlp_ln_b", "gmlp_ws", "gmlp_bs", "ssm_conv_w",
           "ssm_conv_b", "ssm_dt_bias", "ssm_a_log", "ssm_d", "ssm_norm_w", "w_proj_a", "w_proj_b", "w_out",
           "ffn_norm_w", "ffn_w_up", "ffn_conv_w", "ffn_conv_b", "ffn_w_down", "final_norm_w"]
IN_SPLITS = [0, 2048, 4096, 6144, 9216, 9248]


def _columns_from_chips(stack):
    return jnp.transpose(stack, (1, 0, 2)).reshape(stack.shape[1], -1)


def _columns_to_chips(full, parts=N_CHIPS):
    rows, cols = full.shape
    return jnp.transpose(full.reshape(rows, parts, cols // parts), (1, 0, 2))


def kernel(x, mix_norm_w, w_in, gate_bias, gmlp_ln_w, gmlp_ln_b, gmlp_ws, gmlp_bs, ssm_conv_w, ssm_conv_b, ssm_dt_bias, ssm_a_log, ssm_d, ssm_norm_w, w_proj_a, w_proj_b, w_out, ffn_norm_w, ffn_w_up, ffn_conv_w, ffn_conv_b, ffn_w_down, final_norm_w, loss_target, m_mix_norm_w, m_w_in, m_gate_bias, m_gmlp_ln_w, m_gmlp_ln_b, m_gmlp_ws, m_gmlp_bs, m_ssm_conv_w, m_ssm_conv_b, m_ssm_dt_bias, m_ssm_a_log, m_ssm_d, m_ssm_norm_w, m_w_proj_a, m_w_proj_b, m_w_out, m_ffn_norm_w, m_ffn_w_up, m_ffn_conv_w, m_ffn_conv_b, m_ffn_w_down, m_final_norm_w, v_mix_norm_w, v_w_in, v_gate_bias, v_gmlp_ln_w, v_gmlp_ln_b, v_gmlp_ws, v_gmlp_bs, v_ssm_conv_w, v_ssm_conv_b, v_ssm_dt_bias, v_ssm_a_log, v_ssm_d, v_ssm_norm_w, v_w_proj_a, v_w_proj_b, v_w_out, v_ffn_norm_w, v_ffn_w_up, v_ffn_conv_w, v_ffn_conv_b, v_ffn_w_down, v_final_norm_w):
    args = dict(locals())
    weights = {n: args[n] for n in WEIGHTS}
    moments_m = {n: args["m_" + n] for n in WEIGHTS}
    moments_v = {n: args["v_" + n] for n in WEIGHTS}
    chip = 2 * lax.axis_index("x") + lax.axis_index("y")

    shards = [weights["w_in"][0].astype(BF16)] + [weights[n][0] for n in SMALL_SHARDED]
    gathered = _gather_chips_split("gather_weights", shards[:1], shards[1:])
    w_in_s, gb_s, scw_s, fcw_s = [_own_slot(stack, own) for stack, own in zip(gathered, shards)]
    late_shards = [weights[n][0].astype(BF16) for n in LATE]
    w_in_full = _columns_from_chips(w_in_s)
    full = {"w_" + nm: w_in_full[:, IN_SPLITS[k]:IN_SPLITS[k + 1]] for k, nm in enumerate(["g", "za", "z", "xbc", "dt"])}
    full["gate_bias"] = _columns_from_chips(gb_s)
    full["ssm_conv_w"] = _columns_from_chips(scw_s)
    full["ffn_conv_w"] = _columns_from_chips(fcw_s)
    for n in SMALL:
        if n not in SMALL_SHARDED:
            full[n] = weights[n] if n == "final_norm_w" else weights[n][0]
    for n in ("mix_norm_w", "ffn_norm_w", "ssm_conv_b", "ssm_dt_bias", "ssm_a_log", "ssm_d", "ssm_norm_w", "ffn_conv_b"):
        full[n] = full[n].reshape(1, -1)

    loss_part, grad_x, g, pair, received = _local_step(x[0], loss_target[0], full, late_shards)

    per_head = ["ssm_dt_bias", "ssm_a_log", "ssm_d"]
    rest = [n for n in SMALL if n not in per_head]
    head_row = jnp.concatenate([g[n] for n in per_head] + [jnp.zeros((1, LANES - 3 * SSM_HEADS), F32)], axis=1)
    pack = _pack([loss_part, head_row] + [g[n] for n in rest])
    reduced = _unpack(_all_reduce("reduce_small", pack), [(1, LANES), (1, LANES)] + [g[n].shape for n in rest])
    loss = reduced[0][0, 0]
    small_grads = {n: reduced[1][:, k * SSM_HEADS:(k + 1) * SSM_HEADS] for k, n in enumerate(per_head)}
    for n, r in zip(rest, reduced[2:]):
        if n in SMALL_SHARDED:
            width = weights[n].shape[2]
            r = lax.dynamic_slice_in_dim(r, chip * width, width, axis=1)
        small_grads[n] = r
    two_d = lambda a: a.reshape(-1, a.shape[-1])
    upd = _adamw_small("adamw_small", *[[two_d(d[n]) for n in SMALL]
                                        for d in (weights, small_grads, moments_m, moments_v)])
    small_out = [[small_grads[n] for n in SMALL]] + list(upd)
    small_out = [[a.reshape(weights[n].shape) for n, a in zip(SMALL, kind)] for kind in small_out]

    received = [_own_slot(r, lax.dynamic_index_in_dim(p, chip, 0, keepdims=False)) for r, p in zip(received, pair)]
    halves = [_sum_slots("sum_" + n, r, tm=HALF_TILES[n], rs=2 * SUBLANES) for n, r in zip(BIG, received)]
    tiles = {"w_in": 128, "w_proj_a": 256, "w_proj_b": 256, "w_out": 256, "ffn_w_up": 128, "ffn_w_down": 176}
    core = lax.axis_index("c")
    other = _swap_cores("join_grads", halves)
    reduced = [jnp.concatenate([jnp.where(core == 0, a, b), jnp.where(core == 0, b, a)], axis=0)
               for a, b in zip(halves, other)]
    big_out = {}
    for n, grad in zip(BIG, reduced):
        big_out[n] = _adamw("adamw_" + n, weights[n][0], grad, moments_m[n][0], moments_v[n][0],
                            tm=tiles[n], rs=SUBLANES)

    per_kind = [[], [], [], []]
    for n in WEIGHTS:
        for kind in range(4):
            if n in big_out:
                per_kind[kind].append(big_out[n][kind].reshape(weights[n].shape))
            else:
                per_kind[kind].append(small_out[kind][SMALL.index(n)])
    return (loss, grad_x[None], *per_kind[0], *per_kind[1], *per_kind[2], *per_kind[3])
```

```python
import jax
import jax.numpy as jnp
from jax import lax
from jax.experimental import pallas as pl
from jax.experimental.pallas import tpu as pltpu

F32 = jnp.float32
BF16 = jnp.bfloat16
MESH = pl.DeviceIdType.MESH

EPS = 1e-5
D_MODEL = 1024
GMLP_BLOCK = 128
GMLP_GROUPS = 8
CHUNK = 64
SSM_INNER = 2048
SSM_HEADS = 32
SSM_HEAD_DIM = 64
SSM_GROUPS = 4
SSM_HPG = 8
SSM_STATE = 128
SSM_CONV = 4
SSM_XBC = 3072
D_FF = 2816
FFN_CONV = 3
N_CHIPS = 4
N_DEV = 8

ADAM_LR = 0.001
ADAM_B1 = 0.9
ADAM_B2 = 0.999
ADAM_EPS = 1e-08
ADAM_WD = 0.01
ADAM_STEP = 10

VMEM_LIMIT_V7X = 56 * 1024 * 1024
SUBLANES = 8
LANES = 128


def _params(sem=None):
    return pltpu.CompilerParams(dimension_semantics=sem, vmem_limit_bytes=VMEM_LIMIT_V7X)


def _dot(a, b, ca=1, cb=0):
    return lax.dot_general(a.astype(BF16), b.astype(BF16), (((ca,), (cb,)), ((), ())),
                           preferred_element_type=F32)


def _mm(name, a, b, *, ta=False, tb=False, out_dtype=F32, bm, bn, bk, res=None):
    m, k = (a.shape[1], a.shape[0]) if ta else a.shape
    k2, n = (b.shape[1], b.shape[0]) if tb else b.shape
    assert k == k2 and m % bm == 0 and n % bn == 0 and k % bk == 0, (name, a.shape, b.shape)
    nk = k // bk
    a_spec = (pl.BlockSpec((bk, bm), lambda i, j, kk: (kk, i)) if ta
              else pl.BlockSpec((bm, bk), lambda i, j, kk: (i, kk)))
    b_spec = (pl.BlockSpec((bn, bk), lambda i, j, kk: (j, kk)) if tb
              else pl.BlockSpec((bk, bn), lambda i, j, kk: (kk, j)))
    o_spec = pl.BlockSpec((bm, bn), lambda i, j, kk: (i, j))
    has_res = res is not None

    def body(*refs):
        a_ref, b_ref = refs[0], refs[1]
        r_ref = refs[2] if has_res else None
        o_ref = refs[3] if has_res else refs[2]
        p = _dot(a_ref[...], b_ref[...], 0 if ta else 1, 1 if tb else 0)

        def finish(total):
            if has_res:
                total = total + r_ref[...]
            o_ref[...] = total.astype(out_dtype)

        if nk == 1:
            finish(p)
        else:
            acc_ref = refs[-1]
            kk = pl.program_id(2)

            @pl.when(kk == 0)
            def _():
                acc_ref[...] = p

            @pl.when(kk > 0)
            def _():
                acc_ref[...] += p

            @pl.when(kk == nk - 1)
            def _():
                finish(acc_ref[...])

    return pl.pallas_call(
        body, name=name,
        grid=(m // bm, n // bn, nk),
        in_specs=[a_spec, b_spec] + ([o_spec] if has_res else []),
        out_specs=o_spec,
        out_shape=jax.ShapeDtypeStruct((m, n), out_dtype),
        scratch_shapes=[pltpu.VMEM((bm, bn), F32)] if nk > 1 else [],
        compiler_params=_params(("parallel", "parallel", "arbitrary")),
    )(*([a, b] + ([res] if has_res else [])))


def _mm_sum(name, pairs, *, bm, bk, exchange=()):
    nx = len(exchange)
    npair = len(pairs)
    m, n = pairs[0][0].shape[0], pairs[0][1].shape[0]
    steps, first = [], []
    for a, b in pairs:
        k = a.shape[1]
        assert a.shape[0] == m and b.shape == (n, k) and m % bm == 0 and (k % bk == 0 or k < bk), (name, a.shape, b.shape)
        first.append(sum(steps))
        steps.append(max(k // bk, 1))
    total = sum(steps)
    in_specs = []
    for (a, b), off, cnt in zip(pairs, first, steps):
        width = min(bk, a.shape[1])
        in_specs.append(pl.BlockSpec((bm, width), lambda i, kk, off=off, cnt=cnt: (i, jnp.clip(kk - off, 0, cnt - 1))))
        in_specs.append(pl.BlockSpec((n, width), lambda i, kk, off=off, cnt=cnt: (0, jnp.clip(kk - off, 0, cnt - 1))))

    def body(*refs):
        send_refs = refs[2 * npair:2 * npair + nx]
        o_ref = refs[2 * npair + nx]
        recv_refs = refs[2 * npair + nx + 1:2 * npair + 2 * nx + 1]
        acc_ref = refs[2 * npair + 2 * nx + 1]
        i, kk = pl.program_id(0), pl.program_id(1)
        if nx:
            start, finish = _scatter_phases(send_refs, recv_refs, *refs[2 * npair + 2 * nx + 2:])

            @pl.when((i == 0) & (kk == 0))
            def _():
                start()

        for s, (off, cnt) in enumerate(zip(first, steps)):
            @pl.when((kk >= off) & (kk < off + cnt))
            def _(s=s, off=off):
                p = _dot(refs[2 * s][...], refs[2 * s + 1][...], 1, 1)
                if off == 0:
                    @pl.when(kk == 0)
                    def _():
                        acc_ref[...] = p

                    @pl.when(kk > 0)
                    def _():
                        acc_ref[...] += p
                else:
                    acc_ref[...] += p

        @pl.when(kk == total - 1)
        def _():
            o_ref[...] = acc_ref[...]

        if nx:
            @pl.when((i == m // bm - 1) & (kk == total - 1))
            def _():
                finish()

    hbm = pl.BlockSpec(memory_space=pl.ANY)
    res = pl.pallas_call(
        body, name=name, grid=(m // bm, total),
        in_specs=in_specs + [hbm] * nx, out_specs=[pl.BlockSpec((bm, n), lambda i, kk: (i, 0))] + [hbm] * nx,
        out_shape=[jax.ShapeDtypeStruct((m, n), F32)] + [jax.ShapeDtypeStruct(e.shape, e.dtype) for e in exchange],
        scratch_shapes=[pltpu.VMEM((bm, n), F32)] + (_exchange_scratch(nx, N_CHIPS - 1) if nx else []),
        compiler_params=_params(("arbitrary", "arbitrary")),
    )(*[t for pair in pairs for t in pair], *exchange)
    return (res[0], res[1:]) if nx else res[0]


def _rows(name, fn, ins, params, outs, accs, *, tm, rs, unroll=4):
    nrow = ins[0][0].shape[-2]
    while tm % (rs * unroll):
        unroll //= 2
    assert nrow % tm == 0 and tm % rs == 0, (name, nrow, tm, rs)
    n_in, n_p, n_out, n_acc = len(ins), len(params), len(outs), len(accs)
    in_specs = []
    for spec in ins:
        arr, width, cb = spec[:3]
        if len(spec) == 4:
            in_specs.append(pl.BlockSpec((None, tm, width), lambda i, cb=cb, lead=spec[3]: (lead, i, cb)))
        else:
            in_specs.append(pl.BlockSpec((tm, width), lambda i, cb=cb: (i, cb)))
    for p in params:
        in_specs.append(pl.BlockSpec(p.shape, lambda i, nd=p.ndim: (0,) * nd))
    out_specs = [pl.BlockSpec((tm, w), lambda i: (i, 0)) for w, _ in outs]
    out_specs += [pl.BlockSpec(s, lambda i: (0, 0)) for s in accs]
    out_shape = [jax.ShapeDtypeStruct((nrow, w), dt) for w, dt in outs]
    out_shape += [jax.ShapeDtypeStruct(s, F32) for s in accs]

    def body(*refs):
        in_refs = refs[:n_in]
        p_refs = refs[n_in:n_in + n_p]
        o_refs = refs[n_in + n_p:n_in + n_p + n_out]
        a_refs = refs[n_in + n_p + n_out:]
        pv = [p[...] for p in p_refs]

        if n_acc:
            @pl.when(pl.program_id(0) == 0)
            def _():
                for a_ref in a_refs:
                    a_ref[...] = jnp.zeros_like(a_ref)

        def step(r, carry):
            for u in range(unroll):
                sl = pl.ds(pl.multiple_of((r * unroll + u) * rs, rs), rs)
                vals = [ref[sl, :].astype(F32) for ref in in_refs]
                row_out, sums = fn(*vals, *pv)
                for o_ref, v in zip(o_refs, row_out):
                    o_ref[sl, :] = v.astype(o_ref.dtype)
                carry = tuple(c + s for c, s in zip(carry, sums))
            return carry

        init = tuple(jnp.zeros(s, F32) for s in accs)
        total = lax.fori_loop(0, tm // (rs * unroll), step, init)
        for a_ref, t in zip(a_refs, total):
            a_ref[...] += t

    res = pl.pallas_call(
        body, name=name, grid=(nrow // tm,),
        in_specs=in_specs, out_specs=out_specs, out_shape=out_shape,
        compiler_params=_params(("arbitrary",)),
    )(*([s[0] for s in ins] + list(params)))
    return res


def _rms(x, w):
    return x * lax.rsqrt(jnp.mean(x * x, axis=-1, keepdims=True) + EPS) * w


def _colsum(v):
    return jnp.sum(v, axis=0, keepdims=True)


def _rms_fwd(name, x, w):
    def fn(xv, wv):
        return (_rms(xv, wv),), ()
    return _rows(name, fn, [(x, D_MODEL, 0)], [w], [(D_MODEL, BF16)], [], tm=512, rs=16)[0]


def _rms_bwd(name, x, w, dy, dres):
    def fn(xv, dyv, drv, wv):
        _, vjp = jax.vjp(_rms, xv, wv)
        dx, dw = vjp(dyv)
        return (drv + dx,), (dw,)
    return _rows(name, fn, [(x, D_MODEL, 0), (dy, D_MODEL, 0), (dres, D_MODEL, 0)], [w],
                 [(D_MODEL, F32)], [(1, D_MODEL)], tm=512, rs=16)


def _final_loss(name, h, target, w):
    def fn(hv, tv, wv):
        y, vjp = jax.vjp(_rms, hv, wv)
        err = y - tv
        part = 0.5 * jnp.sum(jnp.mean(err * err, axis=-1, keepdims=True), axis=0, keepdims=True)
        dh, dw = vjp(err / D_MODEL)
        return (dh,), (jnp.broadcast_to(part, (1, LANES)), dw)
    return _rows(name, fn, [(h, D_MODEL, 0), (target, D_MODEL, 0)], [w],
                 [(D_MODEL, F32)], [(1, LANES), (1, D_MODEL)], tm=512, rs=16)


def _merge(ga, gb, ya, yb, b0, b1):
    return jax.nn.sigmoid(ga + b0) * ya + jax.nn.sigmoid(gb + b1) * yb


def _merge_fwd(name, g, ya, yb, b0, b1):
    def fn(ga, gb, yav, ybv, b0v, b1v):
        return (_merge(ga, gb, yav, ybv, b0v, b1v),), ()
    return _rows(name, fn, [(g, D_MODEL, 0), (g, D_MODEL, 1), (ya, D_MODEL, 0), (yb, D_MODEL, 0)],
                 [b0, b1], [(D_MODEL, BF16)], [], tm=512, rs=16)[0]


def _merge_bwd(name, g, ya, yb, dm, b0, b1):
    def fn(ga, gb, yav, ybv, dmv, b0v, b1v):
        _, vjp = jax.vjp(_merge, ga, gb, yav, ybv, b0v, b1v)
        dga, dgb, dya, dyb, db0, db1 = vjp(dmv)
        return (jnp.concatenate([dga, dgb], axis=1), dya, dyb), (db0, db1)
    return _rows(name, fn,
                 [(g, D_MODEL, 0), (g, D_MODEL, 1), (ya, D_MODEL, 0), (yb, D_MODEL, 0), (dm, D_MODEL, 0)],
                 [b0, b1], [(2 * D_MODEL, BF16), (D_MODEL, BF16), (D_MODEL, BF16)],
                 [(1, D_MODEL), (1, D_MODEL)], tm=512, rs=16)


GROUP_W = SSM_INNER // SSM_GROUPS


def _gate_norm_group(y, z, nw):
    v = y * jax.nn.silu(z)
    return v * lax.rsqrt(jnp.mean(v * v, axis=-1, keepdims=True) + EPS) * nw


def _gate_norm_fwd(name, y, z, nw):
    def fn(yv, zv, nwv):
        parts = [_gate_norm_group(yv[:, k * GROUP_W:(k + 1) * GROUP_W], zv[:, k * GROUP_W:(k + 1) * GROUP_W],
                                  nwv[:, k * GROUP_W:(k + 1) * GROUP_W]) for k in range(SSM_GROUPS)]
        return (jnp.concatenate(parts, axis=1),), ()
    return _rows(name, fn, [(y, SSM_INNER, 0), (z, SSM_INNER, 0)], [nw], [(SSM_INNER, BF16)], [],
                 tm=512, rs=16)[0]


def _gate_norm_bwd(name, y, z, dout, nw):
    def fn(yv, zv, dv, nwv):
        dys, dzs, dns = [], [], []
        for k in range(SSM_GROUPS):
            sl = slice(k * GROUP_W, (k + 1) * GROUP_W)
            _, vjp = jax.vjp(_gate_norm_group, yv[:, sl], zv[:, sl], nwv[:, sl])
            dy, dz, dn = vjp(dv[:, sl])
            dys.append(dy), dzs.append(dz), dns.append(dn)
        return (jnp.concatenate(dys, axis=1), jnp.concatenate(dzs, axis=1)), (jnp.concatenate(dns, axis=1),)
    return _rows(name, fn, [(y, SSM_INNER, 0), (z, SSM_INNER, 0), (dout, SSM_INNER, 0)], [nw],
                 [(SSM_INNER, BF16), (SSM_INNER, BF16)], [(1, SSM_INNER)], tm=512, rs=16)


def _softplus(v):
    return jnp.maximum(v, 0.0) + jnp.log1p(jnp.exp(-jnp.abs(v)))


def _chunk_cumsum(v, reverse=False):
    row = lax.broadcasted_iota(jnp.int32, v.shape, 0)
    step = 1
    while step < CHUNK:
        if reverse:
            shifted = pltpu.roll(v, CHUNK - step, axis=0)
            v = v + jnp.where(row < CHUNK - step, shifted, 0.0)
        else:
            shifted = pltpu.roll(v, step, axis=0)
            v = v + jnp.where(row >= step, shifted, 0.0)
        step *= 2
    return v


def _dt_prep(name, dt_raw, dt_bias, a_log):
    def fn(rv, bv, alv):
        dt = _softplus(rv + bv)
        return (dt, _chunk_cumsum(dt * (-jnp.exp(alv)))), ()
    return _rows(name, fn, [(dt_raw, SSM_HEADS, 0)], [dt_bias, a_log],
                 [(SSM_HEADS, F32), (SSM_HEADS, F32)], [], tm=512, rs=CHUNK)


def _dt_bwd(name, dt_raw, ddt, da1, da2, dt_bias, a_log):
    def fn(rv, ddv, d1, d2, bv, alv):
        pre = rv + bv
        dt = _softplus(pre)
        a_neg = -jnp.exp(alv)
        back = _chunk_cumsum(d1 + d2, reverse=True)
        d_dt = ddv + back * a_neg
        d_raw = d_dt * jax.nn.sigmoid(pre)
        return (d_raw,), (_colsum(d_raw), _colsum(back * dt) * a_neg)
    return _rows(name, fn, [(dt_raw, SSM_HEADS, 0), (ddt, SSM_HEADS, 0), (da1, SSM_HEADS, 0), (da2, SSM_HEADS, 0)],
                 [dt_bias, a_log], [(SSM_HEADS, BF16)], [(1, SSM_HEADS), (1, SSM_HEADS)], tm=512, rs=CHUNK)


def _adamw_math(w, g, m, v):
    m_new = ADAM_B1 * m + (1.0 - ADAM_B1) * g
    v_new = ADAM_B2 * v + (1.0 - ADAM_B2) * jnp.square(g)
    m_hat = m_new / (1.0 - ADAM_B1 ** ADAM_STEP)
    v_hat = v_new / (1.0 - ADAM_B2 ** ADAM_STEP)
    delta = -ADAM_LR * (m_hat / (jnp.sqrt(v_hat) + ADAM_EPS) + ADAM_WD * w)
    return delta, m_new, v_new


def _adamw(name, w, g, m, v, *, tm, rs):
    width = w.shape[1]

    def fn(wv, mv, vv, gv):
        return (gv,) + _adamw_math(wv, gv, mv, vv), ()
    return _rows(name, fn, [(w, width, 0), (m, width, 0), (v, width, 0), (g, width, 0)],
                 [], [(width, F32)] * 4, [], tm=tm, rs=rs)


def _adamw_small(name, ws, gs, ms, vs):
    n = len(ws)

    def body(*refs):
        w_refs, g_refs, m_refs, v_refs = (refs[k * n:(k + 1) * n] for k in range(4))
        outs = refs[4 * n:]
        for i in range(n):
            res = _adamw_math(w_refs[i][...], g_refs[i][...], m_refs[i][...], v_refs[i][...])
            for k in range(3):
                outs[k * n + i][...] = res[k]

    vmem = pl.BlockSpec(memory_space=pltpu.VMEM)
    res = pl.pallas_call(
        body, name=name, in_specs=[vmem] * (4 * n), out_specs=[vmem] * (3 * n),
        out_shape=[jax.ShapeDtypeStruct(w.shape, F32) for w in ws] * 3,
        compiler_params=pltpu.CompilerParams(vmem_limit_bytes=VMEM_LIMIT_V7X),
    )(*ws, *gs, *ms, *vs)
    return res[:n], res[n:2 * n], res[2 * n:]


def _pair_sum(name, a, b, *, tm):
    shape = a.shape
    flat = (shape[0] * shape[1], shape[2])

    def fn(av, bv):
        return (av.astype(F32) + bv.astype(F32),), ()
    out = _rows(name, fn, [(a.reshape(flat), flat[1], 0), (b.reshape(flat), flat[1], 0)], [], [(flat[1], BF16)], [],
                tm=tm, rs=2 * SUBLANES)[0]
    return out.reshape(shape)


def _sum_slots(name, stack, *, tm, rs):
    width = stack.shape[2]

    def fn(*slots):
        s0, s1, s2, s3 = (s.astype(F32) for s in slots)
        return (((s0 + s1) + s2) + s3,), ()
    return _rows(name, fn, [(stack, width, 0, k) for k in range(N_CHIPS)], [], [(width, F32)], [],
                 tm=tm, rs=rs)[0]


def _layernorm(v, w, b):
    mu = jnp.mean(v, axis=-1, keepdims=True)
    var = jnp.mean(jnp.square(v - mu), axis=-1, keepdims=True)
    return (v - mu) * lax.rsqrt(var + EPS) * w + b


GELU_C = 0.7978845608028654
GELU_A = 0.044715


def _gelu_and_slope(x):
    x2 = x * x
    t = jnp.tanh(GELU_C * x * (1.0 + GELU_A * x2))
    half = 0.5 * (1.0 + t)
    slope = half + 0.5 * x * (1.0 - t * t) * (GELU_C * (1.0 + 3.0 * GELU_A * x2))
    return x * half, slope


def _layernorm_and_back(v, w, b):
    mu = jnp.mean(v, axis=-1, keepdims=True)
    cen = v - mu
    rstd = lax.rsqrt(jnp.mean(cen * cen, axis=-1, keepdims=True) + EPS)
    vhat = cen * rstd

    def back(dout):
        dhat = dout * w
        dv = rstd * (dhat - jnp.mean(dhat, axis=-1, keepdims=True)
                     - vhat * jnp.mean(dhat * vhat, axis=-1, keepdims=True))
        return dv, _colsum(dout * vhat), _colsum(dout)

    return vhat * w + b, back


def _gmlp_mask():
    t = lax.broadcasted_iota(jnp.int32, (GMLP_BLOCK, GMLP_BLOCK), 0) // CHUNK
    s = lax.broadcasted_iota(jnp.int32, (GMLP_BLOCK, GMLP_BLOCK), 1) // CHUNK
    return s <= t


GMLP_TM = 512


def _gmlp_fwd(name, za, ln_w, ln_b, ws, bs_col):
    nrow = za.shape[0]
    tm = GMLP_TM
    width = GMLP_GROUPS * GMLP_BLOCK

    def body(za_ref, lnw_ref, lnb_ref, ws_ref, bs_ref, o_ref, wm_ref):
        mask = _gmlp_mask()
        for g in range(GMLP_GROUPS):
            wm_ref[g] = jnp.where(mask, ws_ref[g], 0.0).astype(BF16)

        def block(n, carry):
            rows = pl.ds(pl.multiple_of(n * GMLP_BLOCK, GMLP_BLOCK), GMLP_BLOCK)
            for g in range(GMLP_GROUPS):
                cols = slice(g * GMLP_BLOCK, (g + 1) * GMLP_BLOCK)
                vcols = slice(width + g * GMLP_BLOCK, width + (g + 1) * GMLP_BLOCK)
                u = jax.nn.gelu(za_ref[rows, cols].astype(F32))
                v = jax.nn.gelu(za_ref[rows, vcols].astype(F32))
                vn = _layernorm(v, lnw_ref[g:g + 1, :], lnb_ref[g:g + 1, :])
                sv = _dot(wm_ref[g], vn) + bs_ref[g]
                o_ref[rows, cols] = (u * sv).astype(o_ref.dtype)
            return carry

        lax.fori_loop(0, tm // GMLP_BLOCK, block, 0)

    small = lambda a: pl.BlockSpec(a.shape, lambda i, nd=a.ndim: (0,) * nd)
    return pl.pallas_call(
        body, name=name, grid=(nrow // tm,),
        in_specs=[pl.BlockSpec((tm, 2 * width), lambda i: (i, 0)), small(ln_w), small(ln_b), small(ws), small(bs_col)],
        out_specs=pl.BlockSpec((tm, width), lambda i: (i, 0)),
        out_shape=jax.ShapeDtypeStruct((nrow, width), BF16),
        scratch_shapes=[pltpu.VMEM((GMLP_GROUPS, GMLP_BLOCK, GMLP_BLOCK), BF16)],
        compiler_params=_params(("arbitrary",)),
    )(za, ln_w, ln_b, ws, bs_col)


def _gmlp_bwd(name, za, dout, ln_w, ln_b, ws, bs_col):
    nrow = za.shape[0]
    tm = GMLP_TM
    width = GMLP_GROUPS * GMLP_BLOCK

    def body(za_ref, do_ref, lnw_ref, lnb_ref, ws_ref, bs_ref, dza_ref, dlnw_ref, dlnb_ref, dws_ref, dbs_ref, wm_ref):
        mask = _gmlp_mask()
        for g in range(GMLP_GROUPS):
            wm_ref[g] = jnp.where(mask, ws_ref[g], 0.0).astype(BF16)

        @pl.when(pl.program_id(0) == 0)
        def _():
            dlnw_ref[...] = jnp.zeros_like(dlnw_ref)
            dlnb_ref[...] = jnp.zeros_like(dlnb_ref)
            dws_ref[...] = jnp.zeros_like(dws_ref)
            dbs_ref[...] = jnp.zeros_like(dbs_ref)

        def block(n, carry):
            rows = pl.ds(pl.multiple_of(n * GMLP_BLOCK, GMLP_BLOCK), GMLP_BLOCK)
            for g in range(GMLP_GROUPS):
                cols = slice(g * GMLP_BLOCK, (g + 1) * GMLP_BLOCK)
                vcols = slice(width + g * GMLP_BLOCK, width + (g + 1) * GMLP_BLOCK)
                u, slope_u = _gelu_and_slope(za_ref[rows, cols].astype(F32))
                v, slope_v = _gelu_and_slope(za_ref[rows, vcols].astype(F32))
                vn, ln_back = _layernorm_and_back(v, lnw_ref[g:g + 1, :], lnb_ref[g:g + 1, :])
                sv = _dot(wm_ref[g], vn) + bs_ref[g]
                d_o = do_ref[rows, cols].astype(F32)
                dsv = d_o * u
                d_wm = _dot(dsv, vn, 1, 1)
                dvn = _dot(wm_ref[g], dsv, 0, 0)
                dv, dlnw, dlnb = ln_back(dvn)
                dza_ref[rows, cols] = (d_o * sv * slope_u).astype(dza_ref.dtype)
                dza_ref[rows, vcols] = (dv * slope_v).astype(dza_ref.dtype)
                dlnw_ref[g:g + 1, :] += dlnw
                dlnb_ref[g:g + 1, :] += dlnb
                dws_ref[g] += jnp.where(mask, d_wm, 0.0)
                dbs_ref[g] += jnp.sum(dsv, axis=1, keepdims=True)
            return carry

        lax.fori_loop(0, tm // GMLP_BLOCK, block, 0)

    small = lambda a: pl.BlockSpec(a.shape, lambda i, nd=a.ndim: (0,) * nd)
    return pl.pallas_call(
        body, name=name, grid=(nrow // tm,),
        in_specs=[pl.BlockSpec((tm, 2 * width), lambda i: (i, 0)), pl.BlockSpec((tm, width), lambda i: (i, 0)),
                  small(ln_w), small(ln_b), small(ws), small(bs_col)],
        out_specs=[pl.BlockSpec((tm, 2 * width), lambda i: (i, 0)), small(ln_w), small(ln_b), small(ws), small(bs_col)],
        out_shape=[jax.ShapeDtypeStruct((nrow, 2 * width), BF16), jax.ShapeDtypeStruct(ln_w.shape, F32),
                   jax.ShapeDtypeStruct(ln_b.shape, F32), jax.ShapeDtypeStruct(ws.shape, F32),
                   jax.ShapeDtypeStruct(bs_col.shape, F32)],
        scratch_shapes=[pltpu.VMEM((GMLP_GROUPS, GMLP_BLOCK, GMLP_BLOCK), BF16)],
        compiler_params=_params(("arbitrary",)),
    )(za, dout, ln_w, ln_b, ws, bs_col)


CONV_TM = 512
CONV_RS = 32
HALO = 2 * SUBLANES


def _tap_rows(w_ref):
    return [w_ref[k:k + 1, :] for k in range(w_ref.shape[0])]


def _halo_specs(nrow, tm, tc):
    per = tm // HALO
    last = nrow // HALO - 1
    main = pl.BlockSpec((tm, tc), lambda j, i: (i, j))
    before = pl.BlockSpec((HALO, tc), lambda j, i: (jnp.maximum(i * per - 1, 0), j))
    after = pl.BlockSpec((HALO, tc), lambda j, i: (jnp.minimum((i + 1) * per, last), j))
    return main, before, after


def _col_spec(rows, tc):
    return pl.BlockSpec((rows, tc), lambda j, i: (0, j))


def _conv_fwd(name, x, w, b, *, tc):
    nrow, ncol = x.shape
    taps = w.shape[0]
    tm, rs = CONV_TM, CONV_RS
    main, before, _ = _halo_specs(nrow, tm, tc)

    def body(x_ref, xb_ref, w_ref, b_ref, o_ref, xw_ref):
        first = pl.program_id(1) == 0
        wv, bv = _tap_rows(w_ref), b_ref[...]
        xw_ref[0:HALO, :] = jnp.where(first, 0.0, xb_ref[...].astype(F32))
        for r in range(tm // rs):
            xw_ref[HALO + r * rs:HALO + (r + 1) * rs, :] = x_ref[r * rs:(r + 1) * rs, :].astype(F32)
        for r in range(tm // rs):
            base = HALO + r * rs
            out = bv + wv[taps - 1] * xw_ref[base:base + rs, :]
            for k in range(taps - 1):
                back = taps - 1 - k
                out = out + wv[k] * xw_ref[base - back:base - back + rs, :]
            o_ref[r * rs:(r + 1) * rs, :] = out.astype(o_ref.dtype)

    return pl.pallas_call(
        body, name=name, grid=(ncol // tc, nrow // tm),
        in_specs=[main, before, _col_spec(taps, tc), _col_spec(1, tc)],
        out_specs=main, out_shape=jax.ShapeDtypeStruct((nrow, ncol), BF16),
        scratch_shapes=[pltpu.VMEM((HALO + tm, tc), F32)],
        compiler_params=_params(("parallel", "arbitrary")),
    )(x, x, w, b)


def _conv_bwd(name, dpre, x, w, *, tc):
    nrow, ncol = x.shape
    taps = w.shape[0]
    tm, rs = CONV_TM, CONV_RS
    nsub = tm // rs
    main, before, after = _halo_specs(nrow, tm, tc)

    def fold(v):
        total = v[0:SUBLANES]
        for q in range(1, rs // SUBLANES):
            total = total + v[q * SUBLANES:(q + 1) * SUBLANES]
        return total

    def body(d_ref, da_ref, x_ref, xb_ref, w_ref, dx_ref, dw_ref, db_ref, dwin_ref, xwin_ref):
        i = pl.program_id(1)
        first, last = i == 0, i == pl.num_programs(1) - 1
        wv = _tap_rows(w_ref)

        @pl.when(first)
        def _():
            dw_ref[...] = jnp.zeros_like(dw_ref)
            db_ref[...] = jnp.zeros_like(db_ref)

        xwin_ref[0:HALO, :] = jnp.where(first, 0.0, xb_ref[...].astype(F32))
        dwin_ref[tm:, :] = jnp.where(last, 0.0, da_ref[...].astype(F32))
        for r in range(nsub):
            dwin_ref[r * rs:(r + 1) * rs, :] = d_ref[r * rs:(r + 1) * rs, :].astype(F32)
            xwin_ref[HALO + r * rs:HALO + (r + 1) * rs, :] = x_ref[r * rs:(r + 1) * rs, :].astype(F32)
        dw = [jnp.zeros((SUBLANES, tc), F32)] * taps
        db = jnp.zeros((SUBLANES, tc), F32)
        for r in range(nsub):
            cur = dwin_ref[r * rs:(r + 1) * rs, :]
            dx = wv[taps - 1] * cur
            for k in range(taps - 1):
                ahead = taps - 1 - k
                dx = dx + wv[k] * dwin_ref[r * rs + ahead:(r + 1) * rs + ahead, :]
            dx_ref[r * rs:(r + 1) * rs, :] = dx.astype(dx_ref.dtype)
            for k in range(taps):
                back = taps - 1 - k
                dw[k] = dw[k] + fold(cur * xwin_ref[HALO + r * rs - back:HALO + (r + 1) * rs - back, :])
            db = db + fold(cur)
        for k in range(taps):
            dw_ref[k:k + 1, :] += _colsum(dw[k])
        db_ref[...] += _colsum(db)

    return pl.pallas_call(
        body, name=name, grid=(ncol // tc, nrow // tm),
        in_specs=[main, after, main, before, _col_spec(taps, tc)],
        out_specs=[main, _col_spec(taps, tc), _col_spec(1, tc)],
        out_shape=[jax.ShapeDtypeStruct((nrow, ncol), BF16), jax.ShapeDtypeStruct((taps, ncol), F32),
                   jax.ShapeDtypeStruct((1, ncol), F32)],
        scratch_shapes=[pltpu.VMEM((tm + HALO, tc), F32), pltpu.VMEM((HALO + tm, tc), F32)],
        compiler_params=_params(("parallel", "arbitrary")),
    )(dpre, dpre, x, x, w)


def _glu(gate, val):
    return jax.nn.silu(gate) * val


def _ffn_act_fwd(name, pg, pv, wg, wv, bg, bv, *, tc):
    nrow, ncol = pg.shape
    taps = wg.shape[0]
    tm, rs = CONV_TM, CONV_RS
    main, before, _ = _halo_specs(nrow, tm, tc)

    def body(pg_ref, pgb_ref, pv_ref, pvb_ref, wg_ref, wv_ref, bg_ref, bv_ref, g_ref, v_ref, a_ref, gwin_ref, vwin_ref):
        first = pl.program_id(1) == 0
        taps_g, taps_v, bgv, bvv = _tap_rows(wg_ref), _tap_rows(wv_ref), bg_ref[...], bv_ref[...]
        gwin_ref[0:HALO, :] = jnp.where(first, 0.0, pgb_ref[...].astype(F32))
        vwin_ref[0:HALO, :] = jnp.where(first, 0.0, pvb_ref[...].astype(F32))
        for r in range(tm // rs):
            gwin_ref[HALO + r * rs:HALO + (r + 1) * rs, :] = pg_ref[r * rs:(r + 1) * rs, :].astype(F32)
            vwin_ref[HALO + r * rs:HALO + (r + 1) * rs, :] = pv_ref[r * rs:(r + 1) * rs, :].astype(F32)

        def conv(win_ref, tap_rows, bias, r):
            base = HALO + r * rs
            out = bias + tap_rows[taps - 1] * win_ref[base:base + rs, :]
            for k in range(taps - 1):
                back = taps - 1 - k
                out = out + tap_rows[k] * win_ref[base - back:base - back + rs, :]
            return out

        for r in range(tm // rs):
            sl = slice(r * rs, (r + 1) * rs)
            gate, val = conv(gwin_ref, taps_g, bgv, r), conv(vwin_ref, taps_v, bvv, r)
            g_ref[sl, :] = gate.astype(g_ref.dtype)
            v_ref[sl, :] = val.astype(v_ref.dtype)
            a_ref[sl, :] = _glu(gate, val).astype(a_ref.dtype)

    return pl.pallas_call(
        body, name=name, grid=(ncol // tc, nrow // tm),
        in_specs=[main, before, main, before, _col_spec(taps, tc), _col_spec(taps, tc), _col_spec(1, tc), _col_spec(1, tc)],
        out_specs=[main, main, main],
        out_shape=[jax.ShapeDtypeStruct((nrow, ncol), BF16)] * 3,
        scratch_shapes=[pltpu.VMEM((HALO + tm, tc), F32), pltpu.VMEM((HALO + tm, tc), F32)],
        compiler_params=_params(("parallel", "arbitrary")),
    )(pg, pg, pv, pv, wg, wv, bg, bv)


def _ffn_act_bwd(name, dact, gate, val):
    def fn(dv, gv, vv):
        _, vjp = jax.vjp(_glu, gv, vv)
        dg, dval = vjp(dv)
        return (dg, dval), ()
    width = dact.shape[1]
    return _rows(name, fn, [(dact, width, 0), (gate, width, 0), (val, width, 0)], [],
                 [(width, BF16), (width, BF16)], [], tm=256, rs=2 * SUBLANES)


SSD_TM = 256
SSD_CHUNKS = SSD_TM // CHUNK
X_OFF, B_OFF, C_OFF = 0, SSM_INNER, SSM_INNER + SSM_GROUPS * SSM_STATE
HP = SSM_HPG * SSM_HEAD_DIM


def _causal_tiled():
    row = lax.broadcasted_iota(jnp.int32, (CHUNK, HP), 0)
    src = lax.broadcasted_iota(jnp.int32, (CHUNK, HP), 1) & (CHUNK - 1)
    return src <= row


def _split2(v):
    hi = v.astype(BF16)
    return hi, (v - hi.astype(F32)).astype(BF16)


def _dot_exact(a, ind):
    hi, lo = (lax.dot_general(p, ind, (((1,), (0,)), ((), ())), preferred_element_type=F32) for p in _split2(a))
    return hi + lo


def _head_indicator():
    head = lax.broadcasted_iota(jnp.int32, (SSM_HEADS, SSM_INNER), 0)
    chan = lax.broadcasted_iota(jnp.int32, (SSM_HEADS, SSM_INNER), 1)
    return (chan // SSM_HEAD_DIM == head).astype(BF16)


def _chunk_decays(ci, dt_ref, ac_ref, ind, ax_ref, dtx_ref, eax_ref, eex_ref, tail_ref):
    rows = pl.ds(pl.multiple_of(ci * CHUNK, CHUNK), CHUNK)
    ax_ref[...] = _dot_exact(ac_ref[rows, :], ind)
    dtx_ref[...] = _dot_exact(dt_ref[rows, :], ind)
    eax_ref[...] = jnp.exp(ax_ref[...])
    eex_ref[...] = jnp.exp(ax_ref[CHUNK - 1:CHUNK, :] - ax_ref[...])
    tail = pl.ds(pl.multiple_of(ci * CHUNK + CHUNK - SUBLANES, SUBLANES), SUBLANES)
    tail_ref[...] = jnp.exp(ac_ref[tail, :])


def _group_decay(ci, g, ax_ref, af_ref, xbc_ref, causal):
    gcols = slice(g * HP, (g + 1) * HP)
    bm = xbc_ref[:, B_OFF + g * SSM_STATE:B_OFF + (g + 1) * SSM_STATE]
    cm = xbc_ref[:, C_OFF + g * SSM_STATE:C_OFF + (g + 1) * SSM_STATE]
    cb_tiled = _dot(cm, jnp.concatenate([bm] * SSM_HPG, axis=0), 1, 1)
    seg = ax_ref[:, gcols] - af_ref[ci, :, gcols]
    decay = jnp.where(causal, jnp.exp(jnp.where(causal, seg, 0.0)), 0.0)
    return bm, cm, cb_tiled * decay, decay


def _ssd_fwd(name, pre, dt, a_cum, a_flat, d_x, ind, shards):
    nrow = pre.shape[0]
    tm = SSD_TM
    nstep = nrow // tm
    ng = len(shards)

    def body(pre_ref, dt_ref, ac_ref, af_ref, dx_ref, ind_ref, *rest):
        shard_refs, (y_ref, st_ref), stack_refs = rest[:ng], rest[ng:ng + 2], rest[ng + 2:2 * ng + 2]
        (h_ref, xbc_ref, ax_ref, dtx_ref, eax_ref, eex_ref, m_ref, xd_ref, yd_ref, tail_ref,
         send_sems, recv_sems) = rest[2 * ng + 2:]
        step = pl.program_id(0)
        start, forward, finish = _gather_phases([s.shape[0] for s in shards], ng, shard_refs, stack_refs,
                                                send_sems, recv_sems)

        @pl.when(step == 0)
        def _():
            h_ref[...] = jnp.zeros_like(h_ref)
            start()

        @pl.when(step == nstep // 2)
        def _():
            forward()

        causal = _causal_tiled()
        ind = ind_ref[...]

        def chunk(ci, carry):
            rows = pl.ds(pl.multiple_of(ci * CHUNK, CHUNK), CHUNK)
            xbc_ref[...] = jax.nn.silu(pre_ref[rows, :].astype(F32))
            _chunk_decays(ci, dt_ref, ac_ref, ind, ax_ref, dtx_ref, eax_ref, eex_ref, tail_ref)
            st_ref[ci] = h_ref[...].astype(st_ref.dtype)
            for g in range(SSM_GROUPS):
                gcols = slice(g * HP, (g + 1) * HP)
                bm, cm, m_all, _ = _group_decay(ci, g, ax_ref, af_ref, xbc_ref, causal)
                m_ref[...] = m_all
                x_g = xbc_ref[:, gcols]
                xd = x_g * dtx_ref[:, gcols]
                xd_ref[...] = xd
                h_g = h_ref[gcols, :]
                for hh in range(SSM_HPG):
                    lc = slice(hh * SSM_HEAD_DIM, (hh + 1) * SSM_HEAD_DIM)
                    yd_ref[:, lc] = _dot(m_ref[:, lc], xd_ref[:, lc])
                y_ref[rows, gcols] = (yd_ref[...] + _dot(cm, h_g, 1, 1) * eax_ref[:, gcols]
                                      + dx_ref[:, gcols] * x_g).astype(y_ref.dtype)
                new = _dot(xd * eex_ref[:, gcols], bm, 0, 0)
                for hh in range(SSM_HPG):
                    h = g * SSM_HPG + hh
                    hrows = slice(h * SSM_HEAD_DIM, (h + 1) * SSM_HEAD_DIM)
                    lrows = slice(hh * SSM_HEAD_DIM, (hh + 1) * SSM_HEAD_DIM)
                    h_ref[hrows, :] = tail_ref[SUBLANES - 1:SUBLANES, h:h + 1] * h_ref[hrows, :] + new[lrows, :]
            return carry

        lax.fori_loop(0, SSD_CHUNKS, chunk, 0)

        @pl.when(step == nstep - 1)
        def _():
            finish()

    nchunk = nrow // CHUNK
    whole = lambda a: pl.BlockSpec(a.shape, lambda i, nd=a.ndim: (0,) * nd)
    hbm = pl.BlockSpec(memory_space=pl.ANY)
    wide = lambda: pltpu.VMEM((CHUNK, SSM_INNER), F32)
    group = lambda: pltpu.VMEM((CHUNK, HP), F32)
    res = pl.pallas_call(
        body, name=name, grid=(nstep,),
        in_specs=[pl.BlockSpec((tm, SSM_XBC), lambda i: (i, 0)), pl.BlockSpec((tm, SSM_HEADS), lambda i: (i, 0)),
                  pl.BlockSpec((tm, SSM_HEADS), lambda i: (i, 0)),
                  pl.BlockSpec((SSD_CHUNKS, 1, SSM_INNER), lambda i: (i, 0, 0)), whole(d_x), whole(ind)] + [hbm] * ng,
        out_specs=[pl.BlockSpec((tm, SSM_INNER), lambda i: (i, 0)),
                   pl.BlockSpec((SSD_CHUNKS, SSM_INNER, SSM_STATE), lambda i: (i, 0, 0))] + [hbm] * ng,
        out_shape=[jax.ShapeDtypeStruct((nrow, SSM_INNER), BF16),
                   jax.ShapeDtypeStruct((nchunk, SSM_INNER, SSM_STATE), BF16)]
        + [jax.ShapeDtypeStruct((N_CHIPS,) + s.shape, s.dtype) for s in shards],
        scratch_shapes=[pltpu.VMEM((SSM_INNER, SSM_STATE), F32), pltpu.VMEM((CHUNK, SSM_XBC), F32),
                        wide(), wide(), wide(), wide(), group(), group(), group(),
                        pltpu.VMEM((SUBLANES, SSM_HEADS), F32)] + _exchange_scratch(ng, GATHER_SEMS),
        compiler_params=_params(("arbitrary",)),
    )(pre, dt, a_cum, a_flat, d_x, ind, *shards)
    return res[0], res[1], res[2:]


def _ssd_bwd(name, pre, dt, a_cum, a_flat, d_x, ind, ind_t, states, dy, pairs):
    nrow = pre.shape[0]
    tm = SSD_TM
    ntile = nrow // tm
    npair = len(pairs)

    def body(pre_ref, dt_ref, ac_ref, af_ref, dx_ref, ind_ref, indt_ref, st_ref, dy_ref, *rest):
        pair_refs = rest[:npair]
        dpre_ref, ddt_ref, da_ref, daf_ref, dd_ref = rest[npair:npair + 5]
        recv_refs = rest[npair + 5:2 * npair + 5]
        (dh_ref, xbc_ref, dxbc_ref, ax_ref, dtx_ref, eax_ref, eex_ref, red_ref,
         m_ref, l_ref, xd_ref, dm_ref, dxd_ref, fold_ref, hd_ref, tail_ref, send_sems, recv_sems) = rest[2 * npair + 5:]
        start, finish = _scatter_phases(pair_refs, recv_refs, send_sems, recv_sems)

        @pl.when(pl.program_id(0) == 0)
        def _():
            dh_ref[...] = jnp.zeros_like(dh_ref)
            dd_ref[...] = jnp.zeros_like(dd_ref)
            start()

        causal = _causal_tiled()
        ind, ind_t = ind_ref[...], indt_ref[...]
        is_last_row = lax.broadcasted_iota(jnp.int32, (CHUNK, 1), 0) == CHUNK - 1
        ones = jnp.ones((CHUNK, SSM_STATE), BF16)

        def chunk(k, ddx):
            ci = SSD_CHUNKS - 1 - k
            rows = pl.ds(pl.multiple_of(ci * CHUNK, CHUNK), CHUNK)
            pre_v = pre_ref[rows, :].astype(F32)
            xbc_ref[...] = jax.nn.silu(pre_v)
            _chunk_decays(ci, dt_ref, ac_ref, ind, ax_ref, dtx_ref, eax_ref, eex_ref, tail_ref)
            ddx_parts = []
            for g in range(SSM_GROUPS):
                gcols = slice(g * HP, (g + 1) * HP)
                bcols = slice(B_OFF + g * SSM_STATE, B_OFF + (g + 1) * SSM_STATE)
                ccols = slice(C_OFF + g * SSM_STATE, C_OFF + (g + 1) * SSM_STATE)
                bm, cm, m_all, decay = _group_decay(ci, g, ax_ref, af_ref, xbc_ref, causal)
                m_ref[...] = m_all
                l_ref[...] = decay
                x_g = xbc_ref[:, gcols]
                xd = x_g * dtx_ref[:, gcols]
                xd_ref[...] = xd
                h_g = st_ref[ci, gcols, :]
                dh_g = dh_ref[gcols, :]
                dy_g = dy_ref[rows, gcols]
                for hh in range(SSM_HPG):
                    h = g * SSM_HPG + hh
                    hcols = slice(h * SSM_HEAD_DIM, (h + 1) * SSM_HEAD_DIM)
                    lc = slice(hh * SSM_HEAD_DIM, (hh + 1) * SSM_HEAD_DIM)
                    dy_h = dy_ref[rows, hcols]
                    dm_ref[:, lc] = _dot(dy_h, xd_ref[:, lc], 1, 1)
                    dxd_ref[:, lc] = _dot(m_ref[:, lc], dy_h, 0, 0)
                ebdh = eex_ref[:, gcols] * _dot(bm, dh_g, 1, 1)
                dxd = dxd_ref[...] + ebdh
                dm = dm_ref[...]
                t = dm * l_ref[...]
                t128 = (t[:, 0:LANES] + t[:, LANES:2 * LANES]) + (t[:, 2 * LANES:3 * LANES] + t[:, 3 * LANES:])
                fold_ref[...] = t128 + pltpu.roll(t128, CHUNK, axis=1)
                dw_sum = fold_ref[:, 0:CHUNK]
                q = dm * m_ref[...]
                dyea = dy_g * eax_ref[:, gcols]
                red_ref[0:CHUNK, gcols] = q + dyea * _dot(cm, h_g, 1, 1)
                red_ref[CHUNK:2 * CHUNK, gcols] = xd * ebdh
                red_ref[2 * CHUNK:3 * CHUNK, gcols] = dxd * x_g
                daf_ref[ci, :, gcols] = -jnp.sum(q, axis=0, keepdims=True)
                ddx_parts.append(jnp.sum(dy_g * x_g, axis=0, keepdims=True))
                dxbc_ref[:, gcols] = dxd * dtx_ref[:, gcols] + dx_ref[:, gcols] * dy_g
                dxbc_ref[:, ccols] = _dot(dw_sum, bm) + _dot(dyea, h_g)
                dxbc_ref[:, bcols] = _dot(dw_sum, cm, 0, 0) + _dot(xd * eex_ref[:, gcols], dh_g)
                dh_new = _dot(dyea, cm, 0, 0)
                for hh in range(SSM_HPG):
                    h = g * SSM_HPG + hh
                    hrows = slice(h * SSM_HEAD_DIM, (h + 1) * SSM_HEAD_DIM)
                    lrows = slice(hh * SSM_HEAD_DIM, (hh + 1) * SSM_HEAD_DIM)
                    hd_ref[h:h + 1, :] = jnp.sum(st_ref[ci, hrows, :] * dh_ref[hrows, :], axis=0, keepdims=True)
                    dh_ref[hrows, :] = tail_ref[SUBLANES - 1:SUBLANES, h:h + 1] * dh_ref[hrows, :] + dh_new[lrows, :]
            sums = _dot_exact(red_ref[...], ind_t)
            ra, ts = sums[:CHUNK], sums[CHUNK:2 * CHUNK]
            hdh = sum(lax.dot_general(ones, p, (((1,), (1,)), ((), ())), preferred_element_type=F32)
                      for p in _split2(hd_ref[...]))
            da_last = jnp.sum(ts, axis=0, keepdims=True) + tail_ref[SUBLANES - 1:SUBLANES, :] * hdh
            da_ref[rows, :] = ra - ts + jnp.where(is_last_row, da_last, 0.0)
            ddt_ref[rows, :] = sums[2 * CHUNK:]
            sig = jax.nn.sigmoid(pre_v)
            dpre_ref[rows, :] = (dxbc_ref[...] * (sig * (1.0 + pre_v * (1.0 - sig)))).astype(dpre_ref.dtype)
            return ddx + jnp.concatenate(ddx_parts, axis=1)

        ddx = lax.fori_loop(0, SSD_CHUNKS, chunk, jnp.zeros((1, SSM_INNER), F32))
        dd_ref[...] += _dot_exact(jnp.broadcast_to(ddx, (SUBLANES, SSM_INNER)), ind_t)

        @pl.when(pl.program_id(0) == ntile - 1)
        def _():
            finish()

    rev = lambda i: ntile - 1 - i
    whole = lambda a: pl.BlockSpec(a.shape, lambda i, nd=a.ndim: (0,) * nd)
    hbm = pl.BlockSpec(memory_space=pl.ANY)
    wide = lambda: pltpu.VMEM((CHUNK, SSM_INNER), F32)
    group = lambda: pltpu.VMEM((CHUNK, HP), F32)
    res = pl.pallas_call(
        body, name=name, grid=(ntile,),
        in_specs=[pl.BlockSpec((tm, SSM_XBC), lambda i: (rev(i), 0)), pl.BlockSpec((tm, SSM_HEADS), lambda i: (rev(i), 0)),
                  pl.BlockSpec((tm, SSM_HEADS), lambda i: (rev(i), 0)),
                  pl.BlockSpec((SSD_CHUNKS, 1, SSM_INNER), lambda i: (rev(i), 0, 0)),
                  whole(d_x), whole(ind), whole(ind_t),
                  pl.BlockSpec((SSD_CHUNKS, SSM_INNER, SSM_STATE), lambda i: (rev(i), 0, 0)),
                  pl.BlockSpec((tm, SSM_INNER), lambda i: (rev(i), 0))] + [hbm] * npair,
        out_specs=[pl.BlockSpec((tm, SSM_XBC), lambda i: (rev(i), 0)), pl.BlockSpec((tm, SSM_HEADS), lambda i: (rev(i), 0)),
                   pl.BlockSpec((tm, SSM_HEADS), lambda i: (rev(i), 0)),
                   pl.BlockSpec((SSD_CHUNKS, 1, SSM_INNER), lambda i: (rev(i), 0, 0)),
                   pl.BlockSpec((SUBLANES, SSM_HEADS), lambda i: (0, 0))] + [hbm] * npair,
        out_shape=[jax.ShapeDtypeStruct((nrow, SSM_XBC), BF16), jax.ShapeDtypeStruct((nrow, SSM_HEADS), F32),
                   jax.ShapeDtypeStruct((nrow, SSM_HEADS), F32), jax.ShapeDtypeStruct((nrow // CHUNK, 1, SSM_INNER), F32),
                   jax.ShapeDtypeStruct((SUBLANES, SSM_HEADS), F32)]
        + [jax.ShapeDtypeStruct(p.shape, p.dtype) for p in pairs],
        scratch_shapes=[pltpu.VMEM((SSM_INNER, SSM_STATE), F32), pltpu.VMEM((CHUNK, SSM_XBC), F32),
                        pltpu.VMEM((CHUNK, SSM_XBC), F32), wide(), wide(), wide(), wide(),
                        pltpu.VMEM((3 * CHUNK, SSM_INNER), F32),
                        group(), group(), group(), group(), group(), pltpu.VMEM((CHUNK, LANES), F32),
                        pltpu.VMEM((SSM_HEADS, SSM_STATE), F32), pltpu.VMEM((SUBLANES, SSM_HEADS), F32)]
        + _exchange_scratch(npair, N_CHIPS - 1),
        compiler_params=_params(("arbitrary",)),
    )(pre, dt, a_cum, a_flat, d_x, ind, ind_t, states, dy, *pairs)
    return res[:5], res[5:]


LATE = ["w_proj_a", "w_proj_b", "w_out", "ffn_w_up", "ffn_w_down"]
HALF_TILES = {"w_in": 128, "w_proj_a": 128, "w_proj_b": 256, "w_out": 128, "ffn_w_up": 128, "ffn_w_down": 176}


def _late_weights(stacks, shards):
    pa, pb, out, up, down = [_own_slot(stack, own) for stack, own in zip(stacks, shards)]
    return {"w_proj_a": pa.reshape(-1, D_MODEL), "w_proj_b": pb.reshape(-1, D_MODEL), "w_out": out.reshape(-1, D_MODEL),
            "w_up_g": _columns_from_chips(up[:2]), "w_up_v": _columns_from_chips(up[2:]),
            "w_down": down.reshape(-1, D_MODEL)}


def _pair_reduce(tag, names, stacks):
    core = lax.axis_index("c")
    own_half = [_row_half(s, core, 1) for s in stacks]
    other_half = _swap_cores("pair_grads_" + tag, [_row_half(s, 1 - core, 1) for s in stacks])
    return [_pair_sum("pair_" + n, a, b, tm=HALF_TILES[n]) for n, a, b in zip(names, own_half, other_half)]


def _local_step(x, target, w, late_shards):
    w = dict(w)
    g = {}
    bs_col = w["gmlp_bs"].reshape(GMLP_GROUPS, GMLP_BLOCK, 1)
    b0, b1 = w["gate_bias"][0:1], w["gate_bias"][1:2]

    xn = _rms_fwd("mix_norm", x, w["mix_norm_w"])
    big = dict(bm=1024, bn=1024, bk=1024)
    act16 = dict(out_dtype=BF16, **big)
    gates = _mm("in_gates", xn, w["w_g"], **act16)
    za = _mm("in_gmlp", xn, w["w_za"], **act16)
    z = _mm("in_z", xn, w["w_z"], **act16)
    xbc = _mm("in_xbc", xn, w["w_xbc"], **act16)
    dt_raw = _mm("in_dt", xn, w["w_dt"], bm=1024, bn=SSM_HEADS, bk=1024)

    pre = _conv_fwd("ssm_conv_fwd", xbc, w["ssm_conv_w"], w["ssm_conv_b"], tc=1024)
    dt, a_cum = _dt_prep("dt_prep", dt_raw, w["ssm_dt_bias"], w["ssm_a_log"])
    a_flat = jnp.transpose(a_cum.reshape(-1, CHUNK, SSM_HEADS), (0, 2, 1)).reshape(-1, 1, SSM_INNER)
    d_x = jnp.repeat(w["ssm_d"], SSM_HEAD_DIM, axis=1)
    ind = _head_indicator()
    y_ssd, states, late_stacks = _ssd_fwd("ssd_fwd", pre, dt, a_cum, a_flat, d_x, ind, late_shards)
    w.update(_late_weights(late_stacks, late_shards))
    yb_pre = _gate_norm_fwd("gate_norm_fwd", y_ssd, z, w["ssm_norm_w"])
    y_b = _mm("proj_b", yb_pre, w["w_proj_b"], bm=1024, bn=1024, bk=SSM_INNER, out_dtype=BF16)

    ya_pre = _gmlp_fwd("gmlp_fwd", za, w["gmlp_ln_w"], w["gmlp_ln_b"], w["gmlp_ws"], bs_col)
    y_a = _mm("proj_a", ya_pre, w["w_proj_a"], **act16)

    merged = _merge_fwd("merge_fwd", gates, y_a, y_b, b0, b1)
    h1 = _mm("out_proj", merged, w["w_out"], res=x, **big)

    hn = _rms_fwd("ffn_norm", h1, w["ffn_norm_w"])
    half = dict(bm=1024, bn=D_FF // 2, bk=1024, out_dtype=BF16)
    pg = _mm("ffn_up_gate", hn, w["w_up_g"], **half)
    pv = _mm("ffn_up_val", hn, w["w_up_v"], **half)
    cw, cb = w["ffn_conv_w"], w["ffn_conv_b"]
    gate, val, act = _ffn_act_fwd("ffn_act_fwd", pg, pv, cw[:, :D_FF], cw[:, D_FF:], cb[:, :D_FF], cb[:, D_FF:],
                                  tc=D_FF // 2)
    h2 = _mm("ffn_down", act, w["w_down"], res=h1, bm=1024, bn=1024, bk=D_FF // 2)

    dh2, loss_part, g["final_norm_w"] = _final_loss("final_loss", h2, target, w["final_norm_w"].reshape(1, D_MODEL))

    dact = _mm("d_act", dh2, w["w_down"], tb=True, **half)
    wgrad = dict(ta=True, bk=min(2048, x.shape[0]), out_dtype=BF16)
    g["w_down"] = _mm("dw_down", act, dh2, bm=D_FF // 2, bn=1024, **wgrad)
    dgate, dval = _ffn_act_bwd("ffn_act_bwd", dact, gate, val)
    dpg, dcwg, dcbg = _conv_bwd("ffn_conv_bwd_gate", dgate, pg, cw[:, :D_FF], tc=D_FF // 2)
    dpv, dcwv, dcbv = _conv_bwd("ffn_conv_bwd_val", dval, pv, cw[:, D_FF:], tc=D_FF // 2)
    g["ffn_conv_w"] = jnp.concatenate([dcwg, dcwv], axis=1)
    g["ffn_conv_b"] = jnp.concatenate([dcbg, dcbv], axis=1)
    dhn = _mm_sum("d_hn", [(dpg, w["w_up_g"]), (dpv, w["w_up_v"])], bm=1024, bk=D_FF // 2)
    g["w_up_g"] = _mm("dw_up_gate", hn, dpg, bm=1024, bn=D_FF // 2, **wgrad)
    g["w_up_v"] = _mm("dw_up_val", hn, dpv, bm=1024, bn=D_FF // 2, **wgrad)
    dh1, g["ffn_norm_w"] = _rms_bwd("ffn_norm_bwd", h1, w["ffn_norm_w"], dhn, dh2)

    dmerged = _mm("d_merged", dh1, w["w_out"], tb=True, **act16)
    g["w_out"] = _mm("dw_out", merged, dh1, bm=1024, bn=1024, **wgrad)
    dgates, dya, dyb, db0, db1 = _merge_bwd("merge_bwd", gates, y_a, y_b, dmerged, b0, b1)
    g["gate_bias"] = jnp.concatenate([db0, db1], axis=0)

    dya_pre = _mm("d_ya_pre", dya, w["w_proj_a"], tb=True, **act16)
    g["w_proj_a"] = _mm("dw_proj_a", ya_pre, dya, bm=1024, bn=1024, **wgrad)
    dyb_pre = _mm("d_yb_pre", dyb, w["w_proj_b"], tb=True, **act16)
    g["w_proj_b"] = _mm("dw_proj_b", yb_pre, dyb, bm=1024, bn=1024, **wgrad)
    late_pairs = _pair_reduce("late", LATE, [
        g["w_proj_a"].reshape(N_CHIPS, -1, D_MODEL), g["w_proj_b"].reshape(N_CHIPS, -1, D_MODEL),
        g["w_out"].reshape(N_CHIPS, -1, D_MODEL),
        jnp.concatenate([_columns_to_chips(g["w_up_g"], 2), _columns_to_chips(g["w_up_v"], 2)], axis=0),
        g["w_down"].reshape(N_CHIPS, -1, D_MODEL)])

    dy_ssd, dz, g["ssm_norm_w"] = _gate_norm_bwd("gate_norm_bwd", y_ssd, z, dyb_pre, w["ssm_norm_w"])
    (dpre, ddt, da_tok, da_flat, dd), late_received = _ssd_bwd(
        "ssd_bwd", pre, dt, a_cum, a_flat, d_x, ind, ind.T, states, dy_ssd, late_pairs)
    g["ssm_d"] = dd[0:1]
    da_src = jnp.transpose(da_flat.reshape(-1, SSM_HEADS, CHUNK), (0, 2, 1)).reshape(-1, SSM_HEADS)
    ddt_raw, g["ssm_dt_bias"], g["ssm_a_log"] = _dt_bwd("dt_bwd", dt_raw, ddt, da_tok, da_src,
                                                         w["ssm_dt_bias"], w["ssm_a_log"])
    dxbc, g["ssm_conv_w"], g["ssm_conv_b"] = _conv_bwd("ssm_conv_bwd", dpre, xbc, w["ssm_conv_w"], tc=1024)

    dza, g["gmlp_ln_w"], g["gmlp_ln_b"], g["gmlp_ws"], dbs = _gmlp_bwd(
        "gmlp_bwd", za, dya_pre, w["gmlp_ln_w"], w["gmlp_ln_b"], w["gmlp_ws"], bs_col)
    g["gmlp_bs"] = dbs.reshape(GMLP_GROUPS, GMLP_BLOCK)

    dw_in = jnp.concatenate([
        _mm("dw_gates", xn, dgates, bm=1024, bn=1024, **wgrad), _mm("dw_gmlp", xn, dza, bm=1024, bn=1024, **wgrad),
        _mm("dw_z", xn, dz, bm=1024, bn=1024, **wgrad), _mm("dw_xbc", xn, dxbc, bm=1024, bn=1024, **wgrad),
        _mm("dw_dt", xn, ddt_raw, bm=1024, bn=SSM_HEADS, **wgrad)], axis=1)
    in_pairs = _pair_reduce("in", ["w_in"], [_columns_to_chips(dw_in)])
    dxn, in_received = _mm_sum("d_xn", [(dgates, w["w_g"]), (dza, w["w_za"]), (dz, w["w_z"]), (dxbc, w["w_xbc"]),
                                        (ddt_raw, w["w_dt"])], bm=1024, bk=1024, exchange=in_pairs)
    grad_x, g["mix_norm_w"] = _rms_bwd("mix_norm_bwd", x, w["mix_norm_w"], dxn, dh1)
    return loss_part, grad_x, g, in_pairs + list(late_pairs), list(in_received) + list(late_received)


def _position():
    return lax.axis_index("x"), lax.axis_index("y"), lax.axis_index("c")


def _own_slot(stack, own):
    chip = 2 * lax.axis_index("x") + lax.axis_index("y")
    return lax.dynamic_update_index_in_dim(stack, own, chip, axis=0)


def _scatter_phases(ins, outs, send_sems, recv_sems):
    n = len(ins)
    x, y, c = _position()
    me = 2 * x + y
    peers = [(1 - x, y), (x, 1 - y), (1 - x, 1 - y)]

    def copy(i, k, src_slot, dst_slot):
        px, py = peers[k]
        return pltpu.make_async_remote_copy(
            src_ref=ins[i].at[src_slot], dst_ref=outs[i].at[dst_slot],
            send_sem=send_sems.at[i, k], recv_sem=recv_sems.at[i, k],
            device_id=(px, py, c), device_id_type=MESH)

    def start():
        for i in range(n):
            for k, (px, py) in enumerate(peers):
                copy(i, k, 2 * px + py, me).start()

    def finish():
        for i in range(n):
            for k, (px, py) in enumerate(peers):
                copy(i, k, me, 2 * px + py).wait_recv()
        for i in range(n):
            for k, (px, py) in enumerate(peers):
                copy(i, k, 2 * px + py, me).wait_send()

    return start, finish


def _exchange_scratch(n, per_array):
    return [pltpu.SemaphoreType.DMA((n, per_array)), pltpu.SemaphoreType.DMA((n, per_array))]


def _half_rows(ref_rows, which):
    half = ref_rows // 2
    return pl.ds(pl.multiple_of(which * half, 2 * SUBLANES), half)


GATHER_SEMS = 2 * (N_CHIPS - 1)


def _gather_phases(nrows, ns, ins, outs, send_sems, recv_sems):
    n = len(ins)
    x, y, c = _position()
    me = 2 * x + y
    sibling = (x, y, 1 - c)
    chips = [(1 - x, y), (x, 1 - y), (1 - x, 1 - y)]

    def remote(i, k, src, dst, to):
        return pltpu.make_async_remote_copy(src_ref=src, dst_ref=dst, send_sem=send_sems.at[i, k],
                                            recv_sem=recv_sems.at[i, k], device_id=to, device_id_type=MESH)

    def over_ici(i, k):
        px, py = chips[k]
        rows = _half_rows(nrows[i], c) if i < ns else slice(None)
        return remote(i, k, ins[i].at[rows], outs[i].at[me, rows], (px, py, c))

    def landed(i, k, which):
        px, py = chips[k]
        return outs[i].at[2 * px + py, _half_rows(nrows[i], which)] if i < ns else outs[i].at[2 * px + py]

    def start():
        for i in range(n):
            for k in range(N_CHIPS - 1):
                over_ici(i, k).start()

    def forward():
        for i in range(n):
            for k in range(N_CHIPS - 1):
                piece = landed(i, k, c)
                remote(i, k, piece, piece, (*chips[k], c)).wait_recv()
                if i < ns:
                    remote(i, N_CHIPS - 1 + k, piece, piece, sibling).start()

    def finish():
        for i in range(ns):
            for k in range(N_CHIPS - 1):
                piece = landed(i, k, 1 - c)
                remote(i, N_CHIPS - 1 + k, piece, piece, sibling).wait_recv()
        for i in range(n):
            for k in range(N_CHIPS - 1):
                over_ici(i, k).wait_send()
                if i < ns:
                    piece = landed(i, k, c)
                    remote(i, N_CHIPS - 1 + k, piece, piece, sibling).wait_send()

    return start, forward, finish


def _gather_chips_split(name, split, whole):
    arrs = list(split) + list(whole)
    n = len(arrs)

    def body(*refs):
        phases = _gather_phases([a.shape[0] for a in arrs], len(split), refs[:n], refs[n:2 * n], *refs[2 * n:])
        for phase in phases:
            phase()

    hbm = pl.BlockSpec(memory_space=pl.ANY)
    return pl.pallas_call(
        body, name=name, in_specs=[hbm] * n, out_specs=[hbm] * n,
        out_shape=[jax.ShapeDtypeStruct((N_CHIPS,) + a.shape, a.dtype) for a in arrs],
        scratch_shapes=_exchange_scratch(n, GATHER_SEMS),
        compiler_params=pltpu.CompilerParams(has_side_effects=True),
    )(*arrs)


def _swap_cores(name, arrs):
    n = len(arrs)

    def body(*refs):
        ins, outs = refs[:n], refs[n:2 * n]
        send_sems, recv_sems = refs[2 * n:]
        x, y, c = _position()
        copies = [pltpu.make_async_remote_copy(src_ref=ins[i], dst_ref=outs[i], send_sem=send_sems.at[i],
                                               recv_sem=recv_sems.at[i], device_id=(x, y, 1 - c), device_id_type=MESH)
                  for i in range(n)]
        for cp in copies:
            cp.start()
        for cp in copies:
            cp.wait_recv()
        for cp in copies:
            cp.wait_send()

    hbm = pl.BlockSpec(memory_space=pl.ANY)
    return pl.pallas_call(
        body, name=name, in_specs=[hbm] * n, out_specs=[hbm] * n,
        out_shape=[jax.ShapeDtypeStruct(a.shape, a.dtype) for a in arrs],
        scratch_shapes=[pltpu.SemaphoreType.DMA((n,)), pltpu.SemaphoreType.DMA((n,))],
        compiler_params=pltpu.CompilerParams(has_side_effects=True),
    )(*arrs)


def _row_half(a, which, axis):
    half = a.shape[axis] // 2
    return lax.dynamic_slice_in_dim(a, which * half, half, axis=axis)


def _all_reduce(name, pack):
    def body(in_ref, out_ref, buf, send_sems, recv_sems):
        x, y, c = _position()
        me = 4 * x + 2 * y + c
        flips = [(dx, dy, dc) for dx in (0, 1) for dy in (0, 1) for dc in (0, 1) if (dx, dy, dc) != (0, 0, 0)]
        peers = [((1 - x) if dx else x, (1 - y) if dy else y, (1 - c) if dc else c) for dx, dy, dc in flips]
        buf[me] = in_ref[...]
        sends = []
        for k, peer in enumerate(peers):
            cp = pltpu.make_async_remote_copy(src_ref=in_ref, dst_ref=buf.at[me], send_sem=send_sems.at[k],
                                              recv_sem=recv_sems.at[k], device_id=peer, device_id_type=MESH)
            cp.start()
            sends.append(cp)
        for k, (px, py, pc) in enumerate(peers):
            pltpu.make_async_remote_copy(src_ref=in_ref, dst_ref=buf.at[4 * px + 2 * py + pc], send_sem=send_sems.at[k],
                                         recv_sem=recv_sems.at[k], device_id=(px, py, pc), device_id_type=MESH).wait_recv()
        total = buf[0]
        for j in range(1, N_DEV):
            total = total + buf[j]
        out_ref[...] = total
        for cp in sends:
            cp.wait_send()

    vmem = pl.BlockSpec(memory_space=pltpu.VMEM)
    return pl.pallas_call(
        body, name=name, in_specs=[vmem], out_specs=vmem,
        out_shape=jax.ShapeDtypeStruct(pack.shape, F32),
        scratch_shapes=[pltpu.VMEM((N_DEV,) + pack.shape, F32), pltpu.SemaphoreType.DMA((N_DEV - 1,)),
                        pltpu.SemaphoreType.DMA((N_DEV - 1,))],
        compiler_params=pltpu.CompilerParams(has_side_effects=True, vmem_limit_bytes=VMEM_LIMIT_V7X),
    )(pack)


def _pack(arrs):
    rows = [a.reshape(-1, LANES) for a in arrs]
    total = sum(r.shape[0] for r in rows)
    rows.append(jnp.zeros((-total % SUBLANES, LANES), F32))
    return jnp.concatenate(rows, axis=0)


def _unpack(pack, shapes):
    out, off = [], 0
    for s in shapes:
        nrow = 1
        for d in s:
            nrow *= d
        nrow //= LANES
        out.append(pack[off:off + nrow].reshape(s))
        off += nrow
    return out


SMALL = ["mix_norm_w", "gate_bias", "gmlp_ln_w", "gmlp_ln_b", "gmlp_ws", "gmlp_bs", "ssm_conv_w", "ssm_conv_b",
         "ssm_dt_bias", "ssm_a_log", "ssm_d", "ssm_norm_w", "ffn_norm_w", "ffn_conv_w", "ffn_conv_b", "final_norm_w"]
SMALL_SHARDED = ("gate_bias", "ssm_conv_w", "ffn_conv_w")
BIG = ["w_in", "w_proj_a", "w_proj_b", "w_out", "ffn_w_up", "ffn_w_down"]
WEIGHTS = ["mix_norm_w", "w_in", "gate_bias", "gmlp_ln_w", "gmlp_ln_b", "gmlp_ws", "gmlp_bs", "ssm_conv_w",
           "ssm_conv_b", "ssm_dt_bias", "ssm_a_log", "ssm_d", "ssm_norm_w", "w_proj_a", "w_proj_b", "w_out",
           "ffn_norm_w", "ffn_w_up", "ffn_conv_w", "ffn_conv_b", "ffn_w_down", "final_norm_w"]
IN_SPLITS = [0, 2048, 4096, 6144, 9216, 9248]


def _columns_from_chips(stack):
    return jnp.transpose(stack, (1, 0, 2)).reshape(stack.shape[1], -1)


def _columns_to_chips(full, parts=N_CHIPS):
    rows, cols = full.shape
    return jnp.transpose(full.reshape(rows, parts, cols // parts), (1, 0, 2))


def kernel(x, mix_norm_w, w_in, gate_bias, gmlp_ln_w, gmlp_ln_b, gmlp_ws, gmlp_bs, ssm_conv_w, ssm_conv_b, ssm_dt_bias, ssm_a_log, ssm_d, ssm_norm_w, w_proj_a, w_proj_b, w_out, ffn_norm_w, ffn_w_up, ffn_conv_w, ffn_conv_b, ffn_w_down, final_norm_w, loss_target, m_mix_norm_w, m_w_in, m_gate_bias, m_gmlp_ln_w, m_gmlp_ln_b, m_gmlp_ws, m_gmlp_bs, m_ssm_conv_w, m_ssm_conv_b, m_ssm_dt_bias, m_ssm_a_log, m_ssm_d, m_ssm_norm_w, m_w_proj_a, m_w_proj_b, m_w_out, m_ffn_norm_w, m_ffn_w_up, m_ffn_conv_w, m_ffn_conv_b, m_ffn_w_down, m_final_norm_w, v_mix_norm_w, v_w_in, v_gate_bias, v_gmlp_ln_w, v_gmlp_ln_b, v_gmlp_ws, v_gmlp_bs, v_ssm_conv_w, v_ssm_conv_b, v_ssm_dt_bias, v_ssm_a_log, v_ssm_d, v_ssm_norm_w, v_w_proj_a, v_w_proj_b, v_w_out, v_ffn_norm_w, v_ffn_w_up, v_ffn_conv_w, v_ffn_conv_b, v_ffn_w_down, v_final_norm_w):
    args = dict(locals())
    weights = {n: args[n] for n in WEIGHTS}
    moments_m = {n: args["m_" + n] for n in WEIGHTS}
    moments_v = {n: args["v_" + n] for n in WEIGHTS}
    chip = 2 * lax.axis_index("x") + lax.axis_index("y")

    shards = [weights["w_in"][0].astype(BF16)] + [weights[n][0] for n in SMALL_SHARDED]
    gathered = _gather_chips_split("gather_weights", shards[:1], shards[1:])
    w_in_s, gb_s, scw_s, fcw_s = [_own_slot(stack, own) for stack, own in zip(gathered, shards)]
    late_shards = [weights[n][0].astype(BF16) for n in LATE]
    w_in_full = _columns_from_chips(w_in_s)
    full = {"w_" + nm: w_in_full[:, IN_SPLITS[k]:IN_SPLITS[k + 1]] for k, nm in enumerate(["g", "za", "z", "xbc", "dt"])}
    full["gate_bias"] = _columns_from_chips(gb_s)
    full["ssm_conv_w"] = _columns_from_chips(scw_s)
    full["ffn_conv_w"] = _columns_from_chips(fcw_s)
    for n in SMALL:
        if n not in SMALL_SHARDED:
            full[n] = weights[n] if n == "final_norm_w" else weights[n][0]
    for n in ("mix_norm_w", "ffn_norm_w", "ssm_conv_b", "ssm_dt_bias", "ssm_a_log", "ssm_d", "ssm_norm_w", "ffn_conv_b"):
        full[n] = full[n].reshape(1, -1)

    loss_part, grad_x, g, pair, received = _local_step(x[0], loss_target[0], full, late_shards)

    per_head = ["ssm_dt_bias", "ssm_a_log", "ssm_d"]
    rest = [n for n in SMALL if n not in per_head]
    head_row = jnp.concatenate([g[n] for n in per_head] + [jnp.zeros((1, LANES - 3 * SSM_HEADS), F32)], axis=1)
    pack = _pack([loss_part, head_row] + [g[n] for n in rest])
    reduced = _unpack(_all_reduce("reduce_small", pack), [(1, LANES), (1, LANES)] + [g[n].shape for n in rest])
    loss = reduced[0][0, 0]
    small_grads = {n: reduced[1][:, k * SSM_HEADS:(k + 1) * SSM_HEADS] for k, n in enumerate(per_head)}
    for n, r in zip(rest, reduced[2:]):
        if n in SMALL_SHARDED:
            width = weights[n].shape[2]
            r = lax.dynamic_slice_in_dim(r, chip * width, width, axis=1)
        small_grads[n] = r
    two_d = lambda a: a.reshape(-1, a.shape[-1])
    upd = _adamw_small("adamw_small", *[[two_d(d[n]) for n in SMALL]
                                        for d in (weights, small_grads, moments_m, moments_v)])
    small_out = [[small_grads[n] for n in SMALL]] + list(upd)
    small_out = [[a.reshape(weights[n].shape) for n, a in zip(SMALL, kind)] for kind in small_out]

    received = [_own_slot(r, lax.dynamic_index_in_dim(p, chip, 0, keepdims=False)) for r, p in zip(received, pair)]
    halves = [_sum_slots("sum_" + n, r, tm=HALF_TILES[n], rs=2 * SUBLANES) for n, r in zip(BIG, received)]
    tiles = {"w_in": 128, "w_proj_a": 256, "w_proj_b": 256, "w_out": 256, "ffn_w_up": 128, "ffn_w_down": 176}
    core = lax.axis_index("c")
    other = _swap_cores("join_grads", halves)
    reduced = [jnp.concatenate([jnp.where(core == 0, a, b), jnp.where(core == 0, b, a)], axis=0)
               for a, b in zip(halves, other)]
    big_out = {}
    for n, grad in zip(BIG, reduced):
        big_out[n] = _adamw("adamw_" + n, weights[n][0], grad, moments_m[n][0], moments_v[n][0],
                            tm=tiles[n], rs=SUBLANES)

    per_kind = [[], [], [], []]
    for n in WEIGHTS:
        for kind in range(4):
            if n in big_out:
                per_kind[kind].append(big_out[n][kind].reshape(weights[n].shape))
            else:
                per_kind[kind].append(small_out[kind][SMALL.index(n)])
    return (loss, grad_x[None], *per_kind[0], *per_kind[1], *per_kind[2], *per_kind[3])
```

```python
import jax
import jax.numpy as jnp
from jax import lax
from jax.experimental import pallas as pl
from jax.experimental.pallas import tpu as pltpu

F32 = jnp.float32
BF16 = jnp.bfloat16
MESH = pl.DeviceIdType.MESH

EPS = 1e-5
D_MODEL = 1024
GMLP_BLOCK = 128
GMLP_GROUPS = 8
CHUNK = 64
SSM_INNER = 2048
SSM_HEADS = 32
SSM_HEAD_DIM = 64
SSM_GROUPS = 4
SSM_HPG = 8
SSM_STATE = 128
SSM_CONV = 4
SSM_XBC = 3072
D_FF = 2816
FFN_CONV = 3
N_CHIPS = 4
N_DEV = 8

ADAM_LR = 0.001
ADAM_B1 = 0.9
ADAM_B2 = 0.999
ADAM_EPS = 1e-08
ADAM_WD = 0.01
ADAM_STEP = 10

VMEM_LIMIT_V7X = 56 * 1024 * 1024
SUBLANES = 8
LANES = 128


def _params(sem=None):
    return pltpu.CompilerParams(dimension_semantics=sem, vmem_limit_bytes=VMEM_LIMIT_V7X)


def _dot(a, b, ca=1, cb=0):
    return lax.dot_general(a.astype(BF16), b.astype(BF16), (((ca,), (cb,)), ((), ())),
                           preferred_element_type=F32)


def _mm(name, a, b, *, ta=False, tb=False, out_dtype=F32, bm, bn, bk, res=None):
    m, k = (a.shape[1], a.shape[0]) if ta else a.shape
    k2, n = (b.shape[1], b.shape[0]) if tb else b.shape
    assert k == k2 and m % bm == 0 and n % bn == 0 and k % bk == 0, (name, a.shape, b.shape)
    nk = k // bk
    a_spec = (pl.BlockSpec((bk, bm), lambda i, j, kk: (kk, i)) if ta
              else pl.BlockSpec((bm, bk), lambda i, j, kk: (i, kk)))
    b_spec = (pl.BlockSpec((bn, bk), lambda i, j, kk: (j, kk)) if tb
              else pl.BlockSpec((bk, bn), lambda i, j, kk: (kk, j)))
    o_spec = pl.BlockSpec((bm, bn), lambda i, j, kk: (i, j))
    has_res = res is not None

    def body(*refs):
        a_ref, b_ref = refs[0], refs[1]
        r_ref = refs[2] if has_res else None
        o_ref = refs[3] if has_res else refs[2]
        p = _dot(a_ref[...], b_ref[...], 0 if ta else 1, 1 if tb else 0)

        def finish(total):
            if has_res:
                total = total + r_ref[...]
            o_ref[...] = total.astype(out_dtype)

        if nk == 1:
            finish(p)
        else:
            acc_ref = refs[-1]
            kk = pl.program_id(2)

            @pl.when(kk == 0)
            def _():
                acc_ref[...] = p

            @pl.when(kk > 0)
            def _():
                acc_ref[...] += p

            @pl.when(kk == nk - 1)
            def _():
                finish(acc_ref[...])

    return pl.pallas_call(
        body, name=name,
        grid=(m // bm, n // bn, nk),
        in_specs=[a_spec, b_spec] + ([o_spec] if has_res else []),
        out_specs=o_spec,
        out_shape=jax.ShapeDtypeStruct((m, n), out_dtype),
        scratch_shapes=[pltpu.VMEM((bm, bn), F32)] if nk > 1 else [],
        compiler_params=_params(("parallel", "parallel", "arbitrary")),
    )(*([a, b] + ([res] if has_res else [])))


def _mm_sum(name, pairs, *, bm, bk, exchange=()):
    nx = len(exchange)
    npair = len(pairs)
    m, n = pairs[0][0].shape[0], pairs[0][1].shape[0]
    steps, first = [], []
    for a, b in pairs:
        k = a.shape[1]
        assert a.shape[0] == m and b.shape == (n, k) and m % bm == 0 and (k % bk == 0 or k < bk), (name, a.shape, b.shape)
        first.append(sum(steps))
        steps.append(max(k // bk, 1))
    total = sum(steps)
    in_specs = []
    for (a, b), off, cnt in zip(pairs, first, steps):
        width = min(bk, a.shape[1])
        in_specs.append(pl.BlockSpec((bm, width), lambda i, kk, off=off, cnt=cnt: (i, jnp.clip(kk - off, 0, cnt - 1))))
        in_specs.append(pl.BlockSpec((n, width), lambda i, kk, off=off, cnt=cnt: (0, jnp.clip(kk - off, 0, cnt - 1))))

    def body(*refs):
        send_refs = refs[2 * npair:2 * npair + nx]
        o_ref = refs[2 * npair + nx]
        recv_refs = refs[2 * npair + nx + 1:2 * npair + 2 * nx + 1]
        acc_ref = refs[2 * npair + 2 * nx + 1]
        i, kk = pl.program_id(0), pl.program_id(1)
        if nx:
            start, finish = _scatter_phases(send_refs, recv_refs, *refs[2 * npair + 2 * nx + 2:])

            @pl.when((i == 0) & (kk == 0))
            def _():
                start()

        for s, (off, cnt) in enumerate(zip(first, steps)):
            @pl.when((kk >= off) & (kk < off + cnt))
            def _(s=s, off=off):
                p = _dot(refs[2 * s][...], refs[2 * s + 1][...], 1, 1)
                if off == 0:
                    @pl.when(kk == 0)
                    def _():
                        acc_ref[...] = p

                    @pl.when(kk > 0)
                    def _():
                        acc_ref[...] += p
                else:
                    acc_ref[...] += p

        @pl.when(kk == total - 1)
        def _():
            o_ref[...] = acc_ref[...]

        if nx:
            @pl.when((i == m // bm - 1) & (kk == total - 1))
            def _():
                finish()

    hbm = pl.BlockSpec(memory_space=pl.ANY)
    res = pl.pallas_call(
        body, name=name, grid=(m // bm, total),
        in_specs=in_specs + [hbm] * nx, out_specs=[pl.BlockSpec((bm, n), lambda i, kk: (i, 0))] + [hbm] * nx,
        out_shape=[jax.ShapeDtypeStruct((m, n), F32)] + [jax.ShapeDtypeStruct(e.shape, e.dtype) for e in exchange],
        scratch_shapes=[pltpu.VMEM((bm, n), F32)] + (_exchange_scratch(nx, N_CHIPS - 1) if nx else []),
        compiler_params=_params(("arbitrary", "arbitrary")),
    )(*[t for pair in pairs for t in pair], *exchange)
    return (res[0], res[1:]) if nx else res[0]


def _rows(name, fn, ins, params, outs, accs, *, tm, rs, unroll=4):
    nrow = ins[0][0].shape[-2]
    while tm % (rs * unroll):
        unroll //= 2
    assert nrow % tm == 0 and tm % rs == 0, (name, nrow, tm, rs)
    n_in, n_p, n_out, n_acc = len(ins), len(params), len(outs), len(accs)
    in_specs = []
    for spec in ins:
        arr, width, cb = spec[:3]
        if len(spec) == 4:
            in_specs.append(pl.BlockSpec((None, tm, width), lambda i, cb=cb, lead=spec[3]: (lead, i, cb)))
        else:
            in_specs.append(pl.BlockSpec((tm, width), lambda i, cb=cb: (i, cb)))
    for p in params:
        in_specs.append(pl.BlockSpec(p.shape, lambda i, nd=p.ndim: (0,) * nd))
    out_specs = [pl.BlockSpec((tm, w), lambda i: (i, 0)) for w, _ in outs]
    out_specs += [pl.BlockSpec(s, lambda i: (0, 0)) for s in accs]
    out_shape = [jax.ShapeDtypeStruct((nrow, w), dt) for w, dt in outs]
    out_shape += [jax.ShapeDtypeStruct(s, F32) for s in accs]

    def body(*refs):
        in_refs = refs[:n_in]
        p_refs = refs[n_in:n_in + n_p]
        o_refs = refs[n_in + n_p:n_in + n_p + n_out]
        a_refs = refs[n_in + n_p + n_out:]
        pv = [p[...] for p in p_refs]

        if n_acc:
            @pl.when(pl.program_id(0) == 0)
            def _():
                for a_ref in a_refs:
                    a_ref[...] = jnp.zeros_like(a_ref)

        def step(r, carry):
            for u in range(unroll):
                sl = pl.ds(pl.multiple_of((r * unroll + u) * rs, rs), rs)
                vals = [ref[sl, :].astype(F32) for ref in in_refs]
                row_out, sums = fn(*vals, *pv)
                for o_ref, v in zip(o_refs, row_out):
                    o_ref[sl, :] = v.astype(o_ref.dtype)
                carry = tuple(c + s for c, s in zip(carry, sums))
            return carry

        init = tuple(jnp.zeros(s, F32) for s in accs)
        total = lax.fori_loop(0, tm // (rs * unroll), step, init)
        for a_ref, t in zip(a_refs, total):
            a_ref[...] += t

    res = pl.pallas_call(
        body, name=name, grid=(nrow // tm,),
        in_specs=in_specs, out_specs=out_specs, out_shape=out_shape,
        compiler_params=_params(("arbitrary",)),
    )(*([s[0] for s in ins] + list(params)))
    return res


def _rms(x, w):
    return x * lax.rsqrt(jnp.mean(x * x, axis=-1, keepdims=True) + EPS) * w


def _colsum(v):
    return jnp.sum(v, axis=0, keepdims=True)


def _rms_fwd(name, x, w):
    def fn(xv, wv):
        return (_rms(xv, wv),), ()
    return _rows(name, fn, [(x, D_MODEL, 0)], [w], [(D_MODEL, BF16)], [], tm=1024, rs=16)[0]


def _rms_bwd(name, x, w, dy, dres):
    def fn(xv, dyv, drv, wv):
        _, vjp = jax.vjp(_rms, xv, wv)
        dx, dw = vjp(dyv)
        return (drv + dx,), (dw,)
    return _rows(name, fn, [(x, D_MODEL, 0), (dy, D_MODEL, 0), (dres, D_MODEL, 0)], [w],
                 [(D_MODEL, F32)], [(1, D_MODEL)], tm=1024, rs=16)


def _final_loss(name, h, target, w):
    def fn(hv, tv, wv):
        y, vjp = jax.vjp(_rms, hv, wv)
        err = y - tv
        part = 0.5 * jnp.sum(jnp.mean(err * err, axis=-1, keepdims=True), axis=0, keepdims=True)
        dh, dw = vjp(err / D_MODEL)
        return (dh,), (jnp.broadcast_to(part, (1, LANES)), dw)
    return _rows(name, fn, [(h, D_MODEL, 0), (target, D_MODEL, 0)], [w],
                 [(D_MODEL, F32)], [(1, LANES), (1, D_MODEL)], tm=1024, rs=16)


def _merge(ga, gb, ya, yb, b0, b1):
    return jax.nn.sigmoid(ga + b0) * ya + jax.nn.sigmoid(gb + b1) * yb


def _merge_fwd(name, g, ya, yb, b0, b1):
    def fn(ga, gb, yav, ybv, b0v, b1v):
        return (_merge(ga, gb, yav, ybv, b0v, b1v),), ()
    return _rows(name, fn, [(g, D_MODEL, 0), (g, D_MODEL, 1), (ya, D_MODEL, 0), (yb, D_MODEL, 0)],
                 [b0, b1], [(D_MODEL, BF16)], [], tm=1024, rs=16)[0]


def _merge_bwd(name, g, ya, yb, dm, b0, b1):
    def fn(ga, gb, yav, ybv, dmv, b0v, b1v):
        _, vjp = jax.vjp(_merge, ga, gb, yav, ybv, b0v, b1v)
        dga, dgb, dya, dyb, db0, db1 = vjp(dmv)
        return (jnp.concatenate([dga, dgb], axis=1), dya, dyb), (db0, db1)
    return _rows(name, fn,
                 [(g, D_MODEL, 0), (g, D_MODEL, 1), (ya, D_MODEL, 0), (yb, D_MODEL, 0), (dm, D_MODEL, 0)],
                 [b0, b1], [(2 * D_MODEL, BF16), (D_MODEL, BF16), (D_MODEL, BF16)],
                 [(1, D_MODEL), (1, D_MODEL)], tm=1024, rs=16)


GROUP_W = SSM_INNER // SSM_GROUPS


def _gate_norm_group(y, z, nw):
    v = y * jax.nn.silu(z)
    return v * lax.rsqrt(jnp.mean(v * v, axis=-1, keepdims=True) + EPS) * nw


def _gate_norm_fwd(name, y, z, nw):
    def fn(yv, zv, nwv):
        parts = [_gate_norm_group(yv[:, k * GROUP_W:(k + 1) * GROUP_W], zv[:, k * GROUP_W:(k + 1) * GROUP_W],
                                  nwv[:, k * GROUP_W:(k + 1) * GROUP_W]) for k in range(SSM_GROUPS)]
        return (jnp.concatenate(parts, axis=1),), ()
    return _rows(name, fn, [(y, SSM_INNER, 0), (z, SSM_INNER, 0)], [nw], [(SSM_INNER, BF16)], [],
                 tm=512, rs=16)[0]


def _gate_norm_bwd(name, y, z, dout, nw):
    def fn(yv, zv, dv, nwv):
        dys, dzs, dns = [], [], []
        for k in range(SSM_GROUPS):
            sl = slice(k * GROUP_W, (k + 1) * GROUP_W)
            _, vjp = jax.vjp(_gate_norm_group, yv[:, sl], zv[:, sl], nwv[:, sl])
            dy, dz, dn = vjp(dv[:, sl])
            dys.append(dy), dzs.append(dz), dns.append(dn)
        return (jnp.concatenate(dys, axis=1), jnp.concatenate(dzs, axis=1)), (jnp.concatenate(dns, axis=1),)
    return _rows(name, fn, [(y, SSM_INNER, 0), (z, SSM_INNER, 0), (dout, SSM_INNER, 0)], [nw],
                 [(SSM_INNER, BF16), (SSM_INNER, BF16)], [(1, SSM_INNER)], tm=512, rs=16)


def _softplus(v):
    return jnp.maximum(v, 0.0) + jnp.log1p(jnp.exp(-jnp.abs(v)))


def _chunk_cumsum(v, reverse=False):
    row = lax.broadcasted_iota(jnp.int32, v.shape, 0)
    step = 1
    while step < CHUNK:
        if reverse:
            shifted = pltpu.roll(v, CHUNK - step, axis=0)
            v = v + jnp.where(row < CHUNK - step, shifted, 0.0)
        else:
            shifted = pltpu.roll(v, step, axis=0)
            v = v + jnp.where(row >= step, shifted, 0.0)
        step *= 2
    return v


def _dt_prep(name, dt_raw, dt_bias, a_log):
    def fn(rv, bv, alv):
        dt = _softplus(rv + bv)
        return (dt, _chunk_cumsum(dt * (-jnp.exp(alv)))), ()
    return _rows(name, fn, [(dt_raw, SSM_HEADS, 0)], [dt_bias, a_log],
                 [(SSM_HEADS, F32), (SSM_HEADS, F32)], [], tm=512, rs=CHUNK)


def _dt_bwd(name, dt_raw, ddt, da1, da2, dt_bias, a_log):
    def fn(rv, ddv, d1, d2, bv, alv):
        pre = rv + bv
        dt = _softplus(pre)
        a_neg = -jnp.exp(alv)
        back = _chunk_cumsum(d1 + d2, reverse=True)
        d_dt = ddv + back * a_neg
        d_raw = d_dt * jax.nn.sigmoid(pre)
        return (d_raw,), (_colsum(d_raw), _colsum(back * dt) * a_neg)
    return _rows(name, fn, [(dt_raw, SSM_HEADS, 0), (ddt, SSM_HEADS, 0), (da1, SSM_HEADS, 0), (da2, SSM_HEADS, 0)],
                 [dt_bias, a_log], [(SSM_HEADS, BF16)], [(1, SSM_HEADS), (1, SSM_HEADS)], tm=512, rs=CHUNK)


def _adamw_math(w, g, m, v):
    m_new = ADAM_B1 * m + (1.0 - ADAM_B1) * g
    v_new = ADAM_B2 * v + (1.0 - ADAM_B2) * jnp.square(g)
    m_hat = m_new / (1.0 - ADAM_B1 ** ADAM_STEP)
    v_hat = v_new / (1.0 - ADAM_B2 ** ADAM_STEP)
    delta = -ADAM_LR * (m_hat / (jnp.sqrt(v_hat) + ADAM_EPS) + ADAM_WD * w)
    return delta, m_new, v_new


def _adamw(name, w, g, m, v, *, tm, rs):
    width = w.shape[1]

    def fn(wv, mv, vv, gv):
        return (gv,) + _adamw_math(wv, gv, mv, vv), ()
    return _rows(name, fn, [(w, width, 0), (m, width, 0), (v, width, 0), (g, width, 0)],
                 [], [(width, F32)] * 4, [], tm=tm, rs=rs)


def _adamw_small(name, ws, gs, ms, vs):
    n = len(ws)

    def body(*refs):
        w_refs, g_refs, m_refs, v_refs = (refs[k * n:(k + 1) * n] for k in range(4))
        outs = refs[4 * n:]
        for i in range(n):
            res = _adamw_math(w_refs[i][...], g_refs[i][...], m_refs[i][...], v_refs[i][...])
            for k in range(3):
                outs[k * n + i][...] = res[k]

    vmem = pl.BlockSpec(memory_space=pltpu.VMEM)
    res = pl.pallas_call(
        body, name=name, in_specs=[vmem] * (4 * n), out_specs=[vmem] * (3 * n),
        out_shape=[jax.ShapeDtypeStruct(w.shape, F32) for w in ws] * 3,
        compiler_params=pltpu.CompilerParams(vmem_limit_bytes=VMEM_LIMIT_V7X),
    )(*ws, *gs, *ms, *vs)
    return res[:n], res[n:2 * n], res[2 * n:]


def _pair_sum(name, a, b, *, tm):
    shape = a.shape
    flat = (shape[0] * shape[1], shape[2])

    def fn(av, bv):
        return (av.astype(F32) + bv.astype(F32),), ()
    out = _rows(name, fn, [(a.reshape(flat), flat[1], 0), (b.reshape(flat), flat[1], 0)], [], [(flat[1], BF16)], [],
                tm=tm, rs=2 * SUBLANES)[0]
    return out.reshape(shape)


def _sum_slots(name, stack, *, tm, rs):
    width = stack.shape[2]

    def fn(*slots):
        s0, s1, s2, s3 = (s.astype(F32) for s in slots)
        return (((s0 + s1) + s2) + s3,), ()
    return _rows(name, fn, [(stack, width, 0, k) for k in range(N_CHIPS)], [], [(width, F32)], [],
                 tm=tm, rs=rs)[0]


def _layernorm(v, w, b):
    mu = jnp.mean(v, axis=-1, keepdims=True)
    var = jnp.mean(jnp.square(v - mu), axis=-1, keepdims=True)
    return (v - mu) * lax.rsqrt(var + EPS) * w + b


GELU_C = 0.7978845608028654
GELU_A = 0.044715


def _gelu_and_slope(x):
    x2 = x * x
    t = jnp.tanh(GELU_C * x * (1.0 + GELU_A * x2))
    half = 0.5 * (1.0 + t)
    slope = half + 0.5 * x * (1.0 - t * t) * (GELU_C * (1.0 + 3.0 * GELU_A * x2))
    return x * half, slope


def _layernorm_and_back(v, w, b):
    mu = jnp.mean(v, axis=-1, keepdims=True)
    cen = v - mu
    rstd = lax.rsqrt(jnp.mean(cen * cen, axis=-1, keepdims=True) + EPS)
    vhat = cen * rstd

    def back(dout):
        dhat = dout * w
        dv = rstd * (dhat - jnp.mean(dhat, axis=-1, keepdims=True)
                     - vhat * jnp.mean(dhat * vhat, axis=-1, keepdims=True))
        return dv, _colsum(dout * vhat), _colsum(dout)

    return vhat * w + b, back


def _gmlp_mask():
    t = lax.broadcasted_iota(jnp.int32, (GMLP_BLOCK, GMLP_BLOCK), 0) // CHUNK
    s = lax.broadcasted_iota(jnp.int32, (GMLP_BLOCK, GMLP_BLOCK), 1) // CHUNK
    return s <= t


GMLP_TM = 1024


def _gmlp_fwd(name, za, ln_w, ln_b, ws, bs_col):
    nrow = za.shape[0]
    tm = GMLP_TM
    width = GMLP_GROUPS * GMLP_BLOCK

    def body(za_ref, lnw_ref, lnb_ref, ws_ref, bs_ref, o_ref, wm_ref):
        mask = _gmlp_mask()
        for g in range(GMLP_GROUPS):
            wm_ref[g] = jnp.where(mask, ws_ref[g], 0.0).astype(BF16)

        def block(n, carry):
            rows = pl.ds(pl.multiple_of(n * GMLP_BLOCK, GMLP_BLOCK), GMLP_BLOCK)
            for g in range(GMLP_GROUPS):
                cols = slice(g * GMLP_BLOCK, (g + 1) * GMLP_BLOCK)
                vcols = slice(width + g * GMLP_BLOCK, width + (g + 1) * GMLP_BLOCK)
                u = jax.nn.gelu(za_ref[rows, cols].astype(F32))
                v = jax.nn.gelu(za_ref[rows, vcols].astype(F32))
                vn = _layernorm(v, lnw_ref[g:g + 1, :], lnb_ref[g:g + 1, :])
                sv = _dot(wm_ref[g], vn) + bs_ref[g]
                o_ref[rows, cols] = (u * sv).astype(o_ref.dtype)
            return carry

        lax.fori_loop(0, tm // GMLP_BLOCK, block, 0)

    small = lambda a: pl.BlockSpec(a.shape, lambda i, nd=a.ndim: (0,) * nd)
    return pl.pallas_call(
        body, name=name, grid=(nrow // tm,),
        in_specs=[pl.BlockSpec((tm, 2 * width), lambda i: (i, 0)), small(ln_w), small(ln_b), small(ws), small(bs_col)],
        out_specs=pl.BlockSpec((tm, width), lambda i: (i, 0)),
        out_shape=jax.ShapeDtypeStruct((nrow, width), BF16),
        scratch_shapes=[pltpu.VMEM((GMLP_GROUPS, GMLP_BLOCK, GMLP_BLOCK), BF16)],
        compiler_params=_params(("arbitrary",)),
    )(za, ln_w, ln_b, ws, bs_col)


def _gmlp_bwd(name, za, dout, ln_w, ln_b, ws, bs_col):
    nrow = za.shape[0]
    tm = GMLP_TM
    width = GMLP_GROUPS * GMLP_BLOCK

    def body(za_ref, do_ref, lnw_ref, lnb_ref, ws_ref, bs_ref, dza_ref, dlnw_ref, dlnb_ref, dws_ref, dbs_ref, wm_ref):
        mask = _gmlp_mask()
        for g in range(GMLP_GROUPS):
            wm_ref[g] = jnp.where(mask, ws_ref[g], 0.0).astype(BF16)

        @pl.when(pl.program_id(0) == 0)
        def _():
            dlnw_ref[...] = jnp.zeros_like(dlnw_ref)
            dlnb_ref[...] = jnp.zeros_like(dlnb_ref)
            dws_ref[...] = jnp.zeros_like(dws_ref)
            dbs_ref[...] = jnp.zeros_like(dbs_ref)

        def block(n, carry):
            rows = pl.ds(pl.multiple_of(n * GMLP_BLOCK, GMLP_BLOCK), GMLP_BLOCK)
            for g in range(GMLP_GROUPS):
                cols = slice(g * GMLP_BLOCK, (g + 1) * GMLP_BLOCK)
                vcols = slice(width + g * GMLP_BLOCK, width + (g + 1) * GMLP_BLOCK)
                u, slope_u = _gelu_and_slope(za_ref[rows, cols].astype(F32))
                v, slope_v = _gelu_and_slope(za_ref[rows, vcols].astype(F32))
                vn, ln_back = _layernorm_and_back(v, lnw_ref[g:g + 1, :], lnb_ref[g:g + 1, :])
                sv = _dot(wm_ref[g], vn) + bs_ref[g]
                d_o = do_ref[rows, cols].astype(F32)
                dsv = d_o * u
                d_wm = _dot(dsv, vn, 1, 1)
                dvn = _dot(wm_ref[g], dsv, 0, 0)
                dv, dlnw, dlnb = ln_back(dvn)
                dza_ref[rows, cols] = (d_o * sv * slope_u).astype(dza_ref.dtype)
                dza_ref[rows, vcols] = (dv * slope_v).astype(dza_ref.dtype)
                dlnw_ref[g:g + 1, :] += dlnw
                dlnb_ref[g:g + 1, :] += dlnb
                dws_ref[g] += jnp.where(mask, d_wm, 0.0)
                dbs_ref[g] += jnp.sum(dsv, axis=1, keepdims=True)
            return carry

        lax.fori_loop(0, tm // GMLP_BLOCK, block, 0)

    small = lambda a: pl.BlockSpec(a.shape, lambda i, nd=a.ndim: (0,) * nd)
    return pl.pallas_call(
        body, name=name, grid=(nrow // tm,),
        in_specs=[pl.BlockSpec((tm, 2 * width), lambda i: (i, 0)), pl.BlockSpec((tm, width), lambda i: (i, 0)),
                  small(ln_w), small(ln_b), small(ws), small(bs_col)],
        out_specs=[pl.BlockSpec((tm, 2 * width), lambda i: (i, 0)), small(ln_w), small(ln_b), small(ws), small(bs_col)],
        out_shape=[jax.ShapeDtypeStruct((nrow, 2 * width), BF16), jax.ShapeDtypeStruct(ln_w.shape, F32),
                   jax.ShapeDtypeStruct(ln_b.shape, F32), jax.ShapeDtypeStruct(ws.shape, F32),
                   jax.ShapeDtypeStruct(bs_col.shape, F32)],
        scratch_shapes=[pltpu.VMEM((GMLP_GROUPS, GMLP_BLOCK, GMLP_BLOCK), BF16)],
        compiler_params=_params(("arbitrary",)),
    )(za, dout, ln_w, ln_b, ws, bs_col)


CONV_TM = 512
CONV_RS = 32
HALO = 2 * SUBLANES


def _tap_rows(w_ref):
    return [w_ref[k:k + 1, :] for k in range(w_ref.shape[0])]


def _halo_specs(nrow, tm, tc):
    per = tm // HALO
    last = nrow // HALO - 1
    main = pl.BlockSpec((tm, tc), lambda j, i: (i, j))
    before = pl.BlockSpec((HALO, tc), lambda j, i: (jnp.maximum(i * per - 1, 0), j))
    after = pl.BlockSpec((HALO, tc), lambda j, i: (jnp.minimum((i + 1) * per, last), j))
    return main, before, after


def _col_spec(rows, tc):
    return pl.BlockSpec((rows, tc), lambda j, i: (0, j))


def _conv_fwd(name, x, w, b, *, tc):
    nrow, ncol = x.shape
    taps = w.shape[0]
    tm, rs = CONV_TM, CONV_RS
    main, before, _ = _halo_specs(nrow, tm, tc)

    def body(x_ref, xb_ref, w_ref, b_ref, o_ref, xw_ref):
        first = pl.program_id(1) == 0
        wv, bv = _tap_rows(w_ref), b_ref[...]
        xw_ref[0:HALO, :] = jnp.where(first, 0.0, xb_ref[...].astype(F32))
        for r in range(tm // rs):
            xw_ref[HALO + r * rs:HALO + (r + 1) * rs, :] = x_ref[r * rs:(r + 1) * rs, :].astype(F32)
        for r in range(tm // rs):
            base = HALO + r * rs
            out = bv + wv[taps - 1] * xw_ref[base:base + rs, :]
            for k in range(taps - 1):
                back = taps - 1 - k
                out = out + wv[k] * xw_ref[base - back:base - back + rs, :]
            o_ref[r * rs:(r + 1) * rs, :] = out.astype(o_ref.dtype)

    return pl.pallas_call(
        body, name=name, grid=(ncol // tc, nrow // tm),
        in_specs=[main, before, _col_spec(taps, tc), _col_spec(1, tc)],
        out_specs=main, out_shape=jax.ShapeDtypeStruct((nrow, ncol), BF16),
        scratch_shapes=[pltpu.VMEM((HALO + tm, tc), F32)],
        compiler_params=_params(("parallel", "arbitrary")),
    )(x, x, w, b)


def _conv_bwd(name, dpre, x, w, *, tc):
    nrow, ncol = x.shape
    taps = w.shape[0]
    tm, rs = CONV_TM, CONV_RS
    nsub = tm // rs
    main, before, after = _halo_specs(nrow, tm, tc)

    def fold(v):
        total = v[0:SUBLANES]
        for q in range(1, rs // SUBLANES):
            total = total + v[q * SUBLANES:(q + 1) * SUBLANES]
        return total

    def body(d_ref, da_ref, x_ref, xb_ref, w_ref, dx_ref, dw_ref, db_ref, dwin_ref, xwin_ref):
        i = pl.program_id(1)
        first, last = i == 0, i == pl.num_programs(1) - 1
        wv = _tap_rows(w_ref)

        @pl.when(first)
        def _():
            dw_ref[...] = jnp.zeros_like(dw_ref)
            db_ref[...] = jnp.zeros_like(db_ref)

        xwin_ref[0:HALO, :] = jnp.where(first, 0.0, xb_ref[...].astype(F32))
        dwin_ref[tm:, :] = jnp.where(last, 0.0, da_ref[...].astype(F32))
        for r in range(nsub):
            dwin_ref[r * rs:(r + 1) * rs, :] = d_ref[r * rs:(r + 1) * rs, :].astype(F32)
            xwin_ref[HALO + r * rs:HALO + (r + 1) * rs, :] = x_ref[r * rs:(r + 1) * rs, :].astype(F32)
        dw = [jnp.zeros((SUBLANES, tc), F32)] * taps
        db = jnp.zeros((SUBLANES, tc), F32)
        for r in range(nsub):
            cur = dwin_ref[r * rs:(r + 1) * rs, :]
            dx = wv[taps - 1] * cur
            for k in range(taps - 1):
                ahead = taps - 1 - k
                dx = dx + wv[k] * dwin_ref[r * rs + ahead:(r + 1) * rs + ahead, :]
            dx_ref[r * rs:(r + 1) * rs, :] = dx.astype(dx_ref.dtype)
            for k in range(taps):
                back = taps - 1 - k
                dw[k] = dw[k] + fold(cur * xwin_ref[HALO + r * rs - back:HALO + (r + 1) * rs - back, :])
            db = db + fold(cur)
        for k in range(taps):
            dw_ref[k:k + 1, :] += _colsum(dw[k])
        db_ref[...] += _colsum(db)

    return pl.pallas_call(
        body, name=name, grid=(ncol // tc, nrow // tm),
        in_specs=[main, after, main, before, _col_spec(taps, tc)],
        out_specs=[main, _col_spec(taps, tc), _col_spec(1, tc)],
        out_shape=[jax.ShapeDtypeStruct((nrow, ncol), BF16), jax.ShapeDtypeStruct((taps, ncol), F32),
                   jax.ShapeDtypeStruct((1, ncol), F32)],
        scratch_shapes=[pltpu.VMEM((tm + HALO, tc), F32), pltpu.VMEM((HALO + tm, tc), F32)],
        compiler_params=_params(("parallel", "arbitrary")),
    )(dpre, dpre, x, x, w)


def _glu(gate, val):
    return jax.nn.silu(gate) * val


def _ffn_act_fwd(name, pg, pv, wg, wv, bg, bv, *, tc):
    nrow, ncol = pg.shape
    taps = wg.shape[0]
    tm, rs = CONV_TM, CONV_RS
    main, before, _ = _halo_specs(nrow, tm, tc)

    def body(pg_ref, pgb_ref, pv_ref, pvb_ref, wg_ref, wv_ref, bg_ref, bv_ref, g_ref, v_ref, a_ref, gwin_ref, vwin_ref):
        first = pl.program_id(1) == 0
        taps_g, taps_v, bgv, bvv = _tap_rows(wg_ref), _tap_rows(wv_ref), bg_ref[...], bv_ref[...]
        gwin_ref[0:HALO, :] = jnp.where(first, 0.0, pgb_ref[...].astype(F32))
        vwin_ref[0:HALO, :] = jnp.where(first, 0.0, pvb_ref[...].astype(F32))
        for r in range(tm // rs):
            gwin_ref[HALO + r * rs:HALO + (r + 1) * rs, :] = pg_ref[r * rs:(r + 1) * rs, :].astype(F32)
            vwin_ref[HALO + r * rs:HALO + (r + 1) * rs, :] = pv_ref[r * rs:(r + 1) * rs, :].astype(F32)

        def conv(win_ref, tap_rows, bias, r):
            base = HALO + r * rs
            out = bias + tap_rows[taps - 1] * win_ref[base:base + rs, :]
            for k in range(taps - 1):
                back = taps - 1 - k
                out = out + tap_rows[k] * win_ref[base - back:base - back + rs, :]
            return out

        for r in range(tm // rs):
            sl = slice(r * rs, (r + 1) * rs)
            gate, val = conv(gwin_ref, taps_g, bgv, r), conv(vwin_ref, taps_v, bvv, r)
            g_ref[sl, :] = gate.astype(g_ref.dtype)
            v_ref[sl, :] = val.astype(v_ref.dtype)
            a_ref[sl, :] = _glu(gate, val).astype(a_ref.dtype)

    return pl.pallas_call(
        body, name=name, grid=(ncol // tc, nrow // tm),
        in_specs=[main, before, main, before, _col_spec(taps, tc), _col_spec(taps, tc), _col_spec(1, tc), _col_spec(1, tc)],
        out_specs=[main, main, main],
        out_shape=[jax.ShapeDtypeStruct((nrow, ncol), BF16)] * 3,
        scratch_shapes=[pltpu.VMEM((HALO + tm, tc), F32), pltpu.VMEM((HALO + tm, tc), F32)],
        compiler_params=_params(("parallel", "arbitrary")),
    )(pg, pg, pv, pv, wg, wv, bg, bv)


def _ffn_act_bwd(name, dact, gate, val):
    def fn(dv, gv, vv):
        _, vjp = jax.vjp(_glu, gv, vv)
        dg, dval = vjp(dv)
        return (dg, dval), ()
    width = dact.shape[1]
    return _rows(name, fn, [(dact, width, 0), (gate, width, 0), (val, width, 0)], [],
                 [(width, BF16), (width, BF16)], [], tm=512, rs=2 * SUBLANES)


SSD_TM = 256
SSD_CHUNKS = SSD_TM // CHUNK
X_OFF, B_OFF, C_OFF = 0, SSM_INNER, SSM_INNER + SSM_GROUPS * SSM_STATE
HP = SSM_HPG * SSM_HEAD_DIM


def _causal_tiled():
    row = lax.broadcasted_iota(jnp.int32, (CHUNK, HP), 0)
    src = lax.broadcasted_iota(jnp.int32, (CHUNK, HP), 1) & (CHUNK - 1)
    return src <= row


def _split2(v):
    hi = v.astype(BF16)
    return hi, (v - hi.astype(F32)).astype(BF16)


def _dot_exact(a, ind):
    hi, lo = (lax.dot_general(p, ind, (((1,), (0,)), ((), ())), preferred_element_type=F32) for p in _split2(a))
    return hi + lo


def _head_indicator():
    head = lax.broadcasted_iota(jnp.int32, (SSM_HEADS, SSM_INNER), 0)
    chan = lax.broadcasted_iota(jnp.int32, (SSM_HEADS, SSM_INNER), 1)
    return (chan // SSM_HEAD_DIM == head).astype(BF16)


def _chunk_decays(ci, dt_ref, ac_ref, ind, ax_ref, dtx_ref, eax_ref, eex_ref, tail_ref):
    rows = pl.ds(pl.multiple_of(ci * CHUNK, CHUNK), CHUNK)
    ax_ref[...] = _dot_exact(ac_ref[rows, :], ind)
    dtx_ref[...] = _dot_exact(dt_ref[rows, :], ind)
    eax_ref[...] = jnp.exp(ax_ref[...])
    eex_ref[...] = jnp.exp(ax_ref[CHUNK - 1:CHUNK, :] - ax_ref[...])
    tail = pl.ds(pl.multiple_of(ci * CHUNK + CHUNK - SUBLANES, SUBLANES), SUBLANES)
    tail_ref[...] = jnp.exp(ac_ref[tail, :])


def _group_decay(ci, g, ax_ref, af_ref, xbc_ref, causal):
    gcols = slice(g * HP, (g + 1) * HP)
    bm = xbc_ref[:, B_OFF + g * SSM_STATE:B_OFF + (g + 1) * SSM_STATE]
    cm = xbc_ref[:, C_OFF + g * SSM_STATE:C_OFF + (g + 1) * SSM_STATE]
    cb_tiled = _dot(cm, jnp.concatenate([bm] * SSM_HPG, axis=0), 1, 1)
    seg = ax_ref[:, gcols] - af_ref[ci, :, gcols]
    decay = jnp.where(causal, jnp.exp(jnp.where(causal, seg, 0.0)), 0.0)
    return bm, cm, cb_tiled * decay, decay


def _ssd_fwd(name, pre, dt, a_cum, a_flat, d_x, ind, shards):
    nrow = pre.shape[0]
    tm = SSD_TM
    nstep = nrow // tm
    ng = len(shards)

    def body(pre_ref, dt_ref, ac_ref, af_ref, dx_ref, ind_ref, *rest):
        shard_refs, (y_ref, st_ref), stack_refs = rest[:ng], rest[ng:ng + 2], rest[ng + 2:2 * ng + 2]
        (h_ref, xbc_ref, ax_ref, dtx_ref, eax_ref, eex_ref, m_ref, xd_ref, yd_ref, tail_ref,
         send_sems, recv_sems) = rest[2 * ng + 2:]
        step = pl.program_id(0)
        start, forward, finish = _gather_phases([s.shape[0] for s in shards], ng, shard_refs, stack_refs,
                                                send_sems, recv_sems)

        @pl.when(step == 0)
        def _():
            h_ref[...] = jnp.zeros_like(h_ref)
            start()

        @pl.when(step == nstep // 2)
        def _():
            forward()

        causal = _causal_tiled()
        ind = ind_ref[...]

        def chunk(ci, carry):
            rows = pl.ds(pl.multiple_of(ci * CHUNK, CHUNK), CHUNK)
            xbc_ref[...] = jax.nn.silu(pre_ref[rows, :].astype(F32))
            _chunk_decays(ci, dt_ref, ac_ref, ind, ax_ref, dtx_ref, eax_ref, eex_ref, tail_ref)
            st_ref[ci] = h_ref[...].astype(st_ref.dtype)
            for g in range(SSM_GROUPS):
                gcols = slice(g * HP, (g + 1) * HP)
                bm, cm, m_all, _ = _group_decay(ci, g, ax_ref, af_ref, xbc_ref, causal)
                m_ref[...] = m_all
                x_g = xbc_ref[:, gcols]
                xd = x_g * dtx_ref[:, gcols]
                xd_ref[...] = xd
                h_g = h_ref[gcols, :]
                for hh in range(SSM_HPG):
                    lc = slice(hh * SSM_HEAD_DIM, (hh + 1) * SSM_HEAD_DIM)
                    yd_ref[:, lc] = _dot(m_ref[:, lc], xd_ref[:, lc])
                y_ref[rows, gcols] = (yd_ref[...] + _dot(cm, h_g, 1, 1) * eax_ref[:, gcols]
                                      + dx_ref[:, gcols] * x_g).astype(y_ref.dtype)
                new = _dot(xd * eex_ref[:, gcols], bm, 0, 0)
                for hh in range(SSM_HPG):
                    h = g * SSM_HPG + hh
                    hrows = slice(h * SSM_HEAD_DIM, (h + 1) * SSM_HEAD_DIM)
                    lrows = slice(hh * SSM_HEAD_DIM, (hh + 1) * SSM_HEAD_DIM)
                    h_ref[hrows, :] = tail_ref[SUBLANES - 1:SUBLANES, h:h + 1] * h_ref[hrows, :] + new[lrows, :]
            return carry

        lax.fori_loop(0, SSD_CHUNKS, chunk, 0)

        @pl.when(step == nstep - 1)
        def _():
            finish()

    nchunk = nrow // CHUNK
    whole = lambda a: pl.BlockSpec(a.shape, lambda i, nd=a.ndim: (0,) * nd)
    hbm = pl.BlockSpec(memory_space=pl.ANY)
    wide = lambda: pltpu.VMEM((CHUNK, SSM_INNER), F32)
    group = lambda: pltpu.VMEM((CHUNK, HP), F32)
    res = pl.pallas_call(
        body, name=name, grid=(nstep,),
        in_specs=[pl.BlockSpec((tm, SSM_XBC), lambda i: (i, 0)), pl.BlockSpec((tm, SSM_HEADS), lambda i: (i, 0)),
                  pl.BlockSpec((tm, SSM_HEADS), lambda i: (i, 0)),
                  pl.BlockSpec((SSD_CHUNKS, 1, SSM_INNER), lambda i: (i, 0, 0)), whole(d_x), whole(ind)] + [hbm] * ng,
        out_specs=[pl.BlockSpec((tm, SSM_INNER), lambda i: (i, 0)),
                   pl.BlockSpec((SSD_CHUNKS, SSM_INNER, SSM_STATE), lambda i: (i, 0, 0))] + [hbm] * ng,
        out_shape=[jax.ShapeDtypeStruct((nrow, SSM_INNER), BF16),
                   jax.ShapeDtypeStruct((nchunk, SSM_INNER, SSM_STATE), BF16)]
        + [jax.ShapeDtypeStruct((N_CHIPS,) + s.shape, s.dtype) for s in shards],
        scratch_shapes=[pltpu.VMEM((SSM_INNER, SSM_STATE), F32), pltpu.VMEM((CHUNK, SSM_XBC), F32),
                        wide(), wide(), wide(), wide(), group(), group(), group(),
                        pltpu.VMEM((SUBLANES, SSM_HEADS), F32)] + _exchange_scratch(ng, GATHER_SEMS),
        compiler_params=_params(("arbitrary",)),
    )(pre, dt, a_cum, a_flat, d_x, ind, *shards)
    return res[0], res[1], res[2:]


def _ssd_bwd(name, pre, dt, a_cum, a_flat, d_x, ind, ind_t, states, dy, pairs):
    nrow = pre.shape[0]
    tm = SSD_TM
    ntile = nrow // tm
    npair = len(pairs)

    def body(pre_ref, dt_ref, ac_ref, af_ref, dx_ref, ind_ref, indt_ref, st_ref, dy_ref, *rest):
        pair_refs = rest[:npair]
        dpre_ref, ddt_ref, da_ref, daf_ref, dd_ref = rest[npair:npair + 5]
        recv_refs = rest[npair + 5:2 * npair + 5]
        (dh_ref, xbc_ref, dxbc_ref, ax_ref, dtx_ref, eax_ref, eex_ref, red_ref,
         m_ref, l_ref, xd_ref, dm_ref, dxd_ref, fold_ref, hd_ref, tail_ref, send_sems, recv_sems) = rest[2 * npair + 5:]
        start, finish = _scatter_phases(pair_refs, recv_refs, send_sems, recv_sems)

        @pl.when(pl.program_id(0) == 0)
        def _():
            dh_ref[...] = jnp.zeros_like(dh_ref)
            dd_ref[...] = jnp.zeros_like(dd_ref)
            start()

        causal = _causal_tiled()
        ind, ind_t = ind_ref[...], indt_ref[...]
        is_last_row = lax.broadcasted_iota(jnp.int32, (CHUNK, 1), 0) == CHUNK - 1
        ones = jnp.ones((CHUNK, SSM_STATE), BF16)

        def chunk(k, ddx):
            ci = SSD_CHUNKS - 1 - k
            rows = pl.ds(pl.multiple_of(ci * CHUNK, CHUNK), CHUNK)
            pre_v = pre_ref[rows, :].astype(F32)
            xbc_ref[...] = jax.nn.silu(pre_v)
            _chunk_decays(ci, dt_ref, ac_ref, ind, ax_ref, dtx_ref, eax_ref, eex_ref, tail_ref)
            ddx_parts = []
            for g in range(SSM_GROUPS):
                gcols = slice(g * HP, (g + 1) * HP)
                bcols = slice(B_OFF + g * SSM_STATE, B_OFF + (g + 1) * SSM_STATE)
                ccols = slice(C_OFF + g * SSM_STATE, C_OFF + (g + 1) * SSM_STATE)
                bm, cm, m_all, decay = _group_decay(ci, g, ax_ref, af_ref, xbc_ref, causal)
                m_ref[...] = m_all
                l_ref[...] = decay
                x_g = xbc_ref[:, gcols]
                xd = x_g * dtx_ref[:, gcols]
                xd_ref[...] = xd
                h_g = st_ref[ci, gcols, :]
                dh_g = dh_ref[gcols, :]
                dy_g = dy_ref[rows, gcols]
                for hh in range(SSM_HPG):
                    h = g * SSM_HPG + hh
                    hcols = slice(h * SSM_HEAD_DIM, (h + 1) * SSM_HEAD_DIM)
                    lc = slice(hh * SSM_HEAD_DIM, (hh + 1) * SSM_HEAD_DIM)
                    dy_h = dy_ref[rows, hcols]
                    dm_ref[:, lc] = _dot(dy_h, xd_ref[:, lc], 1, 1)
                    dxd_ref[:, lc] = _dot(m_ref[:, lc], dy_h, 0, 0)
                ebdh = eex_ref[:, gcols] * _dot(bm, dh_g, 1, 1)
                dxd = dxd_ref[...] + ebdh
                dm = dm_ref[...]
                t = dm * l_ref[...]
                t128 = (t[:, 0:LANES] + t[:, LANES:2 * LANES]) + (t[:, 2 * LANES:3 * LANES] + t[:, 3 * LANES:])
                fold_ref[...] = t128 + pltpu.roll(t128, CHUNK, axis=1)
                dw_sum = fold_ref[:, 0:CHUNK]
                q = dm * m_ref[...]
                dyea = dy_g * eax_ref[:, gcols]
                red_ref[0:CHUNK, gcols] = q + dyea * _dot(cm, h_g, 1, 1)
                red_ref[CHUNK:2 * CHUNK, gcols] = xd * ebdh
                red_ref[2 * CHUNK:3 * CHUNK, gcols] = dxd * x_g
                daf_ref[ci, :, gcols] = -jnp.sum(q, axis=0, keepdims=True)
                ddx_parts.append(jnp.sum(dy_g * x_g, axis=0, keepdims=True))
                dxbc_ref[:, gcols] = dxd * dtx_ref[:, gcols] + dx_ref[:, gcols] * dy_g
                dxbc_ref[:, ccols] = _dot(dw_sum, bm) + _dot(dyea, h_g)
                dxbc_ref[:, bcols] = _dot(dw_sum, cm, 0, 0) + _dot(xd * eex_ref[:, gcols], dh_g)
                dh_new = _dot(dyea, cm, 0, 0)
                for hh in range(SSM_HPG):
                    h = g * SSM_HPG + hh
                    hrows = slice(h * SSM_HEAD_DIM, (h + 1) * SSM_HEAD_DIM)
                    lrows = slice(hh * SSM_HEAD_DIM, (hh + 1) * SSM_HEAD_DIM)
                    hd_ref[h:h + 1, :] = jnp.sum(st_ref[ci, hrows, :] * dh_ref[hrows, :], axis=0, keepdims=True)
                    dh_ref[hrows, :] = tail_ref[SUBLANES - 1:SUBLANES, h:h + 1] * dh_ref[hrows, :] + dh_new[lrows, :]
            sums = _dot_exact(red_ref[...], ind_t)
            ra, ts = sums[:CHUNK], sums[CHUNK:2 * CHUNK]
            hdh = sum(lax.dot_general(ones, p, (((1,), (1,)), ((), ())), preferred_element_type=F32)
                      for p in _split2(hd_ref[...]))
            da_last = jnp.sum(ts, axis=0, keepdims=True) + tail_ref[SUBLANES - 1:SUBLANES, :] * hdh
            da_ref[rows, :] = ra - ts + jnp.where(is_last_row, da_last, 0.0)
            ddt_ref[rows, :] = sums[2 * CHUNK:]
            sig = jax.nn.sigmoid(pre_v)
            dpre_ref[rows, :] = (dxbc_ref[...] * (sig * (1.0 + pre_v * (1.0 - sig)))).astype(dpre_ref.dtype)
            return ddx + jnp.concatenate(ddx_parts, axis=1)

        ddx = lax.fori_loop(0, SSD_CHUNKS, chunk, jnp.zeros((1, SSM_INNER), F32))
        dd_ref[...] += _dot_exact(jnp.broadcast_to(ddx, (SUBLANES, SSM_INNER)), ind_t)

        @pl.when(pl.program_id(0) == ntile - 1)
        def _():
            finish()

    rev = lambda i: ntile - 1 - i
    whole = lambda a: pl.BlockSpec(a.shape, lambda i, nd=a.ndim: (0,) * nd)
    hbm = pl.BlockSpec(memory_space=pl.ANY)
    wide = lambda: pltpu.VMEM((CHUNK, SSM_INNER), F32)
    group = lambda: pltpu.VMEM((CHUNK, HP), F32)
    res = pl.pallas_call(
        body, name=name, grid=(ntile,),
        in_specs=[pl.BlockSpec((tm, SSM_XBC), lambda i: (rev(i), 0)), pl.BlockSpec((tm, SSM_HEADS), lambda i: (rev(i), 0)),
                  pl.BlockSpec((tm, SSM_HEADS), lambda i: (rev(i), 0)),
                  pl.BlockSpec((SSD_CHUNKS, 1, SSM_INNER), lambda i: (rev(i), 0, 0)),
                  whole(d_x), whole(ind), whole(ind_t),
                  pl.BlockSpec((SSD_CHUNKS, SSM_INNER, SSM_STATE), lambda i: (rev(i), 0, 0)),
                  pl.BlockSpec((tm, SSM_INNER), lambda i: (rev(i), 0))] + [hbm] * npair,
        out_specs=[pl.BlockSpec((tm, SSM_XBC), lambda i: (rev(i), 0)), pl.BlockSpec((tm, SSM_HEADS), lambda i: (rev(i), 0)),
                   pl.BlockSpec((tm, SSM_HEADS), lambda i: (rev(i), 0)),
                   pl.BlockSpec((SSD_CHUNKS, 1, SSM_INNER), lambda i: (rev(i), 0, 0)),
                   pl.BlockSpec((SUBLANES, SSM_HEADS), lambda i: (0, 0))] + [hbm] * npair,
        out_shape=[jax.ShapeDtypeStruct((nrow, SSM_XBC), BF16), jax.ShapeDtypeStruct((nrow, SSM_HEADS), F32),
                   jax.ShapeDtypeStruct((nrow, SSM_HEADS), F32), jax.ShapeDtypeStruct((nrow // CHUNK, 1, SSM_INNER), F32),
                   jax.ShapeDtypeStruct((SUBLANES, SSM_HEADS), F32)]
        + [jax.ShapeDtypeStruct(p.shape, p.dtype) for p in pairs],
        scratch_shapes=[pltpu.VMEM((SSM_INNER, SSM_STATE), F32), pltpu.VMEM((CHUNK, SSM_XBC), F32),
                        pltpu.VMEM((CHUNK, SSM_XBC), F32), wide(), wide(), wide(), wide(),
                        pltpu.VMEM((3 * CHUNK, SSM_INNER), F32),
                        group(), group(), group(), group(), group(), pltpu.VMEM((CHUNK, LANES), F32),
                        pltpu.VMEM((SSM_HEADS, SSM_STATE), F32), pltpu.VMEM((SUBLANES, SSM_HEADS), F32)]
        + _exchange_scratch(npair, N_CHIPS - 1),
        compiler_params=_params(("arbitrary",)),
    )(pre, dt, a_cum, a_flat, d_x, ind, ind_t, states, dy, *pairs)
    return res[:5], res[5:]


LATE = ["w_proj_a", "w_proj_b", "w_out", "ffn_w_up", "ffn_w_down"]
HALF_TILES = {"w_in": 128, "w_proj_a": 128, "w_proj_b": 256, "w_out": 128, "ffn_w_up": 128, "ffn_w_down": 176}


def _late_weights(stacks, shards):
    pa, pb, out, up, down = [_own_slot(stack, own) for stack, own in zip(stacks, shards)]
    return {"w_proj_a": pa.reshape(-1, D_MODEL), "w_proj_b": pb.reshape(-1, D_MODEL), "w_out": out.reshape(-1, D_MODEL),
            "w_up_g": _columns_from_chips(up[:2]), "w_up_v": _columns_from_chips(up[2:]),
            "w_down": down.reshape(-1, D_MODEL)}


def _pair_reduce(tag, names, stacks):
    core = lax.axis_index("c")
    own_half = [_row_half(s, core, 1) for s in stacks]
    other_half = _swap_cores("pair_grads_" + tag, [_row_half(s, 1 - core, 1) for s in stacks])
    return [_pair_sum("pair_" + n, a, b, tm=HALF_TILES[n]) for n, a, b in zip(names, own_half, other_half)]


def _local_step(x, target, w, late_shards):
    w = dict(w)
    g = {}
    bs_col = w["gmlp_bs"].reshape(GMLP_GROUPS, GMLP_BLOCK, 1)
    b0, b1 = w["gate_bias"][0:1], w["gate_bias"][1:2]

    xn = _rms_fwd("mix_norm", x, w["mix_norm_w"])
    big = dict(bm=1024, bn=1024, bk=1024)
    act16 = dict(out_dtype=BF16, **big)
    gates = _mm("in_gates", xn, w["w_g"], **act16)
    za = _mm("in_gmlp", xn, w["w_za"], **act16)
    z = _mm("in_z", xn, w["w_z"], **act16)
    xbc = _mm("in_xbc", xn, w["w_xbc"], **act16)
    dt_raw = _mm("in_dt", xn, w["w_dt"], bm=1024, bn=SSM_HEADS, bk=1024)

    pre = _conv_fwd("ssm_conv_fwd", xbc, w["ssm_conv_w"], w["ssm_conv_b"], tc=1024)
    dt, a_cum = _dt_prep("dt_prep", dt_raw, w["ssm_dt_bias"], w["ssm_a_log"])
    a_flat = jnp.transpose(a_cum.reshape(-1, CHUNK, SSM_HEADS), (0, 2, 1)).reshape(-1, 1, SSM_INNER)
    d_x = jnp.repeat(w["ssm_d"], SSM_HEAD_DIM, axis=1)
    ind = _head_indicator()
    y_ssd, states, late_stacks = _ssd_fwd("ssd_fwd", pre, dt, a_cum, a_flat, d_x, ind, late_shards)
    w.update(_late_weights(late_stacks, late_shards))
    yb_pre = _gate_norm_fwd("gate_norm_fwd", y_ssd, z, w["ssm_norm_w"])
    y_b = _mm("proj_b", yb_pre, w["w_proj_b"], bm=1024, bn=1024, bk=SSM_INNER, out_dtype=BF16)

    ya_pre = _gmlp_fwd("gmlp_fwd", za, w["gmlp_ln_w"], w["gmlp_ln_b"], w["gmlp_ws"], bs_col)
    y_a = _mm("proj_a", ya_pre, w["w_proj_a"], **act16)

    merged = _merge_fwd("merge_fwd", gates, y_a, y_b, b0, b1)
    h1 = _mm("out_proj", merged, w["w_out"], res=x, **big)

    hn = _rms_fwd("ffn_norm", h1, w["ffn_norm_w"])
    half = dict(bm=1024, bn=D_FF // 2, bk=1024, out_dtype=BF16)
    pg = _mm("ffn_up_gate", hn, w["w_up_g"], **half)
    pv = _mm("ffn_up_val", hn, w["w_up_v"], **half)
    cw, cb = w["ffn_conv_w"], w["ffn_conv_b"]
    gate, val, act = _ffn_act_fwd("ffn_act_fwd", pg, pv, cw[:, :D_FF], cw[:, D_FF:], cb[:, :D_FF], cb[:, D_FF:],
                                  tc=D_FF // 2)
    h2 = _mm("ffn_down", act, w["w_down"], res=h1, bm=1024, bn=1024, bk=D_FF // 2)

    dh2, loss_part, g["final_norm_w"] = _final_loss("final_loss", h2, target, w["final_norm_w"].reshape(1, D_MODEL))

    dact = _mm("d_act", dh2, w["w_down"], tb=True, **half)
    wgrad = dict(ta=True, bk=min(2048, x.shape[0]), out_dtype=BF16)
    g["w_down"] = _mm("dw_down", act, dh2, bm=D_FF // 2, bn=1024, **wgrad)
    dgate, dval = _ffn_act_bwd("ffn_act_bwd", dact, gate, val)
    dpg, dcwg, dcbg = _conv_bwd("ffn_conv_bwd_gate", dgate, pg, cw[:, :D_FF], tc=D_FF // 2)
    dpv, dcwv, dcbv = _conv_bwd("ffn_conv_bwd_val", dval, pv, cw[:, D_FF:], tc=D_FF // 2)
    g["ffn_conv_w"] = jnp.concatenate([dcwg, dcwv], axis=1)
    g["ffn_conv_b"] = jnp.concatenate([dcbg, dcbv], axis=1)
    dhn = _mm_sum("d_hn", [(dpg, w["w_up_g"]), (dpv, w["w_up_v"])], bm=1024, bk=D_FF // 2)
    g["w_up_g"] = _mm("dw_up_gate", hn, dpg, bm=1024, bn=D_FF // 2, **wgrad)
    g["w_up_v"] = _mm("dw_up_val", hn, dpv, bm=1024, bn=D_FF // 2, **wgrad)
    dh1, g["ffn_norm_w"] = _rms_bwd("ffn_norm_bwd", h1, w["ffn_norm_w"], dhn, dh2)

    dmerged = _mm("d_merged", dh1, w["w_out"], tb=True, **act16)
    g["w_out"] = _mm("dw_out", merged, dh1, bm=1024, bn=1024, **wgrad)
    dgates, dya, dyb, db0, db1 = _merge_bwd("merge_bwd", gates, y_a, y_b, dmerged, b0, b1)
    g["gate_bias"] = jnp.concatenate([db0, db1], axis=0)

    dya_pre = _mm("d_ya_pre", dya, w["w_proj_a"], tb=True, **act16)
    g["w_proj_a"] = _mm("dw_proj_a", ya_pre, dya, bm=1024, bn=1024, **wgrad)
    dyb_pre = _mm("d_yb_pre", dyb, w["w_proj_b"], tb=True, **act16)
    g["w_proj_b"] = _mm("dw_proj_b", yb_pre, dyb, bm=1024, bn=1024, **wgrad)
    late_pairs = _pair_reduce("late", LATE, [
        g["w_proj_a"].reshape(N_CHIPS, -1, D_MODEL), g["w_proj_b"].reshape(N_CHIPS, -1, D_MODEL),
        g["w_out"].reshape(N_CHIPS, -1, D_MODEL),
        jnp.concatenate([_columns_to_chips(g["w_up_g"], 2), _columns_to_chips(g["w_up_v"], 2)], axis=0),
        g["w_down"].reshape(N_CHIPS, -1, D_MODEL)])

    dy_ssd, dz, g["ssm_norm_w"] = _gate_norm_bwd("gate_norm_bwd", y_ssd, z, dyb_pre, w["ssm_norm_w"])
    (dpre, ddt, da_tok, da_flat, dd), late_received = _ssd_bwd(
        "ssd_bwd", pre, dt, a_cum, a_flat, d_x, ind, ind.T, states, dy_ssd, late_pairs)
    g["ssm_d"] = dd[0:1]
    da_src = jnp.transpose(da_flat.reshape(-1, SSM_HEADS, CHUNK), (0, 2, 1)).reshape(-1, SSM_HEADS)
    ddt_raw, g["ssm_dt_bias"], g["ssm_a_log"] = _dt_bwd("dt_bwd", dt_raw, ddt, da_tok, da_src,
                                                         w["ssm_dt_bias"], w["ssm_a_log"])
    dxbc, g["ssm_conv_w"], g["ssm_conv_b"] = _conv_bwd("ssm_conv_bwd", dpre, xbc, w["ssm_conv_w"], tc=1024)

    dza, g["gmlp_ln_w"], g["gmlp_ln_b"], g["gmlp_ws"], dbs = _gmlp_bwd(
        "gmlp_bwd", za, dya_pre, w["gmlp_ln_w"], w["gmlp_ln_b"], w["gmlp_ws"], bs_col)
    g["gmlp_bs"] = dbs.reshape(GMLP_GROUPS, GMLP_BLOCK)

    dw_in = jnp.concatenate([
        _mm("dw_gates", xn, dgates, bm=1024, bn=1024, **wgrad), _mm("dw_gmlp", xn, dza, bm=1024, bn=1024, **wgrad),
        _mm("dw_z", xn, dz, bm=1024, bn=1024, **wgrad), _mm("dw_xbc", xn, dxbc, bm=1024, bn=1024, **wgrad),
        _mm("dw_dt", xn, ddt_raw, bm=1024, bn=SSM_HEADS, **wgrad)], axis=1)
    in_pairs = _pair_reduce("in", ["w_in"], [_columns_to_chips(dw_in)])
    dxn, in_received = _mm_sum("d_xn", [(dgates, w["w_g"]), (dza, w["w_za"]), (dz, w["w_z"]), (dxbc, w["w_xbc"]),
                                        (ddt_raw, w["w_dt"])], bm=1024, bk=1024, exchange=in_pairs)
    grad_x, g["mix_norm_w"] = _rms_bwd("mix_norm_bwd", x, w["mix_norm_w"], dxn, dh1)
    return loss_part, grad_x, g, in_pairs + list(late_pairs), list(in_received) + list(late_received)


def _position():
    return lax.axis_index("x"), lax.axis_index("y"), lax.axis_index("c")


def _own_slot(stack, own):
    chip = 2 * lax.axis_index("x") + lax.axis_index("y")
    return lax.dynamic_update_index_in_dim(stack, own, chip, axis=0)


def _scatter_phases(ins, outs, send_sems, recv_sems):
    n = len(ins)
    x, y, c = _position()
    me = 2 * x + y
    peers = [(1 - x, y), (x, 1 - y), (1 - x, 1 - y)]

    def copy(i, k, src_slot, dst_slot):
        px, py = peers[k]
        return pltpu.make_async_remote_copy(
            src_ref=ins[i].at[src_slot], dst_ref=outs[i].at[dst_slot],
            send_sem=send_sems.at[i, k], recv_sem=recv_sems.at[i, k],
            device_id=(px, py, c), device_id_type=MESH)

    def start():
        for i in range(n):
            for k, (px, py) in enumerate(peers):
                copy(i, k, 2 * px + py, me).start()

    def finish():
        for i in range(n):
            for k, (px, py) in enumerate(peers):
                copy(i, k, me, 2 * px + py).wait_recv()
        for i in range(n):
            for k, (px, py) in enumerate(peers):
                copy(i, k, 2 * px + py, me).wait_send()

    return start, finish


def _exchange_scratch(n, per_array):
    return [pltpu.SemaphoreType.DMA((n, per_array)), pltpu.SemaphoreType.DMA((n, per_array))]


def _half_rows(ref_rows, which):
    half = ref_rows // 2
    return pl.ds(pl.multiple_of(which * half, 2 * SUBLANES), half)


GATHER_SEMS = 2 * (N_CHIPS - 1)


def _gather_phases(nrows, ns, ins, outs, send_sems, recv_sems):
    n = len(ins)
    x, y, c = _position()
    me = 2 * x + y
    sibling = (x, y, 1 - c)
    chips = [(1 - x, y), (x, 1 - y), (1 - x, 1 - y)]

    def remote(i, k, src, dst, to):
        return pltpu.make_async_remote_copy(src_ref=src, dst_ref=dst, send_sem=send_sems.at[i, k],
                                            recv_sem=recv_sems.at[i, k], device_id=to, device_id_type=MESH)

    def over_ici(i, k):
        px, py = chips[k]
        rows = _half_rows(nrows[i], c) if i < ns else slice(None)
        return remote(i, k, ins[i].at[rows], outs[i].at[me, rows], (px, py, c))

    def landed(i, k, which):
        px, py = chips[k]
        return outs[i].at[2 * px + py, _half_rows(nrows[i], which)] if i < ns else outs[i].at[2 * px + py]

    def start():
        for i in range(n):
            for k in range(N_CHIPS - 1):
                over_ici(i, k).start()

    def forward():
        for i in range(n):
            for k in range(N_CHIPS - 1):
                piece = landed(i, k, c)
                remote(i, k, piece, piece, (*chips[k], c)).wait_recv()
                if i < ns:
                    remote(i, N_CHIPS - 1 + k, piece, piece, sibling).start()

    def finish():
        for i in range(ns):
            for k in range(N_CHIPS - 1):
                piece = landed(i, k, 1 - c)
                remote(i, N_CHIPS - 1 + k, piece, piece, sibling).wait_recv()
        for i in range(n):
            for k in range(N_CHIPS - 1):
                over_ici(i, k).wait_send()
                if i < ns:
                    piece = landed(i, k, c)
                    remote(i, N_CHIPS - 1 + k, piece, piece, sibling).wait_send()

    return start, forward, finish


def _gather_chips_split(name, split, whole):
    arrs = list(split) + list(whole)
    n = len(arrs)

    def body(*refs):
        phases = _gather_phases([a.shape[0] for a in arrs], len(split), refs[:n], refs[n:2 * n], *refs[2 * n:])
        for phase in phases:
            phase()

    hbm = pl.BlockSpec(memory_space=pl.ANY)
    return pl.pallas_call(
        body, name=name, in_specs=[hbm] * n, out_specs=[hbm] * n,
        out_shape=[jax.ShapeDtypeStruct((N_CHIPS,) + a.shape, a.dtype) for a in arrs],
        scratch_shapes=_exchange_scratch(n, GATHER_SEMS),
        compiler_params=pltpu.CompilerParams(has_side_effects=True),
    )(*arrs)


def _swap_cores(name, arrs):
    n = len(arrs)

    def body(*refs):
        ins, outs = refs[:n], refs[n:2 * n]
        send_sems, recv_sems = refs[2 * n:]
        x, y, c = _position()
        copies = [pltpu.make_async_remote_copy(src_ref=ins[i], dst_ref=outs[i], send_sem=send_sems.at[i],
                                               recv_sem=recv_sems.at[i], device_id=(x, y, 1 - c), device_id_type=MESH)
                  for i in range(n)]
        for cp in copies:
            cp.start()
        for cp in copies:
            cp.wait_recv()
        for cp in copies:
            cp.wait_send()

    hbm = pl.BlockSpec(memory_space=pl.ANY)
    return pl.pallas_call(
        body, name=name, in_specs=[hbm] * n, out_specs=[hbm] * n,
        out_shape=[jax.ShapeDtypeStruct(a.shape, a.dtype) for a in arrs],
        scratch_shapes=[pltpu.SemaphoreType.DMA((n,)), pltpu.SemaphoreType.DMA((n,))],
        compiler_params=pltpu.CompilerParams(has_side_effects=True),
    )(*arrs)


def _row_half(a, which, axis):
    half = a.shape[axis] // 2
    return lax.dynamic_slice_in_dim(a, which * half, half, axis=axis)


def _all_reduce(name, pack):
    def body(in_ref, out_ref, buf, send_sems, recv_sems):
        x, y, c = _position()
        me = 4 * x + 2 * y + c
        flips = [(dx, dy, dc) for dx in (0, 1) for dy in (0, 1) for dc in (0, 1) if (dx, dy, dc) != (0, 0, 0)]
        peers = [((1 - x) if dx else x, (1 - y) if dy else y, (1 - c) if dc else c) for dx, dy, dc in flips]
        buf[me] = in_ref[...]
        sends = []
        for k, peer in enumerate(peers):
            cp = pltpu.make_async_remote_copy(src_ref=in_ref, dst_ref=buf.at[me], send_sem=send_sems.at[k],
                                              recv_sem=recv_sems.at[k], device_id=peer, device_id_type=MESH)
            cp.start()
            sends.append(cp)
        for k, (px, py, pc) in enumerate(peers):
            pltpu.make_async_remote_copy(src_ref=in_ref, dst_ref=buf.at[4 * px + 2 * py + pc], send_sem=send_sems.at[k],
                                         recv_sem=recv_sems.at[k], device_id=(px, py, pc), device_id_type=MESH).wait_recv()
        total = buf[0]
        for j in range(1, N_DEV):
            total = total + buf[j]
        out_ref[...] = total
        for cp in sends:
            cp.wait_send()

    vmem = pl.BlockSpec(memory_space=pltpu.VMEM)
    return pl.pallas_call(
        body, name=name, in_specs=[vmem], out_specs=vmem,
        out_shape=jax.ShapeDtypeStruct(pack.shape, F32),
        scratch_shapes=[pltpu.VMEM((N_DEV,) + pack.shape, F32), pltpu.SemaphoreType.DMA((N_DEV - 1,)),
                        pltpu.SemaphoreType.DMA((N_DEV - 1,))],
        compiler_params=pltpu.CompilerParams(has_side_effects=True, vmem_limit_bytes=VMEM_LIMIT_V7X),
    )(pack)


def _pack(arrs):
    rows = [a.reshape(-1, LANES) for a in arrs]
    total = sum(r.shape[0] for r in rows)
    rows.append(jnp.zeros((-total % SUBLANES, LANES), F32))
    return jnp.concatenate(rows, axis=0)


def _unpack(pack, shapes):
    out, off = [], 0
    for s in shapes:
        nrow = 1
        for d in s:
            nrow *= d
        nrow //= LANES
        out.append(pack[off:off + nrow].reshape(s))
        off += nrow
    return out


SMALL = ["mix_norm_w", "gate_bias", "gmlp_ln_w", "gmlp_ln_b", "gmlp_ws", "gmlp_bs", "ssm_conv_w", "ssm_conv_b",
         "ssm_dt_bias", "ssm_a_log", "ssm_d", "ssm_norm_w", "ffn_norm_w", "ffn_conv_w", "ffn_conv_b", "final_norm_w"]
SMALL_SHARDED = ("gate_bias", "ssm_conv_w", "ffn_conv_w")
BIG = ["w_in", "w_proj_a", "w_proj_b", "w_out", "ffn_w_up", "ffn_w_down"]
WEIGHTS = ["mix_norm_w", "w_in", "gate_bias", "gmlp_ln_w", "gmlp_ln_b", "gmlp_ws", "gmlp_bs", "ssm_conv_w",
           "ssm_conv_b", "ssm_dt_bias", "ssm_a_log", "ssm_d", "ssm_norm_w", "w_proj_a", "w_proj_b", "w_out",
           "ffn_norm_w", "ffn_w_up", "ffn_conv_w", "ffn_conv_b", "ffn_w_down", "final_norm_w"]
IN_SPLITS = [0, 2048, 4096, 6144, 9216, 9248]


def _columns_from_chips(stack):
    return jnp.transpose(stack, (1, 0, 2)).reshape(stack.shape[1], -1)


def _columns_to_chips(full, parts=N_CHIPS):
    rows, cols = full.shape
    return jnp.transpose(full.reshape(rows, parts, cols // parts), (1, 0, 2))


def kernel(x, mix_norm_w, w_in, gate_bias, gmlp_ln_w, gmlp_ln_b, gmlp_ws, gmlp_bs, ssm_conv_w, ssm_conv_b, ssm_dt_bias, ssm_a_log, ssm_d, ssm_norm_w, w_proj_a, w_proj_b, w_out, ffn_norm_w, ffn_w_up, ffn_conv_w, ffn_conv_b, ffn_w_down, final_norm_w, loss_target, m_mix_norm_w, m_w_in, m_gate_bias, m_gmlp_ln_w, m_gmlp_ln_b, m_gmlp_ws, m_gmlp_bs, m_ssm_conv_w, m_ssm_conv_b, m_ssm_dt_bias, m_ssm_a_log, m_ssm_d, m_ssm_norm_w, m_w_proj_a, m_w_proj_b, m_w_out, m_ffn_norm_w, m_ffn_w_up, m_ffn_conv_w, m_ffn_conv_b, m_ffn_w_down, m_final_norm_w, v_mix_norm_w, v_w_in, v_gate_bias, v_gmlp_ln_w, v_gmlp_ln_b, v_gmlp_ws, v_gmlp_bs, v_ssm_conv_w, v_ssm_conv_b, v_ssm_dt_bias, v_ssm_a_log, v_ssm_d, v_ssm_norm_w, v_w_proj_a, v_w_proj_b, v_w_out, v_ffn_norm_w, v_ffn_w_up, v_ffn_conv_w, v_ffn_conv_b, v_ffn_w_down, v_final_norm_w):
    args = dict(locals())
    weights = {n: args[n] for n in WEIGHTS}
    moments_m = {n: args["m_" + n] for n in WEIGHTS}
    moments_v = {n: args["v_" + n] for n in WEIGHTS}
    chip = 2 * lax.axis_index("x") + lax.axis_index("y")

    shards = [weights["w_in"][0].astype(BF16)] + [weights[n][0] for n in SMALL_SHARDED]
    gathered = _gather_chips_split("gather_weights", shards[:1], shards[1:])
    w_in_s, gb_s, scw_s, fcw_s = [_own_slot(stack, own) for stack, own in zip(gathered, shards)]
    late_shards = [weights[n][0].astype(BF16) for n in LATE]
    w_in_full = _columns_from_chips(w_in_s)
    full = {"w_" + nm: w_in_full[:, IN_SPLITS[k]:IN_SPLITS[k + 1]] for k, nm in enumerate(["g", "za", "z", "xbc", "dt"])}
    full["gate_bias"] = _columns_from_chips(gb_s)
    full["ssm_conv_w"] = _columns_from_chips(scw_s)
    full["ffn_conv_w"] = _columns_from_chips(fcw_s)
    for n in SMALL:
        if n not in SMALL_SHARDED:
            full[n] = weights[n] if n == "final_norm_w" else weights[n][0]
    for n in ("mix_norm_w", "ffn_norm_w", "ssm_conv_b", "ssm_dt_bias", "ssm_a_log", "ssm_d", "ssm_norm_w", "ffn_conv_b"):
        full[n] = full[n].reshape(1, -1)

    loss_part, grad_x, g, pair, received = _local_step(x[0], loss_target[0], full, late_shards)

    per_head = ["ssm_dt_bias", "ssm_a_log", "ssm_d"]
    rest = [n for n in SMALL if n not in per_head]
    head_row = jnp.concatenate([g[n] for n in per_head] + [jnp.zeros((1, LANES - 3 * SSM_HEADS), F32)], axis=1)
    pack = _pack([loss_part, head_row] + [g[n] for n in rest])
    reduced = _unpack(_all_reduce("reduce_small", pack), [(1, LANES), (1, LANES)] + [g[n].shape for n in rest])
    loss = reduced[0][0, 0]
    small_grads = {n: reduced[1][:, k * SSM_HEADS:(k + 1) * SSM_HEADS] for k, n in enumerate(per_head)}
    for n, r in zip(rest, reduced[2:]):
        if n in SMALL_SHARDED:
            width = weights[n].shape[2]
            r = lax.dynamic_slice_in_dim(r, chip * width, width, axis=1)
        small_grads[n] = r
    two_d = lambda a: a.reshape(-1, a.shape[-1])
    upd = _adamw_small("adamw_small", *[[two_d(d[n]) for n in SMALL]
                                        for d in (weights, small_grads, moments_m, moments_v)])
    small_out = [[small_grads[n] for n in SMALL]] + list(upd)
    small_out = [[a.reshape(weights[n].shape) for n, a in zip(SMALL, kind)] for kind in small_out]

    received = [_own_slot(r, lax.dynamic_index_in_dim(p, chip, 0, keepdims=False)) for r, p in zip(received, pair)]
    halves = [_sum_slots("sum_" + n, r, tm=HALF_TILES[n], rs=2 * SUBLANES) for n, r in zip(BIG, received)]
    tiles = {"w_in": 128, "w_proj_a": 256, "w_proj_b": 256, "w_out": 256, "ffn_w_up": 128, "ffn_w_down": 176}
    core = lax.axis_index("c")
    other = _swap_cores("join_grads", halves)
    reduced = [jnp.concatenate([jnp.where(core == 0, a, b), jnp.where(core == 0, b, a)], axis=0)
               for a, b in zip(halves, other)]
    big_out = {}
    for n, grad in zip(BIG, reduced):
        big_out[n] = _adamw("adamw_" + n, weights[n][0], grad, moments_m[n][0], moments_v[n][0],
                            tm=tiles[n], rs=SUBLANES)

    per_kind = [[], [], [], []]
    for n in WEIGHTS:
        for kind in range(4):
            if n in big_out:
                per_kind[kind].append(big_out[n][kind].reshape(weights[n].shape))
            else:
                per_kind[kind].append(small_out[kind][SMALL.index(n)])
    return (loss, grad_x[None], *per_kind[0], *per_kind[1], *per_kind[2], *per_kind[3])
```

```python
import jax
import jax.numpy as jnp
from jax import lax
from jax.experimental import pallas as pl
from jax.experimental.pallas import tpu as pltpu

F32 = jnp.float32
BF16 = jnp.bfloat16
MESH = pl.DeviceIdType.MESH

EPS = 1e-5
D_MODEL = 1024
GMLP_BLOCK = 128
GMLP_GROUPS = 8
CHUNK = 64
SSM_INNER = 2048
SSM_HEADS = 32
SSM_HEAD_DIM = 64
SSM_GROUPS = 4
SSM_HPG = 8
SSM_STATE = 128
SSM_CONV = 4
SSM_XBC = 3072
D_FF = 2816
FFN_CONV = 3
N_CHIPS = 4
N_DEV = 8

ADAM_LR = 0.001
ADAM_B1 = 0.9
ADAM_B2 = 0.999
ADAM_EPS = 1e-08
ADAM_WD = 0.01
ADAM_STEP = 10

VMEM_LIMIT_V7X = 56 * 1024 * 1024
SUBLANES = 8
LANES = 128


def _params(sem=None):
    return pltpu.CompilerParams(dimension_semantics=sem, vmem_limit_bytes=VMEM_LIMIT_V7X)


def _dot(a, b, ca=1, cb=0):
    return lax.dot_general(a.astype(BF16), b.astype(BF16), (((ca,), (cb,)), ((), ())),
                           preferred_element_type=F32)


def _mm(name, a, b, *, ta=False, tb=False, out_dtype=F32, bm, bn, bk, res=None):
    m, k = (a.shape[1], a.shape[0]) if ta else a.shape
    k2, n = (b.shape[1], b.shape[0]) if tb else b.shape
    assert k == k2 and m % bm == 0 and n % bn == 0 and k % bk == 0, (name, a.shape, b.shape)
    nk = k // bk
    a_spec = (pl.BlockSpec((bk, bm), lambda i, j, kk: (kk, i)) if ta
              else pl.BlockSpec((bm, bk), lambda i, j, kk: (i, kk)))
    b_spec = (pl.BlockSpec((bn, bk), lambda i, j, kk: (j, kk)) if tb
              else pl.BlockSpec((bk, bn), lambda i, j, kk: (kk, j)))
    o_spec = pl.BlockSpec((bm, bn), lambda i, j, kk: (i, j))
    has_res = res is not None

    def body(*refs):
        a_ref, b_ref = refs[0], refs[1]
        r_ref = refs[2] if has_res else None
        o_ref = refs[3] if has_res else refs[2]
        p = _dot(a_ref[...], b_ref[...], 0 if ta else 1, 1 if tb else 0)

        def finish(total):
            if has_res:
                total = total + r_ref[...]
            o_ref[...] = total.astype(out_dtype)

        if nk == 1:
            finish(p)
        else:
            acc_ref = refs[-1]
            kk = pl.program_id(2)

            @pl.when(kk == 0)
            def _():
                acc_ref[...] = p

            @pl.when(kk > 0)
            def _():
                acc_ref[...] += p

            @pl.when(kk == nk - 1)
            def _():
                finish(acc_ref[...])

    return pl.pallas_call(
        body, name=name,
        grid=(m // bm, n // bn, nk),
        in_specs=[a_spec, b_spec] + ([o_spec] if has_res else []),
        out_specs=o_spec,
        out_shape=jax.ShapeDtypeStruct((m, n), out_dtype),
        scratch_shapes=[pltpu.VMEM((bm, bn), F32)] if nk > 1 else [],
        compiler_params=_params(("parallel", "parallel", "arbitrary")),
    )(*([a, b] + ([res] if has_res else [])))


def _mm_sum(name, pairs, *, bm, bk, exchange=()):
    nx = len(exchange)
    npair = len(pairs)
    m, n = pairs[0][0].shape[0], pairs[0][1].shape[0]
    steps, first = [], []
    for a, b in pairs:
        k = a.shape[1]
        assert a.shape[0] == m and b.shape == (n, k) and m % bm == 0 and (k % bk == 0 or k < bk), (name, a.shape, b.shape)
        first.append(sum(steps))
        steps.append(max(k // bk, 1))
    total = sum(steps)
    in_specs = []
    for (a, b), off, cnt in zip(pairs, first, steps):
        width = min(bk, a.shape[1])
        in_specs.append(pl.BlockSpec((bm, width), lambda i, kk, off=off, cnt=cnt: (i, jnp.clip(kk - off, 0, cnt - 1))))
        in_specs.append(pl.BlockSpec((n, width), lambda i, kk, off=off, cnt=cnt: (0, jnp.clip(kk - off, 0, cnt - 1))))

    def body(*refs):
        send_refs = refs[2 * npair:2 * npair + nx]
        o_ref = refs[2 * npair + nx]
        recv_refs = refs[2 * npair + nx + 1:2 * npair + 2 * nx + 1]
        acc_ref = refs[2 * npair + 2 * nx + 1]
        i, kk = pl.program_id(0), pl.program_id(1)
        if nx:
            start, finish = _scatter_phases(send_refs, recv_refs, *refs[2 * npair + 2 * nx + 2:])

            @pl.when((i == 0) & (kk == 0))
            def _():
                start()

        for s, (off, cnt) in enumerate(zip(first, steps)):
            @pl.when((kk >= off) & (kk < off + cnt))
            def _(s=s, off=off):
                p = _dot(refs[2 * s][...], refs[2 * s + 1][...], 1, 1)
                if off == 0:
                    @pl.when(kk == 0)
                    def _():
                        acc_ref[...] = p

                    @pl.when(kk > 0)
                    def _():
                        acc_ref[...] += p
                else:
                    acc_ref[...] += p

        @pl.when(kk == total - 1)
        def _():
            o_ref[...] = acc_ref[...]

        if nx:
            @pl.when((i == m // bm - 1) & (kk == total - 1))
            def _():
                finish()

    hbm = pl.BlockSpec(memory_space=pl.ANY)
    res = pl.pallas_call(
        body, name=name, grid=(m // bm, total),
        in_specs=in_specs + [hbm] * nx, out_specs=[pl.BlockSpec((bm, n), lambda i, kk: (i, 0))] + [hbm] * nx,
        out_shape=[jax.ShapeDtypeStruct((m, n), F32)] + [jax.ShapeDtypeStruct(e.shape, e.dtype) for e in exchange],
        scratch_shapes=[pltpu.VMEM((bm, n), F32)] + (_exchange_scratch(nx, N_CHIPS - 1) if nx else []),
        compiler_params=_params(("arbitrary", "arbitrary")),
    )(*[t for pair in pairs for t in pair], *exchange)
    return (res[0], res[1:]) if nx else res[0]


def _rows(name, fn, ins, params, outs, accs, *, tm, rs, unroll=4):
    nrow = ins[0][0].shape[-2]
    while tm % (rs * unroll):
        unroll //= 2
    assert nrow % tm == 0 and tm % rs == 0, (name, nrow, tm, rs)
    n_in, n_p, n_out, n_acc = len(ins), len(params), len(outs), len(accs)
    in_specs = []
    for spec in ins:
        arr, width, cb = spec[:3]
        if len(spec) == 4:
            in_specs.append(pl.BlockSpec((None, tm, width), lambda i, cb=cb, lead=spec[3]: (lead, i, cb)))
        else:
            in_specs.append(pl.BlockSpec((tm, width), lambda i, cb=cb: (i, cb)))
    for p in params:
        in_specs.append(pl.BlockSpec(p.shape, lambda i, nd=p.ndim: (0,) * nd))
    out_specs = [pl.BlockSpec((tm, w), lambda i: (i, 0)) for w, _ in outs]
    out_specs += [pl.BlockSpec(s, lambda i: (0, 0)) for s in accs]
    out_shape = [jax.ShapeDtypeStruct((nrow, w), dt) for w, dt in outs]
    out_shape += [jax.ShapeDtypeStruct(s, F32) for s in accs]

    def body(*refs):
        in_refs = refs[:n_in]
        p_refs = refs[n_in:n_in + n_p]
        o_refs = refs[n_in + n_p:n_in + n_p + n_out]
        a_refs = refs[n_in + n_p + n_out:]
        pv = [p[...] for p in p_refs]

        if n_acc:
            @pl.when(pl.program_id(0) == 0)
            def _():
                for a_ref in a_refs:
                    a_ref[...] = jnp.zeros_like(a_ref)

        def step(r, carry):
            for u in range(unroll):
                sl = pl.ds(pl.multiple_of((r * unroll + u) * rs, rs), rs)
                vals = [ref[sl, :].astype(F32) for ref in in_refs]
                row_out, sums = fn(*vals, *pv)
                for o_ref, v in zip(o_refs, row_out):
                    o_ref[sl, :] = v.astype(o_ref.dtype)
                carry = tuple(c + s for c, s in zip(carry, sums))
            return carry

        init = tuple(jnp.zeros(s, F32) for s in accs)
        total = lax.fori_loop(0, tm // (rs * unroll), step, init)
        for a_ref, t in zip(a_refs, total):
            a_ref[...] += t

    res = pl.pallas_call(
        body, name=name, grid=(nrow // tm,),
        in_specs=in_specs, out_specs=out_specs, out_shape=out_shape,
        compiler_params=_params(("arbitrary",)),
    )(*([s[0] for s in ins] + list(params)))
    return res


def _rms(x, w):
    return x * lax.rsqrt(jnp.mean(x * x, axis=-1, keepdims=True) + EPS) * w


def _colsum(v):
    return jnp.sum(v, axis=0, keepdims=True)


def _rms_fwd(name, x, w):
    def fn(xv, wv):
        return (_rms(xv, wv),), ()
    return _rows(name, fn, [(x, D_MODEL, 0)], [w], [(D_MODEL, BF16)], [], tm=1024, rs=16)[0]


def _rms_bwd(name, x, w, dy, dres):
    def fn(xv, dyv, drv, wv):
        _, vjp = jax.vjp(_rms, xv, wv)
        dx, dw = vjp(dyv)
        return (drv + dx,), (dw,)
    return _rows(name, fn, [(x, D_MODEL, 0), (dy, D_MODEL, 0), (dres, D_MODEL, 0)], [w],
                 [(D_MODEL, F32)], [(1, D_MODEL)], tm=1024, rs=16)


def _final_loss(name, h, target, w):
    def fn(hv, tv, wv):
        y, vjp = jax.vjp(_rms, hv, wv)
        err = y - tv
        part = 0.5 * jnp.sum(jnp.mean(err * err, axis=-1, keepdims=True), axis=0, keepdims=True)
        dh, dw = vjp(err / D_MODEL)
        return (dh,), (jnp.broadcast_to(part, (1, LANES)), dw)
    return _rows(name, fn, [(h, D_MODEL, 0), (target, D_MODEL, 0)], [w],
                 [(D_MODEL, F32)], [(1, LANES), (1, D_MODEL)], tm=1024, rs=16)


def _merge(ga, gb, ya, yb, b0, b1):
    return jax.nn.sigmoid(ga + b0) * ya + jax.nn.sigmoid(gb + b1) * yb


def _merge_fwd(name, g, ya, yb, b0, b1):
    def fn(ga, gb, yav, ybv, b0v, b1v):
        return (_merge(ga, gb, yav, ybv, b0v, b1v),), ()
    return _rows(name, fn, [(g, D_MODEL, 0), (g, D_MODEL, 1), (ya, D_MODEL, 0), (yb, D_MODEL, 0)],
                 [b0, b1], [(D_MODEL, BF16)], [], tm=1024, rs=16)[0]


def _merge_bwd(name, g, ya, yb, dm, b0, b1):
    def fn(ga, gb, yav, ybv, dmv, b0v, b1v):
        _, vjp = jax.vjp(_merge, ga, gb, yav, ybv, b0v, b1v)
        dga, dgb, dya, dyb, db0, db1 = vjp(dmv)
        return (jnp.concatenate([dga, dgb], axis=1), dya, dyb), (db0, db1)
    return _rows(name, fn,
                 [(g, D_MODEL, 0), (g, D_MODEL, 1), (ya, D_MODEL, 0), (yb, D_MODEL, 0), (dm, D_MODEL, 0)],
                 [b0, b1], [(2 * D_MODEL, BF16), (D_MODEL, BF16), (D_MODEL, BF16)],
                 [(1, D_MODEL), (1, D_MODEL)], tm=1024, rs=16)


GROUP_W = SSM_INNER // SSM_GROUPS


def _gate_norm_group(y, z, nw):
    v = y * jax.nn.silu(z)
    return v * lax.rsqrt(jnp.mean(v * v, axis=-1, keepdims=True) + EPS) * nw


def _gate_norm_fwd(name, y, z, nw):
    def fn(yv, zv, nwv):
        parts = [_gate_norm_group(yv[:, k * GROUP_W:(k + 1) * GROUP_W], zv[:, k * GROUP_W:(k + 1) * GROUP_W],
                                  nwv[:, k * GROUP_W:(k + 1) * GROUP_W]) for k in range(SSM_GROUPS)]
        return (jnp.concatenate(parts, axis=1),), ()
    return _rows(name, fn, [(y, SSM_INNER, 0), (z, SSM_INNER, 0)], [nw], [(SSM_INNER, BF16)], [],
                 tm=512, rs=16)[0]


def _gate_norm_bwd(name, y, z, dout, nw):
    def fn(yv, zv, dv, nwv):
        dys, dzs, dns = [], [], []
        for k in range(SSM_GROUPS):
            sl = slice(k * GROUP_W, (k + 1) * GROUP_W)
            _, vjp = jax.vjp(_gate_norm_group, yv[:, sl], zv[:, sl], nwv[:, sl])
            dy, dz, dn = vjp(dv[:, sl])
            dys.append(dy), dzs.append(dz), dns.append(dn)
        return (jnp.concatenate(dys, axis=1), jnp.concatenate(dzs, axis=1)), (jnp.concatenate(dns, axis=1),)
    return _rows(name, fn, [(y, SSM_INNER, 0), (z, SSM_INNER, 0), (dout, SSM_INNER, 0)], [nw],
                 [(SSM_INNER, BF16), (SSM_INNER, BF16)], [(1, SSM_INNER)], tm=512, rs=16)


def _softplus(v):
    return jnp.maximum(v, 0.0) + jnp.log1p(jnp.exp(-jnp.abs(v)))


def _chunk_cumsum(v, reverse=False):
    row = lax.broadcasted_iota(jnp.int32, v.shape, 0)
    step = 1
    while step < CHUNK:
        if reverse:
            shifted = pltpu.roll(v, CHUNK - step, axis=0)
            v = v + jnp.where(row < CHUNK - step, shifted, 0.0)
        else:
            shifted = pltpu.roll(v, step, axis=0)
            v = v + jnp.where(row >= step, shifted, 0.0)
        step *= 2
    return v


def _dt_prep(name, dt_raw, dt_bias, a_log):
    def fn(rv, bv, alv):
        dt = _softplus(rv + bv)
        return (dt, _chunk_cumsum(dt * (-jnp.exp(alv)))), ()
    return _rows(name, fn, [(dt_raw, SSM_HEADS, 0)], [dt_bias, a_log],
                 [(SSM_HEADS, F32), (SSM_HEADS, F32)], [], tm=512, rs=CHUNK)


def _dt_bwd(name, dt_raw, ddt, da1, da2, dt_bias, a_log):
    def fn(rv, ddv, d1, d2, bv, alv):
        pre = rv + bv
        dt = _softplus(pre)
        a_neg = -jnp.exp(alv)
        back = _chunk_cumsum(d1 + d2, reverse=True)
        d_dt = ddv + back * a_neg
        d_raw = d_dt * jax.nn.sigmoid(pre)
        return (d_raw,), (_colsum(d_raw), _colsum(back * dt) * a_neg)
    return _rows(name, fn, [(dt_raw, SSM_HEADS, 0), (ddt, SSM_HEADS, 0), (da1, SSM_HEADS, 0), (da2, SSM_HEADS, 0)],
                 [dt_bias, a_log], [(SSM_HEADS, BF16)], [(1, SSM_HEADS), (1, SSM_HEADS)], tm=512, rs=CHUNK)


def _adamw_math(w, g, m, v):
    m_new = ADAM_B1 * m + (1.0 - ADAM_B1) * g
    v_new = ADAM_B2 * v + (1.0 - ADAM_B2) * jnp.square(g)
    m_hat = m_new / (1.0 - ADAM_B1 ** ADAM_STEP)
    v_hat = v_new / (1.0 - ADAM_B2 ** ADAM_STEP)
    delta = -ADAM_LR * (m_hat / (jnp.sqrt(v_hat) + ADAM_EPS) + ADAM_WD * w)
    return delta, m_new, v_new


def _adamw(name, w, g, m, v, *, tm, rs):
    width = w.shape[1]

    def fn(wv, mv, vv, gv):
        return (gv,) + _adamw_math(wv, gv, mv, vv), ()
    return _rows(name, fn, [(w, width, 0), (m, width, 0), (v, width, 0), (g, width, 0)],
                 [], [(width, F32)] * 4, [], tm=tm, rs=rs)


def _adamw_small(name, ws, gs, ms, vs):
    n = len(ws)

    def body(*refs):
        w_refs, g_refs, m_refs, v_refs = (refs[k * n:(k + 1) * n] for k in range(4))
        outs = refs[4 * n:]
        for i in range(n):
            res = _adamw_math(w_refs[i][...], g_refs[i][...], m_refs[i][...], v_refs[i][...])
            for k in range(3):
                outs[k * n + i][...] = res[k]

    vmem = pl.BlockSpec(memory_space=pltpu.VMEM)
    res = pl.pallas_call(
        body, name=name, in_specs=[vmem] * (4 * n), out_specs=[vmem] * (3 * n),
        out_shape=[jax.ShapeDtypeStruct(w.shape, F32) for w in ws] * 3,
        compiler_params=pltpu.CompilerParams(vmem_limit_bytes=VMEM_LIMIT_V7X),
    )(*ws, *gs, *ms, *vs)
    return res[:n], res[n:2 * n], res[2 * n:]


def _pair_sum(name, a, b, *, tm):
    shape = a.shape
    flat = (shape[0] * shape[1], shape[2])

    def fn(av, bv):
        return (av.astype(F32) + bv.astype(F32),), ()
    out = _rows(name, fn, [(a.reshape(flat), flat[1], 0), (b.reshape(flat), flat[1], 0)], [], [(flat[1], BF16)], [],
                tm=tm, rs=2 * SUBLANES)[0]
    return out.reshape(shape)


def _sum_slots(name, stack, *, tm, rs):
    width = stack.shape[2]

    def fn(*slots):
        s0, s1, s2, s3 = (s.astype(F32) for s in slots)
        return (((s0 + s1) + s2) + s3,), ()
    return _rows(name, fn, [(stack, width, 0, k) for k in range(N_CHIPS)], [], [(width, F32)], [],
                 tm=tm, rs=rs)[0]


def _layernorm(v, w, b):
    mu = jnp.mean(v, axis=-1, keepdims=True)
    var = jnp.mean(jnp.square(v - mu), axis=-1, keepdims=True)
    return (v - mu) * lax.rsqrt(var + EPS) * w + b


GELU_C = 0.7978845608028654
GELU_A = 0.044715


def _gelu_and_slope(x):
    x2 = x * x
    t = jnp.tanh(GELU_C * x * (1.0 + GELU_A * x2))
    half = 0.5 * (1.0 + t)
    slope = half + 0.5 * x * (1.0 - t * t) * (GELU_C * (1.0 + 3.0 * GELU_A * x2))
    return x * half, slope


def _layernorm_and_back(v, w, b):
    mu = jnp.mean(v, axis=-1, keepdims=True)
    cen = v - mu
    rstd = lax.rsqrt(jnp.mean(cen * cen, axis=-1, keepdims=True) + EPS)
    vhat = cen * rstd

    def back(dout):
        dhat = dout * w
        dv = rstd * (dhat - jnp.mean(dhat, axis=-1, keepdims=True)
                     - vhat * jnp.mean(dhat * vhat, axis=-1, keepdims=True))
        return dv, _colsum(dout * vhat), _colsum(dout)

    return vhat * w + b, back


def _gmlp_mask():
    t = lax.broadcasted_iota(jnp.int32, (GMLP_BLOCK, GMLP_BLOCK), 0) // CHUNK
    s = lax.broadcasted_iota(jnp.int32, (GMLP_BLOCK, GMLP_BLOCK), 1) // CHUNK
    return s <= t


GMLP_TM = 1024


def _gmlp_fwd(name, za, ln_w, ln_b, ws, bs_col):
    nrow = za.shape[0]
    tm = GMLP_TM
    width = GMLP_GROUPS * GMLP_BLOCK

    def body(za_ref, lnw_ref, lnb_ref, ws_ref, bs_ref, o_ref, wm_ref):
        mask = _gmlp_mask()
        for g in range(GMLP_GROUPS):
            wm_ref[g] = jnp.where(mask, ws_ref[g], 0.0).astype(BF16)

        def block(n, carry):
            rows = pl.ds(pl.multiple_of(n * GMLP_BLOCK, GMLP_BLOCK), GMLP_BLOCK)
            for g in range(GMLP_GROUPS):
                cols = slice(g * GMLP_BLOCK, (g + 1) * GMLP_BLOCK)
                vcols = slice(width + g * GMLP_BLOCK, width + (g + 1) * GMLP_BLOCK)
                u = jax.nn.gelu(za_ref[rows, cols].astype(F32))
                v = jax.nn.gelu(za_ref[rows, vcols].astype(F32))
                vn = _layernorm(v, lnw_ref[g:g + 1, :], lnb_ref[g:g + 1, :])
                sv = _dot(wm_ref[g], vn) + bs_ref[g]
                o_ref[rows, cols] = (u * sv).astype(o_ref.dtype)
            return carry

        lax.fori_loop(0, tm // GMLP_BLOCK, block, 0)

    small = lambda a: pl.BlockSpec(a.shape, lambda i, nd=a.ndim: (0,) * nd)
    return pl.pallas_call(
        body, name=name, grid=(nrow // tm,),
        in_specs=[pl.BlockSpec((tm, 2 * width), lambda i: (i, 0)), small(ln_w), small(ln_b), small(ws), small(bs_col)],
        out_specs=pl.BlockSpec((tm, width), lambda i: (i, 0)),
        out_shape=jax.ShapeDtypeStruct((nrow, width), BF16),
        scratch_shapes=[pltpu.VMEM((GMLP_GROUPS, GMLP_BLOCK, GMLP_BLOCK), BF16)],
        compiler_params=_params(("arbitrary",)),
    )(za, ln_w, ln_b, ws, bs_col)


def _gmlp_bwd(name, za, dout, ln_w, ln_b, ws, bs_col):
    nrow = za.shape[0]
    tm = GMLP_TM
    width = GMLP_GROUPS * GMLP_BLOCK

    def body(za_ref, do_ref, lnw_ref, lnb_ref, ws_ref, bs_ref, dza_ref, dlnw_ref, dlnb_ref, dws_ref, dbs_ref, wm_ref):
        mask = _gmlp_mask()
        for g in range(GMLP_GROUPS):
            wm_ref[g] = jnp.where(mask, ws_ref[g], 0.0).astype(BF16)

        @pl.when(pl.program_id(0) == 0)
        def _():
            dlnw_ref[...] = jnp.zeros_like(dlnw_ref)
            dlnb_ref[...] = jnp.zeros_like(dlnb_ref)
            dws_ref[...] = jnp.zeros_like(dws_ref)
            dbs_ref[...] = jnp.zeros_like(dbs_ref)

        def block(n, carry):
            rows = pl.ds(pl.multiple_of(n * GMLP_BLOCK, GMLP_BLOCK), GMLP_BLOCK)
            for g in range(GMLP_GROUPS):
                cols = slice(g * GMLP_BLOCK, (g + 1) * GMLP_BLOCK)
                vcols = slice(width + g * GMLP_BLOCK, width + (g + 1) * GMLP_BLOCK)
                u, slope_u = _gelu_and_slope(za_ref[rows, cols].astype(F32))
                v, slope_v = _gelu_and_slope(za_ref[rows, vcols].astype(F32))
                vn, ln_back = _layernorm_and_back(v, lnw_ref[g:g + 1, :], lnb_ref[g:g + 1, :])
                sv = _dot(wm_ref[g], vn) + bs_ref[g]
                d_o = do_ref[rows, cols].astype(F32)
                dsv = d_o * u
                d_wm = _dot(dsv, vn, 1, 1)
                dvn = _dot(wm_ref[g], dsv, 0, 0)
                dv, dlnw, dlnb = ln_back(dvn)
                dza_ref[rows, cols] = (d_o * sv * slope_u).astype(dza_ref.dtype)
                dza_ref[rows, vcols] = (dv * slope_v).astype(dza_ref.dtype)
                dlnw_ref[g:g + 1, :] += dlnw
                dlnb_ref[g:g + 1, :] += dlnb
                dws_ref[g] += jnp.where(mask, d_wm, 0.0)
                dbs_ref[g] += jnp.sum(dsv, axis=1, keepdims=True)
            return carry

        lax.fori_loop(0, tm // GMLP_BLOCK, block, 0)

    small = lambda a: pl.BlockSpec(a.shape, lambda i, nd=a.ndim: (0,) * nd)
    return pl.pallas_call(
        body, name=name, grid=(nrow // tm,),
        in_specs=[pl.BlockSpec((tm, 2 * width), lambda i: (i, 0)), pl.BlockSpec((tm, width), lambda i: (i, 0)),
                  small(ln_w), small(ln_b), small(ws), small(bs_col)],
        out_specs=[pl.BlockSpec((tm, 2 * width), lambda i: (i, 0)), small(ln_w), small(ln_b), small(ws), small(bs_col)],
        out_shape=[jax.ShapeDtypeStruct((nrow, 2 * width), BF16), jax.ShapeDtypeStruct(ln_w.shape, F32),
                   jax.ShapeDtypeStruct(ln_b.shape, F32), jax.ShapeDtypeStruct(ws.shape, F32),
                   jax.ShapeDtypeStruct(bs_col.shape, F32)],
        scratch_shapes=[pltpu.VMEM((GMLP_GROUPS, GMLP_BLOCK, GMLP_BLOCK), BF16)],
        compiler_params=_params(("arbitrary",)),
    )(za, dout, ln_w, ln_b, ws, bs_col)


CONV_TM = 512
CONV_RS = 32
HALO = 2 * SUBLANES


def _tap_rows(w_ref):
    return [w_ref[k:k + 1, :] for k in range(w_ref.shape[0])]


def _halo_specs(nrow, tm, tc):
    per = tm // HALO
    last = nrow // HALO - 1
    main = pl.BlockSpec((tm, tc), lambda j, i: (i, j))
    before = pl.BlockSpec((HALO, tc), lambda j, i: (jnp.maximum(i * per - 1, 0), j))
    after = pl.BlockSpec((HALO, tc), lambda j, i: (jnp.minimum((i + 1) * per, last), j))
    return main, before, after


def _col_spec(rows, tc):
    return pl.BlockSpec((rows, tc), lambda j, i: (0, j))


def _conv_fwd(name, x, w, b, *, tc):
    nrow, ncol = x.shape
    taps = w.shape[0]
    tm, rs = CONV_TM, CONV_RS
    main, before, _ = _halo_specs(nrow, tm, tc)

    def body(x_ref, xb_ref, w_ref, b_ref, o_ref, xw_ref):
        first = pl.program_id(1) == 0
        wv, bv = _tap_rows(w_ref), b_ref[...]
        xw_ref[0:HALO, :] = jnp.where(first, 0.0, xb_ref[...].astype(F32))
        for r in range(tm // rs):
            xw_ref[HALO + r * rs:HALO + (r + 1) * rs, :] = x_ref[r * rs:(r + 1) * rs, :].astype(F32)
        for r in range(tm // rs):
            base = HALO + r * rs
            out = bv + wv[taps - 1] * xw_ref[base:base + rs, :]
            for k in range(taps - 1):
                back = taps - 1 - k
                out = out + wv[k] * xw_ref[base - back:base - back + rs, :]
            o_ref[r * rs:(r + 1) * rs, :] = out.astype(o_ref.dtype)

    return pl.pallas_call(
        body, name=name, grid=(ncol // tc, nrow // tm),
        in_specs=[main, before, _col_spec(taps, tc), _col_spec(1, tc)],
        out_specs=main, out_shape=jax.ShapeDtypeStruct((nrow, ncol), BF16),
        scratch_shapes=[pltpu.VMEM((HALO + tm, tc), F32)],
        compiler_params=_params(("parallel", "arbitrary")),
    )(x, x, w, b)


def _conv_bwd(name, dpre, x, w, *, tc):
    nrow, ncol = x.shape
    taps = w.shape[0]
    tm, rs = CONV_TM, CONV_RS
    nsub = tm // rs
    main, before, after = _halo_specs(nrow, tm, tc)

    def fold(v):
        total = v[0:SUBLANES]
        for q in range(1, rs // SUBLANES):
            total = total + v[q * SUBLANES:(q + 1) * SUBLANES]
        return total

    def body(d_ref, da_ref, x_ref, xb_ref, w_ref, dx_ref, dw_ref, db_ref, dwin_ref, xwin_ref):
        i = pl.program_id(1)
        first, last = i == 0, i == pl.num_programs(1) - 1
        wv = _tap_rows(w_ref)

        @pl.when(first)
        def _():
            dw_ref[...] = jnp.zeros_like(dw_ref)
            db_ref[...] = jnp.zeros_like(db_ref)

        xwin_ref[0:HALO, :] = jnp.where(first, 0.0, xb_ref[...].astype(F32))
        dwin_ref[tm:, :] = jnp.where(last, 0.0, da_ref[...].astype(F32))
        for r in range(nsub):
            dwin_ref[r * rs:(r + 1) * rs, :] = d_ref[r * rs:(r + 1) * rs, :].astype(F32)
            xwin_ref[HALO + r * rs:HALO + (r + 1) * rs, :] = x_ref[r * rs:(r + 1) * rs, :].astype(F32)
        dw = [jnp.zeros((SUBLANES, tc), F32)] * taps
        db = jnp.zeros((SUBLANES, tc), F32)
        for r in range(nsub):
            cur = dwin_ref[r * rs:(r + 1) * rs, :]
            dx = wv[taps - 1] * cur
            for k in range(taps - 1):
                ahead = taps - 1 - k
                dx = dx + wv[k] * dwin_ref[r * rs + ahead:(r + 1) * rs + ahead, :]
            dx_ref[r * rs:(r + 1) * rs, :] = dx.astype(dx_ref.dtype)
            for k in range(taps):
                back = taps - 1 - k
                dw[k] = dw[k] + fold(cur * xwin_ref[HALO + r * rs - back:HALO + (r + 1) * rs - back, :])
            db = db + fold(cur)
        for k in range(taps):
            dw_ref[k:k + 1, :] += _colsum(dw[k])
        db_ref[...] += _colsum(db)

    return pl.pallas_call(
        body, name=name, grid=(ncol // tc, nrow // tm),
        in_specs=[main, after, main, before, _col_spec(taps, tc)],
        out_specs=[main, _col_spec(taps, tc), _col_spec(1, tc)],
        out_shape=[jax.ShapeDtypeStruct((nrow, ncol), BF16), jax.ShapeDtypeStruct((taps, ncol), F32),
                   jax.ShapeDtypeStruct((1, ncol), F32)],
        scratch_shapes=[pltpu.VMEM((tm + HALO, tc), F32), pltpu.VMEM((HALO + tm, tc), F32)],
        compiler_params=_params(("parallel", "arbitrary")),
    )(dpre, dpre, x, x, w)


def _glu(gate, val):
    return jax.nn.silu(gate) * val


def _ffn_act_fwd(name, pg, pv, wg, wv, bg, bv, *, tc):
    nrow, ncol = pg.shape
    taps = wg.shape[0]
    tm, rs = CONV_TM, CONV_RS
    main, before, _ = _halo_specs(nrow, tm, tc)

    def body(pg_ref, pgb_ref, pv_ref, pvb_ref, wg_ref, wv_ref, bg_ref, bv_ref, g_ref, v_ref, a_ref, gwin_ref, vwin_ref):
        first = pl.program_id(1) == 0
        taps_g, taps_v, bgv, bvv = _tap_rows(wg_ref), _tap_rows(wv_ref), bg_ref[...], bv_ref[...]
        gwin_ref[0:HALO, :] = jnp.where(first, 0.0, pgb_ref[...].astype(F32))
        vwin_ref[0:HALO, :] = jnp.where(first, 0.0, pvb_ref[...].astype(F32))
        for r in range(tm // rs):
            gwin_ref[HALO + r * rs:HALO + (r + 1) * rs, :] = pg_ref[r * rs:(r + 1) * rs, :].astype(F32)
            vwin_ref[HALO + r * rs:HALO + (r + 1) * rs, :] = pv_ref[r * rs:(r + 1) * rs, :].astype(F32)

        def conv(win_ref, tap_rows, bias, r):
            base = HALO + r * rs
            out = bias + tap_rows[taps - 1] * win_ref[base:base + rs, :]
            for k in range(taps - 1):
                back = taps - 1 - k
                out = out + tap_rows[k] * win_ref[base - back:base - back + rs, :]
            return out

        for r in range(tm // rs):
            sl = slice(r * rs, (r + 1) * rs)
            gate, val = conv(gwin_ref, taps_g, bgv, r), conv(vwin_ref, taps_v, bvv, r)
            g_ref[sl, :] = gate.astype(g_ref.dtype)
            v_ref[sl, :] = val.astype(v_ref.dtype)
            a_ref[sl, :] = _glu(gate, val).astype(a_ref.dtype)

    return pl.pallas_call(
        body, name=name, grid=(ncol // tc, nrow // tm),
        in_specs=[main, before, main, before, _col_spec(taps, tc), _col_spec(taps, tc), _col_spec(1, tc), _col_spec(1, tc)],
        out_specs=[main, main, main],
        out_shape=[jax.ShapeDtypeStruct((nrow, ncol), BF16)] * 3,
        scratch_shapes=[pltpu.VMEM((HALO + tm, tc), F32), pltpu.VMEM((HALO + tm, tc), F32)],
        compiler_params=_params(("parallel", "arbitrary")),
    )(pg, pg, pv, pv, wg, wv, bg, bv)


def _ffn_act_bwd(name, dact, gate, val):
    def fn(dv, gv, vv):
        _, vjp = jax.vjp(_glu, gv, vv)
        dg, dval = vjp(dv)
        return (dg, dval), ()
    width = dact.shape[1]
    return _rows(name, fn, [(dact, width, 0), (gate, width, 0), (val, width, 0)], [],
                 [(width, BF16), (width, BF16)], [], tm=512, rs=2 * SUBLANES)


SSD_TM = 512
SSD_CHUNKS = SSD_TM // CHUNK
X_OFF, B_OFF, C_OFF = 0, SSM_INNER, SSM_INNER + SSM_GROUPS * SSM_STATE
HP = SSM_HPG * SSM_HEAD_DIM


def _causal_tiled():
    row = lax.broadcasted_iota(jnp.int32, (CHUNK, HP), 0)
    src = lax.broadcasted_iota(jnp.int32, (CHUNK, HP), 1) & (CHUNK - 1)
    return src <= row


def _split2(v):
    hi = v.astype(BF16)
    return hi, (v - hi.astype(F32)).astype(BF16)


def _dot_exact(a, ind):
    hi, lo = (lax.dot_general(p, ind, (((1,), (0,)), ((), ())), preferred_element_type=F32) for p in _split2(a))
    return hi + lo


def _head_indicator():
    head = lax.broadcasted_iota(jnp.int32, (SSM_HEADS, SSM_INNER), 0)
    chan = lax.broadcasted_iota(jnp.int32, (SSM_HEADS, SSM_INNER), 1)
    return (chan // SSM_HEAD_DIM == head).astype(BF16)


def _chunk_decays(ci, dt_ref, ac_ref, ind, ax_ref, dtx_ref, eax_ref, eex_ref, tail_ref):
    rows = pl.ds(pl.multiple_of(ci * CHUNK, CHUNK), CHUNK)
    ax_ref[...] = _dot_exact(ac_ref[rows, :], ind)
    dtx_ref[...] = _dot_exact(dt_ref[rows, :], ind)
    eax_ref[...] = jnp.exp(ax_ref[...])
    eex_ref[...] = jnp.exp(ax_ref[CHUNK - 1:CHUNK, :] - ax_ref[...])
    tail = pl.ds(pl.multiple_of(ci * CHUNK + CHUNK - SUBLANES, SUBLANES), SUBLANES)
    tail_ref[...] = jnp.exp(ac_ref[tail, :])


def _group_decay(ci, g, ax_ref, af_ref, xbc_ref, causal):
    gcols = slice(g * HP, (g + 1) * HP)
    bm = xbc_ref[:, B_OFF + g * SSM_STATE:B_OFF + (g + 1) * SSM_STATE]
    cm = xbc_ref[:, C_OFF + g * SSM_STATE:C_OFF + (g + 1) * SSM_STATE]
    cb_tiled = _dot(cm, jnp.concatenate([bm] * SSM_HPG, axis=0), 1, 1)
    seg = ax_ref[:, gcols] - af_ref[ci, :, gcols]
    decay = jnp.where(causal, jnp.exp(jnp.where(causal, seg, 0.0)), 0.0)
    return bm, cm, cb_tiled * decay, decay


def _ssd_fwd(name, pre, dt, a_cum, a_flat, d_x, ind, shards):
    nrow = pre.shape[0]
    tm = SSD_TM
    nstep = nrow // tm
    ng = len(shards)

    def body(pre_ref, dt_ref, ac_ref, af_ref, dx_ref, ind_ref, *rest):
        shard_refs, (y_ref, st_ref), stack_refs = rest[:ng], rest[ng:ng + 2], rest[ng + 2:2 * ng + 2]
        (h_ref, xbc_ref, ax_ref, dtx_ref, eax_ref, eex_ref, m_ref, xd_ref, yd_ref, tail_ref,
         send_sems, recv_sems) = rest[2 * ng + 2:]
        step = pl.program_id(0)
        start, forward, finish = _gather_phases([s.shape[0] for s in shards], ng, shard_refs, stack_refs,
                                                send_sems, recv_sems)

        @pl.when(step == 0)
        def _():
            h_ref[...] = jnp.zeros_like(h_ref)
            start()

        @pl.when(step == nstep // 2)
        def _():
            forward()

        causal = _causal_tiled()
        ind = ind_ref[...]

        def chunk(ci, carry):
            rows = pl.ds(pl.multiple_of(ci * CHUNK, CHUNK), CHUNK)
            xbc_ref[...] = jax.nn.silu(pre_ref[rows, :].astype(F32))
            _chunk_decays(ci, dt_ref, ac_ref, ind, ax_ref, dtx_ref, eax_ref, eex_ref, tail_ref)
            st_ref[ci] = h_ref[...].astype(st_ref.dtype)
            for g in range(SSM_GROUPS):
                gcols = slice(g * HP, (g + 1) * HP)
                bm, cm, m_all, _ = _group_decay(ci, g, ax_ref, af_ref, xbc_ref, causal)
                m_ref[...] = m_all
                x_g = xbc_ref[:, gcols]
                xd = x_g * dtx_ref[:, gcols]
                xd_ref[...] = xd
                h_g = h_ref[gcols, :]
                for hh in range(SSM_HPG):
                    lc = slice(hh * SSM_HEAD_DIM, (hh + 1) * SSM_HEAD_DIM)
                    yd_ref[:, lc] = _dot(m_ref[:, lc], xd_ref[:, lc])
                y_ref[rows, gcols] = (yd_ref[...] + _dot(cm, h_g, 1, 1) * eax_ref[:, gcols]
                                      + dx_ref[:, gcols] * x_g).astype(y_ref.dtype)
                new = _dot(xd * eex_ref[:, gcols], bm, 0, 0)
                for hh in range(SSM_HPG):
                    h = g * SSM_HPG + hh
                    hrows = slice(h * SSM_HEAD_DIM, (h + 1) * SSM_HEAD_DIM)
                    lrows = slice(hh * SSM_HEAD_DIM, (hh + 1) * SSM_HEAD_DIM)
                    h_ref[hrows, :] = tail_ref[SUBLANES - 1:SUBLANES, h:h + 1] * h_ref[hrows, :] + new[lrows, :]
            return carry

        lax.fori_loop(0, SSD_CHUNKS, chunk, 0)

        @pl.when(step == nstep - 1)
        def _():
            finish()

    nchunk = nrow // CHUNK
    whole = lambda a: pl.BlockSpec(a.shape, lambda i, nd=a.ndim: (0,) * nd)
    hbm = pl.BlockSpec(memory_space=pl.ANY)
    wide = lambda: pltpu.VMEM((CHUNK, SSM_INNER), F32)
    group = lambda: pltpu.VMEM((CHUNK, HP), F32)
    res = pl.pallas_call(
        body, name=name, grid=(nstep,),
        in_specs=[pl.BlockSpec((tm, SSM_XBC), lambda i: (i, 0)), pl.BlockSpec((tm, SSM_HEADS), lambda i: (i, 0)),
                  pl.BlockSpec((tm, SSM_HEADS), lambda i: (i, 0)),
                  pl.BlockSpec((SSD_CHUNKS, 1, SSM_INNER), lambda i: (i, 0, 0)), whole(d_x), whole(ind)] + [hbm] * ng,
        out_specs=[pl.BlockSpec((tm, SSM_INNER), lambda i: (i, 0)),
                   pl.BlockSpec((SSD_CHUNKS, SSM_INNER, SSM_STATE), lambda i: (i, 0, 0))] + [hbm] * ng,
        out_shape=[jax.ShapeDtypeStruct((nrow, SSM_INNER), BF16),
                   jax.ShapeDtypeStruct((nchunk, SSM_INNER, SSM_STATE), BF16)]
        + [jax.ShapeDtypeStruct((N_CHIPS,) + s.shape, s.dtype) for s in shards],
        scratch_shapes=[pltpu.VMEM((SSM_INNER, SSM_STATE), F32), pltpu.VMEM((CHUNK, SSM_XBC), F32),
                        wide(), wide(), wide(), wide(), group(), group(), group(),
                        pltpu.VMEM((SUBLANES, SSM_HEADS), F32)] + _exchange_scratch(ng, GATHER_SEMS),
        compiler_params=_params(("arbitrary",)),
    )(pre, dt, a_cum, a_flat, d_x, ind, *shards)
    return res[0], res[1], res[2:]


def _ssd_bwd(name, pre, dt, a_cum, a_flat, d_x, ind, ind_t, states, dy, pairs):
    nrow = pre.shape[0]
    tm = SSD_TM
    ntile = nrow // tm
    npair = len(pairs)

    def body(pre_ref, dt_ref, ac_ref, af_ref, dx_ref, ind_ref, indt_ref, st_ref, dy_ref, *rest):
        pair_refs = rest[:npair]
        dpre_ref, ddt_ref, da_ref, daf_ref, dd_ref = rest[npair:npair + 5]
        recv_refs = rest[npair + 5:2 * npair + 5]
        (dh_ref, xbc_ref, dxbc_ref, ax_ref, dtx_ref, eax_ref, eex_ref, red_ref,
         m_ref, l_ref, xd_ref, dm_ref, dxd_ref, fold_ref, hd_ref, tail_ref, send_sems, recv_sems) = rest[2 * npair + 5:]
        start, finish = _scatter_phases(pair_refs, recv_refs, send_sems, recv_sems)

        @pl.when(pl.program_id(0) == 0)
        def _():
            dh_ref[...] = jnp.zeros_like(dh_ref)
            dd_ref[...] = jnp.zeros_like(dd_ref)
            start()

        causal = _causal_tiled()
        ind, ind_t = ind_ref[...], indt_ref[...]
        is_last_row = lax.broadcasted_iota(jnp.int32, (CHUNK, 1), 0) == CHUNK - 1
        ones = jnp.ones((CHUNK, SSM_STATE), BF16)

        def chunk(k, ddx):
            ci = SSD_CHUNKS - 1 - k
            rows = pl.ds(pl.multiple_of(ci * CHUNK, CHUNK), CHUNK)
            pre_v = pre_ref[rows, :].astype(F32)
            xbc_ref[...] = jax.nn.silu(pre_v)
            _chunk_decays(ci, dt_ref, ac_ref, ind, ax_ref, dtx_ref, eax_ref, eex_ref, tail_ref)
            ddx_parts = []
            for g in range(SSM_GROUPS):
                gcols = slice(g * HP, (g + 1) * HP)
                bcols = slice(B_OFF + g * SSM_STATE, B_OFF + (g + 1) * SSM_STATE)
                ccols = slice(C_OFF + g * SSM_STATE, C_OFF + (g + 1) * SSM_STATE)
                bm, cm, m_all, decay = _group_decay(ci, g, ax_ref, af_ref, xbc_ref, causal)
                m_ref[...] = m_all
                l_ref[...] = decay
                x_g = xbc_ref[:, gcols]
                xd = x_g * dtx_ref[:, gcols]
                xd_ref[...] = xd
                h_g = st_ref[ci, gcols, :]
                dh_g = dh_ref[gcols, :]
                dy_g = dy_ref[rows, gcols]
                for hh in range(SSM_HPG):
                    h = g * SSM_HPG + hh
                    hcols = slice(h * SSM_HEAD_DIM, (h + 1) * SSM_HEAD_DIM)
                    lc = slice(hh * SSM_HEAD_DIM, (hh + 1) * SSM_HEAD_DIM)
                    dy_h = dy_ref[rows, hcols]
                    dm_ref[:, lc] = _dot(dy_h, xd_ref[:, lc], 1, 1)
                    dxd_ref[:, lc] = _dot(m_ref[:, lc], dy_h, 0, 0)
                ebdh = eex_ref[:, gcols] * _dot(bm, dh_g, 1, 1)
                dxd = dxd_ref[...] + ebdh
                dm = dm_ref[...]
                t = dm * l_ref[...]
                t128 = (t[:, 0:LANES] + t[:, LANES:2 * LANES]) + (t[:, 2 * LANES:3 * LANES] + t[:, 3 * LANES:])
                fold_ref[...] = t128 + pltpu.roll(t128, CHUNK, axis=1)
                dw_sum = fold_ref[:, 0:CHUNK]
                q = dm * m_ref[...]
                dyea = dy_g * eax_ref[:, gcols]
                red_ref[0:CHUNK, gcols] = q + dyea * _dot(cm, h_g, 1, 1)
                red_ref[CHUNK:2 * CHUNK, gcols] = xd * ebdh
                red_ref[2 * CHUNK:3 * CHUNK, gcols] = dxd * x_g
                daf_ref[ci, :, gcols] = -jnp.sum(q, axis=0, keepdims=True)
                ddx_parts.append(jnp.sum(dy_g * x_g, axis=0, keepdims=True))
                dxbc_ref[:, gcols] = dxd * dtx_ref[:, gcols] + dx_ref[:, gcols] * dy_g
                dxbc_ref[:, ccols] = _dot(dw_sum, bm) + _dot(dyea, h_g)
                dxbc_ref[:, bcols] = _dot(dw_sum, cm, 0, 0) + _dot(xd * eex_ref[:, gcols], dh_g)
                dh_new = _dot(dyea, cm, 0, 0)
                for hh in range(SSM_HPG):
                    h = g * SSM_HPG + hh
                    hrows = slice(h * SSM_HEAD_DIM, (h + 1) * SSM_HEAD_DIM)
                    lrows = slice(hh * SSM_HEAD_DIM, (hh + 1) * SSM_HEAD_DIM)
                    hd_ref[h:h + 1, :] = jnp.sum(st_ref[ci, hrows, :] * dh_ref[hrows, :], axis=0, keepdims=True)
                    dh_ref[hrows, :] = tail_ref[SUBLANES - 1:SUBLANES, h:h + 1] * dh_ref[hrows, :] + dh_new[lrows, :]
            sums = _dot_exact(red_ref[...], ind_t)
            ra, ts = sums[:CHUNK], sums[CHUNK:2 * CHUNK]
            hdh = sum(lax.dot_general(ones, p, (((1,), (1,)), ((), ())), preferred_element_type=F32)
                      for p in _split2(hd_ref[...]))
            da_last = jnp.sum(ts, axis=0, keepdims=True) + tail_ref[SUBLANES - 1:SUBLANES, :] * hdh
            da_ref[rows, :] = ra - ts + jnp.where(is_last_row, da_last, 0.0)
            ddt_ref[rows, :] = sums[2 * CHUNK:]
            sig = jax.nn.sigmoid(pre_v)
            dpre_ref[rows, :] = (dxbc_ref[...] * (sig * (1.0 + pre_v * (1.0 - sig)))).astype(dpre_ref.dtype)
            return ddx + jnp.concatenate(ddx_parts, axis=1)

        ddx = lax.fori_loop(0, SSD_CHUNKS, chunk, jnp.zeros((1, SSM_INNER), F32))
        dd_ref[...] += _dot_exact(jnp.broadcast_to(ddx, (SUBLANES, SSM_INNER)), ind_t)

        @pl.when(pl.program_id(0) == ntile - 1)
        def _():
            finish()

    rev = lambda i: ntile - 1 - i
    whole = lambda a: pl.BlockSpec(a.shape, lambda i, nd=a.ndim: (0,) * nd)
    hbm = pl.BlockSpec(memory_space=pl.ANY)
    wide = lambda: pltpu.VMEM((CHUNK, SSM_INNER), F32)
    group = lambda: pltpu.VMEM((CHUNK, HP), F32)
    res = pl.pallas_call(
        body, name=name, grid=(ntile,),
        in_specs=[pl.BlockSpec((tm, SSM_XBC), lambda i: (rev(i), 0)), pl.BlockSpec((tm, SSM_HEADS), lambda i: (rev(i), 0)),
                  pl.BlockSpec((tm, SSM_HEADS), lambda i: (rev(i), 0)),
                  pl.BlockSpec((SSD_CHUNKS, 1, SSM_INNER), lambda i: (rev(i), 0, 0)),
                  whole(d_x), whole(ind), whole(ind_t),
                  pl.BlockSpec((SSD_CHUNKS, SSM_INNER, SSM_STATE), lambda i: (rev(i), 0, 0)),
                  pl.BlockSpec((tm, SSM_INNER), lambda i: (rev(i), 0))] + [hbm] * npair,
        out_specs=[pl.BlockSpec((tm, SSM_XBC), lambda i: (rev(i), 0)), pl.BlockSpec((tm, SSM_HEADS), lambda i: (rev(i), 0)),
                   pl.BlockSpec((tm, SSM_HEADS), lambda i: (rev(i), 0)),
                   pl.BlockSpec((SSD_CHUNKS, 1, SSM_INNER), lambda i: (rev(i), 0, 0)),
                   pl.BlockSpec((SUBLANES, SSM_HEADS), lambda i: (0, 0))] + [hbm] * npair,
        out_shape=[jax.ShapeDtypeStruct((nrow, SSM_XBC), BF16), jax.ShapeDtypeStruct((nrow, SSM_HEADS), F32),
                   jax.ShapeDtypeStruct((nrow, SSM_HEADS), F32), jax.ShapeDtypeStruct((nrow // CHUNK, 1, SSM_INNER), F32),
                   jax.ShapeDtypeStruct((SUBLANES, SSM_HEADS), F32)]
        + [jax.ShapeDtypeStruct(p.shape, p.dtype) for p in pairs],
        scratch_shapes=[pltpu.VMEM((SSM_INNER, SSM_STATE), F32), pltpu.VMEM((CHUNK, SSM_XBC), F32),
                        pltpu.VMEM((CHUNK, SSM_XBC), F32), wide(), wide(), wide(), wide(),
                        pltpu.VMEM((3 * CHUNK, SSM_INNER), F32),
                        group(), group(), group(), group(), group(), pltpu.VMEM((CHUNK, LANES), F32),
                        pltpu.VMEM((SSM_HEADS, SSM_STATE), F32), pltpu.VMEM((SUBLANES, SSM_HEADS), F32)]
        + _exchange_scratch(npair, N_CHIPS - 1),
        compiler_params=_params(("arbitrary",)),
    )(pre, dt, a_cum, a_flat, d_x, ind, ind_t, states, dy, *pairs)
    return res[:5], res[5:]


LATE = ["w_proj_a", "w_proj_b", "w_out", "ffn_w_up", "ffn_w_down"]
HALF_TILES = {"w_in": 128, "w_proj_a": 128, "w_proj_b": 256, "w_out": 128, "ffn_w_up": 128, "ffn_w_down": 176}


def _late_weights(stacks, shards):
    pa, pb, out, up, down = [_own_slot(stack, own) for stack, own in zip(stacks, shards)]
    return {"w_proj_a": pa.reshape(-1, D_MODEL), "w_proj_b": pb.reshape(-1, D_MODEL), "w_out": out.reshape(-1, D_MODEL),
            "w_up_g": _columns_from_chips(up[:2]), "w_up_v": _columns_from_chips(up[2:]),
            "w_down": down.reshape(-1, D_MODEL)}


def _pair_reduce(tag, names, stacks):
    core = lax.axis_index("c")
    own_half = [_row_half(s, core, 1) for s in stacks]
    other_half = _swap_cores("pair_grads_" + tag, [_row_half(s, 1 - core, 1) for s in stacks])
    return [_pair_sum("pair_" + n, a, b, tm=HALF_TILES[n]) for n, a, b in zip(names, own_half, other_half)]


def _local_step(x, target, w, late_shards):
    w = dict(w)
    g = {}
    bs_col = w["gmlp_bs"].reshape(GMLP_GROUPS, GMLP_BLOCK, 1)
    b0, b1 = w["gate_bias"][0:1], w["gate_bias"][1:2]

    xn = _rms_fwd("mix_norm", x, w["mix_norm_w"])
    big = dict(bm=1024, bn=1024, bk=1024)
    act16 = dict(out_dtype=BF16, **big)
    gates = _mm("in_gates", xn, w["w_g"], **act16)
    za = _mm("in_gmlp", xn, w["w_za"], **act16)
    z = _mm("in_z", xn, w["w_z"], **act16)
    xbc = _mm("in_xbc", xn, w["w_xbc"], **act16)
    dt_raw = _mm("in_dt", xn, w["w_dt"], bm=1024, bn=SSM_HEADS, bk=1024)

    pre = _conv_fwd("ssm_conv_fwd", xbc, w["ssm_conv_w"], w["ssm_conv_b"], tc=1024)
    dt, a_cum = _dt_prep("dt_prep", dt_raw, w["ssm_dt_bias"], w["ssm_a_log"])
    a_flat = jnp.transpose(a_cum.reshape(-1, CHUNK, SSM_HEADS), (0, 2, 1)).reshape(-1, 1, SSM_INNER)
    d_x = jnp.repeat(w["ssm_d"], SSM_HEAD_DIM, axis=1)
    ind = _head_indicator()
    y_ssd, states, late_stacks = _ssd_fwd("ssd_fwd", pre, dt, a_cum, a_flat, d_x, ind, late_shards)
    w.update(_late_weights(late_stacks, late_shards))
    yb_pre = _gate_norm_fwd("gate_norm_fwd", y_ssd, z, w["ssm_norm_w"])
    y_b = _mm("proj_b", yb_pre, w["w_proj_b"], bm=1024, bn=1024, bk=SSM_INNER, out_dtype=BF16)

    ya_pre = _gmlp_fwd("gmlp_fwd", za, w["gmlp_ln_w"], w["gmlp_ln_b"], w["gmlp_ws"], bs_col)
    y_a = _mm("proj_a", ya_pre, w["w_proj_a"], **act16)

    merged = _merge_fwd("merge_fwd", gates, y_a, y_b, b0, b1)
    h1 = _mm("out_proj", merged, w["w_out"], res=x, **big)

    hn = _rms_fwd("ffn_norm", h1, w["ffn_norm_w"])
    half = dict(bm=1024, bn=D_FF // 2, bk=1024, out_dtype=BF16)
    pg = _mm("ffn_up_gate", hn, w["w_up_g"], **half)
    pv = _mm("ffn_up_val", hn, w["w_up_v"], **half)
    cw, cb = w["ffn_conv_w"], w["ffn_conv_b"]
    gate, val, act = _ffn_act_fwd("ffn_act_fwd", pg, pv, cw[:, :D_FF], cw[:, D_FF:], cb[:, :D_FF], cb[:, D_FF:],
                                  tc=D_FF // 2)
    h2 = _mm("ffn_down", act, w["w_down"], res=h1, bm=1024, bn=1024, bk=D_FF // 2)

    dh2, loss_part, g["final_norm_w"] = _final_loss("final_loss", h2, target, w["final_norm_w"].reshape(1, D_MODEL))

    dact = _mm("d_act", dh2, w["w_down"], tb=True, **half)
    wgrad = dict(ta=True, bk=min(2048, x.shape[0]), out_dtype=BF16)
    g["w_down"] = _mm("dw_down", act, dh2, bm=D_FF // 2, bn=1024, **wgrad)
    dgate, dval = _ffn_act_bwd("ffn_act_bwd", dact, gate, val)
    dpg, dcwg, dcbg = _conv_bwd("ffn_conv_bwd_gate", dgate, pg, cw[:, :D_FF], tc=D_FF // 2)
    dpv, dcwv, dcbv = _conv_bwd("ffn_conv_bwd_val", dval, pv, cw[:, D_FF:], tc=D_FF // 2)
    g["ffn_conv_w"] = jnp.concatenate([dcwg, dcwv], axis=1)
    g["ffn_conv_b"] = jnp.concatenate([dcbg, dcbv], axis=1)
    dhn = _mm_sum("d_hn", [(dpg, w["w_up_g"]), (dpv, w["w_up_v"])], bm=1024, bk=D_FF // 2)
    g["w_up_g"] = _mm("dw_up_gate", hn, dpg, bm=1024, bn=D_FF // 2, **wgrad)
    g["w_up_v"] = _mm("dw_up_val", hn, dpv, bm=1024, bn=D_FF // 2, **wgrad)
    dh1, g["ffn_norm_w"] = _rms_bwd("ffn_norm_bwd", h1, w["ffn_norm_w"], dhn, dh2)

    dmerged = _mm("d_merged", dh1, w["w_out"], tb=True, **act16)
    g["w_out"] = _mm("dw_out", merged, dh1, bm=1024, bn=1024, **wgrad)
    dgates, dya, dyb, db0, db1 = _merge_bwd("merge_bwd", gates, y_a, y_b, dmerged, b0, b1)
    g["gate_bias"] = jnp.concatenate([db0, db1], axis=0)

    dya_pre = _mm("d_ya_pre", dya, w["w_proj_a"], tb=True, **act16)
    g["w_proj_a"] = _mm("dw_proj_a", ya_pre, dya, bm=1024, bn=1024, **wgrad)
    dyb_pre = _mm("d_yb_pre", dyb, w["w_proj_b"], tb=True, **act16)
    g["w_proj_b"] = _mm("dw_proj_b", yb_pre, dyb, bm=1024, bn=1024, **wgrad)
    late_pairs = _pair_reduce("late", LATE, [
        g["w_proj_a"].reshape(N_CHIPS, -1, D_MODEL), g["w_proj_b"].reshape(N_CHIPS, -1, D_MODEL),
        g["w_out"].reshape(N_CHIPS, -1, D_MODEL),
        jnp.concatenate([_columns_to_chips(g["w_up_g"], 2), _columns_to_chips(g["w_up_v"], 2)], axis=0),
        g["w_down"].reshape(N_CHIPS, -1, D_MODEL)])

    dy_ssd, dz, g["ssm_norm_w"] = _gate_norm_bwd("gate_norm_bwd", y_ssd, z, dyb_pre, w["ssm_norm_w"])
    (dpre, ddt, da_tok, da_flat, dd), late_received = _ssd_bwd(
        "ssd_bwd", pre, dt, a_cum, a_flat, d_x, ind, ind.T, states, dy_ssd, late_pairs)
    g["ssm_d"] = dd[0:1]
    da_src = jnp.transpose(da_flat.reshape(-1, SSM_HEADS, CHUNK), (0, 2, 1)).reshape(-1, SSM_HEADS)
    ddt_raw, g["ssm_dt_bias"], g["ssm_a_log"] = _dt_bwd("dt_bwd", dt_raw, ddt, da_tok, da_src,
                                                         w["ssm_dt_bias"], w["ssm_a_log"])
    dxbc, g["ssm_conv_w"], g["ssm_conv_b"] = _conv_bwd("ssm_conv_bwd", dpre, xbc, w["ssm_conv_w"], tc=1024)

    dza, g["gmlp_ln_w"], g["gmlp_ln_b"], g["gmlp_ws"], dbs = _gmlp_bwd(
        "gmlp_bwd", za, dya_pre, w["gmlp_ln_w"], w["gmlp_ln_b"], w["gmlp_ws"], bs_col)
    g["gmlp_bs"] = dbs.reshape(GMLP_GROUPS, GMLP_BLOCK)

    dw_in = jnp.concatenate([
        _mm("dw_gates", xn, dgates, bm=1024, bn=1024, **wgrad), _mm("dw_gmlp", xn, dza, bm=1024, bn=1024, **wgrad),
        _mm("dw_z", xn, dz, bm=1024, bn=1024, **wgrad), _mm("dw_xbc", xn, dxbc, bm=1024, bn=1024, **wgrad),
        _mm("dw_dt", xn, ddt_raw, bm=1024, bn=SSM_HEADS, **wgrad)], axis=1)
    in_pairs = _pair_reduce("in", ["w_in"], [_columns_to_chips(dw_in)])
    dxn, in_received = _mm_sum("d_xn", [(dgates, w["w_g"]), (dza, w["w_za"]), (dz, w["w_z"]), (dxbc, w["w_xbc"]),
                                        (ddt_raw, w["w_dt"])], bm=1024, bk=1024, exchange=in_pairs)
    grad_x, g["mix_norm_w"] = _rms_bwd("mix_norm_bwd", x, w["mix_norm_w"], dxn, dh1)
    return loss_part, grad_x, g, in_pairs + list(late_pairs), list(in_received) + list(late_received)


def _position():
    return lax.axis_index("x"), lax.axis_index("y"), lax.axis_index("c")


def _own_slot(stack, own):
    chip = 2 * lax.axis_index("x") + lax.axis_index("y")
    return lax.dynamic_update_index_in_dim(stack, own, chip, axis=0)


def _scatter_phases(ins, outs, send_sems, recv_sems):
    n = len(ins)
    x, y, c = _position()
    me = 2 * x + y
    peers = [(1 - x, y), (x, 1 - y), (1 - x, 1 - y)]

    def copy(i, k, src_slot, dst_slot):
        px, py = peers[k]
        return pltpu.make_async_remote_copy(
            src_ref=ins[i].at[src_slot], dst_ref=outs[i].at[dst_slot],
            send_sem=send_sems.at[i, k], recv_sem=recv_sems.at[i, k],
            device_id=(px, py, c), device_id_type=MESH)

    def start():
        for i in range(n):
            for k, (px, py) in enumerate(peers):
                copy(i, k, 2 * px + py, me).start()

    def finish():
        for i in range(n):
            for k, (px, py) in enumerate(peers):
                copy(i, k, me, 2 * px + py).wait_recv()
        for i in range(n):
            for k, (px, py) in enumerate(peers):
                copy(i, k, 2 * px + py, me).wait_send()

    return start, finish


def _exchange_scratch(n, per_array):
    return [pltpu.SemaphoreType.DMA((n, per_array)), pltpu.SemaphoreType.DMA((n, per_array))]


def _half_rows(ref_rows, which):
    half = ref_rows // 2
    return pl.ds(pl.multiple_of(which * half, 2 * SUBLANES), half)


GATHER_SEMS = 2 * (N_CHIPS - 1)


def _gather_phases(nrows, ns, ins, outs, send_sems, recv_sems):
    n = len(ins)
    x, y, c = _position()
    me = 2 * x + y
    sibling = (x, y, 1 - c)
    chips = [(1 - x, y), (x, 1 - y), (1 - x, 1 - y)]

    def remote(i, k, src, dst, to):
        return pltpu.make_async_remote_copy(src_ref=src, dst_ref=dst, send_sem=send_sems.at[i, k],
                                            recv_sem=recv_sems.at[i, k], device_id=to, device_id_type=MESH)

    def over_ici(i, k):
        px, py = chips[k]
        rows = _half_rows(nrows[i], c) if i < ns else slice(None)
        return remote(i, k, ins[i].at[rows], outs[i].at[me, rows], (px, py, c))

    def landed(i, k, which):
        px, py = chips[k]
        return outs[i].at[2 * px + py, _half_rows(nrows[i], which)] if i < ns else outs[i].at[2 * px + py]

    def start():
        for i in range(n):
            for k in range(N_CHIPS - 1):
                over_ici(i, k).start()

    def forward():
        for i in range(n):
            for k in range(N_CHIPS - 1):
                piece = landed(i, k, c)
                remote(i, k, piece, piece, (*chips[k], c)).wait_recv()
                if i < ns:
                    remote(i, N_CHIPS - 1 + k, piece, piece, sibling).start()

    def finish():
        for i in range(ns):
            for k in range(N_CHIPS - 1):
                piece = landed(i, k, 1 - c)
                remote(i, N_CHIPS - 1 + k, piece, piece, sibling).wait_recv()
        for i in range(n):
            for k in range(N_CHIPS - 1):
                over_ici(i, k).wait_send()
                if i < ns:
                    piece = landed(i, k, c)
                    remote(i, N_CHIPS - 1 + k, piece, piece, sibling).wait_send()

    return start, forward, finish


def _gather_chips_split(name, split, whole):
    arrs = list(split) + list(whole)
    n = len(arrs)

    def body(*refs):
        phases = _gather_phases([a.shape[0] for a in arrs], len(split), refs[:n], refs[n:2 * n], *refs[2 * n:])
        for phase in phases:
            phase()

    hbm = pl.BlockSpec(memory_space=pl.ANY)
    return pl.pallas_call(
        body, name=name, in_specs=[hbm] * n, out_specs=[hbm] * n,
        out_shape=[jax.ShapeDtypeStruct((N_CHIPS,) + a.shape, a.dtype) for a in arrs],
        scratch_shapes=_exchange_scratch(n, GATHER_SEMS),
        compiler_params=pltpu.CompilerParams(has_side_effects=True),
    )(*arrs)


def _swap_cores(name, arrs):
    n = len(arrs)

    def body(*refs):
        ins, outs = refs[:n], refs[n:2 * n]
        send_sems, recv_sems = refs[2 * n:]
        x, y, c = _position()
        copies = [pltpu.make_async_remote_copy(src_ref=ins[i], dst_ref=outs[i], send_sem=send_sems.at[i],
                                               recv_sem=recv_sems.at[i], device_id=(x, y, 1 - c), device_id_type=MESH)
                  for i in range(n)]
        for cp in copies:
            cp.start()
        for cp in copies:
            cp.wait_recv()
        for cp in copies:
            cp.wait_send()

    hbm = pl.BlockSpec(memory_space=pl.ANY)
    return pl.pallas_call(
        body, name=name, in_specs=[hbm] * n, out_specs=[hbm] * n,
        out_shape=[jax.ShapeDtypeStruct(a.shape, a.dtype) for a in arrs],
        scratch_shapes=[pltpu.SemaphoreType.DMA((n,)), pltpu.SemaphoreType.DMA((n,))],
        compiler_params=pltpu.CompilerParams(has_side_effects=True),
    )(*arrs)


def _row_half(a, which, axis):
    half = a.shape[axis] // 2
    return lax.dynamic_slice_in_dim(a, which * half, half, axis=axis)


def _all_reduce(name, pack):
    def body(in_ref, out_ref, buf, send_sems, recv_sems):
        x, y, c = _position()
        me = 4 * x + 2 * y + c
        flips = [(dx, dy, dc) for dx in (0, 1) for dy in (0, 1) for dc in (0, 1) if (dx, dy, dc) != (0, 0, 0)]
        peers = [((1 - x) if dx else x, (1 - y) if dy else y, (1 - c) if dc else c) for dx, dy, dc in flips]
        buf[me] = in_ref[...]
        sends = []
        for k, peer in enumerate(peers):
            cp = pltpu.make_async_remote_copy(src_ref=in_ref, dst_ref=buf.at[me], send_sem=send_sems.at[k],
                                              recv_sem=recv_sems.at[k], device_id=peer, device_id_type=MESH)
            cp.start()
            sends.append(cp)
        for k, (px, py, pc) in enumerate(peers):
            pltpu.make_async_remote_copy(src_ref=in_ref, dst_ref=buf.at[4 * px + 2 * py + pc], send_sem=send_sems.at[k],
                                         recv_sem=recv_sems.at[k], device_id=(px, py, pc), device_id_type=MESH).wait_recv()
        total = buf[0]
        for j in range(1, N_DEV):
            total = total + buf[j]
        out_ref[...] = total
        for cp in sends:
            cp.wait_send()

    vmem = pl.BlockSpec(memory_space=pltpu.VMEM)
    return pl.pallas_call(
        body, name=name, in_specs=[vmem], out_specs=vmem,
        out_shape=jax.ShapeDtypeStruct(pack.shape, F32),
        scratch_shapes=[pltpu.VMEM((N_DEV,) + pack.shape, F32), pltpu.SemaphoreType.DMA((N_DEV - 1,)),
                        pltpu.SemaphoreType.DMA((N_DEV - 1,))],
        compiler_params=pltpu.CompilerParams(has_side_effects=True, vmem_limit_bytes=VMEM_LIMIT_V7X),
    )(pack)


def _pack(arrs):
    rows = [a.reshape(-1, LANES) for a in arrs]
    total = sum(r.shape[0] for r in rows)
    rows.append(jnp.zeros((-total % SUBLANES, LANES), F32))
    return jnp.concatenate(rows, axis=0)


def _unpack(pack, shapes):
    out, off = [], 0
    for s in shapes:
        nrow = 1
        for d in s:
            nrow *= d
        nrow //= LANES
        out.append(pack[off:off + nrow].reshape(s))
        off += nrow
    return out


SMALL = ["mix_norm_w", "gate_bias", "gmlp_ln_w", "gmlp_ln_b", "gmlp_ws", "gmlp_bs", "ssm_conv_w", "ssm_conv_b",
         "ssm_dt_bias", "ssm_a_log", "ssm_d", "ssm_norm_w", "ffn_norm_w", "ffn_conv_w", "ffn_conv_b", "final_norm_w"]
SMALL_SHARDED = ("gate_bias", "ssm_conv_w", "ffn_conv_w")
BIG = ["w_in", "w_proj_a", "w_proj_b", "w_out", "ffn_w_up", "ffn_w_down"]
WEIGHTS = ["mix_norm_w", "w_in", "gate_bias", "gmlp_ln_w", "gmlp_ln_b", "gmlp_ws", "gmlp_bs", "ssm_conv_w",
           "ssm_conv_b", "ssm_dt_bias", "ssm_a_log", "ssm_d", "ssm_norm_w", "w_proj_a", "w_proj_b", "w_out",
           "ffn_norm_w", "ffn_w_up", "ffn_conv_w", "ffn_conv_b", "ffn_w_down", "final_norm_w"]
IN_SPLITS = [0, 2048, 4096, 6144, 9216, 9248]


def _columns_from_chips(stack):
    return jnp.transpose(stack, (1, 0, 2)).reshape(stack.shape[1], -1)


def _columns_to_chips(full, parts=N_CHIPS):
    rows, cols = full.shape
    return jnp.transpose(full.reshape(rows, parts, cols // parts), (1, 0, 2))


def kernel(x, mix_norm_w, w_in, gate_bias, gmlp_ln_w, gmlp_ln_b, gmlp_ws, gmlp_bs, ssm_conv_w, ssm_conv_b, ssm_dt_bias, ssm_a_log, ssm_d, ssm_norm_w, w_proj_a, w_proj_b, w_out, ffn_norm_w, ffn_w_up, ffn_conv_w, ffn_conv_b, ffn_w_down, final_norm_w, loss_target, m_mix_norm_w, m_w_in, m_gate_bias, m_gmlp_ln_w, m_gmlp_ln_b, m_gmlp_ws, m_gmlp_bs, m_ssm_conv_w, m_ssm_conv_b, m_ssm_dt_bias, m_ssm_a_log, m_ssm_d, m_ssm_norm_w, m_w_proj_a, m_w_proj_b, m_w_out, m_ffn_norm_w, m_ffn_w_up, m_ffn_conv_w, m_ffn_conv_b, m_ffn_w_down, m_final_norm_w, v_mix_norm_w, v_w_in, v_gate_bias, v_gmlp_ln_w, v_gmlp_ln_b, v_gmlp_ws, v_gmlp_bs, v_ssm_conv_w, v_ssm_conv_b, v_ssm_dt_bias, v_ssm_a_log, v_ssm_d, v_ssm_norm_w, v_w_proj_a, v_w_proj_b, v_w_out, v_ffn_norm_w, v_ffn_w_up, v_ffn_conv_w, v_ffn_conv_b, v_ffn_w_down, v_final_norm_w):
    args = dict(locals())
    weights = {n: args[n] for n in WEIGHTS}
    moments_m = {n: args["m_" + n] for n in WEIGHTS}
    moments_v = {n: args["v_" + n] for n in WEIGHTS}
    chip = 2 * lax.axis_index("x") + lax.axis_index("y")

    shards = [weights["w_in"][0].astype(BF16)] + [weights[n][0] for n in SMALL_SHARDED]
    gathered = _gather_chips_split("gather_weights", shards[:1], shards[1:])
    w_in_s, gb_s, scw_s, fcw_s = [_own_slot(stack, own) for stack, own in zip(gathered, shards)]
    late_shards = [weights[n][0].astype(BF16) for n in LATE]
    w_in_full = _columns_from_chips(w_in_s)
    full = {"w_" + nm: w_in_full[:, IN_SPLITS[k]:IN_SPLITS[k + 1]] for k, nm in enumerate(["g", "za", "z", "xbc", "dt"])}
    full["gate_bias"] = _columns_from_chips(gb_s)
    full["ssm_conv_w"] = _columns_from_chips(scw_s)
    full["ffn_conv_w"] = _columns_from_chips(fcw_s)
    for n in SMALL:
        if n not in SMALL_SHARDED:
            full[n] = weights[n] if n == "final_norm_w" else weights[n][0]
    for n in ("mix_norm_w", "ffn_norm_w", "ssm_conv_b", "ssm_dt_bias", "ssm_a_log", "ssm_d", "ssm_norm_w", "ffn_conv_b"):
        full[n] = full[n].reshape(1, -1)

    loss_part, grad_x, g, pair, received = _local_step(x[0], loss_target[0], full, late_shards)

    per_head = ["ssm_dt_bias", "ssm_a_log", "ssm_d"]
    rest = [n for n in SMALL if n not in per_head]
    head_row = jnp.concatenate([g[n] for n in per_head] + [jnp.zeros((1, LANES - 3 * SSM_HEADS), F32)], axis=1)
    pack = _pack([loss_part, head_row] + [g[n] for n in rest])
    reduced = _unpack(_all_reduce("reduce_small", pack), [(1, LANES), (1, LANES)] + [g[n].shape for n in rest])
    loss = reduced[0][0, 0]
    small_grads = {n: reduced[1][:, k * SSM_HEADS:(k + 1) * SSM_HEADS] for k, n in enumerate(per_head)}
    for n, r in zip(rest, reduced[2:]):
        if n in SMALL_SHARDED:
            width = weights[n].shape[2]
            r = lax.dynamic_slice_in_dim(r, chip * width, width, axis=1)
        small_grads[n] = r
    two_d = lambda a: a.reshape(-1, a.shape[-1])
    upd = _adamw_small("adamw_small", *[[two_d(d[n]) for n in SMALL]
                                        for d in (weights, small_grads, moments_m, moments_v)])
    small_out = [[small_grads[n] for n in SMALL]] + list(upd)
    small_out = [[a.reshape(weights[n].shape) for n, a in zip(SMALL, kind)] for kind in small_out]

    received = [_own_slot(r, lax.dynamic_index_in_dim(p, chip, 0, keepdims=False)) for r, p in zip(received, pair)]
    halves = [_sum_slots("sum_" + n, r, tm=HALF_TILES[n], rs=2 * SUBLANES) for n, r in zip(BIG, received)]
    tiles = {"w_in": 128, "w_proj_a": 256, "w_proj_b": 256, "w_out": 256, "ffn_w_up": 128, "ffn_w_down": 176}
    core = lax.axis_index("c")
    other = _swap_cores("join_grads", halves)
    reduced = [jnp.concatenate([jnp.where(core == 0, a, b), jnp.where(core == 0, b, a)], axis=0)
               for a, b in zip(halves, other)]
    big_out = {}
    for n, grad in zip(BIG, reduced):
        big_out[n] = _adamw("adamw_" + n, weights[n][0], grad, moments_m[n][0], moments_v[n][0],
                            tm=tiles[n], rs=SUBLANES)

    per_kind = [[], [], [], []]
    for n in WEIGHTS:
        for kind in range(4):
            if n in big_out:
                per_kind[kind].append(big_out[n][kind].reshape(weights[n].shape))
            else:
                per_kind[kind].append(small_out[kind][SMALL.index(n)])
    return (loss, grad_x[None], *per_kind[0], *per_kind[1], *per_kind[2], *per_kind[3])
```

```python
import jax
import jax.numpy as jnp
from jax import lax
from jax.experimental import pallas as pl
from jax.experimental.pallas import tpu as pltpu

F32 = jnp.float32
BF16 = jnp.bfloat16
MESH = pl.DeviceIdType.MESH

EPS = 1e-5
D_MODEL = 1024
GMLP_BLOCK = 128
GMLP_GROUPS = 8
CHUNK = 64
SSM_INNER = 2048
SSM_HEADS = 32
SSM_HEAD_DIM = 64
SSM_GROUPS = 4
SSM_HPG = 8
SSM_STATE = 128
SSM_CONV = 4
SSM_XBC = 3072
D_FF = 2816
FFN_CONV = 3
N_CHIPS = 4
N_DEV = 8

ADAM_LR = 0.001
ADAM_B1 = 0.9
ADAM_B2 = 0.999
ADAM_EPS = 1e-08
ADAM_WD = 0.01
ADAM_STEP = 10

VMEM_LIMIT_V7X = 56 * 1024 * 1024
SUBLANES = 8
LANES = 128


def _params(sem=None):
    return pltpu.CompilerParams(dimension_semantics=sem, vmem_limit_bytes=VMEM_LIMIT_V7X)


def _dot(a, b, ca=1, cb=0):
    return lax.dot_general(a.astype(BF16), b.astype(BF16), (((ca,), (cb,)), ((), ())),
                           preferred_element_type=F32)


def _mm(name, a, b, *, ta=False, tb=False, out_dtype=F32, bm, bn, bk, res=None):
    m, k = (a.shape[1], a.shape[0]) if ta else a.shape
    k2, n = (b.shape[1], b.shape[0]) if tb else b.shape
    assert k == k2 and m % bm == 0 and n % bn == 0 and k % bk == 0, (name, a.shape, b.shape)
    nk = k // bk
    a_spec = (pl.BlockSpec((bk, bm), lambda i, j, kk: (kk, i)) if ta
              else pl.BlockSpec((bm, bk), lambda i, j, kk: (i, kk)))
    b_spec = (pl.BlockSpec((bn, bk), lambda i, j, kk: (j, kk)) if tb
              else pl.BlockSpec((bk, bn), lambda i, j, kk: (kk, j)))
    o_spec = pl.BlockSpec((bm, bn), lambda i, j, kk: (i, j))
    has_res = res is not None

    def body(*refs):
        a_ref, b_ref = refs[0], refs[1]
        r_ref = refs[2] if has_res else None
        o_ref = refs[3] if has_res else refs[2]
        p = _dot(a_ref[...], b_ref[...], 0 if ta else 1, 1 if tb else 0)

        def finish(total):
            if has_res:
                total = total + r_ref[...]
            o_ref[...] = total.astype(out_dtype)

        if nk == 1:
            finish(p)
        else:
            acc_ref = refs[-1]
            kk = pl.program_id(2)

            @pl.when(kk == 0)
            def _():
                acc_ref[...] = p

            @pl.when(kk > 0)
            def _():
                acc_ref[...] += p

            @pl.when(kk == nk - 1)
            def _():
                finish(acc_ref[...])

    return pl.pallas_call(
        body, name=name,
        grid=(m // bm, n // bn, nk),
        in_specs=[a_spec, b_spec] + ([o_spec] if has_res else []),
        out_specs=o_spec,
        out_shape=jax.ShapeDtypeStruct((m, n), out_dtype),
        scratch_shapes=[pltpu.VMEM((bm, bn), F32)] if nk > 1 else [],
        compiler_params=_params(("parallel", "parallel", "arbitrary")),
    )(*([a, b] + ([res] if has_res else [])))


def _mm_sum(name, pairs, *, bm, bk, exchange=()):
    nx = len(exchange)
    npair = len(pairs)
    m, n = pairs[0][0].shape[0], pairs[0][1].shape[0]
    steps, first = [], []
    for a, b in pairs:
        k = a.shape[1]
        assert a.shape[0] == m and b.shape == (n, k) and m % bm == 0 and (k % bk == 0 or k < bk), (name, a.shape, b.shape)
        first.append(sum(steps))
        steps.append(max(k // bk, 1))
    total = sum(steps)
    in_specs = []
    for (a, b), off, cnt in zip(pairs, first, steps):
        width = min(bk, a.shape[1])
        in_specs.append(pl.BlockSpec((bm, width), lambda i, kk, off=off, cnt=cnt: (i, jnp.clip(kk - off, 0, cnt - 1))))
        in_specs.append(pl.BlockSpec((n, width), lambda i, kk, off=off, cnt=cnt: (0, jnp.clip(kk - off, 0, cnt - 1))))

    def body(*refs):
        send_refs = refs[2 * npair:2 * npair + nx]
        o_ref = refs[2 * npair + nx]
        recv_refs = refs[2 * npair + nx + 1:2 * npair + 2 * nx + 1]
        acc_ref = refs[2 * npair + 2 * nx + 1]
        i, kk = pl.program_id(0), pl.program_id(1)
        if nx:
            start, finish = _scatter_phases(send_refs, recv_refs, *refs[2 * npair + 2 * nx + 2:])

            @pl.when((i == 0) & (kk == 0))
            def _():
                start()

        for s, (off, cnt) in enumerate(zip(first, steps)):
            @pl.when((kk >= off) & (kk < off + cnt))
            def _(s=s, off=off):
                p = _dot(refs[2 * s][...], refs[2 * s + 1][...], 1, 1)
                if off == 0:
                    @pl.when(kk == 0)
                    def _():
                        acc_ref[...] = p

                    @pl.when(kk > 0)
                    def _():
                        acc_ref[...] += p
                else:
                    acc_ref[...] += p

        @pl.when(kk == total - 1)
        def _():
            o_ref[...] = acc_ref[...]

        if nx:
            @pl.when((i == m // bm - 1) & (kk == total - 1))
            def _():
                finish()

    hbm = pl.BlockSpec(memory_space=pl.ANY)
    res = pl.pallas_call(
        body, name=name, grid=(m // bm, total),
        in_specs=in_specs + [hbm] * nx, out_specs=[pl.BlockSpec((bm, n), lambda i, kk: (i, 0))] + [hbm] * nx,
        out_shape=[jax.ShapeDtypeStruct((m, n), F32)] + [jax.ShapeDtypeStruct(e.shape, e.dtype) for e in exchange],
        scratch_shapes=[pltpu.VMEM((bm, n), F32)] + (_exchange_scratch(nx, N_CHIPS - 1) if nx else []),
        compiler_params=_params(("arbitrary", "arbitrary")),
    )(*[t for pair in pairs for t in pair], *exchange)
    return (res[0], res[1:]) if nx else res[0]


def _rows(name, fn, ins, params, outs, accs, *, tm, rs, unroll=4):
    nrow = ins[0][0].shape[-2]
    while tm % (rs * unroll):
        unroll //= 2
    assert nrow % tm == 0 and tm % rs == 0, (name, nrow, tm, rs)
    n_in, n_p, n_out, n_acc = len(ins), len(params), len(outs), len(accs)
    in_specs = []
    for spec in ins:
        arr, width, cb = spec[:3]
        if len(spec) == 4:
            in_specs.append(pl.BlockSpec((None, tm, width), lambda i, cb=cb, lead=spec[3]: (lead, i, cb)))
        else:
            in_specs.append(pl.BlockSpec((tm, width), lambda i, cb=cb: (i, cb)))
    for p in params:
        in_specs.append(pl.BlockSpec(p.shape, lambda i, nd=p.ndim: (0,) * nd))
    out_specs = [pl.BlockSpec((tm, w), lambda i: (i, 0)) for w, _ in outs]
    out_specs += [pl.BlockSpec(s, lambda i: (0, 0)) for s in accs]
    out_shape = [jax.ShapeDtypeStruct((nrow, w), dt) for w, dt in outs]
    out_shape += [jax.ShapeDtypeStruct(s, F32) for s in accs]

    def body(*refs):
        in_refs = refs[:n_in]
        p_refs = refs[n_in:n_in + n_p]
        o_refs = refs[n_in + n_p:n_in + n_p + n_out]
        a_refs = refs[n_in + n_p + n_out:]
        pv = [p[...] for p in p_refs]

        if n_acc:
            @pl.when(pl.program_id(0) == 0)
            def _():
                for a_ref in a_refs:
                    a_ref[...] = jnp.zeros_like(a_ref)

        def step(r, carry):
            for u in range(unroll):
                sl = pl.ds(pl.multiple_of((r * unroll + u) * rs, rs), rs)
                vals = [ref[sl, :].astype(F32) for ref in in_refs]
                row_out, sums = fn(*vals, *pv)
                for o_ref, v in zip(o_refs, row_out):
                    o_ref[sl, :] = v.astype(o_ref.dtype)
                carry = tuple(c + s for c, s in zip(carry, sums))
            return carry

        init = tuple(jnp.zeros(s, F32) for s in accs)
        total = lax.fori_loop(0, tm // (rs * unroll), step, init)
        for a_ref, t in zip(a_refs, total):
            a_ref[...] += t

    res = pl.pallas_call(
        body, name=name, grid=(nrow // tm,),
        in_specs=in_specs, out_specs=out_specs, out_shape=out_shape,
        compiler_params=_params(("arbitrary",)),
    )(*([s[0] for s in ins] + list(params)))
    return res


def _rms(x, w):
    return x * lax.rsqrt(jnp.mean(x * x, axis=-1, keepdims=True) + EPS) * w


def _colsum(v):
    return jnp.sum(v, axis=0, keepdims=True)


def _rms_fwd(name, x, w):
    def fn(xv, wv):
        return (_rms(xv, wv),), ()
    return _rows(name, fn, [(x, D_MODEL, 0)], [w], [(D_MODEL, BF16)], [], tm=1024, rs=16)[0]


def _rms_bwd(name, x, w, dy, dres):
    def fn(xv, dyv, drv, wv):
        _, vjp = jax.vjp(_rms, xv, wv)
        dx, dw = vjp(dyv)
        return (drv + dx,), (dw,)
    return _rows(name, fn, [(x, D_MODEL, 0), (dy, D_MODEL, 0), (dres, D_MODEL, 0)], [w],
                 [(D_MODEL, F32)], [(1, D_MODEL)], tm=1024, rs=16)


def _final_loss(name, h, target, w):
    def fn(hv, tv, wv):
        y, vjp = jax.vjp(_rms, hv, wv)
        err = y - tv
        part = 0.5 * jnp.sum(jnp.mean(err * err, axis=-1, keepdims=True), axis=0, keepdims=True)
        dh, dw = vjp(err / D_MODEL)
        return (dh,), (jnp.broadcast_to(part, (1, LANES)), dw)
    return _rows(name, fn, [(h, D_MODEL, 0), (target, D_MODEL, 0)], [w],
                 [(D_MODEL, F32)], [(1, LANES), (1, D_MODEL)], tm=1024, rs=16)


def _merge(ga, gb, ya, yb, b0, b1):
    return jax.nn.sigmoid(ga + b0) * ya + jax.nn.sigmoid(gb + b1) * yb


def _merge_fwd(name, g, ya, yb, b0, b1):
    def fn(ga, gb, yav, ybv, b0v, b1v):
        return (_merge(ga, gb, yav, ybv, b0v, b1v),), ()
    return _rows(name, fn, [(g, D_MODEL, 0), (g, D_MODEL, 1), (ya, D_MODEL, 0), (yb, D_MODEL, 0)],
                 [b0, b1], [(D_MODEL, BF16)], [], tm=1024, rs=16)[0]


def _merge_bwd(name, g, ya, yb, dm, b0, b1):
    def fn(ga, gb, yav, ybv, dmv, b0v, b1v):
        _, vjp = jax.vjp(_merge, ga, gb, yav, ybv, b0v, b1v)
        dga, dgb, dya, dyb, db0, db1 = vjp(dmv)
        return (jnp.concatenate([dga, dgb], axis=1), dya, dyb), (db0, db1)
    return _rows(name, fn,
                 [(g, D_MODEL, 0), (g, D_MODEL, 1), (ya, D_MODEL, 0), (yb, D_MODEL, 0), (dm, D_MODEL, 0)],
                 [b0, b1], [(2 * D_MODEL, BF16), (D_MODEL, BF16), (D_MODEL, BF16)],
                 [(1, D_MODEL), (1, D_MODEL)], tm=1024, rs=16)


GROUP_W = SSM_INNER // SSM_GROUPS


def _gate_norm_group(y, z, nw):
    v = y * jax.nn.silu(z)
    return v * lax.rsqrt(jnp.mean(v * v, axis=-1, keepdims=True) + EPS) * nw


def _gate_norm_fwd(name, y, z, nw):
    def fn(yv, zv, nwv):
        parts = [_gate_norm_group(yv[:, k * GROUP_W:(k + 1) * GROUP_W], zv[:, k * GROUP_W:(k + 1) * GROUP_W],
                                  nwv[:, k * GROUP_W:(k + 1) * GROUP_W]) for k in range(SSM_GROUPS)]
        return (jnp.concatenate(parts, axis=1),), ()
    return _rows(name, fn, [(y, SSM_INNER, 0), (z, SSM_INNER, 0)], [nw], [(SSM_INNER, BF16)], [],
                 tm=512, rs=16)[0]


def _gate_norm_bwd(name, y, z, dout, nw):
    def fn(yv, zv, dv, nwv):
        dys, dzs, dns = [], [], []
        for k in range(SSM_GROUPS):
            sl = slice(k * GROUP_W, (k + 1) * GROUP_W)
            _, vjp = jax.vjp(_gate_norm_group, yv[:, sl], zv[:, sl], nwv[:, sl])
            dy, dz, dn = vjp(dv[:, sl])
            dys.append(dy), dzs.append(dz), dns.append(dn)
        return (jnp.concatenate(dys, axis=1), jnp.concatenate(dzs, axis=1)), (jnp.concatenate(dns, axis=1),)
    return _rows(name, fn, [(y, SSM_INNER, 0), (z, SSM_INNER, 0), (dout, SSM_INNER, 0)], [nw],
                 [(SSM_INNER, BF16), (SSM_INNER, BF16)], [(1, SSM_INNER)], tm=512, rs=16)


def _softplus(v):
    return jnp.maximum(v, 0.0) + jnp.log1p(jnp.exp(-jnp.abs(v)))


def _chunk_cumsum(v, reverse=False):
    row = lax.broadcasted_iota(jnp.int32, v.shape, 0)
    step = 1
    while step < CHUNK:
        if reverse:
            shifted = pltpu.roll(v, CHUNK - step, axis=0)
            v = v + jnp.where(row < CHUNK - step, shifted, 0.0)
        else:
            shifted = pltpu.roll(v, step, axis=0)
            v = v + jnp.where(row >= step, shifted, 0.0)
        step *= 2
    return v


def _dt_prep(name, dt_raw, dt_bias, a_log):
    def fn(rv, bv, alv):
        dt = _softplus(rv + bv)
        return (dt, _chunk_cumsum(dt * (-jnp.exp(alv)))), ()
    return _rows(name, fn, [(dt_raw, SSM_HEADS, 0)], [dt_bias, a_log],
                 [(SSM_HEADS, F32), (SSM_HEADS, F32)], [], tm=512, rs=CHUNK)


def _dt_bwd(name, dt_raw, ddt, da1, da2, dt_bias, a_log):
    def fn(rv, ddv, d1, d2, bv, alv):
        pre = rv + bv
        dt = _softplus(pre)
        a_neg = -jnp.exp(alv)
        back = _chunk_cumsum(d1 + d2, reverse=True)
        d_dt = ddv + back * a_neg
        d_raw = d_dt * jax.nn.sigmoid(pre)
        return (d_raw,), (_colsum(d_raw), _colsum(back * dt) * a_neg)
    return _rows(name, fn, [(dt_raw, SSM_HEADS, 0), (ddt, SSM_HEADS, 0), (da1, SSM_HEADS, 0), (da2, SSM_HEADS, 0)],
                 [dt_bias, a_log], [(SSM_HEADS, BF16)], [(1, SSM_HEADS), (1, SSM_HEADS)], tm=512, rs=CHUNK)


def _adamw_math(w, g, m, v):
    m_new = ADAM_B1 * m + (1.0 - ADAM_B1) * g
    v_new = ADAM_B2 * v + (1.0 - ADAM_B2) * jnp.square(g)
    m_hat = m_new / (1.0 - ADAM_B1 ** ADAM_STEP)
    v_hat = v_new / (1.0 - ADAM_B2 ** ADAM_STEP)
    delta = -ADAM_LR * (m_hat / (jnp.sqrt(v_hat) + ADAM_EPS) + ADAM_WD * w)
    return delta, m_new, v_new


def _adamw(name, w, g, m, v, *, tm, rs):
    width = w.shape[1]

    def fn(wv, mv, vv, gv):
        return (gv,) + _adamw_math(wv, gv, mv, vv), ()
    return _rows(name, fn, [(w, width, 0), (m, width, 0), (v, width, 0), (g, width, 0)],
                 [], [(width, F32)] * 4, [], tm=tm, rs=rs)


def _adamw_small(name, ws, gs, ms, vs):
    n = len(ws)

    def body(*refs):
        w_refs, g_refs, m_refs, v_refs = (refs[k * n:(k + 1) * n] for k in range(4))
        outs = refs[4 * n:]
        for i in range(n):
            res = _adamw_math(w_refs[i][...], g_refs[i][...], m_refs[i][...], v_refs[i][...])
            for k in range(3):
                outs[k * n + i][...] = res[k]

    vmem = pl.BlockSpec(memory_space=pltpu.VMEM)
    res = pl.pallas_call(
        body, name=name, in_specs=[vmem] * (4 * n), out_specs=[vmem] * (3 * n),
        out_shape=[jax.ShapeDtypeStruct(w.shape, F32) for w in ws] * 3,
        compiler_params=pltpu.CompilerParams(vmem_limit_bytes=VMEM_LIMIT_V7X),
    )(*ws, *gs, *ms, *vs)
    return res[:n], res[n:2 * n], res[2 * n:]


def _pair_sum(name, a, b, *, tm):
    shape = a.shape
    flat = (shape[0] * shape[1], shape[2])

    def fn(av, bv):
        return (av.astype(F32) + bv.astype(F32),), ()
    out = _rows(name, fn, [(a.reshape(flat), flat[1], 0), (b.reshape(flat), flat[1], 0)], [], [(flat[1], BF16)], [],
                tm=tm, rs=2 * SUBLANES)[0]
    return out.reshape(shape)


def _sum_slots(name, stack, *, tm, rs):
    width = stack.shape[2]

    def fn(*slots):
        s0, s1, s2, s3 = (s.astype(F32) for s in slots)
        return (((s0 + s1) + s2) + s3,), ()
    return _rows(name, fn, [(stack, width, 0, k) for k in range(N_CHIPS)], [], [(width, F32)], [],
                 tm=tm, rs=rs)[0]


def _layernorm(v, w, b):
    mu = jnp.mean(v, axis=-1, keepdims=True)
    var = jnp.mean(jnp.square(v - mu), axis=-1, keepdims=True)
    return (v - mu) * lax.rsqrt(var + EPS) * w + b


GELU_C = 0.7978845608028654
GELU_A = 0.044715


def _gelu_and_slope(x):
    x2 = x * x
    t = jnp.tanh(GELU_C * x * (1.0 + GELU_A * x2))
    half = 0.5 * (1.0 + t)
    slope = half + 0.5 * x * (1.0 - t * t) * (GELU_C * (1.0 + 3.0 * GELU_A * x2))
    return x * half, slope


def _layernorm_and_back(v, w, b):
    mu = jnp.mean(v, axis=-1, keepdims=True)
    cen = v - mu
    rstd = lax.rsqrt(jnp.mean(cen * cen, axis=-1, keepdims=True) + EPS)
    vhat = cen * rstd

    def back(dout):
        dhat = dout * w
        dv = rstd * (dhat - jnp.mean(dhat, axis=-1, keepdims=True)
                     - vhat * jnp.mean(dhat * vhat, axis=-1, keepdims=True))
        return dv, _colsum(dout * vhat), _colsum(dout)

    return vhat * w + b, back


def _gmlp_mask():
    t = lax.broadcasted_iota(jnp.int32, (GMLP_BLOCK, GMLP_BLOCK), 0) // CHUNK
    s = lax.broadcasted_iota(jnp.int32, (GMLP_BLOCK, GMLP_BLOCK), 1) // CHUNK
    return s <= t


GMLP_TM = 1024


def _gmlp_fwd(name, za, ln_w, ln_b, ws, bs_col):
    nrow = za.shape[0]
    tm = GMLP_TM
    width = GMLP_GROUPS * GMLP_BLOCK

    def body(za_ref, lnw_ref, lnb_ref, ws_ref, bs_ref, o_ref, wm_ref):
        mask = _gmlp_mask()
        for g in range(GMLP_GROUPS):
            wm_ref[g] = jnp.where(mask, ws_ref[g], 0.0).astype(BF16)

        def block(n, carry):
            rows = pl.ds(pl.multiple_of(n * GMLP_BLOCK, GMLP_BLOCK), GMLP_BLOCK)
            for g in range(GMLP_GROUPS):
                cols = slice(g * GMLP_BLOCK, (g + 1) * GMLP_BLOCK)
                vcols = slice(width + g * GMLP_BLOCK, width + (g + 1) * GMLP_BLOCK)
                u = jax.nn.gelu(za_ref[rows, cols].astype(F32))
                v = jax.nn.gelu(za_ref[rows, vcols].astype(F32))
                vn = _layernorm(v, lnw_ref[g:g + 1, :], lnb_ref[g:g + 1, :])
                sv = _dot(wm_ref[g], vn) + bs_ref[g]
                o_ref[rows, cols] = (u * sv).astype(o_ref.dtype)
            return carry

        lax.fori_loop(0, tm // GMLP_BLOCK, block, 0)

    small = lambda a: pl.BlockSpec(a.shape, lambda i, nd=a.ndim: (0,) * nd)
    return pl.pallas_call(
        body, name=name, grid=(nrow // tm,),
        in_specs=[pl.BlockSpec((tm, 2 * width), lambda i: (i, 0)), small(ln_w), small(ln_b), small(ws), small(bs_col)],
        out_specs=pl.BlockSpec((tm, width), lambda i: (i, 0)),
        out_shape=jax.ShapeDtypeStruct((nrow, width), BF16),
        scratch_shapes=[pltpu.VMEM((GMLP_GROUPS, GMLP_BLOCK, GMLP_BLOCK), BF16)],
        compiler_params=_params(("arbitrary",)),
    )(za, ln_w, ln_b, ws, bs_col)


def _gmlp_bwd(name, za, dout, ln_w, ln_b, ws, bs_col):
    nrow = za.shape[0]
    tm = GMLP_TM
    width = GMLP_GROUPS * GMLP_BLOCK

    def body(za_ref, do_ref, lnw_ref, lnb_ref, ws_ref, bs_ref, dza_ref, dlnw_ref, dlnb_ref, dws_ref, dbs_ref, wm_ref):
        mask = _gmlp_mask()
        for g in range(GMLP_GROUPS):
            wm_ref[g] = jnp.where(mask, ws_ref[g], 0.0).astype(BF16)

        @pl.when(pl.program_id(0) == 0)
        def _():
            dlnw_ref[...] = jnp.zeros_like(dlnw_ref)
            dlnb_ref[...] = jnp.zeros_like(dlnb_ref)
            dws_ref[...] = jnp.zeros_like(dws_ref)
            dbs_ref[...] = jnp.zeros_like(dbs_ref)

        def block(n, carry):
            rows = pl.ds(pl.multiple_of(n * GMLP_BLOCK, GMLP_BLOCK), GMLP_BLOCK)
            for g in range(GMLP_GROUPS):
                cols = slice(g * GMLP_BLOCK, (g + 1) * GMLP_BLOCK)
                vcols = slice(width + g * GMLP_BLOCK, width + (g + 1) * GMLP_BLOCK)
                u, slope_u = _gelu_and_slope(za_ref[rows, cols].astype(F32))
                v, slope_v = _gelu_and_slope(za_ref[rows, vcols].astype(F32))
                vn, ln_back = _layernorm_and_back(v, lnw_ref[g:g + 1, :], lnb_ref[g:g + 1, :])
                sv = _dot(wm_ref[g], vn) + bs_ref[g]
                d_o = do_ref[rows, cols].astype(F32)
                dsv = d_o * u
                d_wm = _dot(dsv, vn, 1, 1)
                dvn = _dot(wm_ref[g], dsv, 0, 0)
                dv, dlnw, dlnb = ln_back(dvn)
                dza_ref[rows, cols] = (d_o * sv * slope_u).astype(dza_ref.dtype)
                dza_ref[rows, vcols] = (dv * slope_v).astype(dza_ref.dtype)
                dlnw_ref[g:g + 1, :] += dlnw
                dlnb_ref[g:g + 1, :] += dlnb
                dws_ref[g] += jnp.where(mask, d_wm, 0.0)
                dbs_ref[g] += jnp.sum(dsv, axis=1, keepdims=True)
            return carry

        lax.fori_loop(0, tm // GMLP_BLOCK, block, 0)

    small = lambda a: pl.BlockSpec(a.shape, lambda i, nd=a.ndim: (0,) * nd)
    return pl.pallas_call(
        body, name=name, grid=(nrow // tm,),
        in_specs=[pl.BlockSpec((tm, 2 * width), lambda i: (i, 0)), pl.BlockSpec((tm, width), lambda i: (i, 0)),
                  small(ln_w), small(ln_b), small(ws), small(bs_col)],
        out_specs=[pl.BlockSpec((tm, 2 * width), lambda i: (i, 0)), small(ln_w), small(ln_b), small(ws), small(bs_col)],
        out_shape=[jax.ShapeDtypeStruct((nrow, 2 * width), BF16), jax.ShapeDtypeStruct(ln_w.shape, F32),
                   jax.ShapeDtypeStruct(ln_b.shape, F32), jax.ShapeDtypeStruct(ws.shape, F32),
                   jax.ShapeDtypeStruct(bs_col.shape, F32)],
        scratch_shapes=[pltpu.VMEM((GMLP_GROUPS, GMLP_BLOCK, GMLP_BLOCK), BF16)],
        compiler_params=_params(("arbitrary",)),
    )(za, dout, ln_w, ln_b, ws, bs_col)


CONV_TM = 512
CONV_RS = 32
HALO = 2 * SUBLANES


def _tap_rows(w_ref):
    return [w_ref[k:k + 1, :] for k in range(w_ref.shape[0])]


def _halo_specs(nrow, tm, tc):
    per = tm // HALO
    last = nrow // HALO - 1
    main = pl.BlockSpec((tm, tc), lambda j, i: (i, j))
    before = pl.BlockSpec((HALO, tc), lambda j, i: (jnp.maximum(i * per - 1, 0), j))
    after = pl.BlockSpec((HALO, tc), lambda j, i: (jnp.minimum((i + 1) * per, last), j))
    return main, before, after


def _col_spec(rows, tc):
    return pl.BlockSpec((rows, tc), lambda j, i: (0, j))


def _conv_fwd(name, x, w, b, *, tc):
    nrow, ncol = x.shape
    taps = w.shape[0]
    tm, rs = CONV_TM, CONV_RS
    main, before, _ = _halo_specs(nrow, tm, tc)

    def body(x_ref, xb_ref, w_ref, b_ref, o_ref, xw_ref):
        first = pl.program_id(1) == 0
        wv, bv = _tap_rows(w_ref), b_ref[...]
        xw_ref[0:HALO, :] = jnp.where(first, 0.0, xb_ref[...].astype(F32))
        for r in range(tm // rs):
            xw_ref[HALO + r * rs:HALO + (r + 1) * rs, :] = x_ref[r * rs:(r + 1) * rs, :].astype(F32)
        for r in range(tm // rs):
            base = HALO + r * rs
            out = bv + wv[taps - 1] * xw_ref[base:base + rs, :]
            for k in range(taps - 1):
                back = taps - 1 - k
                out = out + wv[k] * xw_ref[base - back:base - back + rs, :]
            o_ref[r * rs:(r + 1) * rs, :] = out.astype(o_ref.dtype)

    return pl.pallas_call(
        body, name=name, grid=(ncol // tc, nrow // tm),
        in_specs=[main, before, _col_spec(taps, tc), _col_spec(1, tc)],
        out_specs=main, out_shape=jax.ShapeDtypeStruct((nrow, ncol), BF16),
        scratch_shapes=[pltpu.VMEM((HALO + tm, tc), F32)],
        compiler_params=_params(("parallel", "arbitrary")),
    )(x, x, w, b)


def _conv_bwd(name, dpre, x, w, *, tc):
    nrow, ncol = x.shape
    taps = w.shape[0]
    tm, rs = CONV_TM, CONV_RS
    nsub = tm // rs
    main, before, after = _halo_specs(nrow, tm, tc)

    def fold(v):
        total = v[0:SUBLANES]
        for q in range(1, rs // SUBLANES):
            total = total + v[q * SUBLANES:(q + 1) * SUBLANES]
        return total

    def body(d_ref, da_ref, x_ref, xb_ref, w_ref, dx_ref, dw_ref, db_ref, dwin_ref, xwin_ref):
        i = pl.program_id(1)
        first, last = i == 0, i == pl.num_programs(1) - 1
        wv = _tap_rows(w_ref)

        @pl.when(first)
        def _():
            dw_ref[...] = jnp.zeros_like(dw_ref)
            db_ref[...] = jnp.zeros_like(db_ref)

        xwin_ref[0:HALO, :] = jnp.where(first, 0.0, xb_ref[...].astype(F32))
        dwin_ref[tm:, :] = jnp.where(last, 0.0, da_ref[...].astype(F32))
        for r in range(nsub):
            dwin_ref[r * rs:(r + 1) * rs, :] = d_ref[r * rs:(r + 1) * rs, :].astype(F32)
            xwin_ref[HALO + r * rs:HALO + (r + 1) * rs, :] = x_ref[r * rs:(r + 1) * rs, :].astype(F32)
        dw = [jnp.zeros((SUBLANES, tc), F32)] * taps
        db = jnp.zeros((SUBLANES, tc), F32)
        for r in range(nsub):
            cur = dwin_ref[r * rs:(r + 1) * rs, :]
            dx = wv[taps - 1] * cur
            for k in range(taps - 1):
                ahead = taps - 1 - k
                dx = dx + wv[k] * dwin_ref[r * rs + ahead:(r + 1) * rs + ahead, :]
            dx_ref[r * rs:(r + 1) * rs, :] = dx.astype(dx_ref.dtype)
            for k in range(taps):
                back = taps - 1 - k
                dw[k] = dw[k] + fold(cur * xwin_ref[HALO + r * rs - back:HALO + (r + 1) * rs - back, :])
            db = db + fold(cur)
        for k in range(taps):
            dw_ref[k:k + 1, :] += _colsum(dw[k])
        db_ref[...] += _colsum(db)

    return pl.pallas_call(
        body, name=name, grid=(ncol // tc, nrow // tm),
        in_specs=[main, after, main, before, _col_spec(taps, tc)],
        out_specs=[main, _col_spec(taps, tc), _col_spec(1, tc)],
        out_shape=[jax.ShapeDtypeStruct((nrow, ncol), BF16), jax.ShapeDtypeStruct((taps, ncol), F32),
                   jax.ShapeDtypeStruct((1, ncol), F32)],
        scratch_shapes=[pltpu.VMEM((tm + HALO, tc), F32), pltpu.VMEM((HALO + tm, tc), F32)],
        compiler_params=_params(("parallel", "arbitrary")),
    )(dpre, dpre, x, x, w)


def _glu(gate, val):
    return jax.nn.silu(gate) * val


def _ffn_act_fwd(name, pg, pv, wg, wv, bg, bv, *, tc):
    nrow, ncol = pg.shape
    taps = wg.shape[0]
    tm, rs = CONV_TM, CONV_RS
    main, before, _ = _halo_specs(nrow, tm, tc)

    def body(pg_ref, pgb_ref, pv_ref, pvb_ref, wg_ref, wv_ref, bg_ref, bv_ref, g_ref, v_ref, a_ref, gwin_ref, vwin_ref):
        first = pl.program_id(1) == 0
        taps_g, taps_v, bgv, bvv = _tap_rows(wg_ref), _tap_rows(wv_ref), bg_ref[...], bv_ref[...]
        gwin_ref[0:HALO, :] = jnp.where(first, 0.0, pgb_ref[...].astype(F32))
        vwin_ref[0:HALO, :] = jnp.where(first, 0.0, pvb_ref[...].astype(F32))
        for r in range(tm // rs):
            gwin_ref[HALO + r * rs:HALO + (r + 1) * rs, :] = pg_ref[r * rs:(r + 1) * rs, :].astype(F32)
            vwin_ref[HALO + r * rs:HALO + (r + 1) * rs, :] = pv_ref[r * rs:(r + 1) * rs, :].astype(F32)

        def conv(win_ref, tap_rows, bias, r):
            base = HALO + r * rs
            out = bias + tap_rows[taps - 1] * win_ref[base:base + rs, :]
            for k in range(taps - 1):
                back = taps - 1 - k
                out = out + tap_rows[k] * win_ref[base - back:base - back + rs, :]
            return out

        for r in range(tm // rs):
            sl = slice(r * rs, (r + 1) * rs)
            gate, val = conv(gwin_ref, taps_g, bgv, r), conv(vwin_ref, taps_v, bvv, r)
            g_ref[sl, :] = gate.astype(g_ref.dtype)
            v_ref[sl, :] = val.astype(v_ref.dtype)
            a_ref[sl, :] = _glu(gate, val).astype(a_ref.dtype)

    return pl.pallas_call(
        body, name=name, grid=(ncol // tc, nrow // tm),
        in_specs=[main, before, main, before, _col_spec(taps, tc), _col_spec(taps, tc), _col_spec(1, tc), _col_spec(1, tc)],
        out_specs=[main, main, main],
        out_shape=[jax.ShapeDtypeStruct((nrow, ncol), BF16)] * 3,
        scratch_shapes=[pltpu.VMEM((HALO + tm, tc), F32), pltpu.VMEM((HALO + tm, tc), F32)],
        compiler_params=_params(("parallel", "arbitrary")),
    )(pg, pg, pv, pv, wg, wv, bg, bv)


def _ffn_act_bwd(name, dact, gate, val):
    def fn(dv, gv, vv):
        _, vjp = jax.vjp(_glu, gv, vv)
        dg, dval = vjp(dv)
        return (dg, dval), ()
    width = dact.shape[1]
    return _rows(name, fn, [(dact, width, 0), (gate, width, 0), (val, width, 0)], [],
                 [(width, BF16), (width, BF16)], [], tm=512, rs=2 * SUBLANES)


SSD_TM = 512
SSD_CHUNKS = SSD_TM // CHUNK
X_OFF, B_OFF, C_OFF = 0, SSM_INNER, SSM_INNER + SSM_GROUPS * SSM_STATE
HP = SSM_HPG * SSM_HEAD_DIM


def _causal_tiled():
    row = lax.broadcasted_iota(jnp.int32, (CHUNK, HP), 0)
    src = lax.broadcasted_iota(jnp.int32, (CHUNK, HP), 1) & (CHUNK - 1)
    return src <= row


def _split2(v):
    hi = v.astype(BF16)
    return hi, (v - hi.astype(F32)).astype(BF16)


def _dot_exact(a, ind):
    hi, lo = (lax.dot_general(p, ind, (((1,), (0,)), ((), ())), preferred_element_type=F32) for p in _split2(a))
    return hi + lo


def _head_indicator():
    head = lax.broadcasted_iota(jnp.int32, (SSM_HEADS, SSM_INNER), 0)
    chan = lax.broadcasted_iota(jnp.int32, (SSM_HEADS, SSM_INNER), 1)
    return (chan // SSM_HEAD_DIM == head).astype(BF16)


def _chunk_decays(ci, dt_ref, ac_ref, ind, ax_ref, dtx_ref, eax_ref, eex_ref, tail_ref):
    rows = pl.ds(pl.multiple_of(ci * CHUNK, CHUNK), CHUNK)
    ax_ref[...] = _dot_exact(ac_ref[rows, :], ind)
    dtx_ref[...] = _dot_exact(dt_ref[rows, :], ind)
    eax_ref[...] = jnp.exp(ax_ref[...])
    eex_ref[...] = jnp.exp(ax_ref[CHUNK - 1:CHUNK, :] - ax_ref[...])
    tail = pl.ds(pl.multiple_of(ci * CHUNK + CHUNK - SUBLANES, SUBLANES), SUBLANES)
    tail_ref[...] = jnp.exp(ac_ref[tail, :])


def _group_decay(ci, g, ax_ref, af_ref, xbc_ref, causal):
    gcols = slice(g * HP, (g + 1) * HP)
    bm = xbc_ref[:, B_OFF + g * SSM_STATE:B_OFF + (g + 1) * SSM_STATE]
    cm = xbc_ref[:, C_OFF + g * SSM_STATE:C_OFF + (g + 1) * SSM_STATE]
    cb_tiled = _dot(cm, jnp.concatenate([bm] * SSM_HPG, axis=0), 1, 1)
    seg = ax_ref[:, gcols] - af_ref[ci, :, gcols]
    decay = jnp.where(causal, jnp.exp(jnp.where(causal, seg, 0.0)), 0.0)
    return bm, cm, cb_tiled * decay, decay


def _ssd_fwd(name, pre, dt, a_cum, a_flat, d_x, ind, shards):
    nrow = pre.shape[0]
    tm = SSD_TM
    nstep = nrow // tm
    ng = len(shards)

    def body(pre_ref, dt_ref, ac_ref, af_ref, dx_ref, ind_ref, *rest):
        shard_refs, (y_ref, st_ref), stack_refs = rest[:ng], rest[ng:ng + 2], rest[ng + 2:2 * ng + 2]
        (h_ref, xbc_ref, ax_ref, dtx_ref, eax_ref, eex_ref, m_ref, xd_ref, yd_ref, tail_ref,
         send_sems, recv_sems) = rest[2 * ng + 2:]
        step = pl.program_id(0)
        start, forward, finish = _gather_phases([s.shape[0] for s in shards], ng, shard_refs, stack_refs,
                                                send_sems, recv_sems)

        @pl.when(step == 0)
        def _():
            h_ref[...] = jnp.zeros_like(h_ref)
            start()

        @pl.when(step == nstep // 2)
        def _():
            forward()

        causal = _causal_tiled()
        ind = ind_ref[...]

        def chunk(ci, carry):
            rows = pl.ds(pl.multiple_of(ci * CHUNK, CHUNK), CHUNK)
            xbc_ref[...] = jax.nn.silu(pre_ref[rows, :].astype(F32))
            _chunk_decays(ci, dt_ref, ac_ref, ind, ax_ref, dtx_ref, eax_ref, eex_ref, tail_ref)
            st_ref[ci] = h_ref[...].astype(st_ref.dtype)
            for g in range(SSM_GROUPS):
                gcols = slice(g * HP, (g + 1) * HP)
                bm, cm, m_all, _ = _group_decay(ci, g, ax_ref, af_ref, xbc_ref, causal)
                m_ref[...] = m_all
                x_g = xbc_ref[:, gcols]
                xd = x_g * dtx_ref[:, gcols]
                xd_ref[...] = xd
                h_g = h_ref[gcols, :]
                for hh in range(SSM_HPG):
                    lc = slice(hh * SSM_HEAD_DIM, (hh + 1) * SSM_HEAD_DIM)
                    yd_ref[:, lc] = _dot(m_ref[:, lc], xd_ref[:, lc])
                y_ref[rows, gcols] = (yd_ref[...] + _dot(cm, h_g, 1, 1) * eax_ref[:, gcols]
                                      + dx_ref[:, gcols] * x_g).astype(y_ref.dtype)
                new = _dot(xd * eex_ref[:, gcols], bm, 0, 0)
                for hh in range(SSM_HPG):
                    h = g * SSM_HPG + hh
                    hrows = slice(h * SSM_HEAD_DIM, (h + 1) * SSM_HEAD_DIM)
                    lrows = slice(hh * SSM_HEAD_DIM, (hh + 1) * SSM_HEAD_DIM)
                    h_ref[hrows, :] = tail_ref[SUBLANES - 1:SUBLANES, h:h + 1] * h_ref[hrows, :] + new[lrows, :]
            return carry

        lax.fori_loop(0, SSD_CHUNKS, chunk, 0)

        @pl.when(step == nstep - 1)
        def _():
            finish()

    nchunk = nrow // CHUNK
    whole = lambda a: pl.BlockSpec(a.shape, lambda i, nd=a.ndim: (0,) * nd)
    hbm = pl.BlockSpec(memory_space=pl.ANY)
    wide = lambda: pltpu.VMEM((CHUNK, SSM_INNER), F32)
    group = lambda: pltpu.VMEM((CHUNK, HP), F32)
    res = pl.pallas_call(
        body, name=name, grid=(nstep,),
        in_specs=[pl.BlockSpec((tm, SSM_XBC), lambda i: (i, 0)), pl.BlockSpec((tm, SSM_HEADS), lambda i: (i, 0)),
                  pl.BlockSpec((tm, SSM_HEADS), lambda i: (i, 0)),
                  pl.BlockSpec((SSD_CHUNKS, 1, SSM_INNER), lambda i: (i, 0, 0)), whole(d_x), whole(ind)] + [hbm] * ng,
        out_specs=[pl.BlockSpec((tm, SSM_INNER), lambda i: (i, 0)),
                   pl.BlockSpec((SSD_CHUNKS, SSM_INNER, SSM_STATE), lambda i: (i, 0, 0))] + [hbm] * ng,
        out_shape=[jax.ShapeDtypeStruct((nrow, SSM_INNER), BF16),
                   jax.ShapeDtypeStruct((nchunk, SSM_INNER, SSM_STATE), BF16)]
        + [jax.ShapeDtypeStruct((N_CHIPS,) + s.shape, s.dtype) for s in shards],
        scratch_shapes=[pltpu.VMEM((SSM_INNER, SSM_STATE), F32), pltpu.VMEM((CHUNK, SSM_XBC), F32),
                        wide(), wide(), wide(), wide(), group(), group(), group(),
                        pltpu.VMEM((SUBLANES, SSM_HEADS), F32)] + _exchange_scratch(ng, GATHER_SEMS),
        compiler_params=_params(("arbitrary",)),
    )(pre, dt, a_cum, a_flat, d_x, ind, *shards)
    return res[0], res[1], res[2:]


def _ssd_bwd(name, pre, dt, a_cum, a_flat, d_x, ind, ind_t, states, dy, pairs):
    nrow = pre.shape[0]
    tm = SSD_TM
    ntile = nrow // tm
    npair = len(pairs)

    def body(pre_ref, dt_ref, ac_ref, af_ref, dx_ref, ind_ref, indt_ref, st_ref, dy_ref, *rest):
        pair_refs = rest[:npair]
        dpre_ref, ddt_ref, da_ref, daf_ref, dd_ref = rest[npair:npair + 5]
        recv_refs = rest[npair + 5:2 * npair + 5]
        (dh_ref, xbc_ref, dxbc_ref, ax_ref, dtx_ref, eax_ref, eex_ref, red_ref,
         m_ref, l_ref, xd_ref, dm_ref, dxd_ref, fold_ref, hd_ref, tail_ref, send_sems, recv_sems) = rest[2 * npair + 5:]
        start, finish = _scatter_phases(pair_refs, recv_refs, send_sems, recv_sems)

        @pl.when(pl.program_id(0) == 0)
        def _():
            dh_ref[...] = jnp.zeros_like(dh_ref)
            dd_ref[...] = jnp.zeros_like(dd_ref)
            start()

        causal = _causal_tiled()
        ind, ind_t = ind_ref[...], indt_ref[...]
        is_last_row = lax.broadcasted_iota(jnp.int32, (CHUNK, 1), 0) == CHUNK - 1
        ones = jnp.ones((CHUNK, SSM_STATE), BF16)

        def chunk(k, ddx):
            ci = SSD_CHUNKS - 1 - k
            rows = pl.ds(pl.multiple_of(ci * CHUNK, CHUNK), CHUNK)
            pre_v = pre_ref[rows, :].astype(F32)
            xbc_ref[...] = jax.nn.silu(pre_v)
            _chunk_decays(ci, dt_ref, ac_ref, ind, ax_ref, dtx_ref, eax_ref, eex_ref, tail_ref)
            ddx_parts = []
            for g in range(SSM_GROUPS):
                gcols = slice(g * HP, (g + 1) * HP)
                bcols = slice(B_OFF + g * SSM_STATE, B_OFF + (g + 1) * SSM_STATE)
                ccols = slice(C_OFF + g * SSM_STATE, C_OFF + (g + 1) * SSM_STATE)
                bm, cm, m_all, decay = _group_decay(ci, g, ax_ref, af_ref, xbc_ref, causal)
                m_ref[...] = m_all
                l_ref[...] = decay
                x_g = xbc_ref[:, gcols]
                xd = x_g * dtx_ref[:, gcols]
                xd_ref[...] = xd
                h_g = st_ref[ci, gcols, :]
                dh_g = dh_ref[gcols, :]
                dy_g = dy_ref[rows, gcols]
                for hh in range(SSM_HPG):
                    h = g * SSM_HPG + hh
                    hcols = slice(h * SSM_HEAD_DIM, (h + 1) * SSM_HEAD_DIM)
                    lc = slice(hh * SSM_HEAD_DIM, (hh + 1) * SSM_HEAD_DIM)
                    dy_h = dy_ref[rows, hcols]
                    dm_ref[:, lc] = _dot(dy_h, xd_ref[:, lc], 1, 1)
                    dxd_ref[:, lc] = _dot(m_ref[:, lc], dy_h, 0, 0)
                ebdh = eex_ref[:, gcols] * _dot(bm, dh_g, 1, 1)
                dxd = dxd_ref[...] + ebdh
                dm = dm_ref[...]
                t = dm * l_ref[...]
                t128 = (t[:, 0:LANES] + t[:, LANES:2 * LANES]) + (t[:, 2 * LANES:3 * LANES] + t[:, 3 * LANES:])
                fold_ref[...] = t128 + pltpu.roll(t128, CHUNK, axis=1)
                dw_sum = fold_ref[:, 0:CHUNK]
                q = dm * m_ref[...]
                dyea = dy_g * eax_ref[:, gcols]
                red_ref[0:CHUNK, gcols] = q + dyea * _dot(cm, h_g, 1, 1)
                red_ref[CHUNK:2 * CHUNK, gcols] = xd * ebdh
                red_ref[2 * CHUNK:3 * CHUNK, gcols] = dxd * x_g
                daf_ref[ci, :, gcols] = -jnp.sum(q, axis=0, keepdims=True)
                ddx_parts.append(jnp.sum(dy_g * x_g, axis=0, keepdims=True))
                dxbc_ref[:, gcols] = dxd * dtx_ref[:, gcols] + dx_ref[:, gcols] * dy_g
                dxbc_ref[:, ccols] = _dot(dw_sum, bm) + _dot(dyea, h_g)
                dxbc_ref[:, bcols] = _dot(dw_sum, cm, 0, 0) + _dot(xd * eex_ref[:, gcols], dh_g)
                dh_new = _dot(dyea, cm, 0, 0)
                for hh in range(SSM_HPG):
                    h = g * SSM_HPG + hh
                    hrows = slice(h * SSM_HEAD_DIM, (h + 1) * SSM_HEAD_DIM)
                    lrows = slice(hh * SSM_HEAD_DIM, (hh + 1) * SSM_HEAD_DIM)
                    hd_ref[h:h + 1, :] = jnp.sum(st_ref[ci, hrows, :] * dh_ref[hrows, :], axis=0, keepdims=True)
                    dh_ref[hrows, :] = tail_ref[SUBLANES - 1:SUBLANES, h:h + 1] * dh_ref[hrows, :] + dh_new[lrows, :]
            sums = _dot_exact(red_ref[...], ind_t)
            ra, ts = sums[:CHUNK], sums[CHUNK:2 * CHUNK]
            hdh = sum(lax.dot_general(ones, p, (((1,), (1,)), ((), ())), preferred_element_type=F32)
                      for p in _split2(hd_ref[...]))
            da_last = jnp.sum(ts, axis=0, keepdims=True) + tail_ref[SUBLANES - 1:SUBLANES, :] * hdh
            da_ref[rows, :] = ra - ts + jnp.where(is_last_row, da_last, 0.0)
            ddt_ref[rows, :] = sums[2 * CHUNK:]
            sig = jax.nn.sigmoid(pre_v)
            dpre_ref[rows, :] = (dxbc_ref[...] * (sig * (1.0 + pre_v * (1.0 - sig)))).astype(dpre_ref.dtype)
            return ddx + jnp.concatenate(ddx_parts, axis=1)

        ddx = lax.fori_loop(0, SSD_CHUNKS, chunk, jnp.zeros((1, SSM_INNER), F32))
        dd_ref[...] += _dot_exact(jnp.broadcast_to(ddx, (SUBLANES, SSM_INNER)), ind_t)

        @pl.when(pl.program_id(0) == ntile - 1)
        def _():
            finish()

    rev = lambda i: ntile - 1 - i
    whole = lambda a: pl.BlockSpec(a.shape, lambda i, nd=a.ndim: (0,) * nd)
    hbm = pl.BlockSpec(memory_space=pl.ANY)
    wide = lambda: pltpu.VMEM((CHUNK, SSM_INNER), F32)
    group = lambda: pltpu.VMEM((CHUNK, HP), F32)
    res = pl.pallas_call(
        body, name=name, grid=(ntile,),
        in_specs=[pl.BlockSpec((tm, SSM_XBC), lambda i: (rev(i), 0)), pl.BlockSpec((tm, SSM_HEADS), lambda i: (rev(i), 0)),
                  pl.BlockSpec((tm, SSM_HEADS), lambda i: (rev(i), 0)),
                  pl.BlockSpec((SSD_CHUNKS, 1, SSM_INNER), lambda i: (rev(i), 0, 0)),
                  whole(d_x), whole(ind), whole(ind_t),
                  pl.BlockSpec((SSD_CHUNKS, SSM_INNER, SSM_STATE), lambda i: (rev(i), 0, 0)),
                  pl.BlockSpec((tm, SSM_INNER), lambda i: (rev(i), 0))] + [hbm] * npair,
        out_specs=[pl.BlockSpec((tm, SSM_XBC), lambda i: (rev(i), 0)), pl.BlockSpec((tm, SSM_HEADS), lambda i: (rev(i), 0)),
                   pl.BlockSpec((tm, SSM_HEADS), lambda i: (rev(i), 0)),
                   pl.BlockSpec((SSD_CHUNKS, 1, SSM_INNER), lambda i: (rev(i), 0, 0)),
                   pl.BlockSpec((SUBLANES, SSM_HEADS), lambda i: (0, 0))] + [hbm] * npair,
        out_shape=[jax.ShapeDtypeStruct((nrow, SSM_XBC), BF16), jax.ShapeDtypeStruct((nrow, SSM_HEADS), F32),
                   jax.ShapeDtypeStruct((nrow, SSM_HEADS), F32), jax.ShapeDtypeStruct((nrow // CHUNK, 1, SSM_INNER), F32),
                   jax.ShapeDtypeStruct((SUBLANES, SSM_HEADS), F32)]
        + [jax.ShapeDtypeStruct(p.shape, p.dtype) for p in pairs],
        scratch_shapes=[pltpu.VMEM((SSM_INNER, SSM_STATE), F32), pltpu.VMEM((CHUNK, SSM_XBC), F32),
                        pltpu.VMEM((CHUNK, SSM_XBC), F32), wide(), wide(), wide(), wide(),
                        pltpu.VMEM((3 * CHUNK, SSM_INNER), F32),
                        group(), group(), group(), group(), group(), pltpu.VMEM((CHUNK, LANES), F32),
                        pltpu.VMEM((SSM_HEADS, SSM_STATE), F32), pltpu.VMEM((SUBLANES, SSM_HEADS), F32)]
        + _exchange_scratch(npair, N_CHIPS - 1),
        compiler_params=_params(("arbitrary",)),
    )(pre, dt, a_cum, a_flat, d_x, ind, ind_t, states, dy, *pairs)
    return res[:5], res[5:]


LATE = ["w_proj_a", "w_proj_b", "w_out", "ffn_w_up", "ffn_w_down"]
HALF_TILES = {"w_in": 128, "w_proj_a": 128, "w_proj_b": 256, "w_out": 128, "ffn_w_up": 128, "ffn_w_down": 176}


def _late_weights(stacks, shards):
    pa, pb, out, up, down = [_own_slot(stack, own) for stack, own in zip(stacks, shards)]
    return {"w_proj_a": pa.reshape(-1, D_MODEL), "w_proj_b": pb.reshape(-1, D_MODEL), "w_out": out.reshape(-1, D_MODEL),
            "w_up_g": _columns_from_chips(up[:2]), "w_up_v": _columns_from_chips(up[2:]),
            "w_down": down.reshape(-1, D_MODEL)}


def _pair_reduce(tag, names, stacks):
    core = lax.axis_index("c")
    own_half = [_row_half(s, core, 1) for s in stacks]
    other_half = _swap_cores("pair_grads_" + tag, [_row_half(s, 1 - core, 1) for s in stacks])
    return [_pair_sum("pair_" + n, a, b, tm=HALF_TILES[n]) for n, a, b in zip(names, own_half, other_half)]


def _local_step(x, target, w, late_shards):
    w = dict(w)
    g = {}
    bs_col = w["gmlp_bs"].reshape(GMLP_GROUPS, GMLP_BLOCK, 1)
    b0, b1 = w["gate_bias"][0:1], w["gate_bias"][1:2]

    xn = _rms_fwd("mix_norm", x, w["mix_norm_w"])
    big = dict(bm=1024, bn=1024, bk=1024)
    act16 = dict(out_dtype=BF16, **big)
    gates = _mm("in_gates", xn, w["w_g"], **act16)
    za = _mm("in_gmlp", xn, w["w_za"], **act16)
    z = _mm("in_z", xn, w["w_z"], **act16)
    xbc = _mm("in_xbc", xn, w["w_xbc"], **act16)
    dt_raw = _mm("in_dt", xn, w["w_dt"], bm=1024, bn=SSM_HEADS, bk=1024)

    pre = _conv_fwd("ssm_conv_fwd", xbc, w["ssm_conv_w"], w["ssm_conv_b"], tc=1024)
    dt, a_cum = _dt_prep("dt_prep", dt_raw, w["ssm_dt_bias"], w["ssm_a_log"])
    a_flat = jnp.transpose(a_cum.reshape(-1, CHUNK, SSM_HEADS), (0, 2, 1)).reshape(-1, 1, SSM_INNER)
    d_x = jnp.repeat(w["ssm_d"], SSM_HEAD_DIM, axis=1)
    ind = _head_indicator()
    y_ssd, states, late_stacks = _ssd_fwd("ssd_fwd", pre, dt, a_cum, a_flat, d_x, ind, late_shards)
    w.update(_late_weights(late_stacks, late_shards))
    yb_pre = _gate_norm_fwd("gate_norm_fwd", y_ssd, z, w["ssm_norm_w"])
    y_b = _mm("proj_b", yb_pre, w["w_proj_b"], bm=1024, bn=1024, bk=SSM_INNER, out_dtype=BF16)

    ya_pre = _gmlp_fwd("gmlp_fwd", za, w["gmlp_ln_w"], w["gmlp_ln_b"], w["gmlp_ws"], bs_col)
    y_a = _mm("proj_a", ya_pre, w["w_proj_a"], **act16)

    merged = _merge_fwd("merge_fwd", gates, y_a, y_b, b0, b1)
    h1 = _mm("out_proj", merged, w["w_out"], res=x, **big)

    hn = _rms_fwd("ffn_norm", h1, w["ffn_norm_w"])
    half = dict(bm=1024, bn=D_FF // 2, bk=1024, out_dtype=BF16)
    pg = _mm("ffn_up_gate", hn, w["w_up_g"], **half)
    pv = _mm("ffn_up_val", hn, w["w_up_v"], **half)
    cw, cb = w["ffn_conv_w"], w["ffn_conv_b"]
    gate, val, act = _ffn_act_fwd("ffn_act_fwd", pg, pv, cw[:, :D_FF], cw[:, D_FF:], cb[:, :D_FF], cb[:, D_FF:],
                                  tc=D_FF // 2)
    h2 = _mm("ffn_down", act, w["w_down"], res=h1, bm=1024, bn=1024, bk=D_FF // 2)

    dh2, loss_part, g["final_norm_w"] = _final_loss("final_loss", h2, target, w["final_norm_w"].reshape(1, D_MODEL))

    dact = _mm("d_act", dh2, w["w_down"], tb=True, **half)
    wgrad = dict(ta=True, bk=min(2048, x.shape[0]), out_dtype=BF16)
    g["w_down"] = _mm("dw_down", act, dh2, bm=D_FF // 2, bn=1024, **wgrad)
    dgate, dval = _ffn_act_bwd("ffn_act_bwd", dact, gate, val)
    dpg, dcwg, dcbg = _conv_bwd("ffn_conv_bwd_gate", dgate, pg, cw[:, :D_FF], tc=D_FF // 2)
    dpv, dcwv, dcbv = _conv_bwd("ffn_conv_bwd_val", dval, pv, cw[:, D_FF:], tc=D_FF // 2)
    g["ffn_conv_w"] = jnp.concatenate([dcwg, dcwv], axis=1)
    g["ffn_conv_b"] = jnp.concatenate([dcbg, dcbv], axis=1)
    dhn = _mm_sum("d_hn", [(dpg, w["w_up_g"]), (dpv, w["w_up_v"])], bm=1024, bk=D_FF // 2)
    g["w_up_g"] = _mm("dw_up_gate", hn, dpg, bm=1024, bn=D_FF // 2, **wgrad)
    g["w_up_v"] = _mm("dw_up_val", hn, dpv, bm=1024, bn=D_FF // 2, **wgrad)
    dh1, g["ffn_norm_w"] = _rms_bwd("ffn_norm_bwd", h1, w["ffn_norm_w"], dhn, dh2)

    dmerged = _mm("d_merged", dh1, w["w_out"], tb=True, **act16)
    g["w_out"] = _mm("dw_out", merged, dh1, bm=1024, bn=1024, **wgrad)
    dgates, dya, dyb, db0, db1 = _merge_bwd("merge_bwd", gates, y_a, y_b, dmerged, b0, b1)
    g["gate_bias"] = jnp.concatenate([db0, db1], axis=0)

    dya_pre = _mm("d_ya_pre", dya, w["w_proj_a"], tb=True, **act16)
    g["w_proj_a"] = _mm("dw_proj_a", ya_pre, dya, bm=1024, bn=1024, **wgrad)
    dyb_pre = _mm("d_yb_pre", dyb, w["w_proj_b"], tb=True, **act16)
    g["w_proj_b"] = _mm("dw_proj_b", yb_pre, dyb, bm=1024, bn=1024, **wgrad)
    late_pairs = _pair_reduce("late", LATE, [
        g["w_proj_a"].reshape(N_CHIPS, -1, D_MODEL), g["w_proj_b"].reshape(N_CHIPS, -1, D_MODEL),
        g["w_out"].reshape(N_CHIPS, -1, D_MODEL),
        jnp.concatenate([_columns_to_chips(g["w_up_g"], 2), _columns_to_chips(g["w_up_v"], 2)], axis=0),
        g["w_down"].reshape(N_CHIPS, -1, D_MODEL)])

    dy_ssd, dz, g["ssm_norm_w"] = _gate_norm_bwd("gate_norm_bwd", y_ssd, z, dyb_pre, w["ssm_norm_w"])
    (dpre, ddt, da_tok, da_flat, dd), late_received = _ssd_bwd(
        "ssd_bwd", pre, dt, a_cum, a_flat, d_x, ind, ind.T, states, dy_ssd, late_pairs)
    g["ssm_d"] = dd[0:1]
    da_src = jnp.transpose(da_flat.reshape(-1, SSM_HEADS, CHUNK), (0, 2, 1)).reshape(-1, SSM_HEADS)
    ddt_raw, g["ssm_dt_bias"], g["ssm_a_log"] = _dt_bwd("dt_bwd", dt_raw, ddt, da_tok, da_src,
                                                         w["ssm_dt_bias"], w["ssm_a_log"])
    dxbc, g["ssm_conv_w"], g["ssm_conv_b"] = _conv_bwd("ssm_conv_bwd", dpre, xbc, w["ssm_conv_w"], tc=1024)

    dza, g["gmlp_ln_w"], g["gmlp_ln_b"], g["gmlp_ws"], dbs = _gmlp_bwd(
        "gmlp_bwd", za, dya_pre, w["gmlp_ln_w"], w["gmlp_ln_b"], w["gmlp_ws"], bs_col)
    g["gmlp_bs"] = dbs.reshape(GMLP_GROUPS, GMLP_BLOCK)

    dw_in = jnp.concatenate([
        _mm("dw_gates", xn, dgates, bm=1024, bn=1024, **wgrad), _mm("dw_gmlp", xn, dza, bm=1024, bn=1024, **wgrad),
        _mm("dw_z", xn, dz, bm=1024, bn=1024, **wgrad), _mm("dw_xbc", xn, dxbc, bm=1024, bn=1024, **wgrad),
        _mm("dw_dt", xn, ddt_raw, bm=1024, bn=SSM_HEADS, **wgrad)], axis=1)
    in_pairs = _pair_reduce("in", ["w_in"], [_columns_to_chips(dw_in)])
    dxn, in_received = _mm_sum("d_xn", [(dgates, w["w_g"]), (dza, w["w_za"]), (dz, w["w_z"]), (dxbc, w["w_xbc"]),
                                        (ddt_raw, w["w_dt"])], bm=1024, bk=1024, exchange=in_pairs)
    grad_x, g["mix_norm_w"] = _rms_bwd("mix_norm_bwd", x, w["mix_norm_w"], dxn, dh1)
    return loss_part, grad_x, g, in_pairs + list(late_pairs), list(in_received) + list(late_received)


def _position():
    return lax.axis_index("x"), lax.axis_index("y"), lax.axis_index("c")


def _own_slot(stack, own):
    chip = 2 * lax.axis_index("x") + lax.axis_index("y")
    return lax.dynamic_update_index_in_dim(stack, own, chip, axis=0)


def _scatter_phases(ins, outs, send_sems, recv_sems):
    n = len(ins)
    x, y, c = _position()
    me = 2 * x + y
    peers = [(1 - x, y), (x, 1 - y), (1 - x, 1 - y)]

    def copy(i, k, src_slot, dst_slot):
        px, py = peers[k]
        return pltpu.make_async_remote_copy(
            src_ref=ins[i].at[src_slot], dst_ref=outs[i].at[dst_slot],
            send_sem=send_sems.at[i, k], recv_sem=recv_sems.at[i, k],
            device_id=(px, py, c), device_id_type=MESH)

    def start():
        for i in range(n):
            for k, (px, py) in enumerate(peers):
                copy(i, k, 2 * px + py, me).start()

    def finish():
        for i in range(n):
            for k, (px, py) in enumerate(peers):
                copy(i, k, me, 2 * px + py).wait_recv()
        for i in range(n):
            for k, (px, py) in enumerate(peers):
                copy(i, k, 2 * px + py, me).wait_send()

    return start, finish


def _exchange_scratch(n, per_array):
    return [pltpu.SemaphoreType.DMA((n, per_array)), pltpu.SemaphoreType.DMA((n, per_array))]


def _half_rows(ref_rows, which):
    half = ref_rows // 2
    return pl.ds(pl.multiple_of(which * half, 2 * SUBLANES), half)


GATHER_SEMS = 2 * (N_CHIPS - 1)


def _gather_phases(nrows, ns, ins, outs, send_sems, recv_sems):
    n = len(ins)
    x, y, c = _position()
    me = 2 * x + y
    sibling = (x, y, 1 - c)
    chips = [(1 - x, y), (x, 1 - y), (1 - x, 1 - y)]

    def remote(i, k, src, dst, to):
        return pltpu.make_async_remote_copy(src_ref=src, dst_ref=dst, send_sem=send_sems.at[i, k],
                                            recv_sem=recv_sems.at[i, k], device_id=to, device_id_type=MESH)

    def over_ici(i, k):
        px, py = chips[k]
        rows = _half_rows(nrows[i], c) if i < ns else slice(None)
        return remote(i, k, ins[i].at[rows], outs[i].at[me, rows], (px, py, c))

    def landed(i, k, which):
        px, py = chips[k]
        return outs[i].at[2 * px + py, _half_rows(nrows[i], which)] if i < ns else outs[i].at[2 * px + py]

    def start():
        for i in range(n):
            for k in range(N_CHIPS - 1):
                over_ici(i, k).start()

    def forward():
        for i in range(n):
            for k in range(N_CHIPS - 1):
                piece = landed(i, k, c)
                remote(i, k, piece, piece, (*chips[k], c)).wait_recv()
                if i < ns:
                    remote(i, N_CHIPS - 1 + k, piece, piece, sibling).start()

    def finish():
        for i in range(ns):
            for k in range(N_CHIPS - 1):
                piece = landed(i, k, 1 - c)
                remote(i, N_CHIPS - 1 + k, piece, piece, sibling).wait_recv()
        for i in range(n):
            for k in range(N_CHIPS - 1):
                over_ici(i, k).wait_send()
                if i < ns:
                    piece = landed(i, k, c)
                    remote(i, N_CHIPS - 1 + k, piece, piece, sibling).wait_send()

    return start, forward, finish


def _gather_chips_split(name, split, whole):
    arrs = list(split) + list(whole)
    n = len(arrs)

    def body(*refs):
        phases = _gather_phases([a.shape[0] for a in arrs], len(split), refs[:n], refs[n:2 * n], *refs[2 * n:])
        for phase in phases:
            phase()

    hbm = pl.BlockSpec(memory_space=pl.ANY)
    return pl.pallas_call(
        body, name=name, in_specs=[hbm] * n, out_specs=[hbm] * n,
        out_shape=[jax.ShapeDtypeStruct((N_CHIPS,) + a.shape, a.dtype) for a in arrs],
        scratch_shapes=_exchange_scratch(n, GATHER_SEMS),
        compiler_params=pltpu.CompilerParams(has_side_effects=True),
    )(*arrs)


def _swap_cores(name, arrs):
    n = len(arrs)

    def body(*refs):
        ins, outs = refs[:n], refs[n:2 * n]
        send_sems, recv_sems = refs[2 * n:]
        x, y, c = _position()
        copies = [pltpu.make_async_remote_copy(src_ref=ins[i], dst_ref=outs[i], send_sem=send_sems.at[i],
                                               recv_sem=recv_sems.at[i], device_id=(x, y, 1 - c), device_id_type=MESH)
                  for i in range(n)]
        for cp in copies:
            cp.start()
        for cp in copies:
            cp.wait_recv()
        for cp in copies:
            cp.wait_send()

    hbm = pl.BlockSpec(memory_space=pl.ANY)
    return pl.pallas_call(
        body, name=name, in_specs=[hbm] * n, out_specs=[hbm] * n,
        out_shape=[jax.ShapeDtypeStruct(a.shape, a.dtype) for a in arrs],
        scratch_shapes=[pltpu.SemaphoreType.DMA((n,)), pltpu.SemaphoreType.DMA((n,))],
        compiler_params=pltpu.CompilerParams(has_side_effects=True),
    )(*arrs)


def _row_half(a, which, axis):
    half = a.shape[axis] // 2
    return lax.dynamic_slice_in_dim(a, which * half, half, axis=axis)


def _all_reduce(name, pack):
    def body(in_ref, out_ref, buf, send_sems, recv_sems):
        x, y, c = _position()
        me = 4 * x + 2 * y + c
        flips = [(dx, dy, dc) for dx in (0, 1) for dy in (0, 1) for dc in (0, 1) if (dx, dy, dc) != (0, 0, 0)]
        peers = [((1 - x) if dx else x, (1 - y) if dy else y, (1 - c) if dc else c) for dx, dy, dc in flips]
        buf[me] = in_ref[...]
        sends = []
        for k, peer in enumerate(peers):
            cp = pltpu.make_async_remote_copy(src_ref=in_ref, dst_ref=buf.at[me], send_sem=send_sems.at[k],
                                              recv_sem=recv_sems.at[k], device_id=peer, device_id_type=MESH)
            cp.start()
            sends.append(cp)
        for k, (px, py, pc) in enumerate(peers):
            pltpu.make_async_remote_copy(src_ref=in_ref, dst_ref=buf.at[4 * px + 2 * py + pc], send_sem=send_sems.at[k],
                                         recv_sem=recv_sems.at[k], device_id=(px, py, pc), device_id_type=MESH).wait_recv()
        total = buf[0]
        for j in range(1, N_DEV):
            total = total + buf[j]
        out_ref[...] = total
        for cp in sends:
            cp.wait_send()

    vmem = pl.BlockSpec(memory_space=pltpu.VMEM)
    return pl.pallas_call(
        body, name=name, in_specs=[vmem], out_specs=vmem,
        out_shape=jax.ShapeDtypeStruct(pack.shape, F32),
        scratch_shapes=[pltpu.VMEM((N_DEV,) + pack.shape, F32), pltpu.SemaphoreType.DMA((N_DEV - 1,)),
                        pltpu.SemaphoreType.DMA((N_DEV - 1,))],
        compiler_params=pltpu.CompilerParams(has_side_effects=True, vmem_limit_bytes=VMEM_LIMIT_V7X),
    )(pack)


def _pack(arrs):
    rows = []
    for a in arrs:
        r = a.reshape(-1, LANES)
        rows.append(jnp.pad(r, ((0, -r.shape[0] % SUBLANES), (0, 0))))
    return jnp.concatenate(rows, axis=0)


def _unpack(pack, shapes):
    out, off = [], 0
    for s in shapes:
        nrow = 1
        for d in s:
            nrow *= d
        nrow //= LANES
        out.append(pack[off:off + nrow].reshape(s))
        off += nrow + (-nrow % SUBLANES)
    return out


SMALL = ["mix_norm_w", "gate_bias", "gmlp_ln_w", "gmlp_ln_b", "gmlp_ws", "gmlp_bs", "ssm_conv_w", "ssm_conv_b",
         "ssm_dt_bias", "ssm_a_log", "ssm_d", "ssm_norm_w", "ffn_norm_w", "ffn_conv_w", "ffn_conv_b", "final_norm_w"]
SMALL_SHARDED = ("gate_bias", "ssm_conv_w", "ffn_conv_w")
BIG = ["w_in", "w_proj_a", "w_proj_b", "w_out", "ffn_w_up", "ffn_w_down"]
WEIGHTS = ["mix_norm_w", "w_in", "gate_bias", "gmlp_ln_w", "gmlp_ln_b", "gmlp_ws", "gmlp_bs", "ssm_conv_w",
           "ssm_conv_b", "ssm_dt_bias", "ssm_a_log", "ssm_d", "ssm_norm_w", "w_proj_a", "w_proj_b", "w_out",
           "ffn_norm_w", "ffn_w_up", "ffn_conv_w", "ffn_conv_b", "ffn_w_down", "final_norm_w"]
IN_SPLITS = [0, 2048, 4096, 6144, 9216, 9248]


def _columns_from_chips(stack):
    return jnp.transpose(stack, (1, 0, 2)).reshape(stack.shape[1], -1)


def _columns_to_chips(full, parts=N_CHIPS):
    rows, cols = full.shape
    return jnp.transpose(full.reshape(rows, parts, cols // parts), (1, 0, 2))


def kernel(x, mix_norm_w, w_in, gate_bias, gmlp_ln_w, gmlp_ln_b, gmlp_ws, gmlp_bs, ssm_conv_w, ssm_conv_b, ssm_dt_bias, ssm_a_log, ssm_d, ssm_norm_w, w_proj_a, w_proj_b, w_out, ffn_norm_w, ffn_w_up, ffn_conv_w, ffn_conv_b, ffn_w_down, final_norm_w, loss_target, m_mix_norm_w, m_w_in, m_gate_bias, m_gmlp_ln_w, m_gmlp_ln_b, m_gmlp_ws, m_gmlp_bs, m_ssm_conv_w, m_ssm_conv_b, m_ssm_dt_bias, m_ssm_a_log, m_ssm_d, m_ssm_norm_w, m_w_proj_a, m_w_proj_b, m_w_out, m_ffn_norm_w, m_ffn_w_up, m_ffn_conv_w, m_ffn_conv_b, m_ffn_w_down, m_final_norm_w, v_mix_norm_w, v_w_in, v_gate_bias, v_gmlp_ln_w, v_gmlp_ln_b, v_gmlp_ws, v_gmlp_bs, v_ssm_conv_w, v_ssm_conv_b, v_ssm_dt_bias, v_ssm_a_log, v_ssm_d, v_ssm_norm_w, v_w_proj_a, v_w_proj_b, v_w_out, v_ffn_norm_w, v_ffn_w_up, v_ffn_conv_w, v_ffn_conv_b, v_ffn_w_down, v_final_norm_w):
    args = dict(locals())
    weights = {n: args[n] for n in WEIGHTS}
    moments_m = {n: args["m_" + n] for n in WEIGHTS}
    moments_v = {n: args["v_" + n] for n in WEIGHTS}
    chip = 2 * lax.axis_index("x") + lax.axis_index("y")

    shards = [weights["w_in"][0].astype(BF16)] + [weights[n][0] for n in SMALL_SHARDED]
    gathered = _gather_chips_split("gather_weights", shards[:1], shards[1:])
    w_in_s, gb_s, scw_s, fcw_s = [_own_slot(stack, own) for stack, own in zip(gathered, shards)]
    late_shards = [weights[n][0].astype(BF16) for n in LATE]
    w_in_full = _columns_from_chips(w_in_s)
    full = {"w_" + nm: w_in_full[:, IN_SPLITS[k]:IN_SPLITS[k + 1]] for k, nm in enumerate(["g", "za", "z", "xbc", "dt"])}
    full["gate_bias"] = _columns_from_chips(gb_s)
    full["ssm_conv_w"] = _columns_from_chips(scw_s)
    full["ffn_conv_w"] = _columns_from_chips(fcw_s)
    for n in SMALL:
        if n not in SMALL_SHARDED:
            full[n] = weights[n] if n == "final_norm_w" else weights[n][0]
    for n in ("mix_norm_w", "ffn_norm_w", "ssm_conv_b", "ssm_dt_bias", "ssm_a_log", "ssm_d", "ssm_norm_w", "ffn_conv_b"):
        full[n] = full[n].reshape(1, -1)

    loss_part, grad_x, g, pair, received = _local_step(x[0], loss_target[0], full, late_shards)

    per_head = ["ssm_dt_bias", "ssm_a_log", "ssm_d"]
    rest = [n for n in SMALL if n not in per_head]
    head_row = jnp.concatenate([g[n] for n in per_head] + [jnp.zeros((1, LANES - 3 * SSM_HEADS), F32)], axis=1)
    pack = _pack([loss_part, head_row] + [g[n] for n in rest])
    reduced = _unpack(_all_reduce("reduce_small", pack), [(1, LANES), (1, LANES)] + [g[n].shape for n in rest])
    loss = reduced[0][0, 0]
    small_grads = {n: reduced[1][:, k * SSM_HEADS:(k + 1) * SSM_HEADS] for k, n in enumerate(per_head)}
    for n, r in zip(rest, reduced[2:]):
        if n in SMALL_SHARDED:
            width = weights[n].shape[2]
            r = lax.dynamic_slice_in_dim(r, chip * width, width, axis=1)
        small_grads[n] = r
    two_d = lambda a: a.reshape(-1, a.shape[-1])
    upd = _adamw_small("adamw_small", *[[two_d(d[n]) for n in SMALL]
                                        for d in (weights, small_grads, moments_m, moments_v)])
    small_out = [[small_grads[n] for n in SMALL]] + list(upd)
    small_out = [[a.reshape(weights[n].shape) for n, a in zip(SMALL, kind)] for kind in small_out]

    received = [_own_slot(r, lax.dynamic_index_in_dim(p, chip, 0, keepdims=False)) for r, p in zip(received, pair)]
    halves = [_sum_slots("sum_" + n, r, tm=HALF_TILES[n], rs=2 * SUBLANES) for n, r in zip(BIG, received)]
    tiles = {"w_in": 128, "w_proj_a": 256, "w_proj_b": 256, "w_out": 256, "ffn_w_up": 128, "ffn_w_down": 176}
    core = lax.axis_index("c")
    other = _swap_cores("join_grads", halves)
    reduced = [jnp.concatenate([jnp.where(core == 0, a, b), jnp.where(core == 0, b, a)], axis=0)
               for a, b in zip(halves, other)]
    big_out = {}
    for n, grad in zip(BIG, reduced):
        big_out[n] = _adamw("adamw_" + n, weights[n][0], grad, moments_m[n][0], moments_v[n][0],
                            tm=tiles[n], rs=SUBLANES)

    per_kind = [[], [], [], []]
    for n in WEIGHTS:
        for kind in range(4):
            if n in big_out:
                per_kind[kind].append(big_out[n][kind].reshape(weights[n].shape))
            else:
                per_kind[kind].append(small_out[kind][SMALL.index(n)])
    return (loss, grad_x[None], *per_kind[0], *per_kind[1], *per_kind[2], *per_kind[3])
```

```python
import jax
import jax.numpy as jnp
from jax import lax
from jax.experimental import pallas as pl
from jax.experimental.pallas import tpu as pltpu

F32 = jnp.float32
BF16 = jnp.bfloat16
MESH = pl.DeviceIdType.MESH

EPS = 1e-5
D_MODEL = 1024
GMLP_BLOCK = 128
GMLP_GROUPS = 8
CHUNK = 64
SSM_INNER = 2048
SSM_HEADS = 32
SSM_HEAD_DIM = 64
SSM_GROUPS = 4
SSM_HPG = 8
SSM_STATE = 128
SSM_CONV = 4
SSM_XBC = 3072
D_FF = 2816
FFN_CONV = 3
N_CHIPS = 4
N_DEV = 8

ADAM_LR = 0.001
ADAM_B1 = 0.9
ADAM_B2 = 0.999
ADAM_EPS = 1e-08
ADAM_WD = 0.01
ADAM_STEP = 10

VMEM_LIMIT_V7X = 56 * 1024 * 1024
SUBLANES = 8
LANES = 128


def _params(sem=None):
    return pltpu.CompilerParams(dimension_semantics=sem, vmem_limit_bytes=VMEM_LIMIT_V7X)


def _dot(a, b, ca=1, cb=0):
    return lax.dot_general(a.astype(BF16), b.astype(BF16), (((ca,), (cb,)), ((), ())),
                           preferred_element_type=F32)


def _mm(name, a, b, *, ta=False, tb=False, out_dtype=F32, bm, bn, bk, res=None):
    m, k = (a.shape[1], a.shape[0]) if ta else a.shape
    k2, n = (b.shape[1], b.shape[0]) if tb else b.shape
    assert k == k2 and m % bm == 0 and n % bn == 0 and k % bk == 0, (name, a.shape, b.shape)
    nk = k // bk
    a_spec = (pl.BlockSpec((bk, bm), lambda i, j, kk: (kk, i)) if ta
              else pl.BlockSpec((bm, bk), lambda i, j, kk: (i, kk)))
    b_spec = (pl.BlockSpec((bn, bk), lambda i, j, kk: (j, kk)) if tb
              else pl.BlockSpec((bk, bn), lambda i, j, kk: (kk, j)))
    o_spec = pl.BlockSpec((bm, bn), lambda i, j, kk: (i, j))
    has_res = res is not None

    def body(*refs):
        a_ref, b_ref = refs[0], refs[1]
        r_ref = refs[2] if has_res else None
        o_ref = refs[3] if has_res else refs[2]
        p = _dot(a_ref[...], b_ref[...], 0 if ta else 1, 1 if tb else 0)

        def finish(total):
            if has_res:
                total = total + r_ref[...]
            o_ref[...] = total.astype(out_dtype)

        if nk == 1:
            finish(p)
        else:
            acc_ref = refs[-1]
            kk = pl.program_id(2)

            @pl.when(kk == 0)
            def _():
                acc_ref[...] = p

            @pl.when(kk > 0)
            def _():
                acc_ref[...] += p

            @pl.when(kk == nk - 1)
            def _():
                finish(acc_ref[...])

    return pl.pallas_call(
        body, name=name,
        grid=(m // bm, n // bn, nk),
        in_specs=[a_spec, b_spec] + ([o_spec] if has_res else []),
        out_specs=o_spec,
        out_shape=jax.ShapeDtypeStruct((m, n), out_dtype),
        scratch_shapes=[pltpu.VMEM((bm, bn), F32)] if nk > 1 else [],
        compiler_params=_params(("parallel", "parallel", "arbitrary")),
    )(*([a, b] + ([res] if has_res else [])))


def _mm_sum(name, pairs, *, bm, bk, exchange=()):
    nx = len(exchange)
    npair = len(pairs)
    m, n = pairs[0][0].shape[0], pairs[0][1].shape[0]
    steps, first = [], []
    for a, b in pairs:
        k = a.shape[1]
        assert a.shape[0] == m and b.shape == (n, k) and m % bm == 0 and (k % bk == 0 or k < bk), (name, a.shape, b.shape)
        first.append(sum(steps))
        steps.append(max(k // bk, 1))
    total = sum(steps)
    in_specs = []
    for (a, b), off, cnt in zip(pairs, first, steps):
        width = min(bk, a.shape[1])
        in_specs.append(pl.BlockSpec((bm, width), lambda i, kk, off=off, cnt=cnt: (i, jnp.clip(kk - off, 0, cnt - 1))))
        in_specs.append(pl.BlockSpec((n, width), lambda i, kk, off=off, cnt=cnt: (0, jnp.clip(kk - off, 0, cnt - 1))))

    def body(*refs):
        send_refs = refs[2 * npair:2 * npair + nx]
        o_ref = refs[2 * npair + nx]
        recv_refs = refs[2 * npair + nx + 1:2 * npair + 2 * nx + 1]
        acc_ref = refs[2 * npair + 2 * nx + 1]
        i, kk = pl.program_id(0), pl.program_id(1)
        if nx:
            start, finish = _scatter_phases(send_refs, recv_refs, *refs[2 * npair + 2 * nx + 2:])

            @pl.when((i == 0) & (kk == 0))
            def _():
                start()

        for s, (off, cnt) in enumerate(zip(first, steps)):
            @pl.when((kk >= off) & (kk < off + cnt))
            def _(s=s, off=off):
                p = _dot(refs[2 * s][...], refs[2 * s + 1][...], 1, 1)
                if off == 0:
                    @pl.when(kk == 0)
                    def _():
                        acc_ref[...] = p

                    @pl.when(kk > 0)
                    def _():
                        acc_ref[...] += p
                else:
                    acc_ref[...] += p

        @pl.when(kk == total - 1)
        def _():
            o_ref[...] = acc_ref[...]

        if nx:
            @pl.when((i == m // bm - 1) & (kk == total - 1))
            def _():
                finish()

    hbm = pl.BlockSpec(memory_space=pl.ANY)
    res = pl.pallas_call(
        body, name=name, grid=(m // bm, total),
        in_specs=in_specs + [hbm] * nx, out_specs=[pl.BlockSpec((bm, n), lambda i, kk: (i, 0))] + [hbm] * nx,
        out_shape=[jax.ShapeDtypeStruct((m, n), F32)] + [jax.ShapeDtypeStruct(e.shape, e.dtype) for e in exchange],
        scratch_shapes=[pltpu.VMEM((bm, n), F32)] + (_exchange_scratch(nx, N_CHIPS - 1) if nx else []),
        compiler_params=_params(("arbitrary", "arbitrary")),
    )(*[t for pair in pairs for t in pair], *exchange)
    return (res[0], res[1:]) if nx else res[0]


def _rows(name, fn, ins, params, outs, accs, *, tm, rs, unroll=4):
    nrow = ins[0][0].shape[-2]
    while tm % (rs * unroll):
        unroll //= 2
    assert nrow % tm == 0 and tm % rs == 0, (name, nrow, tm, rs)
    n_in, n_p, n_out, n_acc = len(ins), len(params), len(outs), len(accs)
    in_specs = []
    for spec in ins:
        arr, width, cb = spec[:3]
        if len(spec) == 4:
            in_specs.append(pl.BlockSpec((None, tm, width), lambda i, cb=cb, lead=spec[3]: (lead, i, cb)))
        else:
            in_specs.append(pl.BlockSpec((tm, width), lambda i, cb=cb: (i, cb)))
    for p in params:
        in_specs.append(pl.BlockSpec(p.shape, lambda i, nd=p.ndim: (0,) * nd))
    out_specs = [pl.BlockSpec((tm, w), lambda i: (i, 0)) for w, _ in outs]
    out_specs += [pl.BlockSpec(s, lambda i: (0, 0)) for s in accs]
    out_shape = [jax.ShapeDtypeStruct((nrow, w), dt) for w, dt in outs]
    out_shape += [jax.ShapeDtypeStruct(s, F32) for s in accs]

    def body(*refs):
        in_refs = refs[:n_in]
        p_refs = refs[n_in:n_in + n_p]
        o_refs = refs[n_in + n_p:n_in + n_p + n_out]
        a_refs = refs[n_in + n_p + n_out:]
        pv = [p[...] for p in p_refs]

        if n_acc:
            @pl.when(pl.program_id(0) == 0)
            def _():
                for a_ref in a_refs:
                    a_ref[...] = jnp.zeros_like(a_ref)

        def step(r, carry):
            for u in range(unroll):
                sl = pl.ds(pl.multiple_of((r * unroll + u) * rs, rs), rs)
                vals = [ref[sl, :].astype(F32) for ref in in_refs]
                row_out, sums = fn(*vals, *pv)
                for o_ref, v in zip(o_refs, row_out):
                    o_ref[sl, :] = v.astype(o_ref.dtype)
                carry = tuple(c + s for c, s in zip(carry, sums))
            return carry

        init = tuple(jnp.zeros(s, F32) for s in accs)
        total = lax.fori_loop(0, tm // (rs * unroll), step, init)
        for a_ref, t in zip(a_refs, total):
            a_ref[...] += t

    res = pl.pallas_call(
        body, name=name, grid=(nrow // tm,),
        in_specs=in_specs, out_specs=out_specs, out_shape=out_shape,
        compiler_params=_params(("arbitrary",)),
    )(*([s[0] for s in ins] + list(params)))
    return res


def _rms(x, w):
    return x * lax.rsqrt(jnp.mean(x * x, axis=-1, keepdims=True) + EPS) * w


def _colsum(v):
    return jnp.sum(v, axis=0, keepdims=True)


def _rms_fwd(name, x, w):
    def fn(xv, wv):
        return (_rms(xv, wv),), ()
    return _rows(name, fn, [(x, D_MODEL, 0)], [w], [(D_MODEL, BF16)], [], tm=1024, rs=16)[0]


def _rms_bwd(name, x, w, dy, dres):
    def fn(xv, dyv, drv, wv):
        _, vjp = jax.vjp(_rms, xv, wv)
        dx, dw = vjp(dyv)
        return (drv + dx,), (dw,)
    return _rows(name, fn, [(x, D_MODEL, 0), (dy, D_MODEL, 0), (dres, D_MODEL, 0)], [w],
                 [(D_MODEL, F32)], [(1, D_MODEL)], tm=1024, rs=16)


def _final_loss(name, h, target, w):
    def fn(hv, tv, wv):
        y, vjp = jax.vjp(_rms, hv, wv)
        err = y - tv
        part = 0.5 * jnp.sum(jnp.mean(err * err, axis=-1, keepdims=True), axis=0, keepdims=True)
        dh, dw = vjp(err / D_MODEL)
        return (dh,), (jnp.broadcast_to(part, (1, LANES)), dw)
    return _rows(name, fn, [(h, D_MODEL, 0), (target, D_MODEL, 0)], [w],
                 [(D_MODEL, F32)], [(1, LANES), (1, D_MODEL)], tm=1024, rs=16)


def _merge(ga, gb, ya, yb, b0, b1):
    return jax.nn.sigmoid(ga + b0) * ya + jax.nn.sigmoid(gb + b1) * yb


def _merge_fwd(name, g, ya, yb, b0, b1):
    def fn(ga, gb, yav, ybv, b0v, b1v):
        return (_merge(ga, gb, yav, ybv, b0v, b1v),), ()
    return _rows(name, fn, [(g, D_MODEL, 0), (g, D_MODEL, 1), (ya, D_MODEL, 0), (yb, D_MODEL, 0)],
                 [b0, b1], [(D_MODEL, BF16)], [], tm=1024, rs=16)[0]


def _merge_bwd(name, g, ya, yb, dm, b0, b1):
    def fn(ga, gb, yav, ybv, dmv, b0v, b1v):
        _, vjp = jax.vjp(_merge, ga, gb, yav, ybv, b0v, b1v)
        dga, dgb, dya, dyb, db0, db1 = vjp(dmv)
        return (jnp.concatenate([dga, dgb], axis=1), dya, dyb), (db0, db1)
    return _rows(name, fn,
                 [(g, D_MODEL, 0), (g, D_MODEL, 1), (ya, D_MODEL, 0), (yb, D_MODEL, 0), (dm, D_MODEL, 0)],
                 [b0, b1], [(2 * D_MODEL, BF16), (D_MODEL, BF16), (D_MODEL, BF16)],
                 [(1, D_MODEL), (1, D_MODEL)], tm=1024, rs=16)


GROUP_W = SSM_INNER // SSM_GROUPS


def _gate_norm_group(y, z, nw):
    v = y * jax.nn.silu(z)
    return v * lax.rsqrt(jnp.mean(v * v, axis=-1, keepdims=True) + EPS) * nw


def _gate_norm_fwd(name, y, z, nw):
    def fn(yv, zv, nwv):
        parts = [_gate_norm_group(yv[:, k * GROUP_W:(k + 1) * GROUP_W], zv[:, k * GROUP_W:(k + 1) * GROUP_W],
                                  nwv[:, k * GROUP_W:(k + 1) * GROUP_W]) for k in range(SSM_GROUPS)]
        return (jnp.concatenate(parts, axis=1),), ()
    return _rows(name, fn, [(y, SSM_INNER, 0), (z, SSM_INNER, 0)], [nw], [(SSM_INNER, BF16)], [],
                 tm=1024, rs=16)[0]


def _gate_norm_bwd(name, y, z, dout, nw):
    def fn(yv, zv, dv, nwv):
        dys, dzs, dns = [], [], []
        for k in range(SSM_GROUPS):
            sl = slice(k * GROUP_W, (k + 1) * GROUP_W)
            _, vjp = jax.vjp(_gate_norm_group, yv[:, sl], zv[:, sl], nwv[:, sl])
            dy, dz, dn = vjp(dv[:, sl])
            dys.append(dy), dzs.append(dz), dns.append(dn)
        return (jnp.concatenate(dys, axis=1), jnp.concatenate(dzs, axis=1)), (jnp.concatenate(dns, axis=1),)
    return _rows(name, fn, [(y, SSM_INNER, 0), (z, SSM_INNER, 0), (dout, SSM_INNER, 0)], [nw],
                 [(SSM_INNER, BF16), (SSM_INNER, BF16)], [(1, SSM_INNER)], tm=1024, rs=16)


def _softplus(v):
    return jnp.maximum(v, 0.0) + jnp.log1p(jnp.exp(-jnp.abs(v)))


def _chunk_cumsum(v, reverse=False):
    row = lax.broadcasted_iota(jnp.int32, v.shape, 0)
    step = 1
    while step < CHUNK:
        if reverse:
            shifted = pltpu.roll(v, CHUNK - step, axis=0)
            v = v + jnp.where(row < CHUNK - step, shifted, 0.0)
        else:
            shifted = pltpu.roll(v, step, axis=0)
            v = v + jnp.where(row >= step, shifted, 0.0)
        step *= 2
    return v


def _dt_prep(name, dt_raw, dt_bias, a_log):
    def fn(rv, bv, alv):
        dt = _softplus(rv + bv)
        return (dt, _chunk_cumsum(dt * (-jnp.exp(alv)))), ()
    return _rows(name, fn, [(dt_raw, SSM_HEADS, 0)], [dt_bias, a_log],
                 [(SSM_HEADS, F32), (SSM_HEADS, F32)], [], tm=512, rs=CHUNK)


def _dt_bwd(name, dt_raw, ddt, da1, da2, dt_bias, a_log):
    def fn(rv, ddv, d1, d2, bv, alv):
        pre = rv + bv
        dt = _softplus(pre)
        a_neg = -jnp.exp(alv)
        back = _chunk_cumsum(d1 + d2, reverse=True)
        d_dt = ddv + back * a_neg
        d_raw = d_dt * jax.nn.sigmoid(pre)
        return (d_raw,), (_colsum(d_raw), _colsum(back * dt) * a_neg)
    return _rows(name, fn, [(dt_raw, SSM_HEADS, 0), (ddt, SSM_HEADS, 0), (da1, SSM_HEADS, 0), (da2, SSM_HEADS, 0)],
                 [dt_bias, a_log], [(SSM_HEADS, BF16)], [(1, SSM_HEADS), (1, SSM_HEADS)], tm=512, rs=CHUNK)


def _adamw_math(w, g, m, v):
    m_new = ADAM_B1 * m + (1.0 - ADAM_B1) * g
    v_new = ADAM_B2 * v + (1.0 - ADAM_B2) * jnp.square(g)
    m_hat = m_new / (1.0 - ADAM_B1 ** ADAM_STEP)
    v_hat = v_new / (1.0 - ADAM_B2 ** ADAM_STEP)
    delta = -ADAM_LR * (m_hat / (jnp.sqrt(v_hat) + ADAM_EPS) + ADAM_WD * w)
    return delta, m_new, v_new


def _adamw(name, w, g, m, v, *, tm, rs):
    width = w.shape[1]

    def fn(wv, mv, vv, gv):
        return (gv,) + _adamw_math(wv, gv, mv, vv), ()
    return _rows(name, fn, [(w, width, 0), (m, width, 0), (v, width, 0), (g, width, 0)],
                 [], [(width, F32)] * 4, [], tm=tm, rs=rs)


def _adamw_small(name, ws, gs, ms, vs):
    n = len(ws)

    def body(*refs):
        w_refs, g_refs, m_refs, v_refs = (refs[k * n:(k + 1) * n] for k in range(4))
        outs = refs[4 * n:]
        for i in range(n):
            res = _adamw_math(w_refs[i][...], g_refs[i][...], m_refs[i][...], v_refs[i][...])
            for k in range(3):
                outs[k * n + i][...] = res[k]

    vmem = pl.BlockSpec(memory_space=pltpu.VMEM)
    res = pl.pallas_call(
        body, name=name, in_specs=[vmem] * (4 * n), out_specs=[vmem] * (3 * n),
        out_shape=[jax.ShapeDtypeStruct(w.shape, F32) for w in ws] * 3,
        compiler_params=pltpu.CompilerParams(vmem_limit_bytes=VMEM_LIMIT_V7X),
    )(*ws, *gs, *ms, *vs)
    return res[:n], res[n:2 * n], res[2 * n:]


def _pair_sum(name, a, b, *, tm):
    shape = a.shape
    flat = (shape[0] * shape[1], shape[2])

    def fn(av, bv):
        return (av.astype(F32) + bv.astype(F32),), ()
    out = _rows(name, fn, [(a.reshape(flat), flat[1], 0), (b.reshape(flat), flat[1], 0)], [], [(flat[1], BF16)], [],
                tm=tm, rs=2 * SUBLANES)[0]
    return out.reshape(shape)


def _sum_slots(name, stack, *, tm, rs):
    width = stack.shape[2]

    def fn(*slots):
        s0, s1, s2, s3 = (s.astype(F32) for s in slots)
        return (((s0 + s1) + s2) + s3,), ()
    return _rows(name, fn, [(stack, width, 0, k) for k in range(N_CHIPS)], [], [(width, F32)], [],
                 tm=tm, rs=rs)[0]


def _layernorm(v, w, b):
    mu = jnp.mean(v, axis=-1, keepdims=True)
    var = jnp.mean(jnp.square(v - mu), axis=-1, keepdims=True)
    return (v - mu) * lax.rsqrt(var + EPS) * w + b


GELU_C = 0.7978845608028654
GELU_A = 0.044715


def _gelu_and_slope(x):
    x2 = x * x
    t = jnp.tanh(GELU_C * x * (1.0 + GELU_A * x2))
    half = 0.5 * (1.0 + t)
    slope = half + 0.5 * x * (1.0 - t * t) * (GELU_C * (1.0 + 3.0 * GELU_A * x2))
    return x * half, slope


def _layernorm_and_back(v, w, b):
    mu = jnp.mean(v, axis=-1, keepdims=True)
    cen = v - mu
    rstd = lax.rsqrt(jnp.mean(cen * cen, axis=-1, keepdims=True) + EPS)
    vhat = cen * rstd

    def back(dout):
        dhat = dout * w
        dv = rstd * (dhat - jnp.mean(dhat, axis=-1, keepdims=True)
                     - vhat * jnp.mean(dhat * vhat, axis=-1, keepdims=True))
        return dv, _colsum(dout * vhat), _colsum(dout)

    return vhat * w + b, back


def _gmlp_mask():
    t = lax.broadcasted_iota(jnp.int32, (GMLP_BLOCK, GMLP_BLOCK), 0) // CHUNK
    s = lax.broadcasted_iota(jnp.int32, (GMLP_BLOCK, GMLP_BLOCK), 1) // CHUNK
    return s <= t


GMLP_TM = 1024


def _gmlp_fwd(name, za, ln_w, ln_b, ws, bs_col):
    nrow = za.shape[0]
    tm = GMLP_TM
    width = GMLP_GROUPS * GMLP_BLOCK

    def body(za_ref, lnw_ref, lnb_ref, ws_ref, bs_ref, o_ref, wm_ref):
        mask = _gmlp_mask()
        for g in range(GMLP_GROUPS):
            wm_ref[g] = jnp.where(mask, ws_ref[g], 0.0).astype(BF16)

        def block(n, carry):
            rows = pl.ds(pl.multiple_of(n * GMLP_BLOCK, GMLP_BLOCK), GMLP_BLOCK)
            for g in range(GMLP_GROUPS):
                cols = slice(g * GMLP_BLOCK, (g + 1) * GMLP_BLOCK)
                vcols = slice(width + g * GMLP_BLOCK, width + (g + 1) * GMLP_BLOCK)
                u = jax.nn.gelu(za_ref[rows, cols].astype(F32))
                v = jax.nn.gelu(za_ref[rows, vcols].astype(F32))
                vn = _layernorm(v, lnw_ref[g:g + 1, :], lnb_ref[g:g + 1, :])
                sv = _dot(wm_ref[g], vn) + bs_ref[g]
                o_ref[rows, cols] = (u * sv).astype(o_ref.dtype)
            return carry

        lax.fori_loop(0, tm // GMLP_BLOCK, block, 0)

    small = lambda a: pl.BlockSpec(a.shape, lambda i, nd=a.ndim: (0,) * nd)
    return pl.pallas_call(
        body, name=name, grid=(nrow // tm,),
        in_specs=[pl.BlockSpec((tm, 2 * width), lambda i: (i, 0)), small(ln_w), small(ln_b), small(ws), small(bs_col)],
        out_specs=pl.BlockSpec((tm, width), lambda i: (i, 0)),
        out_shape=jax.ShapeDtypeStruct((nrow, width), BF16),
        scratch_shapes=[pltpu.VMEM((GMLP_GROUPS, GMLP_BLOCK, GMLP_BLOCK), BF16)],
        compiler_params=_params(("arbitrary",)),
    )(za, ln_w, ln_b, ws, bs_col)


def _gmlp_bwd(name, za, dout, ln_w, ln_b, ws, bs_col):
    nrow = za.shape[0]
    tm = GMLP_TM
    width = GMLP_GROUPS * GMLP_BLOCK

    def body(za_ref, do_ref, lnw_ref, lnb_ref, ws_ref, bs_ref, dza_ref, dlnw_ref, dlnb_ref, dws_ref, dbs_ref, wm_ref):
        mask = _gmlp_mask()
        for g in range(GMLP_GROUPS):
            wm_ref[g] = jnp.where(mask, ws_ref[g], 0.0).astype(BF16)

        @pl.when(pl.program_id(0) == 0)
        def _():
            dlnw_ref[...] = jnp.zeros_like(dlnw_ref)
            dlnb_ref[...] = jnp.zeros_like(dlnb_ref)
            dws_ref[...] = jnp.zeros_like(dws_ref)
            dbs_ref[...] = jnp.zeros_like(dbs_ref)

        def block(n, carry):
            rows = pl.ds(pl.multiple_of(n * GMLP_BLOCK, GMLP_BLOCK), GMLP_BLOCK)
            for g in range(GMLP_GROUPS):
                cols = slice(g * GMLP_BLOCK, (g + 1) * GMLP_BLOCK)
                vcols = slice(width + g * GMLP_BLOCK, width + (g + 1) * GMLP_BLOCK)
                u, slope_u = _gelu_and_slope(za_ref[rows, cols].astype(F32))
                v, slope_v = _gelu_and_slope(za_ref[rows, vcols].astype(F32))
                vn, ln_back = _layernorm_and_back(v, lnw_ref[g:g + 1, :], lnb_ref[g:g + 1, :])
                sv = _dot(wm_ref[g], vn) + bs_ref[g]
                d_o = do_ref[rows, cols].astype(F32)
                dsv = d_o * u
                d_wm = _dot(dsv, vn, 1, 1)
                dvn = _dot(wm_ref[g], dsv, 0, 0)
                dv, dlnw, dlnb = ln_back(dvn)
                dza_ref[rows, cols] = (d_o * sv * slope_u).astype(dza_ref.dtype)
                dza_ref[rows, vcols] = (dv * slope_v).astype(dza_ref.dtype)
                dlnw_ref[g:g + 1, :] += dlnw
                dlnb_ref[g:g + 1, :] += dlnb
                dws_ref[g] += jnp.where(mask, d_wm, 0.0)
                dbs_ref[g] += jnp.sum(dsv, axis=1, keepdims=True)
            return carry

        lax.fori_loop(0, tm // GMLP_BLOCK, block, 0)

    small = lambda a: pl.BlockSpec(a.shape, lambda i, nd=a.ndim: (0,) * nd)
    return pl.pallas_call(
        body, name=name, grid=(nrow // tm,),
        in_specs=[pl.BlockSpec((tm, 2 * width), lambda i: (i, 0)), pl.BlockSpec((tm, width), lambda i: (i, 0)),
                  small(ln_w), small(ln_b), small(ws), small(bs_col)],
        out_specs=[pl.BlockSpec((tm, 2 * width), lambda i: (i, 0)), small(ln_w), small(ln_b), small(ws), small(bs_col)],
        out_shape=[jax.ShapeDtypeStruct((nrow, 2 * width), BF16), jax.ShapeDtypeStruct(ln_w.shape, F32),
                   jax.ShapeDtypeStruct(ln_b.shape, F32), jax.ShapeDtypeStruct(ws.shape, F32),
                   jax.ShapeDtypeStruct(bs_col.shape, F32)],
        scratch_shapes=[pltpu.VMEM((GMLP_GROUPS, GMLP_BLOCK, GMLP_BLOCK), BF16)],
        compiler_params=_params(("arbitrary",)),
    )(za, dout, ln_w, ln_b, ws, bs_col)


CONV_TM = 512
CONV_RS = 32
HALO = 2 * SUBLANES


def _tap_rows(w_ref):
    return [w_ref[k:k + 1, :] for k in range(w_ref.shape[0])]


def _halo_specs(nrow, tm, tc):
    per = tm // HALO
    last = nrow // HALO - 1
    main = pl.BlockSpec((tm, tc), lambda j, i: (i, j))
    before = pl.BlockSpec((HALO, tc), lambda j, i: (jnp.maximum(i * per - 1, 0), j))
    after = pl.BlockSpec((HALO, tc), lambda j, i: (jnp.minimum((i + 1) * per, last), j))
    return main, before, after


def _col_spec(rows, tc):
    return pl.BlockSpec((rows, tc), lambda j, i: (0, j))


def _conv_fwd(name, x, w, b, *, tc):
    nrow, ncol = x.shape
    taps = w.shape[0]
    tm, rs = CONV_TM, CONV_RS
    main, before, _ = _halo_specs(nrow, tm, tc)

    def body(x_ref, xb_ref, w_ref, b_ref, o_ref, xw_ref):
        first = pl.program_id(1) == 0
        wv, bv = _tap_rows(w_ref), b_ref[...]
        xw_ref[0:HALO, :] = jnp.where(first, 0.0, xb_ref[...].astype(F32))
        for r in range(tm // rs):
            xw_ref[HALO + r * rs:HALO + (r + 1) * rs, :] = x_ref[r * rs:(r + 1) * rs, :].astype(F32)
        for r in range(tm // rs):
            base = HALO + r * rs
            out = bv + wv[taps - 1] * xw_ref[base:base + rs, :]
            for k in range(taps - 1):
                back = taps - 1 - k
                out = out + wv[k] * xw_ref[base - back:base - back + rs, :]
            o_ref[r * rs:(r + 1) * rs, :] = out.astype(o_ref.dtype)

    return pl.pallas_call(
        body, name=name, grid=(ncol // tc, nrow // tm),
        in_specs=[main, before, _col_spec(taps, tc), _col_spec(1, tc)],
        out_specs=main, out_shape=jax.ShapeDtypeStruct((nrow, ncol), BF16),
        scratch_shapes=[pltpu.VMEM((HALO + tm, tc), F32)],
        compiler_params=_params(("parallel", "arbitrary")),
    )(x, x, w, b)


def _conv_bwd(name, dpre, x, w, *, tc):
    nrow, ncol = x.shape
    taps = w.shape[0]
    tm, rs = CONV_TM, CONV_RS
    nsub = tm // rs
    main, before, after = _halo_specs(nrow, tm, tc)

    def fold(v):
        total = v[0:SUBLANES]
        for q in range(1, rs // SUBLANES):
            total = total + v[q * SUBLANES:(q + 1) * SUBLANES]
        return total

    def body(d_ref, da_ref, x_ref, xb_ref, w_ref, dx_ref, dw_ref, db_ref, dwin_ref, xwin_ref):
        i = pl.program_id(1)
        first, last = i == 0, i == pl.num_programs(1) - 1
        wv = _tap_rows(w_ref)

        @pl.when(first)
        def _():
            dw_ref[...] = jnp.zeros_like(dw_ref)
            db_ref[...] = jnp.zeros_like(db_ref)

        xwin_ref[0:HALO, :] = jnp.where(first, 0.0, xb_ref[...].astype(F32))
        dwin_ref[tm:, :] = jnp.where(last, 0.0, da_ref[...].astype(F32))
        for r in range(nsub):
            dwin_ref[r * rs:(r + 1) * rs, :] = d_ref[r * rs:(r + 1) * rs, :].astype(F32)
            xwin_ref[HALO + r * rs:HALO + (r + 1) * rs, :] = x_ref[r * rs:(r + 1) * rs, :].astype(F32)
        dw = [jnp.zeros((SUBLANES, tc), F32)] * taps
        db = jnp.zeros((SUBLANES, tc), F32)
        for r in range(nsub):
            cur = dwin_ref[r * rs:(r + 1) * rs, :]
            dx = wv[taps - 1] * cur
            for k in range(taps - 1):
                ahead = taps - 1 - k
                dx = dx + wv[k] * dwin_ref[r * rs + ahead:(r + 1) * rs + ahead, :]
            dx_ref[r * rs:(r + 1) * rs, :] = dx.astype(dx_ref.dtype)
            for k in range(taps):
                back = taps - 1 - k
                dw[k] = dw[k] + fold(cur * xwin_ref[HALO + r * rs - back:HALO + (r + 1) * rs - back, :])
            db = db + fold(cur)
        for k in range(taps):
            dw_ref[k:k + 1, :] += _colsum(dw[k])
        db_ref[...] += _colsum(db)

    return pl.pallas_call(
        body, name=name, grid=(ncol // tc, nrow // tm),
        in_specs=[main, after, main, before, _col_spec(taps, tc)],
        out_specs=[main, _col_spec(taps, tc), _col_spec(1, tc)],
        out_shape=[jax.ShapeDtypeStruct((nrow, ncol), BF16), jax.ShapeDtypeStruct((taps, ncol), F32),
                   jax.ShapeDtypeStruct((1, ncol), F32)],
        scratch_shapes=[pltpu.VMEM((tm + HALO, tc), F32), pltpu.VMEM((HALO + tm, tc), F32)],
        compiler_params=_params(("parallel", "arbitrary")),
    )(dpre, dpre, x, x, w)


def _glu(gate, val):
    return jax.nn.silu(gate) * val


def _ffn_act_fwd(name, pg, pv, wg, wv, bg, bv, *, tc):
    nrow, ncol = pg.shape
    taps = wg.shape[0]
    tm, rs = CONV_TM, CONV_RS
    main, before, _ = _halo_specs(nrow, tm, tc)

    def body(pg_ref, pgb_ref, pv_ref, pvb_ref, wg_ref, wv_ref, bg_ref, bv_ref, g_ref, v_ref, a_ref, gwin_ref, vwin_ref):
        first = pl.program_id(1) == 0
        taps_g, taps_v, bgv, bvv = _tap_rows(wg_ref), _tap_rows(wv_ref), bg_ref[...], bv_ref[...]
        gwin_ref[0:HALO, :] = jnp.where(first, 0.0, pgb_ref[...].astype(F32))
        vwin_ref[0:HALO, :] = jnp.where(first, 0.0, pvb_ref[...].astype(F32))
        for r in range(tm // rs):
            gwin_ref[HALO + r * rs:HALO + (r + 1) * rs, :] = pg_ref[r * rs:(r + 1) * rs, :].astype(F32)
            vwin_ref[HALO + r * rs:HALO + (r + 1) * rs, :] = pv_ref[r * rs:(r + 1) * rs, :].astype(F32)

        def conv(win_ref, tap_rows, bias, r):
            base = HALO + r * rs
            out = bias + tap_rows[taps - 1] * win_ref[base:base + rs, :]
            for k in range(taps - 1):
                back = taps - 1 - k
                out = out + tap_rows[k] * win_ref[base - back:base - back + rs, :]
            return out

        for r in range(tm // rs):
            sl = slice(r * rs, (r + 1) * rs)
            gate, val = conv(gwin_ref, taps_g, bgv, r), conv(vwin_ref, taps_v, bvv, r)
            g_ref[sl, :] = gate.astype(g_ref.dtype)
            v_ref[sl, :] = val.astype(v_ref.dtype)
            a_ref[sl, :] = _glu(gate, val).astype(a_ref.dtype)

    return pl.pallas_call(
        body, name=name, grid=(ncol // tc, nrow // tm),
        in_specs=[main, before, main, before, _col_spec(taps, tc), _col_spec(taps, tc), _col_spec(1, tc), _col_spec(1, tc)],
        out_specs=[main, main, main],
        out_shape=[jax.ShapeDtypeStruct((nrow, ncol), BF16)] * 3,
        scratch_shapes=[pltpu.VMEM((HALO + tm, tc), F32), pltpu.VMEM((HALO + tm, tc), F32)],
        compiler_params=_params(("parallel", "arbitrary")),
    )(pg, pg, pv, pv, wg, wv, bg, bv)


def _ffn_act_bwd(name, dact, gate, val):
    def fn(dv, gv, vv):
        _, vjp = jax.vjp(_glu, gv, vv)
        dg, dval = vjp(dv)
        return (dg, dval), ()
    width = dact.shape[1]
    return _rows(name, fn, [(dact, width, 0), (gate, width, 0), (val, width, 0)], [],
                 [(width, BF16), (width, BF16)], [], tm=512, rs=2 * SUBLANES)


SSD_TM = 512
SSD_CHUNKS = SSD_TM // CHUNK
X_OFF, B_OFF, C_OFF = 0, SSM_INNER, SSM_INNER + SSM_GROUPS * SSM_STATE
HP = SSM_HPG * SSM_HEAD_DIM


def _causal_tiled():
    row = lax.broadcasted_iota(jnp.int32, (CHUNK, HP), 0)
    src = lax.broadcasted_iota(jnp.int32, (CHUNK, HP), 1) & (CHUNK - 1)
    return src <= row


def _split2(v):
    hi = v.astype(BF16)
    return hi, (v - hi.astype(F32)).astype(BF16)


def _dot_exact(a, ind):
    hi, lo = (lax.dot_general(p, ind, (((1,), (0,)), ((), ())), preferred_element_type=F32) for p in _split2(a))
    return hi + lo


def _head_indicator():
    head = lax.broadcasted_iota(jnp.int32, (SSM_HEADS, SSM_INNER), 0)
    chan = lax.broadcasted_iota(jnp.int32, (SSM_HEADS, SSM_INNER), 1)
    return (chan // SSM_HEAD_DIM == head).astype(BF16)


def _chunk_decays(ci, dt_ref, ac_ref, ind, ax_ref, dtx_ref, eax_ref, eex_ref, tail_ref):
    rows = pl.ds(pl.multiple_of(ci * CHUNK, CHUNK), CHUNK)
    ax_ref[...] = _dot_exact(ac_ref[rows, :], ind)
    dtx_ref[...] = _dot_exact(dt_ref[rows, :], ind)
    eax_ref[...] = jnp.exp(ax_ref[...])
    eex_ref[...] = jnp.exp(ax_ref[CHUNK - 1:CHUNK, :] - ax_ref[...])
    tail = pl.ds(pl.multiple_of(ci * CHUNK + CHUNK - SUBLANES, SUBLANES), SUBLANES)
    tail_ref[...] = jnp.exp(ac_ref[tail, :])


def _group_decay(ci, g, ax_ref, af_ref, xbc_ref, causal):
    gcols = slice(g * HP, (g + 1) * HP)
    bm = xbc_ref[:, B_OFF + g * SSM_STATE:B_OFF + (g + 1) * SSM_STATE]
    cm = xbc_ref[:, C_OFF + g * SSM_STATE:C_OFF + (g + 1) * SSM_STATE]
    cb_tiled = _dot(cm, jnp.concatenate([bm] * SSM_HPG, axis=0), 1, 1)
    seg = ax_ref[:, gcols] - af_ref[ci, :, gcols]
    decay = jnp.where(causal, jnp.exp(jnp.where(causal, seg, 0.0)), 0.0)
    return bm, cm, cb_tiled * decay, decay


def _ssd_fwd(name, pre, dt, a_cum, a_flat, d_x, ind, shards):
    nrow = pre.shape[0]
    tm = SSD_TM
    nstep = nrow // tm
    ng = len(shards)

    def body(pre_ref, dt_ref, ac_ref, af_ref, dx_ref, ind_ref, *rest):
        shard_refs, (y_ref, st_ref), stack_refs = rest[:ng], rest[ng:ng + 2], rest[ng + 2:2 * ng + 2]
        (h_ref, xbc_ref, ax_ref, dtx_ref, eax_ref, eex_ref, m_ref, xd_ref, yd_ref, tail_ref,
         send_sems, recv_sems) = rest[2 * ng + 2:]
        step = pl.program_id(0)
        start, forward, finish = _gather_phases([s.shape[0] for s in shards], ng, shard_refs, stack_refs,
                                                send_sems, recv_sems)

        @pl.when(step == 0)
        def _():
            h_ref[...] = jnp.zeros_like(h_ref)
            start()

        @pl.when(step == nstep // 2)
        def _():
            forward()

        causal = _causal_tiled()
        ind = ind_ref[...]

        def chunk(ci, carry):
            rows = pl.ds(pl.multiple_of(ci * CHUNK, CHUNK), CHUNK)
            xbc_ref[...] = jax.nn.silu(pre_ref[rows, :].astype(F32))
            _chunk_decays(ci, dt_ref, ac_ref, ind, ax_ref, dtx_ref, eax_ref, eex_ref, tail_ref)
            st_ref[ci] = h_ref[...].astype(st_ref.dtype)
            for g in range(SSM_GROUPS):
                gcols = slice(g * HP, (g + 1) * HP)
                bm, cm, m_all, _ = _group_decay(ci, g, ax_ref, af_ref, xbc_ref, causal)
                m_ref[...] = m_all
                x_g = xbc_ref[:, gcols]
                xd = x_g * dtx_ref[:, gcols]
                xd_ref[...] = xd
                h_g = h_ref[gcols, :]
                for hh in range(SSM_HPG):
                    lc = slice(hh * SSM_HEAD_DIM, (hh + 1) * SSM_HEAD_DIM)
                    yd_ref[:, lc] = _dot(m_ref[:, lc], xd_ref[:, lc])
                y_ref[rows, gcols] = (yd_ref[...] + _dot(cm, h_g, 1, 1) * eax_ref[:, gcols]
                                      + dx_ref[:, gcols] * x_g).astype(y_ref.dtype)
                new = _dot(xd * eex_ref[:, gcols], bm, 0, 0)
                for hh in range(SSM_HPG):
                    h = g * SSM_HPG + hh
                    hrows = slice(h * SSM_HEAD_DIM, (h + 1) * SSM_HEAD_DIM)
                    lrows = slice(hh * SSM_HEAD_DIM, (hh + 1) * SSM_HEAD_DIM)
                    h_ref[hrows, :] = tail_ref[SUBLANES - 1:SUBLANES, h:h + 1] * h_ref[hrows, :] + new[lrows, :]
            return carry

        lax.fori_loop(0, SSD_CHUNKS, chunk, 0)

        @pl.when(step == nstep - 1)
        def _():
            finish()

    nchunk = nrow // CHUNK
    whole = lambda a: pl.BlockSpec(a.shape, lambda i, nd=a.ndim: (0,) * nd)
    hbm = pl.BlockSpec(memory_space=pl.ANY)
    wide = lambda: pltpu.VMEM((CHUNK, SSM_INNER), F32)
    group = lambda: pltpu.VMEM((CHUNK, HP), F32)
    res = pl.pallas_call(
        body, name=name, grid=(nstep,),
        in_specs=[pl.BlockSpec((tm, SSM_XBC), lambda i: (i, 0)), pl.BlockSpec((tm, SSM_HEADS), lambda i: (i, 0)),
                  pl.BlockSpec((tm, SSM_HEADS), lambda i: (i, 0)),
                  pl.BlockSpec((SSD_CHUNKS, 1, SSM_INNER), lambda i: (i, 0, 0)), whole(d_x), whole(ind)] + [hbm] * ng,
        out_specs=[pl.BlockSpec((tm, SSM_INNER), lambda i: (i, 0)),
                   pl.BlockSpec((SSD_CHUNKS, SSM_INNER, SSM_STATE), lambda i: (i, 0, 0))] + [hbm] * ng,
        out_shape=[jax.ShapeDtypeStruct((nrow, SSM_INNER), BF16),
                   jax.ShapeDtypeStruct((nchunk, SSM_INNER, SSM_STATE), BF16)]
        + [jax.ShapeDtypeStruct((N_CHIPS,) + s.shape, s.dtype) for s in shards],
        scratch_shapes=[pltpu.VMEM((SSM_INNER, SSM_STATE), F32), pltpu.VMEM((CHUNK, SSM_XBC), F32),
                        wide(), wide(), wide(), wide(), group(), group(), group(),
                        pltpu.VMEM((SUBLANES, SSM_HEADS), F32)] + _exchange_scratch(ng, GATHER_SEMS),
        compiler_params=_params(("arbitrary",)),
    )(pre, dt, a_cum, a_flat, d_x, ind, *shards)
    return res[0], res[1], res[2:]


def _ssd_bwd(name, pre, dt, a_cum, a_flat, d_x, ind, ind_t, states, dy, pairs):
    nrow = pre.shape[0]
    tm = SSD_TM
    ntile = nrow // tm
    npair = len(pairs)

    def body(pre_ref, dt_ref, ac_ref, af_ref, dx_ref, ind_ref, indt_ref, st_ref, dy_ref, *rest):
        pair_refs = rest[:npair]
        dpre_ref, ddt_ref, da_ref, daf_ref, dd_ref = rest[npair:npair + 5]
        recv_refs = rest[npair + 5:2 * npair + 5]
        (dh_ref, xbc_ref, dxbc_ref, ax_ref, dtx_ref, eax_ref, eex_ref, red_ref,
         m_ref, l_ref, xd_ref, dm_ref, dxd_ref, fold_ref, hd_ref, tail_ref, send_sems, recv_sems) = rest[2 * npair + 5:]
        start, finish = _scatter_phases(pair_refs, recv_refs, send_sems, recv_sems)

        @pl.when(pl.program_id(0) == 0)
        def _():
            dh_ref[...] = jnp.zeros_like(dh_ref)
            dd_ref[...] = jnp.zeros_like(dd_ref)
            start()

        causal = _causal_tiled()
        ind, ind_t = ind_ref[...], indt_ref[...]
        is_last_row = lax.broadcasted_iota(jnp.int32, (CHUNK, 1), 0) == CHUNK - 1
        ones = jnp.ones((CHUNK, SSM_STATE), BF16)

        def chunk(k, ddx):
            ci = SSD_CHUNKS - 1 - k
            rows = pl.ds(pl.multiple_of(ci * CHUNK, CHUNK), CHUNK)
            pre_v = pre_ref[rows, :].astype(F32)
            xbc_ref[...] = jax.nn.silu(pre_v)
            _chunk_decays(ci, dt_ref, ac_ref, ind, ax_ref, dtx_ref, eax_ref, eex_ref, tail_ref)
            ddx_parts = []
            for g in range(SSM_GROUPS):
                gcols = slice(g * HP, (g + 1) * HP)
                bcols = slice(B_OFF + g * SSM_STATE, B_OFF + (g + 1) * SSM_STATE)
                ccols = slice(C_OFF + g * SSM_STATE, C_OFF + (g + 1) * SSM_STATE)
                bm, cm, m_all, decay = _group_decay(ci, g, ax_ref, af_ref, xbc_ref, causal)
                m_ref[...] = m_all
                l_ref[...] = decay
                x_g = xbc_ref[:, gcols]
                xd = x_g * dtx_ref[:, gcols]
                xd_ref[...] = xd
                h_g = st_ref[ci, gcols, :]
                dh_g = dh_ref[gcols, :]
                dy_g = dy_ref[rows, gcols]
                for hh in range(SSM_HPG):
                    h = g * SSM_HPG + hh
                    hcols = slice(h * SSM_HEAD_DIM, (h + 1) * SSM_HEAD_DIM)
                    lc = slice(hh * SSM_HEAD_DIM, (hh + 1) * SSM_HEAD_DIM)
                    dy_h = dy_ref[rows, hcols]
                    dm_ref[:, lc] = _dot(dy_h, xd_ref[:, lc], 1, 1)
                    dxd_ref[:, lc] = _dot(m_ref[:, lc], dy_h, 0, 0)
                ebdh = eex_ref[:, gcols] * _dot(bm, dh_g, 1, 1)
                dxd = dxd_ref[...] + ebdh
                dm = dm_ref[...]
                t = dm * l_ref[...]
                t128 = (t[:, 0:LANES] + t[:, LANES:2 * LANES]) + (t[:, 2 * LANES:3 * LANES] + t[:, 3 * LANES:])
                fold_ref[...] = t128 + pltpu.roll(t128, CHUNK, axis=1)
                dw_sum = fold_ref[:, 0:CHUNK]
                q = dm * m_ref[...]
                dyea = dy_g * eax_ref[:, gcols]
                red_ref[0:CHUNK, gcols] = q + dyea * _dot(cm, h_g, 1, 1)
                red_ref[CHUNK:2 * CHUNK, gcols] = xd * ebdh
                red_ref[2 * CHUNK:3 * CHUNK, gcols] = dxd * x_g
                daf_ref[ci, :, gcols] = -jnp.sum(q, axis=0, keepdims=True)
                ddx_parts.append(jnp.sum(dy_g * x_g, axis=0, keepdims=True))
                dxbc_ref[:, gcols] = dxd * dtx_ref[:, gcols] + dx_ref[:, gcols] * dy_g
                dxbc_ref[:, ccols] = _dot(dw_sum, bm) + _dot(dyea, h_g)
                dxbc_ref[:, bcols] = _dot(dw_sum, cm, 0, 0) + _dot(xd * eex_ref[:, gcols], dh_g)
                dh_new = _dot(dyea, cm, 0, 0)
                for hh in range(SSM_HPG):
                    h = g * SSM_HPG + hh
                    hrows = slice(h * SSM_HEAD_DIM, (h + 1) * SSM_HEAD_DIM)
                    lrows = slice(hh * SSM_HEAD_DIM, (hh + 1) * SSM_HEAD_DIM)
                    hd_ref[h:h + 1, :] = jnp.sum(st_ref[ci, hrows, :] * dh_ref[hrows, :], axis=0, keepdims=True)
                    dh_ref[hrows, :] = tail_ref[SUBLANES - 1:SUBLANES, h:h + 1] * dh_ref[hrows, :] + dh_new[lrows, :]
            sums = _dot_exact(red_ref[...], ind_t)
            ra, ts = sums[:CHUNK], sums[CHUNK:2 * CHUNK]
            hdh = sum(lax.dot_general(ones, p, (((1,), (1,)), ((), ())), preferred_element_type=F32)
                      for p in _split2(hd_ref[...]))
            da_last = jnp.sum(ts, axis=0, keepdims=True) + tail_ref[SUBLANES - 1:SUBLANES, :] * hdh
            da_ref[rows, :] = ra - ts + jnp.where(is_last_row, da_last, 0.0)
            ddt_ref[rows, :] = sums[2 * CHUNK:]
            sig = jax.nn.sigmoid(pre_v)
            dpre_ref[rows, :] = (dxbc_ref[...] * (sig * (1.0 + pre_v * (1.0 - sig)))).astype(dpre_ref.dtype)
            return ddx + jnp.concatenate(ddx_parts, axis=1)

        ddx = lax.fori_loop(0, SSD_CHUNKS, chunk, jnp.zeros((1, SSM_INNER), F32))
        dd_ref[...] += _dot_exact(jnp.broadcast_to(ddx, (SUBLANES, SSM_INNER)), ind_t)

        @pl.when(pl.program_id(0) == ntile - 1)
        def _():
            finish()

    rev = lambda i: ntile - 1 - i
    whole = lambda a: pl.BlockSpec(a.shape, lambda i, nd=a.ndim: (0,) * nd)
    hbm = pl.BlockSpec(memory_space=pl.ANY)
    wide = lambda: pltpu.VMEM((CHUNK, SSM_INNER), F32)
    group = lambda: pltpu.VMEM((CHUNK, HP), F32)
    res = pl.pallas_call(
        body, name=name, grid=(ntile,),
        in_specs=[pl.BlockSpec((tm, SSM_XBC), lambda i: (rev(i), 0)), pl.BlockSpec((tm, SSM_HEADS), lambda i: (rev(i), 0)),
                  pl.BlockSpec((tm, SSM_HEADS), lambda i: (rev(i), 0)),
                  pl.BlockSpec((SSD_CHUNKS, 1, SSM_INNER), lambda i: (rev(i), 0, 0)),
                  whole(d_x), whole(ind), whole(ind_t),
                  pl.BlockSpec((SSD_CHUNKS, SSM_INNER, SSM_STATE), lambda i: (rev(i), 0, 0)),
                  pl.BlockSpec((tm, SSM_INNER), lambda i: (rev(i), 0))] + [hbm] * npair,
        out_specs=[pl.BlockSpec((tm, SSM_XBC), lambda i: (rev(i), 0)), pl.BlockSpec((tm, SSM_HEADS), lambda i: (rev(i), 0)),
                   pl.BlockSpec((tm, SSM_HEADS), lambda i: (rev(i), 0)),
                   pl.BlockSpec((SSD_CHUNKS, 1, SSM_INNER), lambda i: (rev(i), 0, 0)),
                   pl.BlockSpec((SUBLANES, SSM_HEADS), lambda i: (0, 0))] + [hbm] * npair,
        out_shape=[jax.ShapeDtypeStruct((nrow, SSM_XBC), BF16), jax.ShapeDtypeStruct((nrow, SSM_HEADS), F32),
                   jax.ShapeDtypeStruct((nrow, SSM_HEADS), F32), jax.ShapeDtypeStruct((nrow // CHUNK, 1, SSM_INNER), F32),
                   jax.ShapeDtypeStruct((SUBLANES, SSM_HEADS), F32)]
        + [jax.ShapeDtypeStruct(p.shape, p.dtype) for p in pairs],
        scratch_shapes=[pltpu.VMEM((SSM_INNER, SSM_STATE), F32), pltpu.VMEM((CHUNK, SSM_XBC), F32),
                        pltpu.VMEM((CHUNK, SSM_XBC), F32), wide(), wide(), wide(), wide(),
                        pltpu.VMEM((3 * CHUNK, SSM_INNER), F32),
                        group(), group(), group(), group(), group(), pltpu.VMEM((CHUNK, LANES), F32),
                        pltpu.VMEM((SSM_HEADS, SSM_STATE), F32), pltpu.VMEM((SUBLANES, SSM_HEADS), F32)]
        + _exchange_scratch(npair, N_CHIPS - 1),
        compiler_params=_params(("arbitrary",)),
    )(pre, dt, a_cum, a_flat, d_x, ind, ind_t, states, dy, *pairs)
    return res[:5], res[5:]


LATE = ["w_proj_a", "w_proj_b", "w_out", "ffn_w_up", "ffn_w_down"]
HALF_TILES = {"w_in": 128, "w_proj_a": 128, "w_proj_b": 256, "w_out": 128, "ffn_w_up": 128, "ffn_w_down": 176}


def _late_weights(stacks, shards):
    pa, pb, out, up, down = [_own_slot(stack, own) for stack, own in zip(stacks, shards)]
    return {"w_proj_a": pa.reshape(-1, D_MODEL), "w_proj_b": pb.reshape(-1, D_MODEL), "w_out": out.reshape(-1, D_MODEL),
            "w_up_g": _columns_from_chips(up[:2]), "w_up_v": _columns_from_chips(up[2:]),
            "w_down": down.reshape(-1, D_MODEL)}


def _pair_reduce(tag, names, stacks):
    core = lax.axis_index("c")
    own_half = [_row_half(s, core, 1) for s in stacks]
    other_half = _swap_cores("pair_grads_" + tag, [_row_half(s, 1 - core, 1) for s in stacks])
    return [_pair_sum("pair_" + n, a, b, tm=HALF_TILES[n]) for n, a, b in zip(names, own_half, other_half)]


def _local_step(x, target, w, late_shards):
    w = dict(w)
    g = {}
    bs_col = w["gmlp_bs"].reshape(GMLP_GROUPS, GMLP_BLOCK, 1)
    b0, b1 = w["gate_bias"][0:1], w["gate_bias"][1:2]

    xn = _rms_fwd("mix_norm", x, w["mix_norm_w"])
    big = dict(bm=1024, bn=1024, bk=1024)
    act16 = dict(out_dtype=BF16, **big)
    gates = _mm("in_gates", xn, w["w_g"], **act16)
    za = _mm("in_gmlp", xn, w["w_za"], **act16)
    z = _mm("in_z", xn, w["w_z"], **act16)
    xbc = _mm("in_xbc", xn, w["w_xbc"], **act16)
    dt_raw = _mm("in_dt", xn, w["w_dt"], bm=1024, bn=SSM_HEADS, bk=1024)

    pre = _conv_fwd("ssm_conv_fwd", xbc, w["ssm_conv_w"], w["ssm_conv_b"], tc=1024)
    dt, a_cum = _dt_prep("dt_prep", dt_raw, w["ssm_dt_bias"], w["ssm_a_log"])
    a_flat = jnp.transpose(a_cum.reshape(-1, CHUNK, SSM_HEADS), (0, 2, 1)).reshape(-1, 1, SSM_INNER)
    d_x = jnp.repeat(w["ssm_d"], SSM_HEAD_DIM, axis=1)
    ind = _head_indicator()
    y_ssd, states, late_stacks = _ssd_fwd("ssd_fwd", pre, dt, a_cum, a_flat, d_x, ind, late_shards)
    w.update(_late_weights(late_stacks, late_shards))
    yb_pre = _gate_norm_fwd("gate_norm_fwd", y_ssd, z, w["ssm_norm_w"])
    y_b = _mm("proj_b", yb_pre, w["w_proj_b"], bm=1024, bn=1024, bk=SSM_INNER, out_dtype=BF16)

    ya_pre = _gmlp_fwd("gmlp_fwd", za, w["gmlp_ln_w"], w["gmlp_ln_b"], w["gmlp_ws"], bs_col)
    y_a = _mm("proj_a", ya_pre, w["w_proj_a"], **act16)

    merged = _merge_fwd("merge_fwd", gates, y_a, y_b, b0, b1)
    h1 = _mm("out_proj", merged, w["w_out"], res=x, **big)

    hn = _rms_fwd("ffn_norm", h1, w["ffn_norm_w"])
    half = dict(bm=1024, bn=D_FF // 2, bk=1024, out_dtype=BF16)
    pg = _mm("ffn_up_gate", hn, w["w_up_g"], **half)
    pv = _mm("ffn_up_val", hn, w["w_up_v"], **half)
    cw, cb = w["ffn_conv_w"], w["ffn_conv_b"]
    gate, val, act = _ffn_act_fwd("ffn_act_fwd", pg, pv, cw[:, :D_FF], cw[:, D_FF:], cb[:, :D_FF], cb[:, D_FF:],
                                  tc=D_FF // 2)
    h2 = _mm("ffn_down", act, w["w_down"], res=h1, bm=1024, bn=1024, bk=D_FF // 2)

    dh2, loss_part, g["final_norm_w"] = _final_loss("final_loss", h2, target, w["final_norm_w"].reshape(1, D_MODEL))

    dact = _mm("d_act", dh2, w["w_down"], tb=True, **half)
    wgrad = dict(ta=True, bk=min(2048, x.shape[0]), out_dtype=BF16)
    g["w_down"] = _mm("dw_down", act, dh2, bm=D_FF // 2, bn=1024, **wgrad)
    dgate, dval = _ffn_act_bwd("ffn_act_bwd", dact, gate, val)
    dpg, dcwg, dcbg = _conv_bwd("ffn_conv_bwd_gate", dgate, pg, cw[:, :D_FF], tc=D_FF // 2)
    dpv, dcwv, dcbv = _conv_bwd("ffn_conv_bwd_val", dval, pv, cw[:, D_FF:], tc=D_FF // 2)
    g["ffn_conv_w"] = jnp.concatenate([dcwg, dcwv], axis=1)
    g["ffn_conv_b"] = jnp.concatenate([dcbg, dcbv], axis=1)
    dhn = _mm_sum("d_hn", [(dpg, w["w_up_g"]), (dpv, w["w_up_v"])], bm=1024, bk=D_FF // 2)
    g["w_up_g"] = _mm("dw_up_gate", hn, dpg, bm=1024, bn=D_FF // 2, **wgrad)
    g["w_up_v"] = _mm("dw_up_val", hn, dpv, bm=1024, bn=D_FF // 2, **wgrad)
    dh1, g["ffn_norm_w"] = _rms_bwd("ffn_norm_bwd", h1, w["ffn_norm_w"], dhn, dh2)

    dmerged = _mm("d_merged", dh1, w["w_out"], tb=True, **act16)
    g["w_out"] = _mm("dw_out", merged, dh1, bm=1024, bn=1024, **wgrad)
    dgates, dya, dyb, db0, db1 = _merge_bwd("merge_bwd", gates, y_a, y_b, dmerged, b0, b1)
    g["gate_bias"] = jnp.concatenate([db0, db1], axis=0)

    dya_pre = _mm("d_ya_pre", dya, w["w_proj_a"], tb=True, **act16)
    g["w_proj_a"] = _mm("dw_proj_a", ya_pre, dya, bm=1024, bn=1024, **wgrad)
    dyb_pre = _mm("d_yb_pre", dyb, w["w_proj_b"], tb=True, **act16)
    g["w_proj_b"] = _mm("dw_proj_b", yb_pre, dyb, bm=1024, bn=1024, **wgrad)
    late_pairs = _pair_reduce("late", LATE, [
        g["w_proj_a"].reshape(N_CHIPS, -1, D_MODEL), g["w_proj_b"].reshape(N_CHIPS, -1, D_MODEL),
        g["w_out"].reshape(N_CHIPS, -1, D_MODEL),
        jnp.concatenate([_columns_to_chips(g["w_up_g"], 2), _columns_to_chips(g["w_up_v"], 2)], axis=0),
        g["w_down"].reshape(N_CHIPS, -1, D_MODEL)])

    dy_ssd, dz, g["ssm_norm_w"] = _gate_norm_bwd("gate_norm_bwd", y_ssd, z, dyb_pre, w["ssm_norm_w"])
    (dpre, ddt, da_tok, da_flat, dd), late_received = _ssd_bwd(
        "ssd_bwd", pre, dt, a_cum, a_flat, d_x, ind, ind.T, states, dy_ssd, late_pairs)
    g["ssm_d"] = dd[0:1]
    da_src = jnp.transpose(da_flat.reshape(-1, SSM_HEADS, CHUNK), (0, 2, 1)).reshape(-1, SSM_HEADS)
    ddt_raw, g["ssm_dt_bias"], g["ssm_a_log"] = _dt_bwd("dt_bwd", dt_raw, ddt, da_tok, da_src,
                                                         w["ssm_dt_bias"], w["ssm_a_log"])
    dxbc, g["ssm_conv_w"], g["ssm_conv_b"] = _conv_bwd("ssm_conv_bwd", dpre, xbc, w["ssm_conv_w"], tc=1024)

    dza, g["gmlp_ln_w"], g["gmlp_ln_b"], g["gmlp_ws"], dbs = _gmlp_bwd(
        "gmlp_bwd", za, dya_pre, w["gmlp_ln_w"], w["gmlp_ln_b"], w["gmlp_ws"], bs_col)
    g["gmlp_bs"] = dbs.reshape(GMLP_GROUPS, GMLP_BLOCK)

    dw_in = jnp.concatenate([
        _mm("dw_gates", xn, dgates, bm=1024, bn=1024, **wgrad), _mm("dw_gmlp", xn, dza, bm=1024, bn=1024, **wgrad),
        _mm("dw_z", xn, dz, bm=1024, bn=1024, **wgrad), _mm("dw_xbc", xn, dxbc, bm=1024, bn=1024, **wgrad),
        _mm("dw_dt", xn, ddt_raw, bm=1024, bn=SSM_HEADS, **wgrad)], axis=1)
    in_pairs = _pair_reduce("in", ["w_in"], [_columns_to_chips(dw_in)])
    dxn, in_received = _mm_sum("d_xn", [(dgates, w["w_g"]), (dza, w["w_za"]), (dz, w["w_z"]), (dxbc, w["w_xbc"]),
                                        (ddt_raw, w["w_dt"])], bm=1024, bk=1024, exchange=in_pairs)
    grad_x, g["mix_norm_w"] = _rms_bwd("mix_norm_bwd", x, w["mix_norm_w"], dxn, dh1)
    return loss_part, grad_x, g, in_pairs + list(late_pairs), list(in_received) + list(late_received)


def _position():
    return lax.axis_index("x"), lax.axis_index("y"), lax.axis_index("c")


def _own_slot(stack, own):
    chip = 2 * lax.axis_index("x") + lax.axis_index("y")
    return lax.dynamic_update_index_in_dim(stack, own, chip, axis=0)


def _scatter_phases(ins, outs, send_sems, recv_sems):
    n = len(ins)
    x, y, c = _position()
    me = 2 * x + y
    peers = [(1 - x, y), (x, 1 - y), (1 - x, 1 - y)]

    def copy(i, k, src_slot, dst_slot):
        px, py = peers[k]
        return pltpu.make_async_remote_copy(
            src_ref=ins[i].at[src_slot], dst_ref=outs[i].at[dst_slot],
            send_sem=send_sems.at[i, k], recv_sem=recv_sems.at[i, k],
            device_id=(px, py, c), device_id_type=MESH)

    def start():
        for i in range(n):
            for k, (px, py) in enumerate(peers):
                copy(i, k, 2 * px + py, me).start()

    def finish():
        for i in range(n):
            for k, (px, py) in enumerate(peers):
                copy(i, k, me, 2 * px + py).wait_recv()
        for i in range(n):
            for k, (px, py) in enumerate(peers):
                copy(i, k, 2 * px + py, me).wait_send()

    return start, finish


def _exchange_scratch(n, per_array):
    return [pltpu.SemaphoreType.DMA((n, per_array)), pltpu.SemaphoreType.DMA((n, per_array))]


def _half_rows(ref_rows, which):
    half = ref_rows // 2
    return pl.ds(pl.multiple_of(which * half, 2 * SUBLANES), half)


GATHER_SEMS = 2 * (N_CHIPS - 1)


def _gather_phases(nrows, ns, ins, outs, send_sems, recv_sems):
    n = len(ins)
    x, y, c = _position()
    me = 2 * x + y
    sibling = (x, y, 1 - c)
    chips = [(1 - x, y), (x, 1 - y), (1 - x, 1 - y)]

    def remote(i, k, src, dst, to):
        return pltpu.make_async_remote_copy(src_ref=src, dst_ref=dst, send_sem=send_sems.at[i, k],
                                            recv_sem=recv_sems.at[i, k], device_id=to, device_id_type=MESH)

    def over_ici(i, k):
        px, py = chips[k]
        rows = _half_rows(nrows[i], c) if i < ns else slice(None)
        return remote(i, k, ins[i].at[rows], outs[i].at[me, rows], (px, py, c))

    def landed(i, k, which):
        px, py = chips[k]
        return outs[i].at[2 * px + py, _half_rows(nrows[i], which)] if i < ns else outs[i].at[2 * px + py]

    def start():
        for i in range(n):
            for k in range(N_CHIPS - 1):
                over_ici(i, k).start()

    def forward():
        for i in range(n):
            for k in range(N_CHIPS - 1):
                piece = landed(i, k, c)
                remote(i, k, piece, piece, (*chips[k], c)).wait_recv()
                if i < ns:
                    remote(i, N_CHIPS - 1 + k, piece, piece, sibling).start()

    def finish():
        for i in range(ns):
            for k in range(N_CHIPS - 1):
                piece = landed(i, k, 1 - c)
                remote(i, N_CHIPS - 1 + k, piece, piece, sibling).wait_recv()
        for i in range(n):
            for k in range(N_CHIPS - 1):
                over_ici(i, k).wait_send()
                if i < ns:
                    piece = landed(i, k, c)
                    remote(i, N_CHIPS - 1 + k, piece, piece, sibling).wait_send()

    return start, forward, finish


def _gather_chips_split(name, split, whole):
    arrs = list(split) + list(whole)
    n = len(arrs)

    def body(*refs):
        phases = _gather_phases([a.shape[0] for a in arrs], len(split), refs[:n], refs[n:2 * n], *refs[2 * n:])
        for phase in phases:
            phase()

    hbm = pl.BlockSpec(memory_space=pl.ANY)
    return pl.pallas_call(
        body, name=name, in_specs=[hbm] * n, out_specs=[hbm] * n,
        out_shape=[jax.ShapeDtypeStruct((N_CHIPS,) + a.shape, a.dtype) for a in arrs],
        scratch_shapes=_exchange_scratch(n, GATHER_SEMS),
        compiler_params=pltpu.CompilerParams(has_side_effects=True),
    )(*arrs)


def _swap_cores(name, arrs):
    n = len(arrs)

    def body(*refs):
        ins, outs = refs[:n], refs[n:2 * n]
        send_sems, recv_sems = refs[2 * n:]
        x, y, c = _position()
        copies = [pltpu.make_async_remote_copy(src_ref=ins[i], dst_ref=outs[i], send_sem=send_sems.at[i],
                                               recv_sem=recv_sems.at[i], device_id=(x, y, 1 - c), device_id_type=MESH)
                  for i in range(n)]
        for cp in copies:
            cp.start()
        for cp in copies:
            cp.wait_recv()
        for cp in copies:
            cp.wait_send()

    hbm = pl.BlockSpec(memory_space=pl.ANY)
    return pl.pallas_call(
        body, name=name, in_specs=[hbm] * n, out_specs=[hbm] * n,
        out_shape=[jax.ShapeDtypeStruct(a.shape, a.dtype) for a in arrs],
        scratch_shapes=[pltpu.SemaphoreType.DMA((n,)), pltpu.SemaphoreType.DMA((n,))],
        compiler_params=pltpu.CompilerParams(has_side_effects=True),
    )(*arrs)


def _row_half(a, which, axis):
    half = a.shape[axis] // 2
    return lax.dynamic_slice_in_dim(a, which * half, half, axis=axis)


def _all_reduce(name, pack):
    def body(in_ref, out_ref, buf, send_sems, recv_sems):
        x, y, c = _position()
        me = 4 * x + 2 * y + c
        flips = [(dx, dy, dc) for dx in (0, 1) for dy in (0, 1) for dc in (0, 1) if (dx, dy, dc) != (0, 0, 0)]
        peers = [((1 - x) if dx else x, (1 - y) if dy else y, (1 - c) if dc else c) for dx, dy, dc in flips]
        buf[me] = in_ref[...]
        sends = []
        for k, peer in enumerate(peers):
            cp = pltpu.make_async_remote_copy(src_ref=in_ref, dst_ref=buf.at[me], send_sem=send_sems.at[k],
                                              recv_sem=recv_sems.at[k], device_id=peer, device_id_type=MESH)
            cp.start()
            sends.append(cp)
        for k, (px, py, pc) in enumerate(peers):
            pltpu.make_async_remote_copy(src_ref=in_ref, dst_ref=buf.at[4 * px + 2 * py + pc], send_sem=send_sems.at[k],
                                         recv_sem=recv_sems.at[k], device_id=(px, py, pc), device_id_type=MESH).wait_recv()
        total = buf[0]
        for j in range(1, N_DEV):
            total = total + buf[j]
        out_ref[...] = total
        for cp in sends:
            cp.wait_send()

    vmem = pl.BlockSpec(memory_space=pltpu.VMEM)
    return pl.pallas_call(
        body, name=name, in_specs=[vmem], out_specs=vmem,
        out_shape=jax.ShapeDtypeStruct(pack.shape, F32),
        scratch_shapes=[pltpu.VMEM((N_DEV,) + pack.shape, F32), pltpu.SemaphoreType.DMA((N_DEV - 1,)),
                        pltpu.SemaphoreType.DMA((N_DEV - 1,))],
        compiler_params=pltpu.CompilerParams(has_side_effects=True, vmem_limit_bytes=VMEM_LIMIT_V7X),
    )(pack)


def _pack(arrs):
    rows = []
    for a in arrs:
        r = a.reshape(-1, LANES)
        rows.append(jnp.pad(r, ((0, -r.shape[0] % SUBLANES), (0, 0))))
    return jnp.concatenate(rows, axis=0)


def _unpack(pack, shapes):
    out, off = [], 0
    for s in shapes:
        nrow = 1
        for d in s:
            nrow *= d
        nrow //= LANES
        out.append(pack[off:off + nrow].reshape(s))
        off += nrow + (-nrow % SUBLANES)
    return out


SMALL = ["mix_norm_w", "gate_bias", "gmlp_ln_w", "gmlp_ln_b", "gmlp_ws", "gmlp_bs", "ssm_conv_w", "ssm_conv_b",
         "ssm_dt_bias", "ssm_a_log", "ssm_d", "ssm_norm_w", "ffn_norm_w", "ffn_conv_w", "ffn_conv_b", "final_norm_w"]
SMALL_SHARDED = ("gate_bias", "ssm_conv_w", "ffn_conv_w")
BIG = ["w_in", "w_proj_a", "w_proj_b", "w_out", "ffn_w_up", "ffn_w_down"]
WEIGHTS = ["mix_norm_w", "w_in", "gate_bias", "gmlp_ln_w", "gmlp_ln_b", "gmlp_ws", "gmlp_bs", "ssm_conv_w",
           "ssm_conv_b", "ssm_dt_bias", "ssm_a_log", "ssm_d", "ssm_norm_w", "w_proj_a", "w_proj_b", "w_out",
           "ffn_norm_w", "ffn_w_up", "ffn_conv_w", "ffn_conv_b", "ffn_w_down", "final_norm_w"]
IN_SPLITS = [0, 2048, 4096, 6144, 9216, 9248]


def _columns_from_chips(stack):
    return jnp.transpose(stack, (1, 0, 2)).reshape(stack.shape[1], -1)


def _columns_to_chips(full, parts=N_CHIPS):
    rows, cols = full.shape
    return jnp.transpose(full.reshape(rows, parts, cols // parts), (1, 0, 2))


def kernel(x, mix_norm_w, w_in, gate_bias, gmlp_ln_w, gmlp_ln_b, gmlp_ws, gmlp_bs, ssm_conv_w, ssm_conv_b, ssm_dt_bias, ssm_a_log, ssm_d, ssm_norm_w, w_proj_a, w_proj_b, w_out, ffn_norm_w, ffn_w_up, ffn_conv_w, ffn_conv_b, ffn_w_down, final_norm_w, loss_target, m_mix_norm_w, m_w_in, m_gate_bias, m_gmlp_ln_w, m_gmlp_ln_b, m_gmlp_ws, m_gmlp_bs, m_ssm_conv_w, m_ssm_conv_b, m_ssm_dt_bias, m_ssm_a_log, m_ssm_d, m_ssm_norm_w, m_w_proj_a, m_w_proj_b, m_w_out, m_ffn_norm_w, m_ffn_w_up, m_ffn_conv_w, m_ffn_conv_b, m_ffn_w_down, m_final_norm_w, v_mix_norm_w, v_w_in, v_gate_bias, v_gmlp_ln_w, v_gmlp_ln_b, v_gmlp_ws, v_gmlp_bs, v_ssm_conv_w, v_ssm_conv_b, v_ssm_dt_bias, v_ssm_a_log, v_ssm_d, v_ssm_norm_w, v_w_proj_a, v_w_proj_b, v_w_out, v_ffn_norm_w, v_ffn_w_up, v_ffn_conv_w, v_ffn_conv_b, v_ffn_w_down, v_final_norm_w):
    args = dict(locals())
    weights = {n: args[n] for n in WEIGHTS}
    moments_m = {n: args["m_" + n] for n in WEIGHTS}
    moments_v = {n: args["v_" + n] for n in WEIGHTS}
    chip = 2 * lax.axis_index("x") + lax.axis_index("y")

    shards = [weights["w_in"][0].astype(BF16)] + [weights[n][0] for n in SMALL_SHARDED]
    gathered = _gather_chips_split("gather_weights", shards[:1], shards[1:])
    w_in_s, gb_s, scw_s, fcw_s = [_own_slot(stack, own) for stack, own in zip(gathered, shards)]
    late_shards = [weights[n][0].astype(BF16) for n in LATE]
    w_in_full = _columns_from_chips(w_in_s)
    full = {"w_" + nm: w_in_full[:, IN_SPLITS[k]:IN_SPLITS[k + 1]] for k, nm in enumerate(["g", "za", "z", "xbc", "dt"])}
    full["gate_bias"] = _columns_from_chips(gb_s)
    full["ssm_conv_w"] = _columns_from_chips(scw_s)
    full["ffn_conv_w"] = _columns_from_chips(fcw_s)
    for n in SMALL:
        if n not in SMALL_SHARDED:
            full[n] = weights[n] if n == "final_norm_w" else weights[n][0]
    for n in ("mix_norm_w", "ffn_norm_w", "ssm_conv_b", "ssm_dt_bias", "ssm_a_log", "ssm_d", "ssm_norm_w", "ffn_conv_b"):
        full[n] = full[n].reshape(1, -1)

    loss_part, grad_x, g, pair, received = _local_step(x[0], loss_target[0], full, late_shards)

    per_head = ["ssm_dt_bias", "ssm_a_log", "ssm_d"]
    rest = [n for n in SMALL if n not in per_head]
    head_row = jnp.concatenate([g[n] for n in per_head] + [jnp.zeros((1, LANES - 3 * SSM_HEADS), F32)], axis=1)
    pack = _pack([loss_part, head_row] + [g[n] for n in rest])
    reduced = _unpack(_all_reduce("reduce_small", pack), [(1, LANES), (1, LANES)] + [g[n].shape for n in rest])
    loss = reduced[0][0, 0]
    small_grads = {n: reduced[1][:, k * SSM_HEADS:(k + 1) * SSM_HEADS] for k, n in enumerate(per_head)}
    for n, r in zip(rest, reduced[2:]):
        if n in SMALL_SHARDED:
            width = weights[n].shape[2]
            r = lax.dynamic_slice_in_dim(r, chip * width, width, axis=1)
        small_grads[n] = r
    two_d = lambda a: a.reshape(-1, a.shape[-1])
    upd = _adamw_small("adamw_small", *[[two_d(d[n]) for n in SMALL]
                                        for d in (weights, small_grads, moments_m, moments_v)])
    small_out = [[small_grads[n] for n in SMALL]] + list(upd)
    small_out = [[a.reshape(weights[n].shape) for n, a in zip(SMALL, kind)] for kind in small_out]

    received = [_own_slot(r, lax.dynamic_index_in_dim(p, chip, 0, keepdims=False)) for r, p in zip(received, pair)]
    halves = [_sum_slots("sum_" + n, r, tm=HALF_TILES[n], rs=2 * SUBLANES) for n, r in zip(BIG, received)]
    tiles = {"w_in": 128, "w_proj_a": 256, "w_proj_b": 256, "w_out": 256, "ffn_w_up": 128, "ffn_w_down": 176}
    core = lax.axis_index("c")
    other = _swap_cores("join_grads", halves)
    reduced = [jnp.concatenate([jnp.where(core == 0, a, b), jnp.where(core == 0, b, a)], axis=0)
               for a, b in zip(halves, other)]
    big_out = {}
    for n, grad in zip(BIG, reduced):
        big_out[n] = _adamw("adamw_" + n, weights[n][0], grad, moments_m[n][0], moments_v[n][0],
                            tm=tiles[n], rs=SUBLANES)

    per_kind = [[], [], [], []]
    for n in WEIGHTS:
        for kind in range(4):
            if n in big_out:
                per_kind[kind].append(big_out[n][kind].reshape(weights[n].shape))
            else:
                per_kind[kind].append(small_out[kind][SMALL.index(n)])
    return (loss, grad_x[None], *per_kind[0], *per_kind[1], *per_kind[2], *per_kind[3])
```

```python
import jax
import jax.numpy as jnp
from jax import lax
from jax.experimental import pallas as pl
from jax.experimental.pallas import tpu as pltpu

F32 = jnp.float32
BF16 = jnp.bfloat16
MESH = pl.DeviceIdType.MESH

EPS = 1e-5
D_MODEL = 1024
GMLP_BLOCK = 128
GMLP_GROUPS = 8
CHUNK = 64
SSM_INNER = 2048
SSM_HEADS = 32
SSM_HEAD_DIM = 64
SSM_GROUPS = 4
SSM_HPG = 8
SSM_STATE = 128
SSM_CONV = 4
SSM_XBC = 3072
D_FF = 2816
FFN_CONV = 3
N_CHIPS = 4
N_DEV = 8

ADAM_LR = 0.001
ADAM_B1 = 0.9
ADAM_B2 = 0.999
ADAM_EPS = 1e-08
ADAM_WD = 0.01
ADAM_STEP = 10

VMEM_LIMIT_V7X = 56 * 1024 * 1024
SUBLANES = 8
LANES = 128


def _params(sem=None):
    return pltpu.CompilerParams(dimension_semantics=sem, vmem_limit_bytes=VMEM_LIMIT_V7X)


def _dot(a, b, ca=1, cb=0):
    return lax.dot_general(a.astype(BF16), b.astype(BF16), (((ca,), (cb,)), ((), ())),
                           preferred_element_type=F32)


def _mm(name, a, b, *, ta=False, tb=False, out_dtype=F32, bm, bn, bk, res=None):
    m, k = (a.shape[1], a.shape[0]) if ta else a.shape
    k2, n = (b.shape[1], b.shape[0]) if tb else b.shape
    assert k == k2 and m % bm == 0 and n % bn == 0 and k % bk == 0, (name, a.shape, b.shape)
    nk = k // bk
    a_spec = (pl.BlockSpec((bk, bm), lambda i, j, kk: (kk, i)) if ta
              else pl.BlockSpec((bm, bk), lambda i, j, kk: (i, kk)))
    b_spec = (pl.BlockSpec((bn, bk), lambda i, j, kk: (j, kk)) if tb
              else pl.BlockSpec((bk, bn), lambda i, j, kk: (kk, j)))
    o_spec = pl.BlockSpec((bm, bn), lambda i, j, kk: (i, j))
    has_res = res is not None

    def body(*refs):
        a_ref, b_ref = refs[0], refs[1]
        r_ref = refs[2] if has_res else None
        o_ref = refs[3] if has_res else refs[2]
        p = _dot(a_ref[...], b_ref[...], 0 if ta else 1, 1 if tb else 0)

        def finish(total):
            if has_res:
                total = total + r_ref[...]
            o_ref[...] = total.astype(out_dtype)

        if nk == 1:
            finish(p)
        else:
            acc_ref = refs[-1]
            kk = pl.program_id(2)

            @pl.when(kk == 0)
            def _():
                acc_ref[...] = p

            @pl.when(kk > 0)
            def _():
                acc_ref[...] += p

            @pl.when(kk == nk - 1)
            def _():
                finish(acc_ref[...])

    return pl.pallas_call(
        body, name=name,
        grid=(m // bm, n // bn, nk),
        in_specs=[a_spec, b_spec] + ([o_spec] if has_res else []),
        out_specs=o_spec,
        out_shape=jax.ShapeDtypeStruct((m, n), out_dtype),
        scratch_shapes=[pltpu.VMEM((bm, bn), F32)] if nk > 1 else [],
        compiler_params=_params(("parallel", "parallel", "arbitrary")),
    )(*([a, b] + ([res] if has_res else [])))


def _mm_sum(name, pairs, *, bm, bk, exchange=()):
    nx = len(exchange)
    npair = len(pairs)
    m, n = pairs[0][0].shape[0], pairs[0][1].shape[0]
    steps, first = [], []
    for a, b in pairs:
        k = a.shape[1]
        assert a.shape[0] == m and b.shape == (n, k) and m % bm == 0 and (k % bk == 0 or k < bk), (name, a.shape, b.shape)
        first.append(sum(steps))
        steps.append(max(k // bk, 1))
    total = sum(steps)
    in_specs = []
    for (a, b), off, cnt in zip(pairs, first, steps):
        width = min(bk, a.shape[1])
        in_specs.append(pl.BlockSpec((bm, width), lambda i, kk, off=off, cnt=cnt: (i, jnp.clip(kk - off, 0, cnt - 1))))
        in_specs.append(pl.BlockSpec((n, width), lambda i, kk, off=off, cnt=cnt: (0, jnp.clip(kk - off, 0, cnt - 1))))

    def body(*refs):
        send_refs = refs[2 * npair:2 * npair + nx]
        o_ref = refs[2 * npair + nx]
        recv_refs = refs[2 * npair + nx + 1:2 * npair + 2 * nx + 1]
        acc_ref = refs[2 * npair + 2 * nx + 1]
        i, kk = pl.program_id(0), pl.program_id(1)
        if nx:
            start, finish = _scatter_phases(send_refs, recv_refs, *refs[2 * npair + 2 * nx + 2:])

            @pl.when((i == 0) & (kk == 0))
            def _():
                start()

        for s, (off, cnt) in enumerate(zip(first, steps)):
            @pl.when((kk >= off) & (kk < off + cnt))
            def _(s=s, off=off):
                p = _dot(refs[2 * s][...], refs[2 * s + 1][...], 1, 1)
                if off == 0:
                    @pl.when(kk == 0)
                    def _():
                        acc_ref[...] = p

                    @pl.when(kk > 0)
                    def _():
                        acc_ref[...] += p
                else:
                    acc_ref[...] += p

        @pl.when(kk == total - 1)
        def _():
            o_ref[...] = acc_ref[...]

        if nx:
            @pl.when((i == m // bm - 1) & (kk == total - 1))
            def _():
                finish()

    hbm = pl.BlockSpec(memory_space=pl.ANY)
    res = pl.pallas_call(
        body, name=name, grid=(m // bm, total),
        in_specs=in_specs + [hbm] * nx, out_specs=[pl.BlockSpec((bm, n), lambda i, kk: (i, 0))] + [hbm] * nx,
        out_shape=[jax.ShapeDtypeStruct((m, n), F32)] + [jax.ShapeDtypeStruct(e.shape, e.dtype) for e in exchange],
        scratch_shapes=[pltpu.VMEM((bm, n), F32)] + (_exchange_scratch(nx, N_CHIPS - 1) if nx else []),
        compiler_params=_params(("arbitrary", "arbitrary")),
    )(*[t for pair in pairs for t in pair], *exchange)
    return (res[0], res[1:]) if nx else res[0]


def _rows(name, fn, ins, params, outs, accs, *, tm, rs, unroll=4):
    nrow = ins[0][0].shape[-2]
    while tm % (rs * unroll):
        unroll //= 2
    assert nrow % tm == 0 and tm % rs == 0, (name, nrow, tm, rs)
    n_in, n_p, n_out, n_acc = len(ins), len(params), len(outs), len(accs)
    in_specs = []
    for spec in ins:
        arr, width, cb = spec[:3]
        if len(spec) == 4:
            in_specs.append(pl.BlockSpec((None, tm, width), lambda i, cb=cb, lead=spec[3]: (lead, i, cb)))
        else:
            in_specs.append(pl.BlockSpec((tm, width), lambda i, cb=cb: (i, cb)))
    for p in params:
        in_specs.append(pl.BlockSpec(p.shape, lambda i, nd=p.ndim: (0,) * nd))
    out_specs = [pl.BlockSpec((tm, w), lambda i: (i, 0)) for w, _ in outs]
    out_specs += [pl.BlockSpec(s, lambda i: (0, 0)) for s in accs]
    out_shape = [jax.ShapeDtypeStruct((nrow, w), dt) for w, dt in outs]
    out_shape += [jax.ShapeDtypeStruct(s, F32) for s in accs]

    def body(*refs):
        in_refs = refs[:n_in]
        p_refs = refs[n_in:n_in + n_p]
        o_refs = refs[n_in + n_p:n_in + n_p + n_out]
        a_refs = refs[n_in + n_p + n_out:]
        pv = [p[...] for p in p_refs]

        if n_acc:
            @pl.when(pl.program_id(0) == 0)
            def _():
                for a_ref in a_refs:
                    a_ref[...] = jnp.zeros_like(a_ref)

        def step(r, carry):
            for u in range(unroll):
                sl = pl.ds(pl.multiple_of((r * unroll + u) * rs, rs), rs)
                vals = [ref[sl, :].astype(F32) for ref in in_refs]
                row_out, sums = fn(*vals, *pv)
                for o_ref, v in zip(o_refs, row_out):
                    o_ref[sl, :] = v.astype(o_ref.dtype)
                carry = tuple(c + s for c, s in zip(carry, sums))
            return carry

        init = tuple(jnp.zeros(s, F32) for s in accs)
        total = lax.fori_loop(0, tm // (rs * unroll), step, init)
        for a_ref, t in zip(a_refs, total):
            a_ref[...] += t

    res = pl.pallas_call(
        body, name=name, grid=(nrow // tm,),
        in_specs=in_specs, out_specs=out_specs, out_shape=out_shape,
        compiler_params=_params(("arbitrary",)),
    )(*([s[0] for s in ins] + list(params)))
    return res


def _rms(x, w):
    return x * lax.rsqrt(jnp.mean(x * x, axis=-1, keepdims=True) + EPS) * w


def _colsum(v):
    return jnp.sum(v, axis=0, keepdims=True)


def _rms_fwd(name, x, w):
    def fn(xv, wv):
        return (_rms(xv, wv),), ()
    return _rows(name, fn, [(x, D_MODEL, 0)], [w], [(D_MODEL, BF16)], [], tm=1024, rs=16)[0]


def _rms_fwd_gather(name, x, w, split, whole):
    arrs = list(split) + list(whole)
    n = len(arrs)
    tm, rs, unroll = 1024, 16, 4
    nrow = x.shape[0]
    nstep = nrow // tm

    def body(x_ref, w_ref, *rest):
        in_refs, o_ref, out_refs = rest[:n], rest[n], rest[n + 1:2 * n + 1]
        step = pl.program_id(0)
        start, forward, finish = _gather_phases([a.shape[0] for a in arrs], len(split), in_refs, out_refs,
                                                *rest[2 * n + 1:])

        @pl.when(step == 0)
        def _():
            start()

        wv = w_ref[...]

        def sub(r, carry):
            for u in range(unroll):
                sl = pl.ds(pl.multiple_of((r * unroll + u) * rs, rs), rs)
                o_ref[sl, :] = _rms(x_ref[sl, :], wv).astype(o_ref.dtype)
            return carry

        lax.fori_loop(0, tm // (rs * unroll), sub, 0)

        @pl.when(step == nstep - 1)
        def _():
            forward()
            finish()

    hbm = pl.BlockSpec(memory_space=pl.ANY)
    rows = pl.BlockSpec((tm, D_MODEL), lambda i: (i, 0))
    res = pl.pallas_call(
        body, name=name, grid=(nstep,),
        in_specs=[rows, pl.BlockSpec(w.shape, lambda i: (0, 0))] + [hbm] * n,
        out_specs=[rows] + [hbm] * n,
        out_shape=[jax.ShapeDtypeStruct((nrow, D_MODEL), BF16)]
        + [jax.ShapeDtypeStruct((N_CHIPS,) + a.shape, a.dtype) for a in arrs],
        scratch_shapes=_exchange_scratch(n, GATHER_SEMS),
        compiler_params=_params(("arbitrary",)),
    )(x, w, *arrs)
    return res[0], res[1:]


def _rms_bwd(name, x, w, dy, dres):
    def fn(xv, dyv, drv, wv):
        _, vjp = jax.vjp(_rms, xv, wv)
        dx, dw = vjp(dyv)
        return (drv + dx,), (dw,)
    return _rows(name, fn, [(x, D_MODEL, 0), (dy, D_MODEL, 0), (dres, D_MODEL, 0)], [w],
                 [(D_MODEL, F32)], [(1, D_MODEL)], tm=1024, rs=16)


def _final_loss(name, h, target, w):
    def fn(hv, tv, wv):
        y, vjp = jax.vjp(_rms, hv, wv)
        err = y - tv
        part = 0.5 * jnp.sum(jnp.mean(err * err, axis=-1, keepdims=True), axis=0, keepdims=True)
        dh, dw = vjp(err / D_MODEL)
        return (dh,), (jnp.broadcast_to(part, (1, LANES)), dw)
    return _rows(name, fn, [(h, D_MODEL, 0), (target, D_MODEL, 0)], [w],
                 [(D_MODEL, F32)], [(1, LANES), (1, D_MODEL)], tm=1024, rs=16)


def _merge(ga, gb, ya, yb, b0, b1):
    return jax.nn.sigmoid(ga + b0) * ya + jax.nn.sigmoid(gb + b1) * yb


def _merge_fwd(name, g, ya, yb, b0, b1):
    def fn(ga, gb, yav, ybv, b0v, b1v):
        return (_merge(ga, gb, yav, ybv, b0v, b1v),), ()
    return _rows(name, fn, [(g, D_MODEL, 0), (g, D_MODEL, 1), (ya, D_MODEL, 0), (yb, D_MODEL, 0)],
                 [b0, b1], [(D_MODEL, BF16)], [], tm=1024, rs=16)[0]


def _merge_bwd(name, g, ya, yb, dm, b0, b1):
    def fn(ga, gb, yav, ybv, dmv, b0v, b1v):
        _, vjp = jax.vjp(_merge, ga, gb, yav, ybv, b0v, b1v)
        dga, dgb, dya, dyb, db0, db1 = vjp(dmv)
        return (jnp.concatenate([dga, dgb], axis=1), dya, dyb), (db0, db1)
    return _rows(name, fn,
                 [(g, D_MODEL, 0), (g, D_MODEL, 1), (ya, D_MODEL, 0), (yb, D_MODEL, 0), (dm, D_MODEL, 0)],
                 [b0, b1], [(2 * D_MODEL, BF16), (D_MODEL, BF16), (D_MODEL, BF16)],
                 [(1, D_MODEL), (1, D_MODEL)], tm=1024, rs=16)


GROUP_W = SSM_INNER // SSM_GROUPS


def _gate_norm_group(y, z, nw):
    v = y * jax.nn.silu(z)
    return v * lax.rsqrt(jnp.mean(v * v, axis=-1, keepdims=True) + EPS) * nw


def _gate_norm_fwd(name, y, z, nw):
    def fn(yv, zv, nwv):
        parts = [_gate_norm_group(yv[:, k * GROUP_W:(k + 1) * GROUP_W], zv[:, k * GROUP_W:(k + 1) * GROUP_W],
                                  nwv[:, k * GROUP_W:(k + 1) * GROUP_W]) for k in range(SSM_GROUPS)]
        return (jnp.concatenate(parts, axis=1),), ()
    return _rows(name, fn, [(y, SSM_INNER, 0), (z, SSM_INNER, 0)], [nw], [(SSM_INNER, BF16)], [],
                 tm=512, rs=16)[0]


def _gate_norm_bwd(name, y, z, dout, nw):
    def fn(yv, zv, dv, nwv):
        dys, dzs, dns = [], [], []
        for k in range(SSM_GROUPS):
            sl = slice(k * GROUP_W, (k + 1) * GROUP_W)
            _, vjp = jax.vjp(_gate_norm_group, yv[:, sl], zv[:, sl], nwv[:, sl])
            dy, dz, dn = vjp(dv[:, sl])
            dys.append(dy), dzs.append(dz), dns.append(dn)
        return (jnp.concatenate(dys, axis=1), jnp.concatenate(dzs, axis=1)), (jnp.concatenate(dns, axis=1),)
    return _rows(name, fn, [(y, SSM_INNER, 0), (z, SSM_INNER, 0), (dout, SSM_INNER, 0)], [nw],
                 [(SSM_INNER, BF16), (SSM_INNER, BF16)], [(1, SSM_INNER)], tm=512, rs=16)


def _softplus(v):
    return jnp.maximum(v, 0.0) + jnp.log1p(jnp.exp(-jnp.abs(v)))


def _chunk_cumsum(v, reverse=False):
    row = lax.broadcasted_iota(jnp.int32, v.shape, 0)
    step = 1
    while step < CHUNK:
        if reverse:
            shifted = pltpu.roll(v, CHUNK - step, axis=0)
            v = v + jnp.where(row < CHUNK - step, shifted, 0.0)
        else:
            shifted = pltpu.roll(v, step, axis=0)
            v = v + jnp.where(row >= step, shifted, 0.0)
        step *= 2
    return v


def _dt_prep(name, dt_raw, dt_bias, a_log):
    def fn(rv, bv, alv):
        dt = _softplus(rv + bv)
        return (dt, _chunk_cumsum(dt * (-jnp.exp(alv)))), ()
    return _rows(name, fn, [(dt_raw, SSM_HEADS, 0)], [dt_bias, a_log],
                 [(SSM_HEADS, F32), (SSM_HEADS, F32)], [], tm=512, rs=CHUNK)


def _dt_bwd(name, dt_raw, ddt, da1, da2, dt_bias, a_log):
    def fn(rv, ddv, d1, d2, bv, alv):
        pre = rv + bv
        dt = _softplus(pre)
        a_neg = -jnp.exp(alv)
        back = _chunk_cumsum(d1 + d2, reverse=True)
        d_dt = ddv + back * a_neg
        d_raw = d_dt * jax.nn.sigmoid(pre)
        return (d_raw,), (_colsum(d_raw), _colsum(back * dt) * a_neg)
    return _rows(name, fn, [(dt_raw, SSM_HEADS, 0), (ddt, SSM_HEADS, 0), (da1, SSM_HEADS, 0), (da2, SSM_HEADS, 0)],
                 [dt_bias, a_log], [(SSM_HEADS, BF16)], [(1, SSM_HEADS), (1, SSM_HEADS)], tm=512, rs=CHUNK)


def _adamw_math(w, g, m, v):
    m_new = ADAM_B1 * m + (1.0 - ADAM_B1) * g
    v_new = ADAM_B2 * v + (1.0 - ADAM_B2) * jnp.square(g)
    m_hat = m_new / (1.0 - ADAM_B1 ** ADAM_STEP)
    v_hat = v_new / (1.0 - ADAM_B2 ** ADAM_STEP)
    delta = -ADAM_LR * (m_hat / (jnp.sqrt(v_hat) + ADAM_EPS) + ADAM_WD * w)
    return delta, m_new, v_new


def _adamw(name, w, g, m, v, *, tm, rs):
    width = w.shape[1]

    def fn(wv, mv, vv, gv):
        return (gv,) + _adamw_math(wv, gv, mv, vv), ()
    return _rows(name, fn, [(w, width, 0), (m, width, 0), (v, width, 0), (g, width, 0)],
                 [], [(width, F32)] * 4, [], tm=tm, rs=rs)


def _adamw_small(name, ws, gs, ms, vs):
    n = len(ws)

    def body(*refs):
        w_refs, g_refs, m_refs, v_refs = (refs[k * n:(k + 1) * n] for k in range(4))
        outs = refs[4 * n:]
        for i in range(n):
            res = _adamw_math(w_refs[i][...], g_refs[i][...], m_refs[i][...], v_refs[i][...])
            for k in range(3):
                outs[k * n + i][...] = res[k]

    vmem = pl.BlockSpec(memory_space=pltpu.VMEM)
    res = pl.pallas_call(
        body, name=name, in_specs=[vmem] * (4 * n), out_specs=[vmem] * (3 * n),
        out_shape=[jax.ShapeDtypeStruct(w.shape, F32) for w in ws] * 3,
        compiler_params=pltpu.CompilerParams(vmem_limit_bytes=VMEM_LIMIT_V7X),
    )(*ws, *gs, *ms, *vs)
    return res[:n], res[n:2 * n], res[2 * n:]


def _pair_sum(name, a, b, *, tm):
    shape = a.shape
    flat = (shape[0] * shape[1], shape[2])

    def fn(av, bv):
        return (av.astype(F32) + bv.astype(F32),), ()
    out = _rows(name, fn, [(a.reshape(flat), flat[1], 0), (b.reshape(flat), flat[1], 0)], [], [(flat[1], BF16)], [],
                tm=tm, rs=2 * SUBLANES)[0]
    return out.reshape(shape)


def _sum_slots(name, stack, *, tm, rs):
    width = stack.shape[2]

    def fn(*slots):
        s0, s1, s2, s3 = (s.astype(F32) for s in slots)
        return (((s0 + s1) + s2) + s3,), ()
    return _rows(name, fn, [(stack, width, 0, k) for k in range(N_CHIPS)], [], [(width, F32)], [],
                 tm=tm, rs=rs)[0]


def _layernorm(v, w, b):
    mu = jnp.mean(v, axis=-1, keepdims=True)
    var = jnp.mean(jnp.square(v - mu), axis=-1, keepdims=True)
    return (v - mu) * lax.rsqrt(var + EPS) * w + b


GELU_C = 0.7978845608028654
GELU_A = 0.044715


def _gelu_and_slope(x):
    x2 = x * x
    t = jnp.tanh(GELU_C * x * (1.0 + GELU_A * x2))
    half = 0.5 * (1.0 + t)
    slope = half + 0.5 * x * (1.0 - t * t) * (GELU_C * (1.0 + 3.0 * GELU_A * x2))
    return x * half, slope


def _layernorm_and_back(v, w, b):
    mu = jnp.mean(v, axis=-1, keepdims=True)
    cen = v - mu
    rstd = lax.rsqrt(jnp.mean(cen * cen, axis=-1, keepdims=True) + EPS)
    vhat = cen * rstd

    def back(dout):
        dhat = dout * w
        dv = rstd * (dhat - jnp.mean(dhat, axis=-1, keepdims=True)
                     - vhat * jnp.mean(dhat * vhat, axis=-1, keepdims=True))
        return dv, _colsum(dout * vhat), _colsum(dout)

    return vhat * w + b, back


def _gmlp_mask():
    t = lax.broadcasted_iota(jnp.int32, (GMLP_BLOCK, GMLP_BLOCK), 0) // CHUNK
    s = lax.broadcasted_iota(jnp.int32, (GMLP_BLOCK, GMLP_BLOCK), 1) // CHUNK
    return s <= t


GMLP_TM = 1024


def _gmlp_fwd(name, za, ln_w, ln_b, ws, bs_col):
    nrow = za.shape[0]
    tm = GMLP_TM
    width = GMLP_GROUPS * GMLP_BLOCK

    def body(za_ref, lnw_ref, lnb_ref, ws_ref, bs_ref, o_ref, wm_ref):
        mask = _gmlp_mask()
        for g in range(GMLP_GROUPS):
            wm_ref[g] = jnp.where(mask, ws_ref[g], 0.0).astype(BF16)

        def block(n, carry):
            rows = pl.ds(pl.multiple_of(n * GMLP_BLOCK, GMLP_BLOCK), GMLP_BLOCK)
            for g in range(GMLP_GROUPS):
                cols = slice(g * GMLP_BLOCK, (g + 1) * GMLP_BLOCK)
                vcols = slice(width + g * GMLP_BLOCK, width + (g + 1) * GMLP_BLOCK)
                u = jax.nn.gelu(za_ref[rows, cols].astype(F32))
                v = jax.nn.gelu(za_ref[rows, vcols].astype(F32))
                vn = _layernorm(v, lnw_ref[g:g + 1, :], lnb_ref[g:g + 1, :])
                sv = _dot(wm_ref[g], vn) + bs_ref[g]
                o_ref[rows, cols] = (u * sv).astype(o_ref.dtype)
            return carry

        lax.fori_loop(0, tm // GMLP_BLOCK, block, 0)

    small = lambda a: pl.BlockSpec(a.shape, lambda i, nd=a.ndim: (0,) * nd)
    return pl.pallas_call(
        body, name=name, grid=(nrow // tm,),
        in_specs=[pl.BlockSpec((tm, 2 * width), lambda i: (i, 0)), small(ln_w), small(ln_b), small(ws), small(bs_col)],
        out_specs=pl.BlockSpec((tm, width), lambda i: (i, 0)),
        out_shape=jax.ShapeDtypeStruct((nrow, width), BF16),
        scratch_shapes=[pltpu.VMEM((GMLP_GROUPS, GMLP_BLOCK, GMLP_BLOCK), BF16)],
        compiler_params=_params(("arbitrary",)),
    )(za, ln_w, ln_b, ws, bs_col)


def _gmlp_bwd(name, za, dout, ln_w, ln_b, ws, bs_col):
    nrow = za.shape[0]
    tm = GMLP_TM
    width = GMLP_GROUPS * GMLP_BLOCK

    def body(za_ref, do_ref, lnw_ref, lnb_ref, ws_ref, bs_ref, dza_ref, dlnw_ref, dlnb_ref, dws_ref, dbs_ref, wm_ref):
        mask = _gmlp_mask()
        for g in range(GMLP_GROUPS):
            wm_ref[g] = jnp.where(mask, ws_ref[g], 0.0).astype(BF16)

        @pl.when(pl.program_id(0) == 0)
        def _():
            dlnw_ref[...] = jnp.zeros_like(dlnw_ref)
            dlnb_ref[...] = jnp.zeros_like(dlnb_ref)
            dws_ref[...] = jnp.zeros_like(dws_ref)
            dbs_ref[...] = jnp.zeros_like(dbs_ref)

        def block(n, carry):
            rows = pl.ds(pl.multiple_of(n * GMLP_BLOCK, GMLP_BLOCK), GMLP_BLOCK)
            for g in range(GMLP_GROUPS):
                cols = slice(g * GMLP_BLOCK, (g + 1) * GMLP_BLOCK)
                vcols = slice(width + g * GMLP_BLOCK, width + (g + 1) * GMLP_BLOCK)
                u, slope_u = _gelu_and_slope(za_ref[rows, cols].astype(F32))
                v, slope_v = _gelu_and_slope(za_ref[rows, vcols].astype(F32))
                vn, ln_back = _layernorm_and_back(v, lnw_ref[g:g + 1, :], lnb_ref[g:g + 1, :])
                sv = _dot(wm_ref[g], vn) + bs_ref[g]
                d_o = do_ref[rows, cols].astype(F32)
                dsv = d_o * u
                d_wm = _dot(dsv, vn, 1, 1)
                dvn = _dot(wm_ref[g], dsv, 0, 0)
                dv, dlnw, dlnb = ln_back(dvn)
                dza_ref[rows, cols] = (d_o * sv * slope_u).astype(dza_ref.dtype)
                dza_ref[rows, vcols] = (dv * slope_v).astype(dza_ref.dtype)
                dlnw_ref[g:g + 1, :] += dlnw
                dlnb_ref[g:g + 1, :] += dlnb
                dws_ref[g] += jnp.where(mask, d_wm, 0.0)
                dbs_ref[g] += jnp.sum(dsv, axis=1, keepdims=True)
            return carry

        lax.fori_loop(0, tm // GMLP_BLOCK, block, 0)

    small = lambda a: pl.BlockSpec(a.shape, lambda i, nd=a.ndim: (0,) * nd)
    return pl.pallas_call(
        body, name=name, grid=(nrow // tm,),
        in_specs=[pl.BlockSpec((tm, 2 * width), lambda i: (i, 0)), pl.BlockSpec((tm, width), lambda i: (i, 0)),
                  small(ln_w), small(ln_b), small(ws), small(bs_col)],
        out_specs=[pl.BlockSpec((tm, 2 * width), lambda i: (i, 0)), small(ln_w), small(ln_b), small(ws), small(bs_col)],
        out_shape=[jax.ShapeDtypeStruct((nrow, 2 * width), BF16), jax.ShapeDtypeStruct(ln_w.shape, F32),
                   jax.ShapeDtypeStruct(ln_b.shape, F32), jax.ShapeDtypeStruct(ws.shape, F32),
                   jax.ShapeDtypeStruct(bs_col.shape, F32)],
        scratch_shapes=[pltpu.VMEM((GMLP_GROUPS, GMLP_BLOCK, GMLP_BLOCK), BF16)],
        compiler_params=_params(("arbitrary",)),
    )(za, dout, ln_w, ln_b, ws, bs_col)


CONV_TM = 512
CONV_RS = 32
HALO = 2 * SUBLANES


def _tap_rows(w_ref):
    return [w_ref[k:k + 1, :] for k in range(w_ref.shape[0])]


def _halo_specs(nrow, tm, tc):
    per = tm // HALO
    last = nrow // HALO - 1
    main = pl.BlockSpec((tm, tc), lambda j, i: (i, j))
    before = pl.BlockSpec((HALO, tc), lambda j, i: (jnp.maximum(i * per - 1, 0), j))
    after = pl.BlockSpec((HALO, tc), lambda j, i: (jnp.minimum((i + 1) * per, last), j))
    return main, before, after


def _col_spec(rows, tc):
    return pl.BlockSpec((rows, tc), lambda j, i: (0, j))


def _conv_fwd(name, x, w, b, *, tc):
    nrow, ncol = x.shape
    taps = w.shape[0]
    tm, rs = CONV_TM, CONV_RS
    main, before, _ = _halo_specs(nrow, tm, tc)

    def body(x_ref, xb_ref, w_ref, b_ref, o_ref, xw_ref):
        first = pl.program_id(1) == 0
        wv, bv = _tap_rows(w_ref), b_ref[...]
        xw_ref[0:HALO, :] = jnp.where(first, 0.0, xb_ref[...].astype(F32))
        for r in range(tm // rs):
            xw_ref[HALO + r * rs:HALO + (r + 1) * rs, :] = x_ref[r * rs:(r + 1) * rs, :].astype(F32)
        for r in range(tm // rs):
            base = HALO + r * rs
            out = bv + wv[taps - 1] * xw_ref[base:base + rs, :]
            for k in range(taps - 1):
                back = taps - 1 - k
                out = out + wv[k] * xw_ref[base - back:base - back + rs, :]
            o_ref[r * rs:(r + 1) * rs, :] = out.astype(o_ref.dtype)

    return pl.pallas_call(
        body, name=name, grid=(ncol // tc, nrow // tm),
        in_specs=[main, before, _col_spec(taps, tc), _col_spec(1, tc)],
        out_specs=main, out_shape=jax.ShapeDtypeStruct((nrow, ncol), BF16),
        scratch_shapes=[pltpu.VMEM((HALO + tm, tc), F32)],
        compiler_params=_params(("parallel", "arbitrary")),
    )(x, x, w, b)


def _conv_bwd(name, dpre, x, w, *, tc):
    nrow, ncol = x.shape
    taps = w.shape[0]
    tm, rs = CONV_TM, CONV_RS
    nsub = tm // rs
    main, before, after = _halo_specs(nrow, tm, tc)

    def fold(v):
        total = v[0:SUBLANES]
        for q in range(1, rs // SUBLANES):
            total = total + v[q * SUBLANES:(q + 1) * SUBLANES]
        return total

    def body(d_ref, da_ref, x_ref, xb_ref, w_ref, dx_ref, dw_ref, db_ref, dwin_ref, xwin_ref):
        i = pl.program_id(1)
        first, last = i == 0, i == pl.num_programs(1) - 1
        wv = _tap_rows(w_ref)

        @pl.when(first)
        def _():
            dw_ref[...] = jnp.zeros_like(dw_ref)
            db_ref[...] = jnp.zeros_like(db_ref)

        xwin_ref[0:HALO, :] = jnp.where(first, 0.0, xb_ref[...].astype(F32))
        dwin_ref[tm:, :] = jnp.where(last, 0.0, da_ref[...].astype(F32))
        for r in range(nsub):
            dwin_ref[r * rs:(r + 1) * rs, :] = d_ref[r * rs:(r + 1) * rs, :].astype(F32)
            xwin_ref[HALO + r * rs:HALO + (r + 1) * rs, :] = x_ref[r * rs:(r + 1) * rs, :].astype(F32)
        dw = [jnp.zeros((SUBLANES, tc), F32)] * taps
        db = jnp.zeros((SUBLANES, tc), F32)
        for r in range(nsub):
            cur = dwin_ref[r * rs:(r + 1) * rs, :]
            dx = wv[taps - 1] * cur
            for k in range(taps - 1):
                ahead = taps - 1 - k
                dx = dx + wv[k] * dwin_ref[r * rs + ahead:(r + 1) * rs + ahead, :]
            dx_ref[r * rs:(r + 1) * rs, :] = dx.astype(dx_ref.dtype)
            for k in range(taps):
                back = taps - 1 - k
                dw[k] = dw[k] + fold(cur * xwin_ref[HALO + r * rs - back:HALO + (r + 1) * rs - back, :])
            db = db + fold(cur)
        for k in range(taps):
            dw_ref[k:k + 1, :] += _colsum(dw[k])
        db_ref[...] += _colsum(db)

    return pl.pallas_call(
        body, name=name, grid=(ncol // tc, nrow // tm),
        in_specs=[main, after, main, before, _col_spec(taps, tc)],
        out_specs=[main, _col_spec(taps, tc), _col_spec(1, tc)],
        out_shape=[jax.ShapeDtypeStruct((nrow, ncol), BF16), jax.ShapeDtypeStruct((taps, ncol), F32),
                   jax.ShapeDtypeStruct((1, ncol), F32)],
        scratch_shapes=[pltpu.VMEM((tm + HALO, tc), F32), pltpu.VMEM((HALO + tm, tc), F32)],
        compiler_params=_params(("parallel", "arbitrary")),
    )(dpre, dpre, x, x, w)


def _glu(gate, val):
    return jax.nn.silu(gate) * val


def _ffn_act_fwd(name, pg, pv, wg, wv, bg, bv, *, tc):
    nrow, ncol = pg.shape
    taps = wg.shape[0]
    tm, rs = CONV_TM, CONV_RS
    main, before, _ = _halo_specs(nrow, tm, tc)

    def body(pg_ref, pgb_ref, pv_ref, pvb_ref, wg_ref, wv_ref, bg_ref, bv_ref, g_ref, v_ref, a_ref, gwin_ref, vwin_ref):
        first = pl.program_id(1) == 0
        taps_g, taps_v, bgv, bvv = _tap_rows(wg_ref), _tap_rows(wv_ref), bg_ref[...], bv_ref[...]
        gwin_ref[0:HALO, :] = jnp.where(first, 0.0, pgb_ref[...].astype(F32))
        vwin_ref[0:HALO, :] = jnp.where(first, 0.0, pvb_ref[...].astype(F32))
        for r in range(tm // rs):
            gwin_ref[HALO + r * rs:HALO + (r + 1) * rs, :] = pg_ref[r * rs:(r + 1) * rs, :].astype(F32)
            vwin_ref[HALO + r * rs:HALO + (r + 1) * rs, :] = pv_ref[r * rs:(r + 1) * rs, :].astype(F32)

        def conv(win_ref, tap_rows, bias, r):
            base = HALO + r * rs
            out = bias + tap_rows[taps - 1] * win_ref[base:base + rs, :]
            for k in range(taps - 1):
                back = taps - 1 - k
                out = out + tap_rows[k] * win_ref[base - back:base - back + rs, :]
            return out

        for r in range(tm // rs):
            sl = slice(r * rs, (r + 1) * rs)
            gate, val = conv(gwin_ref, taps_g, bgv, r), conv(vwin_ref, taps_v, bvv, r)
            g_ref[sl, :] = gate.astype(g_ref.dtype)
            v_ref[sl, :] = val.astype(v_ref.dtype)
            a_ref[sl, :] = _glu(gate, val).astype(a_ref.dtype)

    return pl.pallas_call(
        body, name=name, grid=(ncol // tc, nrow // tm),
        in_specs=[main, before, main, before, _col_spec(taps, tc), _col_spec(taps, tc), _col_spec(1, tc), _col_spec(1, tc)],
        out_specs=[main, main, main],
        out_shape=[jax.ShapeDtypeStruct((nrow, ncol), BF16)] * 3,
        scratch_shapes=[pltpu.VMEM((HALO + tm, tc), F32), pltpu.VMEM((HALO + tm, tc), F32)],
        compiler_params=_params(("parallel", "arbitrary")),
    )(pg, pg, pv, pv, wg, wv, bg, bv)


def _ffn_act_bwd(name, dact, gate, val):
    def fn(dv, gv, vv):
        _, vjp = jax.vjp(_glu, gv, vv)
        dg, dval = vjp(dv)
        return (dg, dval), ()
    width = dact.shape[1]
    return _rows(name, fn, [(dact, width, 0), (gate, width, 0), (val, width, 0)], [],
                 [(width, BF16), (width, BF16)], [], tm=512, rs=2 * SUBLANES)


SSD_TM = 512
SSD_CHUNKS = SSD_TM // CHUNK
X_OFF, B_OFF, C_OFF = 0, SSM_INNER, SSM_INNER + SSM_GROUPS * SSM_STATE
HP = SSM_HPG * SSM_HEAD_DIM


def _causal_tiled():
    row = lax.broadcasted_iota(jnp.int32, (CHUNK, HP), 0)
    src = lax.broadcasted_iota(jnp.int32, (CHUNK, HP), 1) & (CHUNK - 1)
    return src <= row


def _split2(v):
    hi = v.astype(BF16)
    return hi, (v - hi.astype(F32)).astype(BF16)


def _dot_exact(a, ind):
    hi, lo = (lax.dot_general(p, ind, (((1,), (0,)), ((), ())), preferred_element_type=F32) for p in _split2(a))
    return hi + lo


def _head_indicator():
    head = lax.broadcasted_iota(jnp.int32, (SSM_HEADS, SSM_INNER), 0)
    chan = lax.broadcasted_iota(jnp.int32, (SSM_HEADS, SSM_INNER), 1)
    return (chan // SSM_HEAD_DIM == head).astype(BF16)


def _chunk_decays(ci, dt_ref, ac_ref, ind, ax_ref, dtx_ref, eax_ref, eex_ref, tail_ref):
    rows = pl.ds(pl.multiple_of(ci * CHUNK, CHUNK), CHUNK)
    ax_ref[...] = _dot_exact(ac_ref[rows, :], ind)
    dtx_ref[...] = _dot_exact(dt_ref[rows, :], ind)
    eax_ref[...] = jnp.exp(ax_ref[...])
    eex_ref[...] = jnp.exp(ax_ref[CHUNK - 1:CHUNK, :] - ax_ref[...])
    tail = pl.ds(pl.multiple_of(ci * CHUNK + CHUNK - SUBLANES, SUBLANES), SUBLANES)
    tail_ref[...] = jnp.exp(ac_ref[tail, :])


def _group_decay(ci, g, ax_ref, af_ref, xbc_ref, causal):
    gcols = slice(g * HP, (g + 1) * HP)
    bm = xbc_ref[:, B_OFF + g * SSM_STATE:B_OFF + (g + 1) * SSM_STATE]
    cm = xbc_ref[:, C_OFF + g * SSM_STATE:C_OFF + (g + 1) * SSM_STATE]
    cb_tiled = _dot(cm, jnp.concatenate([bm] * SSM_HPG, axis=0), 1, 1)
    seg = ax_ref[:, gcols] - af_ref[ci, :, gcols]
    decay = jnp.where(causal, jnp.exp(jnp.where(causal, seg, 0.0)), 0.0)
    return bm, cm, cb_tiled * decay, decay


def _ssd_fwd(name, pre, dt, a_cum, a_flat, d_x, ind, shards):
    nrow = pre.shape[0]
    tm = SSD_TM
    nstep = nrow // tm
    ng = len(shards)

    def body(pre_ref, dt_ref, ac_ref, af_ref, dx_ref, ind_ref, *rest):
        shard_refs, (y_ref, st_ref), stack_refs = rest[:ng], rest[ng:ng + 2], rest[ng + 2:2 * ng + 2]
        (h_ref, xbc_ref, ax_ref, dtx_ref, eax_ref, eex_ref, m_ref, xd_ref, yd_ref, tail_ref,
         send_sems, recv_sems) = rest[2 * ng + 2:]
        step = pl.program_id(0)
        start, forward, finish = _gather_phases([s.shape[0] for s in shards], ng, shard_refs, stack_refs,
                                                send_sems, recv_sems)

        @pl.when(step == 0)
        def _():
            h_ref[...] = jnp.zeros_like(h_ref)
            start()

        @pl.when(step == nstep // 2)
        def _():
            forward()

        causal = _causal_tiled()
        ind = ind_ref[...]

        def chunk(ci, carry):
            rows = pl.ds(pl.multiple_of(ci * CHUNK, CHUNK), CHUNK)
            xbc_ref[...] = jax.nn.silu(pre_ref[rows, :].astype(F32))
            _chunk_decays(ci, dt_ref, ac_ref, ind, ax_ref, dtx_ref, eax_ref, eex_ref, tail_ref)
            st_ref[ci] = h_ref[...].astype(st_ref.dtype)
            for g in range(SSM_GROUPS):
                gcols = slice(g * HP, (g + 1) * HP)
                bm, cm, m_all, _ = _group_decay(ci, g, ax_ref, af_ref, xbc_ref, causal)
                m_ref[...] = m_all
                x_g = xbc_ref[:, gcols]
                xd = x_g * dtx_ref[:, gcols]
                xd_ref[...] = xd
                h_g = h_ref[gcols, :]
                for hh in range(SSM_HPG):
                    lc = slice(hh * SSM_HEAD_DIM, (hh + 1) * SSM_HEAD_DIM)
                    yd_ref[:, lc] = _dot(m_ref[:, lc], xd_ref[:, lc])
                y_ref[rows, gcols] = (yd_ref[...] + _dot(cm, h_g, 1, 1) * eax_ref[:, gcols]
                                      + dx_ref[:, gcols] * x_g).astype(y_ref.dtype)
                new = _dot(xd * eex_ref[:, gcols], bm, 0, 0)
                for hh in range(SSM_HPG):
                    h = g * SSM_HPG + hh
                    hrows = slice(h * SSM_HEAD_DIM, (h + 1) * SSM_HEAD_DIM)
                    lrows = slice(hh * SSM_HEAD_DIM, (hh + 1) * SSM_HEAD_DIM)
                    h_ref[hrows, :] = tail_ref[SUBLANES - 1:SUBLANES, h:h + 1] * h_ref[hrows, :] + new[lrows, :]
            return carry

        lax.fori_loop(0, SSD_CHUNKS, chunk, 0)

        @pl.when(step == nstep - 1)
        def _():
            finish()

    nchunk = nrow // CHUNK
    whole = lambda a: pl.BlockSpec(a.shape, lambda i, nd=a.ndim: (0,) * nd)
    hbm = pl.BlockSpec(memory_space=pl.ANY)
    wide = lambda: pltpu.VMEM((CHUNK, SSM_INNER), F32)
    group = lambda: pltpu.VMEM((CHUNK, HP), F32)
    res = pl.pallas_call(
        body, name=name, grid=(nstep,),
        in_specs=[pl.BlockSpec((tm, SSM_XBC), lambda i: (i, 0)), pl.BlockSpec((tm, SSM_HEADS), lambda i: (i, 0)),
                  pl.BlockSpec((tm, SSM_HEADS), lambda i: (i, 0)),
                  pl.BlockSpec((SSD_CHUNKS, 1, SSM_INNER), lambda i: (i, 0, 0)), whole(d_x), whole(ind)] + [hbm] * ng,
        out_specs=[pl.BlockSpec((tm, SSM_INNER), lambda i: (i, 0)),
                   pl.BlockSpec((SSD_CHUNKS, SSM_INNER, SSM_STATE), lambda i: (i, 0, 0))] + [hbm] * ng,
        out_shape=[jax.ShapeDtypeStruct((nrow, SSM_INNER), BF16),
                   jax.ShapeDtypeStruct((nchunk, SSM_INNER, SSM_STATE), BF16)]
        + [jax.ShapeDtypeStruct((N_CHIPS,) + s.shape, s.dtype) for s in shards],
        scratch_shapes=[pltpu.VMEM((SSM_INNER, SSM_STATE), F32), pltpu.VMEM((CHUNK, SSM_XBC), F32),
                        wide(), wide(), wide(), wide(), group(), group(), group(),
                        pltpu.VMEM((SUBLANES, SSM_HEADS), F32)] + _exchange_scratch(ng, GATHER_SEMS),
        compiler_params=_params(("arbitrary",)),
    )(pre, dt, a_cum, a_flat, d_x, ind, *shards)
    return res[0], res[1], res[2:]


def _ssd_bwd(name, pre, dt, a_cum, a_flat, d_x, ind, ind_t, states, dy, pairs):
    nrow = pre.shape[0]
    tm = SSD_TM
    ntile = nrow // tm
    npair = len(pairs)

    def body(pre_ref, dt_ref, ac_ref, af_ref, dx_ref, ind_ref, indt_ref, st_ref, dy_ref, *rest):
        pair_refs = rest[:npair]
        dpre_ref, ddt_ref, da_ref, daf_ref, dd_ref = rest[npair:npair + 5]
        recv_refs = rest[npair + 5:2 * npair + 5]
        (dh_ref, xbc_ref, dxbc_ref, ax_ref, dtx_ref, eax_ref, eex_ref, red_ref,
         m_ref, l_ref, xd_ref, dm_ref, dxd_ref, fold_ref, hd_ref, tail_ref, send_sems, recv_sems) = rest[2 * npair + 5:]
        start, finish = _scatter_phases(pair_refs, recv_refs, send_sems, recv_sems)

        @pl.when(pl.program_id(0) == 0)
        def _():
            dh_ref[...] = jnp.zeros_like(dh_ref)
            dd_ref[...] = jnp.zeros_like(dd_ref)
            start()

        causal = _causal_tiled()
        ind, ind_t = ind_ref[...], indt_ref[...]
        is_last_row = lax.broadcasted_iota(jnp.int32, (CHUNK, 1), 0) == CHUNK - 1
        ones = jnp.ones((CHUNK, SSM_STATE), BF16)

        def chunk(k, ddx):
            ci = SSD_CHUNKS - 1 - k
            rows = pl.ds(pl.multiple_of(ci * CHUNK, CHUNK), CHUNK)
            pre_v = pre_ref[rows, :].astype(F32)
            xbc_ref[...] = jax.nn.silu(pre_v)
            _chunk_decays(ci, dt_ref, ac_ref, ind, ax_ref, dtx_ref, eax_ref, eex_ref, tail_ref)
            ddx_parts = []
            for g in range(SSM_GROUPS):
                gcols = slice(g * HP, (g + 1) * HP)
                bcols = slice(B_OFF + g * SSM_STATE, B_OFF + (g + 1) * SSM_STATE)
                ccols = slice(C_OFF + g * SSM_STATE, C_OFF + (g + 1) * SSM_STATE)
                bm, cm, m_all, decay = _group_decay(ci, g, ax_ref, af_ref, xbc_ref, causal)
                m_ref[...] = m_all
                l_ref[...] = decay
                x_g = xbc_ref[:, gcols]
                xd = x_g * dtx_ref[:, gcols]
                xd_ref[...] = xd
                h_g = st_ref[ci, gcols, :]
                dh_g = dh_ref[gcols, :]
                dy_g = dy_ref[rows, gcols]
                for hh in range(SSM_HPG):
                    h = g * SSM_HPG + hh
                    hcols = slice(h * SSM_HEAD_DIM, (h + 1) * SSM_HEAD_DIM)
                    lc = slice(hh * SSM_HEAD_DIM, (hh + 1) * SSM_HEAD_DIM)
                    dy_h = dy_ref[rows, hcols]
                    dm_ref[:, lc] = _dot(dy_h, xd_ref[:, lc], 1, 1)
                    dxd_ref[:, lc] = _dot(m_ref[:, lc], dy_h, 0, 0)
                ebdh = eex_ref[:, gcols] * _dot(bm, dh_g, 1, 1)
                dxd = dxd_ref[...] + ebdh
                dm = dm_ref[...]
                t = dm * l_ref[...]
                t128 = (t[:, 0:LANES] + t[:, LANES:2 * LANES]) + (t[:, 2 * LANES:3 * LANES] + t[:, 3 * LANES:])
                fold_ref[...] = t128 + pltpu.roll(t128, CHUNK, axis=1)
                dw_sum = fold_ref[:, 0:CHUNK]
                q = dm * m_ref[...]
                dyea = dy_g * eax_ref[:, gcols]
                red_ref[0:CHUNK, gcols] = q + dyea * _dot(cm, h_g, 1, 1)
                red_ref[CHUNK:2 * CHUNK, gcols] = xd * ebdh
                red_ref[2 * CHUNK:3 * CHUNK, gcols] = dxd * x_g
                daf_ref[ci, :, gcols] = -jnp.sum(q, axis=0, keepdims=True)
                ddx_parts.append(jnp.sum(dy_g * x_g, axis=0, keepdims=True))
                dxbc_ref[:, gcols] = dxd * dtx_ref[:, gcols] + dx_ref[:, gcols] * dy_g
                dxbc_ref[:, ccols] = _dot(dw_sum, bm) + _dot(dyea, h_g)
                dxbc_ref[:, bcols] = _dot(dw_sum, cm, 0, 0) + _dot(xd * eex_ref[:, gcols], dh_g)
                dh_new = _dot(dyea, cm, 0, 0)
                for hh in range(SSM_HPG):
                    h = g * SSM_HPG + hh
                    hrows = slice(h * SSM_HEAD_DIM, (h + 1) * SSM_HEAD_DIM)
                    lrows = slice(hh * SSM_HEAD_DIM, (hh + 1) * SSM_HEAD_DIM)
                    hd_ref[h:h + 1, :] = jnp.sum(st_ref[ci, hrows, :] * dh_ref[hrows, :], axis=0, keepdims=True)
                    dh_ref[hrows, :] = tail_ref[SUBLANES - 1:SUBLANES, h:h + 1] * dh_ref[hrows, :] + dh_new[lrows, :]
            sums = _dot_exact(red_ref[...], ind_t)
            ra, ts = sums[:CHUNK], sums[CHUNK:2 * CHUNK]
            hdh = sum(lax.dot_general(ones, p, (((1,), (1,)), ((), ())), preferred_element_type=F32)
                      for p in _split2(hd_ref[...]))
            da_last = jnp.sum(ts, axis=0, keepdims=True) + tail_ref[SUBLANES - 1:SUBLANES, :] * hdh
            da_ref[rows, :] = ra - ts + jnp.where(is_last_row, da_last, 0.0)
            ddt_ref[rows, :] = sums[2 * CHUNK:]
            sig = jax.nn.sigmoid(pre_v)
            dpre_ref[rows, :] = (dxbc_ref[...] * (sig * (1.0 + pre_v * (1.0 - sig)))).astype(dpre_ref.dtype)
            return ddx + jnp.concatenate(ddx_parts, axis=1)

        ddx = lax.fori_loop(0, SSD_CHUNKS, chunk, jnp.zeros((1, SSM_INNER), F32))
        dd_ref[...] += _dot_exact(jnp.broadcast_to(ddx, (SUBLANES, SSM_INNER)), ind_t)

        @pl.when(pl.program_id(0) == ntile - 1)
        def _():
            finish()

    rev = lambda i: ntile - 1 - i
    whole = lambda a: pl.BlockSpec(a.shape, lambda i, nd=a.ndim: (0,) * nd)
    hbm = pl.BlockSpec(memory_space=pl.ANY)
    wide = lambda: pltpu.VMEM((CHUNK, SSM_INNER), F32)
    group = lambda: pltpu.VMEM((CHUNK, HP), F32)
    res = pl.pallas_call(
        body, name=name, grid=(ntile,),
        in_specs=[pl.BlockSpec((tm, SSM_XBC), lambda i: (rev(i), 0)), pl.BlockSpec((tm, SSM_HEADS), lambda i: (rev(i), 0)),
                  pl.BlockSpec((tm, SSM_HEADS), lambda i: (rev(i), 0)),
                  pl.BlockSpec((SSD_CHUNKS, 1, SSM_INNER), lambda i: (rev(i), 0, 0)),
                  whole(d_x), whole(ind), whole(ind_t),
                  pl.BlockSpec((SSD_CHUNKS, SSM_INNER, SSM_STATE), lambda i: (rev(i), 0, 0)),
                  pl.BlockSpec((tm, SSM_INNER), lambda i: (rev(i), 0))] + [hbm] * npair,
        out_specs=[pl.BlockSpec((tm, SSM_XBC), lambda i: (rev(i), 0)), pl.BlockSpec((tm, SSM_HEADS), lambda i: (rev(i), 0)),
                   pl.BlockSpec((tm, SSM_HEADS), lambda i: (rev(i), 0)),
                   pl.BlockSpec((SSD_CHUNKS, 1, SSM_INNER), lambda i: (rev(i), 0, 0)),
                   pl.BlockSpec((SUBLANES, SSM_HEADS), lambda i: (0, 0))] + [hbm] * npair,
        out_shape=[jax.ShapeDtypeStruct((nrow, SSM_XBC), BF16), jax.ShapeDtypeStruct((nrow, SSM_HEADS), F32),
                   jax.ShapeDtypeStruct((nrow, SSM_HEADS), F32), jax.ShapeDtypeStruct((nrow // CHUNK, 1, SSM_INNER), F32),
                   jax.ShapeDtypeStruct((SUBLANES, SSM_HEADS), F32)]
        + [jax.ShapeDtypeStruct(p.shape, p.dtype) for p in pairs],
        scratch_shapes=[pltpu.VMEM((SSM_INNER, SSM_STATE), F32), pltpu.VMEM((CHUNK, SSM_XBC), F32),
                        pltpu.VMEM((CHUNK, SSM_XBC), F32), wide(), wide(), wide(), wide(),
                        pltpu.VMEM((3 * CHUNK, SSM_INNER), F32),
                        group(), group(), group(), group(), group(), pltpu.VMEM((CHUNK, LANES), F32),
                        pltpu.VMEM((SSM_HEADS, SSM_STATE), F32), pltpu.VMEM((SUBLANES, SSM_HEADS), F32)]
        + _exchange_scratch(npair, N_CHIPS - 1),
        compiler_params=_params(("arbitrary",)),
    )(pre, dt, a_cum, a_flat, d_x, ind, ind_t, states, dy, *pairs)
    return res[:5], res[5:]


LATE = ["w_proj_a", "w_proj_b", "w_out", "ffn_w_up", "ffn_w_down"]
HALF_TILES = {"w_in": 128, "w_proj_a": 128, "w_proj_b": 256, "w_out": 128, "ffn_w_up": 128, "ffn_w_down": 176}


def _late_weights(stacks, shards):
    pa, pb, out, up, down = [_own_slot(stack, own) for stack, own in zip(stacks, shards)]
    return {"w_proj_a": pa.reshape(-1, D_MODEL), "w_proj_b": pb.reshape(-1, D_MODEL), "w_out": out.reshape(-1, D_MODEL),
            "w_up_g": _columns_from_chips(up[:2]), "w_up_v": _columns_from_chips(up[2:]),
            "w_down": down.reshape(-1, D_MODEL)}


def _pair_reduce(tag, names, stacks):
    core = lax.axis_index("c")
    own_half = [_row_half(s, core, 1) for s in stacks]
    other_half = _swap_cores("pair_grads_" + tag, [_row_half(s, 1 - core, 1) for s in stacks])
    return [_pair_sum("pair_" + n, a, b, tm=HALF_TILES[n]) for n, a, b in zip(names, own_half, other_half)]


def _local_step(x, xn, target, w, late_shards):
    w = dict(w)
    g = {}
    bs_col = w["gmlp_bs"].reshape(GMLP_GROUPS, GMLP_BLOCK, 1)
    b0, b1 = w["gate_bias"][0:1], w["gate_bias"][1:2]

    big = dict(bm=1024, bn=1024, bk=1024)
    act16 = dict(out_dtype=BF16, **big)
    gates = _mm("in_gates", xn, w["w_g"], **act16)
    za = _mm("in_gmlp", xn, w["w_za"], **act16)
    z = _mm("in_z", xn, w["w_z"], **act16)
    xbc = _mm("in_xbc", xn, w["w_xbc"], **act16)
    dt_raw = _mm("in_dt", xn, w["w_dt"], bm=1024, bn=SSM_HEADS, bk=1024)

    pre = _conv_fwd("ssm_conv_fwd", xbc, w["ssm_conv_w"], w["ssm_conv_b"], tc=1024)
    dt, a_cum = _dt_prep("dt_prep", dt_raw, w["ssm_dt_bias"], w["ssm_a_log"])
    a_flat = jnp.transpose(a_cum.reshape(-1, CHUNK, SSM_HEADS), (0, 2, 1)).reshape(-1, 1, SSM_INNER)
    d_x = jnp.repeat(w["ssm_d"], SSM_HEAD_DIM, axis=1)
    ind = _head_indicator()
    y_ssd, states, late_stacks = _ssd_fwd("ssd_fwd", pre, dt, a_cum, a_flat, d_x, ind, late_shards)
    w.update(_late_weights(late_stacks, late_shards))
    yb_pre = _gate_norm_fwd("gate_norm_fwd", y_ssd, z, w["ssm_norm_w"])
    y_b = _mm("proj_b", yb_pre, w["w_proj_b"], bm=1024, bn=1024, bk=SSM_INNER, out_dtype=BF16)

    ya_pre = _gmlp_fwd("gmlp_fwd", za, w["gmlp_ln_w"], w["gmlp_ln_b"], w["gmlp_ws"], bs_col)
    y_a = _mm("proj_a", ya_pre, w["w_proj_a"], **act16)

    merged = _merge_fwd("merge_fwd", gates, y_a, y_b, b0, b1)
    h1 = _mm("out_proj", merged, w["w_out"], res=x, **big)

    hn = _rms_fwd("ffn_norm", h1, w["ffn_norm_w"])
    half = dict(bm=1024, bn=D_FF // 2, bk=1024, out_dtype=BF16)
    pg = _mm("ffn_up_gate", hn, w["w_up_g"], **half)
    pv = _mm("ffn_up_val", hn, w["w_up_v"], **half)
    cw, cb = w["ffn_conv_w"], w["ffn_conv_b"]
    gate, val, act = _ffn_act_fwd("ffn_act_fwd", pg, pv, cw[:, :D_FF], cw[:, D_FF:], cb[:, :D_FF], cb[:, D_FF:],
                                  tc=D_FF // 2)
    h2 = _mm("ffn_down", act, w["w_down"], res=h1, bm=1024, bn=1024, bk=D_FF // 2)

    dh2, loss_part, g["final_norm_w"] = _final_loss("final_loss", h2, target, w["final_norm_w"].reshape(1, D_MODEL))

    dact = _mm("d_act", dh2, w["w_down"], tb=True, **half)
    wgrad = dict(ta=True, bk=min(2048, x.shape[0]), out_dtype=BF16)
    g["w_down"] = _mm("dw_down", act, dh2, bm=D_FF // 2, bn=1024, **wgrad)
    dgate, dval = _ffn_act_bwd("ffn_act_bwd", dact, gate, val)
    dpg, dcwg, dcbg = _conv_bwd("ffn_conv_bwd_gate", dgate, pg, cw[:, :D_FF], tc=D_FF // 2)
    dpv, dcwv, dcbv = _conv_bwd("ffn_conv_bwd_val", dval, pv, cw[:, D_FF:], tc=D_FF // 2)
    g["ffn_conv_w"] = jnp.concatenate([dcwg, dcwv], axis=1)
    g["ffn_conv_b"] = jnp.concatenate([dcbg, dcbv], axis=1)
    dhn = _mm_sum("d_hn", [(dpg, w["w_up_g"]), (dpv, w["w_up_v"])], bm=1024, bk=D_FF // 2)
    g["w_up_g"] = _mm("dw_up_gate", hn, dpg, bm=1024, bn=D_FF // 2, **wgrad)
    g["w_up_v"] = _mm("dw_up_val", hn, dpv, bm=1024, bn=D_FF // 2, **wgrad)
    dh1, g["ffn_norm_w"] = _rms_bwd("ffn_norm_bwd", h1, w["ffn_norm_w"], dhn, dh2)

    dmerged = _mm("d_merged", dh1, w["w_out"], tb=True, **act16)
    g["w_out"] = _mm("dw_out", merged, dh1, bm=1024, bn=1024, **wgrad)
    dgates, dya, dyb, db0, db1 = _merge_bwd("merge_bwd", gates, y_a, y_b, dmerged, b0, b1)
    g["gate_bias"] = jnp.concatenate([db0, db1], axis=0)

    dya_pre = _mm("d_ya_pre", dya, w["w_proj_a"], tb=True, **act16)
    g["w_proj_a"] = _mm("dw_proj_a", ya_pre, dya, bm=1024, bn=1024, **wgrad)
    dyb_pre = _mm("d_yb_pre", dyb, w["w_proj_b"], tb=True, **act16)
    g["w_proj_b"] = _mm("dw_proj_b", yb_pre, dyb, bm=1024, bn=1024, **wgrad)
    late_pairs = _pair_reduce("late", LATE, [
        g["w_proj_a"].reshape(N_CHIPS, -1, D_MODEL), g["w_proj_b"].reshape(N_CHIPS, -1, D_MODEL),
        g["w_out"].reshape(N_CHIPS, -1, D_MODEL),
        jnp.concatenate([_columns_to_chips(g["w_up_g"], 2), _columns_to_chips(g["w_up_v"], 2)], axis=0),
        g["w_down"].reshape(N_CHIPS, -1, D_MODEL)])

    dy_ssd, dz, g["ssm_norm_w"] = _gate_norm_bwd("gate_norm_bwd", y_ssd, z, dyb_pre, w["ssm_norm_w"])
    (dpre, ddt, da_tok, da_flat, dd), late_received = _ssd_bwd(
        "ssd_bwd", pre, dt, a_cum, a_flat, d_x, ind, ind.T, states, dy_ssd, late_pairs)
    g["ssm_d"] = dd[0:1]
    da_src = jnp.transpose(da_flat.reshape(-1, SSM_HEADS, CHUNK), (0, 2, 1)).reshape(-1, SSM_HEADS)
    ddt_raw, g["ssm_dt_bias"], g["ssm_a_log"] = _dt_bwd("dt_bwd", dt_raw, ddt, da_tok, da_src,
                                                         w["ssm_dt_bias"], w["ssm_a_log"])
    dxbc, g["ssm_conv_w"], g["ssm_conv_b"] = _conv_bwd("ssm_conv_bwd", dpre, xbc, w["ssm_conv_w"], tc=1024)

    dza, g["gmlp_ln_w"], g["gmlp_ln_b"], g["gmlp_ws"], dbs = _gmlp_bwd(
        "gmlp_bwd", za, dya_pre, w["gmlp_ln_w"], w["gmlp_ln_b"], w["gmlp_ws"], bs_col)
    g["gmlp_bs"] = dbs.reshape(GMLP_GROUPS, GMLP_BLOCK)

    dw_in = jnp.concatenate([
        _mm("dw_gates", xn, dgates, bm=1024, bn=1024, **wgrad), _mm("dw_gmlp", xn, dza, bm=1024, bn=1024, **wgrad),
        _mm("dw_z", xn, dz, bm=1024, bn=1024, **wgrad), _mm("dw_xbc", xn, dxbc, bm=1024, bn=1024, **wgrad),
        _mm("dw_dt", xn, ddt_raw, bm=1024, bn=SSM_HEADS, **wgrad)], axis=1)
    in_pairs = _pair_reduce("in", ["w_in"], [_columns_to_chips(dw_in)])
    dxn, in_received = _mm_sum("d_xn", [(dgates, w["w_g"]), (dza, w["w_za"]), (dz, w["w_z"]), (dxbc, w["w_xbc"]),
                                        (ddt_raw, w["w_dt"])], bm=1024, bk=1024, exchange=in_pairs)
    grad_x, g["mix_norm_w"] = _rms_bwd("mix_norm_bwd", x, w["mix_norm_w"], dxn, dh1)
    return loss_part, grad_x, g, in_pairs + list(late_pairs), list(in_received) + list(late_received)


def _position():
    return lax.axis_index("x"), lax.axis_index("y"), lax.axis_index("c")


def _own_slot(stack, own):
    chip = 2 * lax.axis_index("x") + lax.axis_index("y")
    return lax.dynamic_update_index_in_dim(stack, own, chip, axis=0)


def _scatter_phases(ins, outs, send_sems, recv_sems):
    n = len(ins)
    x, y, c = _position()
    me = 2 * x + y
    peers = [(1 - x, y), (x, 1 - y), (1 - x, 1 - y)]

    def copy(i, k, src_slot, dst_slot):
        px, py = peers[k]
        return pltpu.make_async_remote_copy(
            src_ref=ins[i].at[src_slot], dst_ref=outs[i].at[dst_slot],
            send_sem=send_sems.at[i, k], recv_sem=recv_sems.at[i, k],
            device_id=(px, py, c), device_id_type=MESH)

    def start():
        for i in range(n):
            for k, (px, py) in enumerate(peers):
                copy(i, k, 2 * px + py, me).start()

    def finish():
        for i in range(n):
            for k, (px, py) in enumerate(peers):
                copy(i, k, me, 2 * px + py).wait_recv()
        for i in range(n):
            for k, (px, py) in enumerate(peers):
                copy(i, k, 2 * px + py, me).wait_send()

    return start, finish


def _exchange_scratch(n, per_array):
    return [pltpu.SemaphoreType.DMA((n, per_array)), pltpu.SemaphoreType.DMA((n, per_array))]


def _half_rows(ref_rows, which):
    half = ref_rows // 2
    return pl.ds(pl.multiple_of(which * half, 2 * SUBLANES), half)


GATHER_SEMS = 2 * (N_CHIPS - 1)


def _gather_phases(nrows, ns, ins, outs, send_sems, recv_sems):
    n = len(ins)
    x, y, c = _position()
    me = 2 * x + y
    sibling = (x, y, 1 - c)
    chips = [(1 - x, y), (x, 1 - y), (1 - x, 1 - y)]

    def remote(i, k, src, dst, to):
        return pltpu.make_async_remote_copy(src_ref=src, dst_ref=dst, send_sem=send_sems.at[i, k],
                                            recv_sem=recv_sems.at[i, k], device_id=to, device_id_type=MESH)

    def over_ici(i, k):
        px, py = chips[k]
        rows = _half_rows(nrows[i], c) if i < ns else slice(None)
        return remote(i, k, ins[i].at[rows], outs[i].at[me, rows], (px, py, c))

    def landed(i, k, which):
        px, py = chips[k]
        return outs[i].at[2 * px + py, _half_rows(nrows[i], which)] if i < ns else outs[i].at[2 * px + py]

    def start():
        for i in range(n):
            for k in range(N_CHIPS - 1):
                over_ici(i, k).start()

    def forward():
        for i in range(n):
            for k in range(N_CHIPS - 1):
                piece = landed(i, k, c)
                remote(i, k, piece, piece, (*chips[k], c)).wait_recv()
                if i < ns:
                    remote(i, N_CHIPS - 1 + k, piece, piece, sibling).start()

    def finish():
        for i in range(ns):
            for k in range(N_CHIPS - 1):
                piece = landed(i, k, 1 - c)
                remote(i, N_CHIPS - 1 + k, piece, piece, sibling).wait_recv()
        for i in range(n):
            for k in range(N_CHIPS - 1):
                over_ici(i, k).wait_send()
                if i < ns:
                    piece = landed(i, k, c)
                    remote(i, N_CHIPS - 1 + k, piece, piece, sibling).wait_send()

    return start, forward, finish


def _gather_chips_split(name, split, whole):
    arrs = list(split) + list(whole)
    n = len(arrs)

    def body(*refs):
        phases = _gather_phases([a.shape[0] for a in arrs], len(split), refs[:n], refs[n:2 * n], *refs[2 * n:])
        for phase in phases:
            phase()

    hbm = pl.BlockSpec(memory_space=pl.ANY)
    return pl.pallas_call(
        body, name=name, in_specs=[hbm] * n, out_specs=[hbm] * n,
        out_shape=[jax.ShapeDtypeStruct((N_CHIPS,) + a.shape, a.dtype) for a in arrs],
        scratch_shapes=_exchange_scratch(n, GATHER_SEMS),
        compiler_params=pltpu.CompilerParams(has_side_effects=True),
    )(*arrs)


def _swap_cores(name, arrs):
    n = len(arrs)

    def body(*refs):
        ins, outs = refs[:n], refs[n:2 * n]
        send_sems, recv_sems = refs[2 * n:]
        x, y, c = _position()
        copies = [pltpu.make_async_remote_copy(src_ref=ins[i], dst_ref=outs[i], send_sem=send_sems.at[i],
                                               recv_sem=recv_sems.at[i], device_id=(x, y, 1 - c), device_id_type=MESH)
                  for i in range(n)]
        for cp in copies:
            cp.start()
        for cp in copies:
            cp.wait_recv()
        for cp in copies:
            cp.wait_send()

    hbm = pl.BlockSpec(memory_space=pl.ANY)
    return pl.pallas_call(
        body, name=name, in_specs=[hbm] * n, out_specs=[hbm] * n,
        out_shape=[jax.ShapeDtypeStruct(a.shape, a.dtype) for a in arrs],
        scratch_shapes=[pltpu.SemaphoreType.DMA((n,)), pltpu.SemaphoreType.DMA((n,))],
        compiler_params=pltpu.CompilerParams(has_side_effects=True),
    )(*arrs)


def _row_half(a, which, axis):
    half = a.shape[axis] // 2
    return lax.dynamic_slice_in_dim(a, which * half, half, axis=axis)


def _all_reduce(name, pack):
    def body(in_ref, out_ref, buf, send_sems, recv_sems):
        x, y, c = _position()
        me = 4 * x + 2 * y + c
        flips = [(dx, dy, dc) for dx in (0, 1) for dy in (0, 1) for dc in (0, 1) if (dx, dy, dc) != (0, 0, 0)]
        peers = [((1 - x) if dx else x, (1 - y) if dy else y, (1 - c) if dc else c) for dx, dy, dc in flips]
        buf[me] = in_ref[...]
        sends = []
        for k, peer in enumerate(peers):
            cp = pltpu.make_async_remote_copy(src_ref=in_ref, dst_ref=buf.at[me], send_sem=send_sems.at[k],
                                              recv_sem=recv_sems.at[k], device_id=peer, device_id_type=MESH)
            cp.start()
            sends.append(cp)
        for k, (px, py, pc) in enumerate(peers):
            pltpu.make_async_remote_copy(src_ref=in_ref, dst_ref=buf.at[4 * px + 2 * py + pc], send_sem=send_sems.at[k],
                                         recv_sem=recv_sems.at[k], device_id=(px, py, pc), device_id_type=MESH).wait_recv()
        total = buf[0]
        for j in range(1, N_DEV):
            total = total + buf[j]
        out_ref[...] = total
        for cp in sends:
            cp.wait_send()

    vmem = pl.BlockSpec(memory_space=pltpu.VMEM)
    return pl.pallas_call(
        body, name=name, in_specs=[vmem], out_specs=vmem,
        out_shape=jax.ShapeDtypeStruct(pack.shape, F32),
        scratch_shapes=[pltpu.VMEM((N_DEV,) + pack.shape, F32), pltpu.SemaphoreType.DMA((N_DEV - 1,)),
                        pltpu.SemaphoreType.DMA((N_DEV - 1,))],
        compiler_params=pltpu.CompilerParams(has_side_effects=True, vmem_limit_bytes=VMEM_LIMIT_V7X),
    )(pack)


def _pack(arrs):
    rows = []
    for a in arrs:
        r = a.reshape(-1, LANES)
        rows.append(jnp.pad(r, ((0, -r.shape[0] % SUBLANES), (0, 0))))
    return jnp.concatenate(rows, axis=0)


def _unpack(pack, shapes):
    out, off = [], 0
    for s in shapes:
        nrow = 1
        for d in s:
            nrow *= d
        nrow //= LANES
        out.append(pack[off:off + nrow].reshape(s))
        off += nrow + (-nrow % SUBLANES)
    return out


SMALL = ["mix_norm_w", "gate_bias", "gmlp_ln_w", "gmlp_ln_b", "gmlp_ws", "gmlp_bs", "ssm_conv_w", "ssm_conv_b",
         "ssm_dt_bias", "ssm_a_log", "ssm_d", "ssm_norm_w", "ffn_norm_w", "ffn_conv_w", "ffn_conv_b", "final_norm_w"]
SMALL_SHARDED = ("gate_bias", "ssm_conv_w", "ffn_conv_w")
BIG = ["w_in", "w_proj_a", "w_proj_b", "w_out", "ffn_w_up", "ffn_w_down"]
WEIGHTS = ["mix_norm_w", "w_in", "gate_bias", "gmlp_ln_w", "gmlp_ln_b", "gmlp_ws", "gmlp_bs", "ssm_conv_w",
           "ssm_conv_b", "ssm_dt_bias", "ssm_a_log", "ssm_d", "ssm_norm_w", "w_proj_a", "w_proj_b", "w_out",
           "ffn_norm_w", "ffn_w_up", "ffn_conv_w", "ffn_conv_b", "ffn_w_down", "final_norm_w"]
IN_SPLITS = [0, 2048, 4096, 6144, 9216, 9248]


def _columns_from_chips(stack):
    return jnp.transpose(stack, (1, 0, 2)).reshape(stack.shape[1], -1)


def _columns_to_chips(full, parts=N_CHIPS):
    rows, cols = full.shape
    return jnp.transpose(full.reshape(rows, parts, cols // parts), (1, 0, 2))


def kernel(x, mix_norm_w, w_in, gate_bias, gmlp_ln_w, gmlp_ln_b, gmlp_ws, gmlp_bs, ssm_conv_w, ssm_conv_b, ssm_dt_bias, ssm_a_log, ssm_d, ssm_norm_w, w_proj_a, w_proj_b, w_out, ffn_norm_w, ffn_w_up, ffn_conv_w, ffn_conv_b, ffn_w_down, final_norm_w, loss_target, m_mix_norm_w, m_w_in, m_gate_bias, m_gmlp_ln_w, m_gmlp_ln_b, m_gmlp_ws, m_gmlp_bs, m_ssm_conv_w, m_ssm_conv_b, m_ssm_dt_bias, m_ssm_a_log, m_ssm_d, m_ssm_norm_w, m_w_proj_a, m_w_proj_b, m_w_out, m_ffn_norm_w, m_ffn_w_up, m_ffn_conv_w, m_ffn_conv_b, m_ffn_w_down, m_final_norm_w, v_mix_norm_w, v_w_in, v_gate_bias, v_gmlp_ln_w, v_gmlp_ln_b, v_gmlp_ws, v_gmlp_bs, v_ssm_conv_w, v_ssm_conv_b, v_ssm_dt_bias, v_ssm_a_log, v_ssm_d, v_ssm_norm_w, v_w_proj_a, v_w_proj_b, v_w_out, v_ffn_norm_w, v_ffn_w_up, v_ffn_conv_w, v_ffn_conv_b, v_ffn_w_down, v_final_norm_w):
    args = dict(locals())
    weights = {n: args[n] for n in WEIGHTS}
    moments_m = {n: args["m_" + n] for n in WEIGHTS}
    moments_v = {n: args["v_" + n] for n in WEIGHTS}
    chip = 2 * lax.axis_index("x") + lax.axis_index("y")

    shards = [weights["w_in"][0].astype(BF16)] + [weights[n][0] for n in SMALL_SHARDED]
    xn, gathered = _rms_fwd_gather("mix_norm_gather", x[0], mix_norm_w, shards[:1], shards[1:])
    w_in_s, gb_s, scw_s, fcw_s = [_own_slot(stack, own) for stack, own in zip(gathered, shards)]
    late_shards = [weights[n][0].astype(BF16) for n in LATE]
    w_in_full = _columns_from_chips(w_in_s)
    full = {"w_" + nm: w_in_full[:, IN_SPLITS[k]:IN_SPLITS[k + 1]] for k, nm in enumerate(["g", "za", "z", "xbc", "dt"])}
    full["gate_bias"] = _columns_from_chips(gb_s)
    full["ssm_conv_w"] = _columns_from_chips(scw_s)
    full["ffn_conv_w"] = _columns_from_chips(fcw_s)
    for n in SMALL:
        if n not in SMALL_SHARDED:
            full[n] = weights[n] if n == "final_norm_w" else weights[n][0]
    for n in ("mix_norm_w", "ffn_norm_w", "ssm_conv_b", "ssm_dt_bias", "ssm_a_log", "ssm_d", "ssm_norm_w", "ffn_conv_b"):
        full[n] = full[n].reshape(1, -1)

    loss_part, grad_x, g, pair, received = _local_step(x[0], xn, loss_target[0], full, late_shards)

    per_head = ["ssm_dt_bias", "ssm_a_log", "ssm_d"]
    rest = [n for n in SMALL if n not in per_head]
    head_row = jnp.concatenate([g[n] for n in per_head] + [jnp.zeros((1, LANES - 3 * SSM_HEADS), F32)], axis=1)
    pack = _pack([loss_part, head_row] + [g[n] for n in rest])
    reduced = _unpack(_all_reduce("reduce_small", pack), [(1, LANES), (1, LANES)] + [g[n].shape for n in rest])
    loss = reduced[0][0, 0]
    small_grads = {n: reduced[1][:, k * SSM_HEADS:(k + 1) * SSM_HEADS] for k, n in enumerate(per_head)}
    for n, r in zip(rest, reduced[2:]):
        if n in SMALL_SHARDED:
            width = weights[n].shape[2]
            r = lax.dynamic_slice_in_dim(r, chip * width, width, axis=1)
        small_grads[n] = r
    two_d = lambda a: a.reshape(-1, a.shape[-1])
    upd = _adamw_small("adamw_small", *[[two_d(d[n]) for n in SMALL]
                                        for d in (weights, small_grads, moments_m, moments_v)])
    small_out = [[small_grads[n] for n in SMALL]] + list(upd)
    small_out = [[a.reshape(weights[n].shape) for n, a in zip(SMALL, kind)] for kind in small_out]

    received = [_own_slot(r, lax.dynamic_index_in_dim(p, chip, 0, keepdims=False)) for r, p in zip(received, pair)]
    halves = [_sum_slots("sum_" + n, r, tm=HALF_TILES[n], rs=2 * SUBLANES) for n, r in zip(BIG, received)]
    tiles = {"w_in": 128, "w_proj_a": 256, "w_proj_b": 256, "w_out": 256, "ffn_w_up": 128, "ffn_w_down": 176}
    core = lax.axis_index("c")
    other = _swap_cores("join_grads", halves)
    reduced = [jnp.concatenate([jnp.where(core == 0, a, b), jnp.where(core == 0, b, a)], axis=0)
               for a, b in zip(halves, other)]
    big_out = {}
    for n, grad in zip(BIG, reduced):
        big_out[n] = _adamw("adamw_" + n, weights[n][0], grad, moments_m[n][0], moments_v[n][0],
                            tm=tiles[n], rs=SUBLANES)

    per_kind = [[], [], [], []]
    for n in WEIGHTS:
        for kind in range(4):
            if n in big_out:
                per_kind[kind].append(big_out[n][kind].reshape(weights[n].shape))
            else:
                per_kind[kind].append(small_out[kind][SMALL.index(n)])
    return (loss, grad_x[None], *per_kind[0], *per_kind[1], *per_kind[2], *per_kind[3])
```
